```python
import jax
import jax.numpy as jnp
from jax import lax
import numpy as np


D_MODEL = 1024
BATCH = 16
SEQ = 256
DEPTH = 2
DEC_BATCH = 4
DEC_SEQ = 4096
PAST_LEN = 256

GRID_W = 64
POS_BASE = 10000.0
EPS = 1e-6
ALPHA = (2.0 * DEPTH) ** 0.25
BETA = (8.0 * DEPTH) ** -0.25
N_A_LAYERS = (DEPTH + 1) // 2
N_B_LAYERS = DEPTH // 2
NH_A = 4
DK_A = 128
DV_A = 256
QK_A = NH_A * DK_A
V_A = NH_A * DV_A
P_A = 2 * QK_A + 2 * V_A + 4 * NH_A
CHUNK_A = 64
NH_B = 8
DK_B = D_MODEL // NH_B
DV_B = D_MODEL // NH_B
CHUNK_B = 32
N_EXPERTS = 64
TOP_K = 8
N_GROUPS = 8
TOPK_GROUPS = 4
D_EXPERT = D_MODEL // 4
D_SHARED = D_MODEL // 4
ROUTED_SCALE = 2.5
MOE_BLOCK = 128

kernel_name = 'hybrid_mlstm_hgrn2_moe_diffusion_step'


def layer_norm(x, g, b):
    xf = x.astype(jnp.float32)
    mu = xf.mean(-1, keepdims=True)
    var = jnp.square(xf - mu).mean(-1, keepdims=True)
    return ((xf - mu) * lax.rsqrt(var + EPS) * g.astype(jnp.float32) + b.astype(jnp.float32)).astype(x.dtype)


def grid_pos_embed(rows, dtype):
    quarter = D_MODEL // 4
    omega = 1.0 / (POS_BASE ** (jnp.arange(quarter, dtype=jnp.float32) / quarter))
    r, col = jnp.meshgrid(jnp.arange(rows, dtype=jnp.float32), jnp.arange(GRID_W, dtype=jnp.float32), indexing='ij')
    r = r.reshape(-1, 1) * omega
    col = col.reshape(-1, 1) * omega
    return jnp.concatenate([jnp.sin(r), jnp.cos(r), jnp.sin(col), jnp.cos(col)], axis=-1).astype(dtype)


def to_chunks(a, chunk):
    B, H, S = a.shape[:3]
    return jnp.moveaxis(a.reshape(B, H, S // chunk, chunk, *a.shape[3:]), 2, 0)


def from_chunks(a):
    NC, B, H, L, d = a.shape
    return jnp.moveaxis(a, 0, 2).reshape(B, H, NC * L, d)


def mlstm_scan(q, k, v, ig, lf, C0, n0, m0):
    causal = jnp.tril(jnp.ones((CHUNK_A, CHUNK_A), bool))

    def step(carry, xs):
        C, n, m = carry
        qc, kc, vc, ic, fc = xs
        b = jnp.cumsum(fc, axis=-1)
        logd = jnp.where(causal, b[..., :, None] - b[..., None, :] + ic[..., None, :], -jnp.inf)
        inter = b + m[..., None]
        m_t = jnp.maximum(inter, logd.max(-1))
        dmat = jnp.exp(logd - m_t[..., None])
        e_inter = jnp.exp(inter - m_t)
        s = jnp.einsum('bhtd,bhsd->bhts', qc, kc) * dmat
        num = jnp.einsum('bhts,bhsv->bhtv', s, vc) + e_inter[..., None] * jnp.einsum('bhtd,bhdv->bhtv', qc, C)
        den = s.sum(-1) + e_inter * jnp.einsum('bhtd,bhd->bht', qc, n)
        h = num / jnp.maximum(jnp.abs(den), jnp.exp(-m_t))[..., None]
        bl = b[..., -1]
        logw = bl[..., None] - b + ic
        m_new = jnp.maximum(bl + m, logw.max(-1))
        w = jnp.exp(logw - m_new[..., None])
        decay = jnp.exp(bl + m - m_new)
        C_new = decay[..., None, None] * C + jnp.einsum('bhs,bhsd,bhsv->bhdv', w, kc, vc)
        n_new = decay[..., None] * n + jnp.einsum('bhs,bhsd->bhd', w, kc)
        return (C_new, n_new, m_new), h

    xs = tuple(to_chunks(a, CHUNK_A) for a in (q, k, v, ig, lf))
    init = (C0.astype(jnp.float32), n0.astype(jnp.float32), m0.astype(jnp.float32))
    (C, n, m), h = lax.scan(step, init, xs)
    return from_chunks(h), C, n, m


def hgrn_scan(q, k, v, lf, S0):
    causal = jnp.tril(jnp.ones((CHUNK_B, CHUNK_B), bool))

    def step(St, xs):
        qc, kc, vc, fc = xs
        b = jnp.cumsum(fc, axis=2)
        diff = jnp.where(causal[:, :, None], b[:, :, :, None, :] - b[:, :, None, :, :], -jnp.inf)
        a = jnp.einsum('bhtd,bhsd,bhtsd->bhts', qc, kc, jnp.exp(diff))
        o = jnp.einsum('bhts,bhsv->bhtv', a, vc) + jnp.einsum('bhtd,bhdv->bhtv', qc * jnp.exp(b), St)
        bl = b[:, :, -1:, :]
        S_new = jnp.exp(bl[:, :, 0])[..., None] * St + jnp.einsum('bhsd,bhsv->bhdv', kc * jnp.exp(bl - b), vc)
        return S_new, o

    xs = tuple(to_chunks(a, CHUNK_B) for a in (q, k, v, lf))
    S, o = lax.scan(step, S0.astype(jnp.float32), xs)
    return from_chunks(o), S


def flip_t(a):
    return jnp.flip(a, axis=2)


def mlstm_mixer(h, w_in, b_gates, norm_w, w_out, C0, n0, m0):
    B, S, _ = h.shape
    q, k, v, o, g = jnp.split(h @ w_in, [QK_A, 2 * QK_A, 2 * QK_A + V_A, 2 * QK_A + 2 * V_A], axis=-1)

    def heads(a, d):
        return a.reshape(B, S, NH_A, d).transpose(0, 2, 1, 3).astype(jnp.float32)

    q = heads(q, DK_A)
    k = heads(k, DK_A) * (DK_A ** -0.5)
    v = heads(v, DV_A)
    gates = g.astype(jnp.float32).reshape(B, S, 2, 2, NH_A) + b_gates.astype(jnp.float32)
    gates = gates.transpose(2, 3, 0, 4, 1)
    ig = gates[:, 0]
    lf = jax.nn.log_sigmoid(gates[:, 1])
    hf, Cf, nf, mf = mlstm_scan(q, k, v, ig[0], lf[0], C0[:, 0], n0[:, 0], m0[:, 0])
    hb, Cb, nb, mb = mlstm_scan(flip_t(q), flip_t(k), flip_t(v), flip_t(ig[1]), flip_t(lf[1]),
                                C0[:, 1], n0[:, 1], m0[:, 1])
    y = hf + flip_t(hb)
    mu = y.mean(-1, keepdims=True)
    var = jnp.square(y - mu).mean(-1, keepdims=True)
    y = ((y - mu) * lax.rsqrt(var + EPS)).transpose(0, 2, 1, 3).reshape(B, S, V_A)
    y = (y * norm_w.astype(jnp.float32) * jax.nn.sigmoid(o.astype(jnp.float32))).astype(h.dtype)
    return (y @ w_out, jnp.stack([Cf, Cb], axis=1), jnp.stack([nf, nb], axis=1), jnp.stack([mf, mb], axis=1))


def hgrn_mixer(h, w_in, lb, norm_w, w_out, S0):
    B, S, _ = h.shape
    q, pre_f, pre_b, v, g = jnp.split(h @ w_in, 5, axis=-1)

    def heads(a):
        return a.reshape(B, S, NH_B, DK_B).transpose(0, 2, 1, 3).astype(jnp.float32)

    def gate(pre, lbd):
        pre = pre.astype(jnp.float32)
        lf = jnp.log(lbd + (1.0 - lbd) * jax.nn.sigmoid(pre))
        kk = (1.0 - lbd) * jax.nn.sigmoid(-pre)
        return heads(lf), heads(kk)

    lb = lb.astype(jnp.float32)
    q = heads(jax.nn.silu(q))
    v = heads(v)
    lf_f, k_f = gate(pre_f, lb[0])
    lf_b, k_b = gate(pre_b, lb[1])
    of, Sf = hgrn_scan(q, k_f, v, lf_f, S0[:, 0])
    ob, Sb = hgrn_scan(flip_t(q), flip_t(k_b), flip_t(v), flip_t(lf_b), S0[:, 1])
    y = of + flip_t(ob)
    y = y * lax.rsqrt(jnp.square(y).mean(-1, keepdims=True) + EPS)
    y = y.transpose(0, 2, 1, 3).reshape(B, S, NH_B * DV_B) * norm_w.astype(jnp.float32)
    y = (y * jax.nn.silu(g.astype(jnp.float32))).astype(h.dtype)
    return y @ w_out, jnp.stack([Sf, Sb], axis=1)


def moe_ffn(x, w_router, e_bias, w_gate, w_up, w_down, ws_gate, ws_up, ws_down):
    T = x.shape[0]
    scores = jax.nn.sigmoid(x.astype(jnp.float32) @ w_router.astype(jnp.float32))
    sel = scores + e_bias.astype(jnp.float32)
    grp = sel.reshape(T, N_GROUPS, N_EXPERTS // N_GROUPS)
    grp_score = lax.top_k(grp, 2)[0].sum(-1)
    _, top_g = lax.top_k(grp_score, TOPK_GROUPS)
    gmask = jnp.any(top_g[..., None] == jnp.arange(N_GROUPS), axis=-2)
    emask = jnp.repeat(gmask, N_EXPERTS // N_GROUPS, axis=-1)
    _, top_e = lax.top_k(jnp.where(emask, sel, -jnp.inf), TOP_K)
    w = jnp.take_along_axis(scores, top_e, axis=-1)
    w = w / w.sum(-1, keepdims=True) * ROUTED_SCALE
    A = T * TOP_K
    e_flat = top_e.reshape(A)
    tok_flat = jnp.arange(A, dtype=jnp.int32) // TOP_K
    order = jnp.argsort(e_flat)
    e_sorted = e_flat[order]
    counts = jnp.zeros((N_EXPERTS,), jnp.int32).at[e_flat].add(1)
    start = jnp.cumsum(counts) - counts
    padded = (counts + MOE_BLOCK - 1) // MOE_BLOCK * MOE_BLOCK
    pend = jnp.cumsum(padded)
    pstart = pend - padded
    dst = pstart[e_sorted] + jnp.arange(A, dtype=jnp.int32) - start[e_sorted]
    nb = -(-A // MOE_BLOCK) + N_EXPERTS
    buf_tok = jnp.full((nb * MOE_BLOCK,), T, jnp.int32).at[dst].set(tok_flat[order])
    buf_w = jnp.zeros((nb * MOE_BLOCK,), jnp.float32).at[dst].set(w.reshape(A)[order])
    blk_e = jnp.minimum(jnp.searchsorted(pend, jnp.arange(nb, dtype=jnp.int32) * MOE_BLOCK, side='right'),
                        N_EXPERTS - 1)
    x_pad = jnp.concatenate([x, jnp.zeros((1, D_MODEL), x.dtype)], axis=0)

    def run_block(args):
        tok, e = args
        xb = x_pad[tok]
        hb = jax.nn.silu(xb @ w_gate[e]) * (xb @ w_up[e])
        return hb @ w_down[e]

    yb = lax.map(run_block, (buf_tok.reshape(nb, MOE_BLOCK), blk_e))
    routed = jax.ops.segment_sum(yb.reshape(-1, D_MODEL).astype(jnp.float32) * buf_w[:, None], buf_tok,
                                 num_segments=T + 1)[:T]
    shared = (jax.nn.silu(x @ ws_gate) * (x @ ws_up)) @ ws_down
    return (routed + shared.astype(jnp.float32)).astype(x.dtype)


def trunk(x, cond, C0, n0, m0, S0, w_mod, b_mod, ln_g, ln_b, a_w_in, a_b_gates, a_norm, a_w_out,
          b_w_in, b_lb, b_norm, b_w_out, w_router, e_bias, w_gate, w_up, w_down, ws_gate, ws_up, ws_down):
    sm = jax.nn.softmax(b_lb.astype(jnp.float32), axis=0)
    lb_all = jnp.cumsum(sm, axis=0) - sm[0]
    new_C, new_n, new_m, new_S = [], [], [], []
    for i in range(DEPTH):
        j = i // 2
        mod = jax.nn.silu(cond) @ w_mod[i] + b_mod[i]
        sh_a, sc_a, g_a, sh_f, sc_f, g_f = jnp.split(mod[:, None, :], 6, axis=-1)
        h = x * (1 + sc_a) + sh_a
        if i % 2 == 0:
            out, C, n, m = mlstm_mixer(h, a_w_in[j], a_b_gates[j], a_norm[j], a_w_out[j],
                                       C0[:, j], n0[:, j], m0[:, j])
            new_C.append(C)
            new_n.append(n)
            new_m.append(m)
        else:
            out, S = hgrn_mixer(h, b_w_in[j], lb_all[i], b_norm[j], b_w_out[j], S0[:, j])
            new_S.append(S)
        x = layer_norm(ALPHA * x + g_a * out, ln_g[i, 0], ln_b[i, 0])
        h = x * (1 + sc_f) + sh_f
        ff = moe_ffn(h.reshape(-1, D_MODEL), w_router[i], e_bias[i], w_gate[i], w_up[i], w_down[i],
                     ws_gate[i], ws_up[i], ws_down[i]).reshape(x.shape)
        x = layer_norm(ALPHA * x + g_f * ff, ln_g[i, 1], ln_b[i, 1])
    return x, jnp.stack(new_C, axis=1), jnp.stack(new_n, axis=1), jnp.stack(new_m, axis=1), jnp.stack(new_S, axis=1)


def setup_inputs(seed: int = 0) -> dict:
    key = jax.random.key(seed)
    ks = jax.random.split(key, 32)

    def nrm(k, shape, s=1.0):
        return s * jax.random.normal(k, shape, jnp.float32)

    D = D_MODEL
    return {
        'x_prompt': nrm(ks[0], (BATCH, SEQ, D)),
        'x_sample': nrm(ks[1], (DEC_BATCH, DEC_SEQ, D)),
        'state_mlstm_C': nrm(ks[2], (DEC_BATCH, N_A_LAYERS, 2, NH_A, DK_A, DV_A), 0.5),
        'state_mlstm_n': nrm(ks[3], (DEC_BATCH, N_A_LAYERS, 2, NH_A, DK_A), 0.5),
        'state_mlstm_m': nrm(ks[4], (DEC_BATCH, N_A_LAYERS, 2, NH_A), 0.5),
        'state_hgrn_S': nrm(ks[5], (DEC_BATCH, N_B_LAYERS, 2, NH_B, DK_B, DV_B), 0.5),
        'c': nrm(ks[6], (DEC_BATCH, D)),
        'c_ctx': nrm(ks[7], (D,)),
        'w_mod': nrm(ks[8], (DEPTH, D, 6 * D), 0.5 * D ** -0.5),
        'b_mod': nrm(ks[9], (DEPTH, 6 * D), 0.02),
        'ln_g': 1.0 + nrm(ks[10], (DEPTH, 2, D), 0.02),
        'ln_b': nrm(ks[11], (DEPTH, 2, D), 0.02),
        'a_w_in': nrm(ks[12], (N_A_LAYERS, D, P_A), D ** -0.5),
        'a_b_gates': jnp.concatenate([nrm(ks[13], (N_A_LAYERS, 2, 1, NH_A), 0.1),
                                      3.0 + nrm(ks[14], (N_A_LAYERS, 2, 1, NH_A), 0.5)], axis=2),
        'a_norm': 1.0 + nrm(ks[15], (N_A_LAYERS, V_A), 0.02),
        'a_w_out': nrm(ks[16], (N_A_LAYERS, V_A, D), BETA * V_A ** -0.5),
        'b_w_in': nrm(ks[17], (N_B_LAYERS, D, 5 * D), D ** -0.5),
        'b_lb': nrm(ks[18], (DEPTH, 2, NH_B * DK_B), 0.5),
        'b_norm': 1.0 + nrm(ks[19], (N_B_LAYERS, NH_B * DV_B), 0.02),
        'b_w_out': nrm(ks[20], (N_B_LAYERS, NH_B * DV_B, D), BETA * (NH_B * DV_B) ** -0.5),
        'w_router': nrm(ks[21], (DEPTH, D, N_EXPERTS), D ** -0.5),
        'e_bias': nrm(ks[22], (DEPTH, N_EXPERTS), 0.01),
        'w_gate': nrm(ks[23], (DEPTH, N_EXPERTS, D, D_EXPERT), D ** -0.5),
        'w_up': nrm(ks[24], (DEPTH, N_EXPERTS, D, D_EXPERT), D ** -0.5),
        'w_down': nrm(ks[25], (DEPTH, N_EXPERTS, D_EXPERT, D), BETA * D_EXPERT ** -0.5),
        'ws_gate': nrm(ks[26], (DEPTH, D, D_SHARED), D ** -0.5),
        'ws_up': nrm(ks[27], (DEPTH, D, D_SHARED), D ** -0.5),
        'ws_down': nrm(ks[28], (DEPTH, D_SHARED, D), BETA * D_SHARED ** -0.5),
    }


def reference(x_prompt, x_sample, state_mlstm_C, state_mlstm_n, state_mlstm_m, state_hgrn_S, c, c_ctx,
              w_mod, b_mod, ln_g, ln_b, a_w_in, a_b_gates, a_norm, a_w_out, b_w_in, b_lb, b_norm, b_w_out,
              w_router, e_bias, w_gate, w_up, w_down, ws_gate, ws_up, ws_down):
    weights = (w_mod, b_mod, ln_g, ln_b, a_w_in, a_b_gates, a_norm, a_w_out, b_w_in, b_lb, b_norm, b_w_out,
               w_router, e_bias, w_gate, w_up, w_down, ws_gate, ws_up, ws_down)
    bp = x_prompt.shape[0]
    zC = jnp.zeros((bp, N_A_LAYERS, 2, NH_A, DK_A, DV_A), jnp.float32)
    zn = jnp.zeros((bp, N_A_LAYERS, 2, NH_A, DK_A), jnp.float32)
    zm = jnp.zeros((bp, N_A_LAYERS, 2, NH_A), jnp.float32)
    zS = jnp.zeros((bp, N_B_LAYERS, 2, NH_B, DK_B, DV_B), jnp.float32)
    y_prompt, new_C, new_n, new_m, new_S = trunk(x_prompt, c_ctx[None, :], zC, zn, zm, zS, *weights)
    ROWS = x_sample.shape[1] // GRID_W
    xs = x_sample + grid_pos_embed(ROWS, x_sample.dtype)[None]
    y_sample, _, _, _, _ = trunk(xs, c, state_mlstm_C, state_mlstm_n, state_mlstm_m, state_hgrn_S, *weights)
    return (y_prompt, y_sample, new_C, new_n, new_m, new_S)
```

```python
import functools

import jax
import jax.numpy as jnp
from jax import lax
from jax.experimental import pallas as pl
from jax.experimental.pallas import tpu as pltpu

F32 = jnp.float32
BF16 = jnp.bfloat16
HIGHEST = lax.Precision.HIGHEST

D = 1024
DEPTH = 2
GRID_W = 64
POS_BASE = 10000.0
EPS = 1e-6
ALPHA = (2.0 * DEPTH) ** 0.25
NH_A, DK_A, DV_A = 4, 128, 256
QK_A, V_A = NH_A * DK_A, NH_A * DV_A
NH_B, DK_B = 8, 128
N_EXPERTS, TOP_K, N_GROUPS, TOPK_GROUPS = 64, 8, 8, 4
GROUP_SIZE = N_EXPERTS // N_GROUPS
D_EXPERT = D // 4
ROUTED_SCALE = 2.5

LANES = 128
SUBLANES = 8
COND_ROWS = 8
TOK_TILE = 256
CHUNK_A = 256
VMEM_LIMIT = 56 * 1024 * 1024

NT_DIMS = (((1,), (1,)), ((), ()))


def _params(sem):
    return pltpu.CompilerParams(dimension_semantics=sem, vmem_limit_bytes=VMEM_LIMIT)


def _split3(x):
    hi = x.astype(BF16)
    r = x - hi.astype(F32)
    mid = r.astype(BF16)
    lo = (r - mid.astype(F32)).astype(BF16)
    return hi, mid, lo


def _dot3(a_bf, x, transpose_side=None):
    hi, mid, lo = _split3(x)
    return (jnp.dot(a_bf, hi, preferred_element_type=F32)
            + jnp.dot(a_bf, mid, preferred_element_type=F32)
            + jnp.dot(a_bf, lo, preferred_element_type=F32))


def _dot3_r(x, a_bf):
    hi, mid, lo = _split3(x)
    return (jnp.dot(hi, a_bf, preferred_element_type=F32)
            + jnp.dot(mid, a_bf, preferred_element_type=F32)
            + jnp.dot(lo, a_bf, preferred_element_type=F32))


def _log_sigmoid(x):
    return jnp.minimum(x, 0.0) - jnp.log1p(jnp.exp(-jnp.abs(x)))


def _layer_norm_rows(x, g, b):
    mu = jnp.mean(x, axis=-1, keepdims=True)
    xc = x - mu
    var = jnp.mean(xc * xc, axis=-1, keepdims=True)
    return xc * lax.rsqrt(var + EPS) * g + b


class Geometry:
    def __init__(self, n_prompt, prompt_len, n_sample, sample_len):
        self.n_prompt, self.prompt_len = n_prompt, prompt_len
        self.n_sample, self.sample_len = n_sample, sample_len
        self.t_prompt = n_prompt * prompt_len
        self.t_sample = n_sample * sample_len
        self.t = self.t_prompt + self.t_sample
        assert prompt_len % TOK_TILE == 0 and sample_len % TOK_TILE == 0
        assert n_sample + 1 <= COND_ROWS

    def cond_row(self, tile, tile_rows):
        n_p = self.t_prompt // tile_rows
        return jnp.where(tile < n_p, 0, 1 + (tile - n_p) // (self.sample_len // tile_rows))


def _mod_kernel(cond_ref, w_ref, b_ref, o_ref):
    c = cond_ref[...]
    s = c * jax.nn.sigmoid(c)
    o_ref[0, 0] = jnp.dot(s, w_ref[0], precision=HIGHEST, preferred_element_type=F32) + b_ref[0, 0]


def _modulation(cond, w_mod, b_mod):
    out = pl.pallas_call(
        _mod_kernel,
        grid=(DEPTH, 6),
        in_specs=[
            pl.BlockSpec((COND_ROWS, D), lambda l, j: (0, 0)),
            pl.BlockSpec((1, D, D), lambda l, j: (l, 0, j)),
            pl.BlockSpec((1, 1, 1, D), lambda l, j: (l, j, 0, 0)),
        ],
        out_specs=pl.BlockSpec((1, 1, COND_ROWS, D), lambda l, j: (l, j, 0, 0)),
        out_shape=jax.ShapeDtypeStruct((DEPTH, 6, COND_ROWS, D), F32),
        compiler_params=_params(("arbitrary", "arbitrary")),
    )(cond, w_mod, b_mod.reshape(DEPTH, 6, 1, D))
    return out.transpose(0, 2, 1, 3)


def _proj_a_kernel(x_ref, mod_ref, wq_ref, wkt_ref, wvo_ref, wg_ref, wgt_ref, bg_ref, bgt_ref,
                   q_ref, kt_ref, v_ref, so_ref, gc_ref, gr_ref):
    h = x_ref[...] * (1.0 + mod_ref[1:2, :]) + mod_ref[0:1, :]
    hb = h.astype(BF16)
    q_ref[...] = jnp.dot(hb, wq_ref[...], preferred_element_type=F32).astype(BF16)
    kt = lax.dot_general(wkt_ref[...], hb, NT_DIMS, preferred_element_type=F32)
    kt_ref[...] = (kt * (DK_A ** -0.5)).astype(BF16)
    vo = jnp.dot(hb, wvo_ref[...], preferred_element_type=F32)
    v_ref[...] = vo[:, :V_A].astype(BF16)
    so_ref[...] = jax.nn.sigmoid(vo[:, V_A:]).astype(BF16)
    gc_ref[...] = jnp.dot(h, wg_ref[...], precision=HIGHEST, preferred_element_type=F32) + bg_ref[...]
    gr_ref[...] = lax.dot_general(wgt_ref[...], h, NT_DIMS, precision=HIGHEST,
                                  preferred_element_type=F32) + bgt_ref[...]


def _proj_a(geo, x, mod_l, w_in, b_gates):
    t = geo.t
    n_gate = 4 * NH_A
    wq = w_in[:, :QK_A].astype(BF16)
    wkt = w_in[:, QK_A:2 * QK_A].T.astype(BF16)
    wvo = w_in[:, 2 * QK_A:2 * QK_A + 2 * V_A].astype(BF16)
    wg = w_in[:, 2 * QK_A + 2 * V_A:]
    wg_pad = jnp.pad(wg, ((0, 0), (0, LANES - n_gate)))
    bg = b_gates.reshape(n_gate).astype(F32)
    bg_pad = jnp.pad(bg, (0, LANES - n_gate)).reshape(1, LANES)
    tb = TOK_TILE
    full = lambda shape: pl.BlockSpec(shape, lambda i: (0,) * len(shape))
    return pl.pallas_call(
        _proj_a_kernel,
        grid=(t // tb,),
        in_specs=[
            pl.BlockSpec((tb, D), lambda i: (i, 0)),
            pl.BlockSpec((None, 6, D), lambda i: (geo.cond_row(i, tb), 0, 0)),
            full((D, QK_A)), full((QK_A, D)), full((D, 2 * V_A)), full((D, LANES)), full((n_gate, D)),
            full((1, LANES)), full((n_gate, 1)),
        ],
        out_specs=[
            pl.BlockSpec((tb, QK_A), lambda i: (i, 0)),
            pl.BlockSpec((QK_A, tb), lambda i: (0, i)),
            pl.BlockSpec((tb, V_A), lambda i: (i, 0)),
            pl.BlockSpec((tb, V_A), lambda i: (i, 0)),
            pl.BlockSpec((tb, LANES), lambda i: (i, 0)),
            pl.BlockSpec((n_gate, tb), lambda i: (0, i)),
        ],
        out_shape=[
            jax.ShapeDtypeStruct((t, QK_A), BF16),
            jax.ShapeDtypeStruct((QK_A, t), BF16),
            jax.ShapeDtypeStruct((t, V_A), BF16),
            jax.ShapeDtypeStruct((t, V_A), BF16),
            jax.ShapeDtypeStruct((t, LANES), F32),
            jax.ShapeDtypeStruct((n_gate, t), F32),
        ],
        compiler_params=_params(("parallel",)),
    )(x, mod_l, wq, wkt, wvo, wg_pad, wg.T, bg_pad, bg.reshape(n_gate, 1))


def _mlstm_scan_kernel(*refs, chunk, has_state):
    if has_state:
        (q_ref, kt_ref, v_ref, gc_ref, gr_ref, c0_ref, n0_ref, m0_ref,
         h_ref, c_out, n_out, m_out, c_sc, n_sc, m_sc) = refs
    else:
        (q_ref, kt_ref, v_ref, gc_ref, gr_ref,
         h_ref, c_out, n_out, m_out, c_sc, n_sc, m_sc) = refs
    L = chunk
    d = pl.program_id(1)
    c = pl.program_id(2)
    fwd = d == 0

    @pl.when(c == 0)
    def _():
        if has_state:
            c_sc[...] = c0_ref[0, 0]
            n_sc[...] = n0_ref[0, 0]
            m_sc[...] = m0_ref[0, 0]
        else:
            c_sc[...] = jnp.zeros_like(c_sc)
            n_sc[...] = jnp.zeros_like(n_sc)
            m_sc[...] = jnp.zeros_like(m_sc)

    row = lax.broadcasted_iota(jnp.int32, (L, L), 0)
    col = lax.broadcasted_iota(jnp.int32, (L, L), 1)
    sgn = 1 - 2 * d
    causal = (row - col) * sgn >= 0
    tri = causal.astype(BF16)
    tri_t = ((col - row) * sgn >= 0).astype(BF16)

    gc = gc_ref[...]
    gr = gr_ref[...]
    bc_all = _dot3(tri, _log_sigmoid(gc))
    br_all = _dot3_r(_log_sigmoid(gr), tri_t)
    ones_blk = (lax.broadcasted_iota(jnp.int32, (L, LANES), 1) == 0).astype(BF16)

    for h in range(NH_A):
        b_c = jnp.where(fwd, bc_all[:, 4 + h:5 + h], bc_all[:, 12 + h:13 + h])
        b_r = jnp.where(fwd, br_all[4 + h:5 + h, :], br_all[12 + h:13 + h, :])
        i_r = jnp.where(fwd, gr[h:h + 1, :], gr[8 + h:9 + h, :])
        bl = jnp.where(fwd, b_r[:, L - 1:L], b_r[:, 0:1])
        q = q_ref[:, h * DK_A:(h + 1) * DK_A]
        kt = kt_ref[h * DK_A:(h + 1) * DK_A, :]
        v = v_ref[:, h * DV_A:(h + 1) * DV_A]
        m = m_sc[h:h + 1, 0:1]
        cst = c_sc[h]
        nst = n_sc[h]

        a_r = i_r - b_r
        logd = jnp.where(causal, b_c + a_r, -jnp.inf)
        inter = b_c + m
        m_t = jnp.maximum(inter, jnp.max(logd, axis=1, keepdims=True))
        dmat = jnp.exp(logd - m_t)
        e_int = jnp.exp(inter - m_t)
        s = (jnp.dot(q, kt, preferred_element_type=F32) * dmat).astype(BF16)
        num = (jnp.dot(s, v, preferred_element_type=F32)
               + e_int * jnp.dot(q, cst.astype(BF16), preferred_element_type=F32))
        den = (jnp.dot(s, ones_blk, preferred_element_type=F32)
               + e_int * jnp.dot(q, nst.astype(BF16), preferred_element_type=F32))[:, 0:1]
        h_ref[:, h * DV_A:(h + 1) * DV_A] = num / jnp.maximum(jnp.abs(den), jnp.exp(-m_t))

        logw = bl + a_r
        m_new = jnp.maximum(bl + m, jnp.max(logw, axis=1, keepdims=True))
        w = jnp.exp(logw - m_new)
        decay = jnp.exp(bl + m - m_new)
        kw = (kt.astype(F32) * w).astype(BF16)
        c_sc[h] = decay * cst + jnp.dot(kw, v, preferred_element_type=F32)
        n_sc[h] = decay * nst + jnp.dot(kw, ones_blk, preferred_element_type=F32)
        m_sc[h:h + 1, :] = jnp.broadcast_to(m_new, (1, LANES))

    @pl.when(c == pl.num_programs(2) - 1)
    def _():
        c_out[0, 0] = c_sc[...]
        n_out[0, 0] = n_sc[...]
        m_out[0, 0] = m_sc[...]


def _mlstm_scan(q, kt, v, gc, gr, *, row0, n_seq, seq_len, state=None):
    L = CHUNK_A
    nc = seq_len // L
    blk0 = row0 // L

    def loc_blk(b, d, c):
        return b * nc + c + d * (nc - 1 - 2 * c)

    def tok_blk(b, d, c):
        return blk0 + loc_blk(b, d, c)

    in_specs = [
        pl.BlockSpec((L, QK_A), lambda b, d, c: (tok_blk(b, d, c), 0)),
        pl.BlockSpec((QK_A, L), lambda b, d, c: (0, tok_blk(b, d, c))),
        pl.BlockSpec((L, V_A), lambda b, d, c: (tok_blk(b, d, c), 0)),
        pl.BlockSpec((L, LANES), lambda b, d, c: (tok_blk(b, d, c), 0)),
        pl.BlockSpec((4 * NH_A, L), lambda b, d, c: (0, tok_blk(b, d, c))),
    ]
    args = [q, kt, v, gc, gr]
    if state is not None:
        in_specs += [
            pl.BlockSpec((1, 1, NH_A, DK_A, DV_A), lambda b, d, c: (b, d, 0, 0, 0)),
            pl.BlockSpec((1, 1, NH_A, DK_A, LANES), lambda b, d, c: (b, d, 0, 0, 0)),
            pl.BlockSpec((1, 1, SUBLANES, LANES), lambda b, d, c: (b, d, 0, 0)),
        ]
        args += list(state)
    return pl.pallas_call(
        functools.partial(_mlstm_scan_kernel, chunk=L, has_state=state is not None),
        grid=(n_seq, 2, nc),
        in_specs=in_specs,
        out_specs=[
            pl.BlockSpec((None, L, V_A), lambda b, d, c: (d, loc_blk(b, d, c), 0)),
            pl.BlockSpec((1, 1, NH_A, DK_A, DV_A), lambda b, d, c: (b, d, 0, 0, 0)),
            pl.BlockSpec((1, 1, NH_A, DK_A, LANES), lambda b, d, c: (b, d, 0, 0, 0)),
            pl.BlockSpec((1, 1, SUBLANES, LANES), lambda b, d, c: (b, d, 0, 0)),
        ],
        out_shape=[
            jax.ShapeDtypeStruct((2, n_seq * seq_len, V_A), F32),
            jax.ShapeDtypeStruct((n_seq, 2, NH_A, DK_A, DV_A), F32),
            jax.ShapeDtypeStruct((n_seq, 2, NH_A, DK_A, LANES), F32),
            jax.ShapeDtypeStruct((n_seq, 2, SUBLANES, LANES), F32),
        ],
        scratch_shapes=[
            pltpu.VMEM((NH_A, DK_A, DV_A), F32),
            pltpu.VMEM((NH_A, DK_A, LANES), F32),
            pltpu.VMEM((SUBLANES, LANES), F32),
        ],
        compiler_params=_params(("parallel", "parallel", "arbitrary")),
    )(*args)


def _out_a_kernel(hp_ref, hs_ref, so_ref, nw_ref, w_ref, x_ref, mod_ref, lg_ref, lb_ref, o_ref, *, n_prompt_tiles):
    is_prompt = pl.program_id(0) < n_prompt_tiles
    y = jnp.where(is_prompt, hp_ref[0] + hp_ref[1], hs_ref[0] + hs_ref[1])
    parts = []
    for h in range(NH_A):
        yh = y[:, h * DV_A:(h + 1) * DV_A]
        mu = jnp.mean(yh, axis=-1, keepdims=True)
        yc = yh - mu
        var = jnp.mean(yc * yc, axis=-1, keepdims=True)
        parts.append(yc * lax.rsqrt(var + EPS))
    yn = jnp.concatenate(parts, axis=-1) * nw_ref[...] * so_ref[...].astype(F32)
    out = jnp.dot(yn.astype(BF16), w_ref[...], preferred_element_type=F32)
    o_ref[...] = _layer_norm_rows(ALPHA * x_ref[...] + mod_ref[2:3, :] * out, lg_ref[...], lb_ref[...])


def _out_a(geo, h_prompt, h_sample, so, norm_w, w_out, x, mod_l, ln_g, ln_b):
    t = geo.t
    tb = TOK_TILE
    n_p = geo.t_prompt // tb
    full = lambda shape: pl.BlockSpec(shape, lambda i: (0,) * len(shape))
    return pl.pallas_call(
        functools.partial(_out_a_kernel, n_prompt_tiles=n_p),
        grid=(t // tb,),
        in_specs=[
            pl.BlockSpec((2, tb, V_A), lambda i: (0, jnp.minimum(i, n_p - 1), 0)),
            pl.BlockSpec((2, tb, V_A), lambda i: (0, jnp.maximum(i - n_p, 0), 0)),
            pl.BlockSpec((tb, V_A), lambda i: (i, 0)),
            full((1, V_A)), full((V_A, D)),
            pl.BlockSpec((tb, D), lambda i: (i, 0)),
            pl.BlockSpec((None, 6, D), lambda i: (geo.cond_row(i, tb), 0, 0)),
            full((1, D)), full((1, D)),
        ],
        out_specs=pl.BlockSpec((tb, D), lambda i: (i, 0)),
        out_shape=jax.ShapeDtypeStruct((t, D), F32),
        compiler_params=_params(("parallel",)),
    )(h_prompt, h_sample, so, norm_w.reshape(1, V_A).astype(F32), w_out.astype(BF16), x, mod_l,
      ln_g.reshape(1, D), ln_b.reshape(1, D))


def _proj_b_kernel(x_ref, mod_ref, w_ref, q_ref, pre_ref, v_ref, sg_ref):
    h = x_ref[...] * (1.0 + mod_ref[1:2, :]) + mod_ref[0:1, :]
    z = jnp.dot(h.astype(BF16), w_ref[...], preferred_element_type=F32)
    for hd in range(NH_B):
        lo = hd * DK_B
        qh = z[:, lo:lo + DK_B]
        q_ref[hd] = qh * jax.nn.sigmoid(qh)
        pre_ref[0, hd] = z[:, D + lo:D + lo + DK_B]
        pre_ref[1, hd] = z[:, 2 * D + lo:2 * D + lo + DK_B]
        v_ref[hd] = z[:, 3 * D + lo:3 * D + lo + DK_B].astype(BF16)
    g = z[:, 4 * D:]
    sg_ref[...] = (g * jax.nn.sigmoid(g)).astype(BF16)


def _proj_b(geo, x, mod_l, w_in):
    t = geo.t
    tb = TOK_TILE
    return pl.pallas_call(
        _proj_b_kernel,
        grid=(t // tb,),
        in_specs=[
            pl.BlockSpec((tb, D), lambda i: (i, 0)),
            pl.BlockSpec((None, 6, D), lambda i: (geo.cond_row(i, tb), 0, 0)),
            pl.BlockSpec((D, 5 * D), lambda i: (0, 0)),
        ],
        out_specs=[
            pl.BlockSpec((NH_B, tb, DK_B), lambda i: (0, i, 0)),
            pl.BlockSpec((2, NH_B, tb, DK_B), lambda i: (0, 0, i, 0)),
            pl.BlockSpec((NH_B, tb, DK_B), lambda i: (0, i, 0)),
            pl.BlockSpec((tb, D), lambda i: (i, 0)),
        ],
        out_shape=[
            jax.ShapeDtypeStruct((NH_B, t, DK_B), F32),
            jax.ShapeDtypeStruct((2, NH_B, t, DK_B), F32),
            jax.ShapeDtypeStruct((NH_B, t, DK_B), BF16),
            jax.ShapeDtypeStruct((t, D), BF16),
        ],
        compiler_params=_params(("parallel",)),
    )(x, mod_l, w_in.astype(BF16))


CHUNK_B = 128
BAND = SUBLANES
TN_DIMS = (((0,), (0,)), ((), ()))


def _hgrn_head(q, pre, lbv, v_bf, st, fwd):
    L = q.shape[0]
    f = lbv + (1.0 - lbv) * jax.nn.sigmoid(pre)
    lf = jnp.log(f)
    kk = (1.0 - lbv) * jax.nn.sigmoid(-pre)
    row = lax.broadcasted_iota(jnp.int32, (L, L), 0)
    col = lax.broadcasted_iota(jnp.int32, (L, L), 1)
    tri = ((row >= col) if fwd else (row <= col)).astype(BF16)
    b = _dot3(tri, lf)
    v32 = v_bf.astype(F32)
    tpos = lax.broadcasted_iota(jnp.int32, (L, DK_B), 0)

    o = jnp.sum(q * kk, axis=1, keepdims=True) * v32
    for dl in range(1, BAND):
        shift = dl if fwd else L - dl
        in_blk = (tpos % BAND >= dl) if fwd else (tpos % BAND + dl < BAND)
        e = jnp.exp(jnp.where(in_blk, b - pltpu.roll(b, shift, 0), -jnp.inf))
        a = jnp.sum(q * pltpu.roll(kk, shift, 0) * e, axis=1, keepdims=True)
        o = o + a * pltpu.roll(v32, shift, 0)

    att = jnp.zeros((L, L), F32)
    w = BAND
    while w < L:
        nb = L // (2 * w)
        b3 = b.reshape(nb, 2 * w, DK_B)
        edge = (b3[:, w - 1:w, :] if fwd else b3[:, w:w + 1, :])
        bmid = jnp.broadcast_to(edge, (nb, 2 * w, DK_B)).reshape(L, DK_B)
        second = (tpos // w) % 2 == 1
        t_side = second if fwd else jnp.logical_not(second)
        e = jnp.exp(jnp.where(t_side, b - bmid, bmid - b))
        qt = jnp.where(t_side, q * e, 0.0).astype(BF16)
        ks = jnp.where(t_side, 0.0, kk * e).astype(BF16)
        a = lax.dot_general(qt, ks, NT_DIMS, preferred_element_type=F32)
        att = att + jnp.where(row // (2 * w) == col // (2 * w), a, 0.0)
        w *= 2
    o = o + jnp.dot(att.astype(BF16), v_bf, preferred_element_type=F32)

    bl = b[L - 1:L, :] if fwd else b[0:1, :]
    o = o + lax.dot_general((q * jnp.exp(b)).astype(BF16), st.astype(BF16), NT_DIMS, preferred_element_type=F32)
    kd = (kk * jnp.exp(bl - b)).astype(BF16)
    st_new = jnp.exp(bl) * st + lax.dot_general(v_bf, kd, TN_DIMS, preferred_element_type=F32)
    return o, st_new


def _hgrn_scan_kernel(*refs, has_state):
    if has_state:
        q_ref, pre_ref, v_ref, lb_ref, s0_ref, o_ref, s_out, st_sc = refs
    else:
        q_ref, pre_ref, v_ref, lb_ref, o_ref, s_out, st_sc = refs
    d = pl.program_id(1)
    c = pl.program_id(2)

    @pl.when(c == 0)
    def _():
        if has_state:
            for hd in range(NH_B):
                st_sc[hd] = s0_ref[0, 0, hd].T
        else:
            st_sc[...] = jnp.zeros_like(st_sc)

    def run(fwd):
        def head(hd, carry):
            o, st_new = _hgrn_head(q_ref[hd], pre_ref[hd], lb_ref[hd], v_ref[hd], st_sc[hd], fwd)
            o_ref[hd] = o
            st_sc[hd] = st_new
            return carry
        lax.fori_loop(0, NH_B, head, 0)

    @pl.when(d == 0)
    def _():
        run(True)

    @pl.when(d == 1)
    def _():
        run(False)

    @pl.when(c == pl.num_programs(2) - 1)
    def _():
        for hd in range(NH_B):
            s_out[0, 0, hd] = st_sc[hd].T


def _hgrn_scan(q, pre, v, lbd, *, row0, n_seq, seq_len, state=None):
    L = CHUNK_B
    nc = seq_len // L
    blk0 = row0 // L

    def loc_blk(b, d, c):
        return b * nc + c + d * (nc - 1 - 2 * c)

    def tok_blk(b, d, c):
        return blk0 + loc_blk(b, d, c)

    in_specs = [
        pl.BlockSpec((NH_B, L, DK_B), lambda b, d, c: (0, tok_blk(b, d, c), 0)),
        pl.BlockSpec((None, NH_B, L, DK_B), lambda b, d, c: (d, 0, tok_blk(b, d, c), 0)),
        pl.BlockSpec((NH_B, L, DK_B), lambda b, d, c: (0, tok_blk(b, d, c), 0)),
        pl.BlockSpec((None, NH_B, 1, DK_B), lambda b, d, c: (d, 0, 0, 0)),
    ]
    args = [q, pre, v, lbd]
    if state is not None:
        in_specs.append(pl.BlockSpec((1, 1, NH_B, DK_B, DK_B), lambda b, d, c: (b, d, 0, 0, 0)))
        args.append(state)
    return pl.pallas_call(
        functools.partial(_hgrn_scan_kernel, has_state=state is not None),
        grid=(n_seq, 2, nc),
        in_specs=in_specs,
        out_specs=[
            pl.BlockSpec((None, NH_B, L, DK_B), lambda b, d, c: (d, 0, loc_blk(b, d, c), 0)),
            pl.BlockSpec((1, 1, NH_B, DK_B, DK_B), lambda b, d, c: (b, d, 0, 0, 0)),
        ],
        out_shape=[
            jax.ShapeDtypeStruct((2, NH_B, n_seq * seq_len, DK_B), F32),
            jax.ShapeDtypeStruct((n_seq, 2, NH_B, DK_B, DK_B), F32),
        ],
        scratch_shapes=[pltpu.VMEM((NH_B, DK_B, DK_B), F32)],
        compiler_params=_params(("parallel", "parallel", "arbitrary")),
    )(*args)


def _out_b_kernel(op_ref, os_ref, sg_ref, nw_ref, w_ref, x_ref, mod_ref, lg_ref, lb_ref, out_ref, *, n_prompt_tiles):
    is_prompt = pl.program_id(0) < n_prompt_tiles
    parts = []
    for hd in range(NH_B):
        y = jnp.where(is_prompt, op_ref[0, hd] + op_ref[1, hd], os_ref[0, hd] + os_ref[1, hd])
        parts.append(y * lax.rsqrt(jnp.mean(y * y, axis=-1, keepdims=True) + EPS))
    yn = jnp.concatenate(parts, axis=-1) * nw_ref[...] * sg_ref[...].astype(F32)
    out = jnp.dot(yn.astype(BF16), w_ref[...], preferred_element_type=F32)
    out_ref[...] = _layer_norm_rows(ALPHA * x_ref[...] + mod_ref[2:3, :] * out, lg_ref[...], lb_ref[...])


def _out_b(geo, o_prompt, o_sample, sg, norm_w, w_out, x, mod_l, ln_g, ln_b):
    t = geo.t
    tb = TOK_TILE
    n_p = geo.t_prompt // tb
    full = lambda shape: pl.BlockSpec(shape, lambda i: (0,) * len(shape))
    return pl.pallas_call(
        functools.partial(_out_b_kernel, n_prompt_tiles=n_p),
        grid=(t // tb,),
        in_specs=[
            pl.BlockSpec((2, NH_B, tb, DK_B), lambda i: (0, 0, jnp.minimum(i, n_p - 1), 0)),
            pl.BlockSpec((2, NH_B, tb, DK_B), lambda i: (0, 0, jnp.maximum(i - n_p, 0), 0)),
            pl.BlockSpec((tb, D), lambda i: (i, 0)),
            full((1, D)), full((D, D)),
            pl.BlockSpec((tb, D), lambda i: (i, 0)),
            pl.BlockSpec((None, 6, D), lambda i: (geo.cond_row(i, tb), 0, 0)),
            full((1, D)), full((1, D)),
        ],
        out_specs=pl.BlockSpec((tb, D), lambda i: (i, 0)),
        out_shape=jax.ShapeDtypeStruct((t, D), F32),
        compiler_params=_params(("parallel",)),
    )(o_prompt, o_sample, sg, norm_w.reshape(1, D).astype(F32), w_out.astype(BF16), x, mod_l,
      ln_g.reshape(1, D), ln_b.reshape(1, D))


MOE_BLK = 256
ROW_TILES = D // LANES


def _first_index(hit, iota, size, axis):
    return jnp.min(jnp.where(hit, iota, size), axis=axis, keepdims=True)


def _router_kernel(x_ref, mod_ref, wrt_ref, eb_ref, e_ref, w_ref, r_ref, cnt_ref, cnt_sc):
    i = pl.program_id(0)
    tb = x_ref.shape[0]

    @pl.when(i == 0)
    def _():
        cnt_sc[...] = jnp.zeros_like(cnt_sc)

    h = x_ref[...] * (1.0 + mod_ref[4:5, :]) + mod_ref[3:4, :]
    logits = lax.dot_general(wrt_ref[...], h, NT_DIMS, precision=HIGHEST, preferred_element_type=F32)
    scores = jax.nn.sigmoid(logits)
    sel = scores + eb_ref[...]

    g3 = sel.reshape(N_GROUPS, GROUP_SIZE, tb)
    io3 = lax.broadcasted_iota(jnp.int32, g3.shape, 1)
    m1 = jnp.max(g3, axis=1, keepdims=True)
    first = _first_index(g3 == m1, io3, GROUP_SIZE, 1)
    m2 = jnp.max(jnp.where(io3 == first, -jnp.inf, g3), axis=1, keepdims=True)
    gscore = (m1 + m2).reshape(N_GROUPS, tb)

    iog = lax.broadcasted_iota(jnp.int32, gscore.shape, 0)
    gmask = jnp.zeros(gscore.shape, F32)
    for _ in range(TOPK_GROUPS):
        gm = jnp.max(gscore, axis=0, keepdims=True)
        pick = iog == _first_index(gscore == gm, iog, N_GROUPS, 0)
        gmask = jnp.where(pick, 1.0, gmask)
        gscore = jnp.where(pick, -jnp.inf, gscore)
    emask = jnp.broadcast_to(gmask.reshape(N_GROUPS, 1, tb), (N_GROUPS, GROUP_SIZE, tb)).reshape(N_EXPERTS, tb)
    cand = jnp.where(emask > 0.0, sel, -jnp.inf)

    ioe = lax.broadcasted_iota(jnp.int32, cand.shape, 0)
    picks, wts = [], []
    onehot = jnp.zeros(cand.shape, F32)
    for _ in range(TOP_K):
        cm = jnp.max(cand, axis=0, keepdims=True)
        idx = _first_index(cand == cm, ioe, N_EXPERTS, 0)
        pick = ioe == idx
        picks.append(pick)
        wts.append(jnp.sum(jnp.where(pick, scores, 0.0), axis=0, keepdims=True))
        onehot = onehot + pick.astype(F32)
        cand = jnp.where(pick, -jnp.inf, cand)
        e_ref[pl.ds(len(picks) - 1, 1), :] = idx
    wsum = wts[0]
    for wk in wts[1:]:
        wsum = wsum + wk
    for k in range(TOP_K):
        w_ref[pl.ds(k, 1), :] = wts[k] / wsum * ROUTED_SCALE

    r_io = lax.broadcasted_iota(jnp.int32, (tb, tb), 0)
    c_io = lax.broadcasted_iota(jnp.int32, (tb, tb), 1)
    before = (r_io < c_io).astype(BF16)
    rank = cnt_sc[:, 0:1] + jnp.dot(onehot.astype(BF16), before, preferred_element_type=F32)
    for k in range(TOP_K):
        r_ref[pl.ds(k, 1), :] = jnp.sum(jnp.where(picks[k], rank, 0.0), axis=0, keepdims=True).astype(jnp.int32)
    cnt_sc[...] = cnt_sc[...] + jnp.sum(onehot, axis=1, keepdims=True)
    cnt_ref[...] = cnt_sc[...]


def _router(geo, x, mod_l, w_router, e_bias):
    t = geo.t
    tb = TOK_TILE
    full = lambda shape: pl.BlockSpec(shape, lambda i: (0,) * len(shape))
    e, w, r, cnt = pl.pallas_call(
        _router_kernel,
        grid=(t // tb,),
        in_specs=[
            pl.BlockSpec((tb, D), lambda i: (i, 0)),
            pl.BlockSpec((None, 6, D), lambda i: (geo.cond_row(i, tb), 0, 0)),
            full((N_EXPERTS, D)), full((N_EXPERTS, 1)),
        ],
        out_specs=[
            pl.BlockSpec((TOP_K, tb), lambda i: (0, i)),
            pl.BlockSpec((TOP_K, tb), lambda i: (0, i)),
            pl.BlockSpec((TOP_K, tb), lambda i: (0, i)),
            full((N_EXPERTS, LANES)),
        ],
        out_shape=[
            jax.ShapeDtypeStruct((TOP_K, t), jnp.int32),
            jax.ShapeDtypeStruct((TOP_K, t), F32),
            jax.ShapeDtypeStruct((TOP_K, t), jnp.int32),
            jax.ShapeDtypeStruct((N_EXPERTS, LANES), F32),
        ],
        scratch_shapes=[pltpu.VMEM((N_EXPERTS, LANES), F32)],
        compiler_params=_params(("arbitrary",)),
    )(x, mod_l, w_router.T.astype(F32), e_bias.reshape(N_EXPERTS, 1).astype(F32))
    return e, w, r, cnt[:, 0].astype(jnp.int32)


def _slot_kernel(pstart_ref, e_ref, r_ref, o_ref):
    e = e_ref[...]
    slot = r_ref[...]
    for x in range(N_EXPERTS):
        slot = slot + jnp.where(e == x, pstart_ref[x], 0)
    o_ref[...] = slot


def _slots(geo, pstart, top_e, rank):
    tb = TOK_TILE
    nt = geo.t // tb
    return pl.pallas_call(
        _slot_kernel,
        grid_spec=pltpu.PrefetchScalarGridSpec(
            num_scalar_prefetch=1,
            grid=(nt,),
            in_specs=[pl.BlockSpec((TOP_K, tb), lambda i, p: (0, i)),
                      pl.BlockSpec((TOP_K, tb), lambda i, p: (0, i))],
            out_specs=pl.BlockSpec((None, TOP_K, tb), lambda i, p: (i, 0, 0)),
        ),
        out_shape=jax.ShapeDtypeStruct((nt, TOP_K, tb), jnp.int32),
        compiler_params=_params(("parallel",)),
    )(pstart, top_e, rank)


def _row_slab(ref, row):
    return ref.at[pl.ds(pl.multiple_of(row * ROW_TILES, ROW_TILES), ROW_TILES)]


def _dispatch_kernel(x_ref, mod_ref, slot_hbm, xs_in, xs_hbm, hs, slot_sm, sem_i, sem_d):
    del xs_in
    i = pl.program_id(0)
    tb = x_ref.shape[0]
    idx_copy = pltpu.make_async_copy(slot_hbm.at[i], slot_sm, sem_i)
    idx_copy.start()
    h = x_ref[...] * (1.0 + mod_ref[4:5, :]) + mod_ref[3:4, :]
    for j in range(ROW_TILES):
        hs[pl.ds(j, tb, stride=ROW_TILES), :] = h[:, j * LANES:(j + 1) * LANES]
    idx_copy.wait()

    def issue(tok, carry):
        for k in range(TOP_K):
            pltpu.make_async_copy(_row_slab(hs, tok), _row_slab(xs_hbm, slot_sm[k, tok]), sem_d).start()
        return carry

    lax.fori_loop(0, tb, issue, 0)
    for k in range(TOP_K):
        pltpu.make_async_copy(hs, xs_hbm.at[pl.ds(0, tb * ROW_TILES)], sem_d).wait()


def _dispatch(geo, x, mod_l, slots, n_rows):
    tb = TOK_TILE
    xs0 = jnp.zeros((n_rows * ROW_TILES, LANES), F32)
    return pl.pallas_call(
        _dispatch_kernel,
        grid=(geo.t // tb,),
        in_specs=[
            pl.BlockSpec((tb, D), lambda i: (i, 0)),
            pl.BlockSpec((None, 6, D), lambda i: (geo.cond_row(i, tb), 0, 0)),
            pl.BlockSpec(memory_space=pl.ANY),
            pl.BlockSpec(memory_space=pl.ANY),
        ],
        out_specs=pl.BlockSpec(memory_space=pl.ANY),
        out_shape=jax.ShapeDtypeStruct((n_rows * ROW_TILES, LANES), F32),
        scratch_shapes=[
            pltpu.VMEM((tb * ROW_TILES, LANES), F32),
            pltpu.SMEM((TOP_K, tb), jnp.int32),
            pltpu.SemaphoreType.DMA(()),
            pltpu.SemaphoreType.DMA(()),
        ],
        input_output_aliases={3: 0},
        compiler_params=_params(("arbitrary",)),
    )(x, mod_l, slots, xs0)


def _ffn_kernel(blk_e_ref, n_used_ref, xs_ref, wg_ref, wu_ref, wd_ref, y_ref):
    del blk_e_ref
    b = pl.program_id(0)

    @pl.when(b < n_used_ref[0])
    def _():
        x = jnp.concatenate([xs_ref[pl.ds(j, MOE_BLK, stride=ROW_TILES), :] for j in range(ROW_TILES)],
                            axis=1).astype(BF16)
        g = jnp.dot(x, wg_ref[...], preferred_element_type=F32)
        u = jnp.dot(x, wu_ref[...], preferred_element_type=F32)
        hmid = (g * jax.nn.sigmoid(g) * u).astype(BF16)
        y = jnp.dot(hmid, wd_ref[...], preferred_element_type=F32)
        for j in range(ROW_TILES):
            y_ref[pl.ds(j, MOE_BLK, stride=ROW_TILES), :] = y[:, j * LANES:(j + 1) * LANES]

    @pl.when(b >= n_used_ref[0])
    def _():
        y_ref[...] = jnp.zeros_like(y_ref)


def _ffn(xs, blk_e, n_used, wg, wu, wd, n_blocks):
    def blk(b, be, nu):
        return jnp.minimum(b, nu[0] - 1)

    rows = MOE_BLK * ROW_TILES
    return pl.pallas_call(
        _ffn_kernel,
        grid_spec=pltpu.PrefetchScalarGridSpec(
            num_scalar_prefetch=2,
            grid=(n_blocks,),
            in_specs=[
                pl.BlockSpec((rows, LANES), lambda b, be, nu: (blk(b, be, nu), 0)),
                pl.BlockSpec((None, D, D_EXPERT), lambda b, be, nu: (be[blk(b, be, nu)], 0, 0)),
                pl.BlockSpec((None, D, D_EXPERT), lambda b, be, nu: (be[blk(b, be, nu)], 0, 0)),
                pl.BlockSpec((None, D_EXPERT, D), lambda b, be, nu: (be[blk(b, be, nu)], 0, 0)),
            ],
            out_specs=pl.BlockSpec((rows, LANES), lambda b, be, nu: (b, 0)),
        ),
        out_shape=jax.ShapeDtypeStruct(xs.shape, F32),
        compiler_params=_params(("arbitrary",)),
    )(blk_e, n_used, xs, wg, wu, wd)


def _combine_kernel(x_ref, mod_ref, wt_ref, slot_hbm, yb_hbm, sg_ref, su_ref, sd_ref, lg_ref, lb_ref,
                    o_ref, rows, slot_sm, sem_i, sem_d):
    i = pl.program_id(0)
    tb = x_ref.shape[0]
    idx_copy = pltpu.make_async_copy(slot_hbm.at[i], slot_sm, sem_i)
    idx_copy.start()
    idx_copy.wait()

    def issue(tok, carry):
        for k in range(TOP_K):
            pltpu.make_async_copy(_row_slab(yb_hbm, slot_sm[k, tok]), _row_slab(rows, k * tb + tok), sem_d).start()
        return carry

    lax.fori_loop(0, tb, issue, 0)

    x = x_ref[...]
    hb = (x * (1.0 + mod_ref[4:5, :]) + mod_ref[3:4, :]).astype(BF16)
    g = jnp.dot(hb, sg_ref[...], preferred_element_type=F32)
    u = jnp.dot(hb, su_ref[...], preferred_element_type=F32)
    shared = jnp.dot((g * jax.nn.sigmoid(g) * u).astype(BF16), sd_ref[...], preferred_element_type=F32)

    pltpu.make_async_copy(yb_hbm.at[pl.ds(0, TOP_K * tb * ROW_TILES)], rows, sem_d).wait()
    routed = jnp.zeros((tb, D), F32)
    for k in range(TOP_K):
        base = k * tb * ROW_TILES
        yk = jnp.concatenate([rows[pl.ds(base + j, tb, stride=ROW_TILES), :] for j in range(ROW_TILES)], axis=1)
        routed = routed + yk * wt_ref[:, k:k + 1]
    ff = routed + shared
    o_ref[...] = _layer_norm_rows(ALPHA * x + mod_ref[5:6, :] * ff, lg_ref[...], lb_ref[...])


def _combine(geo, x, mod_l, wt, slots, yb, sg, su, sd, ln_g, ln_b):
    tb = TOK_TILE
    full = lambda shape: pl.BlockSpec(shape, lambda i: (0,) * len(shape))
    return pl.pallas_call(
        _combine_kernel,
        grid=(geo.t // tb,),
        in_specs=[
            pl.BlockSpec((tb, D), lambda i: (i, 0)),
            pl.BlockSpec((None, 6, D), lambda i: (geo.cond_row(i, tb), 0, 0)),
            pl.BlockSpec((tb, TOP_K), lambda i: (i, 0)),
            pl.BlockSpec(memory_space=pl.ANY),
            pl.BlockSpec(memory_space=pl.ANY),
            full((D, D_EXPERT)), full((D, D_EXPERT)), full((D_EXPERT, D)), full((1, D)), full((1, D)),
        ],
        out_specs=pl.BlockSpec((tb, D), lambda i: (i, 0)),
        out_shape=jax.ShapeDtypeStruct((geo.t, D), F32),
        scratch_shapes=[
            pltpu.VMEM((TOP_K * tb * ROW_TILES, LANES), F32),
            pltpu.SMEM((TOP_K, tb), jnp.int32),
            pltpu.SemaphoreType.DMA(()),
            pltpu.SemaphoreType.DMA(()),
        ],
        compiler_params=_params(("arbitrary",)),
    )(x, mod_l, wt, slots, yb, sg.astype(BF16), su.astype(BF16), sd.astype(BF16),
      ln_g.reshape(1, D), ln_b.reshape(1, D))


def _moe_layer(geo, x, mod_l, w_router, e_bias, wg, wu, wd, sg, su, sd, ln_g, ln_b):
    top_e, w, rank, counts = _router(geo, x, mod_l, w_router, e_bias)
    n_blocks = (geo.t * TOP_K) // MOE_BLK + N_EXPERTS
    padded = (counts + MOE_BLK - 1) // MOE_BLK * MOE_BLK
    pend = jnp.cumsum(padded)
    pstart = (pend - padded).astype(jnp.int32)
    blk_row0 = jnp.arange(n_blocks, dtype=jnp.int32) * MOE_BLK
    blk_e = jnp.minimum(jnp.sum((pend[None, :] <= blk_row0[:, None]).astype(jnp.int32), axis=1), N_EXPERTS - 1)
    n_used = (pend[-1:] // MOE_BLK).astype(jnp.int32)
    slots = _slots(geo, pstart, top_e, rank)
    xs = _dispatch(geo, x, mod_l, slots, n_blocks * MOE_BLK)
    yb = _ffn(xs, blk_e, n_used, wg, wu, wd, n_blocks)
    return _combine(geo, x, mod_l, w.T, slots, yb, sg, su, sd, ln_g, ln_b)


def _pos_embed(rows):
    quarter = D // 4
    omega = 1.0 / (POS_BASE ** (jnp.arange(quarter, dtype=F32) / quarter))
    r, col = jnp.meshgrid(jnp.arange(rows, dtype=F32), jnp.arange(GRID_W, dtype=F32), indexing='ij')
    r = r.reshape(-1, 1) * omega
    col = col.reshape(-1, 1) * omega
    return jnp.concatenate([jnp.sin(r), jnp.cos(r), jnp.sin(col), jnp.cos(col)], axis=-1)


def _mlstm_layer(geo, x, mod_l, j, a_w_in, a_b_gates, a_norm, a_w_out, ln_g, ln_b,
                 state_C, state_n, state_m):
    q, kt, v, so, gc, gr = _proj_a(geo, x, mod_l, a_w_in[j], a_b_gates[j])
    hp, c_p, n_p, m_p = _mlstm_scan(q, kt, v, gc, gr, row0=0, n_seq=geo.n_prompt, seq_len=geo.prompt_len)
    ns = geo.n_sample
    n0 = jnp.pad(state_n[:, j].astype(F32)[..., None], ((0, 0),) * 4 + ((0, LANES - 1),))
    m0 = jnp.pad(state_m[:, j].astype(F32), ((0, 0), (0, 0), (0, SUBLANES - NH_A)))
    m0 = jnp.broadcast_to(m0[..., None], (ns, 2, SUBLANES, LANES))
    hs, _, _, _ = _mlstm_scan(q, kt, v, gc, gr, row0=geo.t_prompt, n_seq=ns, seq_len=geo.sample_len,
                              state=(state_C[:, j].astype(F32), n0, m0))
    x1 = _out_a(geo, hp, hs, so, a_norm[j], a_w_out[j], x, mod_l, ln_g, ln_b)
    return x1, c_p, n_p[..., 0], m_p[:, :, :NH_A, 0]


def _hgrn_layer(geo, x, mod_l, j, lb_layer, b_w_in, b_norm, b_w_out, ln_g, ln_b, state_S):
    q, pre, v, sg = _proj_b(geo, x, mod_l, b_w_in[j])
    lbd = lb_layer.reshape(2, NH_B, 1, DK_B)
    op, s_p = _hgrn_scan(q, pre, v, lbd, row0=0, n_seq=geo.n_prompt, seq_len=geo.prompt_len)
    os_, _ = _hgrn_scan(q, pre, v, lbd, row0=geo.t_prompt, n_seq=geo.n_sample, seq_len=geo.sample_len,
                        state=state_S[:, j].astype(F32))
    x1 = _out_b(geo, op, os_, sg, b_norm[j], b_w_out[j], x, mod_l, ln_g, ln_b)
    return x1, s_p


def kernel(x_prompt, x_sample, state_mlstm_C, state_mlstm_n, state_mlstm_m, state_hgrn_S, c, c_ctx, w_mod, b_mod, ln_g, ln_b, a_w_in, a_b_gates, a_norm, a_w_out, b_w_in, b_lb, b_norm, b_w_out, w_router, e_bias, w_gate, w_up, w_down, ws_gate, ws_up, ws_down):
    bp, sp, _ = x_prompt.shape
    bs, ss, _ = x_sample.shape
    geo = Geometry(bp, sp, bs, ss)
    cond = jnp.zeros((COND_ROWS, D), F32).at[0].set(c_ctx).at[1:1 + bs].set(c)
    mod = _modulation(cond, w_mod, b_mod)
    x = jnp.concatenate([x_prompt.reshape(-1, D), (x_sample + _pos_embed(ss // GRID_W)[None]).reshape(-1, D)], axis=0)
    x1, new_c, new_n, new_m = _mlstm_layer(geo, x, mod[0], 0, a_w_in, a_b_gates, a_norm, a_w_out,
                                           ln_g[0, 0], ln_b[0, 0], state_mlstm_C, state_mlstm_n, state_mlstm_m)
    wg, wu, wd = w_gate.astype(BF16), w_up.astype(BF16), w_down.astype(BF16)
    x2 = _moe_layer(geo, x1, mod[0], w_router[0], e_bias[0], wg[0], wu[0], wd[0], ws_gate[0], ws_up[0], ws_down[0],
                    ln_g[0, 1], ln_b[0, 1])
    sm = jax.nn.softmax(b_lb.astype(F32), axis=0)
    lb_all = jnp.cumsum(sm, axis=0) - sm[0]
    x3, new_s = _hgrn_layer(geo, x2, mod[1], 0, lb_all[1], b_w_in, b_norm, b_w_out, ln_g[1, 0], ln_b[1, 0],
                            state_hgrn_S)
    x4 = _moe_layer(geo, x3, mod[1], w_router[1], e_bias[1], wg[1], wu[1], wd[1], ws_gate[1], ws_up[1], ws_down[1],
                    ln_g[1, 1], ln_b[1, 1])
    y_prompt = x4[:geo.t_prompt].reshape(bp, sp, D)
    y_sample = x4[geo.t_prompt:].reshape(bs, ss, D)
    return y_prompt, y_sample, new_c[:, None], new_n[:, None], new_m[:, None], new_s[:, None]
```

```python
import functools

import jax
import jax.numpy as jnp
from jax import lax
from jax.experimental import pallas as pl
from jax.experimental.pallas import tpu as pltpu

F32 = jnp.float32
BF16 = jnp.bfloat16
HIGHEST = lax.Precision.HIGHEST

D = 1024
DEPTH = 2
GRID_W = 64
POS_BASE = 10000.0
EPS = 1e-6
ALPHA = (2.0 * DEPTH) ** 0.25
NH_A, DK_A, DV_A = 4, 128, 256
QK_A, V_A = NH_A * DK_A, NH_A * DV_A
NH_B, DK_B = 8, 128
N_EXPERTS, TOP_K, N_GROUPS, TOPK_GROUPS = 64, 8, 8, 4
GROUP_SIZE = N_EXPERTS // N_GROUPS
D_EXPERT = D // 4
ROUTED_SCALE = 2.5

LANES = 128
SUBLANES = 8
COND_ROWS = 8
TOK_TILE = 256
CHUNK_A = 256
VMEM_LIMIT = 56 * 1024 * 1024

NT_DIMS = (((1,), (1,)), ((), ()))


def _params(sem):
    return pltpu.CompilerParams(dimension_semantics=sem, vmem_limit_bytes=VMEM_LIMIT)


def _split3(x):
    hi = x.astype(BF16)
    r = x - hi.astype(F32)
    mid = r.astype(BF16)
    lo = (r - mid.astype(F32)).astype(BF16)
    return hi, mid, lo


def _dot3(a_bf, x, transpose_side=None):
    hi, mid, lo = _split3(x)
    return (jnp.dot(a_bf, hi, preferred_element_type=F32)
            + jnp.dot(a_bf, mid, preferred_element_type=F32)
            + jnp.dot(a_bf, lo, preferred_element_type=F32))


def _dot3_r(x, a_bf):
    hi, mid, lo = _split3(x)
    return (jnp.dot(hi, a_bf, preferred_element_type=F32)
            + jnp.dot(mid, a_bf, preferred_element_type=F32)
            + jnp.dot(lo, a_bf, preferred_element_type=F32))


def _log_sigmoid(x):
    return jnp.minimum(x, 0.0) - jnp.log1p(jnp.exp(-jnp.abs(x)))


def _layer_norm_rows(x, g, b):
    mu = jnp.mean(x, axis=-1, keepdims=True)
    xc = x - mu
    var = jnp.mean(xc * xc, axis=-1, keepdims=True)
    return xc * lax.rsqrt(var + EPS) * g + b


class Geometry:
    def __init__(self, n_prompt, prompt_len, n_sample, sample_len):
        self.n_prompt, self.prompt_len = n_prompt, prompt_len
        self.n_sample, self.sample_len = n_sample, sample_len
        self.t_prompt = n_prompt * prompt_len
        self.t_sample = n_sample * sample_len
        self.t = self.t_prompt + self.t_sample
        assert prompt_len % TOK_TILE == 0 and sample_len % TOK_TILE == 0
        assert n_sample + 1 <= COND_ROWS

    def cond_row(self, tile, tile_rows):
        n_p = self.t_prompt // tile_rows
        return jnp.where(tile < n_p, 0, 1 + (tile - n_p) // (self.sample_len // tile_rows))


def _mod_kernel(cond_ref, w_ref, b_ref, o_ref):
    c = cond_ref[...]
    s = c * jax.nn.sigmoid(c)
    o_ref[0, 0] = jnp.dot(s, w_ref[0], precision=HIGHEST, preferred_element_type=F32) + b_ref[0, 0]


def _modulation(cond, w_mod, b_mod):
    out = pl.pallas_call(
        _mod_kernel,
        name="modulation",
        grid=(DEPTH, 6),
        in_specs=[
            pl.BlockSpec((COND_ROWS, D), lambda l, j: (0, 0)),
            pl.BlockSpec((1, D, D), lambda l, j: (l, 0, j)),
            pl.BlockSpec((1, 1, 1, D), lambda l, j: (l, j, 0, 0)),
        ],
        out_specs=pl.BlockSpec((1, 1, COND_ROWS, D), lambda l, j: (l, j, 0, 0)),
        out_shape=jax.ShapeDtypeStruct((DEPTH, 6, COND_ROWS, D), F32),
        compiler_params=_params(("arbitrary", "arbitrary")),
    )(cond, w_mod, b_mod.reshape(DEPTH, 6, 1, D))
    return out.transpose(0, 2, 1, 3)


def _proj_a_kernel(x_ref, mod_ref, wq_ref, wkt_ref, wvo_ref, wg_ref, wgt_ref, bg_ref, bgt_ref,
                   q_ref, kt_ref, v_ref, so_ref, gc_ref, gr_ref):
    h = x_ref[...] * (1.0 + mod_ref[1:2, :]) + mod_ref[0:1, :]
    hb = h.astype(BF16)
    q_ref[...] = jnp.dot(hb, wq_ref[...], preferred_element_type=F32).astype(BF16)
    kt = lax.dot_general(wkt_ref[...], hb, NT_DIMS, preferred_element_type=F32)
    kt_ref[...] = (kt * (DK_A ** -0.5)).astype(BF16)
    vo = jnp.dot(hb, wvo_ref[...], preferred_element_type=F32)
    v_ref[...] = vo[:, :V_A].astype(BF16)
    so_ref[...] = jax.nn.sigmoid(vo[:, V_A:]).astype(BF16)
    gc_ref[...] = jnp.dot(h, wg_ref[...], precision=HIGHEST, preferred_element_type=F32) + bg_ref[...]
    gr_ref[...] = lax.dot_general(wgt_ref[...], h, NT_DIMS, precision=HIGHEST,
                                  preferred_element_type=F32) + bgt_ref[...]


def _proj_a(geo, x, mod_l, w_in, b_gates):
    t = geo.t
    n_gate = 4 * NH_A
    wq = w_in[:, :QK_A].astype(BF16)
    wkt = w_in[:, QK_A:2 * QK_A].T.astype(BF16)
    wvo = w_in[:, 2 * QK_A:2 * QK_A + 2 * V_A].astype(BF16)
    wg = w_in[:, 2 * QK_A + 2 * V_A:]
    wg_pad = jnp.pad(wg, ((0, 0), (0, LANES - n_gate)))
    bg = b_gates.reshape(n_gate).astype(F32)
    bg_pad = jnp.pad(bg, (0, LANES - n_gate)).reshape(1, LANES)
    tb = TOK_TILE
    full = lambda shape: pl.BlockSpec(shape, lambda i: (0,) * len(shape))
    return pl.pallas_call(
        _proj_a_kernel,
        name="proj_a",
        grid=(t // tb,),
        in_specs=[
            pl.BlockSpec((tb, D), lambda i: (i, 0)),
            pl.BlockSpec((None, 6, D), lambda i: (geo.cond_row(i, tb), 0, 0)),
            full((D, QK_A)), full((QK_A, D)), full((D, 2 * V_A)), full((D, LANES)), full((n_gate, D)),
            full((1, LANES)), full((n_gate, 1)),
        ],
        out_specs=[
            pl.BlockSpec((tb, QK_A), lambda i: (i, 0)),
            pl.BlockSpec((QK_A, tb), lambda i: (0, i)),
            pl.BlockSpec((tb, V_A), lambda i: (i, 0)),
            pl.BlockSpec((tb, V_A), lambda i: (i, 0)),
            pl.BlockSpec((tb, LANES), lambda i: (i, 0)),
            pl.BlockSpec((n_gate, tb), lambda i: (0, i)),
        ],
        out_shape=[
            jax.ShapeDtypeStruct((t, QK_A), BF16),
            jax.ShapeDtypeStruct((QK_A, t), BF16),
            jax.ShapeDtypeStruct((t, V_A), BF16),
            jax.ShapeDtypeStruct((t, V_A), BF16),
            jax.ShapeDtypeStruct((t, LANES), F32),
            jax.ShapeDtypeStruct((n_gate, t), F32),
        ],
        compiler_params=_params(("parallel",)),
    )(x, mod_l, wq, wkt, wvo, wg_pad, wg.T, bg_pad, bg.reshape(n_gate, 1))


def _mlstm_scan_kernel(*refs, chunk, has_state):
    if has_state:
        (q_ref, kt_ref, v_ref, gc_ref, gr_ref, c0_ref, n0_ref, m0_ref,
         h_ref, c_out, n_out, m_out, c_sc, n_sc, m_sc) = refs
    else:
        (q_ref, kt_ref, v_ref, gc_ref, gr_ref,
         h_ref, c_out, n_out, m_out, c_sc, n_sc, m_sc) = refs
    L = chunk
    d = pl.program_id(1)
    c = pl.program_id(2)
    fwd = d == 0

    @pl.when(c == 0)
    def _():
        if has_state:
            c_sc[...] = c0_ref[0, 0]
            n_sc[...] = n0_ref[0, 0]
            m_sc[...] = m0_ref[0, 0]
        else:
            c_sc[...] = jnp.zeros_like(c_sc)
            n_sc[...] = jnp.zeros_like(n_sc)
            m_sc[...] = jnp.zeros_like(m_sc)

    row = lax.broadcasted_iota(jnp.int32, (L, L), 0)
    col = lax.broadcasted_iota(jnp.int32, (L, L), 1)
    sgn = 1 - 2 * d
    causal = (row - col) * sgn >= 0
    tri = causal.astype(BF16)
    tri_t = ((col - row) * sgn >= 0).astype(BF16)

    gc = gc_ref[...]
    gr = gr_ref[...]
    bc_all = _dot3(tri, _log_sigmoid(gc))
    br_all = _dot3_r(_log_sigmoid(gr), tri_t)
    ones_blk = (lax.broadcasted_iota(jnp.int32, (L, LANES), 1) == 0).astype(BF16)

    for h in range(NH_A):
        b_c = jnp.where(fwd, bc_all[:, 4 + h:5 + h], bc_all[:, 12 + h:13 + h])
        b_r = jnp.where(fwd, br_all[4 + h:5 + h, :], br_all[12 + h:13 + h, :])
        i_r = jnp.where(fwd, gr[h:h + 1, :], gr[8 + h:9 + h, :])
        bl = jnp.where(fwd, b_r[:, L - 1:L], b_r[:, 0:1])
        q = q_ref[:, h * DK_A:(h + 1) * DK_A]
        kt = kt_ref[h * DK_A:(h + 1) * DK_A, :]
        v = v_ref[:, h * DV_A:(h + 1) * DV_A]
        m = m_sc[h:h + 1, 0:1]
        cst = c_sc[h]
        nst = n_sc[h]

        a_r = i_r - b_r
        logd = jnp.where(causal, b_c + a_r, -jnp.inf)
        inter = b_c + m
        m_t = jnp.maximum(inter, jnp.max(logd, axis=1, keepdims=True))
        dmat = jnp.exp(logd - m_t)
        e_int = jnp.exp(inter - m_t)
        s = (jnp.dot(q, kt, preferred_element_type=F32) * dmat).astype(BF16)
        num = (jnp.dot(s, v, preferred_element_type=F32)
               + e_int * jnp.dot(q, cst.astype(BF16), preferred_element_type=F32))
        den = (jnp.dot(s, ones_blk, preferred_element_type=F32)
               + e_int * jnp.dot(q, nst.astype(BF16), preferred_element_type=F32))[:, 0:1]
        h_ref[:, h * DV_A:(h + 1) * DV_A] = num / jnp.maximum(jnp.abs(den), jnp.exp(-m_t))

        logw = bl + a_r
        m_new = jnp.maximum(bl + m, jnp.max(logw, axis=1, keepdims=True))
        w = jnp.exp(logw - m_new)
        decay = jnp.exp(bl + m - m_new)
        kw = (kt.astype(F32) * w).astype(BF16)
        c_sc[h] = decay * cst + jnp.dot(kw, v, preferred_element_type=F32)
        n_sc[h] = decay * nst + jnp.dot(kw, ones_blk, preferred_element_type=F32)
        m_sc[h:h + 1, :] = jnp.broadcast_to(m_new, (1, LANES))

    @pl.when(c == pl.num_programs(2) - 1)
    def _():
        c_out[0, 0] = c_sc[...]
        n_out[0, 0] = n_sc[...]
        m_out[0, 0] = m_sc[...]


def _mlstm_scan(q, kt, v, gc, gr, *, row0, n_seq, seq_len, state=None):
    L = CHUNK_A
    nc = seq_len // L
    blk0 = row0 // L

    def loc_blk(b, d, c):
        return b * nc + c + d * (nc - 1 - 2 * c)

    def tok_blk(b, d, c):
        return blk0 + loc_blk(b, d, c)

    in_specs = [
        pl.BlockSpec((L, QK_A), lambda b, d, c: (tok_blk(b, d, c), 0)),
        pl.BlockSpec((QK_A, L), lambda b, d, c: (0, tok_blk(b, d, c))),
        pl.BlockSpec((L, V_A), lambda b, d, c: (tok_blk(b, d, c), 0)),
        pl.BlockSpec((L, LANES), lambda b, d, c: (tok_blk(b, d, c), 0)),
        pl.BlockSpec((4 * NH_A, L), lambda b, d, c: (0, tok_blk(b, d, c))),
    ]
    args = [q, kt, v, gc, gr]
    if state is not None:
        in_specs += [
            pl.BlockSpec((1, 1, NH_A, DK_A, DV_A), lambda b, d, c: (b, d, 0, 0, 0)),
            pl.BlockSpec((1, 1, NH_A, DK_A, LANES), lambda b, d, c: (b, d, 0, 0, 0)),
            pl.BlockSpec((1, 1, SUBLANES, LANES), lambda b, d, c: (b, d, 0, 0)),
        ]
        args += list(state)
    return pl.pallas_call(
        functools.partial(_mlstm_scan_kernel, chunk=L, has_state=state is not None),
        name="mlstm_scan_seeded" if state is not None else "mlstm_scan",
        grid=(n_seq, 2, nc),
        in_specs=in_specs,
        out_specs=[
            pl.BlockSpec((None, L, V_A), lambda b, d, c: (d, loc_blk(b, d, c), 0)),
            pl.BlockSpec((1, 1, NH_A, DK_A, DV_A), lambda b, d, c: (b, d, 0, 0, 0)),
            pl.BlockSpec((1, 1, NH_A, DK_A, LANES), lambda b, d, c: (b, d, 0, 0, 0)),
            pl.BlockSpec((1, 1, SUBLANES, LANES), lambda b, d, c: (b, d, 0, 0)),
        ],
        out_shape=[
            jax.ShapeDtypeStruct((2, n_seq * seq_len, V_A), F32),
            jax.ShapeDtypeStruct((n_seq, 2, NH_A, DK_A, DV_A), F32),
            jax.ShapeDtypeStruct((n_seq, 2, NH_A, DK_A, LANES), F32),
            jax.ShapeDtypeStruct((n_seq, 2, SUBLANES, LANES), F32),
        ],
        scratch_shapes=[
            pltpu.VMEM((NH_A, DK_A, DV_A), F32),
            pltpu.VMEM((NH_A, DK_A, LANES), F32),
            pltpu.VMEM((SUBLANES, LANES), F32),
        ],
        compiler_params=_params(("parallel", "parallel", "arbitrary")),
    )(*args)


def _out_a_kernel(hp_ref, hs_ref, so_ref, nw_ref, w_ref, x_ref, mod_ref, lg_ref, lb_ref, o_ref, *, n_prompt_tiles):
    is_prompt = pl.program_id(0) < n_prompt_tiles
    y = jnp.where(is_prompt, hp_ref[0] + hp_ref[1], hs_ref[0] + hs_ref[1])
    parts = []
    for h in range(NH_A):
        yh = y[:, h * DV_A:(h + 1) * DV_A]
        mu = jnp.mean(yh, axis=-1, keepdims=True)
        yc = yh - mu
        var = jnp.mean(yc * yc, axis=-1, keepdims=True)
        parts.append(yc * lax.rsqrt(var + EPS))
    yn = jnp.concatenate(parts, axis=-1) * nw_ref[...] * so_ref[...].astype(F32)
    out = jnp.dot(yn.astype(BF16), w_ref[...], preferred_element_type=F32)
    o_ref[...] = _layer_norm_rows(ALPHA * x_ref[...] + mod_ref[2:3, :] * out, lg_ref[...], lb_ref[...])


def _out_a(geo, h_prompt, h_sample, so, norm_w, w_out, x, mod_l, ln_g, ln_b):
    t = geo.t
    tb = TOK_TILE
    n_p = geo.t_prompt // tb
    full = lambda shape: pl.BlockSpec(shape, lambda i: (0,) * len(shape))
    return pl.pallas_call(
        functools.partial(_out_a_kernel, n_prompt_tiles=n_p),
        name="out_a",
        grid=(t // tb,),
        in_specs=[
            pl.BlockSpec((2, tb, V_A), lambda i: (0, jnp.minimum(i, n_p - 1), 0)),
            pl.BlockSpec((2, tb, V_A), lambda i: (0, jnp.maximum(i - n_p, 0), 0)),
            pl.BlockSpec((tb, V_A), lambda i: (i, 0)),
            full((1, V_A)), full((V_A, D)),
            pl.BlockSpec((tb, D), lambda i: (i, 0)),
            pl.BlockSpec((None, 6, D), lambda i: (geo.cond_row(i, tb), 0, 0)),
            full((1, D)), full((1, D)),
        ],
        out_specs=pl.BlockSpec((tb, D), lambda i: (i, 0)),
        out_shape=jax.ShapeDtypeStruct((t, D), F32),
        compiler_params=_params(("parallel",)),
    )(h_prompt, h_sample, so, norm_w.reshape(1, V_A).astype(F32), w_out.astype(BF16), x, mod_l,
      ln_g.reshape(1, D), ln_b.reshape(1, D))


def _proj_b_kernel(x_ref, mod_ref, w_ref, q_ref, pre_ref, v_ref, sg_ref):
    h = x_ref[...] * (1.0 + mod_ref[1:2, :]) + mod_ref[0:1, :]
    z = jnp.dot(h.astype(BF16), w_ref[...], preferred_element_type=F32)
    for hd in range(NH_B):
        lo = hd * DK_B
        qh = z[:, lo:lo + DK_B]
        q_ref[hd] = qh * jax.nn.sigmoid(qh)
        pre_ref[0, hd] = z[:, D + lo:D + lo + DK_B]
        pre_ref[1, hd] = z[:, 2 * D + lo:2 * D + lo + DK_B]
        v_ref[hd] = z[:, 3 * D + lo:3 * D + lo + DK_B].astype(BF16)
    g = z[:, 4 * D:]
    sg_ref[...] = (g * jax.nn.sigmoid(g)).astype(BF16)


def _proj_b(geo, x, mod_l, w_in):
    t = geo.t
    tb = TOK_TILE
    return pl.pallas_call(
        _proj_b_kernel,
        name="proj_b",
        grid=(t // tb,),
        in_specs=[
            pl.BlockSpec((tb, D), lambda i: (i, 0)),
            pl.BlockSpec((None, 6, D), lambda i: (geo.cond_row(i, tb), 0, 0)),
            pl.BlockSpec((D, 5 * D), lambda i: (0, 0)),
        ],
        out_specs=[
            pl.BlockSpec((NH_B, tb, DK_B), lambda i: (0, i, 0)),
            pl.BlockSpec((2, NH_B, tb, DK_B), lambda i: (0, 0, i, 0)),
            pl.BlockSpec((NH_B, tb, DK_B), lambda i: (0, i, 0)),
            pl.BlockSpec((tb, D), lambda i: (i, 0)),
        ],
        out_shape=[
            jax.ShapeDtypeStruct((NH_B, t, DK_B), F32),
            jax.ShapeDtypeStruct((2, NH_B, t, DK_B), F32),
            jax.ShapeDtypeStruct((NH_B, t, DK_B), BF16),
            jax.ShapeDtypeStruct((t, D), BF16),
        ],
        compiler_params=_params(("parallel",)),
    )(x, mod_l, w_in.astype(BF16))


CHUNK_B = 128
BAND = SUBLANES // 2
TN_DIMS = (((0,), (0,)), ((), ()))


def _hgrn_head(q, pre, lbv, v_bf, st, fwd):
    L = q.shape[0]
    f = lbv + (1.0 - lbv) * jax.nn.sigmoid(pre)
    lf = jnp.log(f)
    kk = (1.0 - lbv) * jax.nn.sigmoid(-pre)
    row = lax.broadcasted_iota(jnp.int32, (L, L), 0)
    col = lax.broadcasted_iota(jnp.int32, (L, L), 1)
    tri = ((row >= col) if fwd else (row <= col)).astype(BF16)
    b = _dot3(tri, lf)
    v32 = v_bf.astype(F32)
    tpos = lax.broadcasted_iota(jnp.int32, (L, DK_B), 0)

    o = jnp.sum(q * kk, axis=1, keepdims=True) * v32
    for dl in range(1, BAND):
        shift = dl if fwd else L - dl
        in_blk = (tpos % BAND >= dl) if fwd else (tpos % BAND + dl < BAND)
        e = jnp.exp(jnp.where(in_blk, b - pltpu.roll(b, shift, 0), -jnp.inf))
        a = jnp.sum(q * pltpu.roll(kk, shift, 0) * e, axis=1, keepdims=True)
        o = o + a * pltpu.roll(v32, shift, 0)

    att = jnp.zeros((L, L), F32)
    w = BAND
    while w < L:
        nb = L // (2 * w)
        b3 = b.reshape(nb, 2 * w, DK_B)
        edge = (b3[:, w - 1:w, :] if fwd else b3[:, w:w + 1, :])
        bmid = jnp.broadcast_to(edge, (nb, 2 * w, DK_B)).reshape(L, DK_B)
        second = (tpos // w) % 2 == 1
        t_side = second if fwd else jnp.logical_not(second)
        e = jnp.exp(jnp.where(t_side, b - bmid, bmid - b))
        qt = jnp.where(t_side, q * e, 0.0).astype(BF16)
        ks = jnp.where(t_side, 0.0, kk * e).astype(BF16)
        a = lax.dot_general(qt, ks, NT_DIMS, preferred_element_type=F32)
        att = att + jnp.where(row // (2 * w) == col // (2 * w), a, 0.0)
        w *= 2
    o = o + jnp.dot(att.astype(BF16), v_bf, preferred_element_type=F32)

    bl = b[L - 1:L, :] if fwd else b[0:1, :]
    o = o + lax.dot_general((q * jnp.exp(b)).astype(BF16), st.astype(BF16), NT_DIMS, preferred_element_type=F32)
    kd = (kk * jnp.exp(bl - b)).astype(BF16)
    st_new = jnp.exp(bl) * st + lax.dot_general(v_bf, kd, TN_DIMS, preferred_element_type=F32)
    return o, st_new


def _hgrn_scan_kernel(*refs, has_state):
    if has_state:
        q_ref, pre_ref, v_ref, lb_ref, s0_ref, o_ref, s_out, st_sc = refs
    else:
        q_ref, pre_ref, v_ref, lb_ref, o_ref, s_out, st_sc = refs
    d = pl.program_id(1)
    c = pl.program_id(2)

    @pl.when(c == 0)
    def _():
        if has_state:
            for hd in range(NH_B):
                st_sc[hd] = s0_ref[0, 0, hd].T
        else:
            st_sc[...] = jnp.zeros_like(st_sc)

    def run(fwd):
        def head(hd, carry):
            o, st_new = _hgrn_head(q_ref[hd], pre_ref[hd], lb_ref[hd], v_ref[hd], st_sc[hd], fwd)
            o_ref[hd] = o
            st_sc[hd] = st_new
            return carry
        lax.fori_loop(0, NH_B, head, 0, unroll=8)

    @pl.when(d == 0)
    def _():
        run(True)

    @pl.when(d == 1)
    def _():
        run(False)

    @pl.when(c == pl.num_programs(2) - 1)
    def _():
        for hd in range(NH_B):
            s_out[0, 0, hd] = st_sc[hd].T


def _hgrn_scan(q, pre, v, lbd, *, row0, n_seq, seq_len, state=None):
    L = CHUNK_B
    nc = seq_len // L
    blk0 = row0 // L

    def loc_blk(b, d, c):
        return b * nc + c + d * (nc - 1 - 2 * c)

    def tok_blk(b, d, c):
        return blk0 + loc_blk(b, d, c)

    in_specs = [
        pl.BlockSpec((NH_B, L, DK_B), lambda b, d, c: (0, tok_blk(b, d, c), 0)),
        pl.BlockSpec((None, NH_B, L, DK_B), lambda b, d, c: (d, 0, tok_blk(b, d, c), 0)),
        pl.BlockSpec((NH_B, L, DK_B), lambda b, d, c: (0, tok_blk(b, d, c), 0)),
        pl.BlockSpec((None, NH_B, 1, DK_B), lambda b, d, c: (d, 0, 0, 0)),
    ]
    args = [q, pre, v, lbd]
    if state is not None:
        in_specs.append(pl.BlockSpec((1, 1, NH_B, DK_B, DK_B), lambda b, d, c: (b, d, 0, 0, 0)))
        args.append(state)
    return pl.pallas_call(
        functools.partial(_hgrn_scan_kernel, has_state=state is not None),
        name="hgrn_scan_seeded" if state is not None else "hgrn_scan",
        grid=(n_seq, 2, nc),
        in_specs=in_specs,
        out_specs=[
            pl.BlockSpec((None, NH_B, L, DK_B), lambda b, d, c: (d, 0, loc_blk(b, d, c), 0)),
            pl.BlockSpec((1, 1, NH_B, DK_B, DK_B), lambda b, d, c: (b, d, 0, 0, 0)),
        ],
        out_shape=[
            jax.ShapeDtypeStruct((2, NH_B, n_seq * seq_len, DK_B), F32),
            jax.ShapeDtypeStruct((n_seq, 2, NH_B, DK_B, DK_B), F32),
        ],
        scratch_shapes=[pltpu.VMEM((NH_B, DK_B, DK_B), F32)],
        compiler_params=_params(("parallel", "parallel", "arbitrary")),
    )(*args)


def _out_b_kernel(op_ref, os_ref, sg_ref, nw_ref, w_ref, x_ref, mod_ref, lg_ref, lb_ref, out_ref, *, n_prompt_tiles):
    is_prompt = pl.program_id(0) < n_prompt_tiles
    parts = []
    for hd in range(NH_B):
        y = jnp.where(is_prompt, op_ref[0, hd] + op_ref[1, hd], os_ref[0, hd] + os_ref[1, hd])
        parts.append(y * lax.rsqrt(jnp.mean(y * y, axis=-1, keepdims=True) + EPS))
    yn = jnp.concatenate(parts, axis=-1) * nw_ref[...] * sg_ref[...].astype(F32)
    out = jnp.dot(yn.astype(BF16), w_ref[...], preferred_element_type=F32)
    out_ref[...] = _layer_norm_rows(ALPHA * x_ref[...] + mod_ref[2:3, :] * out, lg_ref[...], lb_ref[...])


def _out_b(geo, o_prompt, o_sample, sg, norm_w, w_out, x, mod_l, ln_g, ln_b):
    t = geo.t
    tb = TOK_TILE
    n_p = geo.t_prompt // tb
    full = lambda shape: pl.BlockSpec(shape, lambda i: (0,) * len(shape))
    return pl.pallas_call(
        functools.partial(_out_b_kernel, n_prompt_tiles=n_p),
        name="out_b",
        grid=(t // tb,),
        in_specs=[
            pl.BlockSpec((2, NH_B, tb, DK_B), lambda i: (0, 0, jnp.minimum(i, n_p - 1), 0)),
            pl.BlockSpec((2, NH_B, tb, DK_B), lambda i: (0, 0, jnp.maximum(i - n_p, 0), 0)),
            pl.BlockSpec((tb, D), lambda i: (i, 0)),
            full((1, D)), full((D, D)),
            pl.BlockSpec((tb, D), lambda i: (i, 0)),
            pl.BlockSpec((None, 6, D), lambda i: (geo.cond_row(i, tb), 0, 0)),
            full((1, D)), full((1, D)),
        ],
        out_specs=pl.BlockSpec((tb, D), lambda i: (i, 0)),
        out_shape=jax.ShapeDtypeStruct((t, D), F32),
        compiler_params=_params(("parallel",)),
    )(o_prompt, o_sample, sg, norm_w.reshape(1, D).astype(F32), w_out.astype(BF16), x, mod_l,
      ln_g.reshape(1, D), ln_b.reshape(1, D))


MOE_BLK = 256
ROW_TILES = D // 2 // LANES
U32 = jnp.uint32


def _first_index(hit, iota, size, axis):
    return jnp.min(jnp.where(hit, iota, size), axis=axis, keepdims=True)


def _router_kernel(x_ref, mod_ref, wrt_ref, eb_ref, e_ref, w_ref, r_ref, cnt_ref, cnt_sc):
    i = pl.program_id(0)
    tb = x_ref.shape[0]

    @pl.when(i == 0)
    def _():
        cnt_sc[...] = jnp.zeros_like(cnt_sc)

    h = x_ref[...] * (1.0 + mod_ref[4:5, :]) + mod_ref[3:4, :]
    logits = lax.dot_general(wrt_ref[...], h, NT_DIMS, precision=HIGHEST, preferred_element_type=F32)
    scores = jax.nn.sigmoid(logits)
    sel = scores + eb_ref[...]

    g3 = sel.reshape(N_GROUPS, GROUP_SIZE, tb)
    io3 = lax.broadcasted_iota(jnp.int32, g3.shape, 1)
    m1 = jnp.max(g3, axis=1, keepdims=True)
    first = _first_index(g3 == m1, io3, GROUP_SIZE, 1)
    m2 = jnp.max(jnp.where(io3 == first, -jnp.inf, g3), axis=1, keepdims=True)
    gscore = (m1 + m2).reshape(N_GROUPS, tb)

    iog = lax.broadcasted_iota(jnp.int32, gscore.shape, 0)
    gmask = jnp.zeros(gscore.shape, F32)
    for _ in range(TOPK_GROUPS):
        gm = jnp.max(gscore, axis=0, keepdims=True)
        pick = iog == _first_index(gscore == gm, iog, N_GROUPS, 0)
        gmask = jnp.where(pick, 1.0, gmask)
        gscore = jnp.where(pick, -jnp.inf, gscore)
    emask = jnp.broadcast_to(gmask.reshape(N_GROUPS, 1, tb), (N_GROUPS, GROUP_SIZE, tb)).reshape(N_EXPERTS, tb)
    cand = jnp.where(emask > 0.0, sel, -jnp.inf)

    ioe = lax.broadcasted_iota(jnp.int32, cand.shape, 0)
    picks, wts = [], []
    onehot = jnp.zeros(cand.shape, F32)
    for _ in range(TOP_K):
        cm = jnp.max(cand, axis=0, keepdims=True)
        idx = _first_index(cand == cm, ioe, N_EXPERTS, 0)
        pick = ioe == idx
        picks.append(pick)
        wts.append(jnp.sum(jnp.where(pick, scores, 0.0), axis=0, keepdims=True))
        onehot = onehot + pick.astype(F32)
        cand = jnp.where(pick, -jnp.inf, cand)
        e_ref[pl.ds(len(picks) - 1, 1), :] = idx
    wsum = wts[0]
    for wk in wts[1:]:
        wsum = wsum + wk
    for k in range(TOP_K):
        w_ref[pl.ds(k, 1), :] = wts[k] / wsum * ROUTED_SCALE

    r_io = lax.broadcasted_iota(jnp.int32, (tb, tb), 0)
    c_io = lax.broadcasted_iota(jnp.int32, (tb, tb), 1)
    before = (r_io < c_io).astype(BF16)
    rank = cnt_sc[:, 0:1] + jnp.dot(onehot.astype(BF16), before, preferred_element_type=F32)
    for k in range(TOP_K):
        r_ref[pl.ds(k, 1), :] = jnp.sum(jnp.where(picks[k], rank, 0.0), axis=0, keepdims=True).astype(jnp.int32)
    cnt_sc[...] = cnt_sc[...] + jnp.sum(onehot, axis=1, keepdims=True)
    cnt_ref[...] = cnt_sc[...]


def _router(geo, x, mod_l, w_router, e_bias):
    t = geo.t
    tb = TOK_TILE
    full = lambda shape: pl.BlockSpec(shape, lambda i: (0,) * len(shape))
    e, w, r, cnt = pl.pallas_call(
        _router_kernel,
        name="router",
        grid=(t // tb,),
        in_specs=[
            pl.BlockSpec((tb, D), lambda i: (i, 0)),
            pl.BlockSpec((None, 6, D), lambda i: (geo.cond_row(i, tb), 0, 0)),
            full((N_EXPERTS, D)), full((N_EXPERTS, 1)),
        ],
        out_specs=[
            pl.BlockSpec((TOP_K, tb), lambda i: (0, i)),
            pl.BlockSpec((TOP_K, tb), lambda i: (0, i)),
            pl.BlockSpec((TOP_K, tb), lambda i: (0, i)),
            full((N_EXPERTS, LANES)),
        ],
        out_shape=[
            jax.ShapeDtypeStruct((TOP_K, t), jnp.int32),
            jax.ShapeDtypeStruct((TOP_K, t), F32),
            jax.ShapeDtypeStruct((TOP_K, t), jnp.int32),
            jax.ShapeDtypeStruct((N_EXPERTS, LANES), F32),
        ],
        scratch_shapes=[pltpu.VMEM((N_EXPERTS, LANES), F32)],
        compiler_params=_params(("arbitrary",)),
    )(x, mod_l, w_router.T.astype(F32), e_bias.reshape(N_EXPERTS, 1).astype(F32))
    return e, w, r, cnt[:, 0].astype(jnp.int32)


def _slot_kernel(pstart_ref, e_ref, r_ref, o_ref):
    e = e_ref[...]
    slot = r_ref[...]
    for x in range(N_EXPERTS):
        slot = slot + jnp.where(e == x, pstart_ref[x], 0)
    o_ref[...] = slot


def _slots(geo, pstart, top_e, rank):
    tb = TOK_TILE
    nt = geo.t // tb
    return pl.pallas_call(
        _slot_kernel,
        name="slots",
        grid_spec=pltpu.PrefetchScalarGridSpec(
            num_scalar_prefetch=1,
            grid=(nt,),
            in_specs=[pl.BlockSpec((TOP_K, tb), lambda i, p: (0, i)),
                      pl.BlockSpec((TOP_K, tb), lambda i, p: (0, i))],
            out_specs=pl.BlockSpec((None, TOP_K, tb), lambda i, p: (i, 0, 0)),
        ),
        out_shape=jax.ShapeDtypeStruct((nt, TOP_K, tb), jnp.int32),
        compiler_params=_params(("parallel",)),
    )(pstart, top_e, rank)


def _pack_rows(x):
    hi = pltpu.bitcast(x[:, :D // 2].astype(BF16).astype(F32), U32)
    lo = pltpu.bitcast(x[:, D // 2:].astype(BF16).astype(F32), U32)
    return hi | (lo >> 16)


def _unpack_rows(w):
    hi = pltpu.bitcast(w & jnp.uint32(0xFFFF0000), F32)
    lo = pltpu.bitcast(w << 16, F32)
    return jnp.concatenate([hi, lo], axis=1)


def _store_packed(ref, x):
    n = x.shape[0]
    words = _pack_rows(x)
    for j in range(ROW_TILES):
        ref[pl.ds(j, n, stride=ROW_TILES), :] = words[:, j * LANES:(j + 1) * LANES]


def _load_packed(ref, row0, n):
    return _unpack_rows(jnp.concatenate(
        [ref[pl.ds(row0 * ROW_TILES + j, n, stride=ROW_TILES), :] for j in range(ROW_TILES)], axis=1))


def _packed_row(ref, row):
    return ref.at[pl.ds(pl.multiple_of(row * ROW_TILES, ROW_TILES), ROW_TILES)]


def _dispatch_kernel(x_ref, mod_ref, slot_hbm, xs_in, xs_hbm, hs, slot_sm, sem_i, sem_d):
    del xs_in
    i = pl.program_id(0)
    tb = x_ref.shape[0]
    idx_copy = pltpu.make_async_copy(slot_hbm.at[i], slot_sm, sem_i)
    idx_copy.start()
    _store_packed(hs, x_ref[...] * (1.0 + mod_ref[4:5, :]) + mod_ref[3:4, :])
    idx_copy.wait()

    def issue(tok, carry):
        for k in range(TOP_K):
            pltpu.make_async_copy(_packed_row(hs, tok), xs_hbm.at[slot_sm[k, tok]], sem_d).start()
        return carry

    lax.fori_loop(0, tb, issue, 0)
    for k in range(TOP_K):
        pltpu.make_async_copy(hs, hs, sem_d).wait()


def _dispatch(geo, x, mod_l, slots, n_rows):
    tb = TOK_TILE
    xs0 = jnp.zeros((n_rows, ROW_TILES, LANES), U32)
    return pl.pallas_call(
        _dispatch_kernel,
        name="dispatch",
        grid=(geo.t // tb,),
        in_specs=[
            pl.BlockSpec((tb, D), lambda i: (i, 0)),
            pl.BlockSpec((None, 6, D), lambda i: (geo.cond_row(i, tb), 0, 0)),
            pl.BlockSpec(memory_space=pl.ANY),
            pl.BlockSpec(memory_space=pl.ANY),
        ],
        out_specs=pl.BlockSpec(memory_space=pl.ANY),
        out_shape=jax.ShapeDtypeStruct((n_rows, ROW_TILES, LANES), U32),
        scratch_shapes=[
            pltpu.VMEM((tb * ROW_TILES, LANES), U32),
            pltpu.SMEM((TOP_K, tb), jnp.int32),
            pltpu.SemaphoreType.DMA(()),
            pltpu.SemaphoreType.DMA(()),
        ],
        input_output_aliases={3: 0},
        compiler_params=_params(("arbitrary",)),
    )(x, mod_l, slots, xs0)


def _ffn_kernel(blk_e_ref, n_used_ref, xs_ref, wg_ref, wu_ref, wd_ref, y_ref, wg_sc, wu_sc, wd_sc):
    b = pl.program_id(0)
    used = b < n_used_ref[0]
    new_expert = (b == 0) | (blk_e_ref[b] != blk_e_ref[jnp.maximum(b - 1, 0)])

    @pl.when(used & new_expert)
    def _():
        wg_sc[...] = wg_ref[...].astype(BF16)
        wu_sc[...] = wu_ref[...].astype(BF16)
        wd_sc[...] = wd_ref[...].astype(BF16)

    @pl.when(used)
    def _():
        x = _load_packed(xs_ref, 0, MOE_BLK).astype(BF16)
        g = jnp.dot(x, wg_sc[...], preferred_element_type=F32)
        u = jnp.dot(x, wu_sc[...], preferred_element_type=F32)
        hmid = (g * jax.nn.sigmoid(g) * u).astype(BF16)
        _store_packed(y_ref, jnp.dot(hmid, wd_sc[...], preferred_element_type=F32))

    @pl.when(jnp.logical_not(used))
    def _():
        y_ref[...] = jnp.zeros_like(y_ref)


def _ffn(xs, blk_e, n_used, layer, wg, wu, wd, n_blocks):
    def blk(b, be, nu):
        return jnp.maximum(jnp.minimum(b, nu[0] - 1), 0)

    def w_idx(b, be, nu):
        return (layer, be[blk(b, be, nu)], 0, 0)

    return pl.pallas_call(
        _ffn_kernel,
        name="expert_ffn",
        grid_spec=pltpu.PrefetchScalarGridSpec(
            num_scalar_prefetch=2,
            grid=(n_blocks,),
            in_specs=[
                pl.BlockSpec((MOE_BLK * ROW_TILES, LANES), lambda b, be, nu: (blk(b, be, nu), 0)),
                pl.BlockSpec((None, None, D, D_EXPERT), w_idx),
                pl.BlockSpec((None, None, D, D_EXPERT), w_idx),
                pl.BlockSpec((None, None, D_EXPERT, D), w_idx),
            ],
            out_specs=pl.BlockSpec((MOE_BLK * ROW_TILES, LANES), lambda b, be, nu: (b, 0)),
            scratch_shapes=[pltpu.VMEM((D, D_EXPERT), BF16), pltpu.VMEM((D, D_EXPERT), BF16),
                            pltpu.VMEM((D_EXPERT, D), BF16)],
        ),
        out_shape=jax.ShapeDtypeStruct((xs.shape[0] * ROW_TILES, LANES), U32),
        compiler_params=_params(("arbitrary",)),
    )(blk_e, n_used, xs.reshape(-1, LANES), wg, wu, wd).reshape(xs.shape)


def _combine_kernel(x_ref, mod_ref, wt_ref, slot_hbm, yb_hbm, sg_ref, su_ref, sd_ref, lg_ref, lb_ref,
                    o_ref, rows, slot_sm, sem_i, sem_d):
    i = pl.program_id(0)
    tb = x_ref.shape[0]
    idx_copy = pltpu.make_async_copy(slot_hbm.at[i], slot_sm, sem_i)
    idx_copy.start()
    idx_copy.wait()

    def issue(tok, carry):
        for k in range(TOP_K):
            pltpu.make_async_copy(yb_hbm.at[slot_sm[k, tok]], _packed_row(rows, k * tb + tok), sem_d).start()
        return carry

    lax.fori_loop(0, tb, issue, 0)

    x = x_ref[...]
    hb = (x * (1.0 + mod_ref[4:5, :]) + mod_ref[3:4, :]).astype(BF16)
    g = jnp.dot(hb, sg_ref[...], preferred_element_type=F32)
    u = jnp.dot(hb, su_ref[...], preferred_element_type=F32)
    shared = jnp.dot((g * jax.nn.sigmoid(g) * u).astype(BF16), sd_ref[...], preferred_element_type=F32)

    pltpu.make_async_copy(rows, rows, sem_d).wait()
    routed = jnp.zeros((tb, D), F32)
    for k in range(TOP_K):
        routed = routed + _load_packed(rows, k * tb, tb) * wt_ref[:, k:k + 1]
    ff = routed + shared
    o_ref[...] = _layer_norm_rows(ALPHA * x + mod_ref[5:6, :] * ff, lg_ref[...], lb_ref[...])


def _combine(geo, x, mod_l, wt, slots, yb, sg, su, sd, ln_g, ln_b):
    tb = TOK_TILE
    full = lambda shape: pl.BlockSpec(shape, lambda i: (0,) * len(shape))
    return pl.pallas_call(
        _combine_kernel,
        name="combine",
        grid=(geo.t // tb,),
        in_specs=[
            pl.BlockSpec((tb, D), lambda i: (i, 0)),
            pl.BlockSpec((None, 6, D), lambda i: (geo.cond_row(i, tb), 0, 0)),
            pl.BlockSpec((tb, TOP_K), lambda i: (i, 0)),
            pl.BlockSpec(memory_space=pl.ANY),
            pl.BlockSpec(memory_space=pl.ANY),
            full((D, D_EXPERT)), full((D, D_EXPERT)), full((D_EXPERT, D)), full((1, D)), full((1, D)),
        ],
        out_specs=pl.BlockSpec((tb, D), lambda i: (i, 0)),
        out_shape=jax.ShapeDtypeStruct((geo.t, D), F32),
        scratch_shapes=[
            pltpu.VMEM((TOP_K * tb * ROW_TILES, LANES), U32),
            pltpu.SMEM((TOP_K, tb), jnp.int32),
            pltpu.SemaphoreType.DMA(()),
            pltpu.SemaphoreType.DMA(()),
        ],
        compiler_params=_params(("arbitrary",)),
    )(x, mod_l, wt, slots, yb, sg.astype(BF16), su.astype(BF16), sd.astype(BF16),
      ln_g.reshape(1, D), ln_b.reshape(1, D))


def _moe_layer(geo, x, mod_l, w_router, e_bias, layer, wg, wu, wd, sg, su, sd, ln_g, ln_b):
    top_e, w, rank, counts = _router(geo, x, mod_l, w_router, e_bias)
    n_blocks = (geo.t * TOP_K) // MOE_BLK + N_EXPERTS
    padded = (counts + MOE_BLK - 1) // MOE_BLK * MOE_BLK
    pend = jnp.cumsum(padded)
    pstart = (pend - padded).astype(jnp.int32)
    blk_row0 = jnp.arange(n_blocks, dtype=jnp.int32) * MOE_BLK
    blk_e = jnp.minimum(jnp.sum((pend[None, :] <= blk_row0[:, None]).astype(jnp.int32), axis=1), N_EXPERTS - 1)
    n_used = (pend[-1:] // MOE_BLK).astype(jnp.int32)
    slots = _slots(geo, pstart, top_e, rank)
    xs = _dispatch(geo, x, mod_l, slots, n_blocks * MOE_BLK)
    yb = _ffn(xs, blk_e, n_used, layer, wg, wu, wd, n_blocks)
    return _combine(geo, x, mod_l, w.T, slots, yb, sg, su, sd, ln_g, ln_b)


def _pos_embed(rows):
    quarter = D // 4
    omega = 1.0 / (POS_BASE ** (jnp.arange(quarter, dtype=F32) / quarter))
    r, col = jnp.meshgrid(jnp.arange(rows, dtype=F32), jnp.arange(GRID_W, dtype=F32), indexing='ij')
    r = r.reshape(-1, 1) * omega
    col = col.reshape(-1, 1) * omega
    return jnp.concatenate([jnp.sin(r), jnp.cos(r), jnp.sin(col), jnp.cos(col)], axis=-1)


def _mlstm_layer(geo, x, mod_l, j, a_w_in, a_b_gates, a_norm, a_w_out, ln_g, ln_b,
                 state_C, state_n, state_m):
    q, kt, v, so, gc, gr = _proj_a(geo, x, mod_l, a_w_in[j], a_b_gates[j])
    hp, c_p, n_p, m_p = _mlstm_scan(q, kt, v, gc, gr, row0=0, n_seq=geo.n_prompt, seq_len=geo.prompt_len)
    ns = geo.n_sample
    n0 = jnp.pad(state_n[:, j].astype(F32)[..., None], ((0, 0),) * 4 + ((0, LANES - 1),))
    m0 = jnp.pad(state_m[:, j].astype(F32), ((0, 0), (0, 0), (0, SUBLANES - NH_A)))
    m0 = jnp.broadcast_to(m0[..., None], (ns, 2, SUBLANES, LANES))
    hs, _, _, _ = _mlstm_scan(q, kt, v, gc, gr, row0=geo.t_prompt, n_seq=ns, seq_len=geo.sample_len,
                              state=(state_C[:, j].astype(F32), n0, m0))
    x1 = _out_a(geo, hp, hs, so, a_norm[j], a_w_out[j], x, mod_l, ln_g, ln_b)
    return x1, c_p, n_p[..., 0], m_p[:, :, :NH_A, 0]


def _hgrn_layer(geo, x, mod_l, j, lb_layer, b_w_in, b_norm, b_w_out, ln_g, ln_b, state_S):
    q, pre, v, sg = _proj_b(geo, x, mod_l, b_w_in[j])
    lbd = lb_layer.reshape(2, NH_B, 1, DK_B)
    op, s_p = _hgrn_scan(q, pre, v, lbd, row0=0, n_seq=geo.n_prompt, seq_len=geo.prompt_len)
    os_, _ = _hgrn_scan(q, pre, v, lbd, row0=geo.t_prompt, n_seq=geo.n_sample, seq_len=geo.sample_len,
                        state=state_S[:, j].astype(F32))
    x1 = _out_b(geo, op, os_, sg, b_norm[j], b_w_out[j], x, mod_l, ln_g, ln_b)
    return x1, s_p


def kernel(x_prompt, x_sample, state_mlstm_C, state_mlstm_n, state_mlstm_m, state_hgrn_S, c, c_ctx, w_mod, b_mod, ln_g, ln_b, a_w_in, a_b_gates, a_norm, a_w_out, b_w_in, b_lb, b_norm, b_w_out, w_router, e_bias, w_gate, w_up, w_down, ws_gate, ws_up, ws_down):
    bp, sp, _ = x_prompt.shape
    bs, ss, _ = x_sample.shape
    geo = Geometry(bp, sp, bs, ss)
    cond = jnp.zeros((COND_ROWS, D), F32).at[0].set(c_ctx).at[1:1 + bs].set(c)
    mod = _modulation(cond, w_mod, b_mod)
    x = jnp.concatenate([x_prompt.reshape(-1, D), (x_sample + _pos_embed(ss // GRID_W)[None]).reshape(-1, D)], axis=0)
    x1, new_c, new_n, new_m = _mlstm_layer(geo, x, mod[0], 0, a_w_in, a_b_gates, a_norm, a_w_out,
                                           ln_g[0, 0], ln_b[0, 0], state_mlstm_C, state_mlstm_n, state_mlstm_m)
    x2 = _moe_layer(geo, x1, mod[0], w_router[0], e_bias[0], 0, w_gate, w_up, w_down, ws_gate[0], ws_up[0], ws_down[0],
                    ln_g[0, 1], ln_b[0, 1])
    sm = jax.nn.softmax(b_lb.astype(F32), axis=0)
    lb_all = jnp.cumsum(sm, axis=0) - sm[0]
    x3, new_s = _hgrn_layer(geo, x2, mod[1], 0, lb_all[1], b_w_in, b_norm, b_w_out, ln_g[1, 0], ln_b[1, 0],
                            state_hgrn_S)
    x4 = _moe_layer(geo, x3, mod[1], w_router[1], e_bias[1], 1, w_gate, w_up, w_down, ws_gate[1], ws_up[1], ws_down[1],
                    ln_g[1, 1], ln_b[1, 1])
    y_prompt = x4[:geo.t_prompt].reshape(bp, sp, D)
    y_sample = x4[geo.t_prompt:].reshape(bs, ss, D)
    return y_prompt, y_sample, new_c[:, None], new_n[:, None], new_m[:, None], new_s[:, None]
```

```python
import functools

import jax
import jax.numpy as jnp
from jax import lax
from jax.experimental import pallas as pl
from jax.experimental.pallas import tpu as pltpu

F32 = jnp.float32
BF16 = jnp.bfloat16
HIGHEST = lax.Precision.HIGHEST

D = 1024
DEPTH = 2
GRID_W = 64
POS_BASE = 10000.0
EPS = 1e-6
ALPHA = (2.0 * DEPTH) ** 0.25
NH_A, DK_A, DV_A = 4, 128, 256
QK_A, V_A = NH_A * DK_A, NH_A * DV_A
NH_B, DK_B = 8, 128
N_EXPERTS, TOP_K, N_GROUPS, TOPK_GROUPS = 64, 8, 8, 4
GROUP_SIZE = N_EXPERTS // N_GROUPS
D_EXPERT = D // 4
ROUTED_SCALE = 2.5

LANES = 128
SUBLANES = 8
COND_ROWS = 8
TOK_TILE = 256
CHUNK_A = 256
VMEM_LIMIT = 56 * 1024 * 1024

NT_DIMS = (((1,), (1,)), ((), ()))


def _params(sem):
    return pltpu.CompilerParams(dimension_semantics=sem, vmem_limit_bytes=VMEM_LIMIT)


def _split3(x):
    hi = x.astype(BF16)
    r = x - hi.astype(F32)
    mid = r.astype(BF16)
    lo = (r - mid.astype(F32)).astype(BF16)
    return hi, mid, lo


def _dot3(a_bf, x, transpose_side=None):
    hi, mid, lo = _split3(x)
    return (jnp.dot(a_bf, hi, preferred_element_type=F32)
            + jnp.dot(a_bf, mid, preferred_element_type=F32)
            + jnp.dot(a_bf, lo, preferred_element_type=F32))


def _dot3_r(x, a_bf):
    hi, mid, lo = _split3(x)
    return (jnp.dot(hi, a_bf, preferred_element_type=F32)
            + jnp.dot(mid, a_bf, preferred_element_type=F32)
            + jnp.dot(lo, a_bf, preferred_element_type=F32))


def _log_sigmoid(x):
    return jnp.minimum(x, 0.0) - jnp.log1p(jnp.exp(-jnp.abs(x)))


def _layer_norm_rows(x, g, b):
    mu = jnp.mean(x, axis=-1, keepdims=True)
    xc = x - mu
    var = jnp.mean(xc * xc, axis=-1, keepdims=True)
    return xc * lax.rsqrt(var + EPS) * g + b


class Geometry:
    def __init__(self, n_prompt, prompt_len, n_sample, sample_len):
        self.n_prompt, self.prompt_len = n_prompt, prompt_len
        self.n_sample, self.sample_len = n_sample, sample_len
        self.t_prompt = n_prompt * prompt_len
        self.t_sample = n_sample * sample_len
        self.t = self.t_prompt + self.t_sample
        assert prompt_len % TOK_TILE == 0 and sample_len % TOK_TILE == 0
        assert n_sample + 1 <= COND_ROWS

    def cond_row(self, tile, tile_rows):
        n_p = self.t_prompt // tile_rows
        return jnp.where(tile < n_p, 0, 1 + (tile - n_p) // (self.sample_len // tile_rows))


def _mod_kernel(cond_ref, w_ref, b_ref, o_ref):
    c = cond_ref[...]
    s = c * jax.nn.sigmoid(c)
    o_ref[0, 0] = jnp.dot(s, w_ref[0], precision=HIGHEST, preferred_element_type=F32) + b_ref[0, 0]


def _modulation(cond, w_mod, b_mod):
    out = pl.pallas_call(
        _mod_kernel,
        name="modulation",
        grid=(DEPTH, 6),
        in_specs=[
            pl.BlockSpec((COND_ROWS, D), lambda l, j: (0, 0)),
            pl.BlockSpec((1, D, D), lambda l, j: (l, 0, j)),
            pl.BlockSpec((1, 1, 1, D), lambda l, j: (l, j, 0, 0)),
        ],
        out_specs=pl.BlockSpec((1, 1, COND_ROWS, D), lambda l, j: (l, j, 0, 0)),
        out_shape=jax.ShapeDtypeStruct((DEPTH, 6, COND_ROWS, D), F32),
        compiler_params=_params(("arbitrary", "arbitrary")),
    )(cond, w_mod, b_mod.reshape(DEPTH, 6, 1, D))
    return out.transpose(0, 2, 1, 3)


def _proj_a_kernel(x_ref, mod_ref, wq_ref, wkt_ref, wvo_ref, wg_ref, wgt_ref, bg_ref, bgt_ref,
                   q_ref, kt_ref, v_ref, so_ref, gc_ref, gr_ref):
    h = x_ref[...] * (1.0 + mod_ref[1:2, :]) + mod_ref[0:1, :]
    hb = h.astype(BF16)
    q_ref[...] = jnp.dot(hb, wq_ref[...], preferred_element_type=F32).astype(BF16)
    kt = lax.dot_general(wkt_ref[...], hb, NT_DIMS, preferred_element_type=F32)
    kt_ref[...] = (kt * (DK_A ** -0.5)).astype(BF16)
    vo = jnp.dot(hb, wvo_ref[...], preferred_element_type=F32)
    v_ref[...] = vo[:, :V_A].astype(BF16)
    so_ref[...] = jax.nn.sigmoid(vo[:, V_A:]).astype(BF16)
    gc_ref[...] = jnp.dot(h, wg_ref[...], precision=HIGHEST, preferred_element_type=F32) + bg_ref[...]
    gr_ref[...] = lax.dot_general(wgt_ref[...], h, NT_DIMS, precision=HIGHEST,
                                  preferred_element_type=F32) + bgt_ref[...]


def _proj_a(geo, x, mod_l, w_in, b_gates):
    t = geo.t
    n_gate = 4 * NH_A
    wq = w_in[:, :QK_A].astype(BF16)
    wkt = w_in[:, QK_A:2 * QK_A].T.astype(BF16)
    wvo = w_in[:, 2 * QK_A:2 * QK_A + 2 * V_A].astype(BF16)
    wg = w_in[:, 2 * QK_A + 2 * V_A:]
    wg_pad = jnp.pad(wg, ((0, 0), (0, LANES - n_gate)))
    bg = b_gates.reshape(n_gate).astype(F32)
    bg_pad = jnp.pad(bg, (0, LANES - n_gate)).reshape(1, LANES)
    tb = TOK_TILE
    full = lambda shape: pl.BlockSpec(shape, lambda i: (0,) * len(shape))
    return pl.pallas_call(
        _proj_a_kernel,
        name="proj_a",
        grid=(t // tb,),
        in_specs=[
            pl.BlockSpec((tb, D), lambda i: (i, 0)),
            pl.BlockSpec((None, 6, D), lambda i: (geo.cond_row(i, tb), 0, 0)),
            full((D, QK_A)), full((QK_A, D)), full((D, 2 * V_A)), full((D, LANES)), full((n_gate, D)),
            full((1, LANES)), full((n_gate, 1)),
        ],
        out_specs=[
            pl.BlockSpec((tb, QK_A), lambda i: (i, 0)),
            pl.BlockSpec((QK_A, tb), lambda i: (0, i)),
            pl.BlockSpec((tb, V_A), lambda i: (i, 0)),
            pl.BlockSpec((tb, V_A), lambda i: (i, 0)),
            pl.BlockSpec((tb, LANES), lambda i: (i, 0)),
            pl.BlockSpec((n_gate, tb), lambda i: (0, i)),
        ],
        out_shape=[
            jax.ShapeDtypeStruct((t, QK_A), BF16),
            jax.ShapeDtypeStruct((QK_A, t), BF16),
            jax.ShapeDtypeStruct((t, V_A), BF16),
            jax.ShapeDtypeStruct((t, V_A), BF16),
            jax.ShapeDtypeStruct((t, LANES), F32),
            jax.ShapeDtypeStruct((n_gate, t), F32),
        ],
        compiler_params=_params(("parallel",)),
    )(x, mod_l, wq, wkt, wvo, wg_pad, wg.T, bg_pad, bg.reshape(n_gate, 1))


def _mlstm_scan_kernel(*refs, chunk, has_state):
    if has_state:
        (q_ref, kt_ref, v_ref, gc_ref, gr_ref, c0_ref, n0_ref, m0_ref,
         h_ref, c_out, n_out, m_out, c_sc, n_sc, m_sc) = refs
    else:
        (q_ref, kt_ref, v_ref, gc_ref, gr_ref,
         h_ref, c_out, n_out, m_out, c_sc, n_sc, m_sc) = refs
    L = chunk
    d = pl.program_id(1)
    c = pl.program_id(2)
    fwd = d == 0

    @pl.when(c == 0)
    def _():
        if has_state:
            c_sc[...] = c0_ref[0, 0]
            n_sc[...] = n0_ref[0, 0]
            m_sc[...] = m0_ref[0, 0]
        else:
            c_sc[...] = jnp.zeros_like(c_sc)
            n_sc[...] = jnp.zeros_like(n_sc)
            m_sc[...] = jnp.zeros_like(m_sc)

    row = lax.broadcasted_iota(jnp.int32, (L, L), 0)
    col = lax.broadcasted_iota(jnp.int32, (L, L), 1)
    sgn = 1 - 2 * d
    causal = (row - col) * sgn >= 0
    tri = causal.astype(BF16)
    tri_t = ((col - row) * sgn >= 0).astype(BF16)

    gc = gc_ref[...]
    gr = gr_ref[...]
    bc_all = _dot3(tri, _log_sigmoid(gc))
    br_all = _dot3_r(_log_sigmoid(gr), tri_t)
    ones_blk = (lax.broadcasted_iota(jnp.int32, (L, LANES), 1) == 0).astype(BF16)

    for h in range(NH_A):
        b_c = jnp.where(fwd, bc_all[:, 4 + h:5 + h], bc_all[:, 12 + h:13 + h])
        b_r = jnp.where(fwd, br_all[4 + h:5 + h, :], br_all[12 + h:13 + h, :])
        i_r = jnp.where(fwd, gr[h:h + 1, :], gr[8 + h:9 + h, :])
        bl = jnp.where(fwd, b_r[:, L - 1:L], b_r[:, 0:1])
        q = q_ref[:, h * DK_A:(h + 1) * DK_A]
        kt = kt_ref[h * DK_A:(h + 1) * DK_A, :]
        v = v_ref[:, h * DV_A:(h + 1) * DV_A]
        m = m_sc[h:h + 1, 0:1]
        cst = c_sc[h]
        nst = n_sc[h]

        a_r = i_r - b_r
        logd = jnp.where(causal, b_c + a_r, -jnp.inf)
        inter = b_c + m
        m_t = jnp.maximum(inter, jnp.max(logd, axis=1, keepdims=True))
        dmat = jnp.exp(logd - m_t)
        e_int = jnp.exp(inter - m_t)
        s = (jnp.dot(q, kt, preferred_element_type=F32) * dmat).astype(BF16)
        num = (jnp.dot(s, v, preferred_element_type=F32)
               + e_int * jnp.dot(q, cst.astype(BF16), preferred_element_type=F32))
        den = (jnp.dot(s, ones_blk, preferred_element_type=F32)
               + e_int * jnp.dot(q, nst.astype(BF16), preferred_element_type=F32))[:, 0:1]
        h_ref[:, h * DV_A:(h + 1) * DV_A] = num / jnp.maximum(jnp.abs(den), jnp.exp(-m_t))

        logw = bl + a_r
        m_new = jnp.maximum(bl + m, jnp.max(logw, axis=1, keepdims=True))
        w = jnp.exp(logw - m_new)
        decay = jnp.exp(bl + m - m_new)
        kw = (kt.astype(F32) * w).astype(BF16)
        c_sc[h] = decay * cst + jnp.dot(kw, v, preferred_element_type=F32)
        n_sc[h] = decay * nst + jnp.dot(kw, ones_blk, preferred_element_type=F32)
        m_sc[h:h + 1, :] = jnp.broadcast_to(m_new, (1, LANES))

    @pl.when(c == pl.num_programs(2) - 1)
    def _():
        c_out[0, 0] = c_sc[...]
        n_out[0, 0] = n_sc[...]
        m_out[0, 0] = m_sc[...]


def _mlstm_scan(q, kt, v, gc, gr, *, row0, n_seq, seq_len, state=None):
    L = CHUNK_A
    nc = seq_len // L
    blk0 = row0 // L

    def loc_blk(b, d, c):
        return b * nc + c + d * (nc - 1 - 2 * c)

    def tok_blk(b, d, c):
        return blk0 + loc_blk(b, d, c)

    in_specs = [
        pl.BlockSpec((L, QK_A), lambda b, d, c: (tok_blk(b, d, c), 0)),
        pl.BlockSpec((QK_A, L), lambda b, d, c: (0, tok_blk(b, d, c))),
        pl.BlockSpec((L, V_A), lambda b, d, c: (tok_blk(b, d, c), 0)),
        pl.BlockSpec((L, LANES), lambda b, d, c: (tok_blk(b, d, c), 0)),
        pl.BlockSpec((4 * NH_A, L), lambda b, d, c: (0, tok_blk(b, d, c))),
    ]
    args = [q, kt, v, gc, gr]
    if state is not None:
        in_specs += [
            pl.BlockSpec((1, 1, NH_A, DK_A, DV_A), lambda b, d, c: (b, d, 0, 0, 0)),
            pl.BlockSpec((1, 1, NH_A, DK_A, LANES), lambda b, d, c: (b, d, 0, 0, 0)),
            pl.BlockSpec((1, 1, SUBLANES, LANES), lambda b, d, c: (b, d, 0, 0)),
        ]
        args += list(state)
    return pl.pallas_call(
        functools.partial(_mlstm_scan_kernel, chunk=L, has_state=state is not None),
        name="mlstm_scan_seeded" if state is not None else "mlstm_scan",
        grid=(n_seq, 2, nc),
        in_specs=in_specs,
        out_specs=[
            pl.BlockSpec((None, L, V_A), lambda b, d, c: (d, loc_blk(b, d, c), 0)),
            pl.BlockSpec((1, 1, NH_A, DK_A, DV_A), lambda b, d, c: (b, d, 0, 0, 0)),
            pl.BlockSpec((1, 1, NH_A, DK_A, LANES), lambda b, d, c: (b, d, 0, 0, 0)),
            pl.BlockSpec((1, 1, SUBLANES, LANES), lambda b, d, c: (b, d, 0, 0)),
        ],
        out_shape=[
            jax.ShapeDtypeStruct((2, n_seq * seq_len, V_A), F32),
            jax.ShapeDtypeStruct((n_seq, 2, NH_A, DK_A, DV_A), F32),
            jax.ShapeDtypeStruct((n_seq, 2, NH_A, DK_A, LANES), F32),
            jax.ShapeDtypeStruct((n_seq, 2, SUBLANES, LANES), F32),
        ],
        scratch_shapes=[
            pltpu.VMEM((NH_A, DK_A, DV_A), F32),
            pltpu.VMEM((NH_A, DK_A, LANES), F32),
            pltpu.VMEM((SUBLANES, LANES), F32),
        ],
        compiler_params=_params(("parallel", "parallel", "arbitrary")),
    )(*args)


def _out_a_kernel(hp_ref, hs_ref, so_ref, nw_ref, w_ref, x_ref, mod_ref, lg_ref, lb_ref, o_ref, *, n_prompt_tiles):
    is_prompt = pl.program_id(0) < n_prompt_tiles
    y = jnp.where(is_prompt, hp_ref[0] + hp_ref[1], hs_ref[0] + hs_ref[1])
    parts = []
    for h in range(NH_A):
        yh = y[:, h * DV_A:(h + 1) * DV_A]
        mu = jnp.mean(yh, axis=-1, keepdims=True)
        yc = yh - mu
        var = jnp.mean(yc * yc, axis=-1, keepdims=True)
        parts.append(yc * lax.rsqrt(var + EPS))
    yn = jnp.concatenate(parts, axis=-1) * nw_ref[...] * so_ref[...].astype(F32)
    out = jnp.dot(yn.astype(BF16), w_ref[...], preferred_element_type=F32)
    o_ref[...] = _layer_norm_rows(ALPHA * x_ref[...] + mod_ref[2:3, :] * out, lg_ref[...], lb_ref[...])


def _out_a(geo, h_prompt, h_sample, so, norm_w, w_out, x, mod_l, ln_g, ln_b):
    t = geo.t
    tb = TOK_TILE
    n_p = geo.t_prompt // tb
    full = lambda shape: pl.BlockSpec(shape, lambda i: (0,) * len(shape))
    return pl.pallas_call(
        functools.partial(_out_a_kernel, n_prompt_tiles=n_p),
        name="out_a",
        grid=(t // tb,),
        in_specs=[
            pl.BlockSpec((2, tb, V_A), lambda i: (0, jnp.minimum(i, n_p - 1), 0)),
            pl.BlockSpec((2, tb, V_A), lambda i: (0, jnp.maximum(i - n_p, 0), 0)),
            pl.BlockSpec((tb, V_A), lambda i: (i, 0)),
            full((1, V_A)), full((V_A, D)),
            pl.BlockSpec((tb, D), lambda i: (i, 0)),
            pl.BlockSpec((None, 6, D), lambda i: (geo.cond_row(i, tb), 0, 0)),
            full((1, D)), full((1, D)),
        ],
        out_specs=pl.BlockSpec((tb, D), lambda i: (i, 0)),
        out_shape=jax.ShapeDtypeStruct((t, D), F32),
        compiler_params=_params(("parallel",)),
    )(h_prompt, h_sample, so, norm_w.reshape(1, V_A).astype(F32), w_out.astype(BF16), x, mod_l,
      ln_g.reshape(1, D), ln_b.reshape(1, D))


def _proj_b_kernel(x_ref, mod_ref, w_ref, q_ref, pre_ref, v_ref, sg_ref):
    h = x_ref[...] * (1.0 + mod_ref[1:2, :]) + mod_ref[0:1, :]
    z = jnp.dot(h.astype(BF16), w_ref[...], preferred_element_type=F32)
    for hd in range(NH_B):
        lo = hd * DK_B
        qh = z[:, lo:lo + DK_B]
        q_ref[hd] = qh * jax.nn.sigmoid(qh)
        pre_ref[0, hd] = z[:, D + lo:D + lo + DK_B]
        pre_ref[1, hd] = z[:, 2 * D + lo:2 * D + lo + DK_B]
        v_ref[hd] = z[:, 3 * D + lo:3 * D + lo + DK_B].astype(BF16)
    g = z[:, 4 * D:]
    sg_ref[...] = (g * jax.nn.sigmoid(g)).astype(BF16)


def _proj_b(geo, x, mod_l, w_in):
    t = geo.t
    tb = TOK_TILE
    return pl.pallas_call(
        _proj_b_kernel,
        name="proj_b",
        grid=(t // tb,),
        in_specs=[
            pl.BlockSpec((tb, D), lambda i: (i, 0)),
            pl.BlockSpec((None, 6, D), lambda i: (geo.cond_row(i, tb), 0, 0)),
            pl.BlockSpec((D, 5 * D), lambda i: (0, 0)),
        ],
        out_specs=[
            pl.BlockSpec((NH_B, tb, DK_B), lambda i: (0, i, 0)),
            pl.BlockSpec((2, NH_B, tb, DK_B), lambda i: (0, 0, i, 0)),
            pl.BlockSpec((NH_B, tb, DK_B), lambda i: (0, i, 0)),
            pl.BlockSpec((tb, D), lambda i: (i, 0)),
        ],
        out_shape=[
            jax.ShapeDtypeStruct((NH_B, t, DK_B), F32),
            jax.ShapeDtypeStruct((2, NH_B, t, DK_B), F32),
            jax.ShapeDtypeStruct((NH_B, t, DK_B), BF16),
            jax.ShapeDtypeStruct((t, D), BF16),
        ],
        compiler_params=_params(("parallel",)),
    )(x, mod_l, w_in.astype(BF16))


CHUNK_B = 128
BAND = SUBLANES // 2
TN_DIMS = (((0,), (0,)), ((), ()))


def _hgrn_head(q, pre, lbv, v_bf, st, fwd):
    L = q.shape[0]
    f = lbv + (1.0 - lbv) * jax.nn.sigmoid(pre)
    lf = jnp.log(f)
    kk = (1.0 - lbv) * jax.nn.sigmoid(-pre)
    row = lax.broadcasted_iota(jnp.int32, (L, L), 0)
    col = lax.broadcasted_iota(jnp.int32, (L, L), 1)
    tri = ((row >= col) if fwd else (row <= col)).astype(BF16)
    b = _dot3(tri, lf)
    v32 = v_bf.astype(F32)
    tpos = lax.broadcasted_iota(jnp.int32, (L, DK_B), 0)

    o = jnp.sum(q * kk, axis=1, keepdims=True) * v32
    for dl in range(1, BAND):
        shift = dl if fwd else L - dl
        in_blk = (tpos % BAND >= dl) if fwd else (tpos % BAND + dl < BAND)
        e = jnp.exp(jnp.where(in_blk, b - pltpu.roll(b, shift, 0), -jnp.inf))
        a = jnp.sum(q * pltpu.roll(kk, shift, 0) * e, axis=1, keepdims=True)
        o = o + a * pltpu.roll(v32, shift, 0)

    att = jnp.zeros((L, L), F32)
    w = BAND
    while w < L:
        nb = L // (2 * w)
        b3 = b.reshape(nb, 2 * w, DK_B)
        edge = (b3[:, w - 1:w, :] if fwd else b3[:, w:w + 1, :])
        bmid = jnp.broadcast_to(edge, (nb, 2 * w, DK_B)).reshape(L, DK_B)
        second = (tpos // w) % 2 == 1
        t_side = second if fwd else jnp.logical_not(second)
        e = jnp.exp(jnp.where(t_side, b - bmid, bmid - b))
        qt = jnp.where(t_side, q * e, 0.0).astype(BF16)
        ks = jnp.where(t_side, 0.0, kk * e).astype(BF16)
        a = lax.dot_general(qt, ks, NT_DIMS, preferred_element_type=F32)
        att = att + jnp.where(row // (2 * w) == col // (2 * w), a, 0.0)
        w *= 2
    o = o + jnp.dot(att.astype(BF16), v_bf, preferred_element_type=F32)

    bl = b[L - 1:L, :] if fwd else b[0:1, :]
    o = o + lax.dot_general((q * jnp.exp(b)).astype(BF16), st.astype(BF16), NT_DIMS, preferred_element_type=F32)
    kd = (kk * jnp.exp(bl - b)).astype(BF16)
    st_new = jnp.exp(bl) * st + lax.dot_general(v_bf, kd, TN_DIMS, preferred_element_type=F32)
    return o, st_new


def _hgrn_scan_kernel(*refs, has_state):
    if has_state:
        q_ref, pre_ref, v_ref, lb_ref, s0_ref, o_ref, s_out, st_sc = refs
    else:
        q_ref, pre_ref, v_ref, lb_ref, o_ref, s_out, st_sc = refs
    d = pl.program_id(1)
    c = pl.program_id(2)

    @pl.when(c == 0)
    def _():
        if has_state:
            for hd in range(NH_B):
                st_sc[hd] = s0_ref[0, 0, hd].T
        else:
            st_sc[...] = jnp.zeros_like(st_sc)

    def run(fwd):
        def head(hd, carry):
            o, st_new = _hgrn_head(q_ref[hd], pre_ref[hd], lb_ref[hd], v_ref[hd], st_sc[hd], fwd)
            o_ref[hd] = o
            st_sc[hd] = st_new
            return carry
        lax.fori_loop(0, NH_B, head, 0, unroll=8)

    @pl.when(d == 0)
    def _():
        run(True)

    @pl.when(d == 1)
    def _():
        run(False)

    @pl.when(c == pl.num_programs(2) - 1)
    def _():
        for hd in range(NH_B):
            s_out[0, 0, hd] = st_sc[hd].T


def _hgrn_scan(q, pre, v, lbd, *, row0, n_seq, seq_len, state=None):
    L = CHUNK_B
    nc = seq_len // L
    blk0 = row0 // L

    def loc_blk(b, d, c):
        return b * nc + c + d * (nc - 1 - 2 * c)

    def tok_blk(b, d, c):
        return blk0 + loc_blk(b, d, c)

    in_specs = [
        pl.BlockSpec((NH_B, L, DK_B), lambda b, d, c: (0, tok_blk(b, d, c), 0)),
        pl.BlockSpec((None, NH_B, L, DK_B), lambda b, d, c: (d, 0, tok_blk(b, d, c), 0)),
        pl.BlockSpec((NH_B, L, DK_B), lambda b, d, c: (0, tok_blk(b, d, c), 0)),
        pl.BlockSpec((None, NH_B, 1, DK_B), lambda b, d, c: (d, 0, 0, 0)),
    ]
    args = [q, pre, v, lbd]
    if state is not None:
        in_specs.append(pl.BlockSpec((1, 1, NH_B, DK_B, DK_B), lambda b, d, c: (b, d, 0, 0, 0)))
        args.append(state)
    return pl.pallas_call(
        functools.partial(_hgrn_scan_kernel, has_state=state is not None),
        name="hgrn_scan_seeded" if state is not None else "hgrn_scan",
        grid=(n_seq, 2, nc),
        in_specs=in_specs,
        out_specs=[
            pl.BlockSpec((None, NH_B, L, DK_B), lambda b, d, c: (d, 0, loc_blk(b, d, c), 0)),
            pl.BlockSpec((1, 1, NH_B, DK_B, DK_B), lambda b, d, c: (b, d, 0, 0, 0)),
        ],
        out_shape=[
            jax.ShapeDtypeStruct((2, NH_B, n_seq * seq_len, DK_B), F32),
            jax.ShapeDtypeStruct((n_seq, 2, NH_B, DK_B, DK_B), F32),
        ],
        scratch_shapes=[pltpu.VMEM((NH_B, DK_B, DK_B), F32)],
        compiler_params=_params(("parallel", "parallel", "arbitrary")),
    )(*args)


def _out_b_kernel(op_ref, os_ref, sg_ref, nw_ref, w_ref, x_ref, mod_ref, lg_ref, lb_ref, out_ref, *, n_prompt_tiles):
    is_prompt = pl.program_id(0) < n_prompt_tiles
    parts = []
    for hd in range(NH_B):
        y = jnp.where(is_prompt, op_ref[0, hd] + op_ref[1, hd], os_ref[0, hd] + os_ref[1, hd])
        parts.append(y * lax.rsqrt(jnp.mean(y * y, axis=-1, keepdims=True) + EPS))
    yn = jnp.concatenate(parts, axis=-1) * nw_ref[...] * sg_ref[...].astype(F32)
    out = jnp.dot(yn.astype(BF16), w_ref[...], preferred_element_type=F32)
    out_ref[...] = _layer_norm_rows(ALPHA * x_ref[...] + mod_ref[2:3, :] * out, lg_ref[...], lb_ref[...])


def _out_b(geo, o_prompt, o_sample, sg, norm_w, w_out, x, mod_l, ln_g, ln_b):
    t = geo.t
    tb = TOK_TILE
    n_p = geo.t_prompt // tb
    full = lambda shape: pl.BlockSpec(shape, lambda i: (0,) * len(shape))
    return pl.pallas_call(
        functools.partial(_out_b_kernel, n_prompt_tiles=n_p),
        name="out_b",
        grid=(t // tb,),
        in_specs=[
            pl.BlockSpec((2, NH_B, tb, DK_B), lambda i: (0, 0, jnp.minimum(i, n_p - 1), 0)),
            pl.BlockSpec((2, NH_B, tb, DK_B), lambda i: (0, 0, jnp.maximum(i - n_p, 0), 0)),
            pl.BlockSpec((tb, D), lambda i: (i, 0)),
            full((1, D)), full((D, D)),
            pl.BlockSpec((tb, D), lambda i: (i, 0)),
            pl.BlockSpec((None, 6, D), lambda i: (geo.cond_row(i, tb), 0, 0)),
            full((1, D)), full((1, D)),
        ],
        out_specs=pl.BlockSpec((tb, D), lambda i: (i, 0)),
        out_shape=jax.ShapeDtypeStruct((t, D), F32),
        compiler_params=_params(("parallel",)),
    )(o_prompt, o_sample, sg, norm_w.reshape(1, D).astype(F32), w_out.astype(BF16), x, mod_l,
      ln_g.reshape(1, D), ln_b.reshape(1, D))


MOE_BLK = 256
ROW_TILES = D // LANES


def _first_index(hit, iota, size, axis):
    return jnp.min(jnp.where(hit, iota, size), axis=axis, keepdims=True)


def _router_kernel(x_ref, mod_ref, wrt_ref, eb_ref, e_ref, w_ref, r_ref, cnt_ref, cnt_sc):
    i = pl.program_id(0)
    tb = x_ref.shape[0]

    @pl.when(i == 0)
    def _():
        cnt_sc[...] = jnp.zeros_like(cnt_sc)

    h = x_ref[...] * (1.0 + mod_ref[4:5, :]) + mod_ref[3:4, :]
    logits = lax.dot_general(wrt_ref[...], h, NT_DIMS, precision=HIGHEST, preferred_element_type=F32)
    scores = jax.nn.sigmoid(logits)
    sel = scores + eb_ref[...]

    g3 = sel.reshape(N_GROUPS, GROUP_SIZE, tb)
    io3 = lax.broadcasted_iota(jnp.int32, g3.shape, 1)
    m1 = jnp.max(g3, axis=1, keepdims=True)
    first = _first_index(g3 == m1, io3, GROUP_SIZE, 1)
    m2 = jnp.max(jnp.where(io3 == first, -jnp.inf, g3), axis=1, keepdims=True)
    gscore = (m1 + m2).reshape(N_GROUPS, tb)

    iog = lax.broadcasted_iota(jnp.int32, gscore.shape, 0)
    gmask = jnp.zeros(gscore.shape, F32)
    for _ in range(TOPK_GROUPS):
        gm = jnp.max(gscore, axis=0, keepdims=True)
        pick = iog == _first_index(gscore == gm, iog, N_GROUPS, 0)
        gmask = jnp.where(pick, 1.0, gmask)
        gscore = jnp.where(pick, -jnp.inf, gscore)
    emask = jnp.broadcast_to(gmask.reshape(N_GROUPS, 1, tb), (N_GROUPS, GROUP_SIZE, tb)).reshape(N_EXPERTS, tb)
    cand = jnp.where(emask > 0.0, sel, -jnp.inf)

    ioe = lax.broadcasted_iota(jnp.int32, cand.shape, 0)
    picks, wts = [], []
    onehot = jnp.zeros(cand.shape, F32)
    for _ in range(TOP_K):
        cm = jnp.max(cand, axis=0, keepdims=True)
        idx = _first_index(cand == cm, ioe, N_EXPERTS, 0)
        pick = ioe == idx
        picks.append(pick)
        wts.append(jnp.sum(jnp.where(pick, scores, 0.0), axis=0, keepdims=True))
        onehot = onehot + pick.astype(F32)
        cand = jnp.where(pick, -jnp.inf, cand)
        e_ref[pl.ds(len(picks) - 1, 1), :] = idx
    wsum = wts[0]
    for wk in wts[1:]:
        wsum = wsum + wk
    for k in range(TOP_K):
        w_ref[pl.ds(k, 1), :] = wts[k] / wsum * ROUTED_SCALE

    r_io = lax.broadcasted_iota(jnp.int32, (tb, tb), 0)
    c_io = lax.broadcasted_iota(jnp.int32, (tb, tb), 1)
    before = (r_io < c_io).astype(BF16)
    rank = cnt_sc[:, 0:1] + jnp.dot(onehot.astype(BF16), before, preferred_element_type=F32)
    for k in range(TOP_K):
        r_ref[pl.ds(k, 1), :] = jnp.sum(jnp.where(picks[k], rank, 0.0), axis=0, keepdims=True).astype(jnp.int32)
    cnt_sc[...] = cnt_sc[...] + jnp.sum(onehot, axis=1, keepdims=True)
    cnt_ref[...] = cnt_sc[...]


def _router(geo, x, mod_l, w_router, e_bias):
    t = geo.t
    tb = TOK_TILE
    full = lambda shape: pl.BlockSpec(shape, lambda i: (0,) * len(shape))
    e, w, r, cnt = pl.pallas_call(
        _router_kernel,
        name="router",
        grid=(t // tb,),
        in_specs=[
            pl.BlockSpec((tb, D), lambda i: (i, 0)),
            pl.BlockSpec((None, 6, D), lambda i: (geo.cond_row(i, tb), 0, 0)),
            full((N_EXPERTS, D)), full((N_EXPERTS, 1)),
        ],
        out_specs=[
            pl.BlockSpec((TOP_K, tb), lambda i: (0, i)),
            pl.BlockSpec((TOP_K, tb), lambda i: (0, i)),
            pl.BlockSpec((TOP_K, tb), lambda i: (0, i)),
            full((N_EXPERTS, LANES)),
        ],
        out_shape=[
            jax.ShapeDtypeStruct((TOP_K, t), jnp.int32),
            jax.ShapeDtypeStruct((TOP_K, t), F32),
            jax.ShapeDtypeStruct((TOP_K, t), jnp.int32),
            jax.ShapeDtypeStruct((N_EXPERTS, LANES), F32),
        ],
        scratch_shapes=[pltpu.VMEM((N_EXPERTS, LANES), F32)],
        compiler_params=_params(("arbitrary",)),
    )(x, mod_l, w_router.T.astype(F32), e_bias.reshape(N_EXPERTS, 1).astype(F32))
    return e, w, r, cnt[:, 0].astype(jnp.int32)


def _slot_kernel(pstart_ref, e_ref, r_ref, o_ref):
    e = e_ref[...]
    slot = r_ref[...]
    for x in range(N_EXPERTS):
        slot = slot + jnp.where(e == x, pstart_ref[x], 0)
    o_ref[...] = slot


def _slots(geo, pstart, top_e, rank):
    tb = TOK_TILE
    nt = geo.t // tb
    return pl.pallas_call(
        _slot_kernel,
        name="slots",
        grid_spec=pltpu.PrefetchScalarGridSpec(
            num_scalar_prefetch=1,
            grid=(nt,),
            in_specs=[pl.BlockSpec((TOP_K, tb), lambda i, p: (0, i)),
                      pl.BlockSpec((TOP_K, tb), lambda i, p: (0, i))],
            out_specs=pl.BlockSpec((None, TOP_K, tb), lambda i, p: (i, 0, 0)),
        ),
        out_shape=jax.ShapeDtypeStruct((nt, TOP_K, tb), jnp.int32),
        compiler_params=_params(("parallel",)),
    )(pstart, top_e, rank)


def _store_packed(ref, x):
    n = x.shape[0]
    for j in range(ROW_TILES):
        ref[pl.ds(j, n, stride=ROW_TILES), :] = x[:, j * LANES:(j + 1) * LANES]


def _load_packed(ref, row0, n):
    return jnp.concatenate(
        [ref[pl.ds(row0 * ROW_TILES + j, n, stride=ROW_TILES), :] for j in range(ROW_TILES)], axis=1)


def _packed_row(ref, row):
    return ref.at[pl.ds(pl.multiple_of(row * ROW_TILES, ROW_TILES), ROW_TILES)]


def _dispatch_kernel(x_ref, mod_ref, slot_hbm, xs_in, xs_hbm, hs, slot_sm, sem_i, sem_d):
    del xs_in
    i = pl.program_id(0)
    tb = x_ref.shape[0]
    idx_copy = pltpu.make_async_copy(slot_hbm.at[i], slot_sm, sem_i)
    idx_copy.start()
    _store_packed(hs, x_ref[...] * (1.0 + mod_ref[4:5, :]) + mod_ref[3:4, :])
    idx_copy.wait()

    def issue(tok, carry):
        for k in range(TOP_K):
            pltpu.make_async_copy(_packed_row(hs, tok), xs_hbm.at[slot_sm[k, tok]], sem_d).start()
        return carry

    lax.fori_loop(0, tb, issue, 0)
    for k in range(TOP_K):
        pltpu.make_async_copy(hs, hs, sem_d).wait()


def _dispatch(geo, x, mod_l, slots, n_rows):
    tb = TOK_TILE
    xs0 = jnp.zeros((n_rows, ROW_TILES, LANES), F32)
    return pl.pallas_call(
        _dispatch_kernel,
        name="dispatch",
        grid=(geo.t // tb,),
        in_specs=[
            pl.BlockSpec((tb, D), lambda i: (i, 0)),
            pl.BlockSpec((None, 6, D), lambda i: (geo.cond_row(i, tb), 0, 0)),
            pl.BlockSpec(memory_space=pl.ANY),
            pl.BlockSpec(memory_space=pl.ANY),
        ],
        out_specs=pl.BlockSpec(memory_space=pl.ANY),
        out_shape=jax.ShapeDtypeStruct((n_rows, ROW_TILES, LANES), F32),
        scratch_shapes=[
            pltpu.VMEM((tb * ROW_TILES, LANES), F32),
            pltpu.SMEM((TOP_K, tb), jnp.int32),
            pltpu.SemaphoreType.DMA(()),
            pltpu.SemaphoreType.DMA(()),
        ],
        input_output_aliases={3: 0},
        compiler_params=_params(("arbitrary",)),
    )(x, mod_l, slots, xs0)


def _ffn_kernel(blk_e_ref, n_used_ref, xs_ref, wg_ref, wu_ref, wd_ref, y_ref, wg_sc, wu_sc, wd_sc):
    b = pl.program_id(0)
    used = b < n_used_ref[0]
    new_expert = (b == 0) | (blk_e_ref[b] != blk_e_ref[jnp.maximum(b - 1, 0)])

    @pl.when(used & new_expert)
    def _():
        wg_sc[...] = wg_ref[...].astype(BF16)
        wu_sc[...] = wu_ref[...].astype(BF16)
        wd_sc[...] = wd_ref[...].astype(BF16)

    @pl.when(used)
    def _():
        x = _load_packed(xs_ref, 0, MOE_BLK).astype(BF16)
        g = jnp.dot(x, wg_sc[...], preferred_element_type=F32)
        u = jnp.dot(x, wu_sc[...], preferred_element_type=F32)
        hmid = (g * jax.nn.sigmoid(g) * u).astype(BF16)
        _store_packed(y_ref, jnp.dot(hmid, wd_sc[...], preferred_element_type=F32))

    @pl.when(jnp.logical_not(used))
    def _():
        y_ref[...] = jnp.zeros_like(y_ref)


def _ffn(xs, blk_e, n_used, layer, wg, wu, wd, n_blocks):
    def blk(b, be, nu):
        return jnp.maximum(jnp.minimum(b, nu[0] - 1), 0)

    def w_idx(b, be, nu):
        return (layer, be[blk(b, be, nu)], 0, 0)

    return pl.pallas_call(
        _ffn_kernel,
        name="expert_ffn",
        grid_spec=pltpu.PrefetchScalarGridSpec(
            num_scalar_prefetch=2,
            grid=(n_blocks,),
            in_specs=[
                pl.BlockSpec((MOE_BLK * ROW_TILES, LANES), lambda b, be, nu: (blk(b, be, nu), 0)),
                pl.BlockSpec((None, None, D, D_EXPERT), w_idx),
                pl.BlockSpec((None, None, D, D_EXPERT), w_idx),
                pl.BlockSpec((None, None, D_EXPERT, D), w_idx),
            ],
            out_specs=pl.BlockSpec((MOE_BLK * ROW_TILES, LANES), lambda b, be, nu: (b, 0)),
            scratch_shapes=[pltpu.VMEM((D, D_EXPERT), BF16), pltpu.VMEM((D, D_EXPERT), BF16),
                            pltpu.VMEM((D_EXPERT, D), BF16)],
        ),
        out_shape=jax.ShapeDtypeStruct((xs.shape[0] * ROW_TILES, LANES), F32),
        compiler_params=_params(("arbitrary",)),
    )(blk_e, n_used, xs.reshape(-1, LANES), wg, wu, wd).reshape(xs.shape)


def _combine_kernel(x_ref, mod_ref, wt_ref, slot_hbm, yb_hbm, sg_ref, su_ref, sd_ref, lg_ref, lb_ref,
                    o_ref, rows, slot_sm, sem_i, sem_d):
    i = pl.program_id(0)
    tb = x_ref.shape[0]
    idx_copy = pltpu.make_async_copy(slot_hbm.at[i], slot_sm, sem_i)
    idx_copy.start()
    idx_copy.wait()

    def issue(tok, carry):
        for k in range(TOP_K):
            pltpu.make_async_copy(yb_hbm.at[slot_sm[k, tok]], _packed_row(rows, k * tb + tok), sem_d).start()
        return carry

    lax.fori_loop(0, tb, issue, 0)

    x = x_ref[...]
    hb = (x * (1.0 + mod_ref[4:5, :]) + mod_ref[3:4, :]).astype(BF16)
    g = jnp.dot(hb, sg_ref[...], preferred_element_type=F32)
    u = jnp.dot(hb, su_ref[...], preferred_element_type=F32)
    shared = jnp.dot((g * jax.nn.sigmoid(g) * u).astype(BF16), sd_ref[...], preferred_element_type=F32)

    pltpu.make_async_copy(rows, rows, sem_d).wait()
    routed = jnp.zeros((tb, D), F32)
    for k in range(TOP_K):
        routed = routed + _load_packed(rows, k * tb, tb) * wt_ref[:, k:k + 1]
    ff = routed + shared
    o_ref[...] = _layer_norm_rows(ALPHA * x + mod_ref[5:6, :] * ff, lg_ref[...], lb_ref[...])


def _combine(geo, x, mod_l, wt, slots, yb, sg, su, sd, ln_g, ln_b):
    tb = TOK_TILE
    full = lambda shape: pl.BlockSpec(shape, lambda i: (0,) * len(shape))
    return pl.pallas_call(
        _combine_kernel,
        name="combine",
        grid=(geo.t // tb,),
        in_specs=[
            pl.BlockSpec((tb, D), lambda i: (i, 0)),
            pl.BlockSpec((None, 6, D), lambda i: (geo.cond_row(i, tb), 0, 0)),
            pl.BlockSpec((tb, TOP_K), lambda i: (i, 0)),
            pl.BlockSpec(memory_space=pl.ANY),
            pl.BlockSpec(memory_space=pl.ANY),
            full((D, D_EXPERT)), full((D, D_EXPERT)), full((D_EXPERT, D)), full((1, D)), full((1, D)),
        ],
        out_specs=pl.BlockSpec((tb, D), lambda i: (i, 0)),
        out_shape=jax.ShapeDtypeStruct((geo.t, D), F32),
        scratch_shapes=[
            pltpu.VMEM((TOP_K * tb * ROW_TILES, LANES), F32),
            pltpu.SMEM((TOP_K, tb), jnp.int32),
            pltpu.SemaphoreType.DMA(()),
            pltpu.SemaphoreType.DMA(()),
        ],
        compiler_params=_params(("arbitrary",)),
    )(x, mod_l, wt, slots, yb, sg.astype(BF16), su.astype(BF16), sd.astype(BF16),
      ln_g.reshape(1, D), ln_b.reshape(1, D))


def _moe_layer(geo, x, mod_l, w_router, e_bias, layer, wg, wu, wd, sg, su, sd, ln_g, ln_b):
    top_e, w, rank, counts = _router(geo, x, mod_l, w_router, e_bias)
    n_blocks = (geo.t * TOP_K) // MOE_BLK + N_EXPERTS
    padded = (counts + MOE_BLK - 1) // MOE_BLK * MOE_BLK
    pend = jnp.cumsum(padded)
    pstart = (pend - padded).astype(jnp.int32)
    blk_row0 = jnp.arange(n_blocks, dtype=jnp.int32) * MOE_BLK
    blk_e = jnp.minimum(jnp.sum((pend[None, :] <= blk_row0[:, None]).astype(jnp.int32), axis=1), N_EXPERTS - 1)
    n_used = (pend[-1:] // MOE_BLK).astype(jnp.int32)
    slots = _slots(geo, pstart, top_e, rank)
    xs = _dispatch(geo, x, mod_l, slots, n_blocks * MOE_BLK)
    yb = _ffn(xs, blk_e, n_used, layer, wg, wu, wd, n_blocks)
    return _combine(geo, x, mod_l, w.T, slots, yb, sg, su, sd, ln_g, ln_b)


def _pos_embed(rows):
    quarter = D // 4
    omega = 1.0 / (POS_BASE ** (jnp.arange(quarter, dtype=F32) / quarter))
    r, col = jnp.meshgrid(jnp.arange(rows, dtype=F32), jnp.arange(GRID_W, dtype=F32), indexing='ij')
    r = r.reshape(-1, 1) * omega
    col = col.reshape(-1, 1) * omega
    return jnp.concatenate([jnp.sin(r), jnp.cos(r), jnp.sin(col), jnp.cos(col)], axis=-1)


def _mlstm_layer(geo, x, mod_l, j, a_w_in, a_b_gates, a_norm, a_w_out, ln_g, ln_b,
                 state_C, state_n, state_m):
    q, kt, v, so, gc, gr = _proj_a(geo, x, mod_l, a_w_in[j], a_b_gates[j])
    hp, c_p, n_p, m_p = _mlstm_scan(q, kt, v, gc, gr, row0=0, n_seq=geo.n_prompt, seq_len=geo.prompt_len)
    ns = geo.n_sample
    n0 = jnp.pad(state_n[:, j].astype(F32)[..., None], ((0, 0),) * 4 + ((0, LANES - 1),))
    m0 = jnp.pad(state_m[:, j].astype(F32), ((0, 0), (0, 0), (0, SUBLANES - NH_A)))
    m0 = jnp.broadcast_to(m0[..., None], (ns, 2, SUBLANES, LANES))
    hs, _, _, _ = _mlstm_scan(q, kt, v, gc, gr, row0=geo.t_prompt, n_seq=ns, seq_len=geo.sample_len,
                              state=(state_C[:, j].astype(F32), n0, m0))
    x1 = _out_a(geo, hp, hs, so, a_norm[j], a_w_out[j], x, mod_l, ln_g, ln_b)
    return x1, c_p, n_p[..., 0], m_p[:, :, :NH_A, 0]


def _hgrn_layer(geo, x, mod_l, j, lb_layer, b_w_in, b_norm, b_w_out, ln_g, ln_b, state_S):
    q, pre, v, sg = _proj_b(geo, x, mod_l, b_w_in[j])
    lbd = lb_layer.reshape(2, NH_B, 1, DK_B)
    op, s_p = _hgrn_scan(q, pre, v, lbd, row0=0, n_seq=geo.n_prompt, seq_len=geo.prompt_len)
    os_, _ = _hgrn_scan(q, pre, v, lbd, row0=geo.t_prompt, n_seq=geo.n_sample, seq_len=geo.sample_len,
                        state=state_S[:, j].astype(F32))
    x1 = _out_b(geo, op, os_, sg, b_norm[j], b_w_out[j], x, mod_l, ln_g, ln_b)
    return x1, s_p


def kernel(x_prompt, x_sample, state_mlstm_C, state_mlstm_n, state_mlstm_m, state_hgrn_S, c, c_ctx, w_mod, b_mod, ln_g, ln_b, a_w_in, a_b_gates, a_norm, a_w_out, b_w_in, b_lb, b_norm, b_w_out, w_router, e_bias, w_gate, w_up, w_down, ws_gate, ws_up, ws_down):
    bp, sp, _ = x_prompt.shape
    bs, ss, _ = x_sample.shape
    geo = Geometry(bp, sp, bs, ss)
    cond = jnp.zeros((COND_ROWS, D), F32).at[0].set(c_ctx).at[1:1 + bs].set(c)
    mod = _modulation(cond, w_mod, b_mod)
    x = jnp.concatenate([x_prompt.reshape(-1, D), (x_sample + _pos_embed(ss // GRID_W)[None]).reshape(-1, D)], axis=0)
    x1, new_c, new_n, new_m = _mlstm_layer(geo, x, mod[0], 0, a_w_in, a_b_gates, a_norm, a_w_out,
                                           ln_g[0, 0], ln_b[0, 0], state_mlstm_C, state_mlstm_n, state_mlstm_m)
    x2 = _moe_layer(geo, x1, mod[0], w_router[0], e_bias[0], 0, w_gate, w_up, w_down, ws_gate[0], ws_up[0], ws_down[0],
                    ln_g[0, 1], ln_b[0, 1])
    sm = jax.nn.softmax(b_lb.astype(F32), axis=0)
    lb_all = jnp.cumsum(sm, axis=0) - sm[0]
    x3, new_s = _hgrn_layer(geo, x2, mod[1], 0, lb_all[1], b_w_in, b_norm, b_w_out, ln_g[1, 0], ln_b[1, 0],
                            state_hgrn_S)
    x4 = _moe_layer(geo, x3, mod[1], w_router[1], e_bias[1], 1, w_gate, w_up, w_down, ws_gate[1], ws_up[1], ws_down[1],
                    ln_g[1, 1], ln_b[1, 1])
    y_prompt = x4[:geo.t_prompt].reshape(bp, sp, D)
    y_sample = x4[geo.t_prompt:].reshape(bs, ss, D)
    return y_prompt, y_sample, new_c[:, None], new_n[:, None], new_m[:, None], new_s[:, None]
```

```python
import functools

import jax
import jax.numpy as jnp
from jax import lax
from jax.experimental import pallas as pl
from jax.experimental.pallas import tpu as pltpu
from jax.experimental.pallas import tpu_sc as plsc

F32 = jnp.float32
BF16 = jnp.bfloat16
HIGHEST = lax.Precision.HIGHEST

D = 1024
DEPTH = 2
GRID_W = 64
POS_BASE = 10000.0
EPS = 1e-6
ALPHA = (2.0 * DEPTH) ** 0.25
NH_A, DK_A, DV_A = 4, 128, 256
QK_A, V_A = NH_A * DK_A, NH_A * DV_A
NH_B, DK_B = 8, 128
N_EXPERTS, TOP_K, N_GROUPS, TOPK_GROUPS = 64, 8, 8, 4
GROUP_SIZE = N_EXPERTS // N_GROUPS
D_EXPERT = D // 4
ROUTED_SCALE = 2.5

LANES = 128
SUBLANES = 8
COND_ROWS = 8
TOK_TILE = 256
CHUNK_A = 256
VMEM_LIMIT = 56 * 1024 * 1024

NT_DIMS = (((1,), (1,)), ((), ()))


def _params(sem):
    return pltpu.CompilerParams(dimension_semantics=sem, vmem_limit_bytes=VMEM_LIMIT)


def _split3(x):
    hi = x.astype(BF16)
    r = x - hi.astype(F32)
    mid = r.astype(BF16)
    lo = (r - mid.astype(F32)).astype(BF16)
    return hi, mid, lo


def _dot3(a_bf, x, transpose_side=None):
    hi, mid, lo = _split3(x)
    return (jnp.dot(a_bf, hi, preferred_element_type=F32)
            + jnp.dot(a_bf, mid, preferred_element_type=F32)
            + jnp.dot(a_bf, lo, preferred_element_type=F32))


def _dot3_r(x, a_bf):
    hi, mid, lo = _split3(x)
    return (jnp.dot(hi, a_bf, preferred_element_type=F32)
            + jnp.dot(mid, a_bf, preferred_element_type=F32)
            + jnp.dot(lo, a_bf, preferred_element_type=F32))


def _log_sigmoid(x):
    return jnp.minimum(x, 0.0) - jnp.log1p(jnp.exp(-jnp.abs(x)))


def _layer_norm_rows(x, g, b):
    mu = jnp.mean(x, axis=-1, keepdims=True)
    xc = x - mu
    var = jnp.mean(xc * xc, axis=-1, keepdims=True)
    return xc * lax.rsqrt(var + EPS) * g + b


class Geometry:
    def __init__(self, n_prompt, prompt_len, n_sample, sample_len):
        self.n_prompt, self.prompt_len = n_prompt, prompt_len
        self.n_sample, self.sample_len = n_sample, sample_len
        self.t_prompt = n_prompt * prompt_len
        self.t_sample = n_sample * sample_len
        self.t = self.t_prompt + self.t_sample
        assert prompt_len % TOK_TILE == 0 and sample_len % TOK_TILE == 0
        assert n_sample + 1 <= COND_ROWS

    def cond_row(self, tile, tile_rows):
        n_p = self.t_prompt // tile_rows
        return jnp.where(tile < n_p, 0, 1 + (tile - n_p) // (self.sample_len // tile_rows))


def _mod_kernel(cond_ref, w_ref, b_ref, o_ref):
    c = cond_ref[...]
    s = c * jax.nn.sigmoid(c)
    o_ref[0, 0] = jnp.dot(s, w_ref[0], precision=HIGHEST, preferred_element_type=F32) + b_ref[0, 0]


def _modulation(cond, w_mod, b_mod):
    out = pl.pallas_call(
        _mod_kernel,
        name="modulation",
        grid=(DEPTH, 6),
        in_specs=[
            pl.BlockSpec((COND_ROWS, D), lambda l, j: (0, 0)),
            pl.BlockSpec((1, D, D), lambda l, j: (l, 0, j)),
            pl.BlockSpec((1, 1, 1, D), lambda l, j: (l, j, 0, 0)),
        ],
        out_specs=pl.BlockSpec((1, 1, COND_ROWS, D), lambda l, j: (l, j, 0, 0)),
        out_shape=jax.ShapeDtypeStruct((DEPTH, 6, COND_ROWS, D), F32),
        compiler_params=_params(("arbitrary", "arbitrary")),
    )(cond, w_mod, b_mod.reshape(DEPTH, 6, 1, D))
    return out.transpose(0, 2, 1, 3)


def _proj_a_kernel(x_ref, mod_ref, wq_ref, wkt_ref, wvo_ref, wg_ref, wgt_ref, bg_ref, bgt_ref,
                   q_ref, kt_ref, v_ref, so_ref, gc_ref, gr_ref):
    h = x_ref[...] * (1.0 + mod_ref[1:2, :]) + mod_ref[0:1, :]
    hb = h.astype(BF16)
    q_ref[...] = jnp.dot(hb, wq_ref[...], preferred_element_type=F32).astype(BF16)
    kt = lax.dot_general(wkt_ref[...], hb, NT_DIMS, preferred_element_type=F32)
    kt_ref[...] = (kt * (DK_A ** -0.5)).astype(BF16)
    vo = jnp.dot(hb, wvo_ref[...], preferred_element_type=F32)
    v_ref[...] = vo[:, :V_A].astype(BF16)
    so_ref[...] = jax.nn.sigmoid(vo[:, V_A:]).astype(BF16)
    gc_ref[...] = jnp.dot(h, wg_ref[...], precision=HIGHEST, preferred_element_type=F32) + bg_ref[...]
    gr_ref[...] = lax.dot_general(wgt_ref[...], h, NT_DIMS, precision=HIGHEST,
                                  preferred_element_type=F32) + bgt_ref[...]


def _proj_a(geo, x, mod_l, w_in, b_gates):
    t = geo.t
    n_gate = 4 * NH_A
    wq = w_in[:, :QK_A].astype(BF16)
    wkt = w_in[:, QK_A:2 * QK_A].T.astype(BF16)
    wvo = w_in[:, 2 * QK_A:2 * QK_A + 2 * V_A].astype(BF16)
    wg = w_in[:, 2 * QK_A + 2 * V_A:]
    wg_pad = jnp.pad(wg, ((0, 0), (0, LANES - n_gate)))
    bg = b_gates.reshape(n_gate).astype(F32)
    bg_pad = jnp.pad(bg, (0, LANES - n_gate)).reshape(1, LANES)
    tb = TOK_TILE
    full = lambda shape: pl.BlockSpec(shape, lambda i: (0,) * len(shape))
    return pl.pallas_call(
        _proj_a_kernel,
        name="proj_a",
        grid=(t // tb,),
        in_specs=[
            pl.BlockSpec((tb, D), lambda i: (i, 0)),
            pl.BlockSpec((None, 6, D), lambda i: (geo.cond_row(i, tb), 0, 0)),
            full((D, QK_A)), full((QK_A, D)), full((D, 2 * V_A)), full((D, LANES)), full((n_gate, D)),
            full((1, LANES)), full((n_gate, 1)),
        ],
        out_specs=[
            pl.BlockSpec((tb, QK_A), lambda i: (i, 0)),
            pl.BlockSpec((QK_A, tb), lambda i: (0, i)),
            pl.BlockSpec((tb, V_A), lambda i: (i, 0)),
            pl.BlockSpec((tb, V_A), lambda i: (i, 0)),
            pl.BlockSpec((tb, LANES), lambda i: (i, 0)),
            pl.BlockSpec((n_gate, tb), lambda i: (0, i)),
        ],
        out_shape=[
            jax.ShapeDtypeStruct((t, QK_A), BF16),
            jax.ShapeDtypeStruct((QK_A, t), BF16),
            jax.ShapeDtypeStruct((t, V_A), BF16),
            jax.ShapeDtypeStruct((t, V_A), BF16),
            jax.ShapeDtypeStruct((t, LANES), F32),
            jax.ShapeDtypeStruct((n_gate, t), F32),
        ],
        compiler_params=_params(("parallel",)),
    )(x, mod_l, wq, wkt, wvo, wg_pad, wg.T, bg_pad, bg.reshape(n_gate, 1))


def _mlstm_scan_kernel(*refs, chunk, has_state):
    if has_state:
        (q_ref, kt_ref, v_ref, gc_ref, gr_ref, c0_ref, n0_ref, m0_ref,
         h_ref, c_out, n_out, m_out, c_sc, n_sc, m_sc) = refs
    else:
        (q_ref, kt_ref, v_ref, gc_ref, gr_ref,
         h_ref, c_out, n_out, m_out, c_sc, n_sc, m_sc) = refs
    L = chunk
    d = pl.program_id(1)
    c = pl.program_id(2)
    fwd = d == 0

    @pl.when(c == 0)
    def _():
        if has_state:
            c_sc[...] = c0_ref[0, 0]
            n_sc[...] = n0_ref[0, 0]
            m_sc[...] = m0_ref[0, 0]
        else:
            c_sc[...] = jnp.zeros_like(c_sc)
            n_sc[...] = jnp.zeros_like(n_sc)
            m_sc[...] = jnp.zeros_like(m_sc)

    row = lax.broadcasted_iota(jnp.int32, (L, L), 0)
    col = lax.broadcasted_iota(jnp.int32, (L, L), 1)
    sgn = 1 - 2 * d
    causal = (row - col) * sgn >= 0
    tri = causal.astype(BF16)
    tri_t = ((col - row) * sgn >= 0).astype(BF16)

    gc = gc_ref[...]
    gr = gr_ref[...]
    bc_all = _dot3(tri, _log_sigmoid(gc))
    br_all = _dot3_r(_log_sigmoid(gr), tri_t)
    ones_blk = (lax.broadcasted_iota(jnp.int32, (L, LANES), 1) == 0).astype(BF16)

    for h in range(NH_A):
        b_c = jnp.where(fwd, bc_all[:, 4 + h:5 + h], bc_all[:, 12 + h:13 + h])
        b_r = jnp.where(fwd, br_all[4 + h:5 + h, :], br_all[12 + h:13 + h, :])
        i_r = jnp.where(fwd, gr[h:h + 1, :], gr[8 + h:9 + h, :])
        bl = jnp.where(fwd, b_r[:, L - 1:L], b_r[:, 0:1])
        q = q_ref[:, h * DK_A:(h + 1) * DK_A]
        kt = kt_ref[h * DK_A:(h + 1) * DK_A, :]
        v = v_ref[:, h * DV_A:(h + 1) * DV_A]
        m = m_sc[h:h + 1, 0:1]
        cst = c_sc[h]
        nst = n_sc[h]

        a_r = i_r - b_r
        logd = jnp.where(causal, b_c + a_r, -jnp.inf)
        inter = b_c + m
        m_t = jnp.maximum(inter, jnp.max(logd, axis=1, keepdims=True))
        dmat = jnp.exp(logd - m_t)
        e_int = jnp.exp(inter - m_t)
        s = (jnp.dot(q, kt, preferred_element_type=F32) * dmat).astype(BF16)
        num = (jnp.dot(s, v, preferred_element_type=F32)
               + e_int * jnp.dot(q, cst.astype(BF16), preferred_element_type=F32))
        den = (jnp.dot(s, ones_blk, preferred_element_type=F32)
               + e_int * jnp.dot(q, nst.astype(BF16), preferred_element_type=F32))[:, 0:1]
        h_ref[:, h * DV_A:(h + 1) * DV_A] = num / jnp.maximum(jnp.abs(den), jnp.exp(-m_t))

        logw = bl + a_r
        m_new = jnp.maximum(bl + m, jnp.max(logw, axis=1, keepdims=True))
        w = jnp.exp(logw - m_new)
        decay = jnp.exp(bl + m - m_new)
        kw = (kt.astype(F32) * w).astype(BF16)
        c_sc[h] = decay * cst + jnp.dot(kw, v, preferred_element_type=F32)
        n_sc[h] = decay * nst + jnp.dot(kw, ones_blk, preferred_element_type=F32)
        m_sc[h:h + 1, :] = jnp.broadcast_to(m_new, (1, LANES))

    @pl.when(c == pl.num_programs(2) - 1)
    def _():
        c_out[0, 0] = c_sc[...]
        n_out[0, 0] = n_sc[...]
        m_out[0, 0] = m_sc[...]


def _mlstm_scan(q, kt, v, gc, gr, *, row0, n_seq, seq_len, state=None):
    L = CHUNK_A
    nc = seq_len // L
    blk0 = row0 // L

    def loc_blk(b, d, c):
        return b * nc + c + d * (nc - 1 - 2 * c)

    def tok_blk(b, d, c):
        return blk0 + loc_blk(b, d, c)

    in_specs = [
        pl.BlockSpec((L, QK_A), lambda b, d, c: (tok_blk(b, d, c), 0)),
        pl.BlockSpec((QK_A, L), lambda b, d, c: (0, tok_blk(b, d, c))),
        pl.BlockSpec((L, V_A), lambda b, d, c: (tok_blk(b, d, c), 0)),
        pl.BlockSpec((L, LANES), lambda b, d, c: (tok_blk(b, d, c), 0)),
        pl.BlockSpec((4 * NH_A, L), lambda b, d, c: (0, tok_blk(b, d, c))),
    ]
    args = [q, kt, v, gc, gr]
    if state is not None:
        in_specs += [
            pl.BlockSpec((1, 1, NH_A, DK_A, DV_A), lambda b, d, c: (b, d, 0, 0, 0)),
            pl.BlockSpec((1, 1, NH_A, DK_A, LANES), lambda b, d, c: (b, d, 0, 0, 0)),
            pl.BlockSpec((1, 1, SUBLANES, LANES), lambda b, d, c: (b, d, 0, 0)),
        ]
        args += list(state)
    return pl.pallas_call(
        functools.partial(_mlstm_scan_kernel, chunk=L, has_state=state is not None),
        name="mlstm_scan_seeded" if state is not None else "mlstm_scan",
        grid=(n_seq, 2, nc),
        in_specs=in_specs,
        out_specs=[
            pl.BlockSpec((None, L, V_A), lambda b, d, c: (d, loc_blk(b, d, c), 0)),
            pl.BlockSpec((1, 1, NH_A, DK_A, DV_A), lambda b, d, c: (b, d, 0, 0, 0)),
            pl.BlockSpec((1, 1, NH_A, DK_A, LANES), lambda b, d, c: (b, d, 0, 0, 0)),
            pl.BlockSpec((1, 1, SUBLANES, LANES), lambda b, d, c: (b, d, 0, 0)),
        ],
        out_shape=[
            jax.ShapeDtypeStruct((2, n_seq * seq_len, V_A), F32),
            jax.ShapeDtypeStruct((n_seq, 2, NH_A, DK_A, DV_A), F32),
            jax.ShapeDtypeStruct((n_seq, 2, NH_A, DK_A, LANES), F32),
            jax.ShapeDtypeStruct((n_seq, 2, SUBLANES, LANES), F32),
        ],
        scratch_shapes=[
            pltpu.VMEM((NH_A, DK_A, DV_A), F32),
            pltpu.VMEM((NH_A, DK_A, LANES), F32),
            pltpu.VMEM((SUBLANES, LANES), F32),
        ],
        compiler_params=_params(("parallel", "parallel", "arbitrary")),
    )(*args)


def _out_a_kernel(hp_ref, hs_ref, so_ref, nw_ref, w_ref, x_ref, mod_ref, lg_ref, lb_ref, o_ref, *, n_prompt_tiles):
    is_prompt = pl.program_id(0) < n_prompt_tiles
    y = jnp.where(is_prompt, hp_ref[0] + hp_ref[1], hs_ref[0] + hs_ref[1])
    parts = []
    for h in range(NH_A):
        yh = y[:, h * DV_A:(h + 1) * DV_A]
        mu = jnp.mean(yh, axis=-1, keepdims=True)
        yc = yh - mu
        var = jnp.mean(yc * yc, axis=-1, keepdims=True)
        parts.append(yc * lax.rsqrt(var + EPS))
    yn = jnp.concatenate(parts, axis=-1) * nw_ref[...] * so_ref[...].astype(F32)
    out = jnp.dot(yn.astype(BF16), w_ref[...], preferred_element_type=F32)
    o_ref[...] = _layer_norm_rows(ALPHA * x_ref[...] + mod_ref[2:3, :] * out, lg_ref[...], lb_ref[...])


def _out_a(geo, h_prompt, h_sample, so, norm_w, w_out, x, mod_l, ln_g, ln_b):
    t = geo.t
    tb = TOK_TILE
    n_p = geo.t_prompt // tb
    full = lambda shape: pl.BlockSpec(shape, lambda i: (0,) * len(shape))
    return pl.pallas_call(
        functools.partial(_out_a_kernel, n_prompt_tiles=n_p),
        name="out_a",
        grid=(t // tb,),
        in_specs=[
            pl.BlockSpec((2, tb, V_A), lambda i: (0, jnp.minimum(i, n_p - 1), 0)),
            pl.BlockSpec((2, tb, V_A), lambda i: (0, jnp.maximum(i - n_p, 0), 0)),
            pl.BlockSpec((tb, V_A), lambda i: (i, 0)),
            full((1, V_A)), full((V_A, D)),
            pl.BlockSpec((tb, D), lambda i: (i, 0)),
            pl.BlockSpec((None, 6, D), lambda i: (geo.cond_row(i, tb), 0, 0)),
            full((1, D)), full((1, D)),
        ],
        out_specs=pl.BlockSpec((tb, D), lambda i: (i, 0)),
        out_shape=jax.ShapeDtypeStruct((t, D), F32),
        compiler_params=_params(("parallel",)),
    )(h_prompt, h_sample, so, norm_w.reshape(1, V_A).astype(F32), w_out.astype(BF16), x, mod_l,
      ln_g.reshape(1, D), ln_b.reshape(1, D))


def _proj_b_kernel(x_ref, mod_ref, w_ref, q_ref, pre_ref, v_ref, sg_ref):
    h = x_ref[...] * (1.0 + mod_ref[1:2, :]) + mod_ref[0:1, :]
    z = jnp.dot(h.astype(BF16), w_ref[...], preferred_element_type=F32)
    for hd in range(NH_B):
        lo = hd * DK_B
        qh = z[:, lo:lo + DK_B]
        q_ref[hd] = qh * jax.nn.sigmoid(qh)
        pre_ref[0, hd] = z[:, D + lo:D + lo + DK_B]
        pre_ref[1, hd] = z[:, 2 * D + lo:2 * D + lo + DK_B]
        v_ref[hd] = z[:, 3 * D + lo:3 * D + lo + DK_B].astype(BF16)
    g = z[:, 4 * D:]
    sg_ref[...] = (g * jax.nn.sigmoid(g)).astype(BF16)


def _proj_b(geo, x, mod_l, w_in):
    t = geo.t
    tb = TOK_TILE
    return pl.pallas_call(
        _proj_b_kernel,
        name="proj_b",
        grid=(t // tb,),
        in_specs=[
            pl.BlockSpec((tb, D), lambda i: (i, 0)),
            pl.BlockSpec((None, 6, D), lambda i: (geo.cond_row(i, tb), 0, 0)),
            pl.BlockSpec((D, 5 * D), lambda i: (0, 0)),
        ],
        out_specs=[
            pl.BlockSpec((NH_B, tb, DK_B), lambda i: (0, i, 0)),
            pl.BlockSpec((2, NH_B, tb, DK_B), lambda i: (0, 0, i, 0)),
            pl.BlockSpec((NH_B, tb, DK_B), lambda i: (0, i, 0)),
            pl.BlockSpec((tb, D), lambda i: (i, 0)),
        ],
        out_shape=[
            jax.ShapeDtypeStruct((NH_B, t, DK_B), F32),
            jax.ShapeDtypeStruct((2, NH_B, t, DK_B), F32),
            jax.ShapeDtypeStruct((NH_B, t, DK_B), BF16),
            jax.ShapeDtypeStruct((t, D), BF16),
        ],
        compiler_params=_params(("parallel",)),
    )(x, mod_l, w_in.astype(BF16))


CHUNK_B = 128
BAND = SUBLANES // 2
TN_DIMS = (((0,), (0,)), ((), ()))


def _hgrn_head(q, pre, lbv, v_bf, st, fwd):
    L = q.shape[0]
    f = lbv + (1.0 - lbv) * jax.nn.sigmoid(pre)
    lf = jnp.log(f)
    kk = (1.0 - lbv) * jax.nn.sigmoid(-pre)
    row = lax.broadcasted_iota(jnp.int32, (L, L), 0)
    col = lax.broadcasted_iota(jnp.int32, (L, L), 1)
    tri = ((row >= col) if fwd else (row <= col)).astype(BF16)
    b = _dot3(tri, lf)
    v32 = v_bf.astype(F32)
    tpos = lax.broadcasted_iota(jnp.int32, (L, DK_B), 0)

    o = jnp.sum(q * kk, axis=1, keepdims=True) * v32
    for dl in range(1, BAND):
        shift = dl if fwd else L - dl
        in_blk = (tpos % BAND >= dl) if fwd else (tpos % BAND + dl < BAND)
        e = jnp.exp(jnp.where(in_blk, b - pltpu.roll(b, shift, 0), -jnp.inf))
        a = jnp.sum(q * pltpu.roll(kk, shift, 0) * e, axis=1, keepdims=True)
        o = o + a * pltpu.roll(v32, shift, 0)

    att = jnp.zeros((L, L), F32)
    w = BAND
    while w < L:
        nb = L // (2 * w)
        b3 = b.reshape(nb, 2 * w, DK_B)
        edge = (b3[:, w - 1:w, :] if fwd else b3[:, w:w + 1, :])
        bmid = jnp.broadcast_to(edge, (nb, 2 * w, DK_B)).reshape(L, DK_B)
        second = (tpos // w) % 2 == 1
        t_side = second if fwd else jnp.logical_not(second)
        e = jnp.exp(jnp.where(t_side, b - bmid, bmid - b))
        qt = jnp.where(t_side, q * e, 0.0).astype(BF16)
        ks = jnp.where(t_side, 0.0, kk * e).astype(BF16)
        a = lax.dot_general(qt, ks, NT_DIMS, preferred_element_type=F32)
        att = att + jnp.where(row // (2 * w) == col // (2 * w), a, 0.0)
        w *= 2
    o = o + jnp.dot(att.astype(BF16), v_bf, preferred_element_type=F32)

    bl = b[L - 1:L, :] if fwd else b[0:1, :]
    o = o + lax.dot_general((q * jnp.exp(b)).astype(BF16), st.astype(BF16), NT_DIMS, preferred_element_type=F32)
    kd = (kk * jnp.exp(bl - b)).astype(BF16)
    st_new = jnp.exp(bl) * st + lax.dot_general(v_bf, kd, TN_DIMS, preferred_element_type=F32)
    return o, st_new


def _hgrn_scan_kernel(*refs, has_state):
    if has_state:
        q_ref, pre_ref, v_ref, lb_ref, s0_ref, o_ref, s_out, st_sc = refs
    else:
        q_ref, pre_ref, v_ref, lb_ref, o_ref, s_out, st_sc = refs
    d = pl.program_id(1)
    c = pl.program_id(2)

    @pl.when(c == 0)
    def _():
        if has_state:
            for hd in range(NH_B):
                st_sc[hd] = s0_ref[0, 0, hd].T
        else:
            st_sc[...] = jnp.zeros_like(st_sc)

    def run(fwd):
        def head(hd, carry):
            o, st_new = _hgrn_head(q_ref[hd], pre_ref[hd], lb_ref[hd], v_ref[hd], st_sc[hd], fwd)
            o_ref[hd] = o
            st_sc[hd] = st_new
            return carry
        lax.fori_loop(0, NH_B, head, 0, unroll=8)

    @pl.when(d == 0)
    def _():
        run(True)

    @pl.when(d == 1)
    def _():
        run(False)

    @pl.when(c == pl.num_programs(2) - 1)
    def _():
        for hd in range(NH_B):
            s_out[0, 0, hd] = st_sc[hd].T


def _hgrn_scan(q, pre, v, lbd, *, row0, n_seq, seq_len, state=None):
    L = CHUNK_B
    nc = seq_len // L
    blk0 = row0 // L

    def loc_blk(b, d, c):
        return b * nc + c + d * (nc - 1 - 2 * c)

    def tok_blk(b, d, c):
        return blk0 + loc_blk(b, d, c)

    in_specs = [
        pl.BlockSpec((NH_B, L, DK_B), lambda b, d, c: (0, tok_blk(b, d, c), 0)),
        pl.BlockSpec((None, NH_B, L, DK_B), lambda b, d, c: (d, 0, tok_blk(b, d, c), 0)),
        pl.BlockSpec((NH_B, L, DK_B), lambda b, d, c: (0, tok_blk(b, d, c), 0)),
        pl.BlockSpec((None, NH_B, 1, DK_B), lambda b, d, c: (d, 0, 0, 0)),
    ]
    args = [q, pre, v, lbd]
    if state is not None:
        in_specs.append(pl.BlockSpec((1, 1, NH_B, DK_B, DK_B), lambda b, d, c: (b, d, 0, 0, 0)))
        args.append(state)
    return pl.pallas_call(
        functools.partial(_hgrn_scan_kernel, has_state=state is not None),
        name="hgrn_scan_seeded" if state is not None else "hgrn_scan",
        grid=(n_seq, 2, nc),
        in_specs=in_specs,
        out_specs=[
            pl.BlockSpec((None, NH_B, L, DK_B), lambda b, d, c: (d, 0, loc_blk(b, d, c), 0)),
            pl.BlockSpec((1, 1, NH_B, DK_B, DK_B), lambda b, d, c: (b, d, 0, 0, 0)),
        ],
        out_shape=[
            jax.ShapeDtypeStruct((2, NH_B, n_seq * seq_len, DK_B), F32),
            jax.ShapeDtypeStruct((n_seq, 2, NH_B, DK_B, DK_B), F32),
        ],
        scratch_shapes=[pltpu.VMEM((NH_B, DK_B, DK_B), F32)],
        compiler_params=_params(("parallel", "parallel", "arbitrary")),
    )(*args)


def _out_b_kernel(op_ref, os_ref, sg_ref, nw_ref, w_ref, x_ref, mod_ref, lg_ref, lb_ref, out_ref, *, n_prompt_tiles):
    is_prompt = pl.program_id(0) < n_prompt_tiles
    parts = []
    for hd in range(NH_B):
        y = jnp.where(is_prompt, op_ref[0, hd] + op_ref[1, hd], os_ref[0, hd] + os_ref[1, hd])
        parts.append(y * lax.rsqrt(jnp.mean(y * y, axis=-1, keepdims=True) + EPS))
    yn = jnp.concatenate(parts, axis=-1) * nw_ref[...] * sg_ref[...].astype(F32)
    out = jnp.dot(yn.astype(BF16), w_ref[...], preferred_element_type=F32)
    out_ref[...] = _layer_norm_rows(ALPHA * x_ref[...] + mod_ref[2:3, :] * out, lg_ref[...], lb_ref[...])


def _out_b(geo, o_prompt, o_sample, sg, norm_w, w_out, x, mod_l, ln_g, ln_b):
    t = geo.t
    tb = TOK_TILE
    n_p = geo.t_prompt // tb
    full = lambda shape: pl.BlockSpec(shape, lambda i: (0,) * len(shape))
    return pl.pallas_call(
        functools.partial(_out_b_kernel, n_prompt_tiles=n_p),
        name="out_b",
        grid=(t // tb,),
        in_specs=[
            pl.BlockSpec((2, NH_B, tb, DK_B), lambda i: (0, 0, jnp.minimum(i, n_p - 1), 0)),
            pl.BlockSpec((2, NH_B, tb, DK_B), lambda i: (0, 0, jnp.maximum(i - n_p, 0), 0)),
            pl.BlockSpec((tb, D), lambda i: (i, 0)),
            full((1, D)), full((D, D)),
            pl.BlockSpec((tb, D), lambda i: (i, 0)),
            pl.BlockSpec((None, 6, D), lambda i: (geo.cond_row(i, tb), 0, 0)),
            full((1, D)), full((1, D)),
        ],
        out_specs=pl.BlockSpec((tb, D), lambda i: (i, 0)),
        out_shape=jax.ShapeDtypeStruct((t, D), F32),
        compiler_params=_params(("parallel",)),
    )(o_prompt, o_sample, sg, norm_w.reshape(1, D).astype(F32), w_out.astype(BF16), x, mod_l,
      ln_g.reshape(1, D), ln_b.reshape(1, D))


MOE_BLK = 256
CHUNK_W = 256
ROW_CHUNKS = D // CHUNK_W
SC_WINDOW = 128


def _first_index(hit, iota, size, axis):
    return jnp.min(jnp.where(hit, iota, size), axis=axis, keepdims=True)


def _router_kernel(x_ref, mod_ref, wrt_ref, eb_ref, e_ref, w_ref, r_ref, cnt_ref, h_ref, cnt_sc):
    i = pl.program_id(0)
    tb = x_ref.shape[0]

    @pl.when(i == 0)
    def _():
        cnt_sc[...] = jnp.zeros_like(cnt_sc)

    h = x_ref[...] * (1.0 + mod_ref[4:5, :]) + mod_ref[3:4, :]
    for c in range(ROW_CHUNKS):
        h_ref[c] = h[:, c * CHUNK_W:(c + 1) * CHUNK_W]
    logits = lax.dot_general(wrt_ref[...], h, NT_DIMS, precision=HIGHEST, preferred_element_type=F32)
    scores = jax.nn.sigmoid(logits)
    sel = scores + eb_ref[...]

    g3 = sel.reshape(N_GROUPS, GROUP_SIZE, tb)
    io3 = lax.broadcasted_iota(jnp.int32, g3.shape, 1)
    m1 = jnp.max(g3, axis=1, keepdims=True)
    first = _first_index(g3 == m1, io3, GROUP_SIZE, 1)
    m2 = jnp.max(jnp.where(io3 == first, -jnp.inf, g3), axis=1, keepdims=True)
    gscore = (m1 + m2).reshape(N_GROUPS, tb)

    iog = lax.broadcasted_iota(jnp.int32, gscore.shape, 0)
    gmask = jnp.zeros(gscore.shape, F32)
    for _ in range(TOPK_GROUPS):
        gm = jnp.max(gscore, axis=0, keepdims=True)
        pick = iog == _first_index(gscore == gm, iog, N_GROUPS, 0)
        gmask = jnp.where(pick, 1.0, gmask)
        gscore = jnp.where(pick, -jnp.inf, gscore)
    emask = jnp.broadcast_to(gmask.reshape(N_GROUPS, 1, tb), (N_GROUPS, GROUP_SIZE, tb)).reshape(N_EXPERTS, tb)
    cand = jnp.where(emask > 0.0, sel, -jnp.inf)

    ioe = lax.broadcasted_iota(jnp.int32, cand.shape, 0)
    picks, wts = [], []
    onehot = jnp.zeros(cand.shape, F32)
    for _ in range(TOP_K):
        cm = jnp.max(cand, axis=0, keepdims=True)
        idx = _first_index(cand == cm, ioe, N_EXPERTS, 0)
        pick = ioe == idx
        picks.append(pick)
        wts.append(jnp.sum(jnp.where(pick, scores, 0.0), axis=0, keepdims=True))
        onehot = onehot + pick.astype(F32)
        cand = jnp.where(pick, -jnp.inf, cand)
        e_ref[pl.ds(len(picks) - 1, 1), :] = idx
    wsum = wts[0]
    for wk in wts[1:]:
        wsum = wsum + wk
    for k in range(TOP_K):
        w_ref[pl.ds(k, 1), :] = wts[k] / wsum * ROUTED_SCALE

    r_io = lax.broadcasted_iota(jnp.int32, (tb, tb), 0)
    c_io = lax.broadcasted_iota(jnp.int32, (tb, tb), 1)
    before = (r_io < c_io).astype(BF16)
    rank = cnt_sc[:, 0:1] + jnp.dot(onehot.astype(BF16), before, preferred_element_type=F32)
    for k in range(TOP_K):
        r_ref[pl.ds(k, 1), :] = jnp.sum(jnp.where(picks[k], rank, 0.0), axis=0, keepdims=True).astype(jnp.int32)
    cnt_sc[...] = cnt_sc[...] + jnp.sum(onehot, axis=1, keepdims=True)
    cnt_ref[...] = cnt_sc[...]


def _router(geo, x, mod_l, w_router, e_bias):
    t = geo.t
    tb = TOK_TILE
    full = lambda shape: pl.BlockSpec(shape, lambda i: (0,) * len(shape))
    e, w, r, cnt, h = pl.pallas_call(
        _router_kernel,
        name="router",
        grid=(t // tb,),
        in_specs=[
            pl.BlockSpec((tb, D), lambda i: (i, 0)),
            pl.BlockSpec((None, 6, D), lambda i: (geo.cond_row(i, tb), 0, 0)),
            full((N_EXPERTS, D)), full((N_EXPERTS, 1)),
        ],
        out_specs=[
            pl.BlockSpec((TOP_K, tb), lambda i: (0, i)),
            pl.BlockSpec((TOP_K, tb), lambda i: (0, i)),
            pl.BlockSpec((TOP_K, tb), lambda i: (0, i)),
            full((N_EXPERTS, LANES)),
            pl.BlockSpec((ROW_CHUNKS, tb, CHUNK_W), lambda i: (0, i, 0)),
        ],
        out_shape=[
            jax.ShapeDtypeStruct((TOP_K, t), jnp.int32),
            jax.ShapeDtypeStruct((TOP_K, t), F32),
            jax.ShapeDtypeStruct((TOP_K, t), jnp.int32),
            jax.ShapeDtypeStruct((N_EXPERTS, LANES), F32),
            jax.ShapeDtypeStruct((ROW_CHUNKS, t, CHUNK_W), F32),
        ],
        scratch_shapes=[pltpu.VMEM((N_EXPERTS, LANES), F32)],
        compiler_params=_params(("arbitrary",)),
    )(x, mod_l, w_router.T.astype(F32), e_bias.reshape(N_EXPERTS, 1).astype(F32))
    return e, w, r, cnt[:, 0].astype(jnp.int32), h


def _slot_kernel(pstart_ref, e_ref, r_ref, o_ref):
    e = e_ref[...]
    slot = r_ref[...]
    for x in range(N_EXPERTS):
        slot = slot + jnp.where(e == x, pstart_ref[x], 0)
    o_ref[...] = slot


def _slots(geo, pstart, top_e, rank):
    tb = TOK_TILE
    return pl.pallas_call(
        _slot_kernel,
        name="slots",
        grid_spec=pltpu.PrefetchScalarGridSpec(
            num_scalar_prefetch=1,
            grid=(geo.t // tb,),
            in_specs=[pl.BlockSpec((TOP_K, tb), lambda i, p: (0, i)),
                      pl.BlockSpec((TOP_K, tb), lambda i, p: (0, i))],
            out_specs=pl.BlockSpec((TOP_K, tb), lambda i, p: (0, i)),
        ),
        out_shape=jax.ShapeDtypeStruct((TOP_K, geo.t), jnp.int32),
        compiler_params=_params(("parallel",)),
    )(pstart, top_e, rank)


def _sc_mesh():
    return plsc.VectorSubcoreMesh(core_axis_name="core", subcore_axis_name="subcore")


def _sc_scatter(rows, idx, n_out, src_block):
    n_idx = idx.shape[0]

    @pl.kernel(out_type=jax.ShapeDtypeStruct((n_out, CHUNK_W), rows.dtype), mesh=_sc_mesh(), scratch_types=[],
               name="sc_dispatch")
    def scatter(x_hbm, i_hbm, o_hbm):
        def body(x_vmem, i_vmem):
            pltpu.sync_copy(x_vmem, o_hbm.at[i_vmem.at[0]])

        pltpu.emit_pipeline(
            body,
            grid=(n_idx // SC_WINDOW,),
            in_specs=[pl.BlockSpec((SC_WINDOW, CHUNK_W), index_map=lambda w: (src_block(w), 0)),
                      pl.BlockSpec((1, SC_WINDOW), index_map=lambda w: (0, w))],
            out_specs=[],
            core_axis_name=("core", "subcore"),
            dimension_semantics=(pltpu.PARALLEL,),
        )(x_hbm, i_hbm)

    return scatter(rows, idx.reshape(1, n_idx))


def _sc_gather(table, idx):
    n_idx = idx.shape[0]

    @pl.kernel(out_type=jax.ShapeDtypeStruct((n_idx, CHUNK_W), table.dtype), mesh=_sc_mesh(),
               name="sc_combine_gather")
    def gather(t_hbm, i_hbm, o_hbm):
        def body(i_vmem, o_vmem):
            pltpu.sync_copy(t_hbm.at[i_vmem.at[0]], o_vmem)

        pltpu.emit_pipeline(
            body,
            grid=(n_idx // SC_WINDOW,),
            in_specs=[pl.BlockSpec((1, SC_WINDOW), index_map=lambda w: (0, w))],
            out_specs=[pl.BlockSpec((SC_WINDOW, CHUNK_W), index_map=lambda w: (w, 0))],
            core_axis_name=("core", "subcore"),
            dimension_semantics=(pltpu.PARALLEL,),
        )(i_hbm, o_hbm)

    return gather(table, idx.reshape(1, n_idx))


def _ffn_kernel(blk_e_ref, blk_valid_ref, n_used_ref, xs_ref, wg_ref, wu_ref, wd_ref, y_ref, wg_sc, wu_sc, wd_sc):
    b = pl.program_id(0)
    used = b < n_used_ref[0]
    new_expert = (b == 0) | (blk_e_ref[b] != blk_e_ref[jnp.maximum(b - 1, 0)])

    @pl.when(used & new_expert)
    def _():
        wg_sc[...] = wg_ref[...].astype(BF16)
        wu_sc[...] = wu_ref[...].astype(BF16)
        wd_sc[...] = wd_ref[...].astype(BF16)

    @pl.when(used)
    def _():
        x = jnp.concatenate([xs_ref[c] for c in range(ROW_CHUNKS)], axis=1)
        row = lax.broadcasted_iota(jnp.int32, (MOE_BLK, 1), 0)
        x = jnp.where(row < blk_valid_ref[b], x, 0.0).astype(BF16)
        g = jnp.dot(x, wg_sc[...], preferred_element_type=F32)
        u = jnp.dot(x, wu_sc[...], preferred_element_type=F32)
        hmid = (g * jax.nn.sigmoid(g) * u).astype(BF16)
        y = jnp.dot(hmid, wd_sc[...], preferred_element_type=F32)
        for c in range(ROW_CHUNKS):
            y_ref[c] = y[:, c * CHUNK_W:(c + 1) * CHUNK_W]

    @pl.when(jnp.logical_not(used))
    def _():
        y_ref[...] = jnp.zeros_like(y_ref)


def _ffn(xs, blk_e, blk_valid, n_used, layer, wg, wu, wd, n_blocks):
    def blk(b, be, bv, nu):
        return jnp.maximum(jnp.minimum(b, nu[0] - 1), 0)

    def w_idx(b, be, bv, nu):
        return (layer, be[blk(b, be, bv, nu)], 0, 0)

    return pl.pallas_call(
        _ffn_kernel,
        name="expert_ffn",
        grid_spec=pltpu.PrefetchScalarGridSpec(
            num_scalar_prefetch=3,
            grid=(n_blocks,),
            in_specs=[
                pl.BlockSpec((ROW_CHUNKS, MOE_BLK, CHUNK_W), lambda b, be, bv, nu: (0, blk(b, be, bv, nu), 0)),
                pl.BlockSpec((None, None, D, D_EXPERT), w_idx),
                pl.BlockSpec((None, None, D, D_EXPERT), w_idx),
                pl.BlockSpec((None, None, D_EXPERT, D), w_idx),
            ],
            out_specs=pl.BlockSpec((ROW_CHUNKS, MOE_BLK, CHUNK_W), lambda b, be, bv, nu: (0, b, 0)),
            scratch_shapes=[pltpu.VMEM((D, D_EXPERT), BF16), pltpu.VMEM((D, D_EXPERT), BF16),
                            pltpu.VMEM((D_EXPERT, D), BF16)],
        ),
        out_shape=jax.ShapeDtypeStruct(xs.shape, F32),
        compiler_params=_params(("arbitrary",)),
    )(blk_e, blk_valid, n_used, xs, wg, wu, wd)


def _combine_kernel(x_ref, mod_ref, wt_ref, y_ref, sg_ref, su_ref, sd_ref, lg_ref, lb_ref, o_ref):
    x = x_ref[...]
    hb = (x * (1.0 + mod_ref[4:5, :]) + mod_ref[3:4, :]).astype(BF16)
    g = jnp.dot(hb, sg_ref[...], preferred_element_type=F32)
    u = jnp.dot(hb, su_ref[...], preferred_element_type=F32)
    ff = jnp.dot((g * jax.nn.sigmoid(g) * u).astype(BF16), sd_ref[...], preferred_element_type=F32)
    for k in range(TOP_K):
        yk = jnp.concatenate([y_ref[c, k] for c in range(ROW_CHUNKS)], axis=1)
        ff = ff + yk * wt_ref[:, k:k + 1]
    o_ref[...] = _layer_norm_rows(ALPHA * x + mod_ref[5:6, :] * ff, lg_ref[...], lb_ref[...])


def _combine(geo, x, mod_l, wt, ytok, sg, su, sd, ln_g, ln_b):
    tb = TOK_TILE
    full = lambda shape: pl.BlockSpec(shape, lambda i: (0,) * len(shape))
    return pl.pallas_call(
        _combine_kernel,
        name="combine",
        grid=(geo.t // tb,),
        in_specs=[
            pl.BlockSpec((tb, D), lambda i: (i, 0)),
            pl.BlockSpec((None, 6, D), lambda i: (geo.cond_row(i, tb), 0, 0)),
            pl.BlockSpec((tb, TOP_K), lambda i: (i, 0)),
            pl.BlockSpec((ROW_CHUNKS, TOP_K, tb, CHUNK_W), lambda i: (0, 0, i, 0)),
            full((D, D_EXPERT)), full((D, D_EXPERT)), full((D_EXPERT, D)), full((1, D)), full((1, D)),
        ],
        out_specs=pl.BlockSpec((tb, D), lambda i: (i, 0)),
        out_shape=jax.ShapeDtypeStruct((geo.t, D), F32),
        compiler_params=_params(("parallel",)),
    )(x, mod_l, wt, ytok, sg.astype(BF16), su.astype(BF16), sd.astype(BF16),
      ln_g.reshape(1, D), ln_b.reshape(1, D))


def _moe_layer(geo, x, mod_l, w_router, e_bias, layer, wg, wu, wd, sg, su, sd, ln_g, ln_b):
    t = geo.t
    top_e, w, rank, counts, h = _router(geo, x, mod_l, w_router, e_bias)
    n_blocks = (t * TOP_K) // MOE_BLK + N_EXPERTS
    n_rows = n_blocks * MOE_BLK
    padded = (counts + MOE_BLK - 1) // MOE_BLK * MOE_BLK
    pend = jnp.cumsum(padded)
    pstart = (pend - padded).astype(jnp.int32)
    blk_row0 = jnp.arange(n_blocks, dtype=jnp.int32) * MOE_BLK
    blk_e = jnp.minimum(jnp.sum((pend[None, :] <= blk_row0[:, None]).astype(jnp.int32), axis=1), N_EXPERTS - 1)
    blk_valid = jnp.clip(pstart[blk_e] + counts[blk_e] - blk_row0, 0, MOE_BLK).astype(jnp.int32)
    n_used = (pend[-1:] // MOE_BLK).astype(jnp.int32)
    slots = _slots(geo, pstart, top_e, rank)
    idx = (slots.reshape(1, TOP_K * t) + (jnp.arange(ROW_CHUNKS, dtype=jnp.int32) * n_rows)[:, None]).reshape(-1)
    win_per_chunk = TOP_K * t // SC_WINDOW
    tok_windows = t // SC_WINDOW

    def src_block(wdw):
        return (wdw // win_per_chunk) * tok_windows + (wdw % win_per_chunk) % tok_windows

    xs = _sc_scatter(h.reshape(ROW_CHUNKS * t, CHUNK_W), idx, ROW_CHUNKS * n_rows, src_block)
    yb = _ffn(xs.reshape(ROW_CHUNKS, n_rows, CHUNK_W), blk_e, blk_valid, n_used, layer, wg, wu, wd, n_blocks)
    ytok = _sc_gather(yb.reshape(ROW_CHUNKS * n_rows, CHUNK_W), idx)
    return _combine(geo, x, mod_l, w.T, ytok.reshape(ROW_CHUNKS, TOP_K, t, CHUNK_W), sg, su, sd, ln_g, ln_b)


def _pos_embed(rows):
    quarter = D // 4
    omega = 1.0 / (POS_BASE ** (jnp.arange(quarter, dtype=F32) / quarter))
    r, col = jnp.meshgrid(jnp.arange(rows, dtype=F32), jnp.arange(GRID_W, dtype=F32), indexing='ij')
    r = r.reshape(-1, 1) * omega
    col = col.reshape(-1, 1) * omega
    return jnp.concatenate([jnp.sin(r), jnp.cos(r), jnp.sin(col), jnp.cos(col)], axis=-1)


def _mlstm_layer(geo, x, mod_l, j, a_w_in, a_b_gates, a_norm, a_w_out, ln_g, ln_b,
                 state_C, state_n, state_m):
    q, kt, v, so, gc, gr = _proj_a(geo, x, mod_l, a_w_in[j], a_b_gates[j])
    hp, c_p, n_p, m_p = _mlstm_scan(q, kt, v, gc, gr, row0=0, n_seq=geo.n_prompt, seq_len=geo.prompt_len)
    ns = geo.n_sample
    n0 = jnp.pad(state_n[:, j].astype(F32)[..., None], ((0, 0),) * 4 + ((0, LANES - 1),))
    m0 = jnp.pad(state_m[:, j].astype(F32), ((0, 0), (0, 0), (0, SUBLANES - NH_A)))
    m0 = jnp.broadcast_to(m0[..., None], (ns, 2, SUBLANES, LANES))
    hs, _, _, _ = _mlstm_scan(q, kt, v, gc, gr, row0=geo.t_prompt, n_seq=ns, seq_len=geo.sample_len,
                              state=(state_C[:, j].astype(F32), n0, m0))
    x1 = _out_a(geo, hp, hs, so, a_norm[j], a_w_out[j], x, mod_l, ln_g, ln_b)
    return x1, c_p, n_p[..., 0], m_p[:, :, :NH_A, 0]


def _hgrn_layer(geo, x, mod_l, j, lb_layer, b_w_in, b_norm, b_w_out, ln_g, ln_b, state_S):
    q, pre, v, sg = _proj_b(geo, x, mod_l, b_w_in[j])
    lbd = lb_layer.reshape(2, NH_B, 1, DK_B)
    op, s_p = _hgrn_scan(q, pre, v, lbd, row0=0, n_seq=geo.n_prompt, seq_len=geo.prompt_len)
    os_, _ = _hgrn_scan(q, pre, v, lbd, row0=geo.t_prompt, n_seq=geo.n_sample, seq_len=geo.sample_len,
                        state=state_S[:, j].astype(F32))
    x1 = _out_b(geo, op, os_, sg, b_norm[j], b_w_out[j], x, mod_l, ln_g, ln_b)
    return x1, s_p


def kernel(x_prompt, x_sample, state_mlstm_C, state_mlstm_n, state_mlstm_m, state_hgrn_S, c, c_ctx, w_mod, b_mod, ln_g, ln_b, a_w_in, a_b_gates, a_norm, a_w_out, b_w_in, b_lb, b_norm, b_w_out, w_router, e_bias, w_gate, w_up, w_down, ws_gate, ws_up, ws_down):
    bp, sp, _ = x_prompt.shape
    bs, ss, _ = x_sample.shape
    geo = Geometry(bp, sp, bs, ss)
    cond = jnp.zeros((COND_ROWS, D), F32).at[0].set(c_ctx).at[1:1 + bs].set(c)
    mod = _modulation(cond, w_mod, b_mod)
    x = jnp.concatenate([x_prompt.reshape(-1, D), (x_sample + _pos_embed(ss // GRID_W)[None]).reshape(-1, D)], axis=0)
    x1, new_c, new_n, new_m = _mlstm_layer(geo, x, mod[0], 0, a_w_in, a_b_gates, a_norm, a_w_out,
                                           ln_g[0, 0], ln_b[0, 0], state_mlstm_C, state_mlstm_n, state_mlstm_m)
    x2 = _moe_layer(geo, x1, mod[0], w_router[0], e_bias[0], 0, w_gate, w_up, w_down, ws_gate[0], ws_up[0], ws_down[0],
                    ln_g[0, 1], ln_b[0, 1])
    sm = jax.nn.softmax(b_lb.astype(F32), axis=0)
    lb_all = jnp.cumsum(sm, axis=0) - sm[0]
    x3, new_s = _hgrn_layer(geo, x2, mod[1], 0, lb_all[1], b_w_in, b_norm, b_w_out, ln_g[1, 0], ln_b[1, 0],
                            state_hgrn_S)
    x4 = _moe_layer(geo, x3, mod[1], w_router[1], e_bias[1], 1, w_gate, w_up, w_down, ws_gate[1], ws_up[1], ws_down[1],
                    ln_g[1, 1], ln_b[1, 1])
    y_prompt = x4[:geo.t_prompt].reshape(bp, sp, D)
    y_sample = x4[geo.t_prompt:].reshape(bs, ss, D)
    return y_prompt, y_sample, new_c[:, None], new_n[:, None], new_m[:, None], new_s[:, None]
```

```python
import functools

import jax
import jax.numpy as jnp
from jax import lax
from jax.experimental import pallas as pl
from jax.experimental.pallas import tpu as pltpu
from jax.experimental.pallas import tpu_sc as plsc

F32 = jnp.float32
BF16 = jnp.bfloat16
HIGHEST = lax.Precision.HIGHEST

D = 1024
DEPTH = 2
GRID_W = 64
POS_BASE = 10000.0
EPS = 1e-6
ALPHA = (2.0 * DEPTH) ** 0.25
NH_A, DK_A, DV_A = 4, 128, 256
QK_A, V_A = NH_A * DK_A, NH_A * DV_A
NH_B, DK_B = 8, 128
N_EXPERTS, TOP_K, N_GROUPS, TOPK_GROUPS = 64, 8, 8, 4
GROUP_SIZE = N_EXPERTS // N_GROUPS
D_EXPERT = D // 4
ROUTED_SCALE = 2.5

LANES = 128
SUBLANES = 8
COND_ROWS = 8
TOK_TILE = 256
CHUNK_A = 256
VMEM_LIMIT = 56 * 1024 * 1024

NT_DIMS = (((1,), (1,)), ((), ()))


def _params(sem):
    return pltpu.CompilerParams(dimension_semantics=sem, vmem_limit_bytes=VMEM_LIMIT)


def _split3(x):
    hi = x.astype(BF16)
    r = x - hi.astype(F32)
    mid = r.astype(BF16)
    lo = (r - mid.astype(F32)).astype(BF16)
    return hi, mid, lo


def _dot3(a_bf, x, transpose_side=None):
    hi, mid, lo = _split3(x)
    return (jnp.dot(a_bf, hi, preferred_element_type=F32)
            + jnp.dot(a_bf, mid, preferred_element_type=F32)
            + jnp.dot(a_bf, lo, preferred_element_type=F32))


def _dot3_r(x, a_bf):
    hi, mid, lo = _split3(x)
    return (jnp.dot(hi, a_bf, preferred_element_type=F32)
            + jnp.dot(mid, a_bf, preferred_element_type=F32)
            + jnp.dot(lo, a_bf, preferred_element_type=F32))


def _log_sigmoid(x):
    return jnp.minimum(x, 0.0) - jnp.log1p(jnp.exp(-jnp.abs(x)))


def _layer_norm_rows(x, g, b):
    mu = jnp.mean(x, axis=-1, keepdims=True)
    xc = x - mu
    var = jnp.mean(xc * xc, axis=-1, keepdims=True)
    return xc * lax.rsqrt(var + EPS) * g + b


class Geometry:
    def __init__(self, n_prompt, prompt_len, n_sample, sample_len):
        self.n_prompt, self.prompt_len = n_prompt, prompt_len
        self.n_sample, self.sample_len = n_sample, sample_len
        self.t_prompt = n_prompt * prompt_len
        self.t_sample = n_sample * sample_len
        self.t = self.t_prompt + self.t_sample
        assert prompt_len % TOK_TILE == 0 and sample_len % TOK_TILE == 0
        assert n_sample + 1 <= COND_ROWS

    def cond_row(self, tile, tile_rows):
        n_p = self.t_prompt // tile_rows
        return jnp.where(tile < n_p, 0, 1 + (tile - n_p) // (self.sample_len // tile_rows))


def _mod_kernel(cond_ref, w_ref, b_ref, o_ref):
    c = cond_ref[...]
    s = c * jax.nn.sigmoid(c)
    o_ref[0, 0] = jnp.dot(s, w_ref[0], precision=HIGHEST, preferred_element_type=F32) + b_ref[0, 0]


def _modulation(cond, w_mod, b_mod):
    out = pl.pallas_call(
        _mod_kernel,
        name="modulation",
        grid=(DEPTH, 6),
        in_specs=[
            pl.BlockSpec((COND_ROWS, D), lambda l, j: (0, 0)),
            pl.BlockSpec((1, D, D), lambda l, j: (l, 0, j)),
            pl.BlockSpec((1, 1, 1, D), lambda l, j: (l, j, 0, 0)),
        ],
        out_specs=pl.BlockSpec((1, 1, COND_ROWS, D), lambda l, j: (l, j, 0, 0)),
        out_shape=jax.ShapeDtypeStruct((DEPTH, 6, COND_ROWS, D), F32),
        compiler_params=_params(("arbitrary", "arbitrary")),
    )(cond, w_mod, b_mod.reshape(DEPTH, 6, 1, D))
    return out.transpose(0, 2, 1, 3)


def _proj_a_kernel(x_ref, mod_ref, wq_ref, wkt_ref, wvo_ref, wg_ref, wgt_ref, bg_ref, bgt_ref,
                   q_ref, kt_ref, v_ref, so_ref, gc_ref, gr_ref):
    h = x_ref[...] * (1.0 + mod_ref[1:2, :]) + mod_ref[0:1, :]
    hb = h.astype(BF16)
    q_ref[...] = jnp.dot(hb, wq_ref[...], preferred_element_type=F32).astype(BF16)
    kt = lax.dot_general(wkt_ref[...], hb, NT_DIMS, preferred_element_type=F32)
    kt_ref[...] = (kt * (DK_A ** -0.5)).astype(BF16)
    vo = jnp.dot(hb, wvo_ref[...], preferred_element_type=F32)
    v_ref[...] = vo[:, :V_A].astype(BF16)
    so_ref[...] = jax.nn.sigmoid(vo[:, V_A:]).astype(BF16)
    gc_ref[...] = jnp.dot(h, wg_ref[...], precision=HIGHEST, preferred_element_type=F32) + bg_ref[...]
    gr_ref[...] = lax.dot_general(wgt_ref[...], h, NT_DIMS, precision=HIGHEST,
                                  preferred_element_type=F32) + bgt_ref[...]


def _proj_a(geo, x, mod_l, w_in, b_gates):
    t = geo.t
    n_gate = 4 * NH_A
    wq = w_in[:, :QK_A].astype(BF16)
    wkt = w_in[:, QK_A:2 * QK_A].T.astype(BF16)
    wvo = w_in[:, 2 * QK_A:2 * QK_A + 2 * V_A].astype(BF16)
    wg = w_in[:, 2 * QK_A + 2 * V_A:]
    wg_pad = jnp.pad(wg, ((0, 0), (0, LANES - n_gate)))
    bg = b_gates.reshape(n_gate).astype(F32)
    bg_pad = jnp.pad(bg, (0, LANES - n_gate)).reshape(1, LANES)
    tb = TOK_TILE
    full = lambda shape: pl.BlockSpec(shape, lambda i: (0,) * len(shape))
    return pl.pallas_call(
        _proj_a_kernel,
        name="proj_a",
        grid=(t // tb,),
        in_specs=[
            pl.BlockSpec((tb, D), lambda i: (i, 0)),
            pl.BlockSpec((None, 6, D), lambda i: (geo.cond_row(i, tb), 0, 0)),
            full((D, QK_A)), full((QK_A, D)), full((D, 2 * V_A)), full((D, LANES)), full((n_gate, D)),
            full((1, LANES)), full((n_gate, 1)),
        ],
        out_specs=[
            pl.BlockSpec((tb, QK_A), lambda i: (i, 0)),
            pl.BlockSpec((QK_A, tb), lambda i: (0, i)),
            pl.BlockSpec((tb, V_A), lambda i: (i, 0)),
            pl.BlockSpec((tb, V_A), lambda i: (i, 0)),
            pl.BlockSpec((tb, LANES), lambda i: (i, 0)),
            pl.BlockSpec((n_gate, tb), lambda i: (0, i)),
        ],
        out_shape=[
            jax.ShapeDtypeStruct((t, QK_A), BF16),
            jax.ShapeDtypeStruct((QK_A, t), BF16),
            jax.ShapeDtypeStruct((t, V_A), BF16),
            jax.ShapeDtypeStruct((t, V_A), BF16),
            jax.ShapeDtypeStruct((t, LANES), F32),
            jax.ShapeDtypeStruct((n_gate, t), F32),
        ],
        compiler_params=_params(("parallel",)),
    )(x, mod_l, wq, wkt, wvo, wg_pad, wg.T, bg_pad, bg.reshape(n_gate, 1))


def _mlstm_scan_kernel(*refs, chunk, has_state):
    if has_state:
        (q_ref, kt_ref, v_ref, gc_ref, gr_ref, c0_ref, n0_ref, m0_ref,
         h_ref, c_out, n_out, m_out, c_sc, n_sc, m_sc) = refs
    else:
        (q_ref, kt_ref, v_ref, gc_ref, gr_ref,
         h_ref, c_out, n_out, m_out, c_sc, n_sc, m_sc) = refs
    L = chunk
    d = pl.program_id(1)
    c = pl.program_id(2)
    fwd = d == 0

    @pl.when(c == 0)
    def _():
        if has_state:
            c_sc[...] = c0_ref[0, 0]
            n_sc[...] = n0_ref[0, 0]
            m_sc[...] = m0_ref[0, 0]
        else:
            c_sc[...] = jnp.zeros_like(c_sc)
            n_sc[...] = jnp.zeros_like(n_sc)
            m_sc[...] = jnp.zeros_like(m_sc)

    row = lax.broadcasted_iota(jnp.int32, (L, L), 0)
    col = lax.broadcasted_iota(jnp.int32, (L, L), 1)
    sgn = 1 - 2 * d
    causal = (row - col) * sgn >= 0
    tri = causal.astype(BF16)
    tri_t = ((col - row) * sgn >= 0).astype(BF16)

    gc = gc_ref[...]
    gr = gr_ref[...]
    bc_all = _dot3(tri, _log_sigmoid(gc))
    br_all = _dot3_r(_log_sigmoid(gr), tri_t)
    ones_blk = (lax.broadcasted_iota(jnp.int32, (L, LANES), 1) == 0).astype(BF16)

    for h in range(NH_A):
        b_c = jnp.where(fwd, bc_all[:, 4 + h:5 + h], bc_all[:, 12 + h:13 + h])
        b_r = jnp.where(fwd, br_all[4 + h:5 + h, :], br_all[12 + h:13 + h, :])
        i_r = jnp.where(fwd, gr[h:h + 1, :], gr[8 + h:9 + h, :])
        bl = jnp.where(fwd, b_r[:, L - 1:L], b_r[:, 0:1])
        q = q_ref[:, h * DK_A:(h + 1) * DK_A]
        kt = kt_ref[h * DK_A:(h + 1) * DK_A, :]
        v = v_ref[:, h * DV_A:(h + 1) * DV_A]
        m = m_sc[h:h + 1, 0:1]
        cst = c_sc[h]
        nst = n_sc[h]

        a_r = i_r - b_r
        logd = jnp.where(causal, b_c + a_r, -jnp.inf)
        inter = b_c + m
        m_t = jnp.maximum(inter, jnp.max(logd, axis=1, keepdims=True))
        dmat = jnp.exp(logd - m_t)
        e_int = jnp.exp(inter - m_t)
        s = (jnp.dot(q, kt, preferred_element_type=F32) * dmat).astype(BF16)
        num = (jnp.dot(s, v, preferred_element_type=F32)
               + e_int * jnp.dot(q, cst.astype(BF16), preferred_element_type=F32))
        den = (jnp.dot(s, ones_blk, preferred_element_type=F32)
               + e_int * jnp.dot(q, nst.astype(BF16), preferred_element_type=F32))[:, 0:1]
        h_ref[:, h * DV_A:(h + 1) * DV_A] = num / jnp.maximum(jnp.abs(den), jnp.exp(-m_t))

        logw = bl + a_r
        m_new = jnp.maximum(bl + m, jnp.max(logw, axis=1, keepdims=True))
        w = jnp.exp(logw - m_new)
        decay = jnp.exp(bl + m - m_new)
        kw = (kt.astype(F32) * w).astype(BF16)
        c_sc[h] = decay * cst + jnp.dot(kw, v, preferred_element_type=F32)
        n_sc[h] = decay * nst + jnp.dot(kw, ones_blk, preferred_element_type=F32)
        m_sc[h:h + 1, :] = jnp.broadcast_to(m_new, (1, LANES))

    @pl.when(c == pl.num_programs(2) - 1)
    def _():
        c_out[0, 0] = c_sc[...]
        n_out[0, 0] = n_sc[...]
        m_out[0, 0] = m_sc[...]


def _mlstm_scan(q, kt, v, gc, gr, *, row0, n_seq, seq_len, state=None):
    L = CHUNK_A
    nc = seq_len // L
    blk0 = row0 // L

    def loc_blk(b, d, c):
        return b * nc + c + d * (nc - 1 - 2 * c)

    def tok_blk(b, d, c):
        return blk0 + loc_blk(b, d, c)

    in_specs = [
        pl.BlockSpec((L, QK_A), lambda b, d, c: (tok_blk(b, d, c), 0)),
        pl.BlockSpec((QK_A, L), lambda b, d, c: (0, tok_blk(b, d, c))),
        pl.BlockSpec((L, V_A), lambda b, d, c: (tok_blk(b, d, c), 0)),
        pl.BlockSpec((L, LANES), lambda b, d, c: (tok_blk(b, d, c), 0)),
        pl.BlockSpec((4 * NH_A, L), lambda b, d, c: (0, tok_blk(b, d, c))),
    ]
    args = [q, kt, v, gc, gr]
    if state is not None:
        in_specs += [
            pl.BlockSpec((1, 1, NH_A, DK_A, DV_A), lambda b, d, c: (b, d, 0, 0, 0)),
            pl.BlockSpec((1, 1, NH_A, DK_A, LANES), lambda b, d, c: (b, d, 0, 0, 0)),
            pl.BlockSpec((1, 1, SUBLANES, LANES), lambda b, d, c: (b, d, 0, 0)),
        ]
        args += list(state)
    return pl.pallas_call(
        functools.partial(_mlstm_scan_kernel, chunk=L, has_state=state is not None),
        name="mlstm_scan_seeded" if state is not None else "mlstm_scan",
        grid=(n_seq, 2, nc),
        in_specs=in_specs,
        out_specs=[
            pl.BlockSpec((None, L, V_A), lambda b, d, c: (d, loc_blk(b, d, c), 0)),
            pl.BlockSpec((1, 1, NH_A, DK_A, DV_A), lambda b, d, c: (b, d, 0, 0, 0)),
            pl.BlockSpec((1, 1, NH_A, DK_A, LANES), lambda b, d, c: (b, d, 0, 0, 0)),
            pl.BlockSpec((1, 1, SUBLANES, LANES), lambda b, d, c: (b, d, 0, 0)),
        ],
        out_shape=[
            jax.ShapeDtypeStruct((2, n_seq * seq_len, V_A), F32),
            jax.ShapeDtypeStruct((n_seq, 2, NH_A, DK_A, DV_A), F32),
            jax.ShapeDtypeStruct((n_seq, 2, NH_A, DK_A, LANES), F32),
            jax.ShapeDtypeStruct((n_seq, 2, SUBLANES, LANES), F32),
        ],
        scratch_shapes=[
            pltpu.VMEM((NH_A, DK_A, DV_A), F32),
            pltpu.VMEM((NH_A, DK_A, LANES), F32),
            pltpu.VMEM((SUBLANES, LANES), F32),
        ],
        compiler_params=_params(("parallel", "parallel", "arbitrary")),
    )(*args)


def _out_a_kernel(hp_ref, hs_ref, so_ref, nw_ref, w_ref, x_ref, mod_ref, lg_ref, lb_ref, o_ref, *, n_prompt_tiles):
    is_prompt = pl.program_id(0) < n_prompt_tiles
    y = jnp.where(is_prompt, hp_ref[0] + hp_ref[1], hs_ref[0] + hs_ref[1])
    parts = []
    for h in range(NH_A):
        yh = y[:, h * DV_A:(h + 1) * DV_A]
        mu = jnp.mean(yh, axis=-1, keepdims=True)
        yc = yh - mu
        var = jnp.mean(yc * yc, axis=-1, keepdims=True)
        parts.append(yc * lax.rsqrt(var + EPS))
    yn = jnp.concatenate(parts, axis=-1) * nw_ref[...] * so_ref[...].astype(F32)
    out = jnp.dot(yn.astype(BF16), w_ref[...], preferred_element_type=F32)
    o_ref[...] = _layer_norm_rows(ALPHA * x_ref[...] + mod_ref[2:3, :] * out, lg_ref[...], lb_ref[...])


def _out_a(geo, h_prompt, h_sample, so, norm_w, w_out, x, mod_l, ln_g, ln_b):
    t = geo.t
    tb = TOK_TILE
    n_p = geo.t_prompt // tb
    full = lambda shape: pl.BlockSpec(shape, lambda i: (0,) * len(shape))
    return pl.pallas_call(
        functools.partial(_out_a_kernel, n_prompt_tiles=n_p),
        name="out_a",
        grid=(t // tb,),
        in_specs=[
            pl.BlockSpec((2, tb, V_A), lambda i: (0, jnp.minimum(i, n_p - 1), 0)),
            pl.BlockSpec((2, tb, V_A), lambda i: (0, jnp.maximum(i - n_p, 0), 0)),
            pl.BlockSpec((tb, V_A), lambda i: (i, 0)),
            full((1, V_A)), full((V_A, D)),
            pl.BlockSpec((tb, D), lambda i: (i, 0)),
            pl.BlockSpec((None, 6, D), lambda i: (geo.cond_row(i, tb), 0, 0)),
            full((1, D)), full((1, D)),
        ],
        out_specs=pl.BlockSpec((tb, D), lambda i: (i, 0)),
        out_shape=jax.ShapeDtypeStruct((t, D), F32),
        compiler_params=_params(("parallel",)),
    )(h_prompt, h_sample, so, norm_w.reshape(1, V_A).astype(F32), w_out.astype(BF16), x, mod_l,
      ln_g.reshape(1, D), ln_b.reshape(1, D))


def _proj_b_kernel(x_ref, mod_ref, w_ref, q_ref, pre_ref, v_ref, sg_ref):
    h = x_ref[...] * (1.0 + mod_ref[1:2, :]) + mod_ref[0:1, :]
    z = jnp.dot(h.astype(BF16), w_ref[...], preferred_element_type=F32)
    for hd in range(NH_B):
        lo = hd * DK_B
        qh = z[:, lo:lo + DK_B]
        q_ref[hd] = qh * jax.nn.sigmoid(qh)
        pre_ref[0, hd] = z[:, D + lo:D + lo + DK_B]
        pre_ref[1, hd] = z[:, 2 * D + lo:2 * D + lo + DK_B]
        v_ref[hd] = z[:, 3 * D + lo:3 * D + lo + DK_B].astype(BF16)
    g = z[:, 4 * D:]
    sg_ref[...] = (g * jax.nn.sigmoid(g)).astype(BF16)


def _proj_b(geo, x, mod_l, w_in):
    t = geo.t
    tb = TOK_TILE
    return pl.pallas_call(
        _proj_b_kernel,
        name="proj_b",
        grid=(t // tb,),
        in_specs=[
            pl.BlockSpec((tb, D), lambda i: (i, 0)),
            pl.BlockSpec((None, 6, D), lambda i: (geo.cond_row(i, tb), 0, 0)),
            pl.BlockSpec((D, 5 * D), lambda i: (0, 0)),
        ],
        out_specs=[
            pl.BlockSpec((NH_B, tb, DK_B), lambda i: (0, i, 0)),
            pl.BlockSpec((2, NH_B, tb, DK_B), lambda i: (0, 0, i, 0)),
            pl.BlockSpec((NH_B, tb, DK_B), lambda i: (0, i, 0)),
            pl.BlockSpec((tb, D), lambda i: (i, 0)),
        ],
        out_shape=[
            jax.ShapeDtypeStruct((NH_B, t, DK_B), F32),
            jax.ShapeDtypeStruct((2, NH_B, t, DK_B), F32),
            jax.ShapeDtypeStruct((NH_B, t, DK_B), BF16),
            jax.ShapeDtypeStruct((t, D), BF16),
        ],
        compiler_params=_params(("parallel",)),
    )(x, mod_l, w_in.astype(BF16))


CHUNK_B = 128
BAND = SUBLANES // 2
TN_DIMS = (((0,), (0,)), ((), ()))


def _hgrn_head(q, pre, lbv, v_bf, st, fwd):
    L = q.shape[0]
    f = lbv + (1.0 - lbv) * jax.nn.sigmoid(pre)
    lf = jnp.log(f)
    kk = (1.0 - lbv) * jax.nn.sigmoid(-pre)
    row = lax.broadcasted_iota(jnp.int32, (L, L), 0)
    col = lax.broadcasted_iota(jnp.int32, (L, L), 1)
    tri = ((row >= col) if fwd else (row <= col)).astype(BF16)
    b = _dot3(tri, lf)
    v32 = v_bf.astype(F32)
    tpos = lax.broadcasted_iota(jnp.int32, (L, DK_B), 0)

    o = jnp.sum(q * kk, axis=1, keepdims=True) * v32
    for dl in range(1, BAND):
        shift = dl if fwd else L - dl
        in_blk = (tpos % BAND >= dl) if fwd else (tpos % BAND + dl < BAND)
        e = jnp.exp(jnp.where(in_blk, b - pltpu.roll(b, shift, 0), -jnp.inf))
        a = jnp.sum(q * pltpu.roll(kk, shift, 0) * e, axis=1, keepdims=True)
        o = o + a * pltpu.roll(v32, shift, 0)

    att = jnp.zeros((L, L), F32)
    w = BAND
    while w < L:
        nb = L // (2 * w)
        b3 = b.reshape(nb, 2 * w, DK_B)
        edge = (b3[:, w - 1:w, :] if fwd else b3[:, w:w + 1, :])
        bmid = jnp.broadcast_to(edge, (nb, 2 * w, DK_B)).reshape(L, DK_B)
        second = (tpos // w) % 2 == 1
        t_side = second if fwd else jnp.logical_not(second)
        e = jnp.exp(jnp.where(t_side, b - bmid, bmid - b))
        qt = jnp.where(t_side, q * e, 0.0).astype(BF16)
        ks = jnp.where(t_side, 0.0, kk * e).astype(BF16)
        a = lax.dot_general(qt, ks, NT_DIMS, preferred_element_type=F32)
        att = att + jnp.where(row // (2 * w) == col // (2 * w), a, 0.0)
        w *= 2
    o = o + jnp.dot(att.astype(BF16), v_bf, preferred_element_type=F32)

    bl = b[L - 1:L, :] if fwd else b[0:1, :]
    o = o + lax.dot_general((q * jnp.exp(b)).astype(BF16), st.astype(BF16), NT_DIMS, preferred_element_type=F32)
    kd = (kk * jnp.exp(bl - b)).astype(BF16)
    st_new = jnp.exp(bl) * st + lax.dot_general(v_bf, kd, TN_DIMS, preferred_element_type=F32)
    return o, st_new


def _hgrn_scan_kernel(*refs, has_state):
    if has_state:
        q_ref, pre_ref, v_ref, lb_ref, s0_ref, o_ref, s_out, st_sc = refs
    else:
        q_ref, pre_ref, v_ref, lb_ref, o_ref, s_out, st_sc = refs
    d = pl.program_id(1)
    c = pl.program_id(2)

    @pl.when(c == 0)
    def _():
        if has_state:
            for hd in range(NH_B):
                st_sc[hd] = s0_ref[0, 0, hd].T
        else:
            st_sc[...] = jnp.zeros_like(st_sc)

    def run(fwd):
        def head(hd, carry):
            o, st_new = _hgrn_head(q_ref[hd], pre_ref[hd], lb_ref[hd], v_ref[hd], st_sc[hd], fwd)
            o_ref[hd] = o
            st_sc[hd] = st_new
            return carry
        lax.fori_loop(0, NH_B, head, 0, unroll=8)

    @pl.when(d == 0)
    def _():
        run(True)

    @pl.when(d == 1)
    def _():
        run(False)

    @pl.when(c == pl.num_programs(2) - 1)
    def _():
        for hd in range(NH_B):
            s_out[0, 0, hd] = st_sc[hd].T


def _hgrn_scan(q, pre, v, lbd, *, row0, n_seq, seq_len, state=None):
    L = CHUNK_B
    nc = seq_len // L
    blk0 = row0 // L

    def loc_blk(b, d, c):
        return b * nc + c + d * (nc - 1 - 2 * c)

    def tok_blk(b, d, c):
        return blk0 + loc_blk(b, d, c)

    in_specs = [
        pl.BlockSpec((NH_B, L, DK_B), lambda b, d, c: (0, tok_blk(b, d, c), 0)),
        pl.BlockSpec((None, NH_B, L, DK_B), lambda b, d, c: (d, 0, tok_blk(b, d, c), 0)),
        pl.BlockSpec((NH_B, L, DK_B), lambda b, d, c: (0, tok_blk(b, d, c), 0)),
        pl.BlockSpec((None, NH_B, 1, DK_B), lambda b, d, c: (d, 0, 0, 0)),
    ]
    args = [q, pre, v, lbd]
    if state is not None:
        in_specs.append(pl.BlockSpec((1, 1, NH_B, DK_B, DK_B), lambda b, d, c: (b, d, 0, 0, 0)))
        args.append(state)
    return pl.pallas_call(
        functools.partial(_hgrn_scan_kernel, has_state=state is not None),
        name="hgrn_scan_seeded" if state is not None else "hgrn_scan",
        grid=(n_seq, 2, nc),
        in_specs=in_specs,
        out_specs=[
            pl.BlockSpec((None, NH_B, L, DK_B), lambda b, d, c: (d, 0, loc_blk(b, d, c), 0)),
            pl.BlockSpec((1, 1, NH_B, DK_B, DK_B), lambda b, d, c: (b, d, 0, 0, 0)),
        ],
        out_shape=[
            jax.ShapeDtypeStruct((2, NH_B, n_seq * seq_len, DK_B), F32),
            jax.ShapeDtypeStruct((n_seq, 2, NH_B, DK_B, DK_B), F32),
        ],
        scratch_shapes=[pltpu.VMEM((NH_B, DK_B, DK_B), F32)],
        compiler_params=_params(("parallel", "parallel", "arbitrary")),
    )(*args)


def _out_b_kernel(op_ref, os_ref, sg_ref, nw_ref, w_ref, x_ref, mod_ref, lg_ref, lb_ref, out_ref, *, n_prompt_tiles):
    is_prompt = pl.program_id(0) < n_prompt_tiles
    parts = []
    for hd in range(NH_B):
        y = jnp.where(is_prompt, op_ref[0, hd] + op_ref[1, hd], os_ref[0, hd] + os_ref[1, hd])
        parts.append(y * lax.rsqrt(jnp.mean(y * y, axis=-1, keepdims=True) + EPS))
    yn = jnp.concatenate(parts, axis=-1) * nw_ref[...] * sg_ref[...].astype(F32)
    out = jnp.dot(yn.astype(BF16), w_ref[...], preferred_element_type=F32)
    out_ref[...] = _layer_norm_rows(ALPHA * x_ref[...] + mod_ref[2:3, :] * out, lg_ref[...], lb_ref[...])


def _out_b(geo, o_prompt, o_sample, sg, norm_w, w_out, x, mod_l, ln_g, ln_b):
    t = geo.t
    tb = TOK_TILE
    n_p = geo.t_prompt // tb
    full = lambda shape: pl.BlockSpec(shape, lambda i: (0,) * len(shape))
    return pl.pallas_call(
        functools.partial(_out_b_kernel, n_prompt_tiles=n_p),
        name="out_b",
        grid=(t // tb,),
        in_specs=[
            pl.BlockSpec((2, NH_B, tb, DK_B), lambda i: (0, 0, jnp.minimum(i, n_p - 1), 0)),
            pl.BlockSpec((2, NH_B, tb, DK_B), lambda i: (0, 0, jnp.maximum(i - n_p, 0), 0)),
            pl.BlockSpec((tb, D), lambda i: (i, 0)),
            full((1, D)), full((D, D)),
            pl.BlockSpec((tb, D), lambda i: (i, 0)),
            pl.BlockSpec((None, 6, D), lambda i: (geo.cond_row(i, tb), 0, 0)),
            full((1, D)), full((1, D)),
        ],
        out_specs=pl.BlockSpec((tb, D), lambda i: (i, 0)),
        out_shape=jax.ShapeDtypeStruct((t, D), F32),
        compiler_params=_params(("parallel",)),
    )(o_prompt, o_sample, sg, norm_w.reshape(1, D).astype(F32), w_out.astype(BF16), x, mod_l,
      ln_g.reshape(1, D), ln_b.reshape(1, D))


MOE_BLK = 512
CHUNK_W = 256
ROW_CHUNKS = D // CHUNK_W
SC_WINDOW = 128


def _first_index(hit, iota, size, axis):
    return jnp.min(jnp.where(hit, iota, size), axis=axis, keepdims=True)


def _router_kernel(x_ref, mod_ref, wrt_ref, eb_ref, e_ref, w_ref, r_ref, cnt_ref, h_ref, cnt_sc):
    i = pl.program_id(0)
    tb = x_ref.shape[0]

    @pl.when(i == 0)
    def _():
        cnt_sc[...] = jnp.zeros_like(cnt_sc)

    h = x_ref[...] * (1.0 + mod_ref[4:5, :]) + mod_ref[3:4, :]
    for c in range(ROW_CHUNKS):
        h_ref[c] = h[:, c * CHUNK_W:(c + 1) * CHUNK_W]
    logits = lax.dot_general(wrt_ref[...], h, NT_DIMS, precision=HIGHEST, preferred_element_type=F32)
    scores = jax.nn.sigmoid(logits)
    sel = scores + eb_ref[...]

    g3 = sel.reshape(N_GROUPS, GROUP_SIZE, tb)
    io3 = lax.broadcasted_iota(jnp.int32, g3.shape, 1)
    m1 = jnp.max(g3, axis=1, keepdims=True)
    first = _first_index(g3 == m1, io3, GROUP_SIZE, 1)
    m2 = jnp.max(jnp.where(io3 == first, -jnp.inf, g3), axis=1, keepdims=True)
    gscore = (m1 + m2).reshape(N_GROUPS, tb)

    iog = lax.broadcasted_iota(jnp.int32, gscore.shape, 0)
    gmask = jnp.zeros(gscore.shape, F32)
    for _ in range(TOPK_GROUPS):
        gm = jnp.max(gscore, axis=0, keepdims=True)
        pick = iog == _first_index(gscore == gm, iog, N_GROUPS, 0)
        gmask = jnp.where(pick, 1.0, gmask)
        gscore = jnp.where(pick, -jnp.inf, gscore)
    emask = jnp.broadcast_to(gmask.reshape(N_GROUPS, 1, tb), (N_GROUPS, GROUP_SIZE, tb)).reshape(N_EXPERTS, tb)
    cand = jnp.where(emask > 0.0, sel, -jnp.inf)

    ioe = lax.broadcasted_iota(jnp.int32, cand.shape, 0)
    picks, wts = [], []
    onehot = jnp.zeros(cand.shape, F32)
    for _ in range(TOP_K):
        cm = jnp.max(cand, axis=0, keepdims=True)
        idx = _first_index(cand == cm, ioe, N_EXPERTS, 0)
        pick = ioe == idx
        picks.append(pick)
        wts.append(jnp.sum(jnp.where(pick, scores, 0.0), axis=0, keepdims=True))
        onehot = onehot + pick.astype(F32)
        cand = jnp.where(pick, -jnp.inf, cand)
        e_ref[pl.ds(len(picks) - 1, 1), :] = idx
    wsum = wts[0]
    for wk in wts[1:]:
        wsum = wsum + wk
    for k in range(TOP_K):
        w_ref[pl.ds(k, 1), :] = wts[k] / wsum * ROUTED_SCALE

    r_io = lax.broadcasted_iota(jnp.int32, (tb, tb), 0)
    c_io = lax.broadcasted_iota(jnp.int32, (tb, tb), 1)
    before = (r_io < c_io).astype(BF16)
    rank = cnt_sc[:, 0:1] + jnp.dot(onehot.astype(BF16), before, preferred_element_type=F32)
    for k in range(TOP_K):
        r_ref[pl.ds(k, 1), :] = jnp.sum(jnp.where(picks[k], rank, 0.0), axis=0, keepdims=True).astype(jnp.int32)
    cnt_sc[...] = cnt_sc[...] + jnp.sum(onehot, axis=1, keepdims=True)
    cnt_ref[...] = cnt_sc[...]


def _router(geo, x, mod_l, w_router, e_bias):
    t = geo.t
    tb = TOK_TILE
    full = lambda shape: pl.BlockSpec(shape, lambda i: (0,) * len(shape))
    e, w, r, cnt, h = pl.pallas_call(
        _router_kernel,
        name="router",
        grid=(t // tb,),
        in_specs=[
            pl.BlockSpec((tb, D), lambda i: (i, 0)),
            pl.BlockSpec((None, 6, D), lambda i: (geo.cond_row(i, tb), 0, 0)),
            full((N_EXPERTS, D)), full((N_EXPERTS, 1)),
        ],
        out_specs=[
            pl.BlockSpec((TOP_K, tb), lambda i: (0, i)),
            pl.BlockSpec((TOP_K, tb), lambda i: (0, i)),
            pl.BlockSpec((TOP_K, tb), lambda i: (0, i)),
            full((N_EXPERTS, LANES)),
            pl.BlockSpec((ROW_CHUNKS, tb, CHUNK_W), lambda i: (0, i, 0)),
        ],
        out_shape=[
            jax.ShapeDtypeStruct((TOP_K, t), jnp.int32),
            jax.ShapeDtypeStruct((TOP_K, t), F32),
            jax.ShapeDtypeStruct((TOP_K, t), jnp.int32),
            jax.ShapeDtypeStruct((N_EXPERTS, LANES), F32),
            jax.ShapeDtypeStruct((ROW_CHUNKS, t, CHUNK_W), F32),
        ],
        scratch_shapes=[pltpu.VMEM((N_EXPERTS, LANES), F32)],
        compiler_params=_params(("arbitrary",)),
    )(x, mod_l, w_router.T.astype(F32), e_bias.reshape(N_EXPERTS, 1).astype(F32))
    return e, w, r, cnt[:, 0].astype(jnp.int32), h


def _slot_kernel(pstart_ref, e_ref, r_ref, o_ref):
    e = e_ref[...]
    slot = r_ref[...]
    for x in range(N_EXPERTS):
        slot = slot + jnp.where(e == x, pstart_ref[x], 0)
    o_ref[...] = slot


def _slots(geo, pstart, top_e, rank):
    tb = TOK_TILE
    return pl.pallas_call(
        _slot_kernel,
        name="slots",
        grid_spec=pltpu.PrefetchScalarGridSpec(
            num_scalar_prefetch=1,
            grid=(geo.t // tb,),
            in_specs=[pl.BlockSpec((TOP_K, tb), lambda i, p: (0, i)),
                      pl.BlockSpec((TOP_K, tb), lambda i, p: (0, i))],
            out_specs=pl.BlockSpec((TOP_K, tb), lambda i, p: (0, i)),
        ),
        out_shape=jax.ShapeDtypeStruct((TOP_K, geo.t), jnp.int32),
        compiler_params=_params(("parallel",)),
    )(pstart, top_e, rank)


def _sc_mesh():
    return plsc.VectorSubcoreMesh(core_axis_name="core", subcore_axis_name="subcore")


def _sc_scatter(rows, idx, n_out, src_block):
    n_idx = idx.shape[0]

    @pl.kernel(out_type=jax.ShapeDtypeStruct((n_out, CHUNK_W), rows.dtype), mesh=_sc_mesh(), scratch_types=[],
               name="sc_dispatch")
    def scatter(x_hbm, i_hbm, o_hbm):
        def body(x_vmem, i_vmem):
            pltpu.sync_copy(x_vmem, o_hbm.at[i_vmem.at[0]])

        pltpu.emit_pipeline(
            body,
            grid=(n_idx // SC_WINDOW,),
            in_specs=[pl.BlockSpec((SC_WINDOW, CHUNK_W), index_map=lambda w: (src_block(w), 0)),
                      pl.BlockSpec((1, SC_WINDOW), index_map=lambda w: (0, w))],
            out_specs=[],
            core_axis_name=("core", "subcore"),
            dimension_semantics=(pltpu.PARALLEL,),
        )(x_hbm, i_hbm)

    return scatter(rows, idx.reshape(1, n_idx))


def _sc_gather(table, idx):
    n_idx = idx.shape[0]

    @pl.kernel(out_type=jax.ShapeDtypeStruct((n_idx, CHUNK_W), table.dtype), mesh=_sc_mesh(),
               name="sc_combine_gather")
    def gather(t_hbm, i_hbm, o_hbm):
        def body(i_vmem, o_vmem):
            pltpu.sync_copy(t_hbm.at[i_vmem.at[0]], o_vmem)

        pltpu.emit_pipeline(
            body,
            grid=(n_idx // SC_WINDOW,),
            in_specs=[pl.BlockSpec((1, SC_WINDOW), index_map=lambda w: (0, w))],
            out_specs=[pl.BlockSpec((SC_WINDOW, CHUNK_W), index_map=lambda w: (w, 0))],
            core_axis_name=("core", "subcore"),
            dimension_semantics=(pltpu.PARALLEL,),
        )(i_hbm, o_hbm)

    return gather(table, idx.reshape(1, n_idx))


def _ffn_kernel(blk_e_ref, blk_valid_ref, n_used_ref, xs_ref, wg_ref, wu_ref, wd_ref, y_ref, wg_sc, wu_sc, wd_sc):
    b = pl.program_id(0)
    used = b < n_used_ref[0]
    new_expert = (b == 0) | (blk_e_ref[b] != blk_e_ref[jnp.maximum(b - 1, 0)])

    @pl.when(used & new_expert)
    def _():
        wg_sc[...] = wg_ref[...].astype(BF16)
        wu_sc[...] = wu_ref[...].astype(BF16)
        wd_sc[...] = wd_ref[...].astype(BF16)

    @pl.when(used)
    def _():
        x = jnp.concatenate([xs_ref[c] for c in range(ROW_CHUNKS)], axis=1)
        row = lax.broadcasted_iota(jnp.int32, (MOE_BLK, 1), 0)
        x = jnp.where(row < blk_valid_ref[b], x, 0.0).astype(BF16)
        g = jnp.dot(x, wg_sc[...], preferred_element_type=F32)
        u = jnp.dot(x, wu_sc[...], preferred_element_type=F32)
        hmid = (g * jax.nn.sigmoid(g) * u).astype(BF16)
        y = jnp.dot(hmid, wd_sc[...], preferred_element_type=F32)
        for c in range(ROW_CHUNKS):
            y_ref[c] = y[:, c * CHUNK_W:(c + 1) * CHUNK_W]

    @pl.when(jnp.logical_not(used))
    def _():
        y_ref[...] = jnp.zeros_like(y_ref)


def _ffn(xs, blk_e, blk_valid, n_used, layer, wg, wu, wd, n_blocks):
    def blk(b, be, bv, nu):
        return jnp.maximum(jnp.minimum(b, nu[0] - 1), 0)

    def w_idx(b, be, bv, nu):
        return (layer, be[blk(b, be, bv, nu)], 0, 0)

    return pl.pallas_call(
        _ffn_kernel,
        name="expert_ffn",
        grid_spec=pltpu.PrefetchScalarGridSpec(
            num_scalar_prefetch=3,
            grid=(n_blocks,),
            in_specs=[
                pl.BlockSpec((ROW_CHUNKS, MOE_BLK, CHUNK_W), lambda b, be, bv, nu: (0, blk(b, be, bv, nu), 0)),
                pl.BlockSpec((None, None, D, D_EXPERT), w_idx),
                pl.BlockSpec((None, None, D, D_EXPERT), w_idx),
                pl.BlockSpec((None, None, D_EXPERT, D), w_idx),
            ],
            out_specs=pl.BlockSpec((ROW_CHUNKS, MOE_BLK, CHUNK_W), lambda b, be, bv, nu: (0, b, 0)),
            scratch_shapes=[pltpu.VMEM((D, D_EXPERT), BF16), pltpu.VMEM((D, D_EXPERT), BF16),
                            pltpu.VMEM((D_EXPERT, D), BF16)],
        ),
        out_shape=jax.ShapeDtypeStruct(xs.shape, F32),
        compiler_params=_params(("arbitrary",)),
    )(blk_e, blk_valid, n_used, xs, wg, wu, wd)


def _combine_kernel(x_ref, mod_ref, wt_ref, y_ref, sg_ref, su_ref, sd_ref, lg_ref, lb_ref, *o_refs, n_prompt_tiles):
    x = x_ref[...]
    hb = (x * (1.0 + mod_ref[4:5, :]) + mod_ref[3:4, :]).astype(BF16)
    g = jnp.dot(hb, sg_ref[...], preferred_element_type=F32)
    u = jnp.dot(hb, su_ref[...], preferred_element_type=F32)
    ff = jnp.dot((g * jax.nn.sigmoid(g) * u).astype(BF16), sd_ref[...], preferred_element_type=F32)
    for k in range(TOP_K):
        yk = jnp.concatenate([y_ref[c, k] for c in range(ROW_CHUNKS)], axis=1)
        ff = ff + yk * wt_ref[:, k:k + 1]
    out = _layer_norm_rows(ALPHA * x + mod_ref[5:6, :] * ff, lg_ref[...], lb_ref[...])
    if len(o_refs) == 1:
        o_refs[0][...] = out
    else:
        is_prompt = pl.program_id(0) < n_prompt_tiles

        @pl.when(is_prompt)
        def _():
            o_refs[0][...] = out

        @pl.when(jnp.logical_not(is_prompt))
        def _():
            o_refs[1][...] = out


def _combine(geo, x, mod_l, wt, ytok, sg, su, sd, ln_g, ln_b, split=False):
    tb = TOK_TILE
    n_p = geo.t_prompt // tb
    full = lambda shape: pl.BlockSpec(shape, lambda i: (0,) * len(shape))
    if split:
        out_specs = [pl.BlockSpec((tb, D), lambda i: (jnp.minimum(i, n_p - 1), 0)),
                     pl.BlockSpec((tb, D), lambda i: (jnp.maximum(i - n_p, 0), 0))]
        out_shape = [jax.ShapeDtypeStruct((geo.t_prompt, D), F32), jax.ShapeDtypeStruct((geo.t_sample, D), F32)]
    else:
        out_specs = pl.BlockSpec((tb, D), lambda i: (i, 0))
        out_shape = jax.ShapeDtypeStruct((geo.t, D), F32)
    return pl.pallas_call(
        functools.partial(_combine_kernel, n_prompt_tiles=n_p),
        name="combine",
        grid=(geo.t // tb,),
        in_specs=[
            pl.BlockSpec((tb, D), lambda i: (i, 0)),
            pl.BlockSpec((None, 6, D), lambda i: (geo.cond_row(i, tb), 0, 0)),
            pl.BlockSpec((tb, TOP_K), lambda i: (i, 0)),
            pl.BlockSpec((ROW_CHUNKS, TOP_K, tb, CHUNK_W), lambda i: (0, 0, i, 0)),
            full((D, D_EXPERT)), full((D, D_EXPERT)), full((D_EXPERT, D)), full((1, D)), full((1, D)),
        ],
        out_specs=out_specs,
        out_shape=out_shape,
        compiler_params=_params(("arbitrary",)),
    )(x, mod_l, wt, ytok, sg.astype(BF16), su.astype(BF16), sd.astype(BF16),
      ln_g.reshape(1, D), ln_b.reshape(1, D))


def _moe_layer(geo, x, mod_l, w_router, e_bias, layer, wg, wu, wd, sg, su, sd, ln_g, ln_b, split=False):
    t = geo.t
    top_e, w, rank, counts, h = _router(geo, x, mod_l, w_router, e_bias)
    n_blocks = (t * TOP_K) // MOE_BLK + N_EXPERTS
    n_rows = n_blocks * MOE_BLK
    padded = (counts + MOE_BLK - 1) // MOE_BLK * MOE_BLK
    pend = jnp.cumsum(padded)
    pstart = (pend - padded).astype(jnp.int32)
    blk_row0 = jnp.arange(n_blocks, dtype=jnp.int32) * MOE_BLK
    blk_e = jnp.minimum(jnp.sum((pend[None, :] <= blk_row0[:, None]).astype(jnp.int32), axis=1), N_EXPERTS - 1)
    blk_valid = jnp.clip(pstart[blk_e] + counts[blk_e] - blk_row0, 0, MOE_BLK).astype(jnp.int32)
    n_used = (pend[-1:] // MOE_BLK).astype(jnp.int32)
    slots = _slots(geo, pstart, top_e, rank)
    idx = (slots.reshape(1, TOP_K * t) + (jnp.arange(ROW_CHUNKS, dtype=jnp.int32) * n_rows)[:, None]).reshape(-1)
    win_per_chunk = TOP_K * t // SC_WINDOW
    tok_windows = t // SC_WINDOW

    def src_block(wdw):
        return (wdw // win_per_chunk) * tok_windows + (wdw % win_per_chunk) % tok_windows

    xs = _sc_scatter(h.reshape(ROW_CHUNKS * t, CHUNK_W), idx, ROW_CHUNKS * n_rows, src_block)
    yb = _ffn(xs.reshape(ROW_CHUNKS, n_rows, CHUNK_W), blk_e, blk_valid, n_used, layer, wg, wu, wd, n_blocks)
    ytok = _sc_gather(yb.reshape(ROW_CHUNKS * n_rows, CHUNK_W), idx)
    return _combine(geo, x, mod_l, w.T, ytok.reshape(ROW_CHUNKS, TOP_K, t, CHUNK_W), sg, su, sd, ln_g, ln_b,
                    split=split)


def _pos_embed(rows):
    quarter = D // 4
    omega = 1.0 / (POS_BASE ** (jnp.arange(quarter, dtype=F32) / quarter))
    r, col = jnp.meshgrid(jnp.arange(rows, dtype=F32), jnp.arange(GRID_W, dtype=F32), indexing='ij')
    r = r.reshape(-1, 1) * omega
    col = col.reshape(-1, 1) * omega
    return jnp.concatenate([jnp.sin(r), jnp.cos(r), jnp.sin(col), jnp.cos(col)], axis=-1)


def _mlstm_layer(geo, x, mod_l, j, a_w_in, a_b_gates, a_norm, a_w_out, ln_g, ln_b,
                 state_C, state_n, state_m):
    q, kt, v, so, gc, gr = _proj_a(geo, x, mod_l, a_w_in[j], a_b_gates[j])
    hp, c_p, n_p, m_p = _mlstm_scan(q, kt, v, gc, gr, row0=0, n_seq=geo.n_prompt, seq_len=geo.prompt_len)
    ns = geo.n_sample
    n0 = jnp.pad(state_n[:, j].astype(F32)[..., None], ((0, 0),) * 4 + ((0, LANES - 1),))
    m0 = jnp.pad(state_m[:, j].astype(F32), ((0, 0), (0, 0), (0, SUBLANES - NH_A)))
    m0 = jnp.broadcast_to(m0[..., None], (ns, 2, SUBLANES, LANES))
    hs, _, _, _ = _mlstm_scan(q, kt, v, gc, gr, row0=geo.t_prompt, n_seq=ns, seq_len=geo.sample_len,
                              state=(state_C[:, j].astype(F32), n0, m0))
    x1 = _out_a(geo, hp, hs, so, a_norm[j], a_w_out[j], x, mod_l, ln_g, ln_b)
    return x1, c_p, n_p[..., 0], m_p[:, :, :NH_A, 0]


def _hgrn_layer(geo, x, mod_l, j, lb_layer, b_w_in, b_norm, b_w_out, ln_g, ln_b, state_S):
    q, pre, v, sg = _proj_b(geo, x, mod_l, b_w_in[j])
    lbd = lb_layer.reshape(2, NH_B, 1, DK_B)
    op, s_p = _hgrn_scan(q, pre, v, lbd, row0=0, n_seq=geo.n_prompt, seq_len=geo.prompt_len)
    os_, _ = _hgrn_scan(q, pre, v, lbd, row0=geo.t_prompt, n_seq=geo.n_sample, seq_len=geo.sample_len,
                        state=state_S[:, j].astype(F32))
    x1 = _out_b(geo, op, os_, sg, b_norm[j], b_w_out[j], x, mod_l, ln_g, ln_b)
    return x1, s_p


def kernel(x_prompt, x_sample, state_mlstm_C, state_mlstm_n, state_mlstm_m, state_hgrn_S, c, c_ctx, w_mod, b_mod, ln_g, ln_b, a_w_in, a_b_gates, a_norm, a_w_out, b_w_in, b_lb, b_norm, b_w_out, w_router, e_bias, w_gate, w_up, w_down, ws_gate, ws_up, ws_down):
    bp, sp, _ = x_prompt.shape
    bs, ss, _ = x_sample.shape
    geo = Geometry(bp, sp, bs, ss)
    cond = jnp.zeros((COND_ROWS, D), F32).at[0].set(c_ctx).at[1:1 + bs].set(c)
    mod = _modulation(cond, w_mod, b_mod)
    x = jnp.concatenate([x_prompt.reshape(-1, D), (x_sample + _pos_embed(ss // GRID_W)[None]).reshape(-1, D)], axis=0)
    x1, new_c, new_n, new_m = _mlstm_layer(geo, x, mod[0], 0, a_w_in, a_b_gates, a_norm, a_w_out,
                                           ln_g[0, 0], ln_b[0, 0], state_mlstm_C, state_mlstm_n, state_mlstm_m)
    x2 = _moe_layer(geo, x1, mod[0], w_router[0], e_bias[0], 0, w_gate, w_up, w_down, ws_gate[0], ws_up[0], ws_down[0],
                    ln_g[0, 1], ln_b[0, 1])
    sm = jax.nn.softmax(b_lb.astype(F32), axis=0)
    lb_all = jnp.cumsum(sm, axis=0) - sm[0]
    x3, new_s = _hgrn_layer(geo, x2, mod[1], 0, lb_all[1], b_w_in, b_norm, b_w_out, ln_g[1, 0], ln_b[1, 0],
                            state_hgrn_S)
    y_p, y_s = _moe_layer(geo, x3, mod[1], w_router[1], e_bias[1], 1, w_gate, w_up, w_down, ws_gate[1], ws_up[1],
                          ws_down[1], ln_g[1, 1], ln_b[1, 1], split=True)
    y_prompt = y_p.reshape(bp, sp, D)
    y_sample = y_s.reshape(bs, ss, D)
    return y_prompt, y_sample, new_c[:, None], new_n[:, None], new_m[:, None], new_s[:, None]
```

```python
import functools

import jax
import jax.numpy as jnp
from jax import lax
from jax.experimental import pallas as pl
from jax.experimental.pallas import tpu as pltpu
from jax.experimental.pallas import tpu_sc as plsc

F32 = jnp.float32
BF16 = jnp.bfloat16
HIGHEST = lax.Precision.HIGHEST

D = 1024
DEPTH = 2
GRID_W = 64
POS_BASE = 10000.0
EPS = 1e-6
ALPHA = (2.0 * DEPTH) ** 0.25
NH_A, DK_A, DV_A = 4, 128, 256
QK_A, V_A = NH_A * DK_A, NH_A * DV_A
NH_B, DK_B = 8, 128
N_EXPERTS, TOP_K, N_GROUPS, TOPK_GROUPS = 64, 8, 8, 4
GROUP_SIZE = N_EXPERTS // N_GROUPS
D_EXPERT = D // 4
ROUTED_SCALE = 2.5

LANES = 128
SUBLANES = 8
COND_ROWS = 8
TOK_TILE = 256
CHUNK_A = 256
VMEM_LIMIT = 56 * 1024 * 1024

NT_DIMS = (((1,), (1,)), ((), ()))


def _params(sem):
    return pltpu.CompilerParams(dimension_semantics=sem, vmem_limit_bytes=VMEM_LIMIT)


def _split3(x):
    hi = x.astype(BF16)
    r = x - hi.astype(F32)
    mid = r.astype(BF16)
    lo = (r - mid.astype(F32)).astype(BF16)
    return hi, mid, lo


def _dot3(a_bf, x, transpose_side=None):
    hi, mid, lo = _split3(x)
    return (jnp.dot(a_bf, hi, preferred_element_type=F32)
            + jnp.dot(a_bf, mid, preferred_element_type=F32)
            + jnp.dot(a_bf, lo, preferred_element_type=F32))


def _dot3_r(x, a_bf):
    hi, mid, lo = _split3(x)
    return (jnp.dot(hi, a_bf, preferred_element_type=F32)
            + jnp.dot(mid, a_bf, preferred_element_type=F32)
            + jnp.dot(lo, a_bf, preferred_element_type=F32))


def _log_sigmoid(x):
    return jnp.minimum(x, 0.0) - jnp.log1p(jnp.exp(-jnp.abs(x)))


def _layer_norm_rows(x, g, b):
    mu = jnp.mean(x, axis=-1, keepdims=True)
    xc = x - mu
    var = jnp.mean(xc * xc, axis=-1, keepdims=True)
    return xc * lax.rsqrt(var + EPS) * g + b


class Geometry:
    def __init__(self, n_prompt, prompt_len, n_sample, sample_len):
        self.n_prompt, self.prompt_len = n_prompt, prompt_len
        self.n_sample, self.sample_len = n_sample, sample_len
        self.t_prompt = n_prompt * prompt_len
        self.t_sample = n_sample * sample_len
        self.t = self.t_prompt + self.t_sample
        assert prompt_len % TOK_TILE == 0 and sample_len % TOK_TILE == 0
        assert n_sample + 1 <= COND_ROWS

    def cond_row(self, tile, tile_rows):
        n_p = self.t_prompt // tile_rows
        return jnp.where(tile < n_p, 0, 1 + (tile - n_p) // (self.sample_len // tile_rows))


def _mod_kernel(cond_ref, w_ref, b_ref, o_ref):
    c = cond_ref[...]
    s = c * jax.nn.sigmoid(c)
    o_ref[0, 0] = jnp.dot(s, w_ref[0], precision=HIGHEST, preferred_element_type=F32) + b_ref[0, 0]


def _modulation(cond, w_mod, b_mod):
    out = pl.pallas_call(
        _mod_kernel,
        name="modulation",
        grid=(DEPTH, 6),
        in_specs=[
            pl.BlockSpec((COND_ROWS, D), lambda l, j: (0, 0)),
            pl.BlockSpec((1, D, D), lambda l, j: (l, 0, j)),
            pl.BlockSpec((1, 1, 1, D), lambda l, j: (l, j, 0, 0)),
        ],
        out_specs=pl.BlockSpec((1, 1, COND_ROWS, D), lambda l, j: (l, j, 0, 0)),
        out_shape=jax.ShapeDtypeStruct((DEPTH, 6, COND_ROWS, D), F32),
        compiler_params=_params(("arbitrary", "arbitrary")),
    )(cond, w_mod, b_mod.reshape(DEPTH, 6, 1, D))
    return out.transpose(0, 2, 1, 3)


def _proj_a_kernel(x_ref, mod_ref, wq_ref, wkt_ref, wvo_ref, wg_ref, wgt_ref, bg_ref, bgt_ref,
                   q_ref, kt_ref, v_ref, so_ref, gc_ref, gr_ref):
    h = x_ref[...] * (1.0 + mod_ref[1:2, :]) + mod_ref[0:1, :]
    hb = h.astype(BF16)
    q_ref[...] = jnp.dot(hb, wq_ref[...], preferred_element_type=F32).astype(BF16)
    kt = lax.dot_general(wkt_ref[...], hb, NT_DIMS, preferred_element_type=F32)
    kt_ref[...] = (kt * (DK_A ** -0.5)).astype(BF16)
    vo = jnp.dot(hb, wvo_ref[...], preferred_element_type=F32)
    v_ref[...] = vo[:, :V_A].astype(BF16)
    so_ref[...] = jax.nn.sigmoid(vo[:, V_A:]).astype(BF16)
    gc_ref[...] = jnp.dot(h, wg_ref[...], precision=HIGHEST, preferred_element_type=F32) + bg_ref[...]
    gr_ref[...] = lax.dot_general(wgt_ref[...], h, NT_DIMS, precision=HIGHEST,
                                  preferred_element_type=F32) + bgt_ref[...]


def _proj_a(geo, x, mod_l, w_in, b_gates):
    t = geo.t
    n_gate = 4 * NH_A
    wq = w_in[:, :QK_A].astype(BF16)
    wkt = w_in[:, QK_A:2 * QK_A].T.astype(BF16)
    wvo = w_in[:, 2 * QK_A:2 * QK_A + 2 * V_A].astype(BF16)
    wg = w_in[:, 2 * QK_A + 2 * V_A:]
    wg_pad = jnp.pad(wg, ((0, 0), (0, LANES - n_gate)))
    bg = b_gates.reshape(n_gate).astype(F32)
    bg_pad = jnp.pad(bg, (0, LANES - n_gate)).reshape(1, LANES)
    tb = TOK_TILE
    full = lambda shape: pl.BlockSpec(shape, lambda i: (0,) * len(shape))
    return pl.pallas_call(
        _proj_a_kernel,
        name="proj_a",
        grid=(t // tb,),
        in_specs=[
            pl.BlockSpec((tb, D), lambda i: (i, 0)),
            pl.BlockSpec((None, 6, D), lambda i: (geo.cond_row(i, tb), 0, 0)),
            full((D, QK_A)), full((QK_A, D)), full((D, 2 * V_A)), full((D, LANES)), full((n_gate, D)),
            full((1, LANES)), full((n_gate, 1)),
        ],
        out_specs=[
            pl.BlockSpec((tb, QK_A), lambda i: (i, 0)),
            pl.BlockSpec((QK_A, tb), lambda i: (0, i)),
            pl.BlockSpec((tb, V_A), lambda i: (i, 0)),
            pl.BlockSpec((tb, V_A), lambda i: (i, 0)),
            pl.BlockSpec((tb, LANES), lambda i: (i, 0)),
            pl.BlockSpec((n_gate, tb), lambda i: (0, i)),
        ],
        out_shape=[
            jax.ShapeDtypeStruct((t, QK_A), BF16),
            jax.ShapeDtypeStruct((QK_A, t), BF16),
            jax.ShapeDtypeStruct((t, V_A), BF16),
            jax.ShapeDtypeStruct((t, V_A), BF16),
            jax.ShapeDtypeStruct((t, LANES), F32),
            jax.ShapeDtypeStruct((n_gate, t), F32),
        ],
        compiler_params=_params(("parallel",)),
    )(x, mod_l, wq, wkt, wvo, wg_pad, wg.T, bg_pad, bg.reshape(n_gate, 1))


def _mlstm_scan_kernel(*refs, chunk, has_state):
    if has_state:
        (q_ref, kt_ref, v_ref, gc_ref, gr_ref, c0_ref, n0_ref, m0_ref,
         h_ref, c_out, n_out, m_out, c_sc, n_sc, m_sc) = refs
    else:
        (q_ref, kt_ref, v_ref, gc_ref, gr_ref,
         h_ref, c_out, n_out, m_out, c_sc, n_sc, m_sc) = refs
    L = chunk
    d = pl.program_id(1)
    c = pl.program_id(2)
    fwd = d == 0

    @pl.when(c == 0)
    def _():
        if has_state:
            c_sc[...] = c0_ref[0, 0]
            n_sc[...] = n0_ref[0, 0]
            m_sc[...] = m0_ref[0, 0]
        else:
            c_sc[...] = jnp.zeros_like(c_sc)
            n_sc[...] = jnp.zeros_like(n_sc)
            m_sc[...] = jnp.zeros_like(m_sc)

    row = lax.broadcasted_iota(jnp.int32, (L, L), 0)
    col = lax.broadcasted_iota(jnp.int32, (L, L), 1)
    sgn = 1 - 2 * d
    causal = (row - col) * sgn >= 0
    tri = causal.astype(BF16)
    tri_t = ((col - row) * sgn >= 0).astype(BF16)

    gc = gc_ref[...]
    gr = gr_ref[...]
    bc_all = _dot3(tri, _log_sigmoid(gc))
    br_all = _dot3_r(_log_sigmoid(gr), tri_t)
    ones_blk = (lax.broadcasted_iota(jnp.int32, (L, LANES), 1) == 0).astype(BF16)

    for h in range(NH_A):
        b_c = jnp.where(fwd, bc_all[:, 4 + h:5 + h], bc_all[:, 12 + h:13 + h])
        b_r = jnp.where(fwd, br_all[4 + h:5 + h, :], br_all[12 + h:13 + h, :])
        i_r = jnp.where(fwd, gr[h:h + 1, :], gr[8 + h:9 + h, :])
        bl = jnp.where(fwd, b_r[:, L - 1:L], b_r[:, 0:1])
        q = q_ref[:, h * DK_A:(h + 1) * DK_A]
        kt = kt_ref[h * DK_A:(h + 1) * DK_A, :]
        v = v_ref[:, h * DV_A:(h + 1) * DV_A]
        m = m_sc[h:h + 1, 0:1]
        cst = c_sc[h]
        nst = n_sc[h]

        a_r = i_r - b_r
        logd = jnp.where(causal, b_c + a_r, -jnp.inf)
        inter = b_c + m
        m_t = jnp.maximum(inter, jnp.max(logd, axis=1, keepdims=True))
        dmat = jnp.exp(logd - m_t)
        e_int = jnp.exp(inter - m_t)
        s = (jnp.dot(q, kt, preferred_element_type=F32) * dmat).astype(BF16)
        num = (jnp.dot(s, v, preferred_element_type=F32)
               + e_int * jnp.dot(q, cst.astype(BF16), preferred_element_type=F32))
        den = (jnp.dot(s, ones_blk, preferred_element_type=F32)
               + e_int * jnp.dot(q, nst.astype(BF16), preferred_element_type=F32))[:, 0:1]
        h_ref[:, h * DV_A:(h + 1) * DV_A] = num / jnp.maximum(jnp.abs(den), jnp.exp(-m_t))

        logw = bl + a_r
        m_new = jnp.maximum(bl + m, jnp.max(logw, axis=1, keepdims=True))
        w = jnp.exp(logw - m_new)
        decay = jnp.exp(bl + m - m_new)
        kw = (kt.astype(F32) * w).astype(BF16)
        c_sc[h] = decay * cst + jnp.dot(kw, v, preferred_element_type=F32)
        n_sc[h] = decay * nst + jnp.dot(kw, ones_blk, preferred_element_type=F32)
        m_sc[h:h + 1, :] = jnp.broadcast_to(m_new, (1, LANES))

    @pl.when(c == pl.num_programs(2) - 1)
    def _():
        c_out[0, 0] = c_sc[...]
        n_out[0, 0] = n_sc[...]
        m_out[0, 0] = m_sc[...]


def _mlstm_scan(q, kt, v, gc, gr, *, row0, n_seq, seq_len, state=None):
    L = CHUNK_A
    nc = seq_len // L
    blk0 = row0 // L

    def loc_blk(b, d, c):
        return b * nc + c + d * (nc - 1 - 2 * c)

    def tok_blk(b, d, c):
        return blk0 + loc_blk(b, d, c)

    in_specs = [
        pl.BlockSpec((L, QK_A), lambda b, d, c: (tok_blk(b, d, c), 0)),
        pl.BlockSpec((QK_A, L), lambda b, d, c: (0, tok_blk(b, d, c))),
        pl.BlockSpec((L, V_A), lambda b, d, c: (tok_blk(b, d, c), 0)),
        pl.BlockSpec((L, LANES), lambda b, d, c: (tok_blk(b, d, c), 0)),
        pl.BlockSpec((4 * NH_A, L), lambda b, d, c: (0, tok_blk(b, d, c))),
    ]
    args = [q, kt, v, gc, gr]
    if state is not None:
        in_specs += [
            pl.BlockSpec((1, 1, NH_A, DK_A, DV_A), lambda b, d, c: (b, d, 0, 0, 0)),
            pl.BlockSpec((1, 1, NH_A, DK_A, LANES), lambda b, d, c: (b, d, 0, 0, 0)),
            pl.BlockSpec((1, 1, SUBLANES, LANES), lambda b, d, c: (b, d, 0, 0)),
        ]
        args += list(state)
    return pl.pallas_call(
        functools.partial(_mlstm_scan_kernel, chunk=L, has_state=state is not None),
        name="mlstm_scan_seeded" if state is not None else "mlstm_scan",
        grid=(n_seq, 2, nc),
        in_specs=in_specs,
        out_specs=[
            pl.BlockSpec((None, L, V_A), lambda b, d, c: (d, loc_blk(b, d, c), 0)),
            pl.BlockSpec((1, 1, NH_A, DK_A, DV_A), lambda b, d, c: (b, d, 0, 0, 0)),
            pl.BlockSpec((1, 1, NH_A, DK_A, LANES), lambda b, d, c: (b, d, 0, 0, 0)),
            pl.BlockSpec((1, 1, SUBLANES, LANES), lambda b, d, c: (b, d, 0, 0)),
        ],
        out_shape=[
            jax.ShapeDtypeStruct((2, n_seq * seq_len, V_A), F32),
            jax.ShapeDtypeStruct((n_seq, 2, NH_A, DK_A, DV_A), F32),
            jax.ShapeDtypeStruct((n_seq, 2, NH_A, DK_A, LANES), F32),
            jax.ShapeDtypeStruct((n_seq, 2, SUBLANES, LANES), F32),
        ],
        scratch_shapes=[
            pltpu.VMEM((NH_A, DK_A, DV_A), F32),
            pltpu.VMEM((NH_A, DK_A, LANES), F32),
            pltpu.VMEM((SUBLANES, LANES), F32),
        ],
        compiler_params=_params(("parallel", "parallel", "arbitrary")),
    )(*args)


def _out_a_kernel(hp_ref, hs_ref, so_ref, nw_ref, w_ref, x_ref, mod_ref, lg_ref, lb_ref, o_ref, *, n_prompt_tiles):
    is_prompt = pl.program_id(0) < n_prompt_tiles
    y = jnp.where(is_prompt, hp_ref[0] + hp_ref[1], hs_ref[0] + hs_ref[1])
    parts = []
    for h in range(NH_A):
        yh = y[:, h * DV_A:(h + 1) * DV_A]
        mu = jnp.mean(yh, axis=-1, keepdims=True)
        yc = yh - mu
        var = jnp.mean(yc * yc, axis=-1, keepdims=True)
        parts.append(yc * lax.rsqrt(var + EPS))
    yn = jnp.concatenate(parts, axis=-1) * nw_ref[...] * so_ref[...].astype(F32)
    out = jnp.dot(yn.astype(BF16), w_ref[...], preferred_element_type=F32)
    o_ref[...] = _layer_norm_rows(ALPHA * x_ref[...] + mod_ref[2:3, :] * out, lg_ref[...], lb_ref[...])


def _out_a(geo, h_prompt, h_sample, so, norm_w, w_out, x, mod_l, ln_g, ln_b):
    t = geo.t
    tb = TOK_TILE
    n_p = geo.t_prompt // tb
    full = lambda shape: pl.BlockSpec(shape, lambda i: (0,) * len(shape))
    return pl.pallas_call(
        functools.partial(_out_a_kernel, n_prompt_tiles=n_p),
        name="out_a",
        grid=(t // tb,),
        in_specs=[
            pl.BlockSpec((2, tb, V_A), lambda i: (0, jnp.minimum(i, n_p - 1), 0)),
            pl.BlockSpec((2, tb, V_A), lambda i: (0, jnp.maximum(i - n_p, 0), 0)),
            pl.BlockSpec((tb, V_A), lambda i: (i, 0)),
            full((1, V_A)), full((V_A, D)),
            pl.BlockSpec((tb, D), lambda i: (i, 0)),
            pl.BlockSpec((None, 6, D), lambda i: (geo.cond_row(i, tb), 0, 0)),
            full((1, D)), full((1, D)),
        ],
        out_specs=pl.BlockSpec((tb, D), lambda i: (i, 0)),
        out_shape=jax.ShapeDtypeStruct((t, D), F32),
        compiler_params=_params(("parallel",)),
    )(h_prompt, h_sample, so, norm_w.reshape(1, V_A).astype(F32), w_out.astype(BF16), x, mod_l,
      ln_g.reshape(1, D), ln_b.reshape(1, D))


def _proj_b_kernel(x_ref, mod_ref, w_ref, q_ref, pre_ref, v_ref, sg_ref):
    h = x_ref[...] * (1.0 + mod_ref[1:2, :]) + mod_ref[0:1, :]
    z = jnp.dot(h.astype(BF16), w_ref[...], preferred_element_type=F32)
    for hd in range(NH_B):
        lo = hd * DK_B
        qh = z[:, lo:lo + DK_B]
        q_ref[hd] = qh * jax.nn.sigmoid(qh)
        pre_ref[0, hd] = z[:, D + lo:D + lo + DK_B]
        pre_ref[1, hd] = z[:, 2 * D + lo:2 * D + lo + DK_B]
        v_ref[hd] = z[:, 3 * D + lo:3 * D + lo + DK_B].astype(BF16)
    g = z[:, 4 * D:]
    sg_ref[...] = (g * jax.nn.sigmoid(g)).astype(BF16)


def _proj_b(geo, x, mod_l, w_in):
    t = geo.t
    tb = TOK_TILE
    return pl.pallas_call(
        _proj_b_kernel,
        name="proj_b",
        grid=(t // tb,),
        in_specs=[
            pl.BlockSpec((tb, D), lambda i: (i, 0)),
            pl.BlockSpec((None, 6, D), lambda i: (geo.cond_row(i, tb), 0, 0)),
            pl.BlockSpec((D, 5 * D), lambda i: (0, 0)),
        ],
        out_specs=[
            pl.BlockSpec((NH_B, tb, DK_B), lambda i: (0, i, 0)),
            pl.BlockSpec((2, NH_B, tb, DK_B), lambda i: (0, 0, i, 0)),
            pl.BlockSpec((NH_B, tb, DK_B), lambda i: (0, i, 0)),
            pl.BlockSpec((tb, D), lambda i: (i, 0)),
        ],
        out_shape=[
            jax.ShapeDtypeStruct((NH_B, t, DK_B), F32),
            jax.ShapeDtypeStruct((2, NH_B, t, DK_B), F32),
            jax.ShapeDtypeStruct((NH_B, t, DK_B), BF16),
            jax.ShapeDtypeStruct((t, D), BF16),
        ],
        compiler_params=_params(("parallel",)),
    )(x, mod_l, w_in.astype(BF16))


CHUNK_B = 128
BAND = SUBLANES // 2
TN_DIMS = (((0,), (0,)), ((), ()))


def _hgrn_head(q, pre, lbv, v_bf, st, fwd):
    L = q.shape[0]
    sg = jax.nn.sigmoid(pre)
    f = lbv + (1.0 - lbv) * sg
    lf = jnp.log(f)
    kk = (1.0 - lbv) * (1.0 - sg)
    row = lax.broadcasted_iota(jnp.int32, (L, L), 0)
    col = lax.broadcasted_iota(jnp.int32, (L, L), 1)
    tri = ((row >= col) if fwd else (row <= col)).astype(BF16)
    b = _dot3(tri, lf)
    tpos = lax.broadcasted_iota(jnp.int32, (L, DK_B), 0)
    blk_bits = row ^ col
    lag = jnp.where(blk_bits < BAND, (row - col) if fwd else (col - row), -1)

    step = 1 if fwd else L - 1
    att = jnp.where(lag == 0, jnp.sum(q * kk, axis=1, keepdims=True), 0.0)
    f_r, kk_r, g = f, kk, f
    for dl in range(1, BAND):
        if dl > 1:
            f_r = pltpu.roll(f_r, step, 0)
            g = g * f_r
        kk_r = pltpu.roll(kk_r, step, 0)
        att = jnp.where(lag == dl, jnp.sum(q * kk_r * g, axis=1, keepdims=True), att)

    w = BAND
    while w < L:
        nb = L // (2 * w)
        b3 = b.reshape(nb, 2 * w, DK_B)
        edge = (b3[:, w - 1:w, :] if fwd else b3[:, w:w + 1, :])
        bmid = jnp.broadcast_to(edge, (nb, 2 * w, DK_B)).reshape(L, DK_B)
        second = (tpos & w) != 0
        t_side = second if fwd else jnp.logical_not(second)
        e = jnp.exp(jnp.where(t_side, b - bmid, bmid - b))
        qt = jnp.where(t_side, q * e, 0.0).astype(BF16)
        ks = jnp.where(t_side, 0.0, kk * e).astype(BF16)
        a = lax.dot_general(qt, ks, NT_DIMS, preferred_element_type=F32)
        att = att + jnp.where(blk_bits < 2 * w, a, 0.0)
        w *= 2
    o = jnp.dot(att.astype(BF16), v_bf, preferred_element_type=F32)

    bl = b[L - 1:L, :] if fwd else b[0:1, :]
    o = o + lax.dot_general((q * jnp.exp(b)).astype(BF16), st.astype(BF16), NT_DIMS, preferred_element_type=F32)
    kd = (kk * jnp.exp(bl - b)).astype(BF16)
    st_new = jnp.exp(bl) * st + lax.dot_general(v_bf, kd, TN_DIMS, preferred_element_type=F32)
    return o, st_new


def _hgrn_scan_kernel(*refs, has_state):
    if has_state:
        q_ref, pre_ref, v_ref, lb_ref, s0_ref, o_ref, s_out, st_sc = refs
    else:
        q_ref, pre_ref, v_ref, lb_ref, o_ref, s_out, st_sc = refs
    d = pl.program_id(1)
    c = pl.program_id(2)

    @pl.when(c == 0)
    def _():
        if has_state:
            for hd in range(NH_B):
                st_sc[hd] = s0_ref[0, 0, hd].T
        else:
            st_sc[...] = jnp.zeros_like(st_sc)

    def run(fwd):
        def head(hd, carry):
            o, st_new = _hgrn_head(q_ref[hd], pre_ref[hd], lb_ref[hd], v_ref[hd], st_sc[hd], fwd)
            o_ref[hd] = o
            st_sc[hd] = st_new
            return carry
        lax.fori_loop(0, NH_B, head, 0, unroll=8)

    @pl.when(d == 0)
    def _():
        run(True)

    @pl.when(d == 1)
    def _():
        run(False)

    @pl.when(c == pl.num_programs(2) - 1)
    def _():
        for hd in range(NH_B):
            s_out[0, 0, hd] = st_sc[hd].T


def _hgrn_scan(q, pre, v, lbd, *, row0, n_seq, seq_len, state=None):
    L = CHUNK_B
    nc = seq_len // L
    blk0 = row0 // L

    def loc_blk(b, d, c):
        return b * nc + c + d * (nc - 1 - 2 * c)

    def tok_blk(b, d, c):
        return blk0 + loc_blk(b, d, c)

    in_specs = [
        pl.BlockSpec((NH_B, L, DK_B), lambda b, d, c: (0, tok_blk(b, d, c), 0)),
        pl.BlockSpec((None, NH_B, L, DK_B), lambda b, d, c: (d, 0, tok_blk(b, d, c), 0)),
        pl.BlockSpec((NH_B, L, DK_B), lambda b, d, c: (0, tok_blk(b, d, c), 0)),
        pl.BlockSpec((None, NH_B, 1, DK_B), lambda b, d, c: (d, 0, 0, 0)),
    ]
    args = [q, pre, v, lbd]
    if state is not None:
        in_specs.append(pl.BlockSpec((1, 1, NH_B, DK_B, DK_B), lambda b, d, c: (b, d, 0, 0, 0)))
        args.append(state)
    return pl.pallas_call(
        functools.partial(_hgrn_scan_kernel, has_state=state is not None),
        name="hgrn_scan_seeded" if state is not None else "hgrn_scan",
        grid=(n_seq, 2, nc),
        in_specs=in_specs,
        out_specs=[
            pl.BlockSpec((None, NH_B, L, DK_B), lambda b, d, c: (d, 0, loc_blk(b, d, c), 0)),
            pl.BlockSpec((1, 1, NH_B, DK_B, DK_B), lambda b, d, c: (b, d, 0, 0, 0)),
        ],
        out_shape=[
            jax.ShapeDtypeStruct((2, NH_B, n_seq * seq_len, DK_B), F32),
            jax.ShapeDtypeStruct((n_seq, 2, NH_B, DK_B, DK_B), F32),
        ],
        scratch_shapes=[pltpu.VMEM((NH_B, DK_B, DK_B), F32)],
        compiler_params=_params(("parallel", "parallel", "arbitrary")),
    )(*args)


def _out_b_kernel(op_ref, os_ref, sg_ref, nw_ref, w_ref, x_ref, mod_ref, lg_ref, lb_ref, out_ref, *, n_prompt_tiles):
    is_prompt = pl.program_id(0) < n_prompt_tiles
    parts = []
    for hd in range(NH_B):
        y = jnp.where(is_prompt, op_ref[0, hd] + op_ref[1, hd], os_ref[0, hd] + os_ref[1, hd])
        parts.append(y * lax.rsqrt(jnp.mean(y * y, axis=-1, keepdims=True) + EPS))
    yn = jnp.concatenate(parts, axis=-1) * nw_ref[...] * sg_ref[...].astype(F32)
    out = jnp.dot(yn.astype(BF16), w_ref[...], preferred_element_type=F32)
    out_ref[...] = _layer_norm_rows(ALPHA * x_ref[...] + mod_ref[2:3, :] * out, lg_ref[...], lb_ref[...])


def _out_b(geo, o_prompt, o_sample, sg, norm_w, w_out, x, mod_l, ln_g, ln_b):
    t = geo.t
    tb = TOK_TILE
    n_p = geo.t_prompt // tb
    full = lambda shape: pl.BlockSpec(shape, lambda i: (0,) * len(shape))
    return pl.pallas_call(
        functools.partial(_out_b_kernel, n_prompt_tiles=n_p),
        name="out_b",
        grid=(t // tb,),
        in_specs=[
            pl.BlockSpec((2, NH_B, tb, DK_B), lambda i: (0, 0, jnp.minimum(i, n_p - 1), 0)),
            pl.BlockSpec((2, NH_B, tb, DK_B), lambda i: (0, 0, jnp.maximum(i - n_p, 0), 0)),
            pl.BlockSpec((tb, D), lambda i: (i, 0)),
            full((1, D)), full((D, D)),
            pl.BlockSpec((tb, D), lambda i: (i, 0)),
            pl.BlockSpec((None, 6, D), lambda i: (geo.cond_row(i, tb), 0, 0)),
            full((1, D)), full((1, D)),
        ],
        out_specs=pl.BlockSpec((tb, D), lambda i: (i, 0)),
        out_shape=jax.ShapeDtypeStruct((t, D), F32),
        compiler_params=_params(("parallel",)),
    )(o_prompt, o_sample, sg, norm_w.reshape(1, D).astype(F32), w_out.astype(BF16), x, mod_l,
      ln_g.reshape(1, D), ln_b.reshape(1, D))


MOE_BLK = 512
U32 = jnp.uint32
ROW_WORDS = D // 2
CHUNK_W = 256
ROW_CHUNKS = ROW_WORDS // CHUNK_W
SC_WINDOW = 128


def _pack_rows(x):
    hi = pltpu.bitcast(x[:, :ROW_WORDS].astype(BF16).astype(F32), U32)
    lo = pltpu.bitcast(x[:, ROW_WORDS:].astype(BF16).astype(F32), U32)
    return hi | (lo >> 16)


def _unpack_rows(words):
    hi = pltpu.bitcast(words & jnp.uint32(0xFFFF0000), F32)
    lo = pltpu.bitcast(words << 16, F32)
    return jnp.concatenate([hi, lo], axis=1)


def _store_chunks(chunk_ref, x):
    words = _pack_rows(x)
    for c in range(ROW_CHUNKS):
        chunk_ref(c)[...] = words[:, c * CHUNK_W:(c + 1) * CHUNK_W]


def _load_chunks(chunk_ref):
    return _unpack_rows(jnp.concatenate([chunk_ref(c)[...] for c in range(ROW_CHUNKS)], axis=1))


def _first_index(hit, iota, size, axis):
    return jnp.min(jnp.where(hit, iota, size), axis=axis, keepdims=True)


def _router_kernel(x_ref, mod_ref, wrt_ref, eb_ref, e_ref, w_ref, r_ref, cnt_ref, h_ref, cnt_sc):
    i = pl.program_id(0)
    tb = x_ref.shape[0]

    @pl.when(i == 0)
    def _():
        cnt_sc[...] = jnp.zeros_like(cnt_sc)

    h = x_ref[...] * (1.0 + mod_ref[4:5, :]) + mod_ref[3:4, :]
    _store_chunks(lambda c: h_ref.at[c], h)
    logits = lax.dot_general(wrt_ref[...], h, NT_DIMS, precision=HIGHEST, preferred_element_type=F32)
    scores = jax.nn.sigmoid(logits)
    sel = scores + eb_ref[...]

    g3 = sel.reshape(N_GROUPS, GROUP_SIZE, tb)
    io3 = lax.broadcasted_iota(jnp.int32, g3.shape, 1)
    m1 = jnp.max(g3, axis=1, keepdims=True)
    first = _first_index(g3 == m1, io3, GROUP_SIZE, 1)
    m2 = jnp.max(jnp.where(io3 == first, -jnp.inf, g3), axis=1, keepdims=True)
    gscore = (m1 + m2).reshape(N_GROUPS, tb)

    iog = lax.broadcasted_iota(jnp.int32, gscore.shape, 0)
    gmask = jnp.zeros(gscore.shape, F32)
    for _ in range(TOPK_GROUPS):
        gm = jnp.max(gscore, axis=0, keepdims=True)
        pick = iog == _first_index(gscore == gm, iog, N_GROUPS, 0)
        gmask = jnp.where(pick, 1.0, gmask)
        gscore = jnp.where(pick, -jnp.inf, gscore)
    emask = jnp.broadcast_to(gmask.reshape(N_GROUPS, 1, tb), (N_GROUPS, GROUP_SIZE, tb)).reshape(N_EXPERTS, tb)
    cand = jnp.where(emask > 0.0, sel, -jnp.inf)

    ioe = lax.broadcasted_iota(jnp.int32, cand.shape, 0)
    picks, wts = [], []
    onehot = jnp.zeros(cand.shape, F32)
    for _ in range(TOP_K):
        cm = jnp.max(cand, axis=0, keepdims=True)
        idx = _first_index(cand == cm, ioe, N_EXPERTS, 0)
        pick = ioe == idx
        picks.append(pick)
        wts.append(jnp.sum(jnp.where(pick, scores, 0.0), axis=0, keepdims=True))
        onehot = onehot + pick.astype(F32)
        cand = jnp.where(pick, -jnp.inf, cand)
        e_ref[pl.ds(len(picks) - 1, 1), :] = idx
    wsum = wts[0]
    for wk in wts[1:]:
        wsum = wsum + wk
    for k in range(TOP_K):
        w_ref[pl.ds(k, 1), :] = wts[k] / wsum * ROUTED_SCALE

    r_io = lax.broadcasted_iota(jnp.int32, (tb, tb), 0)
    c_io = lax.broadcasted_iota(jnp.int32, (tb, tb), 1)
    before = (r_io < c_io).astype(BF16)
    rank = cnt_sc[:, 0:1] + jnp.dot(onehot.astype(BF16), before, preferred_element_type=F32)
    for k in range(TOP_K):
        r_ref[pl.ds(k, 1), :] = jnp.sum(jnp.where(picks[k], rank, 0.0), axis=0, keepdims=True).astype(jnp.int32)
    cnt_sc[...] = cnt_sc[...] + jnp.sum(onehot, axis=1, keepdims=True)
    cnt_ref[...] = cnt_sc[...]


def _router(geo, x, mod_l, w_router, e_bias):
    t = geo.t
    tb = TOK_TILE
    full = lambda shape: pl.BlockSpec(shape, lambda i: (0,) * len(shape))
    e, w, r, cnt, h = pl.pallas_call(
        _router_kernel,
        name="router",
        grid=(t // tb,),
        in_specs=[
            pl.BlockSpec((tb, D), lambda i: (i, 0)),
            pl.BlockSpec((None, 6, D), lambda i: (geo.cond_row(i, tb), 0, 0)),
            full((N_EXPERTS, D)), full((N_EXPERTS, 1)),
        ],
        out_specs=[
            pl.BlockSpec((TOP_K, tb), lambda i: (0, i)),
            pl.BlockSpec((TOP_K, tb), lambda i: (0, i)),
            pl.BlockSpec((TOP_K, tb), lambda i: (0, i)),
            full((N_EXPERTS, LANES)),
            pl.BlockSpec((ROW_CHUNKS, tb, CHUNK_W), lambda i: (0, i, 0)),
        ],
        out_shape=[
            jax.ShapeDtypeStruct((TOP_K, t), jnp.int32),
            jax.ShapeDtypeStruct((TOP_K, t), F32),
            jax.ShapeDtypeStruct((TOP_K, t), jnp.int32),
            jax.ShapeDtypeStruct((N_EXPERTS, LANES), F32),
            jax.ShapeDtypeStruct((ROW_CHUNKS, t, CHUNK_W), U32),
        ],
        scratch_shapes=[pltpu.VMEM((N_EXPERTS, LANES), F32)],
        compiler_params=_params(("arbitrary",)),
    )(x, mod_l, w_router.T.astype(F32), e_bias.reshape(N_EXPERTS, 1).astype(F32))
    return e, w, r, cnt[:, 0].astype(jnp.int32), h


def _slot_kernel(pstart_ref, e_ref, r_ref, o_ref):
    e = e_ref[...]
    slot = r_ref[...]
    for x in range(N_EXPERTS):
        slot = slot + jnp.where(e == x, pstart_ref[x], 0)
    o_ref[...] = slot


def _slots(geo, pstart, top_e, rank):
    tb = TOK_TILE
    return pl.pallas_call(
        _slot_kernel,
        name="slots",
        grid_spec=pltpu.PrefetchScalarGridSpec(
            num_scalar_prefetch=1,
            grid=(geo.t // tb,),
            in_specs=[pl.BlockSpec((TOP_K, tb), lambda i, p: (0, i)),
                      pl.BlockSpec((TOP_K, tb), lambda i, p: (0, i))],
            out_specs=pl.BlockSpec((TOP_K, tb), lambda i, p: (0, i)),
        ),
        out_shape=jax.ShapeDtypeStruct((TOP_K, geo.t), jnp.int32),
        compiler_params=_params(("parallel",)),
    )(pstart, top_e, rank)


def _sc_mesh():
    return plsc.VectorSubcoreMesh(core_axis_name="core", subcore_axis_name="subcore")


def _sc_scatter(rows, idx, n_out, src_block):
    n_idx = idx.shape[0]

    @pl.kernel(out_type=jax.ShapeDtypeStruct((n_out, CHUNK_W), rows.dtype), mesh=_sc_mesh(), scratch_types=[],
               name="sc_dispatch")
    def scatter(x_hbm, i_hbm, o_hbm):
        def body(x_vmem, i_vmem):
            pltpu.sync_copy(x_vmem, o_hbm.at[i_vmem.at[0]])

        pltpu.emit_pipeline(
            body,
            grid=(n_idx // SC_WINDOW,),
            in_specs=[pl.BlockSpec((SC_WINDOW, CHUNK_W), index_map=lambda w: (src_block(w), 0)),
                      pl.BlockSpec((1, SC_WINDOW), index_map=lambda w: (0, w))],
            out_specs=[],
            core_axis_name=("core", "subcore"),
            dimension_semantics=(pltpu.PARALLEL,),
        )(x_hbm, i_hbm)

    return scatter(rows, idx.reshape(1, n_idx))


def _sc_gather(table, idx):
    n_idx = idx.shape[0]

    @pl.kernel(out_type=jax.ShapeDtypeStruct((n_idx, CHUNK_W), table.dtype), mesh=_sc_mesh(),
               name="sc_combine_gather")
    def gather(t_hbm, i_hbm, o_hbm):
        def body(i_vmem, o_vmem):
            pltpu.sync_copy(t_hbm.at[i_vmem.at[0]], o_vmem)

        pltpu.emit_pipeline(
            body,
            grid=(n_idx // SC_WINDOW,),
            in_specs=[pl.BlockSpec((1, SC_WINDOW), index_map=lambda w: (0, w))],
            out_specs=[pl.BlockSpec((SC_WINDOW, CHUNK_W), index_map=lambda w: (w, 0))],
            core_axis_name=("core", "subcore"),
            dimension_semantics=(pltpu.PARALLEL,),
        )(i_hbm, o_hbm)

    return gather(table, idx.reshape(1, n_idx))


def _ffn_kernel(blk_e_ref, blk_valid_ref, n_used_ref, xs_ref, wg_ref, wu_ref, wd_ref, y_ref, wg_sc, wu_sc, wd_sc):
    b = pl.program_id(0)
    used = b < n_used_ref[0]
    new_expert = (b == 0) | (blk_e_ref[b] != blk_e_ref[jnp.maximum(b - 1, 0)])

    @pl.when(used & new_expert)
    def _():
        wg_sc[...] = wg_ref[...].astype(BF16)
        wu_sc[...] = wu_ref[...].astype(BF16)
        wd_sc[...] = wd_ref[...].astype(BF16)

    @pl.when(used)
    def _():
        x = _load_chunks(lambda c: xs_ref.at[c])
        row = lax.broadcasted_iota(jnp.int32, (MOE_BLK, 1), 0)
        x = jnp.where(row < blk_valid_ref[b], x, 0.0).astype(BF16)
        g = jnp.dot(x, wg_sc[...], preferred_element_type=F32)
        u = jnp.dot(x, wu_sc[...], preferred_element_type=F32)
        hmid = (g * jax.nn.sigmoid(g) * u).astype(BF16)
        _store_chunks(lambda c: y_ref.at[c], jnp.dot(hmid, wd_sc[...], preferred_element_type=F32))

    @pl.when(jnp.logical_not(used))
    def _():
        y_ref[...] = jnp.zeros_like(y_ref)


def _ffn(xs, blk_e, blk_valid, n_used, layer, wg, wu, wd, n_blocks):
    def blk(b, be, bv, nu):
        return jnp.maximum(jnp.minimum(b, nu[0] - 1), 0)

    def w_idx(b, be, bv, nu):
        return (layer, be[blk(b, be, bv, nu)], 0, 0)

    return pl.pallas_call(
        _ffn_kernel,
        name="expert_ffn",
        grid_spec=pltpu.PrefetchScalarGridSpec(
            num_scalar_prefetch=3,
            grid=(n_blocks,),
            in_specs=[
                pl.BlockSpec((ROW_CHUNKS, MOE_BLK, CHUNK_W), lambda b, be, bv, nu: (0, blk(b, be, bv, nu), 0)),
                pl.BlockSpec((None, None, D, D_EXPERT), w_idx),
                pl.BlockSpec((None, None, D, D_EXPERT), w_idx),
                pl.BlockSpec((None, None, D_EXPERT, D), w_idx),
            ],
            out_specs=pl.BlockSpec((ROW_CHUNKS, MOE_BLK, CHUNK_W), lambda b, be, bv, nu: (0, b, 0)),
            scratch_shapes=[pltpu.VMEM((D, D_EXPERT), BF16), pltpu.VMEM((D, D_EXPERT), BF16),
                            pltpu.VMEM((D_EXPERT, D), BF16)],
        ),
        out_shape=jax.ShapeDtypeStruct(xs.shape, U32),
        compiler_params=_params(("arbitrary",)),
    )(blk_e, blk_valid, n_used, xs, wg, wu, wd)


def _combine_kernel(x_ref, mod_ref, wt_ref, y_ref, sg_ref, su_ref, sd_ref, lg_ref, lb_ref, *o_refs, n_prompt_tiles):
    x = x_ref[...]
    hb = (x * (1.0 + mod_ref[4:5, :]) + mod_ref[3:4, :]).astype(BF16)
    g = jnp.dot(hb, sg_ref[...], preferred_element_type=F32)
    u = jnp.dot(hb, su_ref[...], preferred_element_type=F32)
    ff = jnp.dot((g * jax.nn.sigmoid(g) * u).astype(BF16), sd_ref[...], preferred_element_type=F32)
    for k in range(TOP_K):
        ff = ff + _load_chunks(lambda c: y_ref.at[c, k]) * wt_ref[:, k:k + 1]
    out = _layer_norm_rows(ALPHA * x + mod_ref[5:6, :] * ff, lg_ref[...], lb_ref[...])
    if len(o_refs) == 1:
        o_refs[0][...] = out
    else:
        is_prompt = pl.program_id(0) < n_prompt_tiles

        @pl.when(is_prompt)
        def _():
            o_refs[0][...] = out

        @pl.when(jnp.logical_not(is_prompt))
        def _():
            o_refs[1][...] = out


def _combine(geo, x, mod_l, wt, ytok, sg, su, sd, ln_g, ln_b, split=False):
    tb = TOK_TILE
    n_p = geo.t_prompt // tb
    full = lambda shape: pl.BlockSpec(shape, lambda i: (0,) * len(shape))
    if split:
        out_specs = [pl.BlockSpec((tb, D), lambda i: (jnp.minimum(i, n_p - 1), 0)),
                     pl.BlockSpec((tb, D), lambda i: (jnp.maximum(i - n_p, 0), 0))]
        out_shape = [jax.ShapeDtypeStruct((geo.t_prompt, D), F32), jax.ShapeDtypeStruct((geo.t_sample, D), F32)]
    else:
        out_specs = pl.BlockSpec((tb, D), lambda i: (i, 0))
        out_shape = jax.ShapeDtypeStruct((geo.t, D), F32)
    return pl.pallas_call(
        functools.partial(_combine_kernel, n_prompt_tiles=n_p),
        name="combine",
        grid=(geo.t // tb,),
        in_specs=[
            pl.BlockSpec((tb, D), lambda i: (i, 0)),
            pl.BlockSpec((None, 6, D), lambda i: (geo.cond_row(i, tb), 0, 0)),
            pl.BlockSpec((tb, TOP_K), lambda i: (i, 0)),
            pl.BlockSpec((ROW_CHUNKS, TOP_K, tb, CHUNK_W), lambda i: (0, 0, i, 0)),
            full((D, D_EXPERT)), full((D, D_EXPERT)), full((D_EXPERT, D)), full((1, D)), full((1, D)),
        ],
        out_specs=out_specs,
        out_shape=out_shape,
        compiler_params=_params(("arbitrary",)),
    )(x, mod_l, wt, ytok, sg.astype(BF16), su.astype(BF16), sd.astype(BF16),
      ln_g.reshape(1, D), ln_b.reshape(1, D))


def _moe_layer(geo, x, mod_l, w_router, e_bias, layer, wg, wu, wd, sg, su, sd, ln_g, ln_b, split=False):
    t = geo.t
    top_e, w, rank, counts, h = _router(geo, x, mod_l, w_router, e_bias)
    n_blocks = (t * TOP_K) // MOE_BLK + N_EXPERTS
    n_rows = n_blocks * MOE_BLK
    padded = (counts + MOE_BLK - 1) // MOE_BLK * MOE_BLK
    pend = jnp.cumsum(padded)
    pstart = (pend - padded).astype(jnp.int32)
    blk_row0 = jnp.arange(n_blocks, dtype=jnp.int32) * MOE_BLK
    blk_e = jnp.minimum(jnp.sum((pend[None, :] <= blk_row0[:, None]).astype(jnp.int32), axis=1), N_EXPERTS - 1)
    blk_valid = jnp.clip(pstart[blk_e] + counts[blk_e] - blk_row0, 0, MOE_BLK).astype(jnp.int32)
    n_used = (pend[-1:] // MOE_BLK).astype(jnp.int32)
    slots = _slots(geo, pstart, top_e, rank)
    idx = (slots.reshape(1, TOP_K * t) + (jnp.arange(ROW_CHUNKS, dtype=jnp.int32) * n_rows)[:, None]).reshape(-1)
    win_per_chunk = TOP_K * t // SC_WINDOW
    tok_windows = t // SC_WINDOW

    def src_block(wdw):
        return (wdw // win_per_chunk) * tok_windows + (wdw % win_per_chunk) % tok_windows

    xs = _sc_scatter(h.reshape(ROW_CHUNKS * t, CHUNK_W), idx, ROW_CHUNKS * n_rows, src_block)
    yb = _ffn(xs.reshape(ROW_CHUNKS, n_rows, CHUNK_W), blk_e, blk_valid, n_used, layer, wg, wu, wd, n_blocks)
    ytok = _sc_gather(yb.reshape(ROW_CHUNKS * n_rows, CHUNK_W), idx)
    return _combine(geo, x, mod_l, w.T, ytok.reshape(ROW_CHUNKS, TOP_K, t, CHUNK_W), sg, su, sd, ln_g, ln_b,
                    split=split)


def _pos_embed(rows):
    quarter = D // 4
    omega = 1.0 / (POS_BASE ** (jnp.arange(quarter, dtype=F32) / quarter))
    r, col = jnp.meshgrid(jnp.arange(rows, dtype=F32), jnp.arange(GRID_W, dtype=F32), indexing='ij')
    r = r.reshape(-1, 1) * omega
    col = col.reshape(-1, 1) * omega
    return jnp.concatenate([jnp.sin(r), jnp.cos(r), jnp.sin(col), jnp.cos(col)], axis=-1)


def _mlstm_layer(geo, x, mod_l, j, a_w_in, a_b_gates, a_norm, a_w_out, ln_g, ln_b,
                 state_C, state_n, state_m):
    q, kt, v, so, gc, gr = _proj_a(geo, x, mod_l, a_w_in[j], a_b_gates[j])
    hp, c_p, n_p, m_p = _mlstm_scan(q, kt, v, gc, gr, row0=0, n_seq=geo.n_prompt, seq_len=geo.prompt_len)
    ns = geo.n_sample
    n0 = jnp.pad(state_n[:, j].astype(F32)[..., None], ((0, 0),) * 4 + ((0, LANES - 1),))
    m0 = jnp.pad(state_m[:, j].astype(F32), ((0, 0), (0, 0), (0, SUBLANES - NH_A)))
    m0 = jnp.broadcast_to(m0[..., None], (ns, 2, SUBLANES, LANES))
    hs, _, _, _ = _mlstm_scan(q, kt, v, gc, gr, row0=geo.t_prompt, n_seq=ns, seq_len=geo.sample_len,
                              state=(state_C[:, j].astype(F32), n0, m0))
    x1 = _out_a(geo, hp, hs, so, a_norm[j], a_w_out[j], x, mod_l, ln_g, ln_b)
    return x1, c_p, n_p[..., 0], m_p[:, :, :NH_A, 0]


def _hgrn_layer(geo, x, mod_l, j, lb_layer, b_w_in, b_norm, b_w_out, ln_g, ln_b, state_S):
    q, pre, v, sg = _proj_b(geo, x, mod_l, b_w_in[j])
    lbd = lb_layer.reshape(2, NH_B, 1, DK_B)
    op, s_p = _hgrn_scan(q, pre, v, lbd, row0=0, n_seq=geo.n_prompt, seq_len=geo.prompt_len)
    os_, _ = _hgrn_scan(q, pre, v, lbd, row0=geo.t_prompt, n_seq=geo.n_sample, seq_len=geo.sample_len,
                        state=state_S[:, j].astype(F32))
    x1 = _out_b(geo, op, os_, sg, b_norm[j], b_w_out[j], x, mod_l, ln_g, ln_b)
    return x1, s_p


def kernel(x_prompt, x_sample, state_mlstm_C, state_mlstm_n, state_mlstm_m, state_hgrn_S, c, c_ctx, w_mod, b_mod, ln_g, ln_b, a_w_in, a_b_gates, a_norm, a_w_out, b_w_in, b_lb, b_norm, b_w_out, w_router, e_bias, w_gate, w_up, w_down, ws_gate, ws_up, ws_down):
    bp, sp, _ = x_prompt.shape
    bs, ss, _ = x_sample.shape
    geo = Geometry(bp, sp, bs, ss)
    cond = jnp.zeros((COND_ROWS, D), F32).at[0].set(c_ctx).at[1:1 + bs].set(c)
    mod = _modulation(cond, w_mod, b_mod)
    x = jnp.concatenate([x_prompt.reshape(-1, D), (x_sample + _pos_embed(ss // GRID_W)[None]).reshape(-1, D)], axis=0)
    x1, new_c, new_n, new_m = _mlstm_layer(geo, x, mod[0], 0, a_w_in, a_b_gates, a_norm, a_w_out,
                                           ln_g[0, 0], ln_b[0, 0], state_mlstm_C, state_mlstm_n, state_mlstm_m)
    x2 = _moe_layer(geo, x1, mod[0], w_router[0], e_bias[0], 0, w_gate, w_up, w_down, ws_gate[0], ws_up[0], ws_down[0],
                    ln_g[0, 1], ln_b[0, 1])
    sm = jax.nn.softmax(b_lb.astype(F32), axis=0)
    lb_all = jnp.cumsum(sm, axis=0) - sm[0]
    x3, new_s = _hgrn_layer(geo, x2, mod[1], 0, lb_all[1], b_w_in, b_norm, b_w_out, ln_g[1, 0], ln_b[1, 0],
                            state_hgrn_S)
    y_p, y_s = _moe_layer(geo, x3, mod[1], w_router[1], e_bias[1], 1, w_gate, w_up, w_down, ws_gate[1], ws_up[1],
                          ws_down[1], ln_g[1, 1], ln_b[1, 1], split=True)
    y_prompt = y_p.reshape(bp, sp, D)
    y_sample = y_s.reshape(bs, ss, D)
    return y_prompt, y_sample, new_c[:, None], new_n[:, None], new_m[:, None], new_s[:, None]
```

```python
import functools

import jax
import jax.numpy as jnp
from jax import lax
from jax.experimental import pallas as pl
from jax.experimental.pallas import tpu as pltpu
from jax.experimental.pallas import tpu_sc as plsc

F32 = jnp.float32
BF16 = jnp.bfloat16
HIGHEST = lax.Precision.HIGHEST

D = 1024
DEPTH = 2
GRID_W = 64
POS_BASE = 10000.0
EPS = 1e-6
ALPHA = (2.0 * DEPTH) ** 0.25
NH_A, DK_A, DV_A = 4, 128, 256
QK_A, V_A = NH_A * DK_A, NH_A * DV_A
NH_B, DK_B = 8, 128
N_EXPERTS, TOP_K, N_GROUPS, TOPK_GROUPS = 64, 8, 8, 4
GROUP_SIZE = N_EXPERTS // N_GROUPS
D_EXPERT = D // 4
ROUTED_SCALE = 2.5

LANES = 128
SUBLANES = 8
COND_ROWS = 8
TOK_TILE = 256
CHUNK_A = 256
VMEM_LIMIT = 56 * 1024 * 1024

NT_DIMS = (((1,), (1,)), ((), ()))


def _params(sem):
    return pltpu.CompilerParams(dimension_semantics=sem, vmem_limit_bytes=VMEM_LIMIT)


def _split3(x):
    hi = x.astype(BF16)
    r = x - hi.astype(F32)
    mid = r.astype(BF16)
    lo = (r - mid.astype(F32)).astype(BF16)
    return hi, mid, lo


def _dot3(a_bf, x, transpose_side=None):
    hi, mid, lo = _split3(x)
    return (jnp.dot(a_bf, hi, preferred_element_type=F32)
            + jnp.dot(a_bf, mid, preferred_element_type=F32)
            + jnp.dot(a_bf, lo, preferred_element_type=F32))


def _dot3_r(x, a_bf):
    hi, mid, lo = _split3(x)
    return (jnp.dot(hi, a_bf, preferred_element_type=F32)
            + jnp.dot(mid, a_bf, preferred_element_type=F32)
            + jnp.dot(lo, a_bf, preferred_element_type=F32))


def _log_sigmoid(x):
    return jnp.minimum(x, 0.0) - jnp.log1p(jnp.exp(-jnp.abs(x)))


def _layer_norm_rows(x, g, b):
    mu = jnp.mean(x, axis=-1, keepdims=True)
    xc = x - mu
    var = jnp.mean(xc * xc, axis=-1, keepdims=True)
    return xc * lax.rsqrt(var + EPS) * g + b


class Geometry:
    def __init__(self, n_prompt, prompt_len, n_sample, sample_len):
        self.n_prompt, self.prompt_len = n_prompt, prompt_len
        self.n_sample, self.sample_len = n_sample, sample_len
        self.t_prompt = n_prompt * prompt_len
        self.t_sample = n_sample * sample_len
        self.t = self.t_prompt + self.t_sample
        assert prompt_len % TOK_TILE == 0 and sample_len % TOK_TILE == 0
        assert n_sample + 1 <= COND_ROWS

    def cond_row(self, tile, tile_rows):
        n_p = self.t_prompt // tile_rows
        return jnp.where(tile < n_p, 0, 1 + (tile - n_p) // (self.sample_len // tile_rows))


def _mod_kernel(cond_ref, w_ref, b_ref, o_ref):
    c = cond_ref[...]
    s = c * jax.nn.sigmoid(c)
    o_ref[0, 0] = jnp.dot(s, w_ref[0], precision=HIGHEST, preferred_element_type=F32) + b_ref[0, 0]


def _modulation(cond, w_mod, b_mod):
    out = pl.pallas_call(
        _mod_kernel,
        name="modulation",
        grid=(DEPTH, 6),
        in_specs=[
            pl.BlockSpec((COND_ROWS, D), lambda l, j: (0, 0)),
            pl.BlockSpec((1, D, D), lambda l, j: (l, 0, j)),
            pl.BlockSpec((1, 1, 1, D), lambda l, j: (l, j, 0, 0)),
        ],
        out_specs=pl.BlockSpec((1, 1, COND_ROWS, D), lambda l, j: (l, j, 0, 0)),
        out_shape=jax.ShapeDtypeStruct((DEPTH, 6, COND_ROWS, D), F32),
        compiler_params=_params(("arbitrary", "arbitrary")),
    )(cond, w_mod, b_mod.reshape(DEPTH, 6, 1, D))
    return out.transpose(0, 2, 1, 3)


def _embed_specs(geo, tb):
    n_p = geo.t_prompt // tb
    per_seq = geo.sample_len // tb
    return [pl.BlockSpec((tb, D), lambda i: (jnp.minimum(i, n_p - 1), 0)),
            pl.BlockSpec((tb, D), lambda i: (jnp.maximum(i - n_p, 0), 0)),
            pl.BlockSpec((tb, D), lambda i: (jnp.maximum(i - n_p, 0) % per_seq, 0))]


def _embed_tile(xp_ref, xs_ref, pos_ref, n_prompt_tiles):
    return jnp.where(pl.program_id(0) < n_prompt_tiles, xp_ref[...], xs_ref[...] + pos_ref[...])


def _proj_a_kernel(xp_ref, xs_ref, pos_ref, mod_ref, wq_ref, wkt_ref, wvo_ref, wg_ref, wgt_ref, bg_ref, bgt_ref,
                   q_ref, kt_ref, v_ref, so_ref, gc_ref, gr_ref, *, n_prompt_tiles):
    x = _embed_tile(xp_ref, xs_ref, pos_ref, n_prompt_tiles)
    h = x * (1.0 + mod_ref[1:2, :]) + mod_ref[0:1, :]
    hb = h.astype(BF16)
    q_ref[...] = jnp.dot(hb, wq_ref[...], preferred_element_type=F32).astype(BF16)
    kt = lax.dot_general(wkt_ref[...], hb, NT_DIMS, preferred_element_type=F32)
    kt_ref[...] = (kt * (DK_A ** -0.5)).astype(BF16)
    vo = jnp.dot(hb, wvo_ref[...], preferred_element_type=F32)
    v_ref[...] = vo[:, :V_A].astype(BF16)
    so_ref[...] = jax.nn.sigmoid(vo[:, V_A:]).astype(BF16)
    gc_ref[...] = jnp.dot(h, wg_ref[...], precision=HIGHEST, preferred_element_type=F32) + bg_ref[...]
    gr_ref[...] = lax.dot_general(wgt_ref[...], h, NT_DIMS, precision=HIGHEST,
                                  preferred_element_type=F32) + bgt_ref[...]


def _proj_a(geo, x, mod_l, w_in, b_gates):
    t = geo.t
    n_gate = 4 * NH_A
    wq = w_in[:, :QK_A].astype(BF16)
    wkt = w_in[:, QK_A:2 * QK_A].T.astype(BF16)
    wvo = w_in[:, 2 * QK_A:2 * QK_A + 2 * V_A].astype(BF16)
    wg = w_in[:, 2 * QK_A + 2 * V_A:]
    wg_pad = jnp.pad(wg, ((0, 0), (0, LANES - n_gate)))
    bg = b_gates.reshape(n_gate).astype(F32)
    bg_pad = jnp.pad(bg, (0, LANES - n_gate)).reshape(1, LANES)
    tb = TOK_TILE
    full = lambda shape: pl.BlockSpec(shape, lambda i: (0,) * len(shape))
    return pl.pallas_call(
        functools.partial(_proj_a_kernel, n_prompt_tiles=geo.t_prompt // tb),
        name="proj_a",
        grid=(t // tb,),
        in_specs=_embed_specs(geo, tb) + [
            pl.BlockSpec((None, 6, D), lambda i: (geo.cond_row(i, tb), 0, 0)),
            full((D, QK_A)), full((QK_A, D)), full((D, 2 * V_A)), full((D, LANES)), full((n_gate, D)),
            full((1, LANES)), full((n_gate, 1)),
        ],
        out_specs=[
            pl.BlockSpec((tb, QK_A), lambda i: (i, 0)),
            pl.BlockSpec((QK_A, tb), lambda i: (0, i)),
            pl.BlockSpec((tb, V_A), lambda i: (i, 0)),
            pl.BlockSpec((tb, V_A), lambda i: (i, 0)),
            pl.BlockSpec((tb, LANES), lambda i: (i, 0)),
            pl.BlockSpec((n_gate, tb), lambda i: (0, i)),
        ],
        out_shape=[
            jax.ShapeDtypeStruct((t, QK_A), BF16),
            jax.ShapeDtypeStruct((QK_A, t), BF16),
            jax.ShapeDtypeStruct((t, V_A), BF16),
            jax.ShapeDtypeStruct((t, V_A), BF16),
            jax.ShapeDtypeStruct((t, LANES), F32),
            jax.ShapeDtypeStruct((n_gate, t), F32),
        ],
        compiler_params=_params(("parallel",)),
    )(*x, mod_l, wq, wkt, wvo, wg_pad, wg.T, bg_pad, bg.reshape(n_gate, 1))


def _mlstm_scan_kernel(*refs, chunk, has_state):
    if has_state:
        (q_ref, kt_ref, v_ref, gc_ref, gr_ref, c0_ref, n0_ref, m0_ref,
         h_ref, c_out, n_out, m_out, c_sc, n_sc, m_sc) = refs
    else:
        (q_ref, kt_ref, v_ref, gc_ref, gr_ref,
         h_ref, c_out, n_out, m_out, c_sc, n_sc, m_sc) = refs
    L = chunk
    d = pl.program_id(1)
    c = pl.program_id(2)
    fwd = d == 0

    @pl.when(c == 0)
    def _():
        if has_state:
            c_sc[...] = c0_ref[0, 0]
            n_sc[...] = n0_ref[0, 0]
            m_sc[...] = m0_ref[0, 0]
        else:
            c_sc[...] = jnp.zeros_like(c_sc)
            n_sc[...] = jnp.zeros_like(n_sc)
            m_sc[...] = jnp.zeros_like(m_sc)

    row = lax.broadcasted_iota(jnp.int32, (L, L), 0)
    col = lax.broadcasted_iota(jnp.int32, (L, L), 1)
    sgn = 1 - 2 * d
    causal = (row - col) * sgn >= 0
    tri = causal.astype(BF16)
    tri_t = ((col - row) * sgn >= 0).astype(BF16)

    gc = gc_ref[...]
    gr = gr_ref[...]
    bc_all = _dot3(tri, _log_sigmoid(gc))
    br_all = _dot3_r(_log_sigmoid(gr), tri_t)
    ones_blk = (lax.broadcasted_iota(jnp.int32, (L, LANES), 1) == 0).astype(BF16)

    for h in range(NH_A):
        b_c = jnp.where(fwd, bc_all[:, 4 + h:5 + h], bc_all[:, 12 + h:13 + h])
        b_r = jnp.where(fwd, br_all[4 + h:5 + h, :], br_all[12 + h:13 + h, :])
        i_r = jnp.where(fwd, gr[h:h + 1, :], gr[8 + h:9 + h, :])
        bl = jnp.where(fwd, b_r[:, L - 1:L], b_r[:, 0:1])
        q = q_ref[:, h * DK_A:(h + 1) * DK_A]
        kt = kt_ref[h * DK_A:(h + 1) * DK_A, :]
        v = v_ref[:, h * DV_A:(h + 1) * DV_A]
        m = m_sc[h:h + 1, 0:1]
        cst = c_sc[h]
        nst = n_sc[h]

        a_r = i_r - b_r
        logd = jnp.where(causal, b_c + a_r, -jnp.inf)
        inter = b_c + m
        m_t = jnp.maximum(inter, jnp.max(logd, axis=1, keepdims=True))
        dmat = jnp.exp(logd - m_t)
        e_int = jnp.exp(inter - m_t)
        s = (jnp.dot(q, kt, preferred_element_type=F32) * dmat).astype(BF16)
        num = (jnp.dot(s, v, preferred_element_type=F32)
               + e_int * jnp.dot(q, cst.astype(BF16), preferred_element_type=F32))
        den = (jnp.dot(s, ones_blk, preferred_element_type=F32)
               + e_int * jnp.dot(q, nst.astype(BF16), preferred_element_type=F32))[:, 0:1]
        h_ref[:, h * DV_A:(h + 1) * DV_A] = num / jnp.maximum(jnp.abs(den), jnp.exp(-m_t))

        logw = bl + a_r
        m_new = jnp.maximum(bl + m, jnp.max(logw, axis=1, keepdims=True))
        w = jnp.exp(logw - m_new)
        decay = jnp.exp(bl + m - m_new)
        kw = (kt.astype(F32) * w).astype(BF16)
        c_sc[h] = decay * cst + jnp.dot(kw, v, preferred_element_type=F32)
        n_sc[h] = decay * nst + jnp.dot(kw, ones_blk, preferred_element_type=F32)
        m_sc[h:h + 1, :] = jnp.broadcast_to(m_new, (1, LANES))

    @pl.when(c == pl.num_programs(2) - 1)
    def _():
        c_out[0, 0] = c_sc[...]
        n_out[0, 0] = n_sc[...]
        m_out[0, 0] = m_sc[...]


def _mlstm_scan(q, kt, v, gc, gr, *, row0, n_seq, seq_len, state=None):
    L = CHUNK_A
    nc = seq_len // L
    blk0 = row0 // L

    def loc_blk(b, d, c):
        return b * nc + c + d * (nc - 1 - 2 * c)

    def tok_blk(b, d, c):
        return blk0 + loc_blk(b, d, c)

    in_specs = [
        pl.BlockSpec((L, QK_A), lambda b, d, c: (tok_blk(b, d, c), 0)),
        pl.BlockSpec((QK_A, L), lambda b, d, c: (0, tok_blk(b, d, c))),
        pl.BlockSpec((L, V_A), lambda b, d, c: (tok_blk(b, d, c), 0)),
        pl.BlockSpec((L, LANES), lambda b, d, c: (tok_blk(b, d, c), 0)),
        pl.BlockSpec((4 * NH_A, L), lambda b, d, c: (0, tok_blk(b, d, c))),
    ]
    args = [q, kt, v, gc, gr]
    if state is not None:
        in_specs += [
            pl.BlockSpec((1, 1, NH_A, DK_A, DV_A), lambda b, d, c: (b, d, 0, 0, 0)),
            pl.BlockSpec((1, 1, NH_A, DK_A, LANES), lambda b, d, c: (b, d, 0, 0, 0)),
            pl.BlockSpec((1, 1, SUBLANES, LANES), lambda b, d, c: (b, d, 0, 0)),
        ]
        args += list(state)
    return pl.pallas_call(
        functools.partial(_mlstm_scan_kernel, chunk=L, has_state=state is not None),
        name="mlstm_scan_seeded" if state is not None else "mlstm_scan",
        grid=(n_seq, 2, nc),
        in_specs=in_specs,
        out_specs=[
            pl.BlockSpec((None, L, V_A), lambda b, d, c: (d, loc_blk(b, d, c), 0)),
            pl.BlockSpec((1, 1, NH_A, DK_A, DV_A), lambda b, d, c: (b, d, 0, 0, 0)),
            pl.BlockSpec((1, 1, NH_A, DK_A, LANES), lambda b, d, c: (b, d, 0, 0, 0)),
            pl.BlockSpec((1, 1, SUBLANES, LANES), lambda b, d, c: (b, d, 0, 0)),
        ],
        out_shape=[
            jax.ShapeDtypeStruct((2, n_seq * seq_len, V_A), F32),
            jax.ShapeDtypeStruct((n_seq, 2, NH_A, DK_A, DV_A), F32),
            jax.ShapeDtypeStruct((n_seq, 2, NH_A, DK_A, LANES), F32),
            jax.ShapeDtypeStruct((n_seq, 2, SUBLANES, LANES), F32),
        ],
        scratch_shapes=[
            pltpu.VMEM((NH_A, DK_A, DV_A), F32),
            pltpu.VMEM((NH_A, DK_A, LANES), F32),
            pltpu.VMEM((SUBLANES, LANES), F32),
        ],
        compiler_params=_params(("parallel", "parallel", "arbitrary")),
    )(*args)


def _out_a_kernel(hp_ref, hs_ref, so_ref, nw_ref, w_ref, xp_ref, xs_ref, pos_ref, mod_ref, lg_ref, lb_ref, o_ref, *,
                  n_prompt_tiles):
    is_prompt = pl.program_id(0) < n_prompt_tiles
    x = _embed_tile(xp_ref, xs_ref, pos_ref, n_prompt_tiles)
    y = jnp.where(is_prompt, hp_ref[0] + hp_ref[1], hs_ref[0] + hs_ref[1])
    parts = []
    for h in range(NH_A):
        yh = y[:, h * DV_A:(h + 1) * DV_A]
        mu = jnp.mean(yh, axis=-1, keepdims=True)
        yc = yh - mu
        var = jnp.mean(yc * yc, axis=-1, keepdims=True)
        parts.append(yc * lax.rsqrt(var + EPS))
    yn = jnp.concatenate(parts, axis=-1) * nw_ref[...] * so_ref[...].astype(F32)
    out = jnp.dot(yn.astype(BF16), w_ref[...], preferred_element_type=F32)
    o_ref[...] = _layer_norm_rows(ALPHA * x + mod_ref[2:3, :] * out, lg_ref[...], lb_ref[...])


def _out_a(geo, h_prompt, h_sample, so, norm_w, w_out, x, mod_l, ln_g, ln_b):
    t = geo.t
    tb = TOK_TILE
    n_p = geo.t_prompt // tb
    full = lambda shape: pl.BlockSpec(shape, lambda i: (0,) * len(shape))
    return pl.pallas_call(
        functools.partial(_out_a_kernel, n_prompt_tiles=n_p),
        name="out_a",
        grid=(t // tb,),
        in_specs=[
            pl.BlockSpec((2, tb, V_A), lambda i: (0, jnp.minimum(i, n_p - 1), 0)),
            pl.BlockSpec((2, tb, V_A), lambda i: (0, jnp.maximum(i - n_p, 0), 0)),
            pl.BlockSpec((tb, V_A), lambda i: (i, 0)),
            full((1, V_A)), full((V_A, D)),
        ] + _embed_specs(geo, tb) + [
            pl.BlockSpec((None, 6, D), lambda i: (geo.cond_row(i, tb), 0, 0)),
            full((1, D)), full((1, D)),
        ],
        out_specs=pl.BlockSpec((tb, D), lambda i: (i, 0)),
        out_shape=jax.ShapeDtypeStruct((t, D), F32),
        compiler_params=_params(("parallel",)),
    )(h_prompt, h_sample, so, norm_w.reshape(1, V_A).astype(F32), w_out.astype(BF16), *x, mod_l,
      ln_g.reshape(1, D), ln_b.reshape(1, D))


def _proj_b_kernel(x_ref, mod_ref, w_ref, q_ref, pre_ref, v_ref, sg_ref):
    h = x_ref[...] * (1.0 + mod_ref[1:2, :]) + mod_ref[0:1, :]
    z = jnp.dot(h.astype(BF16), w_ref[...], preferred_element_type=F32)
    for hd in range(NH_B):
        lo = hd * DK_B
        qh = z[:, lo:lo + DK_B]
        q_ref[hd] = qh * jax.nn.sigmoid(qh)
        pre_ref[0, hd] = z[:, D + lo:D + lo + DK_B]
        pre_ref[1, hd] = z[:, 2 * D + lo:2 * D + lo + DK_B]
        v_ref[hd] = z[:, 3 * D + lo:3 * D + lo + DK_B].astype(BF16)
    g = z[:, 4 * D:]
    sg_ref[...] = (g * jax.nn.sigmoid(g)).astype(BF16)


def _proj_b(geo, x, mod_l, w_in):
    t = geo.t
    tb = TOK_TILE
    return pl.pallas_call(
        _proj_b_kernel,
        name="proj_b",
        grid=(t // tb,),
        in_specs=[
            pl.BlockSpec((tb, D), lambda i: (i, 0)),
            pl.BlockSpec((None, 6, D), lambda i: (geo.cond_row(i, tb), 0, 0)),
            pl.BlockSpec((D, 5 * D), lambda i: (0, 0)),
        ],
        out_specs=[
            pl.BlockSpec((NH_B, tb, DK_B), lambda i: (0, i, 0)),
            pl.BlockSpec((2, NH_B, tb, DK_B), lambda i: (0, 0, i, 0)),
            pl.BlockSpec((NH_B, tb, DK_B), lambda i: (0, i, 0)),
            pl.BlockSpec((tb, D), lambda i: (i, 0)),
        ],
        out_shape=[
            jax.ShapeDtypeStruct((NH_B, t, DK_B), F32),
            jax.ShapeDtypeStruct((2, NH_B, t, DK_B), F32),
            jax.ShapeDtypeStruct((NH_B, t, DK_B), BF16),
            jax.ShapeDtypeStruct((t, D), BF16),
        ],
        compiler_params=_params(("parallel",)),
    )(x, mod_l, w_in.astype(BF16))


CHUNK_B = 128
BAND = SUBLANES // 2
TN_DIMS = (((0,), (0,)), ((), ()))


def _hgrn_head(q, pre, lbv, v_bf, st, fwd):
    L = q.shape[0]
    sg = jax.nn.sigmoid(pre)
    f = lbv + (1.0 - lbv) * sg
    lf = jnp.log(f)
    kk = (1.0 - lbv) * (1.0 - sg)
    row = lax.broadcasted_iota(jnp.int32, (L, L), 0)
    col = lax.broadcasted_iota(jnp.int32, (L, L), 1)
    tri = ((row >= col) if fwd else (row <= col)).astype(BF16)
    b = _dot3(tri, lf)
    tpos = lax.broadcasted_iota(jnp.int32, (L, DK_B), 0)
    blk_bits = row ^ col
    lag = jnp.where(blk_bits < BAND, (row - col) if fwd else (col - row), -1)

    step = 1 if fwd else L - 1
    att = jnp.where(lag == 0, jnp.sum(q * kk, axis=1, keepdims=True), 0.0)
    f_r, kk_r, g = f, kk, f
    for dl in range(1, BAND):
        if dl > 1:
            f_r = pltpu.roll(f_r, step, 0)
            g = g * f_r
        kk_r = pltpu.roll(kk_r, step, 0)
        att = jnp.where(lag == dl, jnp.sum(q * kk_r * g, axis=1, keepdims=True), att)

    w = BAND
    while w < L:
        nb = L // (2 * w)
        b3 = b.reshape(nb, 2 * w, DK_B)
        edge = (b3[:, w - 1:w, :] if fwd else b3[:, w:w + 1, :])
        bmid = jnp.broadcast_to(edge, (nb, 2 * w, DK_B)).reshape(L, DK_B)
        second = (tpos & w) != 0
        t_side = second if fwd else jnp.logical_not(second)
        e = jnp.exp(jnp.where(t_side, b - bmid, bmid - b))
        qt = jnp.where(t_side, q * e, 0.0).astype(BF16)
        ks = jnp.where(t_side, 0.0, kk * e).astype(BF16)
        a = lax.dot_general(qt, ks, NT_DIMS, preferred_element_type=F32)
        att = att + jnp.where(blk_bits < 2 * w, a, 0.0)
        w *= 2
    o = jnp.dot(att.astype(BF16), v_bf, preferred_element_type=F32)

    bl = b[L - 1:L, :] if fwd else b[0:1, :]
    o = o + lax.dot_general((q * jnp.exp(b)).astype(BF16), st.astype(BF16), NT_DIMS, preferred_element_type=F32)
    kd = (kk * jnp.exp(bl - b)).astype(BF16)
    st_new = jnp.exp(bl) * st + lax.dot_general(v_bf, kd, TN_DIMS, preferred_element_type=F32)
    return o, st_new


def _hgrn_scan_kernel(*refs, has_state):
    if has_state:
        q_ref, pre_ref, v_ref, lb_ref, s0_ref, o_ref, s_out, st_sc = refs
    else:
        q_ref, pre_ref, v_ref, lb_ref, o_ref, s_out, st_sc = refs
    d = pl.program_id(1)
    c = pl.program_id(2)

    @pl.when(c == 0)
    def _():
        if has_state:
            for hd in range(NH_B):
                st_sc[hd] = s0_ref[0, 0, hd].T
        else:
            st_sc[...] = jnp.zeros_like(st_sc)

    def run(fwd):
        def head(hd, carry):
            o, st_new = _hgrn_head(q_ref[hd], pre_ref[hd], lb_ref[hd], v_ref[hd], st_sc[hd], fwd)
            o_ref[hd] = o
            st_sc[hd] = st_new
            return carry
        lax.fori_loop(0, NH_B, head, 0, unroll=8)

    @pl.when(d == 0)
    def _():
        run(True)

    @pl.when(d == 1)
    def _():
        run(False)

    @pl.when(c == pl.num_programs(2) - 1)
    def _():
        for hd in range(NH_B):
            s_out[0, 0, hd] = st_sc[hd].T


def _hgrn_scan(q, pre, v, lbd, *, row0, n_seq, seq_len, state=None):
    L = CHUNK_B
    nc = seq_len // L
    blk0 = row0 // L

    def loc_blk(b, d, c):
        return b * nc + c + d * (nc - 1 - 2 * c)

    def tok_blk(b, d, c):
        return blk0 + loc_blk(b, d, c)

    in_specs = [
        pl.BlockSpec((NH_B, L, DK_B), lambda b, d, c: (0, tok_blk(b, d, c), 0)),
        pl.BlockSpec((None, NH_B, L, DK_B), lambda b, d, c: (d, 0, tok_blk(b, d, c), 0)),
        pl.BlockSpec((NH_B, L, DK_B), lambda b, d, c: (0, tok_blk(b, d, c), 0)),
        pl.BlockSpec((None, NH_B, 1, DK_B), lambda b, d, c: (d, 0, 0, 0)),
    ]
    args = [q, pre, v, lbd]
    if state is not None:
        in_specs.append(pl.BlockSpec((1, 1, NH_B, DK_B, DK_B), lambda b, d, c: (b, d, 0, 0, 0)))
        args.append(state)
    return pl.pallas_call(
        functools.partial(_hgrn_scan_kernel, has_state=state is not None),
        name="hgrn_scan_seeded" if state is not None else "hgrn_scan",
        grid=(n_seq, 2, nc),
        in_specs=in_specs,
        out_specs=[
            pl.BlockSpec((None, NH_B, L, DK_B), lambda b, d, c: (d, 0, loc_blk(b, d, c), 0)),
            pl.BlockSpec((1, 1, NH_B, DK_B, DK_B), lambda b, d, c: (b, d, 0, 0, 0)),
        ],
        out_shape=[
            jax.ShapeDtypeStruct((2, NH_B, n_seq * seq_len, DK_B), F32),
            jax.ShapeDtypeStruct((n_seq, 2, NH_B, DK_B, DK_B), F32),
        ],
        scratch_shapes=[pltpu.VMEM((NH_B, DK_B, DK_B), F32)],
        compiler_params=_params(("parallel", "parallel", "arbitrary")),
    )(*args)


def _out_b_kernel(op_ref, os_ref, sg_ref, nw_ref, w_ref, x_ref, mod_ref, lg_ref, lb_ref, out_ref, *, n_prompt_tiles):
    is_prompt = pl.program_id(0) < n_prompt_tiles
    parts = []
    for hd in range(NH_B):
        y = jnp.where(is_prompt, op_ref[0, hd] + op_ref[1, hd], os_ref[0, hd] + os_ref[1, hd])
        parts.append(y * lax.rsqrt(jnp.mean(y * y, axis=-1, keepdims=True) + EPS))
    yn = jnp.concatenate(parts, axis=-1) * nw_ref[...] * sg_ref[...].astype(F32)
    out = jnp.dot(yn.astype(BF16), w_ref[...], preferred_element_type=F32)
    out_ref[...] = _layer_norm_rows(ALPHA * x_ref[...] + mod_ref[2:3, :] * out, lg_ref[...], lb_ref[...])


def _out_b(geo, o_prompt, o_sample, sg, norm_w, w_out, x, mod_l, ln_g, ln_b):
    t = geo.t
    tb = TOK_TILE
    n_p = geo.t_prompt // tb
    full = lambda shape: pl.BlockSpec(shape, lambda i: (0,) * len(shape))
    return pl.pallas_call(
        functools.partial(_out_b_kernel, n_prompt_tiles=n_p),
        name="out_b",
        grid=(t // tb,),
        in_specs=[
            pl.BlockSpec((2, NH_B, tb, DK_B), lambda i: (0, 0, jnp.minimum(i, n_p - 1), 0)),
            pl.BlockSpec((2, NH_B, tb, DK_B), lambda i: (0, 0, jnp.maximum(i - n_p, 0), 0)),
            pl.BlockSpec((tb, D), lambda i: (i, 0)),
            full((1, D)), full((D, D)),
            pl.BlockSpec((tb, D), lambda i: (i, 0)),
            pl.BlockSpec((None, 6, D), lambda i: (geo.cond_row(i, tb), 0, 0)),
            full((1, D)), full((1, D)),
        ],
        out_specs=pl.BlockSpec((tb, D), lambda i: (i, 0)),
        out_shape=jax.ShapeDtypeStruct((t, D), F32),
        compiler_params=_params(("parallel",)),
    )(o_prompt, o_sample, sg, norm_w.reshape(1, D).astype(F32), w_out.astype(BF16), x, mod_l,
      ln_g.reshape(1, D), ln_b.reshape(1, D))


MOE_BLK = 512
U32 = jnp.uint32
ROW_WORDS = D // 2
CHUNK_W = 256
ROW_CHUNKS = ROW_WORDS // CHUNK_W
SC_WINDOW = 128


def _pack_rows(x):
    hi = pltpu.bitcast(x[:, :ROW_WORDS].astype(BF16).astype(F32), U32)
    lo = pltpu.bitcast(x[:, ROW_WORDS:].astype(BF16).astype(F32), U32)
    return hi | (lo >> 16)


def _unpack_rows(words):
    hi = pltpu.bitcast(words & jnp.uint32(0xFFFF0000), F32)
    lo = pltpu.bitcast(words << 16, F32)
    return jnp.concatenate([hi, lo], axis=1)


def _store_chunks(chunk_ref, x):
    words = _pack_rows(x)
    for c in range(ROW_CHUNKS):
        chunk_ref(c)[...] = words[:, c * CHUNK_W:(c + 1) * CHUNK_W]


def _load_chunks(chunk_ref):
    return _unpack_rows(jnp.concatenate([chunk_ref(c)[...] for c in range(ROW_CHUNKS)], axis=1))


def _first_index(hit, iota, size, axis):
    return jnp.min(jnp.where(hit, iota, size), axis=axis, keepdims=True)


def _router_kernel(x_ref, mod_ref, wrt_ref, eb_ref, e_ref, w_ref, r_ref, cnt_ref, h_ref, cnt_sc):
    i = pl.program_id(0)
    tb = x_ref.shape[0]

    @pl.when(i == 0)
    def _():
        cnt_sc[...] = jnp.zeros_like(cnt_sc)

    h = x_ref[...] * (1.0 + mod_ref[4:5, :]) + mod_ref[3:4, :]
    _store_chunks(lambda c: h_ref.at[c], h)
    logits = lax.dot_general(wrt_ref[...], h, NT_DIMS, precision=HIGHEST, preferred_element_type=F32)
    scores = jax.nn.sigmoid(logits)
    sel = scores + eb_ref[...]

    g3 = sel.reshape(N_GROUPS, GROUP_SIZE, tb)
    io3 = lax.broadcasted_iota(jnp.int32, g3.shape, 1)
    m1 = jnp.max(g3, axis=1, keepdims=True)
    first = _first_index(g3 == m1, io3, GROUP_SIZE, 1)
    m2 = jnp.max(jnp.where(io3 == first, -jnp.inf, g3), axis=1, keepdims=True)
    gscore = (m1 + m2).reshape(N_GROUPS, tb)

    iog = lax.broadcasted_iota(jnp.int32, gscore.shape, 0)
    gmask = jnp.zeros(gscore.shape, F32)
    for _ in range(TOPK_GROUPS):
        gm = jnp.max(gscore, axis=0, keepdims=True)
        pick = iog == _first_index(gscore == gm, iog, N_GROUPS, 0)
        gmask = jnp.where(pick, 1.0, gmask)
        gscore = jnp.where(pick, -jnp.inf, gscore)
    emask = jnp.broadcast_to(gmask.reshape(N_GROUPS, 1, tb), (N_GROUPS, GROUP_SIZE, tb)).reshape(N_EXPERTS, tb)
    cand = jnp.where(emask > 0.0, sel, -jnp.inf)

    ioe = lax.broadcasted_iota(jnp.int32, cand.shape, 0)
    picks, wts = [], []
    onehot = jnp.zeros(cand.shape, F32)
    for _ in range(TOP_K):
        cm = jnp.max(cand, axis=0, keepdims=True)
        idx = _first_index(cand == cm, ioe, N_EXPERTS, 0)
        pick = ioe == idx
        picks.append(pick)
        wts.append(jnp.sum(jnp.where(pick, scores, 0.0), axis=0, keepdims=True))
        onehot = onehot + pick.astype(F32)
        cand = jnp.where(pick, -jnp.inf, cand)
        e_ref[pl.ds(len(picks) - 1, 1), :] = idx
    wsum = wts[0]
    for wk in wts[1:]:
        wsum = wsum + wk
    for k in range(TOP_K):
        w_ref[pl.ds(k, 1), :] = wts[k] / wsum * ROUTED_SCALE

    r_io = lax.broadcasted_iota(jnp.int32, (tb, tb), 0)
    c_io = lax.broadcasted_iota(jnp.int32, (tb, tb), 1)
    before = (r_io < c_io).astype(BF16)
    rank = cnt_sc[:, 0:1] + jnp.dot(onehot.astype(BF16), before, preferred_element_type=F32)
    for k in range(TOP_K):
        r_ref[pl.ds(k, 1), :] = jnp.sum(jnp.where(picks[k], rank, 0.0), axis=0, keepdims=True).astype(jnp.int32)
    cnt_sc[...] = cnt_sc[...] + jnp.sum(onehot, axis=1, keepdims=True)
    cnt_ref[...] = cnt_sc[...]


def _router(geo, x, mod_l, w_router, e_bias):
    t = geo.t
    tb = TOK_TILE
    full = lambda shape: pl.BlockSpec(shape, lambda i: (0,) * len(shape))
    e, w, r, cnt, h = pl.pallas_call(
        _router_kernel,
        name="router",
        grid=(t // tb,),
        in_specs=[
            pl.BlockSpec((tb, D), lambda i: (i, 0)),
            pl.BlockSpec((None, 6, D), lambda i: (geo.cond_row(i, tb), 0, 0)),
            full((N_EXPERTS, D)), full((N_EXPERTS, 1)),
        ],
        out_specs=[
            pl.BlockSpec((TOP_K, tb), lambda i: (0, i)),
            pl.BlockSpec((TOP_K, tb), lambda i: (0, i)),
            pl.BlockSpec((TOP_K, tb), lambda i: (0, i)),
            full((N_EXPERTS, LANES)),
            pl.BlockSpec((ROW_CHUNKS, tb, CHUNK_W), lambda i: (0, i, 0)),
        ],
        out_shape=[
            jax.ShapeDtypeStruct((TOP_K, t), jnp.int32),
            jax.ShapeDtypeStruct((TOP_K, t), F32),
            jax.ShapeDtypeStruct((TOP_K, t), jnp.int32),
            jax.ShapeDtypeStruct((N_EXPERTS, LANES), F32),
            jax.ShapeDtypeStruct((ROW_CHUNKS, t, CHUNK_W), U32),
        ],
        scratch_shapes=[pltpu.VMEM((N_EXPERTS, LANES), F32)],
        compiler_params=_params(("arbitrary",)),
    )(x, mod_l, w_router.T.astype(F32), e_bias.reshape(N_EXPERTS, 1).astype(F32))
    return e, w, r, cnt[:, 0].astype(jnp.int32), h


def _slot_kernel(pstart_ref, e_ref, r_ref, o_ref):
    e = e_ref[...]
    slot = r_ref[...]
    for x in range(N_EXPERTS):
        slot = slot + jnp.where(e == x, pstart_ref[x], 0)
    o_ref[...] = slot


def _slots(geo, pstart, top_e, rank):
    tb = TOK_TILE
    return pl.pallas_call(
        _slot_kernel,
        name="slots",
        grid_spec=pltpu.PrefetchScalarGridSpec(
            num_scalar_prefetch=1,
            grid=(geo.t // tb,),
            in_specs=[pl.BlockSpec((TOP_K, tb), lambda i, p: (0, i)),
                      pl.BlockSpec((TOP_K, tb), lambda i, p: (0, i))],
            out_specs=pl.BlockSpec((TOP_K, tb), lambda i, p: (0, i)),
        ),
        out_shape=jax.ShapeDtypeStruct((TOP_K, geo.t), jnp.int32),
        compiler_params=_params(("parallel",)),
    )(pstart, top_e, rank)


def _block_meta_kernel(pstart_ref, counts_ref, pend_ref, e_ref, v_ref):
    row0 = lax.broadcasted_iota(jnp.int32, e_ref.shape, 1) * MOE_BLK
    blk_e = jnp.zeros(e_ref.shape, jnp.int32)
    for x in range(N_EXPERTS):
        blk_e = blk_e + jnp.where(pend_ref[x] <= row0, 1, 0)
    blk_e = jnp.minimum(blk_e, N_EXPERTS - 1)
    last = jnp.zeros(e_ref.shape, jnp.int32)
    for x in range(N_EXPERTS):
        last = last + jnp.where(blk_e == x, pstart_ref[x] + counts_ref[x], 0)
    e_ref[...] = blk_e
    v_ref[...] = jnp.clip(last - row0, 0, MOE_BLK)


def _block_meta(pstart, counts, pend, n_blocks):
    e, v = pl.pallas_call(
        _block_meta_kernel,
        name="block_meta",
        grid_spec=pltpu.PrefetchScalarGridSpec(
            num_scalar_prefetch=3,
            grid=(1,),
            in_specs=[],
            out_specs=[pl.BlockSpec((1, n_blocks), lambda i, a, b, c: (0, 0)),
                       pl.BlockSpec((1, n_blocks), lambda i, a, b, c: (0, 0))],
        ),
        out_shape=[jax.ShapeDtypeStruct((1, n_blocks), jnp.int32), jax.ShapeDtypeStruct((1, n_blocks), jnp.int32)],
        compiler_params=_params(("arbitrary",)),
    )(pstart, counts, pend)
    return e[0], v[0]


def _sc_mesh():
    return plsc.VectorSubcoreMesh(core_axis_name="core", subcore_axis_name="subcore")


def _sc_scatter(rows, idx, n_out, src_block):
    n_idx = idx.shape[0]

    @pl.kernel(out_type=jax.ShapeDtypeStruct((n_out, CHUNK_W), rows.dtype), mesh=_sc_mesh(), scratch_types=[],
               name="sc_dispatch")
    def scatter(x_hbm, i_hbm, o_hbm):
        def body(x_vmem, i_vmem):
            pltpu.sync_copy(x_vmem, o_hbm.at[i_vmem.at[0]])

        pltpu.emit_pipeline(
            body,
            grid=(n_idx // SC_WINDOW,),
            in_specs=[pl.BlockSpec((SC_WINDOW, CHUNK_W), index_map=lambda w: (src_block(w), 0)),
                      pl.BlockSpec((1, SC_WINDOW), index_map=lambda w: (0, w))],
            out_specs=[],
            core_axis_name=("core", "subcore"),
            dimension_semantics=(pltpu.PARALLEL,),
        )(x_hbm, i_hbm)

    return scatter(rows, idx.reshape(1, n_idx))


def _sc_gather(table, idx):
    n_idx = idx.shape[0]

    @pl.kernel(out_type=jax.ShapeDtypeStruct((n_idx, CHUNK_W), table.dtype), mesh=_sc_mesh(),
               name="sc_combine_gather")
    def gather(t_hbm, i_hbm, o_hbm):
        def body(i_vmem, o_vmem):
            pltpu.sync_copy(t_hbm.at[i_vmem.at[0]], o_vmem)

        pltpu.emit_pipeline(
            body,
            grid=(n_idx // SC_WINDOW,),
            in_specs=[pl.BlockSpec((1, SC_WINDOW), index_map=lambda w: (0, w))],
            out_specs=[pl.BlockSpec((SC_WINDOW, CHUNK_W), index_map=lambda w: (w, 0))],
            core_axis_name=("core", "subcore"),
            dimension_semantics=(pltpu.PARALLEL,),
        )(i_hbm, o_hbm)

    return gather(table, idx.reshape(1, n_idx))


def _ffn_kernel(blk_e_ref, blk_valid_ref, n_used_ref, xs_ref, wg_ref, wu_ref, wd_ref, y_ref, wg_sc, wu_sc, wd_sc):
    b = pl.program_id(0)
    used = b < n_used_ref[0]
    new_expert = (b == 0) | (blk_e_ref[b] != blk_e_ref[jnp.maximum(b - 1, 0)])

    @pl.when(used & new_expert)
    def _():
        wg_sc[...] = wg_ref[...].astype(BF16)
        wu_sc[...] = wu_ref[...].astype(BF16)
        wd_sc[...] = wd_ref[...].astype(BF16)

    @pl.when(used)
    def _():
        x = _load_chunks(lambda c: xs_ref.at[c])
        row = lax.broadcasted_iota(jnp.int32, (MOE_BLK, 1), 0)
        x = jnp.where(row < blk_valid_ref[b], x, 0.0).astype(BF16)
        g = jnp.dot(x, wg_sc[...], preferred_element_type=F32)
        u = jnp.dot(x, wu_sc[...], preferred_element_type=F32)
        hmid = (g * jax.nn.sigmoid(g) * u).astype(BF16)
        _store_chunks(lambda c: y_ref.at[c], jnp.dot(hmid, wd_sc[...], preferred_element_type=F32))

    @pl.when(jnp.logical_not(used))
    def _():
        y_ref[...] = jnp.zeros_like(y_ref)


def _ffn(xs, blk_e, blk_valid, n_used, layer, wg, wu, wd, n_blocks):
    def blk(b, be, bv, nu):
        return jnp.maximum(jnp.minimum(b, nu[0] - 1), 0)

    def w_idx(b, be, bv, nu):
        return (layer, be[blk(b, be, bv, nu)], 0, 0)

    return pl.pallas_call(
        _ffn_kernel,
        name="expert_ffn",
        grid_spec=pltpu.PrefetchScalarGridSpec(
            num_scalar_prefetch=3,
            grid=(n_blocks,),
            in_specs=[
                pl.BlockSpec((ROW_CHUNKS, MOE_BLK, CHUNK_W), lambda b, be, bv, nu: (0, blk(b, be, bv, nu), 0)),
                pl.BlockSpec((None, None, D, D_EXPERT), w_idx),
                pl.BlockSpec((None, None, D, D_EXPERT), w_idx),
                pl.BlockSpec((None, None, D_EXPERT, D), w_idx),
            ],
            out_specs=pl.BlockSpec((ROW_CHUNKS, MOE_BLK, CHUNK_W), lambda b, be, bv, nu: (0, b, 0)),
            scratch_shapes=[pltpu.VMEM((D, D_EXPERT), BF16), pltpu.VMEM((D, D_EXPERT), BF16),
                            pltpu.VMEM((D_EXPERT, D), BF16)],
        ),
        out_shape=jax.ShapeDtypeStruct(xs.shape, U32),
        compiler_params=_params(("arbitrary",)),
    )(blk_e, blk_valid, n_used, xs, wg, wu, wd)


def _combine_kernel(x_ref, mod_ref, wt_ref, y_ref, sg_ref, su_ref, sd_ref, lg_ref, lb_ref, *o_refs, n_prompt_tiles):
    x = x_ref[...]
    hb = (x * (1.0 + mod_ref[4:5, :]) + mod_ref[3:4, :]).astype(BF16)
    g = jnp.dot(hb, sg_ref[...], preferred_element_type=F32)
    u = jnp.dot(hb, su_ref[...], preferred_element_type=F32)
    ff = jnp.dot((g * jax.nn.sigmoid(g) * u).astype(BF16), sd_ref[...], preferred_element_type=F32)
    for k in range(TOP_K):
        ff = ff + _load_chunks(lambda c: y_ref.at[c, k]) * wt_ref[:, k:k + 1]
    out = _layer_norm_rows(ALPHA * x + mod_ref[5:6, :] * ff, lg_ref[...], lb_ref[...])
    if len(o_refs) == 1:
        o_refs[0][...] = out
    else:
        is_prompt = pl.program_id(0) < n_prompt_tiles

        @pl.when(is_prompt)
        def _():
            o_refs[0][...] = out

        @pl.when(jnp.logical_not(is_prompt))
        def _():
            o_refs[1][...] = out


def _combine(geo, x, mod_l, wt, ytok, sg, su, sd, ln_g, ln_b, split=False):
    tb = TOK_TILE
    n_p = geo.t_prompt // tb
    full = lambda shape: pl.BlockSpec(shape, lambda i: (0,) * len(shape))
    if split:
        out_specs = [pl.BlockSpec((tb, D), lambda i: (jnp.minimum(i, n_p - 1), 0)),
                     pl.BlockSpec((tb, D), lambda i: (jnp.maximum(i - n_p, 0), 0))]
        out_shape = [jax.ShapeDtypeStruct((geo.t_prompt, D), F32), jax.ShapeDtypeStruct((geo.t_sample, D), F32)]
    else:
        out_specs = pl.BlockSpec((tb, D), lambda i: (i, 0))
        out_shape = jax.ShapeDtypeStruct((geo.t, D), F32)
    return pl.pallas_call(
        functools.partial(_combine_kernel, n_prompt_tiles=n_p),
        name="combine",
        grid=(geo.t // tb,),
        in_specs=[
            pl.BlockSpec((tb, D), lambda i: (i, 0)),
            pl.BlockSpec((None, 6, D), lambda i: (geo.cond_row(i, tb), 0, 0)),
            pl.BlockSpec((tb, TOP_K), lambda i: (i, 0)),
            pl.BlockSpec((ROW_CHUNKS, TOP_K, tb, CHUNK_W), lambda i: (0, 0, i, 0)),
            full((D, D_EXPERT)), full((D, D_EXPERT)), full((D_EXPERT, D)), full((1, D)), full((1, D)),
        ],
        out_specs=out_specs,
        out_shape=out_shape,
        compiler_params=_params(("arbitrary",)),
    )(x, mod_l, wt, ytok, sg.astype(BF16), su.astype(BF16), sd.astype(BF16),
      ln_g.reshape(1, D), ln_b.reshape(1, D))


def _moe_layer(geo, x, mod_l, w_router, e_bias, layer, wg, wu, wd, sg, su, sd, ln_g, ln_b, split=False):
    t = geo.t
    top_e, w, rank, counts, h = _router(geo, x, mod_l, w_router, e_bias)
    n_blocks = (t * TOP_K) // MOE_BLK + N_EXPERTS
    n_rows = n_blocks * MOE_BLK
    padded = (counts + MOE_BLK - 1) // MOE_BLK * MOE_BLK
    pend = jnp.cumsum(padded)
    pstart = (pend - padded).astype(jnp.int32)
    blk_e, blk_valid = _block_meta(pstart, counts, pend.astype(jnp.int32), n_blocks)
    n_used = (pend[-1:] // MOE_BLK).astype(jnp.int32)
    slots = _slots(geo, pstart, top_e, rank)
    idx = (slots.reshape(1, TOP_K * t) + (jnp.arange(ROW_CHUNKS, dtype=jnp.int32) * n_rows)[:, None]).reshape(-1)
    win_per_chunk = TOP_K * t // SC_WINDOW
    tok_windows = t // SC_WINDOW

    def src_block(wdw):
        return (wdw // win_per_chunk) * tok_windows + (wdw % win_per_chunk) % tok_windows

    xs = _sc_scatter(h.reshape(ROW_CHUNKS * t, CHUNK_W), idx, ROW_CHUNKS * n_rows, src_block)
    yb = _ffn(xs.reshape(ROW_CHUNKS, n_rows, CHUNK_W), blk_e, blk_valid, n_used, layer, wg, wu, wd, n_blocks)
    ytok = _sc_gather(yb.reshape(ROW_CHUNKS * n_rows, CHUNK_W), idx)
    return _combine(geo, x, mod_l, w.T, ytok.reshape(ROW_CHUNKS, TOP_K, t, CHUNK_W), sg, su, sd, ln_g, ln_b,
                    split=split)


def _pos_embed(rows):
    quarter = D // 4
    omega = 1.0 / (POS_BASE ** (jnp.arange(quarter, dtype=F32) / quarter))
    r, col = jnp.meshgrid(jnp.arange(rows, dtype=F32), jnp.arange(GRID_W, dtype=F32), indexing='ij')
    r = r.reshape(-1, 1) * omega
    col = col.reshape(-1, 1) * omega
    return jnp.concatenate([jnp.sin(r), jnp.cos(r), jnp.sin(col), jnp.cos(col)], axis=-1)


def _mlstm_layer(geo, x, mod_l, j, a_w_in, a_b_gates, a_norm, a_w_out, ln_g, ln_b,
                 state_C, state_n, state_m):
    q, kt, v, so, gc, gr = _proj_a(geo, x, mod_l, a_w_in[j], a_b_gates[j])
    hp, c_p, n_p, m_p = _mlstm_scan(q, kt, v, gc, gr, row0=0, n_seq=geo.n_prompt, seq_len=geo.prompt_len)
    ns = geo.n_sample
    n0 = jnp.pad(state_n[:, j].astype(F32)[..., None], ((0, 0),) * 4 + ((0, LANES - 1),))
    m0 = jnp.pad(state_m[:, j].astype(F32), ((0, 0), (0, 0), (0, SUBLANES - NH_A)))
    m0 = jnp.broadcast_to(m0[..., None], (ns, 2, SUBLANES, LANES))
    hs, _, _, _ = _mlstm_scan(q, kt, v, gc, gr, row0=geo.t_prompt, n_seq=ns, seq_len=geo.sample_len,
                              state=(state_C[:, j].astype(F32), n0, m0))
    x1 = _out_a(geo, hp, hs, so, a_norm[j], a_w_out[j], x, mod_l, ln_g, ln_b)
    return x1, c_p, n_p[..., 0], m_p[:, :, :NH_A, 0]


def _hgrn_layer(geo, x, mod_l, j, lb_layer, b_w_in, b_norm, b_w_out, ln_g, ln_b, state_S):
    q, pre, v, sg = _proj_b(geo, x, mod_l, b_w_in[j])
    lbd = lb_layer.reshape(2, NH_B, 1, DK_B)
    op, s_p = _hgrn_scan(q, pre, v, lbd, row0=0, n_seq=geo.n_prompt, seq_len=geo.prompt_len)
    os_, _ = _hgrn_scan(q, pre, v, lbd, row0=geo.t_prompt, n_seq=geo.n_sample, seq_len=geo.sample_len,
                        state=state_S[:, j].astype(F32))
    x1 = _out_b(geo, op, os_, sg, b_norm[j], b_w_out[j], x, mod_l, ln_g, ln_b)
    return x1, s_p


def kernel(x_prompt, x_sample, state_mlstm_C, state_mlstm_n, state_mlstm_m, state_hgrn_S, c, c_ctx, w_mod, b_mod, ln_g, ln_b, a_w_in, a_b_gates, a_norm, a_w_out, b_w_in, b_lb, b_norm, b_w_out, w_router, e_bias, w_gate, w_up, w_down, ws_gate, ws_up, ws_down):
    bp, sp, _ = x_prompt.shape
    bs, ss, _ = x_sample.shape
    geo = Geometry(bp, sp, bs, ss)
    cond = jnp.zeros((COND_ROWS, D), F32).at[0].set(c_ctx).at[1:1 + bs].set(c)
    mod = _modulation(cond, w_mod, b_mod)
    x = (x_prompt.reshape(-1, D), x_sample.reshape(-1, D), _pos_embed(ss // GRID_W))
    x1, new_c, new_n, new_m = _mlstm_layer(geo, x, mod[0], 0, a_w_in, a_b_gates, a_norm, a_w_out,
                                           ln_g[0, 0], ln_b[0, 0], state_mlstm_C, state_mlstm_n, state_mlstm_m)
    x2 = _moe_layer(geo, x1, mod[0], w_router[0], e_bias[0], 0, w_gate, w_up, w_down, ws_gate[0], ws_up[0], ws_down[0],
                    ln_g[0, 1], ln_b[0, 1])
    sm = jax.nn.softmax(b_lb.astype(F32), axis=0)
    lb_all = jnp.cumsum(sm, axis=0) - sm[0]
    x3, new_s = _hgrn_layer(geo, x2, mod[1], 0, lb_all[1], b_w_in, b_norm, b_w_out, ln_g[1, 0], ln_b[1, 0],
                            state_hgrn_S)
    y_p, y_s = _moe_layer(geo, x3, mod[1], w_router[1], e_bias[1], 1, w_gate, w_up, w_down, ws_gate[1], ws_up[1],
                          ws_down[1], ln_g[1, 1], ln_b[1, 1], split=True)
    y_prompt = y_p.reshape(bp, sp, D)
    y_sample = y_s.reshape(bs, ss, D)
    return y_prompt, y_sample, new_c[:, None], new_n[:, None], new_m[:, None], new_s[:, None]
```

```python
import functools

import jax
import jax.numpy as jnp
from jax import lax
from jax.experimental import pallas as pl
from jax.experimental.pallas import tpu as pltpu
from jax.experimental.pallas import tpu_sc as plsc

F32 = jnp.float32
BF16 = jnp.bfloat16
HIGHEST = lax.Precision.HIGHEST

D = 1024
DEPTH = 2
GRID_W = 64
POS_BASE = 10000.0
EPS = 1e-6
ALPHA = (2.0 * DEPTH) ** 0.25
NH_A, DK_A, DV_A = 4, 128, 256
QK_A, V_A = NH_A * DK_A, NH_A * DV_A
NH_B, DK_B = 8, 128
N_EXPERTS, TOP_K, N_GROUPS, TOPK_GROUPS = 64, 8, 8, 4
GROUP_SIZE = N_EXPERTS // N_GROUPS
D_EXPERT = D // 4
ROUTED_SCALE = 2.5

LANES = 128
SUBLANES = 8
COND_ROWS = 8
TOK_TILE = 256
TOK_TILE_L = 512
SLOT_TILE = 2048
CHUNK_A = 256
VMEM_LIMIT = 56 * 1024 * 1024

NT_DIMS = (((1,), (1,)), ((), ()))


def _params(sem):
    return pltpu.CompilerParams(dimension_semantics=sem, vmem_limit_bytes=VMEM_LIMIT)


def _split3(x):
    hi = x.astype(BF16)
    r = x - hi.astype(F32)
    mid = r.astype(BF16)
    lo = (r - mid.astype(F32)).astype(BF16)
    return hi, mid, lo


def _dot3(a_bf, x, transpose_side=None):
    hi, mid, lo = _split3(x)
    return (jnp.dot(a_bf, hi, preferred_element_type=F32)
            + jnp.dot(a_bf, mid, preferred_element_type=F32)
            + jnp.dot(a_bf, lo, preferred_element_type=F32))


def _dot3_r(x, a_bf):
    hi, mid, lo = _split3(x)
    return (jnp.dot(hi, a_bf, preferred_element_type=F32)
            + jnp.dot(mid, a_bf, preferred_element_type=F32)
            + jnp.dot(lo, a_bf, preferred_element_type=F32))


def _log_sigmoid(x):
    return jnp.minimum(x, 0.0) - jnp.log1p(jnp.exp(-jnp.abs(x)))


def _layer_norm_rows(x, g, b):
    mu = jnp.mean(x, axis=-1, keepdims=True)
    xc = x - mu
    var = jnp.mean(xc * xc, axis=-1, keepdims=True)
    return xc * lax.rsqrt(var + EPS) * g + b


class Geometry:
    def __init__(self, n_prompt, prompt_len, n_sample, sample_len):
        self.n_prompt, self.prompt_len = n_prompt, prompt_len
        self.n_sample, self.sample_len = n_sample, sample_len
        self.t_prompt = n_prompt * prompt_len
        self.t_sample = n_sample * sample_len
        self.t = self.t_prompt + self.t_sample
        assert self.t_prompt % TOK_TILE_L == 0 and sample_len % TOK_TILE_L == 0 and self.t % SLOT_TILE == 0
        assert n_sample + 1 <= COND_ROWS

    def cond_row(self, tile, tile_rows):
        n_p = self.t_prompt // tile_rows
        return jnp.where(tile < n_p, 0, 1 + (tile - n_p) // (self.sample_len // tile_rows))


def _mod_kernel(cond_ref, w_ref, b_ref, o_ref):
    c = cond_ref[...]
    s = c * jax.nn.sigmoid(c)
    o_ref[0, 0] = jnp.dot(s, w_ref[0], precision=HIGHEST, preferred_element_type=F32) + b_ref[0, 0]


def _modulation(cond, w_mod, b_mod):
    out = pl.pallas_call(
        _mod_kernel,
        name="modulation",
        grid=(DEPTH, 6),
        in_specs=[
            pl.BlockSpec((COND_ROWS, D), lambda l, j: (0, 0)),
            pl.BlockSpec((1, D, D), lambda l, j: (l, 0, j)),
            pl.BlockSpec((1, 1, 1, D), lambda l, j: (l, j, 0, 0)),
        ],
        out_specs=pl.BlockSpec((1, 1, COND_ROWS, D), lambda l, j: (l, j, 0, 0)),
        out_shape=jax.ShapeDtypeStruct((DEPTH, 6, COND_ROWS, D), F32),
        compiler_params=_params(("arbitrary", "arbitrary")),
    )(cond, w_mod, b_mod.reshape(DEPTH, 6, 1, D))
    return out.transpose(0, 2, 1, 3)


def _embed_specs(geo, tb):
    n_p = geo.t_prompt // tb
    per_seq = geo.sample_len // tb
    return [pl.BlockSpec((tb, D), lambda i: (jnp.minimum(i, n_p - 1), 0)),
            pl.BlockSpec((tb, D), lambda i: (jnp.maximum(i - n_p, 0), 0)),
            pl.BlockSpec((tb, D), lambda i: (jnp.maximum(i - n_p, 0) % per_seq, 0))]


def _embed_tile(xp_ref, xs_ref, pos_ref, n_prompt_tiles):
    return jnp.where(pl.program_id(0) < n_prompt_tiles, xp_ref[...], xs_ref[...] + pos_ref[...])


def _proj_a_kernel(xp_ref, xs_ref, pos_ref, mod_ref, wq_ref, wkt_ref, wvo_ref, wg_ref, wgt_ref, bg_ref, bgt_ref,
                   q_ref, kt_ref, v_ref, so_ref, gc_ref, gr_ref, *, n_prompt_tiles):
    x = _embed_tile(xp_ref, xs_ref, pos_ref, n_prompt_tiles)
    h = x * (1.0 + mod_ref[1:2, :]) + mod_ref[0:1, :]
    hb = h.astype(BF16)
    q_ref[...] = jnp.dot(hb, wq_ref[...], preferred_element_type=F32).astype(BF16)
    kt = lax.dot_general(wkt_ref[...], hb, NT_DIMS, preferred_element_type=F32)
    kt_ref[...] = (kt * (DK_A ** -0.5)).astype(BF16)
    vo = jnp.dot(hb, wvo_ref[...], preferred_element_type=F32)
    v_ref[...] = vo[:, :V_A].astype(BF16)
    so_ref[...] = jax.nn.sigmoid(vo[:, V_A:]).astype(BF16)
    gc_ref[...] = jnp.dot(h, wg_ref[...], precision=HIGHEST, preferred_element_type=F32) + bg_ref[...]
    gr_ref[...] = lax.dot_general(wgt_ref[...], h, NT_DIMS, precision=HIGHEST,
                                  preferred_element_type=F32) + bgt_ref[...]


def _proj_a(geo, x, mod_l, w_in, b_gates):
    t = geo.t
    n_gate = 4 * NH_A
    wq = w_in[:, :QK_A].astype(BF16)
    wkt = w_in[:, QK_A:2 * QK_A].T.astype(BF16)
    wvo = w_in[:, 2 * QK_A:2 * QK_A + 2 * V_A].astype(BF16)
    wg = w_in[:, 2 * QK_A + 2 * V_A:]
    wg_pad = jnp.pad(wg, ((0, 0), (0, LANES - n_gate)))
    bg = b_gates.reshape(n_gate).astype(F32)
    bg_pad = jnp.pad(bg, (0, LANES - n_gate)).reshape(1, LANES)
    tb = TOK_TILE_L
    full = lambda shape: pl.BlockSpec(shape, lambda i: (0,) * len(shape))
    return pl.pallas_call(
        functools.partial(_proj_a_kernel, n_prompt_tiles=geo.t_prompt // tb),
        name="proj_a",
        grid=(t // tb,),
        in_specs=_embed_specs(geo, tb) + [
            pl.BlockSpec((None, 6, D), lambda i: (geo.cond_row(i, tb), 0, 0)),
            full((D, QK_A)), full((QK_A, D)), full((D, 2 * V_A)), full((D, LANES)), full((n_gate, D)),
            full((1, LANES)), full((n_gate, 1)),
        ],
        out_specs=[
            pl.BlockSpec((tb, QK_A), lambda i: (i, 0)),
            pl.BlockSpec((QK_A, tb), lambda i: (0, i)),
            pl.BlockSpec((tb, V_A), lambda i: (i, 0)),
            pl.BlockSpec((tb, V_A), lambda i: (i, 0)),
            pl.BlockSpec((tb, LANES), lambda i: (i, 0)),
            pl.BlockSpec((n_gate, tb), lambda i: (0, i)),
        ],
        out_shape=[
            jax.ShapeDtypeStruct((t, QK_A), BF16),
            jax.ShapeDtypeStruct((QK_A, t), BF16),
            jax.ShapeDtypeStruct((t, V_A), BF16),
            jax.ShapeDtypeStruct((t, V_A), BF16),
            jax.ShapeDtypeStruct((t, LANES), F32),
            jax.ShapeDtypeStruct((n_gate, t), F32),
        ],
        compiler_params=_params(("parallel",)),
    )(*x, mod_l, wq, wkt, wvo, wg_pad, wg.T, bg_pad, bg.reshape(n_gate, 1))


def _mlstm_scan_kernel(*refs, chunk, has_state):
    if has_state:
        (q_ref, kt_ref, v_ref, gc_ref, gr_ref, c0_ref, n0_ref, m0_ref,
         h_ref, c_out, n_out, m_out, c_sc, n_sc, m_sc) = refs
    else:
        (q_ref, kt_ref, v_ref, gc_ref, gr_ref,
         h_ref, c_out, n_out, m_out, c_sc, n_sc, m_sc) = refs
    L = chunk
    d = pl.program_id(1)
    c = pl.program_id(2)
    fwd = d == 0

    @pl.when(c == 0)
    def _():
        if has_state:
            c_sc[...] = c0_ref[0, 0]
            n_sc[...] = n0_ref[0, 0]
            m_sc[...] = m0_ref[0, 0]
        else:
            c_sc[...] = jnp.zeros_like(c_sc)
            n_sc[...] = jnp.zeros_like(n_sc)
            m_sc[...] = jnp.zeros_like(m_sc)

    row = lax.broadcasted_iota(jnp.int32, (L, L), 0)
    col = lax.broadcasted_iota(jnp.int32, (L, L), 1)
    sgn = 1 - 2 * d
    causal = (row - col) * sgn >= 0
    tri = causal.astype(BF16)
    tri_t = ((col - row) * sgn >= 0).astype(BF16)

    gc = gc_ref[...]
    gr = gr_ref[...]
    bc_all = _dot3(tri, _log_sigmoid(gc))
    br_all = _dot3_r(_log_sigmoid(gr), tri_t)
    ones_blk = (lax.broadcasted_iota(jnp.int32, (L, LANES), 1) == 0).astype(BF16)

    for h in range(NH_A):
        b_c = jnp.where(fwd, bc_all[:, 4 + h:5 + h], bc_all[:, 12 + h:13 + h])
        b_r = jnp.where(fwd, br_all[4 + h:5 + h, :], br_all[12 + h:13 + h, :])
        i_r = jnp.where(fwd, gr[h:h + 1, :], gr[8 + h:9 + h, :])
        bl = jnp.where(fwd, b_r[:, L - 1:L], b_r[:, 0:1])
        q = q_ref[:, h * DK_A:(h + 1) * DK_A]
        kt = kt_ref[h * DK_A:(h + 1) * DK_A, :]
        v = v_ref[:, h * DV_A:(h + 1) * DV_A]
        m = m_sc[h:h + 1, 0:1]
        cst = c_sc[h]
        nst = n_sc[h]

        a_r = i_r - b_r
        logd = jnp.where(causal, b_c + a_r, -jnp.inf)
        inter = b_c + m
        m_t = jnp.maximum(inter, jnp.max(logd, axis=1, keepdims=True))
        dmat = jnp.exp(logd - m_t)
        e_int = jnp.exp(inter - m_t)
        s = (jnp.dot(q, kt, preferred_element_type=F32) * dmat).astype(BF16)
        num = (jnp.dot(s, v, preferred_element_type=F32)
               + e_int * jnp.dot(q, cst.astype(BF16), preferred_element_type=F32))
        den = (jnp.dot(s, ones_blk, preferred_element_type=F32)
               + e_int * jnp.dot(q, nst.astype(BF16), preferred_element_type=F32))[:, 0:1]
        h_ref[:, h * DV_A:(h + 1) * DV_A] = num / jnp.maximum(jnp.abs(den), jnp.exp(-m_t))

        logw = bl + a_r
        m_new = jnp.maximum(bl + m, jnp.max(logw, axis=1, keepdims=True))
        w = jnp.exp(logw - m_new)
        decay = jnp.exp(bl + m - m_new)
        kw = (kt.astype(F32) * w).astype(BF16)
        c_sc[h] = decay * cst + jnp.dot(kw, v, preferred_element_type=F32)
        n_sc[h] = decay * nst + jnp.dot(kw, ones_blk, preferred_element_type=F32)
        m_sc[h:h + 1, :] = jnp.broadcast_to(m_new, (1, LANES))

    @pl.when(c == pl.num_programs(2) - 1)
    def _():
        c_out[0, 0] = c_sc[...]
        n_out[0, 0] = n_sc[...]
        m_out[0, 0] = m_sc[...]


def _mlstm_scan(q, kt, v, gc, gr, *, row0, n_seq, seq_len, state=None):
    L = CHUNK_A
    nc = seq_len // L
    blk0 = row0 // L

    def loc_blk(b, d, c):
        return b * nc + c + d * (nc - 1 - 2 * c)

    def tok_blk(b, d, c):
        return blk0 + loc_blk(b, d, c)

    in_specs = [
        pl.BlockSpec((L, QK_A), lambda b, d, c: (tok_blk(b, d, c), 0)),
        pl.BlockSpec((QK_A, L), lambda b, d, c: (0, tok_blk(b, d, c))),
        pl.BlockSpec((L, V_A), lambda b, d, c: (tok_blk(b, d, c), 0)),
        pl.BlockSpec((L, LANES), lambda b, d, c: (tok_blk(b, d, c), 0)),
        pl.BlockSpec((4 * NH_A, L), lambda b, d, c: (0, tok_blk(b, d, c))),
    ]
    args = [q, kt, v, gc, gr]
    if state is not None:
        in_specs += [
            pl.BlockSpec((1, 1, NH_A, DK_A, DV_A), lambda b, d, c: (b, d, 0, 0, 0)),
            pl.BlockSpec((1, 1, NH_A, DK_A, LANES), lambda b, d, c: (b, d, 0, 0, 0)),
            pl.BlockSpec((1, 1, SUBLANES, LANES), lambda b, d, c: (b, d, 0, 0)),
        ]
        args += list(state)
    return pl.pallas_call(
        functools.partial(_mlstm_scan_kernel, chunk=L, has_state=state is not None),
        name="mlstm_scan_seeded" if state is not None else "mlstm_scan",
        grid=(n_seq, 2, nc),
        in_specs=in_specs,
        out_specs=[
            pl.BlockSpec((None, L, V_A), lambda b, d, c: (d, loc_blk(b, d, c), 0)),
            pl.BlockSpec((1, 1, NH_A, DK_A, DV_A), lambda b, d, c: (b, d, 0, 0, 0)),
            pl.BlockSpec((1, 1, NH_A, DK_A, LANES), lambda b, d, c: (b, d, 0, 0, 0)),
            pl.BlockSpec((1, 1, SUBLANES, LANES), lambda b, d, c: (b, d, 0, 0)),
        ],
        out_shape=[
            jax.ShapeDtypeStruct((2, n_seq * seq_len, V_A), F32),
            jax.ShapeDtypeStruct((n_seq, 2, NH_A, DK_A, DV_A), F32),
            jax.ShapeDtypeStruct((n_seq, 2, NH_A, DK_A, LANES), F32),
            jax.ShapeDtypeStruct((n_seq, 2, SUBLANES, LANES), F32),
        ],
        scratch_shapes=[
            pltpu.VMEM((NH_A, DK_A, DV_A), F32),
            pltpu.VMEM((NH_A, DK_A, LANES), F32),
            pltpu.VMEM((SUBLANES, LANES), F32),
        ],
        compiler_params=_params(("parallel", "parallel", "arbitrary")),
    )(*args)


def _out_a_kernel(hp_ref, hs_ref, so_ref, nw_ref, w_ref, xp_ref, xs_ref, pos_ref, mod_ref, lg_ref, lb_ref, o_ref, *,
                  n_prompt_tiles):
    is_prompt = pl.program_id(0) < n_prompt_tiles
    x = _embed_tile(xp_ref, xs_ref, pos_ref, n_prompt_tiles)
    y = jnp.where(is_prompt, hp_ref[0] + hp_ref[1], hs_ref[0] + hs_ref[1])
    parts = []
    for h in range(NH_A):
        yh = y[:, h * DV_A:(h + 1) * DV_A]
        mu = jnp.mean(yh, axis=-1, keepdims=True)
        yc = yh - mu
        var = jnp.mean(yc * yc, axis=-1, keepdims=True)
        parts.append(yc * lax.rsqrt(var + EPS))
    yn = jnp.concatenate(parts, axis=-1) * nw_ref[...] * so_ref[...].astype(F32)
    out = jnp.dot(yn.astype(BF16), w_ref[...], preferred_element_type=F32)
    o_ref[...] = _layer_norm_rows(ALPHA * x + mod_ref[2:3, :] * out, lg_ref[...], lb_ref[...])


def _out_a(geo, h_prompt, h_sample, so, norm_w, w_out, x, mod_l, ln_g, ln_b):
    t = geo.t
    tb = TOK_TILE_L
    n_p = geo.t_prompt // tb
    full = lambda shape: pl.BlockSpec(shape, lambda i: (0,) * len(shape))
    return pl.pallas_call(
        functools.partial(_out_a_kernel, n_prompt_tiles=n_p),
        name="out_a",
        grid=(t // tb,),
        in_specs=[
            pl.BlockSpec((2, tb, V_A), lambda i: (0, jnp.minimum(i, n_p - 1), 0)),
            pl.BlockSpec((2, tb, V_A), lambda i: (0, jnp.maximum(i - n_p, 0), 0)),
            pl.BlockSpec((tb, V_A), lambda i: (i, 0)),
            full((1, V_A)), full((V_A, D)),
        ] + _embed_specs(geo, tb) + [
            pl.BlockSpec((None, 6, D), lambda i: (geo.cond_row(i, tb), 0, 0)),
            full((1, D)), full((1, D)),
        ],
        out_specs=pl.BlockSpec((tb, D), lambda i: (i, 0)),
        out_shape=jax.ShapeDtypeStruct((t, D), F32),
        compiler_params=_params(("parallel",)),
    )(h_prompt, h_sample, so, norm_w.reshape(1, V_A).astype(F32), w_out.astype(BF16), *x, mod_l,
      ln_g.reshape(1, D), ln_b.reshape(1, D))


def _proj_b_kernel(x_ref, mod_ref, w_ref, q_ref, pre_ref, v_ref, sg_ref):
    h = x_ref[...] * (1.0 + mod_ref[1:2, :]) + mod_ref[0:1, :]
    z = jnp.dot(h.astype(BF16), w_ref[...], preferred_element_type=F32)
    for hd in range(NH_B):
        lo = hd * DK_B
        qh = z[:, lo:lo + DK_B]
        q_ref[hd] = qh * jax.nn.sigmoid(qh)
        pre_ref[0, hd] = z[:, D + lo:D + lo + DK_B]
        pre_ref[1, hd] = z[:, 2 * D + lo:2 * D + lo + DK_B]
        v_ref[hd] = z[:, 3 * D + lo:3 * D + lo + DK_B].astype(BF16)
    g = z[:, 4 * D:]
    sg_ref[...] = (g * jax.nn.sigmoid(g)).astype(BF16)


def _proj_b(geo, x, mod_l, w_in):
    t = geo.t
    tb = TOK_TILE
    return pl.pallas_call(
        _proj_b_kernel,
        name="proj_b",
        grid=(t // tb,),
        in_specs=[
            pl.BlockSpec((tb, D), lambda i: (i, 0)),
            pl.BlockSpec((None, 6, D), lambda i: (geo.cond_row(i, tb), 0, 0)),
            pl.BlockSpec((D, 5 * D), lambda i: (0, 0)),
        ],
        out_specs=[
            pl.BlockSpec((NH_B, tb, DK_B), lambda i: (0, i, 0)),
            pl.BlockSpec((2, NH_B, tb, DK_B), lambda i: (0, 0, i, 0)),
            pl.BlockSpec((NH_B, tb, DK_B), lambda i: (0, i, 0)),
            pl.BlockSpec((tb, D), lambda i: (i, 0)),
        ],
        out_shape=[
            jax.ShapeDtypeStruct((NH_B, t, DK_B), F32),
            jax.ShapeDtypeStruct((2, NH_B, t, DK_B), F32),
            jax.ShapeDtypeStruct((NH_B, t, DK_B), BF16),
            jax.ShapeDtypeStruct((t, D), BF16),
        ],
        compiler_params=_params(("parallel",)),
    )(x, mod_l, w_in.astype(BF16))


CHUNK_B = 128
BAND = SUBLANES // 2
TN_DIMS = (((0,), (0,)), ((), ()))


def _hgrn_head(q, pre, lbv, v_bf, st, fwd):
    L = q.shape[0]
    sg = jax.nn.sigmoid(pre)
    f = lbv + (1.0 - lbv) * sg
    lf = jnp.log(f)
    kk = (1.0 - lbv) * (1.0 - sg)
    row = lax.broadcasted_iota(jnp.int32, (L, L), 0)
    col = lax.broadcasted_iota(jnp.int32, (L, L), 1)
    tri = ((row >= col) if fwd else (row <= col)).astype(BF16)
    b = _dot3(tri, lf)
    tpos = lax.broadcasted_iota(jnp.int32, (L, DK_B), 0)
    blk_bits = row ^ col
    lag = jnp.where(blk_bits < BAND, (row - col) if fwd else (col - row), -1)

    step = 1 if fwd else L - 1
    att = jnp.where(lag == 0, jnp.sum(q * kk, axis=1, keepdims=True), 0.0)
    f_r, kk_r, g = f, kk, f
    for dl in range(1, BAND):
        if dl > 1:
            f_r = pltpu.roll(f_r, step, 0)
            g = g * f_r
        kk_r = pltpu.roll(kk_r, step, 0)
        att = jnp.where(lag == dl, jnp.sum(q * kk_r * g, axis=1, keepdims=True), att)

    w = BAND
    while w < L:
        nb = L // (2 * w)
        b3 = b.reshape(nb, 2 * w, DK_B)
        edge = (b3[:, w - 1:w, :] if fwd else b3[:, w:w + 1, :])
        bmid = jnp.broadcast_to(edge, (nb, 2 * w, DK_B)).reshape(L, DK_B)
        second = (tpos & w) != 0
        t_side = second if fwd else jnp.logical_not(second)
        e = jnp.exp(jnp.where(t_side, b - bmid, bmid - b))
        qt = jnp.where(t_side, q * e, 0.0).astype(BF16)
        ks = jnp.where(t_side, 0.0, kk * e).astype(BF16)
        a = lax.dot_general(qt, ks, NT_DIMS, preferred_element_type=F32)
        att = att + jnp.where(blk_bits < 2 * w, a, 0.0)
        w *= 2
    o = jnp.dot(att.astype(BF16), v_bf, preferred_element_type=F32)

    bl = b[L - 1:L, :] if fwd else b[0:1, :]
    o = o + lax.dot_general((q * jnp.exp(b)).astype(BF16), st.astype(BF16), NT_DIMS, preferred_element_type=F32)
    kd = (kk * jnp.exp(bl - b)).astype(BF16)
    st_new = jnp.exp(bl) * st + lax.dot_general(v_bf, kd, TN_DIMS, preferred_element_type=F32)
    return o, st_new


def _hgrn_scan_kernel(*refs, has_state):
    if has_state:
        q_ref, pre_ref, v_ref, lb_ref, s0_ref, o_ref, s_out, st_sc = refs
    else:
        q_ref, pre_ref, v_ref, lb_ref, o_ref, s_out, st_sc = refs
    d = pl.program_id(1)
    c = pl.program_id(2)

    @pl.when(c == 0)
    def _():
        if has_state:
            for hd in range(NH_B):
                st_sc[hd] = s0_ref[0, 0, hd].T
        else:
            st_sc[...] = jnp.zeros_like(st_sc)

    def run(fwd):
        def head(hd, carry):
            o, st_new = _hgrn_head(q_ref[hd], pre_ref[hd], lb_ref[hd], v_ref[hd], st_sc[hd], fwd)
            o_ref[hd] = o
            st_sc[hd] = st_new
            return carry
        lax.fori_loop(0, NH_B, head, 0, unroll=8)

    @pl.when(d == 0)
    def _():
        run(True)

    @pl.when(d == 1)
    def _():
        run(False)

    @pl.when(c == pl.num_programs(2) - 1)
    def _():
        for hd in range(NH_B):
            s_out[0, 0, hd] = st_sc[hd].T


def _hgrn_scan(q, pre, v, lbd, *, row0, n_seq, seq_len, state=None):
    L = CHUNK_B
    nc = seq_len // L
    blk0 = row0 // L

    def loc_blk(b, d, c):
        return b * nc + c + d * (nc - 1 - 2 * c)

    def tok_blk(b, d, c):
        return blk0 + loc_blk(b, d, c)

    in_specs = [
        pl.BlockSpec((NH_B, L, DK_B), lambda b, d, c: (0, tok_blk(b, d, c), 0)),
        pl.BlockSpec((None, NH_B, L, DK_B), lambda b, d, c: (d, 0, tok_blk(b, d, c), 0)),
        pl.BlockSpec((NH_B, L, DK_B), lambda b, d, c: (0, tok_blk(b, d, c), 0)),
        pl.BlockSpec((None, NH_B, 1, DK_B), lambda b, d, c: (d, 0, 0, 0)),
    ]
    args = [q, pre, v, lbd]
    if state is not None:
        in_specs.append(pl.BlockSpec((1, 1, NH_B, DK_B, DK_B), lambda b, d, c: (b, d, 0, 0, 0)))
        args.append(state)
    return pl.pallas_call(
        functools.partial(_hgrn_scan_kernel, has_state=state is not None),
        name="hgrn_scan_seeded" if state is not None else "hgrn_scan",
        grid=(n_seq, 2, nc),
        in_specs=in_specs,
        out_specs=[
            pl.BlockSpec((None, NH_B, L, DK_B), lambda b, d, c: (d, 0, loc_blk(b, d, c), 0)),
            pl.BlockSpec((1, 1, NH_B, DK_B, DK_B), lambda b, d, c: (b, d, 0, 0, 0)),
        ],
        out_shape=[
            jax.ShapeDtypeStruct((2, NH_B, n_seq * seq_len, DK_B), F32),
            jax.ShapeDtypeStruct((n_seq, 2, NH_B, DK_B, DK_B), F32),
        ],
        scratch_shapes=[pltpu.VMEM((NH_B, DK_B, DK_B), F32)],
        compiler_params=_params(("parallel", "parallel", "arbitrary")),
    )(*args)


def _out_b_kernel(op_ref, os_ref, sg_ref, nw_ref, w_ref, x_ref, mod_ref, lg_ref, lb_ref, out_ref, *, n_prompt_tiles):
    is_prompt = pl.program_id(0) < n_prompt_tiles
    parts = []
    for hd in range(NH_B):
        y = jnp.where(is_prompt, op_ref[0, hd] + op_ref[1, hd], os_ref[0, hd] + os_ref[1, hd])
        parts.append(y * lax.rsqrt(jnp.mean(y * y, axis=-1, keepdims=True) + EPS))
    yn = jnp.concatenate(parts, axis=-1) * nw_ref[...] * sg_ref[...].astype(F32)
    out = jnp.dot(yn.astype(BF16), w_ref[...], preferred_element_type=F32)
    out_ref[...] = _layer_norm_rows(ALPHA * x_ref[...] + mod_ref[2:3, :] * out, lg_ref[...], lb_ref[...])


def _out_b(geo, o_prompt, o_sample, sg, norm_w, w_out, x, mod_l, ln_g, ln_b):
    t = geo.t
    tb = TOK_TILE_L
    n_p = geo.t_prompt // tb
    full = lambda shape: pl.BlockSpec(shape, lambda i: (0,) * len(shape))
    return pl.pallas_call(
        functools.partial(_out_b_kernel, n_prompt_tiles=n_p),
        name="out_b",
        grid=(t // tb,),
        in_specs=[
            pl.BlockSpec((2, NH_B, tb, DK_B), lambda i: (0, 0, jnp.minimum(i, n_p - 1), 0)),
            pl.BlockSpec((2, NH_B, tb, DK_B), lambda i: (0, 0, jnp.maximum(i - n_p, 0), 0)),
            pl.BlockSpec((tb, D), lambda i: (i, 0)),
            full((1, D)), full((D, D)),
            pl.BlockSpec((tb, D), lambda i: (i, 0)),
            pl.BlockSpec((None, 6, D), lambda i: (geo.cond_row(i, tb), 0, 0)),
            full((1, D)), full((1, D)),
        ],
        out_specs=pl.BlockSpec((tb, D), lambda i: (i, 0)),
        out_shape=jax.ShapeDtypeStruct((t, D), F32),
        compiler_params=_params(("parallel",)),
    )(o_prompt, o_sample, sg, norm_w.reshape(1, D).astype(F32), w_out.astype(BF16), x, mod_l,
      ln_g.reshape(1, D), ln_b.reshape(1, D))


MOE_BLK = 512
U32 = jnp.uint32
ROW_WORDS = D // 2
CHUNK_W = 256
ROW_CHUNKS = ROW_WORDS // CHUNK_W
SC_WINDOW = 128


def _pack_rows(x):
    hi = pltpu.bitcast(x[:, :ROW_WORDS].astype(BF16).astype(F32), U32)
    lo = pltpu.bitcast(x[:, ROW_WORDS:].astype(BF16).astype(F32), U32)
    return hi | (lo >> 16)


def _unpack_rows(words):
    hi = pltpu.bitcast(words & jnp.uint32(0xFFFF0000), F32)
    lo = pltpu.bitcast(words << 16, F32)
    return jnp.concatenate([hi, lo], axis=1)


def _store_chunks(chunk_ref, x):
    words = _pack_rows(x)
    for c in range(ROW_CHUNKS):
        chunk_ref(c)[...] = words[:, c * CHUNK_W:(c + 1) * CHUNK_W]


def _load_chunks(chunk_ref):
    return _unpack_rows(jnp.concatenate([chunk_ref(c)[...] for c in range(ROW_CHUNKS)], axis=1))


def _first_index(hit, iota, size, axis):
    return jnp.min(jnp.where(hit, iota, size), axis=axis, keepdims=True)


def _router_kernel(x_ref, mod_ref, wrt_ref, eb_ref, e_ref, w_ref, r_ref, cnt_ref, h_ref, cnt_sc):
    i = pl.program_id(0)
    tb = x_ref.shape[0]

    @pl.when(i == 0)
    def _():
        cnt_sc[...] = jnp.zeros_like(cnt_sc)

    h = x_ref[...] * (1.0 + mod_ref[4:5, :]) + mod_ref[3:4, :]
    _store_chunks(lambda c: h_ref.at[c], h)
    logits = lax.dot_general(wrt_ref[...], h, NT_DIMS, precision=HIGHEST, preferred_element_type=F32)
    scores = jax.nn.sigmoid(logits)
    sel = scores + eb_ref[...]

    g3 = sel.reshape(N_GROUPS, GROUP_SIZE, tb)
    io3 = lax.broadcasted_iota(jnp.int32, g3.shape, 1)
    m1 = jnp.max(g3, axis=1, keepdims=True)
    first = _first_index(g3 == m1, io3, GROUP_SIZE, 1)
    m2 = jnp.max(jnp.where(io3 == first, -jnp.inf, g3), axis=1, keepdims=True)
    gscore = (m1 + m2).reshape(N_GROUPS, tb)

    iog = lax.broadcasted_iota(jnp.int32, gscore.shape, 0)
    gmask = jnp.zeros(gscore.shape, F32)
    for _ in range(TOPK_GROUPS):
        gm = jnp.max(gscore, axis=0, keepdims=True)
        pick = iog == _first_index(gscore == gm, iog, N_GROUPS, 0)
        gmask = jnp.where(pick, 1.0, gmask)
        gscore = jnp.where(pick, -jnp.inf, gscore)
    emask = jnp.broadcast_to(gmask.reshape(N_GROUPS, 1, tb), (N_GROUPS, GROUP_SIZE, tb)).reshape(N_EXPERTS, tb)
    cand = jnp.where(emask > 0.0, sel, -jnp.inf)

    ioe = lax.broadcasted_iota(jnp.int32, cand.shape, 0)
    picks, wts = [], []
    onehot = jnp.zeros(cand.shape, F32)
    for _ in range(TOP_K):
        cm = jnp.max(cand, axis=0, keepdims=True)
        idx = _first_index(cand == cm, ioe, N_EXPERTS, 0)
        pick = ioe == idx
        picks.append(pick)
        wts.append(jnp.sum(jnp.where(pick, scores, 0.0), axis=0, keepdims=True))
        onehot = onehot + pick.astype(F32)
        cand = jnp.where(pick, -jnp.inf, cand)
        e_ref[pl.ds(len(picks) - 1, 1), :] = idx
    wsum = wts[0]
    for wk in wts[1:]:
        wsum = wsum + wk
    for k in range(TOP_K):
        w_ref[pl.ds(k, 1), :] = wts[k] / wsum * ROUTED_SCALE

    r_io = lax.broadcasted_iota(jnp.int32, (tb, tb), 0)
    c_io = lax.broadcasted_iota(jnp.int32, (tb, tb), 1)
    before = (r_io < c_io).astype(BF16)
    rank = cnt_sc[:, 0:1] + jnp.dot(onehot.astype(BF16), before, preferred_element_type=F32)
    for k in range(TOP_K):
        r_ref[pl.ds(k, 1), :] = jnp.sum(jnp.where(picks[k], rank, 0.0), axis=0, keepdims=True).astype(jnp.int32)
    cnt_sc[...] = cnt_sc[...] + jnp.sum(onehot, axis=1, keepdims=True)
    cnt_ref[...] = cnt_sc[...]


def _router(geo, x, mod_l, w_router, e_bias):
    t = geo.t
    tb = TOK_TILE_L
    full = lambda shape: pl.BlockSpec(shape, lambda i: (0,) * len(shape))
    e, w, r, cnt, h = pl.pallas_call(
        _router_kernel,
        name="router",
        grid=(t // tb,),
        in_specs=[
            pl.BlockSpec((tb, D), lambda i: (i, 0)),
            pl.BlockSpec((None, 6, D), lambda i: (geo.cond_row(i, tb), 0, 0)),
            full((N_EXPERTS, D)), full((N_EXPERTS, 1)),
        ],
        out_specs=[
            pl.BlockSpec((TOP_K, tb), lambda i: (0, i)),
            pl.BlockSpec((TOP_K, tb), lambda i: (0, i)),
            pl.BlockSpec((TOP_K, tb), lambda i: (0, i)),
            full((N_EXPERTS, LANES)),
            pl.BlockSpec((ROW_CHUNKS, tb, CHUNK_W), lambda i: (0, i, 0)),
        ],
        out_shape=[
            jax.ShapeDtypeStruct((TOP_K, t), jnp.int32),
            jax.ShapeDtypeStruct((TOP_K, t), F32),
            jax.ShapeDtypeStruct((TOP_K, t), jnp.int32),
            jax.ShapeDtypeStruct((N_EXPERTS, LANES), F32),
            jax.ShapeDtypeStruct((ROW_CHUNKS, t, CHUNK_W), U32),
        ],
        scratch_shapes=[pltpu.VMEM((N_EXPERTS, LANES), F32)],
        compiler_params=_params(("arbitrary",)),
    )(x, mod_l, w_router.T.astype(F32), e_bias.reshape(N_EXPERTS, 1).astype(F32))
    return e, w, r, cnt[:, 0].astype(jnp.int32), h


def _slot_kernel(pstart_ref, e_ref, r_ref, o_ref):
    e = e_ref[...]
    slot = r_ref[...]
    for x in range(N_EXPERTS):
        slot = slot + jnp.where(e == x, pstart_ref[x], 0)
    o_ref[...] = slot


def _slots(geo, pstart, top_e, rank):
    tb = SLOT_TILE
    return pl.pallas_call(
        _slot_kernel,
        name="slots",
        grid_spec=pltpu.PrefetchScalarGridSpec(
            num_scalar_prefetch=1,
            grid=(geo.t // tb,),
            in_specs=[pl.BlockSpec((TOP_K, tb), lambda i, p: (0, i)),
                      pl.BlockSpec((TOP_K, tb), lambda i, p: (0, i))],
            out_specs=pl.BlockSpec((TOP_K, tb), lambda i, p: (0, i)),
        ),
        out_shape=jax.ShapeDtypeStruct((TOP_K, geo.t), jnp.int32),
        compiler_params=_params(("parallel",)),
    )(pstart, top_e, rank)


def _block_meta_kernel(pstart_ref, counts_ref, pend_ref, e_ref, v_ref):
    row0 = lax.broadcasted_iota(jnp.int32, e_ref.shape, 1) * MOE_BLK
    blk_e = jnp.zeros(e_ref.shape, jnp.int32)
    for x in range(N_EXPERTS):
        blk_e = blk_e + jnp.where(pend_ref[x] <= row0, 1, 0)
    blk_e = jnp.minimum(blk_e, N_EXPERTS - 1)
    last = jnp.zeros(e_ref.shape, jnp.int32)
    for x in range(N_EXPERTS):
        last = last + jnp.where(blk_e == x, pstart_ref[x] + counts_ref[x], 0)
    e_ref[...] = blk_e
    v_ref[...] = jnp.clip(last - row0, 0, MOE_BLK)


def _block_meta(pstart, counts, pend, n_blocks):
    e, v = pl.pallas_call(
        _block_meta_kernel,
        name="block_meta",
        grid_spec=pltpu.PrefetchScalarGridSpec(
            num_scalar_prefetch=3,
            grid=(1,),
            in_specs=[],
            out_specs=[pl.BlockSpec((1, n_blocks), lambda i, a, b, c: (0, 0)),
                       pl.BlockSpec((1, n_blocks), lambda i, a, b, c: (0, 0))],
        ),
        out_shape=[jax.ShapeDtypeStruct((1, n_blocks), jnp.int32), jax.ShapeDtypeStruct((1, n_blocks), jnp.int32)],
        compiler_params=_params(("arbitrary",)),
    )(pstart, counts, pend)
    return e[0], v[0]


def _sc_mesh():
    return plsc.VectorSubcoreMesh(core_axis_name="core", subcore_axis_name="subcore")


def _sc_scatter(rows, idx, n_out, src_block):
    n_idx = idx.shape[0]

    @pl.kernel(out_type=jax.ShapeDtypeStruct((n_out, CHUNK_W), rows.dtype), mesh=_sc_mesh(), scratch_types=[],
               name="sc_dispatch")
    def scatter(x_hbm, i_hbm, o_hbm):
        def body(x_vmem, i_vmem):
            pltpu.sync_copy(x_vmem, o_hbm.at[i_vmem.at[0]])

        pltpu.emit_pipeline(
            body,
            grid=(n_idx // SC_WINDOW,),
            in_specs=[pl.BlockSpec((SC_WINDOW, CHUNK_W), index_map=lambda w: (src_block(w), 0)),
                      pl.BlockSpec((1, SC_WINDOW), index_map=lambda w: (0, w))],
            out_specs=[],
            core_axis_name=("core", "subcore"),
            dimension_semantics=(pltpu.PARALLEL,),
        )(x_hbm, i_hbm)

    return scatter(rows, idx.reshape(1, n_idx))


def _sc_gather(table, idx):
    n_idx = idx.shape[0]

    @pl.kernel(out_type=jax.ShapeDtypeStruct((n_idx, CHUNK_W), table.dtype), mesh=_sc_mesh(),
               name="sc_combine_gather")
    def gather(t_hbm, i_hbm, o_hbm):
        def body(i_vmem, o_vmem):
            pltpu.sync_copy(t_hbm.at[i_vmem.at[0]], o_vmem)

        pltpu.emit_pipeline(
            body,
            grid=(n_idx // SC_WINDOW,),
            in_specs=[pl.BlockSpec((1, SC_WINDOW), index_map=lambda w: (0, w))],
            out_specs=[pl.BlockSpec((SC_WINDOW, CHUNK_W), index_map=lambda w: (w, 0))],
            core_axis_name=("core", "subcore"),
            dimension_semantics=(pltpu.PARALLEL,),
        )(i_hbm, o_hbm)

    return gather(table, idx.reshape(1, n_idx))


def _ffn_kernel(blk_e_ref, blk_valid_ref, n_used_ref, xs_ref, wg_ref, wu_ref, wd_ref, y_ref, wg_sc, wu_sc, wd_sc):
    b = pl.program_id(0)
    used = b < n_used_ref[0]
    new_expert = (b == 0) | (blk_e_ref[b] != blk_e_ref[jnp.maximum(b - 1, 0)])

    @pl.when(used & new_expert)
    def _():
        wg_sc[...] = wg_ref[...].astype(BF16)
        wu_sc[...] = wu_ref[...].astype(BF16)
        wd_sc[...] = wd_ref[...].astype(BF16)

    @pl.when(used)
    def _():
        x = _load_chunks(lambda c: xs_ref.at[c])
        row = lax.broadcasted_iota(jnp.int32, (MOE_BLK, 1), 0)
        x = jnp.where(row < blk_valid_ref[b], x, 0.0).astype(BF16)
        g = jnp.dot(x, wg_sc[...], preferred_element_type=F32)
        u = jnp.dot(x, wu_sc[...], preferred_element_type=F32)
        hmid = (g * jax.nn.sigmoid(g) * u).astype(BF16)
        _store_chunks(lambda c: y_ref.at[c], jnp.dot(hmid, wd_sc[...], preferred_element_type=F32))

    @pl.when(jnp.logical_not(used))
    def _():
        y_ref[...] = jnp.zeros_like(y_ref)


def _ffn(xs, blk_e, blk_valid, n_used, layer, wg, wu, wd, n_blocks):
    def blk(b, be, bv, nu):
        return jnp.maximum(jnp.minimum(b, nu[0] - 1), 0)

    def w_idx(b, be, bv, nu):
        return (layer, be[blk(b, be, bv, nu)], 0, 0)

    return pl.pallas_call(
        _ffn_kernel,
        name="expert_ffn",
        grid_spec=pltpu.PrefetchScalarGridSpec(
            num_scalar_prefetch=3,
            grid=(n_blocks,),
            in_specs=[
                pl.BlockSpec((ROW_CHUNKS, MOE_BLK, CHUNK_W), lambda b, be, bv, nu: (0, blk(b, be, bv, nu), 0)),
                pl.BlockSpec((None, None, D, D_EXPERT), w_idx),
                pl.BlockSpec((None, None, D, D_EXPERT), w_idx),
                pl.BlockSpec((None, None, D_EXPERT, D), w_idx),
            ],
            out_specs=pl.BlockSpec((ROW_CHUNKS, MOE_BLK, CHUNK_W), lambda b, be, bv, nu: (0, b, 0)),
            scratch_shapes=[pltpu.VMEM((D, D_EXPERT), BF16), pltpu.VMEM((D, D_EXPERT), BF16),
                            pltpu.VMEM((D_EXPERT, D), BF16)],
        ),
        out_shape=jax.ShapeDtypeStruct(xs.shape, U32),
        compiler_params=_params(("arbitrary",)),
    )(blk_e, blk_valid, n_used, xs, wg, wu, wd)


def _combine_kernel(x_ref, mod_ref, wt_ref, y_ref, sg_ref, su_ref, sd_ref, lg_ref, lb_ref, *o_refs, n_prompt_tiles):
    x = x_ref[...]
    hb = (x * (1.0 + mod_ref[4:5, :]) + mod_ref[3:4, :]).astype(BF16)
    g = jnp.dot(hb, sg_ref[...], preferred_element_type=F32)
    u = jnp.dot(hb, su_ref[...], preferred_element_type=F32)
    ff = jnp.dot((g * jax.nn.sigmoid(g) * u).astype(BF16), sd_ref[...], preferred_element_type=F32)
    for k in range(TOP_K):
        ff = ff + _load_chunks(lambda c: y_ref.at[c, k]) * wt_ref[:, k:k + 1]
    out = _layer_norm_rows(ALPHA * x + mod_ref[5:6, :] * ff, lg_ref[...], lb_ref[...])
    if len(o_refs) == 1:
        o_refs[0][...] = out
    else:
        is_prompt = pl.program_id(0) < n_prompt_tiles

        @pl.when(is_prompt)
        def _():
            o_refs[0][...] = out

        @pl.when(jnp.logical_not(is_prompt))
        def _():
            o_refs[1][...] = out


def _combine(geo, x, mod_l, wt, ytok, sg, su, sd, ln_g, ln_b, split=False):
    tb = TOK_TILE_L
    n_p = geo.t_prompt // tb
    full = lambda shape: pl.BlockSpec(shape, lambda i: (0,) * len(shape))
    if split:
        out_specs = [pl.BlockSpec((tb, D), lambda i: (jnp.minimum(i, n_p - 1), 0)),
                     pl.BlockSpec((tb, D), lambda i: (jnp.maximum(i - n_p, 0), 0))]
        out_shape = [jax.ShapeDtypeStruct((geo.t_prompt, D), F32), jax.ShapeDtypeStruct((geo.t_sample, D), F32)]
    else:
        out_specs = pl.BlockSpec((tb, D), lambda i: (i, 0))
        out_shape = jax.ShapeDtypeStruct((geo.t, D), F32)
    return pl.pallas_call(
        functools.partial(_combine_kernel, n_prompt_tiles=n_p),
        name="combine",
        grid=(geo.t // tb,),
        in_specs=[
            pl.BlockSpec((tb, D), lambda i: (i, 0)),
            pl.BlockSpec((None, 6, D), lambda i: (geo.cond_row(i, tb), 0, 0)),
            pl.BlockSpec((tb, TOP_K), lambda i: (i, 0)),
            pl.BlockSpec((ROW_CHUNKS, TOP_K, tb, CHUNK_W), lambda i: (0, 0, i, 0)),
            full((D, D_EXPERT)), full((D, D_EXPERT)), full((D_EXPERT, D)), full((1, D)), full((1, D)),
        ],
        out_specs=out_specs,
        out_shape=out_shape,
        compiler_params=_params(("arbitrary",)),
    )(x, mod_l, wt, ytok, sg.astype(BF16), su.astype(BF16), sd.astype(BF16),
      ln_g.reshape(1, D), ln_b.reshape(1, D))


def _moe_layer(geo, x, mod_l, w_router, e_bias, layer, wg, wu, wd, sg, su, sd, ln_g, ln_b, split=False):
    t = geo.t
    top_e, w, rank, counts, h = _router(geo, x, mod_l, w_router, e_bias)
    n_blocks = (t * TOP_K) // MOE_BLK + N_EXPERTS
    n_rows = n_blocks * MOE_BLK
    padded = (counts + MOE_BLK - 1) // MOE_BLK * MOE_BLK
    pend = jnp.cumsum(padded)
    pstart = (pend - padded).astype(jnp.int32)
    blk_e, blk_valid = _block_meta(pstart, counts, pend.astype(jnp.int32), n_blocks)
    n_used = (pend[-1:] // MOE_BLK).astype(jnp.int32)
    slots = _slots(geo, pstart, top_e, rank)
    idx = (slots.reshape(1, TOP_K * t) + (jnp.arange(ROW_CHUNKS, dtype=jnp.int32) * n_rows)[:, None]).reshape(-1)
    win_per_chunk = TOP_K * t // SC_WINDOW
    tok_windows = t // SC_WINDOW

    def src_block(wdw):
        return (wdw // win_per_chunk) * tok_windows + (wdw % win_per_chunk) % tok_windows

    xs = _sc_scatter(h.reshape(ROW_CHUNKS * t, CHUNK_W), idx, ROW_CHUNKS * n_rows, src_block)
    yb = _ffn(xs.reshape(ROW_CHUNKS, n_rows, CHUNK_W), blk_e, blk_valid, n_used, layer, wg, wu, wd, n_blocks)
    ytok = _sc_gather(yb.reshape(ROW_CHUNKS * n_rows, CHUNK_W), idx)
    return _combine(geo, x, mod_l, w.T, ytok.reshape(ROW_CHUNKS, TOP_K, t, CHUNK_W), sg, su, sd, ln_g, ln_b,
                    split=split)


def _pos_embed(rows):
    quarter = D // 4
    omega = 1.0 / (POS_BASE ** (jnp.arange(quarter, dtype=F32) / quarter))
    r, col = jnp.meshgrid(jnp.arange(rows, dtype=F32), jnp.arange(GRID_W, dtype=F32), indexing='ij')
    r = r.reshape(-1, 1) * omega
    col = col.reshape(-1, 1) * omega
    return jnp.concatenate([jnp.sin(r), jnp.cos(r), jnp.sin(col), jnp.cos(col)], axis=-1)


def _mlstm_layer(geo, x, mod_l, j, a_w_in, a_b_gates, a_norm, a_w_out, ln_g, ln_b,
                 state_C, state_n, state_m):
    q, kt, v, so, gc, gr = _proj_a(geo, x, mod_l, a_w_in[j], a_b_gates[j])
    hp, c_p, n_p, m_p = _mlstm_scan(q, kt, v, gc, gr, row0=0, n_seq=geo.n_prompt, seq_len=geo.prompt_len)
    ns = geo.n_sample
    n0 = jnp.pad(state_n[:, j].astype(F32)[..., None], ((0, 0),) * 4 + ((0, LANES - 1),))
    m0 = jnp.pad(state_m[:, j].astype(F32), ((0, 0), (0, 0), (0, SUBLANES - NH_A)))
    m0 = jnp.broadcast_to(m0[..., None], (ns, 2, SUBLANES, LANES))
    hs, _, _, _ = _mlstm_scan(q, kt, v, gc, gr, row0=geo.t_prompt, n_seq=ns, seq_len=geo.sample_len,
                              state=(state_C[:, j].astype(F32), n0, m0))
    x1 = _out_a(geo, hp, hs, so, a_norm[j], a_w_out[j], x, mod_l, ln_g, ln_b)
    return x1, c_p, n_p[..., 0], m_p[:, :, :NH_A, 0]


def _hgrn_layer(geo, x, mod_l, j, lb_layer, b_w_in, b_norm, b_w_out, ln_g, ln_b, state_S):
    q, pre, v, sg = _proj_b(geo, x, mod_l, b_w_in[j])
    lbd = lb_layer.reshape(2, NH_B, 1, DK_B)
    op, s_p = _hgrn_scan(q, pre, v, lbd, row0=0, n_seq=geo.n_prompt, seq_len=geo.prompt_len)
    os_, _ = _hgrn_scan(q, pre, v, lbd, row0=geo.t_prompt, n_seq=geo.n_sample, seq_len=geo.sample_len,
                        state=state_S[:, j].astype(F32))
    x1 = _out_b(geo, op, os_, sg, b_norm[j], b_w_out[j], x, mod_l, ln_g, ln_b)
    return x1, s_p


def kernel(x_prompt, x_sample, state_mlstm_C, state_mlstm_n, state_mlstm_m, state_hgrn_S, c, c_ctx, w_mod, b_mod, ln_g, ln_b, a_w_in, a_b_gates, a_norm, a_w_out, b_w_in, b_lb, b_norm, b_w_out, w_router, e_bias, w_gate, w_up, w_down, ws_gate, ws_up, ws_down):
    bp, sp, _ = x_prompt.shape
    bs, ss, _ = x_sample.shape
    geo = Geometry(bp, sp, bs, ss)
    cond = jnp.zeros((COND_ROWS, D), F32).at[0].set(c_ctx).at[1:1 + bs].set(c)
    mod = _modulation(cond, w_mod, b_mod)
    x = (x_prompt.reshape(-1, D), x_sample.reshape(-1, D), _pos_embed(ss // GRID_W))
    x1, new_c, new_n, new_m = _mlstm_layer(geo, x, mod[0], 0, a_w_in, a_b_gates, a_norm, a_w_out,
                                           ln_g[0, 0], ln_b[0, 0], state_mlstm_C, state_mlstm_n, state_mlstm_m)
    x2 = _moe_layer(geo, x1, mod[0], w_router[0], e_bias[0], 0, w_gate, w_up, w_down, ws_gate[0], ws_up[0], ws_down[0],
                    ln_g[0, 1], ln_b[0, 1])
    sm = jax.nn.softmax(b_lb.astype(F32), axis=0)
    lb_all = jnp.cumsum(sm, axis=0) - sm[0]
    x3, new_s = _hgrn_layer(geo, x2, mod[1], 0, lb_all[1], b_w_in, b_norm, b_w_out, ln_g[1, 0], ln_b[1, 0],
                            state_hgrn_S)
    y_p, y_s = _moe_layer(geo, x3, mod[1], w_router[1], e_bias[1], 1, w_gate, w_up, w_down, ws_gate[1], ws_up[1],
                          ws_down[1], ln_g[1, 1], ln_b[1, 1], split=True)
    y_prompt = y_p.reshape(bp, sp, D)
    y_sample = y_s.reshape(bs, ss, D)
    return y_prompt, y_sample, new_c[:, None], new_n[:, None], new_m[:, None], new_s[:, None]
```

```python
import functools

import jax
import jax.numpy as jnp
from jax import lax
from jax.experimental import pallas as pl
from jax.experimental.pallas import tpu as pltpu
from jax.experimental.pallas import tpu_sc as plsc

F32 = jnp.float32
BF16 = jnp.bfloat16
HIGHEST = lax.Precision.HIGHEST

D = 1024
DEPTH = 2
GRID_W = 64
POS_BASE = 10000.0
EPS = 1e-6
ALPHA = (2.0 * DEPTH) ** 0.25
NH_A, DK_A, DV_A = 4, 128, 256
QK_A, V_A = NH_A * DK_A, NH_A * DV_A
NH_B, DK_B = 8, 128
N_EXPERTS, TOP_K, N_GROUPS, TOPK_GROUPS = 64, 8, 8, 4
GROUP_SIZE = N_EXPERTS // N_GROUPS
D_EXPERT = D // 4
ROUTED_SCALE = 2.5

LANES = 128
SUBLANES = 8
COND_ROWS = 8
TOK_TILE = 256
TOK_TILE_L = 512
SLOT_TILE = 2048
CHUNK_A = 256
VMEM_LIMIT = 56 * 1024 * 1024

NT_DIMS = (((1,), (1,)), ((), ()))


def _params(sem):
    return pltpu.CompilerParams(dimension_semantics=sem, vmem_limit_bytes=VMEM_LIMIT)


def _split3(x):
    hi = x.astype(BF16)
    r = x - hi.astype(F32)
    mid = r.astype(BF16)
    lo = (r - mid.astype(F32)).astype(BF16)
    return hi, mid, lo


def _dot3(a_bf, x, transpose_side=None):
    hi, mid, lo = _split3(x)
    return (jnp.dot(a_bf, hi, preferred_element_type=F32)
            + jnp.dot(a_bf, mid, preferred_element_type=F32)
            + jnp.dot(a_bf, lo, preferred_element_type=F32))


def _dot3_r(x, a_bf):
    hi, mid, lo = _split3(x)
    return (jnp.dot(hi, a_bf, preferred_element_type=F32)
            + jnp.dot(mid, a_bf, preferred_element_type=F32)
            + jnp.dot(lo, a_bf, preferred_element_type=F32))


def _log_sigmoid(x):
    return jnp.minimum(x, 0.0) - jnp.log1p(jnp.exp(-jnp.abs(x)))


def _layer_norm_rows(x, g, b):
    mu = jnp.mean(x, axis=-1, keepdims=True)
    xc = x - mu
    var = jnp.mean(xc * xc, axis=-1, keepdims=True)
    return xc * lax.rsqrt(var + EPS) * g + b


class Geometry:
    def __init__(self, n_prompt, prompt_len, n_sample, sample_len):
        self.n_prompt, self.prompt_len = n_prompt, prompt_len
        self.n_sample, self.sample_len = n_sample, sample_len
        self.t_prompt = n_prompt * prompt_len
        self.t_sample = n_sample * sample_len
        self.t = self.t_prompt + self.t_sample
        assert self.t_prompt % TOK_TILE_L == 0 and sample_len % TOK_TILE_L == 0 and self.t % SLOT_TILE == 0
        assert n_sample + 1 <= COND_ROWS

    def cond_row(self, tile, tile_rows):
        n_p = self.t_prompt // tile_rows
        return jnp.where(tile < n_p, 0, 1 + (tile - n_p) // (self.sample_len // tile_rows))


def _mod_kernel(cond_ref, w_ref, b_ref, o_ref):
    c = cond_ref[...]
    s = c * jax.nn.sigmoid(c)
    o_ref[0, 0] = jnp.dot(s, w_ref[0], precision=HIGHEST, preferred_element_type=F32) + b_ref[0, 0]


def _modulation(cond, w_mod, b_mod):
    out = pl.pallas_call(
        _mod_kernel,
        name="modulation",
        grid=(DEPTH, 6),
        in_specs=[
            pl.BlockSpec((COND_ROWS, D), lambda l, j: (0, 0)),
            pl.BlockSpec((1, D, D), lambda l, j: (l, 0, j)),
            pl.BlockSpec((1, 1, 1, D), lambda l, j: (l, j, 0, 0)),
        ],
        out_specs=pl.BlockSpec((1, 1, COND_ROWS, D), lambda l, j: (l, j, 0, 0)),
        out_shape=jax.ShapeDtypeStruct((DEPTH, 6, COND_ROWS, D), F32),
        compiler_params=_params(("arbitrary", "arbitrary")),
    )(cond, w_mod, b_mod.reshape(DEPTH, 6, 1, D))
    return out.transpose(0, 2, 1, 3)


def _embed_specs(geo, tb):
    n_p = geo.t_prompt // tb
    per_seq = geo.sample_len // tb
    return [pl.BlockSpec((tb, D), lambda i: (jnp.minimum(i, n_p - 1), 0)),
            pl.BlockSpec((tb, D), lambda i: (jnp.maximum(i - n_p, 0), 0)),
            pl.BlockSpec((tb, D), lambda i: (jnp.maximum(i - n_p, 0) % per_seq, 0))]


def _embed_tile(xp_ref, xs_ref, pos_ref, n_prompt_tiles):
    return jnp.where(pl.program_id(0) < n_prompt_tiles, xp_ref[...], xs_ref[...] + pos_ref[...])


def _proj_a_kernel(xp_ref, xs_ref, pos_ref, mod_ref, wq_ref, wkt_ref, wvo_ref, wgt_ref, bgt_ref,
                   q_ref, kt_ref, v_ref, so_ref, gr_ref, *, n_prompt_tiles):
    x = _embed_tile(xp_ref, xs_ref, pos_ref, n_prompt_tiles)
    h = x * (1.0 + mod_ref[1:2, :]) + mod_ref[0:1, :]
    hb = h.astype(BF16)
    q_ref[...] = jnp.dot(hb, wq_ref[...], preferred_element_type=F32).astype(BF16)
    kt = lax.dot_general(wkt_ref[...], hb, NT_DIMS, preferred_element_type=F32)
    kt_ref[...] = (kt * (DK_A ** -0.5)).astype(BF16)
    vo = jnp.dot(hb, wvo_ref[...], preferred_element_type=F32)
    v_ref[...] = vo[:, :V_A].astype(BF16)
    so_ref[...] = jax.nn.sigmoid(vo[:, V_A:]).astype(BF16)
    gr_ref[...] = lax.dot_general(wgt_ref[...], h, NT_DIMS, precision=HIGHEST,
                                  preferred_element_type=F32) + bgt_ref[...]


def _proj_a(geo, x, mod_l, w_in, b_gates):
    t = geo.t
    n_gate = 4 * NH_A
    wq = w_in[:, :QK_A].astype(BF16)
    wkt = w_in[:, QK_A:2 * QK_A].T.astype(BF16)
    wvo = w_in[:, 2 * QK_A:2 * QK_A + 2 * V_A].astype(BF16)
    wg = w_in[:, 2 * QK_A + 2 * V_A:]
    bg = b_gates.reshape(n_gate).astype(F32)
    tb = TOK_TILE_L
    full = lambda shape: pl.BlockSpec(shape, lambda i: (0,) * len(shape))
    return pl.pallas_call(
        functools.partial(_proj_a_kernel, n_prompt_tiles=geo.t_prompt // tb),
        name="proj_a",
        grid=(t // tb,),
        in_specs=_embed_specs(geo, tb) + [
            pl.BlockSpec((None, 6, D), lambda i: (geo.cond_row(i, tb), 0, 0)),
            full((D, QK_A)), full((QK_A, D)), full((D, 2 * V_A)), full((n_gate, D)), full((n_gate, 1)),
        ],
        out_specs=[
            pl.BlockSpec((tb, QK_A), lambda i: (i, 0)),
            pl.BlockSpec((QK_A, tb), lambda i: (0, i)),
            pl.BlockSpec((tb, V_A), lambda i: (i, 0)),
            pl.BlockSpec((tb, V_A), lambda i: (i, 0)),
            pl.BlockSpec((n_gate, tb), lambda i: (0, i)),
        ],
        out_shape=[
            jax.ShapeDtypeStruct((t, QK_A), BF16),
            jax.ShapeDtypeStruct((QK_A, t), BF16),
            jax.ShapeDtypeStruct((t, V_A), BF16),
            jax.ShapeDtypeStruct((t, V_A), BF16),
            jax.ShapeDtypeStruct((n_gate, t), F32),
        ],
        compiler_params=_params(("parallel",)),
    )(*x, mod_l, wq, wkt, wvo, wg.T, bg.reshape(n_gate, 1))


def _mlstm_scan_kernel(*refs, chunk, has_state):
    if has_state:
        (q_ref, kt_ref, v_ref, gr_ref, c0_ref, n0_ref, m0_ref,
         h_ref, c_out, n_out, m_out, c_sc, n_sc, m_sc) = refs
    else:
        (q_ref, kt_ref, v_ref, gr_ref,
         h_ref, c_out, n_out, m_out, c_sc, n_sc, m_sc) = refs
    L = chunk
    d = pl.program_id(1)
    c = pl.program_id(2)
    fwd = d == 0

    @pl.when(c == 0)
    def _():
        if has_state:
            c_sc[...] = c0_ref[0, 0]
            n_sc[...] = n0_ref[0, 0]
            m_sc[...] = m0_ref[0, 0]
        else:
            c_sc[...] = jnp.zeros_like(c_sc)
            n_sc[...] = jnp.zeros_like(n_sc)
            m_sc[...] = jnp.zeros_like(m_sc)

    row = lax.broadcasted_iota(jnp.int32, (L, L), 0)
    col = lax.broadcasted_iota(jnp.int32, (L, L), 1)
    sgn = 1 - 2 * d
    causal = (row - col) * sgn >= 0
    tri_t = ((col - row) * sgn >= 0).astype(BF16)

    gr = gr_ref[...]
    br_all = _dot3_r(_log_sigmoid(gr), tri_t)
    bc_all = jnp.concatenate([br_all, jnp.zeros((LANES - br_all.shape[0], L), F32)], axis=0).T
    ones_blk = (lax.broadcasted_iota(jnp.int32, (L, LANES), 1) == 0).astype(BF16)

    for h in range(NH_A):
        b_c = jnp.where(fwd, bc_all[:, 4 + h:5 + h], bc_all[:, 12 + h:13 + h])
        b_r = jnp.where(fwd, br_all[4 + h:5 + h, :], br_all[12 + h:13 + h, :])
        i_r = jnp.where(fwd, gr[h:h + 1, :], gr[8 + h:9 + h, :])
        bl = jnp.where(fwd, b_r[:, L - 1:L], b_r[:, 0:1])
        q = q_ref[:, h * DK_A:(h + 1) * DK_A]
        kt = kt_ref[h * DK_A:(h + 1) * DK_A, :]
        v = v_ref[:, h * DV_A:(h + 1) * DV_A]
        m = m_sc[h:h + 1, 0:1]
        cst = c_sc[h]
        nst = n_sc[h]

        a_r = i_r - b_r
        logd = jnp.where(causal, b_c + a_r, -jnp.inf)
        inter = b_c + m
        m_t = jnp.maximum(inter, jnp.max(logd, axis=1, keepdims=True))
        dmat = jnp.exp(logd - m_t)
        e_int = jnp.exp(inter - m_t)
        s = (jnp.dot(q, kt, preferred_element_type=F32) * dmat).astype(BF16)
        num = (jnp.dot(s, v, preferred_element_type=F32)
               + e_int * jnp.dot(q, cst.astype(BF16), preferred_element_type=F32))
        den = (jnp.dot(s, ones_blk, preferred_element_type=F32)
               + e_int * jnp.dot(q, nst.astype(BF16), preferred_element_type=F32))[:, 0:1]
        h_ref[:, h * DV_A:(h + 1) * DV_A] = num / jnp.maximum(jnp.abs(den), jnp.exp(-m_t))

        logw = bl + a_r
        m_new = jnp.maximum(bl + m, jnp.max(logw, axis=1, keepdims=True))
        w = jnp.exp(logw - m_new)
        decay = jnp.exp(bl + m - m_new)
        kw = (kt.astype(F32) * w).astype(BF16)
        c_sc[h] = decay * cst + jnp.dot(kw, v, preferred_element_type=F32)
        n_sc[h] = decay * nst + jnp.dot(kw, ones_blk, preferred_element_type=F32)
        m_sc[h:h + 1, :] = jnp.broadcast_to(m_new, (1, LANES))

    @pl.when(c == pl.num_programs(2) - 1)
    def _():
        c_out[0, 0] = c_sc[...]
        n_out[0, 0] = n_sc[...]
        m_out[0, 0] = m_sc[...]


def _mlstm_scan(q, kt, v, gr, *, row0, n_seq, seq_len, state=None):
    L = CHUNK_A
    nc = seq_len // L
    blk0 = row0 // L

    def loc_blk(b, d, c):
        return b * nc + c + d * (nc - 1 - 2 * c)

    def tok_blk(b, d, c):
        return blk0 + loc_blk(b, d, c)

    in_specs = [
        pl.BlockSpec((L, QK_A), lambda b, d, c: (tok_blk(b, d, c), 0)),
        pl.BlockSpec((QK_A, L), lambda b, d, c: (0, tok_blk(b, d, c))),
        pl.BlockSpec((L, V_A), lambda b, d, c: (tok_blk(b, d, c), 0)),
        pl.BlockSpec((4 * NH_A, L), lambda b, d, c: (0, tok_blk(b, d, c))),
    ]
    args = [q, kt, v, gr]
    if state is not None:
        in_specs += [
            pl.BlockSpec((1, 1, NH_A, DK_A, DV_A), lambda b, d, c: (b, d, 0, 0, 0)),
            pl.BlockSpec((1, 1, NH_A, DK_A, LANES), lambda b, d, c: (b, d, 0, 0, 0)),
            pl.BlockSpec((1, 1, SUBLANES, LANES), lambda b, d, c: (b, d, 0, 0)),
        ]
        args += list(state)
    return pl.pallas_call(
        functools.partial(_mlstm_scan_kernel, chunk=L, has_state=state is not None),
        name="mlstm_scan_seeded" if state is not None else "mlstm_scan",
        grid=(n_seq, 2, nc),
        in_specs=in_specs,
        out_specs=[
            pl.BlockSpec((None, L, V_A), lambda b, d, c: (d, loc_blk(b, d, c), 0)),
            pl.BlockSpec((1, 1, NH_A, DK_A, DV_A), lambda b, d, c: (b, d, 0, 0, 0)),
            pl.BlockSpec((1, 1, NH_A, DK_A, LANES), lambda b, d, c: (b, d, 0, 0, 0)),
            pl.BlockSpec((1, 1, SUBLANES, LANES), lambda b, d, c: (b, d, 0, 0)),
        ],
        out_shape=[
            jax.ShapeDtypeStruct((2, n_seq * seq_len, V_A), F32),
            jax.ShapeDtypeStruct((n_seq, 2, NH_A, DK_A, DV_A), F32),
            jax.ShapeDtypeStruct((n_seq, 2, NH_A, DK_A, LANES), F32),
            jax.ShapeDtypeStruct((n_seq, 2, SUBLANES, LANES), F32),
        ],
        scratch_shapes=[
            pltpu.VMEM((NH_A, DK_A, DV_A), F32),
            pltpu.VMEM((NH_A, DK_A, LANES), F32),
            pltpu.VMEM((SUBLANES, LANES), F32),
        ],
        compiler_params=_params(("parallel", "parallel", "arbitrary")),
    )(*args)


def _out_a_kernel(hp_ref, hs_ref, so_ref, nw_ref, w_ref, xp_ref, xs_ref, pos_ref, mod_ref, lg_ref, lb_ref, o_ref, *,
                  n_prompt_tiles):
    is_prompt = pl.program_id(0) < n_prompt_tiles
    x = _embed_tile(xp_ref, xs_ref, pos_ref, n_prompt_tiles)
    y = jnp.where(is_prompt, hp_ref[0] + hp_ref[1], hs_ref[0] + hs_ref[1])
    parts = []
    for h in range(NH_A):
        yh = y[:, h * DV_A:(h + 1) * DV_A]
        mu = jnp.mean(yh, axis=-1, keepdims=True)
        yc = yh - mu
        var = jnp.mean(yc * yc, axis=-1, keepdims=True)
        parts.append(yc * lax.rsqrt(var + EPS))
    yn = jnp.concatenate(parts, axis=-1) * nw_ref[...] * so_ref[...].astype(F32)
    out = jnp.dot(yn.astype(BF16), w_ref[...], preferred_element_type=F32)
    o_ref[...] = _layer_norm_rows(ALPHA * x + mod_ref[2:3, :] * out, lg_ref[...], lb_ref[...])


def _out_a(geo, h_prompt, h_sample, so, norm_w, w_out, x, mod_l, ln_g, ln_b):
    t = geo.t
    tb = TOK_TILE_L
    n_p = geo.t_prompt // tb
    full = lambda shape: pl.BlockSpec(shape, lambda i: (0,) * len(shape))
    return pl.pallas_call(
        functools.partial(_out_a_kernel, n_prompt_tiles=n_p),
        name="out_a",
        grid=(t // tb,),
        in_specs=[
            pl.BlockSpec((2, tb, V_A), lambda i: (0, jnp.minimum(i, n_p - 1), 0)),
            pl.BlockSpec((2, tb, V_A), lambda i: (0, jnp.maximum(i - n_p, 0), 0)),
            pl.BlockSpec((tb, V_A), lambda i: (i, 0)),
            full((1, V_A)), full((V_A, D)),
        ] + _embed_specs(geo, tb) + [
            pl.BlockSpec((None, 6, D), lambda i: (geo.cond_row(i, tb), 0, 0)),
            full((1, D)), full((1, D)),
        ],
        out_specs=pl.BlockSpec((tb, D), lambda i: (i, 0)),
        out_shape=jax.ShapeDtypeStruct((t, D), F32),
        compiler_params=_params(("parallel",)),
    )(h_prompt, h_sample, so, norm_w.reshape(1, V_A).astype(F32), w_out.astype(BF16), *x, mod_l,
      ln_g.reshape(1, D), ln_b.reshape(1, D))


def _proj_b_kernel(x_ref, mod_ref, w_ref, q_ref, pre_ref, v_ref, sg_ref):
    h = x_ref[...] * (1.0 + mod_ref[1:2, :]) + mod_ref[0:1, :]
    z = jnp.dot(h.astype(BF16), w_ref[...], preferred_element_type=F32)
    for hd in range(NH_B):
        lo = hd * DK_B
        qh = z[:, lo:lo + DK_B]
        q_ref[hd] = qh * jax.nn.sigmoid(qh)
        pre_ref[0, hd] = z[:, D + lo:D + lo + DK_B]
        pre_ref[1, hd] = z[:, 2 * D + lo:2 * D + lo + DK_B]
        v_ref[hd] = z[:, 3 * D + lo:3 * D + lo + DK_B].astype(BF16)
    g = z[:, 4 * D:]
    sg_ref[...] = (g * jax.nn.sigmoid(g)).astype(BF16)


def _proj_b(geo, x, mod_l, w_in):
    t = geo.t
    tb = TOK_TILE
    return pl.pallas_call(
        _proj_b_kernel,
        name="proj_b",
        grid=(t // tb,),
        in_specs=[
            pl.BlockSpec((tb, D), lambda i: (i, 0)),
            pl.BlockSpec((None, 6, D), lambda i: (geo.cond_row(i, tb), 0, 0)),
            pl.BlockSpec((D, 5 * D), lambda i: (0, 0)),
        ],
        out_specs=[
            pl.BlockSpec((NH_B, tb, DK_B), lambda i: (0, i, 0)),
            pl.BlockSpec((2, NH_B, tb, DK_B), lambda i: (0, 0, i, 0)),
            pl.BlockSpec((NH_B, tb, DK_B), lambda i: (0, i, 0)),
            pl.BlockSpec((tb, D), lambda i: (i, 0)),
        ],
        out_shape=[
            jax.ShapeDtypeStruct((NH_B, t, DK_B), F32),
            jax.ShapeDtypeStruct((2, NH_B, t, DK_B), F32),
            jax.ShapeDtypeStruct((NH_B, t, DK_B), BF16),
            jax.ShapeDtypeStruct((t, D), BF16),
        ],
        compiler_params=_params(("parallel",)),
    )(x, mod_l, w_in.astype(BF16))


CHUNK_B = 128
BAND = SUBLANES // 2
TN_DIMS = (((0,), (0,)), ((), ()))


def _hgrn_head(q, pre, lbv, v_bf, st, fwd):
    L = q.shape[0]
    sg = jax.nn.sigmoid(pre)
    f = lbv + (1.0 - lbv) * sg
    lf = jnp.log(f)
    kk = (1.0 - lbv) * (1.0 - sg)
    row = lax.broadcasted_iota(jnp.int32, (L, L), 0)
    col = lax.broadcasted_iota(jnp.int32, (L, L), 1)
    tri = ((row >= col) if fwd else (row <= col)).astype(BF16)
    b = _dot3(tri, lf)
    blk_bits = row ^ col
    lag = jnp.where(blk_bits < BAND, (row - col) if fwd else (col - row), -1)
    pair_bits = jnp.where((row > col) if fwd else (row < col), blk_bits, L)

    step = 1 if fwd else L - 1
    att = jnp.where(lag == 0, jnp.sum(q * kk, axis=1, keepdims=True), 0.0)
    f_r, kk_r, g = f, kk, f
    for dl in range(1, BAND):
        if dl > 1:
            f_r = pltpu.roll(f_r, step, 0)
            g = g * f_r
        kk_r = pltpu.roll(kk_r, step, 0)
        att = jnp.where(lag == dl, jnp.sum(q * kk_r * g, axis=1, keepdims=True), att)

    w = BAND
    while w < L:
        nb = L // (2 * w)
        b3 = b.reshape(nb, 2 * w, DK_B)
        edge = (b3[:, w - 1:w, :] if fwd else b3[:, w:w + 1, :])
        bmid = jnp.broadcast_to(edge, (nb, 2 * w, DK_B)).reshape(L, DK_B)
        e = jnp.exp(-jnp.abs(b - bmid))
        a = lax.dot_general((q * e).astype(BF16), (kk * e).astype(BF16), NT_DIMS, preferred_element_type=F32)
        att = jnp.where((pair_bits >> (w.bit_length() - 1)) == 1, a, att)
        w *= 2
    o = jnp.dot(att.astype(BF16), v_bf, preferred_element_type=F32)

    bl = b[L - 1:L, :] if fwd else b[0:1, :]
    o = o + lax.dot_general((q * jnp.exp(b)).astype(BF16), st.astype(BF16), NT_DIMS, preferred_element_type=F32)
    kd = (kk * jnp.exp(bl - b)).astype(BF16)
    st_new = jnp.exp(bl) * st + lax.dot_general(v_bf, kd, TN_DIMS, preferred_element_type=F32)
    return o, st_new


def _hgrn_scan_kernel(*refs, has_state):
    if has_state:
        q_ref, pre_ref, v_ref, lb_ref, s0_ref, o_ref, s_out, st_sc = refs
    else:
        q_ref, pre_ref, v_ref, lb_ref, o_ref, s_out, st_sc = refs
    d = pl.program_id(1)
    c = pl.program_id(2)

    @pl.when(c == 0)
    def _():
        if has_state:
            for hd in range(NH_B):
                st_sc[hd] = s0_ref[0, 0, hd].T
        else:
            st_sc[...] = jnp.zeros_like(st_sc)

    def run(fwd):
        def head(hd, carry):
            o, st_new = _hgrn_head(q_ref[hd], pre_ref[hd], lb_ref[hd], v_ref[hd], st_sc[hd], fwd)
            o_ref[hd] = o
            st_sc[hd] = st_new
            return carry
        lax.fori_loop(0, NH_B, head, 0, unroll=8)

    @pl.when(d == 0)
    def _():
        run(True)

    @pl.when(d == 1)
    def _():
        run(False)

    @pl.when(c == pl.num_programs(2) - 1)
    def _():
        for hd in range(NH_B):
            s_out[0, 0, hd] = st_sc[hd].T


def _hgrn_scan(q, pre, v, lbd, *, row0, n_seq, seq_len, state=None):
    L = CHUNK_B
    nc = seq_len // L
    blk0 = row0 // L

    def loc_blk(b, d, c):
        return b * nc + c + d * (nc - 1 - 2 * c)

    def tok_blk(b, d, c):
        return blk0 + loc_blk(b, d, c)

    in_specs = [
        pl.BlockSpec((NH_B, L, DK_B), lambda b, d, c: (0, tok_blk(b, d, c), 0)),
        pl.BlockSpec((None, NH_B, L, DK_B), lambda b, d, c: (d, 0, tok_blk(b, d, c), 0)),
        pl.BlockSpec((NH_B, L, DK_B), lambda b, d, c: (0, tok_blk(b, d, c), 0)),
        pl.BlockSpec((None, NH_B, 1, DK_B), lambda b, d, c: (d, 0, 0, 0)),
    ]
    args = [q, pre, v, lbd]
    if state is not None:
        in_specs.append(pl.BlockSpec((1, 1, NH_B, DK_B, DK_B), lambda b, d, c: (b, d, 0, 0, 0)))
        args.append(state)
    return pl.pallas_call(
        functools.partial(_hgrn_scan_kernel, has_state=state is not None),
        name="hgrn_scan_seeded" if state is not None else "hgrn_scan",
        grid=(n_seq, 2, nc),
        in_specs=in_specs,
        out_specs=[
            pl.BlockSpec((None, NH_B, L, DK_B), lambda b, d, c: (d, 0, loc_blk(b, d, c), 0)),
            pl.BlockSpec((1, 1, NH_B, DK_B, DK_B), lambda b, d, c: (b, d, 0, 0, 0)),
        ],
        out_shape=[
            jax.ShapeDtypeStruct((2, NH_B, n_seq * seq_len, DK_B), F32),
            jax.ShapeDtypeStruct((n_seq, 2, NH_B, DK_B, DK_B), F32),
        ],
        scratch_shapes=[pltpu.VMEM((NH_B, DK_B, DK_B), F32)],
        compiler_params=_params(("parallel", "parallel", "arbitrary")),
    )(*args)


def _out_b_kernel(op_ref, os_ref, sg_ref, nw_ref, w_ref, x_ref, mod_ref, lg_ref, lb_ref, out_ref, *, n_prompt_tiles):
    is_prompt = pl.program_id(0) < n_prompt_tiles
    parts = []
    for hd in range(NH_B):
        y = jnp.where(is_prompt, op_ref[0, hd] + op_ref[1, hd], os_ref[0, hd] + os_ref[1, hd])
        parts.append(y * lax.rsqrt(jnp.mean(y * y, axis=-1, keepdims=True) + EPS))
    yn = jnp.concatenate(parts, axis=-1) * nw_ref[...] * sg_ref[...].astype(F32)
    out = jnp.dot(yn.astype(BF16), w_ref[...], preferred_element_type=F32)
    out_ref[...] = _layer_norm_rows(ALPHA * x_ref[...] + mod_ref[2:3, :] * out, lg_ref[...], lb_ref[...])


def _out_b(geo, o_prompt, o_sample, sg, norm_w, w_out, x, mod_l, ln_g, ln_b):
    t = geo.t
    tb = TOK_TILE_L
    n_p = geo.t_prompt // tb
    full = lambda shape: pl.BlockSpec(shape, lambda i: (0,) * len(shape))
    return pl.pallas_call(
        functools.partial(_out_b_kernel, n_prompt_tiles=n_p),
        name="out_b",
        grid=(t // tb,),
        in_specs=[
            pl.BlockSpec((2, NH_B, tb, DK_B), lambda i: (0, 0, jnp.minimum(i, n_p - 1), 0)),
            pl.BlockSpec((2, NH_B, tb, DK_B), lambda i: (0, 0, jnp.maximum(i - n_p, 0), 0)),
            pl.BlockSpec((tb, D), lambda i: (i, 0)),
            full((1, D)), full((D, D)),
            pl.BlockSpec((tb, D), lambda i: (i, 0)),
            pl.BlockSpec((None, 6, D), lambda i: (geo.cond_row(i, tb), 0, 0)),
            full((1, D)), full((1, D)),
        ],
        out_specs=pl.BlockSpec((tb, D), lambda i: (i, 0)),
        out_shape=jax.ShapeDtypeStruct((t, D), F32),
        compiler_params=_params(("parallel",)),
    )(o_prompt, o_sample, sg, norm_w.reshape(1, D).astype(F32), w_out.astype(BF16), x, mod_l,
      ln_g.reshape(1, D), ln_b.reshape(1, D))


MOE_BLK = 1024
U32 = jnp.uint32
ROW_WORDS = D // 2
CHUNK_W = 256
ROW_CHUNKS = ROW_WORDS // CHUNK_W
SC_WINDOW = 128


def _pack_rows(x):
    hi = pltpu.bitcast(x[:, :ROW_WORDS].astype(BF16).astype(F32), U32)
    lo = pltpu.bitcast(x[:, ROW_WORDS:].astype(BF16).astype(F32), U32)
    return hi | (lo >> 16)


def _unpack_rows(words):
    hi = pltpu.bitcast(words & jnp.uint32(0xFFFF0000), F32)
    lo = pltpu.bitcast(words << 16, F32)
    return jnp.concatenate([hi, lo], axis=1)


def _store_chunks(chunk_ref, x):
    words = _pack_rows(x)
    for c in range(ROW_CHUNKS):
        chunk_ref(c)[...] = words[:, c * CHUNK_W:(c + 1) * CHUNK_W]


def _load_chunks(chunk_ref):
    return _unpack_rows(jnp.concatenate([chunk_ref(c)[...] for c in range(ROW_CHUNKS)], axis=1))


def _first_index(hit, iota, size, axis):
    return jnp.min(jnp.where(hit, iota, size), axis=axis, keepdims=True)


def _router_kernel(x_ref, mod_ref, wrt_ref, eb_ref, e_ref, w_ref, r_ref, cnt_ref, h_ref, cnt_sc):
    i = pl.program_id(0)
    tb = x_ref.shape[0]

    @pl.when(i == 0)
    def _():
        cnt_sc[...] = jnp.zeros_like(cnt_sc)

    h = x_ref[...] * (1.0 + mod_ref[4:5, :]) + mod_ref[3:4, :]
    _store_chunks(lambda c: h_ref.at[c], h)
    logits = lax.dot_general(wrt_ref[...], h, NT_DIMS, precision=HIGHEST, preferred_element_type=F32)
    scores = jax.nn.sigmoid(logits)
    sel = scores + eb_ref[...]

    g3 = sel.reshape(N_GROUPS, GROUP_SIZE, tb)
    io3 = lax.broadcasted_iota(jnp.int32, g3.shape, 1)
    m1 = jnp.max(g3, axis=1, keepdims=True)
    first = _first_index(g3 == m1, io3, GROUP_SIZE, 1)
    m2 = jnp.max(jnp.where(io3 == first, -jnp.inf, g3), axis=1, keepdims=True)
    gscore = (m1 + m2).reshape(N_GROUPS, tb)

    iog = lax.broadcasted_iota(jnp.int32, gscore.shape, 0)
    gmask = jnp.zeros(gscore.shape, F32)
    for _ in range(TOPK_GROUPS):
        gm = jnp.max(gscore, axis=0, keepdims=True)
        pick = iog == _first_index(gscore == gm, iog, N_GROUPS, 0)
        gmask = jnp.where(pick, 1.0, gmask)
        gscore = jnp.where(pick, -jnp.inf, gscore)
    emask = jnp.broadcast_to(gmask.reshape(N_GROUPS, 1, tb), (N_GROUPS, GROUP_SIZE, tb)).reshape(N_EXPERTS, tb)
    cand = jnp.where(emask > 0.0, sel, -jnp.inf)

    ioe = lax.broadcasted_iota(jnp.int32, cand.shape, 0)
    picks, wts = [], []
    onehot = jnp.zeros(cand.shape, F32)
    for _ in range(TOP_K):
        cm = jnp.max(cand, axis=0, keepdims=True)
        idx = _first_index(cand == cm, ioe, N_EXPERTS, 0)
        pick = ioe == idx
        picks.append(pick)
        wts.append(jnp.sum(jnp.where(pick, scores, 0.0), axis=0, keepdims=True))
        onehot = onehot + pick.astype(F32)
        cand = jnp.where(pick, -jnp.inf, cand)
        e_ref[pl.ds(len(picks) - 1, 1), :] = idx
    wsum = wts[0]
    for wk in wts[1:]:
        wsum = wsum + wk
    for k in range(TOP_K):
        w_ref[pl.ds(k, 1), :] = wts[k] / wsum * ROUTED_SCALE

    r_io = lax.broadcasted_iota(jnp.int32, (tb, tb), 0)
    c_io = lax.broadcasted_iota(jnp.int32, (tb, tb), 1)
    before = (r_io < c_io).astype(BF16)
    rank = cnt_sc[:, 0:1] + jnp.dot(onehot.astype(BF16), before, preferred_element_type=F32)
    for k in range(TOP_K):
        r_ref[pl.ds(k, 1), :] = jnp.sum(jnp.where(picks[k], rank, 0.0), axis=0, keepdims=True).astype(jnp.int32)
    cnt_sc[...] = cnt_sc[...] + jnp.sum(onehot, axis=1, keepdims=True)
    cnt_ref[...] = cnt_sc[...]


def _router(geo, x, mod_l, w_router, e_bias):
    t = geo.t
    tb = TOK_TILE_L
    full = lambda shape: pl.BlockSpec(shape, lambda i: (0,) * len(shape))
    e, w, r, cnt, h = pl.pallas_call(
        _router_kernel,
        name="router",
        grid=(t // tb,),
        in_specs=[
            pl.BlockSpec((tb, D), lambda i: (i, 0)),
            pl.BlockSpec((None, 6, D), lambda i: (geo.cond_row(i, tb), 0, 0)),
            full((N_EXPERTS, D)), full((N_EXPERTS, 1)),
        ],
        out_specs=[
            pl.BlockSpec((TOP_K, tb), lambda i: (0, i)),
            pl.BlockSpec((TOP_K, tb), lambda i: (0, i)),
            pl.BlockSpec((TOP_K, tb), lambda i: (0, i)),
            full((N_EXPERTS, LANES)),
            pl.BlockSpec((ROW_CHUNKS, tb, CHUNK_W), lambda i: (0, i, 0)),
        ],
        out_shape=[
            jax.ShapeDtypeStruct((TOP_K, t), jnp.int32),
            jax.ShapeDtypeStruct((TOP_K, t), F32),
            jax.ShapeDtypeStruct((TOP_K, t), jnp.int32),
            jax.ShapeDtypeStruct((N_EXPERTS, LANES), F32),
            jax.ShapeDtypeStruct((ROW_CHUNKS, t, CHUNK_W), U32),
        ],
        scratch_shapes=[pltpu.VMEM((N_EXPERTS, LANES), F32)],
        compiler_params=_params(("arbitrary",)),
    )(x, mod_l, w_router.T.astype(F32), e_bias.reshape(N_EXPERTS, 1).astype(F32))
    return e, w, r, cnt[:, 0].astype(jnp.int32), h


def _slot_kernel(pstart_ref, e_ref, r_ref, o_ref):
    e = e_ref[...]
    slot = r_ref[...]
    for x in range(N_EXPERTS):
        slot = slot + jnp.where(e == x, pstart_ref[x], 0)
    o_ref[...] = slot


def _slots(geo, pstart, top_e, rank):
    tb = SLOT_TILE
    return pl.pallas_call(
        _slot_kernel,
        name="slots",
        grid_spec=pltpu.PrefetchScalarGridSpec(
            num_scalar_prefetch=1,
            grid=(geo.t // tb,),
            in_specs=[pl.BlockSpec((TOP_K, tb), lambda i, p: (0, i)),
                      pl.BlockSpec((TOP_K, tb), lambda i, p: (0, i))],
            out_specs=pl.BlockSpec((TOP_K, tb), lambda i, p: (0, i)),
        ),
        out_shape=jax.ShapeDtypeStruct((TOP_K, geo.t), jnp.int32),
        compiler_params=_params(("parallel",)),
    )(pstart, top_e, rank)


def _block_meta_kernel(pstart_ref, counts_ref, pend_ref, e_ref, v_ref):
    row0 = lax.broadcasted_iota(jnp.int32, e_ref.shape, 1) * MOE_BLK
    blk_e = jnp.zeros(e_ref.shape, jnp.int32)
    for x in range(N_EXPERTS):
        blk_e = blk_e + jnp.where(pend_ref[x] <= row0, 1, 0)
    blk_e = jnp.minimum(blk_e, N_EXPERTS - 1)
    last = jnp.zeros(e_ref.shape, jnp.int32)
    for x in range(N_EXPERTS):
        last = last + jnp.where(blk_e == x, pstart_ref[x] + counts_ref[x], 0)
    e_ref[...] = blk_e
    v_ref[...] = jnp.clip(last - row0, 0, MOE_BLK)


def _block_meta(pstart, counts, pend, n_blocks):
    e, v = pl.pallas_call(
        _block_meta_kernel,
        name="block_meta",
        grid_spec=pltpu.PrefetchScalarGridSpec(
            num_scalar_prefetch=3,
            grid=(1,),
            in_specs=[],
            out_specs=[pl.BlockSpec((1, n_blocks), lambda i, a, b, c: (0, 0)),
                       pl.BlockSpec((1, n_blocks), lambda i, a, b, c: (0, 0))],
        ),
        out_shape=[jax.ShapeDtypeStruct((1, n_blocks), jnp.int32), jax.ShapeDtypeStruct((1, n_blocks), jnp.int32)],
        compiler_params=_params(("arbitrary",)),
    )(pstart, counts, pend)
    return e[0], v[0]


def _sc_mesh():
    return plsc.VectorSubcoreMesh(core_axis_name="core", subcore_axis_name="subcore")


def _sc_scatter(rows, idx, n_out, src_block):
    n_idx = idx.shape[0]

    @pl.kernel(out_type=jax.ShapeDtypeStruct((n_out, CHUNK_W), rows.dtype), mesh=_sc_mesh(), scratch_types=[],
               name="sc_dispatch")
    def scatter(x_hbm, i_hbm, o_hbm):
        def body(x_vmem, i_vmem):
            pltpu.sync_copy(x_vmem, o_hbm.at[i_vmem.at[0]])

        pltpu.emit_pipeline(
            body,
            grid=(n_idx // SC_WINDOW,),
            in_specs=[pl.BlockSpec((SC_WINDOW, CHUNK_W), index_map=lambda w: (src_block(w), 0)),
                      pl.BlockSpec((1, SC_WINDOW), index_map=lambda w: (0, w))],
            out_specs=[],
            core_axis_name=("core", "subcore"),
            dimension_semantics=(pltpu.PARALLEL,),
        )(x_hbm, i_hbm)

    return scatter(rows, idx.reshape(1, n_idx))


def _sc_gather(table, idx):
    n_idx = idx.shape[0]

    @pl.kernel(out_type=jax.ShapeDtypeStruct((n_idx, CHUNK_W), table.dtype), mesh=_sc_mesh(),
               name="sc_combine_gather")
    def gather(t_hbm, i_hbm, o_hbm):
        def body(i_vmem, o_vmem):
            pltpu.sync_copy(t_hbm.at[i_vmem.at[0]], o_vmem)

        pltpu.emit_pipeline(
            body,
            grid=(n_idx // SC_WINDOW,),
            in_specs=[pl.BlockSpec((1, SC_WINDOW), index_map=lambda w: (0, w))],
            out_specs=[pl.BlockSpec((SC_WINDOW, CHUNK_W), index_map=lambda w: (w, 0))],
            core_axis_name=("core", "subcore"),
            dimension_semantics=(pltpu.PARALLEL,),
        )(i_hbm, o_hbm)

    return gather(table, idx.reshape(1, n_idx))


def _ffn_kernel(blk_e_ref, blk_valid_ref, n_used_ref, xs_ref, wg_ref, wu_ref, wd_ref, y_ref, wg_sc, wu_sc, wd_sc):
    b = pl.program_id(0)
    used = b < n_used_ref[0]
    new_expert = (b == 0) | (blk_e_ref[b] != blk_e_ref[jnp.maximum(b - 1, 0)])

    @pl.when(used & new_expert)
    def _():
        wg_sc[...] = wg_ref[...].astype(BF16)
        wu_sc[...] = wu_ref[...].astype(BF16)
        wd_sc[...] = wd_ref[...].astype(BF16)

    @pl.when(used)
    def _():
        x = _load_chunks(lambda c: xs_ref.at[c])
        row = lax.broadcasted_iota(jnp.int32, (MOE_BLK, 1), 0)
        x = jnp.where(row < blk_valid_ref[b], x, 0.0).astype(BF16)
        g = jnp.dot(x, wg_sc[...], preferred_element_type=F32)
        u = jnp.dot(x, wu_sc[...], preferred_element_type=F32)
        hmid = (g * jax.nn.sigmoid(g) * u).astype(BF16)
        _store_chunks(lambda c: y_ref.at[c], jnp.dot(hmid, wd_sc[...], preferred_element_type=F32))

    @pl.when(jnp.logical_not(used))
    def _():
        y_ref[...] = jnp.zeros_like(y_ref)


def _ffn(xs, blk_e, blk_valid, n_used, layer, wg, wu, wd, n_blocks):
    def blk(b, be, bv, nu):
        return jnp.maximum(jnp.minimum(b, nu[0] - 1), 0)

    def w_idx(b, be, bv, nu):
        return (layer, be[blk(b, be, bv, nu)], 0, 0)

    return pl.pallas_call(
        _ffn_kernel,
        name="expert_ffn",
        grid_spec=pltpu.PrefetchScalarGridSpec(
            num_scalar_prefetch=3,
            grid=(n_blocks,),
            in_specs=[
                pl.BlockSpec((ROW_CHUNKS, MOE_BLK, CHUNK_W), lambda b, be, bv, nu: (0, blk(b, be, bv, nu), 0)),
                pl.BlockSpec((None, None, D, D_EXPERT), w_idx),
                pl.BlockSpec((None, None, D, D_EXPERT), w_idx),
                pl.BlockSpec((None, None, D_EXPERT, D), w_idx),
            ],
            out_specs=pl.BlockSpec((ROW_CHUNKS, MOE_BLK, CHUNK_W), lambda b, be, bv, nu: (0, b, 0)),
            scratch_shapes=[pltpu.VMEM((D, D_EXPERT), BF16), pltpu.VMEM((D, D_EXPERT), BF16),
                            pltpu.VMEM((D_EXPERT, D), BF16)],
        ),
        out_shape=jax.ShapeDtypeStruct(xs.shape, U32),
        compiler_params=_params(("arbitrary",)),
    )(blk_e, blk_valid, n_used, xs, wg, wu, wd)


def _combine_kernel(x_ref, mod_ref, wt_ref, y_ref, sg_ref, su_ref, sd_ref, lg_ref, lb_ref, *o_refs, n_prompt_tiles):
    x = x_ref[...]
    hb = (x * (1.0 + mod_ref[4:5, :]) + mod_ref[3:4, :]).astype(BF16)
    g = jnp.dot(hb, sg_ref[...], preferred_element_type=F32)
    u = jnp.dot(hb, su_ref[...], preferred_element_type=F32)
    ff = jnp.dot((g * jax.nn.sigmoid(g) * u).astype(BF16), sd_ref[...], preferred_element_type=F32)
    for k in range(TOP_K):
        ff = ff + _load_chunks(lambda c: y_ref.at[c, k]) * wt_ref[:, k:k + 1]
    out = _layer_norm_rows(ALPHA * x + mod_ref[5:6, :] * ff, lg_ref[...], lb_ref[...])
    if len(o_refs) == 1:
        o_refs[0][...] = out
    else:
        is_prompt = pl.program_id(0) < n_prompt_tiles

        @pl.when(is_prompt)
        def _():
            o_refs[0][...] = out

        @pl.when(jnp.logical_not(is_prompt))
        def _():
            o_refs[1][...] = out


def _combine(geo, x, mod_l, wt, ytok, sg, su, sd, ln_g, ln_b, split=False):
    tb = TOK_TILE_L
    n_p = geo.t_prompt // tb
    full = lambda shape: pl.BlockSpec(shape, lambda i: (0,) * len(shape))
    if split:
        out_specs = [pl.BlockSpec((tb, D), lambda i: (jnp.minimum(i, n_p - 1), 0)),
                     pl.BlockSpec((tb, D), lambda i: (jnp.maximum(i - n_p, 0), 0))]
        out_shape = [jax.ShapeDtypeStruct((geo.t_prompt, D), F32), jax.ShapeDtypeStruct((geo.t_sample, D), F32)]
    else:
        out_specs = pl.BlockSpec((tb, D), lambda i: (i, 0))
        out_shape = jax.ShapeDtypeStruct((geo.t, D), F32)
    return pl.pallas_call(
        functools.partial(_combine_kernel, n_prompt_tiles=n_p),
        name="combine",
        grid=(geo.t // tb,),
        in_specs=[
            pl.BlockSpec((tb, D), lambda i: (i, 0)),
            pl.BlockSpec((None, 6, D), lambda i: (geo.cond_row(i, tb), 0, 0)),
            pl.BlockSpec((tb, TOP_K), lambda i: (i, 0)),
            pl.BlockSpec((ROW_CHUNKS, TOP_K, tb, CHUNK_W), lambda i: (0, 0, i, 0)),
            full((D, D_EXPERT)), full((D, D_EXPERT)), full((D_EXPERT, D)), full((1, D)), full((1, D)),
        ],
        out_specs=out_specs,
        out_shape=out_shape,
        compiler_params=_params(("arbitrary",)),
    )(x, mod_l, wt, ytok, sg.astype(BF16), su.astype(BF16), sd.astype(BF16),
      ln_g.reshape(1, D), ln_b.reshape(1, D))


def _moe_layer(geo, x, mod_l, w_router, e_bias, layer, wg, wu, wd, sg, su, sd, ln_g, ln_b, split=False):
    t = geo.t
    top_e, w, rank, counts, h = _router(geo, x, mod_l, w_router, e_bias)
    n_blocks = (t * TOP_K) // MOE_BLK + N_EXPERTS
    n_rows = n_blocks * MOE_BLK
    padded = (counts + MOE_BLK - 1) // MOE_BLK * MOE_BLK
    pend = jnp.cumsum(padded)
    pstart = (pend - padded).astype(jnp.int32)
    blk_e, blk_valid = _block_meta(pstart, counts, pend.astype(jnp.int32), n_blocks)
    n_used = (pend[-1:] // MOE_BLK).astype(jnp.int32)
    slots = _slots(geo, pstart, top_e, rank)
    idx = (slots.reshape(1, TOP_K * t) + (jnp.arange(ROW_CHUNKS, dtype=jnp.int32) * n_rows)[:, None]).reshape(-1)
    win_per_chunk = TOP_K * t // SC_WINDOW
    tok_windows = t // SC_WINDOW

    def src_block(wdw):
        return (wdw // win_per_chunk) * tok_windows + (wdw % win_per_chunk) % tok_windows

    xs = _sc_scatter(h.reshape(ROW_CHUNKS * t, CHUNK_W), idx, ROW_CHUNKS * n_rows, src_block)
    yb = _ffn(xs.reshape(ROW_CHUNKS, n_rows, CHUNK_W), blk_e, blk_valid, n_used, layer, wg, wu, wd, n_blocks)
    ytok = _sc_gather(yb.reshape(ROW_CHUNKS * n_rows, CHUNK_W), idx)
    return _combine(geo, x, mod_l, w.T, ytok.reshape(ROW_CHUNKS, TOP_K, t, CHUNK_W), sg, su, sd, ln_g, ln_b,
                    split=split)


def _pos_embed(rows):
    quarter = D // 4
    omega = 1.0 / (POS_BASE ** (jnp.arange(quarter, dtype=F32) / quarter))
    r, col = jnp.meshgrid(jnp.arange(rows, dtype=F32), jnp.arange(GRID_W, dtype=F32), indexing='ij')
    r = r.reshape(-1, 1) * omega
    col = col.reshape(-1, 1) * omega
    return jnp.concatenate([jnp.sin(r), jnp.cos(r), jnp.sin(col), jnp.cos(col)], axis=-1)


def _mlstm_layer(geo, x, mod_l, j, a_w_in, a_b_gates, a_norm, a_w_out, ln_g, ln_b,
                 state_C, state_n, state_m):
    q, kt, v, so, gr = _proj_a(geo, x, mod_l, a_w_in[j], a_b_gates[j])
    hp, c_p, n_p, m_p = _mlstm_scan(q, kt, v, gr, row0=0, n_seq=geo.n_prompt, seq_len=geo.prompt_len)
    ns = geo.n_sample
    n0 = jnp.pad(state_n[:, j].astype(F32)[..., None], ((0, 0),) * 4 + ((0, LANES - 1),))
    m0 = jnp.pad(state_m[:, j].astype(F32), ((0, 0), (0, 0), (0, SUBLANES - NH_A)))
    m0 = jnp.broadcast_to(m0[..., None], (ns, 2, SUBLANES, LANES))
    hs, _, _, _ = _mlstm_scan(q, kt, v, gr, row0=geo.t_prompt, n_seq=ns, seq_len=geo.sample_len,
                              state=(state_C[:, j].astype(F32), n0, m0))
    x1 = _out_a(geo, hp, hs, so, a_norm[j], a_w_out[j], x, mod_l, ln_g, ln_b)
    return x1, c_p, n_p[..., 0], m_p[:, :, :NH_A, 0]


def _hgrn_layer(geo, x, mod_l, j, lb_layer, b_w_in, b_norm, b_w_out, ln_g, ln_b, state_S):
    q, pre, v, sg = _proj_b(geo, x, mod_l, b_w_in[j])
    lbd = lb_layer.reshape(2, NH_B, 1, DK_B)
    op, s_p = _hgrn_scan(q, pre, v, lbd, row0=0, n_seq=geo.n_prompt, seq_len=geo.prompt_len)
    os_, _ = _hgrn_scan(q, pre, v, lbd, row0=geo.t_prompt, n_seq=geo.n_sample, seq_len=geo.sample_len,
                        state=state_S[:, j].astype(F32))
    x1 = _out_b(geo, op, os_, sg, b_norm[j], b_w_out[j], x, mod_l, ln_g, ln_b)
    return x1, s_p


def kernel(x_prompt, x_sample, state_mlstm_C, state_mlstm_n, state_mlstm_m, state_hgrn_S, c, c_ctx, w_mod, b_mod, ln_g, ln_b, a_w_in, a_b_gates, a_norm, a_w_out, b_w_in, b_lb, b_norm, b_w_out, w_router, e_bias, w_gate, w_up, w_down, ws_gate, ws_up, ws_down):
    bp, sp, _ = x_prompt.shape
    bs, ss, _ = x_sample.shape
    geo = Geometry(bp, sp, bs, ss)
    cond = jnp.zeros((COND_ROWS, D), F32).at[0].set(c_ctx).at[1:1 + bs].set(c)
    mod = _modulation(cond, w_mod, b_mod)
    x = (x_prompt.reshape(-1, D), x_sample.reshape(-1, D), _pos_embed(ss // GRID_W))
    x1, new_c, new_n, new_m = _mlstm_layer(geo, x, mod[0], 0, a_w_in, a_b_gates, a_norm, a_w_out,
                                           ln_g[0, 0], ln_b[0, 0], state_mlstm_C, state_mlstm_n, state_mlstm_m)
    x2 = _moe_layer(geo, x1, mod[0], w_router[0], e_bias[0], 0, w_gate, w_up, w_down, ws_gate[0], ws_up[0], ws_down[0],
                    ln_g[0, 1], ln_b[0, 1])
    sm = jax.nn.softmax(b_lb.astype(F32), axis=0)
    lb_all = jnp.cumsum(sm, axis=0) - sm[0]
    x3, new_s = _hgrn_layer(geo, x2, mod[1], 0, lb_all[1], b_w_in, b_norm, b_w_out, ln_g[1, 0], ln_b[1, 0],
                            state_hgrn_S)
    y_p, y_s = _moe_layer(geo, x3, mod[1], w_router[1], e_bias[1], 1, w_gate, w_up, w_down, ws_gate[1], ws_up[1],
                          ws_down[1], ln_g[1, 1], ln_b[1, 1], split=True)
    y_prompt = y_p.reshape(bp, sp, D)
    y_sample = y_s.reshape(bs, ss, D)
    return y_prompt, y_sample, new_c[:, None], new_n[:, None], new_m[:, None], new_s[:, None]
```

```python
import functools

import jax
import jax.numpy as jnp
from jax import lax
from jax.experimental import pallas as pl
from jax.experimental.pallas import tpu as pltpu
from jax.experimental.pallas import tpu_sc as plsc

F32 = jnp.float32
BF16 = jnp.bfloat16
HIGHEST = lax.Precision.HIGHEST

D = 1024
DEPTH = 2
GRID_W = 64
POS_BASE = 10000.0
EPS = 1e-6
ALPHA = (2.0 * DEPTH) ** 0.25
NH_A, DK_A, DV_A = 4, 128, 256
QK_A, V_A = NH_A * DK_A, NH_A * DV_A
NH_B, DK_B = 8, 128
N_EXPERTS, TOP_K, N_GROUPS, TOPK_GROUPS = 64, 8, 8, 4
GROUP_SIZE = N_EXPERTS // N_GROUPS
D_EXPERT = D // 4
ROUTED_SCALE = 2.5

LANES = 128
SUBLANES = 8
COND_ROWS = 8
TOK_TILE = 256
TOK_TILE_L = 512
SLOT_TILE = 2048
CHUNK_A = 256
VMEM_LIMIT = 56 * 1024 * 1024

NT_DIMS = (((1,), (1,)), ((), ()))


def _params(sem):
    return pltpu.CompilerParams(dimension_semantics=sem, vmem_limit_bytes=VMEM_LIMIT)


def _split3(x):
    hi = x.astype(BF16)
    r = x - hi.astype(F32)
    mid = r.astype(BF16)
    lo = (r - mid.astype(F32)).astype(BF16)
    return hi, mid, lo


def _dot3(a_bf, x, transpose_side=None):
    hi, mid, lo = _split3(x)
    return (jnp.dot(a_bf, hi, preferred_element_type=F32)
            + jnp.dot(a_bf, mid, preferred_element_type=F32)
            + jnp.dot(a_bf, lo, preferred_element_type=F32))


def _dot3_r(x, a_bf):
    hi, mid, lo = _split3(x)
    return (jnp.dot(hi, a_bf, preferred_element_type=F32)
            + jnp.dot(mid, a_bf, preferred_element_type=F32)
            + jnp.dot(lo, a_bf, preferred_element_type=F32))


def _log_sigmoid(x):
    return jnp.minimum(x, 0.0) - jnp.log1p(jnp.exp(-jnp.abs(x)))


def _layer_norm_rows(x, g, b):
    mu = jnp.mean(x, axis=-1, keepdims=True)
    xc = x - mu
    var = jnp.mean(xc * xc, axis=-1, keepdims=True)
    return xc * lax.rsqrt(var + EPS) * g + b


class Geometry:
    def __init__(self, n_prompt, prompt_len, n_sample, sample_len):
        self.n_prompt, self.prompt_len = n_prompt, prompt_len
        self.n_sample, self.sample_len = n_sample, sample_len
        self.t_prompt = n_prompt * prompt_len
        self.t_sample = n_sample * sample_len
        self.t = self.t_prompt + self.t_sample
        assert self.t_prompt % TOK_TILE_L == 0 and sample_len % TOK_TILE_L == 0 and self.t % SLOT_TILE == 0
        assert n_sample + 1 <= COND_ROWS

    def cond_row(self, tile, tile_rows):
        n_p = self.t_prompt // tile_rows
        return jnp.where(tile < n_p, 0, 1 + (tile - n_p) // (self.sample_len // tile_rows))


def _mod_kernel(cond_ref, w_ref, b_ref, o_ref):
    c = cond_ref[...]
    s = c * jax.nn.sigmoid(c)
    o_ref[0, 0] = jnp.dot(s, w_ref[0], precision=HIGHEST, preferred_element_type=F32) + b_ref[0, 0]


def _modulation(cond, w_mod, b_mod):
    out = pl.pallas_call(
        _mod_kernel,
        name="modulation",
        grid=(DEPTH, 6),
        in_specs=[
            pl.BlockSpec((COND_ROWS, D), lambda l, j: (0, 0)),
            pl.BlockSpec((1, D, D), lambda l, j: (l, 0, j)),
            pl.BlockSpec((1, 1, 1, D), lambda l, j: (l, j, 0, 0)),
        ],
        out_specs=pl.BlockSpec((1, 1, COND_ROWS, D), lambda l, j: (l, j, 0, 0)),
        out_shape=jax.ShapeDtypeStruct((DEPTH, 6, COND_ROWS, D), F32),
        compiler_params=_params(("arbitrary", "arbitrary")),
    )(cond, w_mod, b_mod.reshape(DEPTH, 6, 1, D))
    return out.transpose(0, 2, 1, 3)


def _embed_specs(geo, tb):
    n_p = geo.t_prompt // tb
    per_seq = geo.sample_len // tb
    return [pl.BlockSpec((tb, D), lambda i: (jnp.minimum(i, n_p - 1), 0)),
            pl.BlockSpec((tb, D), lambda i: (jnp.maximum(i - n_p, 0), 0)),
            pl.BlockSpec((tb, D), lambda i: (jnp.maximum(i - n_p, 0) % per_seq, 0))]


def _embed_tile(xp_ref, xs_ref, pos_ref, n_prompt_tiles):
    return jnp.where(pl.program_id(0) < n_prompt_tiles, xp_ref[...], xs_ref[...] + pos_ref[...])


def _proj_a_kernel(xp_ref, xs_ref, pos_ref, mod_ref, wq_ref, wkt_ref, wvo_ref, wgt_ref, bgt_ref,
                   q_ref, kt_ref, v_ref, so_ref, gr_ref, *, n_prompt_tiles):
    x = _embed_tile(xp_ref, xs_ref, pos_ref, n_prompt_tiles)
    h = x * (1.0 + mod_ref[1:2, :]) + mod_ref[0:1, :]
    hb = h.astype(BF16)
    q_ref[...] = jnp.dot(hb, wq_ref[...], preferred_element_type=F32).astype(BF16)
    kt = lax.dot_general(wkt_ref[...], hb, NT_DIMS, preferred_element_type=F32)
    kt_ref[...] = (kt * (DK_A ** -0.5)).astype(BF16)
    vo = jnp.dot(hb, wvo_ref[...], preferred_element_type=F32)
    v_ref[...] = vo[:, :V_A].astype(BF16)
    so_ref[...] = jax.nn.sigmoid(vo[:, V_A:]).astype(BF16)
    gr_ref[...] = lax.dot_general(wgt_ref[...], h, NT_DIMS, precision=HIGHEST,
                                  preferred_element_type=F32) + bgt_ref[...]


def _proj_a(geo, x, mod_l, w_in, b_gates):
    t = geo.t
    n_gate = 4 * NH_A
    wq = w_in[:, :QK_A].astype(BF16)
    wkt = w_in[:, QK_A:2 * QK_A].T.astype(BF16)
    wvo = w_in[:, 2 * QK_A:2 * QK_A + 2 * V_A].astype(BF16)
    wg = w_in[:, 2 * QK_A + 2 * V_A:]
    bg = b_gates.reshape(n_gate).astype(F32)
    tb = TOK_TILE_L
    full = lambda shape: pl.BlockSpec(shape, lambda i: (0,) * len(shape))
    return pl.pallas_call(
        functools.partial(_proj_a_kernel, n_prompt_tiles=geo.t_prompt // tb),
        name="proj_a",
        grid=(t // tb,),
        in_specs=_embed_specs(geo, tb) + [
            pl.BlockSpec((None, 6, D), lambda i: (geo.cond_row(i, tb), 0, 0)),
            full((D, QK_A)), full((QK_A, D)), full((D, 2 * V_A)), full((n_gate, D)), full((n_gate, 1)),
        ],
        out_specs=[
            pl.BlockSpec((tb, QK_A), lambda i: (i, 0)),
            pl.BlockSpec((QK_A, tb), lambda i: (0, i)),
            pl.BlockSpec((tb, V_A), lambda i: (i, 0)),
            pl.BlockSpec((tb, V_A), lambda i: (i, 0)),
            pl.BlockSpec((n_gate, tb), lambda i: (0, i)),
        ],
        out_shape=[
            jax.ShapeDtypeStruct((t, QK_A), BF16),
            jax.ShapeDtypeStruct((QK_A, t), BF16),
            jax.ShapeDtypeStruct((t, V_A), BF16),
            jax.ShapeDtypeStruct((t, V_A), BF16),
            jax.ShapeDtypeStruct((n_gate, t), F32),
        ],
        compiler_params=_params(("parallel",)),
    )(*x, mod_l, wq, wkt, wvo, wg.T, bg.reshape(n_gate, 1))


def _mlstm_scan_kernel(*refs, chunk, has_state):
    if has_state:
        (q_ref, kt_ref, v_ref, gr_ref, c0_ref, n0_ref, m0_ref,
         h_ref, c_out, n_out, m_out, c_sc, n_sc, m_sc) = refs
    else:
        (q_ref, kt_ref, v_ref, gr_ref,
         h_ref, c_out, n_out, m_out, c_sc, n_sc, m_sc) = refs
    L = chunk
    d = pl.program_id(1)
    c = pl.program_id(2)
    fwd = d == 0

    @pl.when(c == 0)
    def _():
        if has_state:
            c_sc[...] = c0_ref[0, 0]
            n_sc[...] = n0_ref[0, 0]
            m_sc[...] = m0_ref[0, 0]
        else:
            c_sc[...] = jnp.zeros_like(c_sc)
            n_sc[...] = jnp.zeros_like(n_sc)
            m_sc[...] = jnp.zeros_like(m_sc)

    row = lax.broadcasted_iota(jnp.int32, (L, L), 0)
    col = lax.broadcasted_iota(jnp.int32, (L, L), 1)
    sgn = 1 - 2 * d
    causal = (row - col) * sgn >= 0
    tri_t = ((col - row) * sgn >= 0).astype(BF16)

    gr = gr_ref[...]
    br_all = _dot3_r(_log_sigmoid(gr), tri_t)
    bc_all = jnp.concatenate([br_all, jnp.zeros((LANES - br_all.shape[0], L), F32)], axis=0).T
    ones_blk = (lax.broadcasted_iota(jnp.int32, (L, LANES), 1) == 0).astype(BF16)

    for h in range(NH_A):
        b_c = jnp.where(fwd, bc_all[:, 4 + h:5 + h], bc_all[:, 12 + h:13 + h])
        b_r = jnp.where(fwd, br_all[4 + h:5 + h, :], br_all[12 + h:13 + h, :])
        i_r = jnp.where(fwd, gr[h:h + 1, :], gr[8 + h:9 + h, :])
        bl = jnp.where(fwd, b_r[:, L - 1:L], b_r[:, 0:1])
        q = q_ref[:, h * DK_A:(h + 1) * DK_A]
        kt = kt_ref[h * DK_A:(h + 1) * DK_A, :]
        v = v_ref[:, h * DV_A:(h + 1) * DV_A]
        m = m_sc[h:h + 1, 0:1]
        cst = c_sc[h]
        nst = n_sc[h]

        a_r = i_r - b_r
        logd = jnp.where(causal, b_c + a_r, -jnp.inf)
        inter = b_c + m
        m_t = jnp.maximum(inter, jnp.max(logd, axis=1, keepdims=True))
        dmat = jnp.exp(logd - m_t)
        e_int = jnp.exp(inter - m_t)
        s = (jnp.dot(q, kt, preferred_element_type=F32) * dmat).astype(BF16)
        num = (jnp.dot(s, v, preferred_element_type=F32)
               + e_int * jnp.dot(q, cst.astype(BF16), preferred_element_type=F32))
        den = (jnp.dot(s, ones_blk, preferred_element_type=F32)
               + e_int * jnp.dot(q, nst.astype(BF16), preferred_element_type=F32))[:, 0:1]
        h_ref[:, h * DV_A:(h + 1) * DV_A] = num / jnp.maximum(jnp.abs(den), jnp.exp(-m_t))

        logw = bl + a_r
        m_new = jnp.maximum(bl + m, jnp.max(logw, axis=1, keepdims=True))
        w = jnp.exp(logw - m_new)
        decay = jnp.exp(bl + m - m_new)
        kw = (kt.astype(F32) * w).astype(BF16)
        c_sc[h] = decay * cst + jnp.dot(kw, v, preferred_element_type=F32)
        n_sc[h] = decay * nst + jnp.dot(kw, ones_blk, preferred_element_type=F32)
        m_sc[h:h + 1, :] = jnp.broadcast_to(m_new, (1, LANES))

    @pl.when(c == pl.num_programs(2) - 1)
    def _():
        c_out[0, 0] = c_sc[...]
        n_out[0, 0] = n_sc[...]
        m_out[0, 0] = m_sc[...]


def _mlstm_scan(q, kt, v, gr, *, row0, n_seq, seq_len, state=None):
    L = CHUNK_A
    nc = seq_len // L
    blk0 = row0 // L

    def loc_blk(b, d, c):
        return b * nc + c + d * (nc - 1 - 2 * c)

    def tok_blk(b, d, c):
        return blk0 + loc_blk(b, d, c)

    in_specs = [
        pl.BlockSpec((L, QK_A), lambda b, d, c: (tok_blk(b, d, c), 0)),
        pl.BlockSpec((QK_A, L), lambda b, d, c: (0, tok_blk(b, d, c))),
        pl.BlockSpec((L, V_A), lambda b, d, c: (tok_blk(b, d, c), 0)),
        pl.BlockSpec((4 * NH_A, L), lambda b, d, c: (0, tok_blk(b, d, c))),
    ]
    args = [q, kt, v, gr]
    if state is not None:
        in_specs += [
            pl.BlockSpec((1, 1, NH_A, DK_A, DV_A), lambda b, d, c: (b, d, 0, 0, 0)),
            pl.BlockSpec((1, 1, NH_A, DK_A, LANES), lambda b, d, c: (b, d, 0, 0, 0)),
            pl.BlockSpec((1, 1, SUBLANES, LANES), lambda b, d, c: (b, d, 0, 0)),
        ]
        args += list(state)
    return pl.pallas_call(
        functools.partial(_mlstm_scan_kernel, chunk=L, has_state=state is not None),
        name="mlstm_scan_seeded" if state is not None else "mlstm_scan",
        grid=(n_seq, 2, nc),
        in_specs=in_specs,
        out_specs=[
            pl.BlockSpec((None, L, V_A), lambda b, d, c: (d, loc_blk(b, d, c), 0)),
            pl.BlockSpec((1, 1, NH_A, DK_A, DV_A), lambda b, d, c: (b, d, 0, 0, 0)),
            pl.BlockSpec((1, 1, NH_A, DK_A, LANES), lambda b, d, c: (b, d, 0, 0, 0)),
            pl.BlockSpec((1, 1, SUBLANES, LANES), lambda b, d, c: (b, d, 0, 0)),
        ],
        out_shape=[
            jax.ShapeDtypeStruct((2, n_seq * seq_len, V_A), F32),
            jax.ShapeDtypeStruct((n_seq, 2, NH_A, DK_A, DV_A), F32),
            jax.ShapeDtypeStruct((n_seq, 2, NH_A, DK_A, LANES), F32),
            jax.ShapeDtypeStruct((n_seq, 2, SUBLANES, LANES), F32),
        ],
        scratch_shapes=[
            pltpu.VMEM((NH_A, DK_A, DV_A), F32),
            pltpu.VMEM((NH_A, DK_A, LANES), F32),
            pltpu.VMEM((SUBLANES, LANES), F32),
        ],
        compiler_params=_params(("parallel", "parallel", "arbitrary")),
    )(*args)


def _out_a_kernel(hp_ref, hs_ref, so_ref, nw_ref, w_ref, xp_ref, xs_ref, pos_ref, mod_ref, lg_ref, lb_ref, o_ref, *,
                  n_prompt_tiles):
    is_prompt = pl.program_id(0) < n_prompt_tiles
    x = _embed_tile(xp_ref, xs_ref, pos_ref, n_prompt_tiles)
    y = jnp.where(is_prompt, hp_ref[0] + hp_ref[1], hs_ref[0] + hs_ref[1])
    parts = []
    for h in range(NH_A):
        yh = y[:, h * DV_A:(h + 1) * DV_A]
        mu = jnp.mean(yh, axis=-1, keepdims=True)
        yc = yh - mu
        var = jnp.mean(yc * yc, axis=-1, keepdims=True)
        parts.append(yc * lax.rsqrt(var + EPS))
    yn = jnp.concatenate(parts, axis=-1) * nw_ref[...] * so_ref[...].astype(F32)
    out = jnp.dot(yn.astype(BF16), w_ref[...], preferred_element_type=F32)
    o_ref[...] = _layer_norm_rows(ALPHA * x + mod_ref[2:3, :] * out, lg_ref[...], lb_ref[...])


def _out_a(geo, h_prompt, h_sample, so, norm_w, w_out, x, mod_l, ln_g, ln_b):
    t = geo.t
    tb = TOK_TILE_L
    n_p = geo.t_prompt // tb
    full = lambda shape: pl.BlockSpec(shape, lambda i: (0,) * len(shape))
    return pl.pallas_call(
        functools.partial(_out_a_kernel, n_prompt_tiles=n_p),
        name="out_a",
        grid=(t // tb,),
        in_specs=[
            pl.BlockSpec((2, tb, V_A), lambda i: (0, jnp.minimum(i, n_p - 1), 0)),
            pl.BlockSpec((2, tb, V_A), lambda i: (0, jnp.maximum(i - n_p, 0), 0)),
            pl.BlockSpec((tb, V_A), lambda i: (i, 0)),
            full((1, V_A)), full((V_A, D)),
        ] + _embed_specs(geo, tb) + [
            pl.BlockSpec((None, 6, D), lambda i: (geo.cond_row(i, tb), 0, 0)),
            full((1, D)), full((1, D)),
        ],
        out_specs=pl.BlockSpec((tb, D), lambda i: (i, 0)),
        out_shape=jax.ShapeDtypeStruct((t, D), F32),
        compiler_params=_params(("parallel",)),
    )(h_prompt, h_sample, so, norm_w.reshape(1, V_A).astype(F32), w_out.astype(BF16), *x, mod_l,
      ln_g.reshape(1, D), ln_b.reshape(1, D))


def _proj_b_kernel(x_ref, mod_ref, w_ref, q_ref, pre_ref, v_ref, sg_ref):
    h = x_ref[...] * (1.0 + mod_ref[1:2, :]) + mod_ref[0:1, :]
    z = jnp.dot(h.astype(BF16), w_ref[...], preferred_element_type=F32)
    for hd in range(NH_B):
        lo = hd * DK_B
        qh = z[:, lo:lo + DK_B]
        q_ref[hd] = qh * jax.nn.sigmoid(qh)
        pre_ref[0, hd] = z[:, D + lo:D + lo + DK_B]
        pre_ref[1, hd] = z[:, 2 * D + lo:2 * D + lo + DK_B]
        v_ref[hd] = z[:, 3 * D + lo:3 * D + lo + DK_B].astype(BF16)
    g = z[:, 4 * D:]
    sg_ref[...] = (g * jax.nn.sigmoid(g)).astype(BF16)


def _proj_b(geo, x, mod_l, w_in):
    t = geo.t
    tb = TOK_TILE
    return pl.pallas_call(
        _proj_b_kernel,
        name="proj_b",
        grid=(t // tb,),
        in_specs=[
            pl.BlockSpec((tb, D), lambda i: (i, 0)),
            pl.BlockSpec((None, 6, D), lambda i: (geo.cond_row(i, tb), 0, 0)),
            pl.BlockSpec((D, 5 * D), lambda i: (0, 0)),
        ],
        out_specs=[
            pl.BlockSpec((NH_B, tb, DK_B), lambda i: (0, i, 0)),
            pl.BlockSpec((2, NH_B, tb, DK_B), lambda i: (0, 0, i, 0)),
            pl.BlockSpec((NH_B, tb, DK_B), lambda i: (0, i, 0)),
            pl.BlockSpec((tb, D), lambda i: (i, 0)),
        ],
        out_shape=[
            jax.ShapeDtypeStruct((NH_B, t, DK_B), F32),
            jax.ShapeDtypeStruct((2, NH_B, t, DK_B), F32),
            jax.ShapeDtypeStruct((NH_B, t, DK_B), BF16),
            jax.ShapeDtypeStruct((t, D), BF16),
        ],
        compiler_params=_params(("parallel",)),
    )(x, mod_l, w_in.astype(BF16))


CHUNK_B = 128
BAND = SUBLANES // 2
TN_DIMS = (((0,), (0,)), ((), ()))


def _hgrn_head(q, pre, lbv, v_bf, st, fwd):
    L = q.shape[0]
    sg = jax.nn.sigmoid(pre)
    f = lbv + (1.0 - lbv) * sg
    lf = jnp.log(f)
    kk = (1.0 - lbv) * (1.0 - sg)
    row = lax.broadcasted_iota(jnp.int32, (L, L), 0)
    col = lax.broadcasted_iota(jnp.int32, (L, L), 1)
    tri = ((row >= col) if fwd else (row <= col)).astype(BF16)
    b = _dot3(tri, lf)
    tpos = lax.broadcasted_iota(jnp.int32, (L, DK_B), 0)
    blk_bits = row ^ col
    lag = jnp.where(blk_bits < BAND, (row - col) if fwd else (col - row), -1)

    step = 1 if fwd else L - 1
    att = jnp.where(lag == 0, jnp.sum(q * kk, axis=1, keepdims=True), 0.0)
    f_r, kk_r, g = f, kk, f
    for dl in range(1, BAND):
        if dl > 1:
            f_r = pltpu.roll(f_r, step, 0)
            g = g * f_r
        kk_r = pltpu.roll(kk_r, step, 0)
        att = jnp.where(lag == dl, jnp.sum(q * kk_r * g, axis=1, keepdims=True), att)

    w = BAND
    while w < L:
        nb = L // (2 * w)
        b3 = b.reshape(nb, 2 * w, DK_B)
        edge = (b3[:, w - 1:w, :] if fwd else b3[:, w:w + 1, :])
        bmid = jnp.broadcast_to(edge, (nb, 2 * w, DK_B)).reshape(L, DK_B)
        second = (tpos & w) != 0
        t_side = second if fwd else jnp.logical_not(second)
        e = jnp.exp(jnp.where(t_side, b - bmid, bmid - b))
        qt = jnp.where(t_side, q * e, 0.0).astype(BF16)
        ks = jnp.where(t_side, 0.0, kk * e).astype(BF16)
        a = lax.dot_general(qt, ks, NT_DIMS, preferred_element_type=F32)
        att = att + jnp.where(blk_bits < 2 * w, a, 0.0)
        w *= 2
    o = jnp.dot(att.astype(BF16), v_bf, preferred_element_type=F32)

    bl = b[L - 1:L, :] if fwd else b[0:1, :]
    o = o + lax.dot_general((q * jnp.exp(b)).astype(BF16), st.astype(BF16), NT_DIMS, preferred_element_type=F32)
    kd = (kk * jnp.exp(bl - b)).astype(BF16)
    st_new = jnp.exp(bl) * st + lax.dot_general(v_bf, kd, TN_DIMS, preferred_element_type=F32)
    return o, st_new


def _hgrn_scan_kernel(*refs, has_state):
    if has_state:
        q_ref, pre_ref, v_ref, lb_ref, s0_ref, o_ref, s_out, st_sc = refs
    else:
        q_ref, pre_ref, v_ref, lb_ref, o_ref, s_out, st_sc = refs
    d = pl.program_id(1)
    c = pl.program_id(2)

    @pl.when(c == 0)
    def _():
        if has_state:
            for hd in range(NH_B):
                st_sc[hd] = s0_ref[0, 0, hd].T
        else:
            st_sc[...] = jnp.zeros_like(st_sc)

    def run(fwd):
        def head(hd, carry):
            o, st_new = _hgrn_head(q_ref[hd], pre_ref[hd], lb_ref[hd], v_ref[hd], st_sc[hd], fwd)
            o_ref[hd] = o
            st_sc[hd] = st_new
            return carry
        lax.fori_loop(0, NH_B, head, 0, unroll=8)

    @pl.when(d == 0)
    def _():
        run(True)

    @pl.when(d == 1)
    def _():
        run(False)

    @pl.when(c == pl.num_programs(2) - 1)
    def _():
        for hd in range(NH_B):
            s_out[0, 0, hd] = st_sc[hd].T


def _hgrn_scan(q, pre, v, lbd, *, row0, n_seq, seq_len, state=None):
    L = CHUNK_B
    nc = seq_len // L
    blk0 = row0 // L

    def loc_blk(b, d, c):
        return b * nc + c + d * (nc - 1 - 2 * c)

    def tok_blk(b, d, c):
        return blk0 + loc_blk(b, d, c)

    in_specs = [
        pl.BlockSpec((NH_B, L, DK_B), lambda b, d, c: (0, tok_blk(b, d, c), 0)),
        pl.BlockSpec((None, NH_B, L, DK_B), lambda b, d, c: (d, 0, tok_blk(b, d, c), 0)),
        pl.BlockSpec((NH_B, L, DK_B), lambda b, d, c: (0, tok_blk(b, d, c), 0)),
        pl.BlockSpec((None, NH_B, 1, DK_B), lambda b, d, c: (d, 0, 0, 0)),
    ]
    args = [q, pre, v, lbd]
    if state is not None:
        in_specs.append(pl.BlockSpec((1, 1, NH_B, DK_B, DK_B), lambda b, d, c: (b, d, 0, 0, 0)))
        args.append(state)
    return pl.pallas_call(
        functools.partial(_hgrn_scan_kernel, has_state=state is not None),
        name="hgrn_scan_seeded" if state is not None else "hgrn_scan",
        grid=(n_seq, 2, nc),
        in_specs=in_specs,
        out_specs=[
            pl.BlockSpec((None, NH_B, L, DK_B), lambda b, d, c: (d, 0, loc_blk(b, d, c), 0)),
            pl.BlockSpec((1, 1, NH_B, DK_B, DK_B), lambda b, d, c: (b, d, 0, 0, 0)),
        ],
        out_shape=[
            jax.ShapeDtypeStruct((2, NH_B, n_seq * seq_len, DK_B), F32),
            jax.ShapeDtypeStruct((n_seq, 2, NH_B, DK_B, DK_B), F32),
        ],
        scratch_shapes=[pltpu.VMEM((NH_B, DK_B, DK_B), F32)],
        compiler_params=_params(("parallel", "parallel", "arbitrary")),
    )(*args)


def _out_b_kernel(op_ref, os_ref, sg_ref, nw_ref, w_ref, x_ref, mod_ref, lg_ref, lb_ref, out_ref, *, n_prompt_tiles):
    is_prompt = pl.program_id(0) < n_prompt_tiles
    parts = []
    for hd in range(NH_B):
        y = jnp.where(is_prompt, op_ref[0, hd] + op_ref[1, hd], os_ref[0, hd] + os_ref[1, hd])
        parts.append(y * lax.rsqrt(jnp.mean(y * y, axis=-1, keepdims=True) + EPS))
    yn = jnp.concatenate(parts, axis=-1) * nw_ref[...] * sg_ref[...].astype(F32)
    out = jnp.dot(yn.astype(BF16), w_ref[...], preferred_element_type=F32)
    out_ref[...] = _layer_norm_rows(ALPHA * x_ref[...] + mod_ref[2:3, :] * out, lg_ref[...], lb_ref[...])


def _out_b(geo, o_prompt, o_sample, sg, norm_w, w_out, x, mod_l, ln_g, ln_b):
    t = geo.t
    tb = TOK_TILE_L
    n_p = geo.t_prompt // tb
    full = lambda shape: pl.BlockSpec(shape, lambda i: (0,) * len(shape))
    return pl.pallas_call(
        functools.partial(_out_b_kernel, n_prompt_tiles=n_p),
        name="out_b",
        grid=(t // tb,),
        in_specs=[
            pl.BlockSpec((2, NH_B, tb, DK_B), lambda i: (0, 0, jnp.minimum(i, n_p - 1), 0)),
            pl.BlockSpec((2, NH_B, tb, DK_B), lambda i: (0, 0, jnp.maximum(i - n_p, 0), 0)),
            pl.BlockSpec((tb, D), lambda i: (i, 0)),
            full((1, D)), full((D, D)),
            pl.BlockSpec((tb, D), lambda i: (i, 0)),
            pl.BlockSpec((None, 6, D), lambda i: (geo.cond_row(i, tb), 0, 0)),
            full((1, D)), full((1, D)),
        ],
        out_specs=pl.BlockSpec((tb, D), lambda i: (i, 0)),
        out_shape=jax.ShapeDtypeStruct((t, D), F32),
        compiler_params=_params(("parallel",)),
    )(o_prompt, o_sample, sg, norm_w.reshape(1, D).astype(F32), w_out.astype(BF16), x, mod_l,
      ln_g.reshape(1, D), ln_b.reshape(1, D))


MOE_BLK = 1024
FFN_ROWS = MOE_BLK
U32 = jnp.uint32
ROW_WORDS = D // 2
CHUNK_W = 256
ROW_CHUNKS = ROW_WORDS // CHUNK_W
SC_WINDOW = 128


def _pack_rows(x):
    hi = pltpu.bitcast(x[:, :ROW_WORDS].astype(BF16).astype(F32), U32)
    lo = pltpu.bitcast(x[:, ROW_WORDS:].astype(BF16).astype(F32), U32)
    return hi | (lo >> 16)


def _unpack_rows(words):
    hi = pltpu.bitcast(words & jnp.uint32(0xFFFF0000), F32)
    lo = pltpu.bitcast(words << 16, F32)
    return jnp.concatenate([hi, lo], axis=1)


def _store_chunks(chunk_ref, x):
    words = _pack_rows(x)
    for c in range(ROW_CHUNKS):
        chunk_ref(c)[...] = words[:, c * CHUNK_W:(c + 1) * CHUNK_W]


def _load_chunks(chunk_ref):
    return _unpack_rows(jnp.concatenate([chunk_ref(c)[...] for c in range(ROW_CHUNKS)], axis=1))


def _first_index(hit, iota, size, axis):
    return jnp.min(jnp.where(hit, iota, size), axis=axis, keepdims=True)


def _router_kernel(x_ref, mod_ref, wrt_ref, eb_ref, e_ref, w_ref, r_ref, cnt_ref, h_ref, cnt_sc):
    i = pl.program_id(0)
    tb = x_ref.shape[0]

    @pl.when(i == 0)
    def _():
        cnt_sc[...] = jnp.zeros_like(cnt_sc)

    h = x_ref[...] * (1.0 + mod_ref[4:5, :]) + mod_ref[3:4, :]
    _store_chunks(lambda c: h_ref.at[c], h)
    logits = lax.dot_general(wrt_ref[...], h, NT_DIMS, precision=HIGHEST, preferred_element_type=F32)
    scores = jax.nn.sigmoid(logits)
    sel = scores + eb_ref[...]

    g3 = sel.reshape(N_GROUPS, GROUP_SIZE, tb)
    io3 = lax.broadcasted_iota(jnp.int32, g3.shape, 1)
    m1 = jnp.max(g3, axis=1, keepdims=True)
    first = _first_index(g3 == m1, io3, GROUP_SIZE, 1)
    m2 = jnp.max(jnp.where(io3 == first, -jnp.inf, g3), axis=1, keepdims=True)
    gscore = (m1 + m2).reshape(N_GROUPS, tb)

    iog = lax.broadcasted_iota(jnp.int32, gscore.shape, 0)
    gmask = jnp.zeros(gscore.shape, F32)
    for _ in range(TOPK_GROUPS):
        gm = jnp.max(gscore, axis=0, keepdims=True)
        pick = iog == _first_index(gscore == gm, iog, N_GROUPS, 0)
        gmask = jnp.where(pick, 1.0, gmask)
        gscore = jnp.where(pick, -jnp.inf, gscore)
    emask = jnp.broadcast_to(gmask.reshape(N_GROUPS, 1, tb), (N_GROUPS, GROUP_SIZE, tb)).reshape(N_EXPERTS, tb)
    cand = jnp.where(emask > 0.0, sel, -jnp.inf)

    ioe = lax.broadcasted_iota(jnp.int32, cand.shape, 0)
    picks, wts = [], []
    onehot = jnp.zeros(cand.shape, F32)
    for _ in range(TOP_K):
        cm = jnp.max(cand, axis=0, keepdims=True)
        idx = _first_index(cand == cm, ioe, N_EXPERTS, 0)
        pick = ioe == idx
        picks.append(pick)
        wts.append(jnp.sum(jnp.where(pick, scores, 0.0), axis=0, keepdims=True))
        onehot = onehot + pick.astype(F32)
        cand = jnp.where(pick, -jnp.inf, cand)
        e_ref[pl.ds(len(picks) - 1, 1), :] = idx
    wsum = wts[0]
    for wk in wts[1:]:
        wsum = wsum + wk
    for k in range(TOP_K):
        w_ref[pl.ds(k, 1), :] = wts[k] / wsum * ROUTED_SCALE

    r_io = lax.broadcasted_iota(jnp.int32, (tb, tb), 0)
    c_io = lax.broadcasted_iota(jnp.int32, (tb, tb), 1)
    before = (r_io < c_io).astype(BF16)
    rank = cnt_sc[:, 0:1] + jnp.dot(onehot.astype(BF16), before, preferred_element_type=F32)
    for k in range(TOP_K):
        r_ref[pl.ds(k, 1), :] = jnp.sum(jnp.where(picks[k], rank, 0.0), axis=0, keepdims=True).astype(jnp.int32)
    cnt_sc[...] = cnt_sc[...] + jnp.sum(onehot, axis=1, keepdims=True)
    cnt_ref[...] = cnt_sc[...]


def _router(geo, x, mod_l, w_router, e_bias):
    t = geo.t
    tb = TOK_TILE_L
    full = lambda shape: pl.BlockSpec(shape, lambda i: (0,) * len(shape))
    e, w, r, cnt, h = pl.pallas_call(
        _router_kernel,
        name="router",
        grid=(t // tb,),
        in_specs=[
            pl.BlockSpec((tb, D), lambda i: (i, 0)),
            pl.BlockSpec((None, 6, D), lambda i: (geo.cond_row(i, tb), 0, 0)),
            full((N_EXPERTS, D)), full((N_EXPERTS, 1)),
        ],
        out_specs=[
            pl.BlockSpec((TOP_K, tb), lambda i: (0, i)),
            pl.BlockSpec((TOP_K, tb), lambda i: (0, i)),
            pl.BlockSpec((TOP_K, tb), lambda i: (0, i)),
            full((N_EXPERTS, LANES)),
            pl.BlockSpec((ROW_CHUNKS, tb, CHUNK_W), lambda i: (0, i, 0)),
        ],
        out_shape=[
            jax.ShapeDtypeStruct((TOP_K, t), jnp.int32),
            jax.ShapeDtypeStruct((TOP_K, t), F32),
            jax.ShapeDtypeStruct((TOP_K, t), jnp.int32),
            jax.ShapeDtypeStruct((N_EXPERTS, LANES), F32),
            jax.ShapeDtypeStruct((ROW_CHUNKS, t, CHUNK_W), U32),
        ],
        scratch_shapes=[pltpu.VMEM((N_EXPERTS, LANES), F32)],
        compiler_params=_params(("arbitrary",)),
    )(x, mod_l, w_router.T.astype(F32), e_bias.reshape(N_EXPERTS, 1).astype(F32))
    return e, w, r, cnt[:, 0].astype(jnp.int32), h


def _slot_kernel(pstart_ref, e_ref, r_ref, o_ref):
    e = e_ref[...]
    slot = r_ref[...]
    for x in range(N_EXPERTS):
        slot = slot + jnp.where(e == x, pstart_ref[x], 0)
    o_ref[...] = slot


def _slots(geo, pstart, top_e, rank):
    tb = SLOT_TILE
    return pl.pallas_call(
        _slot_kernel,
        name="slots",
        grid_spec=pltpu.PrefetchScalarGridSpec(
            num_scalar_prefetch=1,
            grid=(geo.t // tb,),
            in_specs=[pl.BlockSpec((TOP_K, tb), lambda i, p: (0, i)),
                      pl.BlockSpec((TOP_K, tb), lambda i, p: (0, i))],
            out_specs=pl.BlockSpec((TOP_K, tb), lambda i, p: (0, i)),
        ),
        out_shape=jax.ShapeDtypeStruct((TOP_K, geo.t), jnp.int32),
        compiler_params=_params(("parallel",)),
    )(pstart, top_e, rank)


def _block_meta_kernel(pstart_ref, counts_ref, pend_ref, e_ref, v_ref):
    row0 = lax.broadcasted_iota(jnp.int32, e_ref.shape, 1) * MOE_BLK
    blk_e = jnp.zeros(e_ref.shape, jnp.int32)
    for x in range(N_EXPERTS):
        blk_e = blk_e + jnp.where(pend_ref[x] <= row0, 1, 0)
    blk_e = jnp.minimum(blk_e, N_EXPERTS - 1)
    last = jnp.zeros(e_ref.shape, jnp.int32)
    for x in range(N_EXPERTS):
        last = last + jnp.where(blk_e == x, pstart_ref[x] + counts_ref[x], 0)
    e_ref[...] = blk_e
    v_ref[...] = jnp.clip(last - row0, 0, MOE_BLK)


def _block_meta(pstart, counts, pend, n_blocks):
    e, v = pl.pallas_call(
        _block_meta_kernel,
        name="block_meta",
        grid_spec=pltpu.PrefetchScalarGridSpec(
            num_scalar_prefetch=3,
            grid=(1,),
            in_specs=[],
            out_specs=[pl.BlockSpec((1, n_blocks), lambda i, a, b, c: (0, 0)),
                       pl.BlockSpec((1, n_blocks), lambda i, a, b, c: (0, 0))],
        ),
        out_shape=[jax.ShapeDtypeStruct((1, n_blocks), jnp.int32), jax.ShapeDtypeStruct((1, n_blocks), jnp.int32)],
        compiler_params=_params(("arbitrary",)),
    )(pstart, counts, pend)
    return e[0], v[0]


def _sc_mesh():
    return plsc.VectorSubcoreMesh(core_axis_name="core", subcore_axis_name="subcore")


def _sc_scatter(rows, idx, n_out, src_block):
    n_idx = idx.shape[0]

    @pl.kernel(out_type=jax.ShapeDtypeStruct((n_out, CHUNK_W), rows.dtype), mesh=_sc_mesh(), scratch_types=[],
               name="sc_dispatch")
    def scatter(x_hbm, i_hbm, o_hbm):
        def body(x_vmem, i_vmem):
            pltpu.sync_copy(x_vmem, o_hbm.at[i_vmem.at[0]])

        pltpu.emit_pipeline(
            body,
            grid=(n_idx // SC_WINDOW,),
            in_specs=[pl.BlockSpec((SC_WINDOW, CHUNK_W), index_map=lambda w: (src_block(w), 0)),
                      pl.BlockSpec((1, SC_WINDOW), index_map=lambda w: (0, w))],
            out_specs=[],
            core_axis_name=("core", "subcore"),
            dimension_semantics=(pltpu.PARALLEL,),
        )(x_hbm, i_hbm)

    return scatter(rows, idx.reshape(1, n_idx))


def _sc_gather(table, idx):
    n_idx = idx.shape[0]

    @pl.kernel(out_type=jax.ShapeDtypeStruct((n_idx, CHUNK_W), table.dtype), mesh=_sc_mesh(),
               name="sc_combine_gather")
    def gather(t_hbm, i_hbm, o_hbm):
        def body(i_vmem, o_vmem):
            pltpu.sync_copy(t_hbm.at[i_vmem.at[0]], o_vmem)

        pltpu.emit_pipeline(
            body,
            grid=(n_idx // SC_WINDOW,),
            in_specs=[pl.BlockSpec((1, SC_WINDOW), index_map=lambda w: (0, w))],
            out_specs=[pl.BlockSpec((SC_WINDOW, CHUNK_W), index_map=lambda w: (w, 0))],
            core_axis_name=("core", "subcore"),
            dimension_semantics=(pltpu.PARALLEL,),
        )(i_hbm, o_hbm)

    return gather(table, idx.reshape(1, n_idx))


def _ffn_kernel(blk_e_ref, blk_valid_ref, n_used_ref, xs_ref, wg_ref, wu_ref, wd_ref, y_ref, wg_sc, wu_sc, wd_sc):
    b = pl.program_id(0)
    used = b < n_used_ref[0]
    new_expert = (b == 0) | (blk_e_ref[b] != blk_e_ref[jnp.maximum(b - 1, 0)])

    @pl.when(used & new_expert)
    def _():
        wg_sc[...] = wg_ref[...].astype(BF16)
        wu_sc[...] = wu_ref[...].astype(BF16)
        wd_sc[...] = wd_ref[...].astype(BF16)

    @pl.when(used)
    def _():
        for r0 in range(0, MOE_BLK, FFN_ROWS):
            rows = pl.ds(r0, FFN_ROWS)
            x = _load_chunks(lambda c: xs_ref.at[c, rows])
            row = r0 + lax.broadcasted_iota(jnp.int32, (FFN_ROWS, 1), 0)
            x = jnp.where(row < blk_valid_ref[b], x, 0.0).astype(BF16)
            g = jnp.dot(x, wg_sc[...], preferred_element_type=F32)
            u = jnp.dot(x, wu_sc[...], preferred_element_type=F32)
            hmid = (g * jax.nn.sigmoid(g) * u).astype(BF16)
            _store_chunks(lambda c: y_ref.at[c, rows], jnp.dot(hmid, wd_sc[...], preferred_element_type=F32))

    @pl.when(jnp.logical_not(used))
    def _():
        y_ref[...] = jnp.zeros_like(y_ref)


def _ffn(xs, blk_e, blk_valid, n_used, layer, wg, wu, wd, n_blocks):
    def blk(b, be, bv, nu):
        return jnp.maximum(jnp.minimum(b, nu[0] - 1), 0)

    def w_idx(b, be, bv, nu):
        return (layer, be[blk(b, be, bv, nu)], 0, 0)

    return pl.pallas_call(
        _ffn_kernel,
        name="expert_ffn",
        grid_spec=pltpu.PrefetchScalarGridSpec(
            num_scalar_prefetch=3,
            grid=(n_blocks,),
            in_specs=[
                pl.BlockSpec((ROW_CHUNKS, MOE_BLK, CHUNK_W), lambda b, be, bv, nu: (0, blk(b, be, bv, nu), 0)),
                pl.BlockSpec((None, None, D, D_EXPERT), w_idx),
                pl.BlockSpec((None, None, D, D_EXPERT), w_idx),
                pl.BlockSpec((None, None, D_EXPERT, D), w_idx),
            ],
            out_specs=pl.BlockSpec((ROW_CHUNKS, MOE_BLK, CHUNK_W), lambda b, be, bv, nu: (0, b, 0)),
            scratch_shapes=[pltpu.VMEM((D, D_EXPERT), BF16), pltpu.VMEM((D, D_EXPERT), BF16),
                            pltpu.VMEM((D_EXPERT, D), BF16)],
        ),
        out_shape=jax.ShapeDtypeStruct(xs.shape, U32),
        compiler_params=_params(("arbitrary",)),
    )(blk_e, blk_valid, n_used, xs, wg, wu, wd)


def _combine_kernel(x_ref, mod_ref, wt_ref, y_ref, sg_ref, su_ref, sd_ref, lg_ref, lb_ref, *o_refs, n_prompt_tiles):
    x = x_ref[...]
    hb = (x * (1.0 + mod_ref[4:5, :]) + mod_ref[3:4, :]).astype(BF16)
    g = jnp.dot(hb, sg_ref[...], preferred_element_type=F32)
    u = jnp.dot(hb, su_ref[...], preferred_element_type=F32)
    ff = jnp.dot((g * jax.nn.sigmoid(g) * u).astype(BF16), sd_ref[...], preferred_element_type=F32)
    for k in range(TOP_K):
        ff = ff + _load_chunks(lambda c: y_ref.at[c, k]) * wt_ref[:, k:k + 1]
    out = _layer_norm_rows(ALPHA * x + mod_ref[5:6, :] * ff, lg_ref[...], lb_ref[...])
    if len(o_refs) == 1:
        o_refs[0][...] = out
    else:
        is_prompt = pl.program_id(0) < n_prompt_tiles

        @pl.when(is_prompt)
        def _():
            o_refs[0][...] = out

        @pl.when(jnp.logical_not(is_prompt))
        def _():
            o_refs[1][...] = out


def _combine(geo, x, mod_l, wt, ytok, sg, su, sd, ln_g, ln_b, split=False):
    tb = TOK_TILE_L
    n_p = geo.t_prompt // tb
    full = lambda shape: pl.BlockSpec(shape, lambda i: (0,) * len(shape))
    if split:
        out_specs = [pl.BlockSpec((tb, D), lambda i: (jnp.minimum(i, n_p - 1), 0)),
                     pl.BlockSpec((tb, D), lambda i: (jnp.maximum(i - n_p, 0), 0))]
        out_shape = [jax.ShapeDtypeStruct((geo.t_prompt, D), F32), jax.ShapeDtypeStruct((geo.t_sample, D), F32)]
    else:
        out_specs = pl.BlockSpec((tb, D), lambda i: (i, 0))
        out_shape = jax.ShapeDtypeStruct((geo.t, D), F32)
    return pl.pallas_call(
        functools.partial(_combine_kernel, n_prompt_tiles=n_p),
        name="combine",
        grid=(geo.t // tb,),
        in_specs=[
            pl.BlockSpec((tb, D), lambda i: (i, 0)),
            pl.BlockSpec((None, 6, D), lambda i: (geo.cond_row(i, tb), 0, 0)),
            pl.BlockSpec((tb, TOP_K), lambda i: (i, 0)),
            pl.BlockSpec((ROW_CHUNKS, TOP_K, tb, CHUNK_W), lambda i: (0, 0, i, 0)),
            full((D, D_EXPERT)), full((D, D_EXPERT)), full((D_EXPERT, D)), full((1, D)), full((1, D)),
        ],
        out_specs=out_specs,
        out_shape=out_shape,
        compiler_params=_params(("arbitrary",)),
    )(x, mod_l, wt, ytok, sg.astype(BF16), su.astype(BF16), sd.astype(BF16),
      ln_g.reshape(1, D), ln_b.reshape(1, D))


def _moe_layer(geo, x, mod_l, w_router, e_bias, layer, wg, wu, wd, sg, su, sd, ln_g, ln_b, split=False):
    t = geo.t
    top_e, w, rank, counts, h = _router(geo, x, mod_l, w_router, e_bias)
    n_blocks = (t * TOP_K) // MOE_BLK + N_EXPERTS
    n_rows = n_blocks * MOE_BLK
    padded = (counts + MOE_BLK - 1) // MOE_BLK * MOE_BLK
    pend = jnp.cumsum(padded)
    pstart = (pend - padded).astype(jnp.int32)
    blk_e, blk_valid = _block_meta(pstart, counts, pend.astype(jnp.int32), n_blocks)
    n_used = (pend[-1:] // MOE_BLK).astype(jnp.int32)
    slots = _slots(geo, pstart, top_e, rank)
    idx = (slots.reshape(1, TOP_K * t) + (jnp.arange(ROW_CHUNKS, dtype=jnp.int32) * n_rows)[:, None]).reshape(-1)
    win_per_chunk = TOP_K * t // SC_WINDOW
    tok_windows = t // SC_WINDOW

    def src_block(wdw):
        return (wdw // win_per_chunk) * tok_windows + (wdw % win_per_chunk) % tok_windows

    xs = _sc_scatter(h.reshape(ROW_CHUNKS * t, CHUNK_W), idx, ROW_CHUNKS * n_rows, src_block)
    yb = _ffn(xs.reshape(ROW_CHUNKS, n_rows, CHUNK_W), blk_e, blk_valid, n_used, layer, wg, wu, wd, n_blocks)
    ytok = _sc_gather(yb.reshape(ROW_CHUNKS * n_rows, CHUNK_W), idx)
    return _combine(geo, x, mod_l, w.T, ytok.reshape(ROW_CHUNKS, TOP_K, t, CHUNK_W), sg, su, sd, ln_g, ln_b,
                    split=split)


def _pos_embed(rows):
    quarter = D // 4
    omega = 1.0 / (POS_BASE ** (jnp.arange(quarter, dtype=F32) / quarter))
    r, col = jnp.meshgrid(jnp.arange(rows, dtype=F32), jnp.arange(GRID_W, dtype=F32), indexing='ij')
    r = r.reshape(-1, 1) * omega
    col = col.reshape(-1, 1) * omega
    return jnp.concatenate([jnp.sin(r), jnp.cos(r), jnp.sin(col), jnp.cos(col)], axis=-1)


def _mlstm_layer(geo, x, mod_l, j, a_w_in, a_b_gates, a_norm, a_w_out, ln_g, ln_b,
                 state_C, state_n, state_m):
    q, kt, v, so, gr = _proj_a(geo, x, mod_l, a_w_in[j], a_b_gates[j])
    hp, c_p, n_p, m_p = _mlstm_scan(q, kt, v, gr, row0=0, n_seq=geo.n_prompt, seq_len=geo.prompt_len)
    ns = geo.n_sample
    n0 = jnp.pad(state_n[:, j].astype(F32)[..., None], ((0, 0),) * 4 + ((0, LANES - 1),))
    m0 = jnp.pad(state_m[:, j].astype(F32), ((0, 0), (0, 0), (0, SUBLANES - NH_A)))
    m0 = jnp.broadcast_to(m0[..., None], (ns, 2, SUBLANES, LANES))
    hs, _, _, _ = _mlstm_scan(q, kt, v, gr, row0=geo.t_prompt, n_seq=ns, seq_len=geo.sample_len,
                              state=(state_C[:, j].astype(F32), n0, m0))
    x1 = _out_a(geo, hp, hs, so, a_norm[j], a_w_out[j], x, mod_l, ln_g, ln_b)
    return x1, c_p, n_p[..., 0], m_p[:, :, :NH_A, 0]


def _hgrn_layer(geo, x, mod_l, j, lb_layer, b_w_in, b_norm, b_w_out, ln_g, ln_b, state_S):
    q, pre, v, sg = _proj_b(geo, x, mod_l, b_w_in[j])
    lbd = lb_layer.reshape(2, NH_B, 1, DK_B)
    op, s_p = _hgrn_scan(q, pre, v, lbd, row0=0, n_seq=geo.n_prompt, seq_len=geo.prompt_len)
    os_, _ = _hgrn_scan(q, pre, v, lbd, row0=geo.t_prompt, n_seq=geo.n_sample, seq_len=geo.sample_len,
                        state=state_S[:, j].astype(F32))
    x1 = _out_b(geo, op, os_, sg, b_norm[j], b_w_out[j], x, mod_l, ln_g, ln_b)
    return x1, s_p


def kernel(x_prompt, x_sample, state_mlstm_C, state_mlstm_n, state_mlstm_m, state_hgrn_S, c, c_ctx, w_mod, b_mod, ln_g, ln_b, a_w_in, a_b_gates, a_norm, a_w_out, b_w_in, b_lb, b_norm, b_w_out, w_router, e_bias, w_gate, w_up, w_down, ws_gate, ws_up, ws_down):
    bp, sp, _ = x_prompt.shape
    bs, ss, _ = x_sample.shape
    geo = Geometry(bp, sp, bs, ss)
    cond = jnp.zeros((COND_ROWS, D), F32).at[0].set(c_ctx).at[1:1 + bs].set(c)
    mod = _modulation(cond, w_mod, b_mod)
    x = (x_prompt.reshape(-1, D), x_sample.reshape(-1, D), _pos_embed(ss // GRID_W))
    x1, new_c, new_n, new_m = _mlstm_layer(geo, x, mod[0], 0, a_w_in, a_b_gates, a_norm, a_w_out,
                                           ln_g[0, 0], ln_b[0, 0], state_mlstm_C, state_mlstm_n, state_mlstm_m)
    x2 = _moe_layer(geo, x1, mod[0], w_router[0], e_bias[0], 0, w_gate, w_up, w_down, ws_gate[0], ws_up[0], ws_down[0],
                    ln_g[0, 1], ln_b[0, 1])
    sm = jax.nn.softmax(b_lb.astype(F32), axis=0)
    lb_all = jnp.cumsum(sm, axis=0) - sm[0]
    x3, new_s = _hgrn_layer(geo, x2, mod[1], 0, lb_all[1], b_w_in, b_norm, b_w_out, ln_g[1, 0], ln_b[1, 0],
                            state_hgrn_S)
    y_p, y_s = _moe_layer(geo, x3, mod[1], w_router[1], e_bias[1], 1, w_gate, w_up, w_down, ws_gate[1], ws_up[1],
                          ws_down[1], ln_g[1, 1], ln_b[1, 1], split=True)
    y_prompt = y_p.reshape(bp, sp, D)
    y_sample = y_s.reshape(bs, ss, D)
    return y_prompt, y_sample, new_c[:, None], new_n[:, None], new_m[:, None], new_s[:, None]
```

```python
import functools
import math

import jax
import jax.numpy as jnp
from jax import lax
from jax.experimental import pallas as pl
from jax.experimental.pallas import tpu as pltpu
from jax.experimental.pallas import tpu_sc as plsc

F32 = jnp.float32
BF16 = jnp.bfloat16
HIGHEST = lax.Precision.HIGHEST

D = 1024
DEPTH = 2
GRID_W = 64
POS_BASE = 10000.0
EPS = 1e-6
ALPHA = (2.0 * DEPTH) ** 0.25
NH_A, DK_A, DV_A = 4, 128, 256
QK_A, V_A = NH_A * DK_A, NH_A * DV_A
NH_B, DK_B = 8, 128
N_EXPERTS, TOP_K, N_GROUPS, TOPK_GROUPS = 64, 8, 8, 4
GROUP_SIZE = N_EXPERTS // N_GROUPS
D_EXPERT = D // 4
ROUTED_SCALE = 2.5

LANES = 128
SUBLANES = 8
COND_ROWS = 8
TOK_TILE = 256
TOK_TILE_L = 512
SLOT_TILE = 2048
CHUNK_A = 256
VMEM_LIMIT = 56 * 1024 * 1024

NT_DIMS = (((1,), (1,)), ((), ()))


def _params(sem):
    return pltpu.CompilerParams(dimension_semantics=sem, vmem_limit_bytes=VMEM_LIMIT)


def _split3(x):
    hi = x.astype(BF16)
    r = x - hi.astype(F32)
    mid = r.astype(BF16)
    lo = (r - mid.astype(F32)).astype(BF16)
    return hi, mid, lo


def _dot3(a_bf, x, transpose_side=None):
    hi, mid, lo = _split3(x)
    return (jnp.dot(a_bf, hi, preferred_element_type=F32)
            + jnp.dot(a_bf, mid, preferred_element_type=F32)
            + jnp.dot(a_bf, lo, preferred_element_type=F32))


def _dot3_r(x, a_bf):
    hi, mid, lo = _split3(x)
    return (jnp.dot(hi, a_bf, preferred_element_type=F32)
            + jnp.dot(mid, a_bf, preferred_element_type=F32)
            + jnp.dot(lo, a_bf, preferred_element_type=F32))


def _log_sigmoid(x):
    return jnp.minimum(x, 0.0) - jnp.log1p(jnp.exp(-jnp.abs(x)))


def _layer_norm_rows(x, g, b):
    mu = jnp.mean(x, axis=-1, keepdims=True)
    xc = x - mu
    var = jnp.mean(xc * xc, axis=-1, keepdims=True)
    return xc * lax.rsqrt(var + EPS) * g + b


class Geometry:
    def __init__(self, n_prompt, prompt_len, n_sample, sample_len, prompt0=0, sample0=0):
        self.n_prompt, self.prompt_len = n_prompt, prompt_len
        self.n_sample, self.sample_len = n_sample, sample_len
        self.prompt0, self.sample0 = prompt0, sample0
        self.t_prompt = n_prompt * prompt_len
        self.t_sample = n_sample * sample_len
        self.t = self.t_prompt + self.t_sample
        assert self.t_prompt % TOK_TILE_L == 0 and sample_len % TOK_TILE_L == 0
        assert (prompt0 * prompt_len) % TOK_TILE_L == 0
        assert n_sample + 1 <= COND_ROWS

    def cond_row(self, tile, tile_rows):
        n_p = self.t_prompt // tile_rows
        return jnp.where(tile < n_p, 0, 1 + (tile - n_p) // (self.sample_len // tile_rows))


def _mod_kernel(cond_ref, w_ref, b_ref, o_ref):
    c = cond_ref[...]
    s = c * jax.nn.sigmoid(c)
    o_ref[0, 0] = jnp.dot(s, w_ref[0], precision=HIGHEST, preferred_element_type=F32) + b_ref[0, 0]


def _modulation(cond, w_mod, b_mod):
    out = pl.pallas_call(
        _mod_kernel,
        name="modulation",
        grid=(DEPTH, 6),
        in_specs=[
            pl.BlockSpec((COND_ROWS, D), lambda l, j: (0, 0)),
            pl.BlockSpec((1, D, D), lambda l, j: (l, 0, j)),
            pl.BlockSpec((1, 1, 1, D), lambda l, j: (l, j, 0, 0)),
        ],
        out_specs=pl.BlockSpec((1, 1, COND_ROWS, D), lambda l, j: (l, j, 0, 0)),
        out_shape=jax.ShapeDtypeStruct((DEPTH, 6, COND_ROWS, D), F32),
        compiler_params=_params(("arbitrary", "arbitrary")),
    )(cond, w_mod, b_mod.reshape(DEPTH, 6, 1, D))
    return out.transpose(0, 2, 1, 3)


def _embed_specs(geo, tb):
    n_p = geo.t_prompt // tb
    per_seq = geo.sample_len // tb
    p0 = geo.prompt0 * geo.prompt_len // tb
    s0 = geo.sample0 * per_seq
    return [pl.BlockSpec((tb, D), lambda i: (p0 + jnp.minimum(i, n_p - 1), 0)),
            pl.BlockSpec((tb, D), lambda i: (s0 + jnp.maximum(i - n_p, 0), 0)),
            pl.BlockSpec((tb, D), lambda i: (jnp.maximum(i - n_p, 0) % per_seq, 0))]


def _embed_tile(xp_ref, xs_ref, pos_ref, n_prompt_tiles):
    return jnp.where(pl.program_id(0) < n_prompt_tiles, xp_ref[...], xs_ref[...] + pos_ref[...])


def _proj_a_kernel(xp_ref, xs_ref, pos_ref, mod_ref, wq_ref, wkt_ref, wvo_ref, wgt_ref, bgt_ref,
                   q_ref, kt_ref, v_ref, so_ref, gr_ref, *, n_prompt_tiles):
    x = _embed_tile(xp_ref, xs_ref, pos_ref, n_prompt_tiles)
    h = x * (1.0 + mod_ref[1:2, :]) + mod_ref[0:1, :]
    hb = h.astype(BF16)
    q_ref[...] = jnp.dot(hb, wq_ref[...], preferred_element_type=F32).astype(BF16)
    kt = lax.dot_general(wkt_ref[...], hb, NT_DIMS, preferred_element_type=F32)
    kt_ref[...] = (kt * (DK_A ** -0.5)).astype(BF16)
    vo = jnp.dot(hb, wvo_ref[...], preferred_element_type=F32)
    v_ref[...] = vo[:, :V_A].astype(BF16)
    so_ref[...] = jax.nn.sigmoid(vo[:, V_A:]).astype(BF16)
    gr_ref[...] = lax.dot_general(wgt_ref[...], h, NT_DIMS, precision=HIGHEST,
                                  preferred_element_type=F32) + bgt_ref[...]


def _proj_a(geo, x, mod_l, w_in, b_gates):
    t = geo.t
    n_gate = 4 * NH_A
    wq = w_in[:, :QK_A].astype(BF16)
    wkt = w_in[:, QK_A:2 * QK_A].T.astype(BF16)
    wvo = w_in[:, 2 * QK_A:2 * QK_A + 2 * V_A].astype(BF16)
    wg = w_in[:, 2 * QK_A + 2 * V_A:]
    bg = b_gates.reshape(n_gate).astype(F32)
    tb = TOK_TILE_L
    full = lambda shape: pl.BlockSpec(shape, lambda i: (0,) * len(shape))
    return pl.pallas_call(
        functools.partial(_proj_a_kernel, n_prompt_tiles=geo.t_prompt // tb),
        name="proj_a",
        grid=(t // tb,),
        in_specs=_embed_specs(geo, tb) + [
            pl.BlockSpec((None, 6, D), lambda i: (geo.cond_row(i, tb), 0, 0)),
            full((D, QK_A)), full((QK_A, D)), full((D, 2 * V_A)), full((n_gate, D)), full((n_gate, 1)),
        ],
        out_specs=[
            pl.BlockSpec((tb, QK_A), lambda i: (i, 0)),
            pl.BlockSpec((QK_A, tb), lambda i: (0, i)),
            pl.BlockSpec((tb, V_A), lambda i: (i, 0)),
            pl.BlockSpec((tb, V_A), lambda i: (i, 0)),
            pl.BlockSpec((n_gate, tb), lambda i: (0, i)),
        ],
        out_shape=[
            jax.ShapeDtypeStruct((t, QK_A), BF16),
            jax.ShapeDtypeStruct((QK_A, t), BF16),
            jax.ShapeDtypeStruct((t, V_A), BF16),
            jax.ShapeDtypeStruct((t, V_A), BF16),
            jax.ShapeDtypeStruct((n_gate, t), F32),
        ],
        compiler_params=_params(("parallel",)),
    )(*x, mod_l, wq, wkt, wvo, wg.T, bg.reshape(n_gate, 1))


def _mlstm_scan_kernel(*refs, chunk, has_state):
    if has_state:
        (q_ref, kt_ref, v_ref, gr_ref, c0_ref, n0_ref, m0_ref,
         h_ref, c_out, n_out, m_out, c_sc, n_sc, m_sc) = refs
    else:
        (q_ref, kt_ref, v_ref, gr_ref,
         h_ref, c_out, n_out, m_out, c_sc, n_sc, m_sc) = refs
    L = chunk
    d = pl.program_id(1)
    c = pl.program_id(2)
    fwd = d == 0

    @pl.when(c == 0)
    def _():
        if has_state:
            c_sc[...] = c0_ref[0, 0]
            n_sc[...] = n0_ref[0, 0]
            m_sc[...] = m0_ref[0, 0]
        else:
            c_sc[...] = jnp.zeros_like(c_sc)
            n_sc[...] = jnp.zeros_like(n_sc)
            m_sc[...] = jnp.zeros_like(m_sc)

    row = lax.broadcasted_iota(jnp.int32, (L, L), 0)
    col = lax.broadcasted_iota(jnp.int32, (L, L), 1)
    sgn = 1 - 2 * d
    causal = (row - col) * sgn >= 0
    tri_t = ((col - row) * sgn >= 0).astype(BF16)

    gr = gr_ref[...]
    br_all = _dot3_r(_log_sigmoid(gr), tri_t)
    bc_all = jnp.concatenate([br_all, jnp.zeros((LANES - br_all.shape[0], L), F32)], axis=0).T
    ones_blk = (lax.broadcasted_iota(jnp.int32, (L, LANES), 1) == 0).astype(BF16)

    for h in range(NH_A):
        b_c = jnp.where(fwd, bc_all[:, 4 + h:5 + h], bc_all[:, 12 + h:13 + h])
        b_r = jnp.where(fwd, br_all[4 + h:5 + h, :], br_all[12 + h:13 + h, :])
        i_r = jnp.where(fwd, gr[h:h + 1, :], gr[8 + h:9 + h, :])
        bl = jnp.where(fwd, b_r[:, L - 1:L], b_r[:, 0:1])
        q = q_ref[:, h * DK_A:(h + 1) * DK_A]
        kt = kt_ref[h * DK_A:(h + 1) * DK_A, :]
        v = v_ref[:, h * DV_A:(h + 1) * DV_A]
        m = m_sc[h:h + 1, 0:1]
        cst = c_sc[h]
        nst = n_sc[h]

        a_r = i_r - b_r
        logd = jnp.where(causal, b_c + a_r, -jnp.inf)
        inter = b_c + m
        m_t = jnp.maximum(inter, jnp.max(logd, axis=1, keepdims=True))
        dmat = jnp.exp(logd - m_t)
        e_int = jnp.exp(inter - m_t)
        s = (jnp.dot(q, kt, preferred_element_type=F32) * dmat).astype(BF16)
        num = (jnp.dot(s, v, preferred_element_type=F32)
               + e_int * jnp.dot(q, cst.astype(BF16), preferred_element_type=F32))
        den = (jnp.dot(s, ones_blk, preferred_element_type=F32)
               + e_int * jnp.dot(q, nst.astype(BF16), preferred_element_type=F32))[:, 0:1]
        h_ref[:, h * DV_A:(h + 1) * DV_A] = num / jnp.maximum(jnp.abs(den), jnp.exp(-m_t))

        logw = bl + a_r
        m_new = jnp.maximum(bl + m, jnp.max(logw, axis=1, keepdims=True))
        w = jnp.exp(logw - m_new)
        decay = jnp.exp(bl + m - m_new)
        kw = (kt.astype(F32) * w).astype(BF16)
        c_sc[h] = decay * cst + jnp.dot(kw, v, preferred_element_type=F32)
        n_sc[h] = decay * nst + jnp.dot(kw, ones_blk, preferred_element_type=F32)
        m_sc[h:h + 1, :] = jnp.broadcast_to(m_new, (1, LANES))

    @pl.when(c == pl.num_programs(2) - 1)
    def _():
        c_out[0, 0] = c_sc[...]
        n_out[0, 0] = n_sc[...]
        m_out[0, 0] = m_sc[...]


def _mlstm_scan(q, kt, v, gr, *, row0, n_seq, seq_len, state=None):
    L = CHUNK_A
    nc = seq_len // L
    blk0 = row0 // L

    def loc_blk(b, d, c):
        return b * nc + c + d * (nc - 1 - 2 * c)

    def tok_blk(b, d, c):
        return blk0 + loc_blk(b, d, c)

    in_specs = [
        pl.BlockSpec((L, QK_A), lambda b, d, c: (tok_blk(b, d, c), 0)),
        pl.BlockSpec((QK_A, L), lambda b, d, c: (0, tok_blk(b, d, c))),
        pl.BlockSpec((L, V_A), lambda b, d, c: (tok_blk(b, d, c), 0)),
        pl.BlockSpec((4 * NH_A, L), lambda b, d, c: (0, tok_blk(b, d, c))),
    ]
    args = [q, kt, v, gr]
    if state is not None:
        in_specs += [
            pl.BlockSpec((1, 1, NH_A, DK_A, DV_A), lambda b, d, c: (b, d, 0, 0, 0)),
            pl.BlockSpec((1, 1, NH_A, DK_A, LANES), lambda b, d, c: (b, d, 0, 0, 0)),
            pl.BlockSpec((1, 1, SUBLANES, LANES), lambda b, d, c: (b, d, 0, 0)),
        ]
        args += list(state)
    return pl.pallas_call(
        functools.partial(_mlstm_scan_kernel, chunk=L, has_state=state is not None),
        name="mlstm_scan_seeded" if state is not None else "mlstm_scan",
        grid=(n_seq, 2, nc),
        in_specs=in_specs,
        out_specs=[
            pl.BlockSpec((None, L, V_A), lambda b, d, c: (d, loc_blk(b, d, c), 0)),
            pl.BlockSpec((1, 1, NH_A, DK_A, DV_A), lambda b, d, c: (b, d, 0, 0, 0)),
            pl.BlockSpec((1, 1, NH_A, DK_A, LANES), lambda b, d, c: (b, d, 0, 0, 0)),
            pl.BlockSpec((1, 1, SUBLANES, LANES), lambda b, d, c: (b, d, 0, 0)),
        ],
        out_shape=[
            jax.ShapeDtypeStruct((2, n_seq * seq_len, V_A), F32),
            jax.ShapeDtypeStruct((n_seq, 2, NH_A, DK_A, DV_A), F32),
            jax.ShapeDtypeStruct((n_seq, 2, NH_A, DK_A, LANES), F32),
            jax.ShapeDtypeStruct((n_seq, 2, SUBLANES, LANES), F32),
        ],
        scratch_shapes=[
            pltpu.VMEM((NH_A, DK_A, DV_A), F32),
            pltpu.VMEM((NH_A, DK_A, LANES), F32),
            pltpu.VMEM((SUBLANES, LANES), F32),
        ],
        compiler_params=_params(("parallel", "parallel", "arbitrary")),
    )(*args)


def _out_a_kernel(hp_ref, hs_ref, so_ref, nw_ref, w_ref, xp_ref, xs_ref, pos_ref, mod_ref, lg_ref, lb_ref, o_ref, *,
                  n_prompt_tiles):
    is_prompt = pl.program_id(0) < n_prompt_tiles
    x = _embed_tile(xp_ref, xs_ref, pos_ref, n_prompt_tiles)
    y = jnp.where(is_prompt, hp_ref[0] + hp_ref[1], hs_ref[0] + hs_ref[1])
    parts = []
    for h in range(NH_A):
        yh = y[:, h * DV_A:(h + 1) * DV_A]
        mu = jnp.mean(yh, axis=-1, keepdims=True)
        yc = yh - mu
        var = jnp.mean(yc * yc, axis=-1, keepdims=True)
        parts.append(yc * lax.rsqrt(var + EPS))
    yn = jnp.concatenate(parts, axis=-1) * nw_ref[...] * so_ref[...].astype(F32)
    out = jnp.dot(yn.astype(BF16), w_ref[...], preferred_element_type=F32)
    o_ref[...] = _layer_norm_rows(ALPHA * x + mod_ref[2:3, :] * out, lg_ref[...], lb_ref[...])


def _out_a(geo, h_prompt, h_sample, so, norm_w, w_out, x, mod_l, ln_g, ln_b):
    t = geo.t
    tb = TOK_TILE_L
    n_p = geo.t_prompt // tb
    full = lambda shape: pl.BlockSpec(shape, lambda i: (0,) * len(shape))
    return pl.pallas_call(
        functools.partial(_out_a_kernel, n_prompt_tiles=n_p),
        name="out_a",
        grid=(t // tb,),
        in_specs=[
            pl.BlockSpec((2, tb, V_A), lambda i: (0, jnp.minimum(i, n_p - 1), 0)),
            pl.BlockSpec((2, tb, V_A), lambda i: (0, jnp.maximum(i - n_p, 0), 0)),
            pl.BlockSpec((tb, V_A), lambda i: (i, 0)),
            full((1, V_A)), full((V_A, D)),
        ] + _embed_specs(geo, tb) + [
            pl.BlockSpec((None, 6, D), lambda i: (geo.cond_row(i, tb), 0, 0)),
            full((1, D)), full((1, D)),
        ],
        out_specs=pl.BlockSpec((tb, D), lambda i: (i, 0)),
        out_shape=jax.ShapeDtypeStruct((t, D), F32),
        compiler_params=_params(("parallel",)),
    )(h_prompt, h_sample, so, norm_w.reshape(1, V_A).astype(F32), w_out.astype(BF16), *x, mod_l,
      ln_g.reshape(1, D), ln_b.reshape(1, D))


def _proj_b_kernel(x_ref, mod_ref, w_ref, q_ref, pre_ref, v_ref, sg_ref):
    h = x_ref[...] * (1.0 + mod_ref[1:2, :]) + mod_ref[0:1, :]
    z = jnp.dot(h.astype(BF16), w_ref[...], preferred_element_type=F32)
    for hd in range(NH_B):
        lo = hd * DK_B
        qh = z[:, lo:lo + DK_B]
        q_ref[hd] = qh * jax.nn.sigmoid(qh)
        pre_ref[0, hd] = z[:, D + lo:D + lo + DK_B]
        pre_ref[1, hd] = z[:, 2 * D + lo:2 * D + lo + DK_B]
        v_ref[hd] = z[:, 3 * D + lo:3 * D + lo + DK_B].astype(BF16)
    g = z[:, 4 * D:]
    sg_ref[...] = (g * jax.nn.sigmoid(g)).astype(BF16)


def _proj_b(geo, x, mod_l, w_in):
    t = geo.t
    tb = TOK_TILE
    return pl.pallas_call(
        _proj_b_kernel,
        name="proj_b",
        grid=(t // tb,),
        in_specs=[
            pl.BlockSpec((tb, D), lambda i: (i, 0)),
            pl.BlockSpec((None, 6, D), lambda i: (geo.cond_row(i, tb), 0, 0)),
            pl.BlockSpec((D, 5 * D), lambda i: (0, 0)),
        ],
        out_specs=[
            pl.BlockSpec((NH_B, tb, DK_B), lambda i: (0, i, 0)),
            pl.BlockSpec((2, NH_B, tb, DK_B), lambda i: (0, 0, i, 0)),
            pl.BlockSpec((NH_B, tb, DK_B), lambda i: (0, i, 0)),
            pl.BlockSpec((tb, D), lambda i: (i, 0)),
        ],
        out_shape=[
            jax.ShapeDtypeStruct((NH_B, t, DK_B), F32),
            jax.ShapeDtypeStruct((2, NH_B, t, DK_B), F32),
            jax.ShapeDtypeStruct((NH_B, t, DK_B), BF16),
            jax.ShapeDtypeStruct((t, D), BF16),
        ],
        compiler_params=_params(("parallel",)),
    )(x, mod_l, w_in.astype(BF16))


CHUNK_B = 128
BAND = SUBLANES // 2
TN_DIMS = (((0,), (0,)), ((), ()))


def _hgrn_head(q, pre, lbv, v_bf, st, fwd):
    L = q.shape[0]
    sg = jax.nn.sigmoid(pre)
    f = lbv + (1.0 - lbv) * sg
    lf = jnp.log(f)
    kk = (1.0 - lbv) * (1.0 - sg)
    row = lax.broadcasted_iota(jnp.int32, (L, L), 0)
    col = lax.broadcasted_iota(jnp.int32, (L, L), 1)
    tri = ((row >= col) if fwd else (row <= col)).astype(BF16)
    b = _dot3(tri, lf)
    tpos = lax.broadcasted_iota(jnp.int32, (L, DK_B), 0)
    blk_bits = row ^ col
    lag = jnp.where(blk_bits < BAND, (row - col) if fwd else (col - row), -1)

    step = 1 if fwd else L - 1
    att = jnp.where(lag == 0, jnp.sum(q * kk, axis=1, keepdims=True), 0.0)
    f_r, kk_r, g = f, kk, f
    for dl in range(1, BAND):
        if dl > 1:
            f_r = pltpu.roll(f_r, step, 0)
            g = g * f_r
        kk_r = pltpu.roll(kk_r, step, 0)
        att = jnp.where(lag == dl, jnp.sum(q * kk_r * g, axis=1, keepdims=True), att)

    w = BAND
    while w < L:
        nb = L // (2 * w)
        b3 = b.reshape(nb, 2 * w, DK_B)
        edge = (b3[:, w - 1:w, :] if fwd else b3[:, w:w + 1, :])
        bmid = jnp.broadcast_to(edge, (nb, 2 * w, DK_B)).reshape(L, DK_B)
        second = (tpos & w) != 0
        t_side = second if fwd else jnp.logical_not(second)
        e = jnp.exp(jnp.where(t_side, b - bmid, bmid - b))
        qt = jnp.where(t_side, q * e, 0.0).astype(BF16)
        ks = jnp.where(t_side, 0.0, kk * e).astype(BF16)
        a = lax.dot_general(qt, ks, NT_DIMS, preferred_element_type=F32)
        att = att + jnp.where(blk_bits < 2 * w, a, 0.0)
        w *= 2
    o = jnp.dot(att.astype(BF16), v_bf, preferred_element_type=F32)

    bl = b[L - 1:L, :] if fwd else b[0:1, :]
    o = o + lax.dot_general((q * jnp.exp(b)).astype(BF16), st.astype(BF16), NT_DIMS, preferred_element_type=F32)
    kd = (kk * jnp.exp(bl - b)).astype(BF16)
    st_new = jnp.exp(bl) * st + lax.dot_general(v_bf, kd, TN_DIMS, preferred_element_type=F32)
    return o, st_new


def _hgrn_scan_kernel(*refs, has_state):
    if has_state:
        q_ref, pre_ref, v_ref, lb_ref, s0_ref, o_ref, s_out, st_sc = refs
    else:
        q_ref, pre_ref, v_ref, lb_ref, o_ref, s_out, st_sc = refs
    d = pl.program_id(1)
    c = pl.program_id(2)

    @pl.when(c == 0)
    def _():
        if has_state:
            for hd in range(NH_B):
                st_sc[hd] = s0_ref[0, 0, hd].T
        else:
            st_sc[...] = jnp.zeros_like(st_sc)

    def run(fwd):
        def head(hd, carry):
            o, st_new = _hgrn_head(q_ref[hd], pre_ref[hd], lb_ref[hd], v_ref[hd], st_sc[hd], fwd)
            o_ref[hd] = o
            st_sc[hd] = st_new
            return carry
        lax.fori_loop(0, NH_B, head, 0, unroll=8)

    @pl.when(d == 0)
    def _():
        run(True)

    @pl.when(d == 1)
    def _():
        run(False)

    @pl.when(c == pl.num_programs(2) - 1)
    def _():
        for hd in range(NH_B):
            s_out[0, 0, hd] = st_sc[hd].T


def _hgrn_scan(q, pre, v, lbd, *, row0, n_seq, seq_len, state=None):
    L = CHUNK_B
    nc = seq_len // L
    blk0 = row0 // L

    def loc_blk(b, d, c):
        return b * nc + c + d * (nc - 1 - 2 * c)

    def tok_blk(b, d, c):
        return blk0 + loc_blk(b, d, c)

    in_specs = [
        pl.BlockSpec((NH_B, L, DK_B), lambda b, d, c: (0, tok_blk(b, d, c), 0)),
        pl.BlockSpec((None, NH_B, L, DK_B), lambda b, d, c: (d, 0, tok_blk(b, d, c), 0)),
        pl.BlockSpec((NH_B, L, DK_B), lambda b, d, c: (0, tok_blk(b, d, c), 0)),
        pl.BlockSpec((None, NH_B, 1, DK_B), lambda b, d, c: (d, 0, 0, 0)),
    ]
    args = [q, pre, v, lbd]
    if state is not None:
        in_specs.append(pl.BlockSpec((1, 1, NH_B, DK_B, DK_B), lambda b, d, c: (b, d, 0, 0, 0)))
        args.append(state)
    return pl.pallas_call(
        functools.partial(_hgrn_scan_kernel, has_state=state is not None),
        name="hgrn_scan_seeded" if state is not None else "hgrn_scan",
        grid=(n_seq, 2, nc),
        in_specs=in_specs,
        out_specs=[
            pl.BlockSpec((None, NH_B, L, DK_B), lambda b, d, c: (d, 0, loc_blk(b, d, c), 0)),
            pl.BlockSpec((1, 1, NH_B, DK_B, DK_B), lambda b, d, c: (b, d, 0, 0, 0)),
        ],
        out_shape=[
            jax.ShapeDtypeStruct((2, NH_B, n_seq * seq_len, DK_B), F32),
            jax.ShapeDtypeStruct((n_seq, 2, NH_B, DK_B, DK_B), F32),
        ],
        scratch_shapes=[pltpu.VMEM((NH_B, DK_B, DK_B), F32)],
        compiler_params=_params(("parallel", "parallel", "arbitrary")),
    )(*args)


def _out_b_kernel(op_ref, os_ref, sg_ref, nw_ref, w_ref, x_ref, mod_ref, lg_ref, lb_ref, out_ref, *, n_prompt_tiles):
    is_prompt = pl.program_id(0) < n_prompt_tiles
    parts = []
    for hd in range(NH_B):
        y = jnp.where(is_prompt, op_ref[0, hd] + op_ref[1, hd], os_ref[0, hd] + os_ref[1, hd])
        parts.append(y * lax.rsqrt(jnp.mean(y * y, axis=-1, keepdims=True) + EPS))
    yn = jnp.concatenate(parts, axis=-1) * nw_ref[...] * sg_ref[...].astype(F32)
    out = jnp.dot(yn.astype(BF16), w_ref[...], preferred_element_type=F32)
    out_ref[...] = _layer_norm_rows(ALPHA * x_ref[...] + mod_ref[2:3, :] * out, lg_ref[...], lb_ref[...])


def _out_b(geo, o_prompt, o_sample, sg, norm_w, w_out, x, mod_l, ln_g, ln_b):
    t = geo.t
    tb = TOK_TILE_L
    n_p = geo.t_prompt // tb
    full = lambda shape: pl.BlockSpec(shape, lambda i: (0,) * len(shape))
    return pl.pallas_call(
        functools.partial(_out_b_kernel, n_prompt_tiles=n_p),
        name="out_b",
        grid=(t // tb,),
        in_specs=[
            pl.BlockSpec((2, NH_B, tb, DK_B), lambda i: (0, 0, jnp.minimum(i, n_p - 1), 0)),
            pl.BlockSpec((2, NH_B, tb, DK_B), lambda i: (0, 0, jnp.maximum(i - n_p, 0), 0)),
            pl.BlockSpec((tb, D), lambda i: (i, 0)),
            full((1, D)), full((D, D)),
            pl.BlockSpec((tb, D), lambda i: (i, 0)),
            pl.BlockSpec((None, 6, D), lambda i: (geo.cond_row(i, tb), 0, 0)),
            full((1, D)), full((1, D)),
        ],
        out_specs=pl.BlockSpec((tb, D), lambda i: (i, 0)),
        out_shape=jax.ShapeDtypeStruct((t, D), F32),
        compiler_params=_params(("parallel",)),
    )(o_prompt, o_sample, sg, norm_w.reshape(1, D).astype(F32), w_out.astype(BF16), x, mod_l,
      ln_g.reshape(1, D), ln_b.reshape(1, D))


N_STREAMS = 2
MOE_BLK = 512
FFN_ROWS = MOE_BLK
U32 = jnp.uint32
ROW_WORDS = D // 2
CHUNK_W = 256
ROW_CHUNKS = ROW_WORDS // CHUNK_W
SC_WINDOW = 128


def _pack_rows(x):
    hi = pltpu.bitcast(x[:, :ROW_WORDS].astype(BF16).astype(F32), U32)
    lo = pltpu.bitcast(x[:, ROW_WORDS:].astype(BF16).astype(F32), U32)
    return hi | (lo >> 16)


def _unpack_rows(words):
    hi = pltpu.bitcast(words & jnp.uint32(0xFFFF0000), F32)
    lo = pltpu.bitcast(words << 16, F32)
    return jnp.concatenate([hi, lo], axis=1)


def _store_chunks(chunk_ref, x):
    words = _pack_rows(x)
    for c in range(ROW_CHUNKS):
        chunk_ref(c)[...] = words[:, c * CHUNK_W:(c + 1) * CHUNK_W]


def _load_chunks(chunk_ref):
    return _unpack_rows(jnp.concatenate([chunk_ref(c)[...] for c in range(ROW_CHUNKS)], axis=1))


def _first_index(hit, iota, size, axis):
    return jnp.min(jnp.where(hit, iota, size), axis=axis, keepdims=True)


def _router_kernel(x_ref, mod_ref, wrt_ref, eb_ref, e_ref, w_ref, r_ref, cnt_ref, h_ref, cnt_sc):
    i = pl.program_id(0)
    tb = x_ref.shape[0]

    @pl.when(i == 0)
    def _():
        cnt_sc[...] = jnp.zeros_like(cnt_sc)

    h = x_ref[...] * (1.0 + mod_ref[4:5, :]) + mod_ref[3:4, :]
    _store_chunks(lambda c: h_ref.at[c], h)
    logits = lax.dot_general(wrt_ref[...], h, NT_DIMS, precision=HIGHEST, preferred_element_type=F32)
    scores = jax.nn.sigmoid(logits)
    sel = scores + eb_ref[...]

    g3 = sel.reshape(N_GROUPS, GROUP_SIZE, tb)
    io3 = lax.broadcasted_iota(jnp.int32, g3.shape, 1)
    m1 = jnp.max(g3, axis=1, keepdims=True)
    first = _first_index(g3 == m1, io3, GROUP_SIZE, 1)
    m2 = jnp.max(jnp.where(io3 == first, -jnp.inf, g3), axis=1, keepdims=True)
    gscore = (m1 + m2).reshape(N_GROUPS, tb)

    iog = lax.broadcasted_iota(jnp.int32, gscore.shape, 0)
    gmask = jnp.zeros(gscore.shape, F32)
    for _ in range(TOPK_GROUPS):
        gm = jnp.max(gscore, axis=0, keepdims=True)
        pick = iog == _first_index(gscore == gm, iog, N_GROUPS, 0)
        gmask = jnp.where(pick, 1.0, gmask)
        gscore = jnp.where(pick, -jnp.inf, gscore)
    emask = jnp.broadcast_to(gmask.reshape(N_GROUPS, 1, tb), (N_GROUPS, GROUP_SIZE, tb)).reshape(N_EXPERTS, tb)
    cand = jnp.where(emask > 0.0, sel, -jnp.inf)

    ioe = lax.broadcasted_iota(jnp.int32, cand.shape, 0)
    picks, wts = [], []
    onehot = jnp.zeros(cand.shape, F32)
    for _ in range(TOP_K):
        cm = jnp.max(cand, axis=0, keepdims=True)
        idx = _first_index(cand == cm, ioe, N_EXPERTS, 0)
        pick = ioe == idx
        picks.append(pick)
        wts.append(jnp.sum(jnp.where(pick, scores, 0.0), axis=0, keepdims=True))
        onehot = onehot + pick.astype(F32)
        cand = jnp.where(pick, -jnp.inf, cand)
        e_ref[pl.ds(len(picks) - 1, 1), :] = idx
    wsum = wts[0]
    for wk in wts[1:]:
        wsum = wsum + wk
    for k in range(TOP_K):
        w_ref[pl.ds(k, 1), :] = wts[k] / wsum * ROUTED_SCALE

    r_io = lax.broadcasted_iota(jnp.int32, (tb, tb), 0)
    c_io = lax.broadcasted_iota(jnp.int32, (tb, tb), 1)
    before = (r_io < c_io).astype(BF16)
    rank = cnt_sc[:, 0:1] + jnp.dot(onehot.astype(BF16), before, preferred_element_type=F32)
    for k in range(TOP_K):
        r_ref[pl.ds(k, 1), :] = jnp.sum(jnp.where(picks[k], rank, 0.0), axis=0, keepdims=True).astype(jnp.int32)
    cnt_sc[...] = cnt_sc[...] + jnp.sum(onehot, axis=1, keepdims=True)
    cnt_ref[...] = cnt_sc[...]


def _router(geo, x, mod_l, w_router, e_bias):
    t = geo.t
    tb = TOK_TILE_L
    full = lambda shape: pl.BlockSpec(shape, lambda i: (0,) * len(shape))
    e, w, r, cnt, h = pl.pallas_call(
        _router_kernel,
        name="router",
        grid=(t // tb,),
        in_specs=[
            pl.BlockSpec((tb, D), lambda i: (i, 0)),
            pl.BlockSpec((None, 6, D), lambda i: (geo.cond_row(i, tb), 0, 0)),
            full((N_EXPERTS, D)), full((N_EXPERTS, 1)),
        ],
        out_specs=[
            pl.BlockSpec((TOP_K, tb), lambda i: (0, i)),
            pl.BlockSpec((TOP_K, tb), lambda i: (0, i)),
            pl.BlockSpec((TOP_K, tb), lambda i: (0, i)),
            full((N_EXPERTS, LANES)),
            pl.BlockSpec((ROW_CHUNKS, tb, CHUNK_W), lambda i: (0, i, 0)),
        ],
        out_shape=[
            jax.ShapeDtypeStruct((TOP_K, t), jnp.int32),
            jax.ShapeDtypeStruct((TOP_K, t), F32),
            jax.ShapeDtypeStruct((TOP_K, t), jnp.int32),
            jax.ShapeDtypeStruct((N_EXPERTS, LANES), F32),
            jax.ShapeDtypeStruct((ROW_CHUNKS, t, CHUNK_W), U32),
        ],
        scratch_shapes=[pltpu.VMEM((N_EXPERTS, LANES), F32)],
        compiler_params=_params(("arbitrary",)),
    )(x, mod_l, w_router.T.astype(F32), e_bias.reshape(N_EXPERTS, 1).astype(F32))
    return e, w, r, cnt[:, 0].astype(jnp.int32), h


def _slot_kernel(pstart_ref, e_ref, r_ref, o_ref):
    e = e_ref[...]
    slot = r_ref[...]
    for x in range(N_EXPERTS):
        slot = slot + jnp.where(e == x, pstart_ref[x], 0)
    o_ref[...] = slot


def _slots(geo, pstart, top_e, rank):
    tb = math.gcd(SLOT_TILE, geo.t)
    return pl.pallas_call(
        _slot_kernel,
        name="slots",
        grid_spec=pltpu.PrefetchScalarGridSpec(
            num_scalar_prefetch=1,
            grid=(geo.t // tb,),
            in_specs=[pl.BlockSpec((TOP_K, tb), lambda i, p: (0, i)),
                      pl.BlockSpec((TOP_K, tb), lambda i, p: (0, i))],
            out_specs=pl.BlockSpec((TOP_K, tb), lambda i, p: (0, i)),
        ),
        out_shape=jax.ShapeDtypeStruct((TOP_K, geo.t), jnp.int32),
        compiler_params=_params(("parallel",)),
    )(pstart, top_e, rank)


def _block_meta_kernel(pstart_ref, counts_ref, pend_ref, e_ref, v_ref):
    row0 = lax.broadcasted_iota(jnp.int32, e_ref.shape, 1) * MOE_BLK
    blk_e = jnp.zeros(e_ref.shape, jnp.int32)
    for x in range(N_EXPERTS):
        blk_e = blk_e + jnp.where(pend_ref[x] <= row0, 1, 0)
    blk_e = jnp.minimum(blk_e, N_EXPERTS - 1)
    last = jnp.zeros(e_ref.shape, jnp.int32)
    for x in range(N_EXPERTS):
        last = last + jnp.where(blk_e == x, pstart_ref[x] + counts_ref[x], 0)
    e_ref[...] = blk_e
    v_ref[...] = jnp.clip(last - row0, 0, MOE_BLK)


def _block_meta(pstart, counts, pend, n_blocks):
    e, v = pl.pallas_call(
        _block_meta_kernel,
        name="block_meta",
        grid_spec=pltpu.PrefetchScalarGridSpec(
            num_scalar_prefetch=3,
            grid=(1,),
            in_specs=[],
            out_specs=[pl.BlockSpec((1, n_blocks), lambda i, a, b, c: (0, 0)),
                       pl.BlockSpec((1, n_blocks), lambda i, a, b, c: (0, 0))],
        ),
        out_shape=[jax.ShapeDtypeStruct((1, n_blocks), jnp.int32), jax.ShapeDtypeStruct((1, n_blocks), jnp.int32)],
        compiler_params=_params(("arbitrary",)),
    )(pstart, counts, pend)
    return e[0], v[0]


def _sc_mesh():
    return plsc.VectorSubcoreMesh(core_axis_name="core", subcore_axis_name="subcore")


def _sc_scatter(rows, idx, n_out, src_block):
    n_idx = idx.shape[0]

    @pl.kernel(out_type=jax.ShapeDtypeStruct((n_out, CHUNK_W), rows.dtype), mesh=_sc_mesh(), scratch_types=[],
               name="sc_dispatch")
    def scatter(x_hbm, i_hbm, o_hbm):
        def body(x_vmem, i_vmem):
            pltpu.sync_copy(x_vmem, o_hbm.at[i_vmem.at[0]])

        pltpu.emit_pipeline(
            body,
            grid=(n_idx // SC_WINDOW,),
            in_specs=[pl.BlockSpec((SC_WINDOW, CHUNK_W), index_map=lambda w: (src_block(w), 0)),
                      pl.BlockSpec((1, SC_WINDOW), index_map=lambda w: (0, w))],
            out_specs=[],
            core_axis_name=("core", "subcore"),
            dimension_semantics=(pltpu.PARALLEL,),
        )(x_hbm, i_hbm)

    return scatter(rows, idx.reshape(1, n_idx))


def _sc_gather(table, idx):
    n_idx = idx.shape[0]

    @pl.kernel(out_type=jax.ShapeDtypeStruct((n_idx, CHUNK_W), table.dtype), mesh=_sc_mesh(),
               name="sc_combine_gather")
    def gather(t_hbm, i_hbm, o_hbm):
        def body(i_vmem, o_vmem):
            pltpu.sync_copy(t_hbm.at[i_vmem.at[0]], o_vmem)

        pltpu.emit_pipeline(
            body,
            grid=(n_idx // SC_WINDOW,),
            in_specs=[pl.BlockSpec((1, SC_WINDOW), index_map=lambda w: (0, w))],
            out_specs=[pl.BlockSpec((SC_WINDOW, CHUNK_W), index_map=lambda w: (w, 0))],
            core_axis_name=("core", "subcore"),
            dimension_semantics=(pltpu.PARALLEL,),
        )(i_hbm, o_hbm)

    return gather(table, idx.reshape(1, n_idx))


def _ffn_kernel(blk_e_ref, blk_valid_ref, n_used_ref, xs_ref, wg_ref, wu_ref, wd_ref, y_ref, wg_sc, wu_sc, wd_sc):
    b = pl.program_id(0)
    used = b < n_used_ref[0]
    new_expert = (b == 0) | (blk_e_ref[b] != blk_e_ref[jnp.maximum(b - 1, 0)])

    @pl.when(used & new_expert)
    def _():
        wg_sc[...] = wg_ref[...].astype(BF16)
        wu_sc[...] = wu_ref[...].astype(BF16)
        wd_sc[...] = wd_ref[...].astype(BF16)

    @pl.when(used)
    def _():
        for r0 in range(0, MOE_BLK, FFN_ROWS):
            rows = pl.ds(r0, FFN_ROWS)
            x = _load_chunks(lambda c: xs_ref.at[c, rows])
            row = r0 + lax.broadcasted_iota(jnp.int32, (FFN_ROWS, 1), 0)
            x = jnp.where(row < blk_valid_ref[b], x, 0.0).astype(BF16)
            g = jnp.dot(x, wg_sc[...], preferred_element_type=F32)
            u = jnp.dot(x, wu_sc[...], preferred_element_type=F32)
            hmid = (g * jax.nn.sigmoid(g) * u).astype(BF16)
            _store_chunks(lambda c: y_ref.at[c, rows], jnp.dot(hmid, wd_sc[...], preferred_element_type=F32))

    @pl.when(jnp.logical_not(used))
    def _():
        y_ref[...] = jnp.zeros_like(y_ref)


def _ffn(xs, blk_e, blk_valid, n_used, layer, wg, wu, wd, n_blocks):
    def blk(b, be, bv, nu):
        return jnp.maximum(jnp.minimum(b, nu[0] - 1), 0)

    def w_idx(b, be, bv, nu):
        return (layer, be[blk(b, be, bv, nu)], 0, 0)

    return pl.pallas_call(
        _ffn_kernel,
        name="expert_ffn",
        grid_spec=pltpu.PrefetchScalarGridSpec(
            num_scalar_prefetch=3,
            grid=(n_blocks,),
            in_specs=[
                pl.BlockSpec((ROW_CHUNKS, MOE_BLK, CHUNK_W), lambda b, be, bv, nu: (0, blk(b, be, bv, nu), 0)),
                pl.BlockSpec((None, None, D, D_EXPERT), w_idx),
                pl.BlockSpec((None, None, D, D_EXPERT), w_idx),
                pl.BlockSpec((None, None, D_EXPERT, D), w_idx),
            ],
            out_specs=pl.BlockSpec((ROW_CHUNKS, MOE_BLK, CHUNK_W), lambda b, be, bv, nu: (0, b, 0)),
            scratch_shapes=[pltpu.VMEM((D, D_EXPERT), BF16), pltpu.VMEM((D, D_EXPERT), BF16),
                            pltpu.VMEM((D_EXPERT, D), BF16)],
        ),
        out_shape=jax.ShapeDtypeStruct(xs.shape, U32),
        compiler_params=_params(("arbitrary",)),
    )(blk_e, blk_valid, n_used, xs, wg, wu, wd)


def _combine_kernel(x_ref, mod_ref, wt_ref, y_ref, sg_ref, su_ref, sd_ref, lg_ref, lb_ref, *o_refs, n_prompt_tiles):
    x = x_ref[...]
    hb = (x * (1.0 + mod_ref[4:5, :]) + mod_ref[3:4, :]).astype(BF16)
    g = jnp.dot(hb, sg_ref[...], preferred_element_type=F32)
    u = jnp.dot(hb, su_ref[...], preferred_element_type=F32)
    ff = jnp.dot((g * jax.nn.sigmoid(g) * u).astype(BF16), sd_ref[...], preferred_element_type=F32)
    for k in range(TOP_K):
        ff = ff + _load_chunks(lambda c: y_ref.at[c, k]) * wt_ref[:, k:k + 1]
    out = _layer_norm_rows(ALPHA * x + mod_ref[5:6, :] * ff, lg_ref[...], lb_ref[...])
    if len(o_refs) == 1:
        o_refs[0][...] = out
    else:
        is_prompt = pl.program_id(0) < n_prompt_tiles

        @pl.when(is_prompt)
        def _():
            o_refs[0][...] = out

        @pl.when(jnp.logical_not(is_prompt))
        def _():
            o_refs[1][...] = out


def _combine(geo, x, mod_l, wt, ytok, sg, su, sd, ln_g, ln_b, split=False):
    tb = TOK_TILE_L
    n_p = geo.t_prompt // tb
    full = lambda shape: pl.BlockSpec(shape, lambda i: (0,) * len(shape))
    if split:
        out_specs = [pl.BlockSpec((tb, D), lambda i: (jnp.minimum(i, n_p - 1), 0)),
                     pl.BlockSpec((tb, D), lambda i: (jnp.maximum(i - n_p, 0), 0))]
        out_shape = [jax.ShapeDtypeStruct((geo.t_prompt, D), F32), jax.ShapeDtypeStruct((geo.t_sample, D), F32)]
    else:
        out_specs = pl.BlockSpec((tb, D), lambda i: (i, 0))
        out_shape = jax.ShapeDtypeStruct((geo.t, D), F32)
    return pl.pallas_call(
        functools.partial(_combine_kernel, n_prompt_tiles=n_p),
        name="combine",
        grid=(geo.t // tb,),
        in_specs=[
            pl.BlockSpec((tb, D), lambda i: (i, 0)),
            pl.BlockSpec((None, 6, D), lambda i: (geo.cond_row(i, tb), 0, 0)),
            pl.BlockSpec((tb, TOP_K), lambda i: (i, 0)),
            pl.BlockSpec((ROW_CHUNKS, TOP_K, tb, CHUNK_W), lambda i: (0, 0, i, 0)),
            full((D, D_EXPERT)), full((D, D_EXPERT)), full((D_EXPERT, D)), full((1, D)), full((1, D)),
        ],
        out_specs=out_specs,
        out_shape=out_shape,
        compiler_params=_params(("arbitrary",)),
    )(x, mod_l, wt, ytok, sg.astype(BF16), su.astype(BF16), sd.astype(BF16),
      ln_g.reshape(1, D), ln_b.reshape(1, D))


def _moe_stages(geo, x, mod_l, w_router, e_bias, layer, wg, wu, wd, sg, su, sd, ln_g, ln_b, split=False):
    t = geo.t
    top_e, w, rank, counts, h = _router(geo, x, mod_l, w_router, e_bias)
    n_blocks = (t * TOP_K) // MOE_BLK + N_EXPERTS
    n_rows = n_blocks * MOE_BLK
    padded = (counts + MOE_BLK - 1) // MOE_BLK * MOE_BLK
    pend = jnp.cumsum(padded)
    pstart = (pend - padded).astype(jnp.int32)
    blk_e, blk_valid = _block_meta(pstart, counts, pend.astype(jnp.int32), n_blocks)
    n_used = (pend[-1:] // MOE_BLK).astype(jnp.int32)
    slots = _slots(geo, pstart, top_e, rank)
    idx = (slots.reshape(1, TOP_K * t) + (jnp.arange(ROW_CHUNKS, dtype=jnp.int32) * n_rows)[:, None]).reshape(-1)
    win_per_chunk = TOP_K * t // SC_WINDOW
    tok_windows = t // SC_WINDOW

    def src_block(wdw):
        return (wdw // win_per_chunk) * tok_windows + (wdw % win_per_chunk) % tok_windows

    xs = _sc_scatter(h.reshape(ROW_CHUNKS * t, CHUNK_W), idx, ROW_CHUNKS * n_rows, src_block)
    yield
    yb = _ffn(xs.reshape(ROW_CHUNKS, n_rows, CHUNK_W), blk_e, blk_valid, n_used, layer, wg, wu, wd, n_blocks)
    ytok = _sc_gather(yb.reshape(ROW_CHUNKS * n_rows, CHUNK_W), idx)
    yield
    return _combine(geo, x, mod_l, w.T, ytok.reshape(ROW_CHUNKS, TOP_K, t, CHUNK_W), sg, su, sd, ln_g, ln_b,
                    split=split)


def _run_interleaved(gens):
    results = [None] * len(gens)
    live = list(range(len(gens)))
    while live:
        for i in list(live):
            try:
                next(gens[i])
            except StopIteration as stop:
                results[i] = stop.value
                live.remove(i)
    return results


def _moe_layer(*args, **kwargs):
    return _run_interleaved([_moe_stages(*args, **kwargs)])[0]


def _pos_embed(rows):
    quarter = D // 4
    omega = 1.0 / (POS_BASE ** (jnp.arange(quarter, dtype=F32) / quarter))
    r, col = jnp.meshgrid(jnp.arange(rows, dtype=F32), jnp.arange(GRID_W, dtype=F32), indexing='ij')
    r = r.reshape(-1, 1) * omega
    col = col.reshape(-1, 1) * omega
    return jnp.concatenate([jnp.sin(r), jnp.cos(r), jnp.sin(col), jnp.cos(col)], axis=-1)


def _mlstm_layer(geo, x, mod_l, j, a_w_in, a_b_gates, a_norm, a_w_out, ln_g, ln_b,
                 state_C, state_n, state_m):
    q, kt, v, so, gr = _proj_a(geo, x, mod_l, a_w_in[j], a_b_gates[j])
    hp, c_p, n_p, m_p = _mlstm_scan(q, kt, v, gr, row0=0, n_seq=geo.n_prompt, seq_len=geo.prompt_len)
    ns = geo.n_sample
    n0 = jnp.pad(state_n[:, j].astype(F32)[..., None], ((0, 0),) * 4 + ((0, LANES - 1),))
    m0 = jnp.pad(state_m[:, j].astype(F32), ((0, 0), (0, 0), (0, SUBLANES - NH_A)))
    m0 = jnp.broadcast_to(m0[..., None], (ns, 2, SUBLANES, LANES))
    hs, _, _, _ = _mlstm_scan(q, kt, v, gr, row0=geo.t_prompt, n_seq=ns, seq_len=geo.sample_len,
                              state=(state_C[:, j].astype(F32), n0, m0))
    x1 = _out_a(geo, hp, hs, so, a_norm[j], a_w_out[j], x, mod_l, ln_g, ln_b)
    return x1, c_p, n_p[..., 0], m_p[:, :, :NH_A, 0]


def _hgrn_layer(geo, x, mod_l, j, lb_layer, b_w_in, b_norm, b_w_out, ln_g, ln_b, state_S):
    q, pre, v, sg = _proj_b(geo, x, mod_l, b_w_in[j])
    lbd = lb_layer.reshape(2, NH_B, 1, DK_B)
    op, s_p = _hgrn_scan(q, pre, v, lbd, row0=0, n_seq=geo.n_prompt, seq_len=geo.prompt_len)
    os_, _ = _hgrn_scan(q, pre, v, lbd, row0=geo.t_prompt, n_seq=geo.n_sample, seq_len=geo.sample_len,
                        state=state_S[:, j].astype(F32))
    x1 = _out_b(geo, op, os_, sg, b_norm[j], b_w_out[j], x, mod_l, ln_g, ln_b)
    return x1, s_p


def kernel(x_prompt, x_sample, state_mlstm_C, state_mlstm_n, state_mlstm_m, state_hgrn_S, c, c_ctx, w_mod, b_mod, ln_g, ln_b, a_w_in, a_b_gates, a_norm, a_w_out, b_w_in, b_lb, b_norm, b_w_out, w_router, e_bias, w_gate, w_up, w_down, ws_gate, ws_up, ws_down):
    bp, sp, _ = x_prompt.shape
    bs, ss, _ = x_sample.shape
    cond = jnp.zeros((COND_ROWS, D), F32).at[0].set(c_ctx).at[1:1 + bs].set(c)
    mod = _modulation(cond, w_mod, b_mod)
    x = (x_prompt.reshape(-1, D), x_sample.reshape(-1, D), _pos_embed(ss // GRID_W))
    sm = jax.nn.softmax(b_lb.astype(F32), axis=0)
    lb_all = jnp.cumsum(sm, axis=0) - sm[0]

    n_streams = N_STREAMS if bp % N_STREAMS == 0 and bs % N_STREAMS == 0 else 1
    pb, sb = bp // n_streams, bs // n_streams

    def stream(s):
        geo = Geometry(pb, sp, sb, ss, prompt0=s * pb, sample0=s * sb)
        rows = [0] + [1 + s * sb + b for b in range(sb)]
        mod_s = mod[:, jnp.array(rows + [0] * (COND_ROWS - len(rows)), jnp.int32)]
        seqs = slice(s * sb, (s + 1) * sb)
        x1, new_c, new_n, new_m = _mlstm_layer(geo, x, mod_s[0], 0, a_w_in, a_b_gates, a_norm, a_w_out,
                                               ln_g[0, 0], ln_b[0, 0], state_mlstm_C[seqs], state_mlstm_n[seqs],
                                               state_mlstm_m[seqs])
        x2 = yield from _moe_stages(geo, x1, mod_s[0], w_router[0], e_bias[0], 0, w_gate, w_up, w_down, ws_gate[0],
                                    ws_up[0], ws_down[0], ln_g[0, 1], ln_b[0, 1])
        x3, new_s = _hgrn_layer(geo, x2, mod_s[1], 0, lb_all[1], b_w_in, b_norm, b_w_out, ln_g[1, 0], ln_b[1, 0],
                                state_hgrn_S[seqs])
        y_p, y_s = yield from _moe_stages(geo, x3, mod_s[1], w_router[1], e_bias[1], 1, w_gate, w_up, w_down,
                                          ws_gate[1], ws_up[1], ws_down[1], ln_g[1, 1], ln_b[1, 1], split=True)
        return y_p.reshape(pb, sp, D), y_s.reshape(sb, ss, D), new_c[:, None], new_n[:, None], new_m[:, None], \
            new_s[:, None]

    outs = _run_interleaved([stream(s) for s in range(n_streams)])
    return tuple(jnp.concatenate(leaf, axis=0) for leaf in zip(*outs))
```

```python
import functools
import math

import jax
import jax.numpy as jnp
from jax import lax
from jax.experimental import pallas as pl
from jax.experimental.pallas import tpu as pltpu
from jax.experimental.pallas import tpu_sc as plsc

F32 = jnp.float32
BF16 = jnp.bfloat16
HIGHEST = lax.Precision.HIGHEST

D = 1024
DEPTH = 2
GRID_W = 64
POS_BASE = 10000.0
EPS = 1e-6
ALPHA = (2.0 * DEPTH) ** 0.25
NH_A, DK_A, DV_A = 4, 128, 256
QK_A, V_A = NH_A * DK_A, NH_A * DV_A
NH_B, DK_B = 8, 128
N_EXPERTS, TOP_K, N_GROUPS, TOPK_GROUPS = 64, 8, 8, 4
GROUP_SIZE = N_EXPERTS // N_GROUPS
D_EXPERT = D // 4
ROUTED_SCALE = 2.5

LANES = 128
SUBLANES = 8
COND_ROWS = 8
TOK_TILE = 256
TOK_TILE_L = 512
SLOT_TILE = 2048
CHUNK_A = 256
VMEM_LIMIT = 56 * 1024 * 1024

NT_DIMS = (((1,), (1,)), ((), ()))


def _params(sem):
    return pltpu.CompilerParams(dimension_semantics=sem, vmem_limit_bytes=VMEM_LIMIT)


def _split3(x):
    hi = x.astype(BF16)
    r = x - hi.astype(F32)
    mid = r.astype(BF16)
    lo = (r - mid.astype(F32)).astype(BF16)
    return hi, mid, lo


def _dot3(a_bf, x, transpose_side=None):
    hi, mid, lo = _split3(x)
    return (jnp.dot(a_bf, hi, preferred_element_type=F32)
            + jnp.dot(a_bf, mid, preferred_element_type=F32)
            + jnp.dot(a_bf, lo, preferred_element_type=F32))


def _dot3_r(x, a_bf):
    hi, mid, lo = _split3(x)
    return (jnp.dot(hi, a_bf, preferred_element_type=F32)
            + jnp.dot(mid, a_bf, preferred_element_type=F32)
            + jnp.dot(lo, a_bf, preferred_element_type=F32))


def _log_sigmoid(x):
    return jnp.minimum(x, 0.0) - jnp.log1p(jnp.exp(-jnp.abs(x)))


def _layer_norm_rows(x, g, b):
    mu = jnp.mean(x, axis=-1, keepdims=True)
    xc = x - mu
    var = jnp.mean(xc * xc, axis=-1, keepdims=True)
    return xc * lax.rsqrt(var + EPS) * g + b


class Geometry:
    def __init__(self, n_prompt, prompt_len, n_sample, sample_len, prompt0=0, sample0=0):
        self.n_prompt, self.prompt_len = n_prompt, prompt_len
        self.n_sample, self.sample_len = n_sample, sample_len
        self.prompt0, self.sample0 = prompt0, sample0
        self.t_prompt = n_prompt * prompt_len
        self.t_sample = n_sample * sample_len
        self.t = self.t_prompt + self.t_sample
        assert self.t_prompt % TOK_TILE_L == 0 and sample_len % TOK_TILE_L == 0
        assert (prompt0 * prompt_len) % TOK_TILE_L == 0
        assert n_sample + 1 <= COND_ROWS

    def cond_row(self, tile, tile_rows):
        n_p = self.t_prompt // tile_rows
        return jnp.where(tile < n_p, 0, 1 + (tile - n_p) // (self.sample_len // tile_rows))


def _mod_kernel(cond_ref, w_ref, b_ref, o_ref):
    c = cond_ref[...]
    s = c * jax.nn.sigmoid(c)
    o_ref[0, 0] = jnp.dot(s, w_ref[0], precision=HIGHEST, preferred_element_type=F32) + b_ref[0, 0]


def _modulation(cond, w_mod, b_mod):
    out = pl.pallas_call(
        _mod_kernel,
        name="modulation",
        grid=(DEPTH, 6),
        in_specs=[
            pl.BlockSpec((COND_ROWS, D), lambda l, j: (0, 0)),
            pl.BlockSpec((1, D, D), lambda l, j: (l, 0, j)),
            pl.BlockSpec((1, 1, 1, D), lambda l, j: (l, j, 0, 0)),
        ],
        out_specs=pl.BlockSpec((1, 1, COND_ROWS, D), lambda l, j: (l, j, 0, 0)),
        out_shape=jax.ShapeDtypeStruct((DEPTH, 6, COND_ROWS, D), F32),
        compiler_params=_params(("arbitrary", "arbitrary")),
    )(cond, w_mod, b_mod.reshape(DEPTH, 6, 1, D))
    return out.transpose(0, 2, 1, 3)


def _embed_specs(geo, tb):
    n_p = geo.t_prompt // tb
    per_seq = geo.sample_len // tb
    p0 = geo.prompt0 * geo.prompt_len // tb
    s0 = geo.sample0 * per_seq
    return [pl.BlockSpec((tb, D), lambda i: (p0 + jnp.minimum(i, n_p - 1), 0)),
            pl.BlockSpec((tb, D), lambda i: (s0 + jnp.maximum(i - n_p, 0), 0)),
            pl.BlockSpec((tb, D), lambda i: (jnp.maximum(i - n_p, 0) % per_seq, 0))]


def _embed_tile(xp_ref, xs_ref, pos_ref, n_prompt_tiles):
    return jnp.where(pl.program_id(0) < n_prompt_tiles, xp_ref[...], xs_ref[...] + pos_ref[...])


def _proj_a_kernel(xp_ref, xs_ref, pos_ref, mod_ref, wq_ref, wkt_ref, wvo_ref, wgt_ref, bgt_ref,
                   q_ref, kt_ref, v_ref, so_ref, gr_ref, *, n_prompt_tiles):
    x = _embed_tile(xp_ref, xs_ref, pos_ref, n_prompt_tiles)
    h = x * (1.0 + mod_ref[1:2, :]) + mod_ref[0:1, :]
    hb = h.astype(BF16)
    q_ref[...] = jnp.dot(hb, wq_ref[...], preferred_element_type=F32).astype(BF16)
    kt = lax.dot_general(wkt_ref[...], hb, NT_DIMS, preferred_element_type=F32)
    kt_ref[...] = (kt * (DK_A ** -0.5)).astype(BF16)
    vo = jnp.dot(hb, wvo_ref[...], preferred_element_type=F32)
    v_ref[...] = vo[:, :V_A].astype(BF16)
    so_ref[...] = jax.nn.sigmoid(vo[:, V_A:]).astype(BF16)
    gr_ref[...] = lax.dot_general(wgt_ref[...], h, NT_DIMS, precision=HIGHEST,
                                  preferred_element_type=F32) + bgt_ref[...]


def _proj_a(geo, x, mod_l, w_in, b_gates):
    t = geo.t
    n_gate = 4 * NH_A
    wq = w_in[:, :QK_A].astype(BF16)
    wkt = w_in[:, QK_A:2 * QK_A].T.astype(BF16)
    wvo = w_in[:, 2 * QK_A:2 * QK_A + 2 * V_A].astype(BF16)
    wg = w_in[:, 2 * QK_A + 2 * V_A:]
    bg = b_gates.reshape(n_gate).astype(F32)
    tb = TOK_TILE_L
    full = lambda shape: pl.BlockSpec(shape, lambda i: (0,) * len(shape))
    return pl.pallas_call(
        functools.partial(_proj_a_kernel, n_prompt_tiles=geo.t_prompt // tb),
        name="proj_a",
        grid=(t // tb,),
        in_specs=_embed_specs(geo, tb) + [
            pl.BlockSpec((None, 6, D), lambda i: (geo.cond_row(i, tb), 0, 0)),
            full((D, QK_A)), full((QK_A, D)), full((D, 2 * V_A)), full((n_gate, D)), full((n_gate, 1)),
        ],
        out_specs=[
            pl.BlockSpec((tb, QK_A), lambda i: (i, 0)),
            pl.BlockSpec((QK_A, tb), lambda i: (0, i)),
            pl.BlockSpec((tb, V_A), lambda i: (i, 0)),
            pl.BlockSpec((tb, V_A), lambda i: (i, 0)),
            pl.BlockSpec((n_gate, tb), lambda i: (0, i)),
        ],
        out_shape=[
            jax.ShapeDtypeStruct((t, QK_A), BF16),
            jax.ShapeDtypeStruct((QK_A, t), BF16),
            jax.ShapeDtypeStruct((t, V_A), BF16),
            jax.ShapeDtypeStruct((t, V_A), BF16),
            jax.ShapeDtypeStruct((n_gate, t), F32),
        ],
        compiler_params=_params(("parallel",)),
    )(*x, mod_l, wq, wkt, wvo, wg.T, bg.reshape(n_gate, 1))


def _mlstm_scan_kernel(*refs, chunk, has_state):
    if has_state:
        (q_ref, kt_ref, v_ref, gr_ref, c0_ref, n0_ref, m0_ref,
         h_ref, c_out, n_out, m_out, c_sc, n_sc, m_sc) = refs
    else:
        (q_ref, kt_ref, v_ref, gr_ref,
         h_ref, c_out, n_out, m_out, c_sc, n_sc, m_sc) = refs
    L = chunk
    d = pl.program_id(1)
    c = pl.program_id(2)
    fwd = d == 0

    @pl.when(c == 0)
    def _():
        if has_state:
            c_sc[...] = c0_ref[0, 0]
            n_sc[...] = n0_ref[0, 0]
            m_sc[...] = m0_ref[0, 0]
        else:
            c_sc[...] = jnp.zeros_like(c_sc)
            n_sc[...] = jnp.zeros_like(n_sc)
            m_sc[...] = jnp.zeros_like(m_sc)

    row = lax.broadcasted_iota(jnp.int32, (L, L), 0)
    col = lax.broadcasted_iota(jnp.int32, (L, L), 1)
    sgn = 1 - 2 * d
    causal = (row - col) * sgn >= 0
    tri_t = ((col - row) * sgn >= 0).astype(BF16)

    gr = gr_ref[...]
    br_all = _dot3_r(_log_sigmoid(gr), tri_t)
    bc_all = jnp.concatenate([br_all, jnp.zeros((LANES - br_all.shape[0], L), F32)], axis=0).T
    ones_blk = (lax.broadcasted_iota(jnp.int32, (L, LANES), 1) == 0).astype(BF16)

    for h in range(NH_A):
        b_c = jnp.where(fwd, bc_all[:, 4 + h:5 + h], bc_all[:, 12 + h:13 + h])
        b_r = jnp.where(fwd, br_all[4 + h:5 + h, :], br_all[12 + h:13 + h, :])
        i_r = jnp.where(fwd, gr[h:h + 1, :], gr[8 + h:9 + h, :])
        bl = jnp.where(fwd, b_r[:, L - 1:L], b_r[:, 0:1])
        q = q_ref[:, h * DK_A:(h + 1) * DK_A]
        kt = kt_ref[h * DK_A:(h + 1) * DK_A, :]
        v = v_ref[:, h * DV_A:(h + 1) * DV_A]
        m = m_sc[h:h + 1, 0:1]
        cst = c_sc[h]
        nst = n_sc[h]

        a_r = i_r - b_r
        logd = jnp.where(causal, b_c + a_r, -jnp.inf)
        inter = b_c + m
        m_t = jnp.maximum(inter, jnp.max(logd, axis=1, keepdims=True))
        dmat = jnp.exp(logd - m_t)
        e_int = jnp.exp(inter - m_t)
        s = (jnp.dot(q, kt, preferred_element_type=F32) * dmat).astype(BF16)
        num = (jnp.dot(s, v, preferred_element_type=F32)
               + e_int * jnp.dot(q, cst.astype(BF16), preferred_element_type=F32))
        den = (jnp.dot(s, ones_blk, preferred_element_type=F32)
               + e_int * jnp.dot(q, nst.astype(BF16), preferred_element_type=F32))[:, 0:1]
        h_ref[:, h * DV_A:(h + 1) * DV_A] = (num / jnp.maximum(jnp.abs(den), jnp.exp(-m_t))).astype(BF16)

        logw = bl + a_r
        m_new = jnp.maximum(bl + m, jnp.max(logw, axis=1, keepdims=True))
        w = jnp.exp(logw - m_new)
        decay = jnp.exp(bl + m - m_new)
        kw = (kt.astype(F32) * w).astype(BF16)
        c_sc[h] = decay * cst + jnp.dot(kw, v, preferred_element_type=F32)
        n_sc[h] = decay * nst + jnp.dot(kw, ones_blk, preferred_element_type=F32)
        m_sc[h:h + 1, :] = jnp.broadcast_to(m_new, (1, LANES))

    @pl.when(c == pl.num_programs(2) - 1)
    def _():
        c_out[0, 0] = c_sc[...]
        n_out[0, 0] = n_sc[...]
        m_out[0, 0] = m_sc[...]


def _mlstm_scan(q, kt, v, gr, *, row0, n_seq, seq_len, state=None):
    L = CHUNK_A
    nc = seq_len // L
    blk0 = row0 // L

    def loc_blk(b, d, c):
        return b * nc + c + d * (nc - 1 - 2 * c)

    def tok_blk(b, d, c):
        return blk0 + loc_blk(b, d, c)

    in_specs = [
        pl.BlockSpec((L, QK_A), lambda b, d, c: (tok_blk(b, d, c), 0)),
        pl.BlockSpec((QK_A, L), lambda b, d, c: (0, tok_blk(b, d, c))),
        pl.BlockSpec((L, V_A), lambda b, d, c: (tok_blk(b, d, c), 0)),
        pl.BlockSpec((4 * NH_A, L), lambda b, d, c: (0, tok_blk(b, d, c))),
    ]
    args = [q, kt, v, gr]
    if state is not None:
        in_specs += [
            pl.BlockSpec((1, 1, NH_A, DK_A, DV_A), lambda b, d, c: (b, d, 0, 0, 0)),
            pl.BlockSpec((1, 1, NH_A, DK_A, LANES), lambda b, d, c: (b, d, 0, 0, 0)),
            pl.BlockSpec((1, 1, SUBLANES, LANES), lambda b, d, c: (b, d, 0, 0)),
        ]
        args += list(state)
    return pl.pallas_call(
        functools.partial(_mlstm_scan_kernel, chunk=L, has_state=state is not None),
        name="mlstm_scan_seeded" if state is not None else "mlstm_scan",
        grid=(n_seq, 2, nc),
        in_specs=in_specs,
        out_specs=[
            pl.BlockSpec((None, L, V_A), lambda b, d, c: (d, loc_blk(b, d, c), 0)),
            pl.BlockSpec((1, 1, NH_A, DK_A, DV_A), lambda b, d, c: (b, d, 0, 0, 0)),
            pl.BlockSpec((1, 1, NH_A, DK_A, LANES), lambda b, d, c: (b, d, 0, 0, 0)),
            pl.BlockSpec((1, 1, SUBLANES, LANES), lambda b, d, c: (b, d, 0, 0)),
        ],
        out_shape=[
            jax.ShapeDtypeStruct((2, n_seq * seq_len, V_A), BF16),
            jax.ShapeDtypeStruct((n_seq, 2, NH_A, DK_A, DV_A), F32),
            jax.ShapeDtypeStruct((n_seq, 2, NH_A, DK_A, LANES), F32),
            jax.ShapeDtypeStruct((n_seq, 2, SUBLANES, LANES), F32),
        ],
        scratch_shapes=[
            pltpu.VMEM((NH_A, DK_A, DV_A), F32),
            pltpu.VMEM((NH_A, DK_A, LANES), F32),
            pltpu.VMEM((SUBLANES, LANES), F32),
        ],
        compiler_params=_params(("parallel", "parallel", "arbitrary")),
    )(*args)


def _out_a_kernel(hp_ref, hs_ref, so_ref, nw_ref, w_ref, xp_ref, xs_ref, pos_ref, mod_ref, lg_ref, lb_ref, o_ref, *,
                  n_prompt_tiles):
    is_prompt = pl.program_id(0) < n_prompt_tiles
    x = _embed_tile(xp_ref, xs_ref, pos_ref, n_prompt_tiles)
    y = jnp.where(is_prompt, hp_ref[0].astype(F32) + hp_ref[1].astype(F32),
                  hs_ref[0].astype(F32) + hs_ref[1].astype(F32))
    parts = []
    for h in range(NH_A):
        yh = y[:, h * DV_A:(h + 1) * DV_A]
        mu = jnp.mean(yh, axis=-1, keepdims=True)
        yc = yh - mu
        var = jnp.mean(yc * yc, axis=-1, keepdims=True)
        parts.append(yc * lax.rsqrt(var + EPS))
    yn = jnp.concatenate(parts, axis=-1) * nw_ref[...] * so_ref[...].astype(F32)
    out = jnp.dot(yn.astype(BF16), w_ref[...], preferred_element_type=F32)
    o_ref[...] = _layer_norm_rows(ALPHA * x + mod_ref[2:3, :] * out, lg_ref[...], lb_ref[...])


def _out_a(geo, h_prompt, h_sample, so, norm_w, w_out, x, mod_l, ln_g, ln_b):
    t = geo.t
    tb = TOK_TILE_L
    n_p = geo.t_prompt // tb
    full = lambda shape: pl.BlockSpec(shape, lambda i: (0,) * len(shape))
    return pl.pallas_call(
        functools.partial(_out_a_kernel, n_prompt_tiles=n_p),
        name="out_a",
        grid=(t // tb,),
        in_specs=[
            pl.BlockSpec((2, tb, V_A), lambda i: (0, jnp.minimum(i, n_p - 1), 0)),
            pl.BlockSpec((2, tb, V_A), lambda i: (0, jnp.maximum(i - n_p, 0), 0)),
            pl.BlockSpec((tb, V_A), lambda i: (i, 0)),
            full((1, V_A)), full((V_A, D)),
        ] + _embed_specs(geo, tb) + [
            pl.BlockSpec((None, 6, D), lambda i: (geo.cond_row(i, tb), 0, 0)),
            full((1, D)), full((1, D)),
        ],
        out_specs=pl.BlockSpec((tb, D), lambda i: (i, 0)),
        out_shape=jax.ShapeDtypeStruct((t, D), F32),
        compiler_params=_params(("parallel",)),
    )(h_prompt, h_sample, so, norm_w.reshape(1, V_A).astype(F32), w_out.astype(BF16), *x, mod_l,
      ln_g.reshape(1, D), ln_b.reshape(1, D))


def _proj_b_kernel(x_ref, mod_ref, w_ref, q_ref, pre_ref, v_ref, sg_ref):
    h = x_ref[...] * (1.0 + mod_ref[1:2, :]) + mod_ref[0:1, :]
    z = jnp.dot(h.astype(BF16), w_ref[...], preferred_element_type=F32)
    for hd in range(NH_B):
        lo = hd * DK_B
        qh = z[:, lo:lo + DK_B]
        q_ref[hd] = qh * jax.nn.sigmoid(qh)
        pre_ref[0, hd] = z[:, D + lo:D + lo + DK_B]
        pre_ref[1, hd] = z[:, 2 * D + lo:2 * D + lo + DK_B]
        v_ref[hd] = z[:, 3 * D + lo:3 * D + lo + DK_B].astype(BF16)
    g = z[:, 4 * D:]
    sg_ref[...] = (g * jax.nn.sigmoid(g)).astype(BF16)


def _proj_b(geo, x, mod_l, w_in):
    t = geo.t
    tb = TOK_TILE
    return pl.pallas_call(
        _proj_b_kernel,
        name="proj_b",
        grid=(t // tb,),
        in_specs=[
            pl.BlockSpec((tb, D), lambda i: (i, 0)),
            pl.BlockSpec((None, 6, D), lambda i: (geo.cond_row(i, tb), 0, 0)),
            pl.BlockSpec((D, 5 * D), lambda i: (0, 0)),
        ],
        out_specs=[
            pl.BlockSpec((NH_B, tb, DK_B), lambda i: (0, i, 0)),
            pl.BlockSpec((2, NH_B, tb, DK_B), lambda i: (0, 0, i, 0)),
            pl.BlockSpec((NH_B, tb, DK_B), lambda i: (0, i, 0)),
            pl.BlockSpec((tb, D), lambda i: (i, 0)),
        ],
        out_shape=[
            jax.ShapeDtypeStruct((NH_B, t, DK_B), F32),
            jax.ShapeDtypeStruct((2, NH_B, t, DK_B), F32),
            jax.ShapeDtypeStruct((NH_B, t, DK_B), BF16),
            jax.ShapeDtypeStruct((t, D), BF16),
        ],
        compiler_params=_params(("parallel",)),
    )(x, mod_l, w_in.astype(BF16))


CHUNK_B = 128
BAND = SUBLANES // 2
TN_DIMS = (((0,), (0,)), ((), ()))


def _hgrn_head(q, pre, lbv, v_bf, st, fwd):
    L = q.shape[0]
    sg = jax.nn.sigmoid(pre)
    f = lbv + (1.0 - lbv) * sg
    lf = jnp.log(f)
    kk = (1.0 - lbv) * (1.0 - sg)
    row = lax.broadcasted_iota(jnp.int32, (L, L), 0)
    col = lax.broadcasted_iota(jnp.int32, (L, L), 1)
    tri = ((row >= col) if fwd else (row <= col)).astype(BF16)
    b = _dot3(tri, lf)
    tpos = lax.broadcasted_iota(jnp.int32, (L, DK_B), 0)
    blk_bits = row ^ col
    lag = jnp.where(blk_bits < BAND, (row - col) if fwd else (col - row), -1)

    step = 1 if fwd else L - 1
    att = jnp.where(lag == 0, jnp.sum(q * kk, axis=1, keepdims=True), 0.0)
    f_r, kk_r, g = f, kk, f
    for dl in range(1, BAND):
        if dl > 1:
            f_r = pltpu.roll(f_r, step, 0)
            g = g * f_r
        kk_r = pltpu.roll(kk_r, step, 0)
        att = jnp.where(lag == dl, jnp.sum(q * kk_r * g, axis=1, keepdims=True), att)

    w = BAND
    while w < L:
        nb = L // (2 * w)
        b3 = b.reshape(nb, 2 * w, DK_B)
        edge = (b3[:, w - 1:w, :] if fwd else b3[:, w:w + 1, :])
        bmid = jnp.broadcast_to(edge, (nb, 2 * w, DK_B)).reshape(L, DK_B)
        second = (tpos & w) != 0
        t_side = second if fwd else jnp.logical_not(second)
        e = jnp.exp(jnp.where(t_side, b - bmid, bmid - b))
        qt = jnp.where(t_side, q * e, 0.0).astype(BF16)
        ks = jnp.where(t_side, 0.0, kk * e).astype(BF16)
        a = lax.dot_general(qt, ks, NT_DIMS, preferred_element_type=F32)
        att = att + jnp.where(blk_bits < 2 * w, a, 0.0)
        w *= 2
    o = jnp.dot(att.astype(BF16), v_bf, preferred_element_type=F32)

    bl = b[L - 1:L, :] if fwd else b[0:1, :]
    o = o + lax.dot_general((q * jnp.exp(b)).astype(BF16), st.astype(BF16), NT_DIMS, preferred_element_type=F32)
    kd = (kk * jnp.exp(bl - b)).astype(BF16)
    st_new = jnp.exp(bl) * st + lax.dot_general(v_bf, kd, TN_DIMS, preferred_element_type=F32)
    return o, st_new


def _hgrn_scan_kernel(*refs, has_state):
    if has_state:
        q_ref, pre_ref, v_ref, lb_ref, s0_ref, o_ref, s_out, st_sc = refs
    else:
        q_ref, pre_ref, v_ref, lb_ref, o_ref, s_out, st_sc = refs
    d = pl.program_id(1)
    c = pl.program_id(2)

    @pl.when(c == 0)
    def _():
        if has_state:
            for hd in range(NH_B):
                st_sc[hd] = s0_ref[0, 0, hd].T
        else:
            st_sc[...] = jnp.zeros_like(st_sc)

    def run(fwd):
        def head(hd, carry):
            o, st_new = _hgrn_head(q_ref[hd], pre_ref[hd], lb_ref[hd], v_ref[hd], st_sc[hd], fwd)
            o_ref[hd] = o.astype(BF16)
            st_sc[hd] = st_new
            return carry
        lax.fori_loop(0, NH_B, head, 0, unroll=8)

    @pl.when(d == 0)
    def _():
        run(True)

    @pl.when(d == 1)
    def _():
        run(False)

    @pl.when(c == pl.num_programs(2) - 1)
    def _():
        for hd in range(NH_B):
            s_out[0, 0, hd] = st_sc[hd].T


def _hgrn_scan(q, pre, v, lbd, *, row0, n_seq, seq_len, state=None):
    L = CHUNK_B
    nc = seq_len // L
    blk0 = row0 // L

    def loc_blk(b, d, c):
        return b * nc + c + d * (nc - 1 - 2 * c)

    def tok_blk(b, d, c):
        return blk0 + loc_blk(b, d, c)

    in_specs = [
        pl.BlockSpec((NH_B, L, DK_B), lambda b, d, c: (0, tok_blk(b, d, c), 0)),
        pl.BlockSpec((None, NH_B, L, DK_B), lambda b, d, c: (d, 0, tok_blk(b, d, c), 0)),
        pl.BlockSpec((NH_B, L, DK_B), lambda b, d, c: (0, tok_blk(b, d, c), 0)),
        pl.BlockSpec((None, NH_B, 1, DK_B), lambda b, d, c: (d, 0, 0, 0)),
    ]
    args = [q, pre, v, lbd]
    if state is not None:
        in_specs.append(pl.BlockSpec((1, 1, NH_B, DK_B, DK_B), lambda b, d, c: (b, d, 0, 0, 0)))
        args.append(state)
    return pl.pallas_call(
        functools.partial(_hgrn_scan_kernel, has_state=state is not None),
        name="hgrn_scan_seeded" if state is not None else "hgrn_scan",
        grid=(n_seq, 2, nc),
        in_specs=in_specs,
        out_specs=[
            pl.BlockSpec((None, NH_B, L, DK_B), lambda b, d, c: (d, 0, loc_blk(b, d, c), 0)),
            pl.BlockSpec((1, 1, NH_B, DK_B, DK_B), lambda b, d, c: (b, d, 0, 0, 0)),
        ],
        out_shape=[
            jax.ShapeDtypeStruct((2, NH_B, n_seq * seq_len, DK_B), BF16),
            jax.ShapeDtypeStruct((n_seq, 2, NH_B, DK_B, DK_B), F32),
        ],
        scratch_shapes=[pltpu.VMEM((NH_B, DK_B, DK_B), F32)],
        compiler_params=_params(("parallel", "parallel", "arbitrary")),
    )(*args)


def _out_b_kernel(op_ref, os_ref, sg_ref, nw_ref, w_ref, x_ref, mod_ref, lg_ref, lb_ref, out_ref, *, n_prompt_tiles):
    is_prompt = pl.program_id(0) < n_prompt_tiles
    parts = []
    for hd in range(NH_B):
        y = jnp.where(is_prompt, op_ref[0, hd].astype(F32) + op_ref[1, hd].astype(F32),
                      os_ref[0, hd].astype(F32) + os_ref[1, hd].astype(F32))
        parts.append(y * lax.rsqrt(jnp.mean(y * y, axis=-1, keepdims=True) + EPS))
    yn = jnp.concatenate(parts, axis=-1) * nw_ref[...] * sg_ref[...].astype(F32)
    out = jnp.dot(yn.astype(BF16), w_ref[...], preferred_element_type=F32)
    out_ref[...] = _layer_norm_rows(ALPHA * x_ref[...] + mod_ref[2:3, :] * out, lg_ref[...], lb_ref[...])


def _out_b(geo, o_prompt, o_sample, sg, norm_w, w_out, x, mod_l, ln_g, ln_b):
    t = geo.t
    tb = TOK_TILE_L
    n_p = geo.t_prompt // tb
    full = lambda shape: pl.BlockSpec(shape, lambda i: (0,) * len(shape))
    return pl.pallas_call(
        functools.partial(_out_b_kernel, n_prompt_tiles=n_p),
        name="out_b",
        grid=(t // tb,),
        in_specs=[
            pl.BlockSpec((2, NH_B, tb, DK_B), lambda i: (0, 0, jnp.minimum(i, n_p - 1), 0)),
            pl.BlockSpec((2, NH_B, tb, DK_B), lambda i: (0, 0, jnp.maximum(i - n_p, 0), 0)),
            pl.BlockSpec((tb, D), lambda i: (i, 0)),
            full((1, D)), full((D, D)),
            pl.BlockSpec((tb, D), lambda i: (i, 0)),
            pl.BlockSpec((None, 6, D), lambda i: (geo.cond_row(i, tb), 0, 0)),
            full((1, D)), full((1, D)),
        ],
        out_specs=pl.BlockSpec((tb, D), lambda i: (i, 0)),
        out_shape=jax.ShapeDtypeStruct((t, D), F32),
        compiler_params=_params(("parallel",)),
    )(o_prompt, o_sample, sg, norm_w.reshape(1, D).astype(F32), w_out.astype(BF16), x, mod_l,
      ln_g.reshape(1, D), ln_b.reshape(1, D))


N_STREAMS = 1
MOE_BLK = 1024
FFN_ROWS = MOE_BLK
U32 = jnp.uint32
ROW_WORDS = D // 2
CHUNK_W = 256
ROW_CHUNKS = ROW_WORDS // CHUNK_W
SC_WINDOW = 128


def _pack_rows(x):
    hi = pltpu.bitcast(x[:, :ROW_WORDS].astype(BF16).astype(F32), U32)
    lo = pltpu.bitcast(x[:, ROW_WORDS:].astype(BF16).astype(F32), U32)
    return hi | (lo >> 16)


def _unpack_rows(words):
    hi = pltpu.bitcast(words & jnp.uint32(0xFFFF0000), F32)
    lo = pltpu.bitcast(words << 16, F32)
    return jnp.concatenate([hi, lo], axis=1)


def _store_chunks(chunk_ref, x):
    words = _pack_rows(x)
    for c in range(ROW_CHUNKS):
        chunk_ref(c)[...] = words[:, c * CHUNK_W:(c + 1) * CHUNK_W]


def _load_chunks(chunk_ref):
    return _unpack_rows(jnp.concatenate([chunk_ref(c)[...] for c in range(ROW_CHUNKS)], axis=1))


def _first_index(hit, iota, size, axis):
    return jnp.min(jnp.where(hit, iota, size), axis=axis, keepdims=True)


def _router_kernel(x_ref, mod_ref, wrt_ref, eb_ref, e_ref, w_ref, r_ref, cnt_ref, h_ref, cnt_sc):
    i = pl.program_id(0)
    tb = x_ref.shape[0]

    @pl.when(i == 0)
    def _():
        cnt_sc[...] = jnp.zeros_like(cnt_sc)

    h = x_ref[...] * (1.0 + mod_ref[4:5, :]) + mod_ref[3:4, :]
    _store_chunks(lambda c: h_ref.at[c], h)
    logits = lax.dot_general(wrt_ref[...], h, NT_DIMS, precision=HIGHEST, preferred_element_type=F32)
    scores = jax.nn.sigmoid(logits)
    sel = scores + eb_ref[...]

    g3 = sel.reshape(N_GROUPS, GROUP_SIZE, tb)
    io3 = lax.broadcasted_iota(jnp.int32, g3.shape, 1)
    m1 = jnp.max(g3, axis=1, keepdims=True)
    first = _first_index(g3 == m1, io3, GROUP_SIZE, 1)
    m2 = jnp.max(jnp.where(io3 == first, -jnp.inf, g3), axis=1, keepdims=True)
    gscore = (m1 + m2).reshape(N_GROUPS, tb)

    iog = lax.broadcasted_iota(jnp.int32, gscore.shape, 0)
    gmask = jnp.zeros(gscore.shape, F32)
    for _ in range(TOPK_GROUPS):
        gm = jnp.max(gscore, axis=0, keepdims=True)
        pick = iog == _first_index(gscore == gm, iog, N_GROUPS, 0)
        gmask = jnp.where(pick, 1.0, gmask)
        gscore = jnp.where(pick, -jnp.inf, gscore)
    emask = jnp.broadcast_to(gmask.reshape(N_GROUPS, 1, tb), (N_GROUPS, GROUP_SIZE, tb)).reshape(N_EXPERTS, tb)
    cand = jnp.where(emask > 0.0, sel, -jnp.inf)

    ioe = lax.broadcasted_iota(jnp.int32, cand.shape, 0)
    picks, wts = [], []
    onehot = jnp.zeros(cand.shape, F32)
    for _ in range(TOP_K):
        cm = jnp.max(cand, axis=0, keepdims=True)
        idx = _first_index(cand == cm, ioe, N_EXPERTS, 0)
        pick = ioe == idx
        picks.append(pick)
        wts.append(jnp.sum(jnp.where(pick, scores, 0.0), axis=0, keepdims=True))
        onehot = onehot + pick.astype(F32)
        cand = jnp.where(pick, -jnp.inf, cand)
        e_ref[pl.ds(len(picks) - 1, 1), :] = idx
    wsum = wts[0]
    for wk in wts[1:]:
        wsum = wsum + wk
    for k in range(TOP_K):
        w_ref[pl.ds(k, 1), :] = wts[k] / wsum * ROUTED_SCALE

    r_io = lax.broadcasted_iota(jnp.int32, (tb, tb), 0)
    c_io = lax.broadcasted_iota(jnp.int32, (tb, tb), 1)
    before = (r_io < c_io).astype(BF16)
    rank = cnt_sc[:, 0:1] + jnp.dot(onehot.astype(BF16), before, preferred_element_type=F32)
    for k in range(TOP_K):
        r_ref[pl.ds(k, 1), :] = jnp.sum(jnp.where(picks[k], rank, 0.0), axis=0, keepdims=True).astype(jnp.int32)
    cnt_sc[...] = cnt_sc[...] + jnp.sum(onehot, axis=1, keepdims=True)
    cnt_ref[...] = cnt_sc[...]


def _router(geo, x, mod_l, w_router, e_bias):
    t = geo.t
    tb = TOK_TILE_L
    full = lambda shape: pl.BlockSpec(shape, lambda i: (0,) * len(shape))
    e, w, r, cnt, h = pl.pallas_call(
        _router_kernel,
        name="router",
        grid=(t // tb,),
        in_specs=[
            pl.BlockSpec((tb, D), lambda i: (i, 0)),
            pl.BlockSpec((None, 6, D), lambda i: (geo.cond_row(i, tb), 0, 0)),
            full((N_EXPERTS, D)), full((N_EXPERTS, 1)),
        ],
        out_specs=[
            pl.BlockSpec((TOP_K, tb), lambda i: (0, i)),
            pl.BlockSpec((TOP_K, tb), lambda i: (0, i)),
            pl.BlockSpec((TOP_K, tb), lambda i: (0, i)),
            full((N_EXPERTS, LANES)),
            pl.BlockSpec((ROW_CHUNKS, tb, CHUNK_W), lambda i: (0, i, 0)),
        ],
        out_shape=[
            jax.ShapeDtypeStruct((TOP_K, t), jnp.int32),
            jax.ShapeDtypeStruct((TOP_K, t), F32),
            jax.ShapeDtypeStruct((TOP_K, t), jnp.int32),
            jax.ShapeDtypeStruct((N_EXPERTS, LANES), F32),
            jax.ShapeDtypeStruct((ROW_CHUNKS, t, CHUNK_W), U32),
        ],
        scratch_shapes=[pltpu.VMEM((N_EXPERTS, LANES), F32)],
        compiler_params=_params(("arbitrary",)),
    )(x, mod_l, w_router.T.astype(F32), e_bias.reshape(N_EXPERTS, 1).astype(F32))
    return e, w, r, cnt[:, 0].astype(jnp.int32), h


def _slot_kernel(pstart_ref, e_ref, r_ref, o_ref):
    e = e_ref[...]
    slot = r_ref[...]
    for x in range(N_EXPERTS):
        slot = slot + jnp.where(e == x, pstart_ref[x], 0)
    o_ref[...] = slot


def _slots(geo, pstart, top_e, rank):
    tb = math.gcd(SLOT_TILE, geo.t)
    return pl.pallas_call(
        _slot_kernel,
        name="slots",
        grid_spec=pltpu.PrefetchScalarGridSpec(
            num_scalar_prefetch=1,
            grid=(geo.t // tb,),
            in_specs=[pl.BlockSpec((TOP_K, tb), lambda i, p: (0, i)),
                      pl.BlockSpec((TOP_K, tb), lambda i, p: (0, i))],
            out_specs=pl.BlockSpec((TOP_K, tb), lambda i, p: (0, i)),
        ),
        out_shape=jax.ShapeDtypeStruct((TOP_K, geo.t), jnp.int32),
        compiler_params=_params(("parallel",)),
    )(pstart, top_e, rank)


def _block_meta_kernel(pstart_ref, counts_ref, pend_ref, e_ref, v_ref):
    row0 = lax.broadcasted_iota(jnp.int32, e_ref.shape, 1) * MOE_BLK
    blk_e = jnp.zeros(e_ref.shape, jnp.int32)
    for x in range(N_EXPERTS):
        blk_e = blk_e + jnp.where(pend_ref[x] <= row0, 1, 0)
    blk_e = jnp.minimum(blk_e, N_EXPERTS - 1)
    last = jnp.zeros(e_ref.shape, jnp.int32)
    for x in range(N_EXPERTS):
        last = last + jnp.where(blk_e == x, pstart_ref[x] + counts_ref[x], 0)
    e_ref[...] = blk_e
    v_ref[...] = jnp.clip(last - row0, 0, MOE_BLK)


def _block_meta(pstart, counts, pend, n_blocks):
    e, v = pl.pallas_call(
        _block_meta_kernel,
        name="block_meta",
        grid_spec=pltpu.PrefetchScalarGridSpec(
            num_scalar_prefetch=3,
            grid=(1,),
            in_specs=[],
            out_specs=[pl.BlockSpec((1, n_blocks), lambda i, a, b, c: (0, 0)),
                       pl.BlockSpec((1, n_blocks), lambda i, a, b, c: (0, 0))],
        ),
        out_shape=[jax.ShapeDtypeStruct((1, n_blocks), jnp.int32), jax.ShapeDtypeStruct((1, n_blocks), jnp.int32)],
        compiler_params=_params(("arbitrary",)),
    )(pstart, counts, pend)
    return e[0], v[0]


def _sc_mesh():
    return plsc.VectorSubcoreMesh(core_axis_name="core", subcore_axis_name="subcore")


def _sc_scatter(rows, idx, n_out, copies):
    n_src = rows.shape[0]
    n_idx = idx.shape[0]
    groups = ROW_CHUNKS
    win_per_group = n_src // groups // SC_WINDOW

    def idx_block(w, k):
        return (0, ((w // win_per_group) * copies + k) * win_per_group + w % win_per_group)

    @pl.kernel(out_type=jax.ShapeDtypeStruct((n_out, CHUNK_W), rows.dtype), mesh=_sc_mesh(), scratch_types=[],
               name="sc_dispatch")
    def scatter(x_hbm, i_hbm, o_hbm):
        def body(x_vmem, *i_vmems):
            for i_vmem in i_vmems:
                pltpu.sync_copy(x_vmem, o_hbm.at[i_vmem.at[0]])

        pltpu.emit_pipeline(
            body,
            grid=(n_src // SC_WINDOW,),
            in_specs=[pl.BlockSpec((SC_WINDOW, CHUNK_W), index_map=lambda w: (w, 0))]
            + [pl.BlockSpec((1, SC_WINDOW), index_map=functools.partial(idx_block, k=k)) for k in range(copies)],
            out_specs=[],
            core_axis_name=("core", "subcore"),
            dimension_semantics=(pltpu.PARALLEL,),
        )(x_hbm, *([i_hbm] * copies))

    return scatter(rows, idx.reshape(1, n_idx))


def _sc_gather(table, idx):
    n_idx = idx.shape[0]

    @pl.kernel(out_type=jax.ShapeDtypeStruct((n_idx, CHUNK_W), table.dtype), mesh=_sc_mesh(),
               name="sc_combine_gather")
    def gather(t_hbm, i_hbm, o_hbm):
        def body(i_vmem, o_vmem):
            pltpu.sync_copy(t_hbm.at[i_vmem.at[0]], o_vmem)

        pltpu.emit_pipeline(
            body,
            grid=(n_idx // SC_WINDOW,),
            in_specs=[pl.BlockSpec((1, SC_WINDOW), index_map=lambda w: (0, w))],
            out_specs=[pl.BlockSpec((SC_WINDOW, CHUNK_W), index_map=lambda w: (w, 0))],
            core_axis_name=("core", "subcore"),
            dimension_semantics=(pltpu.PARALLEL,),
        )(i_hbm, o_hbm)

    return gather(table, idx.reshape(1, n_idx))


def _ffn_kernel(blk_e_ref, blk_valid_ref, n_used_ref, xs_ref, wg_ref, wu_ref, wd_ref, y_ref, wg_sc, wu_sc, wd_sc):
    b = pl.program_id(0)
    used = b < n_used_ref[0]
    new_expert = (b == 0) | (blk_e_ref[b] != blk_e_ref[jnp.maximum(b - 1, 0)])

    @pl.when(used & new_expert)
    def _():
        wg_sc[...] = wg_ref[...].astype(BF16)
        wu_sc[...] = wu_ref[...].astype(BF16)
        wd_sc[...] = wd_ref[...].astype(BF16)

    @pl.when(used)
    def _():
        for r0 in range(0, MOE_BLK, FFN_ROWS):
            rows = pl.ds(r0, FFN_ROWS)
            x = _load_chunks(lambda c: xs_ref.at[c, rows])
            row = r0 + lax.broadcasted_iota(jnp.int32, (FFN_ROWS, 1), 0)
            x = jnp.where(row < blk_valid_ref[b], x, 0.0).astype(BF16)
            g = jnp.dot(x, wg_sc[...], preferred_element_type=F32)
            u = jnp.dot(x, wu_sc[...], preferred_element_type=F32)
            hmid = (g * jax.nn.sigmoid(g) * u).astype(BF16)
            _store_chunks(lambda c: y_ref.at[c, rows], jnp.dot(hmid, wd_sc[...], preferred_element_type=F32))

    @pl.when(jnp.logical_not(used))
    def _():
        y_ref[...] = jnp.zeros_like(y_ref)


def _ffn(xs, blk_e, blk_valid, n_used, layer, wg, wu, wd, n_blocks):
    def blk(b, be, bv, nu):
        return jnp.maximum(jnp.minimum(b, nu[0] - 1), 0)

    def w_idx(b, be, bv, nu):
        return (layer, be[blk(b, be, bv, nu)], 0, 0)

    return pl.pallas_call(
        _ffn_kernel,
        name="expert_ffn",
        grid_spec=pltpu.PrefetchScalarGridSpec(
            num_scalar_prefetch=3,
            grid=(n_blocks,),
            in_specs=[
                pl.BlockSpec((ROW_CHUNKS, MOE_BLK, CHUNK_W), lambda b, be, bv, nu: (0, blk(b, be, bv, nu), 0)),
                pl.BlockSpec((None, None, D, D_EXPERT), w_idx),
                pl.BlockSpec((None, None, D, D_EXPERT), w_idx),
                pl.BlockSpec((None, None, D_EXPERT, D), w_idx),
            ],
            out_specs=pl.BlockSpec((ROW_CHUNKS, MOE_BLK, CHUNK_W), lambda b, be, bv, nu: (0, b, 0)),
            scratch_shapes=[pltpu.VMEM((D, D_EXPERT), BF16), pltpu.VMEM((D, D_EXPERT), BF16),
                            pltpu.VMEM((D_EXPERT, D), BF16)],
        ),
        out_shape=jax.ShapeDtypeStruct(xs.shape, U32),
        compiler_params=_params(("arbitrary",)),
    )(blk_e, blk_valid, n_used, xs, wg, wu, wd)


def _combine_kernel(x_ref, mod_ref, wt_ref, y_ref, sg_ref, su_ref, sd_ref, lg_ref, lb_ref, *o_refs, n_prompt_tiles):
    x = x_ref[...]
    hb = (x * (1.0 + mod_ref[4:5, :]) + mod_ref[3:4, :]).astype(BF16)
    g = jnp.dot(hb, sg_ref[...], preferred_element_type=F32)
    u = jnp.dot(hb, su_ref[...], preferred_element_type=F32)
    ff = jnp.dot((g * jax.nn.sigmoid(g) * u).astype(BF16), sd_ref[...], preferred_element_type=F32)
    for k in range(TOP_K):
        ff = ff + _load_chunks(lambda c: y_ref.at[c, k]) * wt_ref[:, k:k + 1]
    out = _layer_norm_rows(ALPHA * x + mod_ref[5:6, :] * ff, lg_ref[...], lb_ref[...])
    if len(o_refs) == 1:
        o_refs[0][...] = out
    else:
        is_prompt = pl.program_id(0) < n_prompt_tiles

        @pl.when(is_prompt)
        def _():
            o_refs[0][...] = out

        @pl.when(jnp.logical_not(is_prompt))
        def _():
            o_refs[1][...] = out


def _combine(geo, x, mod_l, wt, ytok, sg, su, sd, ln_g, ln_b, split=False):
    tb = TOK_TILE_L
    n_p = geo.t_prompt // tb
    full = lambda shape: pl.BlockSpec(shape, lambda i: (0,) * len(shape))
    if split:
        out_specs = [pl.BlockSpec((tb, D), lambda i: (jnp.minimum(i, n_p - 1), 0)),
                     pl.BlockSpec((tb, D), lambda i: (jnp.maximum(i - n_p, 0), 0))]
        out_shape = [jax.ShapeDtypeStruct((geo.t_prompt, D), F32), jax.ShapeDtypeStruct((geo.t_sample, D), F32)]
    else:
        out_specs = pl.BlockSpec((tb, D), lambda i: (i, 0))
        out_shape = jax.ShapeDtypeStruct((geo.t, D), F32)
    return pl.pallas_call(
        functools.partial(_combine_kernel, n_prompt_tiles=n_p),
        name="combine",
        grid=(geo.t // tb,),
        in_specs=[
            pl.BlockSpec((tb, D), lambda i: (i, 0)),
            pl.BlockSpec((None, 6, D), lambda i: (geo.cond_row(i, tb), 0, 0)),
            pl.BlockSpec((tb, TOP_K), lambda i: (i, 0)),
            pl.BlockSpec((ROW_CHUNKS, TOP_K, tb, CHUNK_W), lambda i: (0, 0, i, 0)),
            full((D, D_EXPERT)), full((D, D_EXPERT)), full((D_EXPERT, D)), full((1, D)), full((1, D)),
        ],
        out_specs=out_specs,
        out_shape=out_shape,
        compiler_params=_params(("arbitrary",)),
    )(x, mod_l, wt, ytok, sg.astype(BF16), su.astype(BF16), sd.astype(BF16),
      ln_g.reshape(1, D), ln_b.reshape(1, D))


def _moe_stages(geo, x, mod_l, w_router, e_bias, layer, wg, wu, wd, sg, su, sd, ln_g, ln_b, split=False):
    t = geo.t
    top_e, w, rank, counts, h = _router(geo, x, mod_l, w_router, e_bias)
    n_blocks = (t * TOP_K) // MOE_BLK + N_EXPERTS
    n_rows = n_blocks * MOE_BLK
    padded = (counts + MOE_BLK - 1) // MOE_BLK * MOE_BLK
    pend = jnp.cumsum(padded)
    pstart = (pend - padded).astype(jnp.int32)
    blk_e, blk_valid = _block_meta(pstart, counts, pend.astype(jnp.int32), n_blocks)
    n_used = (pend[-1:] // MOE_BLK).astype(jnp.int32)
    slots = _slots(geo, pstart, top_e, rank)
    idx = (slots.reshape(1, TOP_K * t) + (jnp.arange(ROW_CHUNKS, dtype=jnp.int32) * n_rows)[:, None]).reshape(-1)
    xs = _sc_scatter(h.reshape(ROW_CHUNKS * t, CHUNK_W), idx, ROW_CHUNKS * n_rows, TOP_K)
    yield
    yb = _ffn(xs.reshape(ROW_CHUNKS, n_rows, CHUNK_W), blk_e, blk_valid, n_used, layer, wg, wu, wd, n_blocks)
    ytok = _sc_gather(yb.reshape(ROW_CHUNKS * n_rows, CHUNK_W), idx)
    yield
    return _combine(geo, x, mod_l, w.T, ytok.reshape(ROW_CHUNKS, TOP_K, t, CHUNK_W), sg, su, sd, ln_g, ln_b,
                    split=split)


def _run_interleaved(gens):
    results = [None] * len(gens)
    live = list(range(len(gens)))
    while live:
        for i in list(live):
            try:
                next(gens[i])
            except StopIteration as stop:
                results[i] = stop.value
                live.remove(i)
    return results


def _moe_layer(*args, **kwargs):
    return _run_interleaved([_moe_stages(*args, **kwargs)])[0]


def _pos_embed(rows):
    quarter = D // 4
    omega = 1.0 / (POS_BASE ** (jnp.arange(quarter, dtype=F32) / quarter))
    r, col = jnp.meshgrid(jnp.arange(rows, dtype=F32), jnp.arange(GRID_W, dtype=F32), indexing='ij')
    r = r.reshape(-1, 1) * omega
    col = col.reshape(-1, 1) * omega
    return jnp.concatenate([jnp.sin(r), jnp.cos(r), jnp.sin(col), jnp.cos(col)], axis=-1)


def _mlstm_layer(geo, x, mod_l, j, a_w_in, a_b_gates, a_norm, a_w_out, ln_g, ln_b,
                 state_C, state_n, state_m):
    q, kt, v, so, gr = _proj_a(geo, x, mod_l, a_w_in[j], a_b_gates[j])
    hp, c_p, n_p, m_p = _mlstm_scan(q, kt, v, gr, row0=0, n_seq=geo.n_prompt, seq_len=geo.prompt_len)
    ns = geo.n_sample
    n0 = jnp.pad(state_n[:, j].astype(F32)[..., None], ((0, 0),) * 4 + ((0, LANES - 1),))
    m0 = jnp.pad(state_m[:, j].astype(F32), ((0, 0), (0, 0), (0, SUBLANES - NH_A)))
    m0 = jnp.broadcast_to(m0[..., None], (ns, 2, SUBLANES, LANES))
    hs, _, _, _ = _mlstm_scan(q, kt, v, gr, row0=geo.t_prompt, n_seq=ns, seq_len=geo.sample_len,
                              state=(state_C[:, j].astype(F32), n0, m0))
    x1 = _out_a(geo, hp, hs, so, a_norm[j], a_w_out[j], x, mod_l, ln_g, ln_b)
    return x1, c_p, n_p[..., 0], m_p[:, :, :NH_A, 0]


def _hgrn_layer(geo, x, mod_l, j, lb_layer, b_w_in, b_norm, b_w_out, ln_g, ln_b, state_S):
    q, pre, v, sg = _proj_b(geo, x, mod_l, b_w_in[j])
    lbd = lb_layer.reshape(2, NH_B, 1, DK_B)
    op, s_p = _hgrn_scan(q, pre, v, lbd, row0=0, n_seq=geo.n_prompt, seq_len=geo.prompt_len)
    os_, _ = _hgrn_scan(q, pre, v, lbd, row0=geo.t_prompt, n_seq=geo.n_sample, seq_len=geo.sample_len,
                        state=state_S[:, j].astype(F32))
    x1 = _out_b(geo, op, os_, sg, b_norm[j], b_w_out[j], x, mod_l, ln_g, ln_b)
    return x1, s_p


def kernel(x_prompt, x_sample, state_mlstm_C, state_mlstm_n, state_mlstm_m, state_hgrn_S, c, c_ctx, w_mod, b_mod, ln_g, ln_b, a_w_in, a_b_gates, a_norm, a_w_out, b_w_in, b_lb, b_norm, b_w_out, w_router, e_bias, w_gate, w_up, w_down, ws_gate, ws_up, ws_down):
    bp, sp, _ = x_prompt.shape
    bs, ss, _ = x_sample.shape
    cond = jnp.zeros((COND_ROWS, D), F32).at[0].set(c_ctx).at[1:1 + bs].set(c)
    mod = _modulation(cond, w_mod, b_mod)
    x = (x_prompt.reshape(-1, D), x_sample.reshape(-1, D), _pos_embed(ss // GRID_W))
    sm = jax.nn.softmax(b_lb.astype(F32), axis=0)
    lb_all = jnp.cumsum(sm, axis=0) - sm[0]

    n_streams = N_STREAMS if bp % N_STREAMS == 0 and bs % N_STREAMS == 0 else 1
    pb, sb = bp // n_streams, bs // n_streams

    def stream(s):
        geo = Geometry(pb, sp, sb, ss, prompt0=s * pb, sample0=s * sb)
        rows = [0] + [1 + s * sb + b for b in range(sb)]
        mod_s = mod[:, jnp.array(rows + [0] * (COND_ROWS - len(rows)), jnp.int32)]
        seqs = slice(s * sb, (s + 1) * sb)
        x1, new_c, new_n, new_m = _mlstm_layer(geo, x, mod_s[0], 0, a_w_in, a_b_gates, a_norm, a_w_out,
                                               ln_g[0, 0], ln_b[0, 0], state_mlstm_C[seqs], state_mlstm_n[seqs],
                                               state_mlstm_m[seqs])
        x2 = yield from _moe_stages(geo, x1, mod_s[0], w_router[0], e_bias[0], 0, w_gate, w_up, w_down, ws_gate[0],
                                    ws_up[0], ws_down[0], ln_g[0, 1], ln_b[0, 1])
        x3, new_s = _hgrn_layer(geo, x2, mod_s[1], 0, lb_all[1], b_w_in, b_norm, b_w_out, ln_g[1, 0], ln_b[1, 0],
                                state_hgrn_S[seqs])
        y_p, y_s = yield from _moe_stages(geo, x3, mod_s[1], w_router[1], e_bias[1], 1, w_gate, w_up, w_down,
                                          ws_gate[1], ws_up[1], ws_down[1], ln_g[1, 1], ln_b[1, 1], split=True)
        return y_p.reshape(pb, sp, D), y_s.reshape(sb, ss, D), new_c[:, None], new_n[:, None], new_m[:, None], \
            new_s[:, None]

    outs = _run_interleaved([stream(s) for s in range(n_streams)])
    return tuple(jnp.concatenate(leaf, axis=0) for leaf in zip(*outs))
```

```python
import functools
import math

import jax
import jax.numpy as jnp
from jax import lax
from jax.experimental import pallas as pl
from jax.experimental.pallas import tpu as pltpu
from jax.experimental.pallas import tpu_sc as plsc

F32 = jnp.float32
BF16 = jnp.bfloat16
HIGHEST = lax.Precision.HIGHEST

D = 1024
DEPTH = 2
GRID_W = 64
POS_BASE = 10000.0
EPS = 1e-6
ALPHA = (2.0 * DEPTH) ** 0.25
NH_A, DK_A, DV_A = 4, 128, 256
QK_A, V_A = NH_A * DK_A, NH_A * DV_A
NH_B, DK_B = 8, 128
N_EXPERTS, TOP_K, N_GROUPS, TOPK_GROUPS = 64, 8, 8, 4
GROUP_SIZE = N_EXPERTS // N_GROUPS
D_EXPERT = D // 4
ROUTED_SCALE = 2.5

LANES = 128
SUBLANES = 8
COND_ROWS = 8
TOK_TILE = 256
TOK_TILE_L = 512
SLOT_TILE = 2048
CHUNK_A = 256
VMEM_LIMIT = 56 * 1024 * 1024

NT_DIMS = (((1,), (1,)), ((), ()))


def _params(sem):
    return pltpu.CompilerParams(dimension_semantics=sem, vmem_limit_bytes=VMEM_LIMIT)


def _split3(x):
    hi = x.astype(BF16)
    r = x - hi.astype(F32)
    mid = r.astype(BF16)
    lo = (r - mid.astype(F32)).astype(BF16)
    return hi, mid, lo


def _dot3(a_bf, x, transpose_side=None):
    hi, mid, lo = _split3(x)
    return (jnp.dot(a_bf, hi, preferred_element_type=F32)
            + jnp.dot(a_bf, mid, preferred_element_type=F32)
            + jnp.dot(a_bf, lo, preferred_element_type=F32))


def _dot3_r(x, a_bf):
    hi, mid, lo = _split3(x)
    return (jnp.dot(hi, a_bf, preferred_element_type=F32)
            + jnp.dot(mid, a_bf, preferred_element_type=F32)
            + jnp.dot(lo, a_bf, preferred_element_type=F32))


def _log_sigmoid(x):
    return jnp.minimum(x, 0.0) - jnp.log1p(jnp.exp(-jnp.abs(x)))


def _layer_norm_rows(x, g, b):
    mu = jnp.mean(x, axis=-1, keepdims=True)
    xc = x - mu
    var = jnp.mean(xc * xc, axis=-1, keepdims=True)
    return xc * lax.rsqrt(var + EPS) * g + b


class Geometry:
    def __init__(self, n_prompt, prompt_len, n_sample, sample_len, prompt0=0, sample0=0):
        self.n_prompt, self.prompt_len = n_prompt, prompt_len
        self.n_sample, self.sample_len = n_sample, sample_len
        self.prompt0, self.sample0 = prompt0, sample0
        self.t_prompt = n_prompt * prompt_len
        self.t_sample = n_sample * sample_len
        self.t = self.t_prompt + self.t_sample
        assert self.t_prompt % TOK_TILE_L == 0 and sample_len % TOK_TILE_L == 0
        assert (prompt0 * prompt_len) % TOK_TILE_L == 0
        assert n_sample + 1 <= COND_ROWS

    def cond_row(self, tile, tile_rows):
        n_p = self.t_prompt // tile_rows
        return jnp.where(tile < n_p, 0, 1 + (tile - n_p) // (self.sample_len // tile_rows))


def _mod_kernel(cond_ref, w_ref, b_ref, o_ref):
    c = cond_ref[...]
    s = c * jax.nn.sigmoid(c)
    o_ref[0, 0] = jnp.dot(s, w_ref[0], precision=HIGHEST, preferred_element_type=F32) + b_ref[0, 0]


def _modulation(cond, w_mod, b_mod):
    out = pl.pallas_call(
        _mod_kernel,
        name="modulation",
        grid=(DEPTH, 6),
        in_specs=[
            pl.BlockSpec((COND_ROWS, D), lambda l, j: (0, 0)),
            pl.BlockSpec((1, D, D), lambda l, j: (l, 0, j)),
            pl.BlockSpec((1, 1, 1, D), lambda l, j: (l, j, 0, 0)),
        ],
        out_specs=pl.BlockSpec((1, 1, COND_ROWS, D), lambda l, j: (l, j, 0, 0)),
        out_shape=jax.ShapeDtypeStruct((DEPTH, 6, COND_ROWS, D), F32),
        compiler_params=_params(("arbitrary", "arbitrary")),
    )(cond, w_mod, b_mod.reshape(DEPTH, 6, 1, D))
    return out.transpose(0, 2, 1, 3)


def _embed_specs(geo, tb):
    n_p = geo.t_prompt // tb
    per_seq = geo.sample_len // tb
    p0 = geo.prompt0 * geo.prompt_len // tb
    s0 = geo.sample0 * per_seq
    return [pl.BlockSpec((tb, D), lambda i: (p0 + jnp.minimum(i, n_p - 1), 0)),
            pl.BlockSpec((tb, D), lambda i: (s0 + jnp.maximum(i - n_p, 0), 0)),
            pl.BlockSpec((tb, D), lambda i: (jnp.maximum(i - n_p, 0) % per_seq, 0))]


def _embed_tile(xp_ref, xs_ref, pos_ref, n_prompt_tiles):
    return jnp.where(pl.program_id(0) < n_prompt_tiles, xp_ref[...], xs_ref[...] + pos_ref[...])


def _proj_a_kernel(xp_ref, xs_ref, pos_ref, mod_ref, wq_ref, wkt_ref, wvo_ref, wgt_ref, bgt_ref,
                   q_ref, kt_ref, v_ref, so_ref, gr_ref, *, n_prompt_tiles):
    x = _embed_tile(xp_ref, xs_ref, pos_ref, n_prompt_tiles)
    h = x * (1.0 + mod_ref[1:2, :]) + mod_ref[0:1, :]
    hb = h.astype(BF16)
    q_ref[...] = jnp.dot(hb, wq_ref[...], preferred_element_type=F32).astype(BF16)
    kt = lax.dot_general(wkt_ref[...], hb, NT_DIMS, preferred_element_type=F32)
    kt_ref[...] = (kt * (DK_A ** -0.5)).astype(BF16)
    vo = jnp.dot(hb, wvo_ref[...], preferred_element_type=F32)
    v_ref[...] = vo[:, :V_A].astype(BF16)
    so_ref[...] = jax.nn.sigmoid(vo[:, V_A:]).astype(BF16)
    gr_ref[...] = lax.dot_general(wgt_ref[...], h, NT_DIMS, precision=HIGHEST,
                                  preferred_element_type=F32) + bgt_ref[...]


def _proj_a(geo, x, mod_l, w_in, b_gates):
    t = geo.t
    n_gate = 4 * NH_A
    wq = w_in[:, :QK_A].astype(BF16)
    wkt = w_in[:, QK_A:2 * QK_A].T.astype(BF16)
    wvo = w_in[:, 2 * QK_A:2 * QK_A + 2 * V_A].astype(BF16)
    wg = w_in[:, 2 * QK_A + 2 * V_A:]
    bg = b_gates.reshape(n_gate).astype(F32)
    tb = TOK_TILE_L
    full = lambda shape: pl.BlockSpec(shape, lambda i: (0,) * len(shape))
    return pl.pallas_call(
        functools.partial(_proj_a_kernel, n_prompt_tiles=geo.t_prompt // tb),
        name="proj_a",
        grid=(t // tb,),
        in_specs=_embed_specs(geo, tb) + [
            pl.BlockSpec((None, 6, D), lambda i: (geo.cond_row(i, tb), 0, 0)),
            full((D, QK_A)), full((QK_A, D)), full((D, 2 * V_A)), full((n_gate, D)), full((n_gate, 1)),
        ],
        out_specs=[
            pl.BlockSpec((tb, QK_A), lambda i: (i, 0)),
            pl.BlockSpec((QK_A, tb), lambda i: (0, i)),
            pl.BlockSpec((tb, V_A), lambda i: (i, 0)),
            pl.BlockSpec((tb, V_A), lambda i: (i, 0)),
            pl.BlockSpec((n_gate, tb), lambda i: (0, i)),
        ],
        out_shape=[
            jax.ShapeDtypeStruct((t, QK_A), BF16),
            jax.ShapeDtypeStruct((QK_A, t), BF16),
            jax.ShapeDtypeStruct((t, V_A), BF16),
            jax.ShapeDtypeStruct((t, V_A), BF16),
            jax.ShapeDtypeStruct((n_gate, t), F32),
        ],
        compiler_params=_params(("parallel",)),
    )(*x, mod_l, wq, wkt, wvo, wg.T, bg.reshape(n_gate, 1))


def _mlstm_scan_kernel(*refs, chunk, has_state):
    if has_state:
        (q_ref, kt_ref, v_ref, gr_ref, c0_ref, n0_ref, m0_ref,
         h_ref, c_out, n_out, m_out, c_sc, n_sc, m_sc) = refs
    else:
        (q_ref, kt_ref, v_ref, gr_ref,
         h_ref, c_out, n_out, m_out, c_sc, n_sc, m_sc) = refs
    L = chunk
    d = pl.program_id(1)
    c = pl.program_id(2)
    fwd = d == 0

    @pl.when(c == 0)
    def _():
        if has_state:
            c_sc[...] = c0_ref[0, 0]
            n_sc[...] = n0_ref[0, 0]
            m_sc[...] = m0_ref[0, 0]
        else:
            c_sc[...] = jnp.zeros_like(c_sc)
            n_sc[...] = jnp.zeros_like(n_sc)
            m_sc[...] = jnp.zeros_like(m_sc)

    row = lax.broadcasted_iota(jnp.int32, (L, L), 0)
    col = lax.broadcasted_iota(jnp.int32, (L, L), 1)
    sgn = 1 - 2 * d
    causal = (row - col) * sgn >= 0
    tri_t = ((col - row) * sgn >= 0).astype(BF16)

    gr = gr_ref[...]
    br_all = _dot3_r(_log_sigmoid(gr), tri_t)
    bc_all = jnp.concatenate([br_all, jnp.zeros((LANES - br_all.shape[0], L), F32)], axis=0).T
    ones_blk = (lax.broadcasted_iota(jnp.int32, (L, LANES), 1) == 0).astype(BF16)

    for h in range(NH_A):
        b_c = jnp.where(fwd, bc_all[:, 4 + h:5 + h], bc_all[:, 12 + h:13 + h])
        b_r = jnp.where(fwd, br_all[4 + h:5 + h, :], br_all[12 + h:13 + h, :])
        i_r = jnp.where(fwd, gr[h:h + 1, :], gr[8 + h:9 + h, :])
        bl = jnp.where(fwd, b_r[:, L - 1:L], b_r[:, 0:1])
        q = q_ref[:, h * DK_A:(h + 1) * DK_A]
        kt = kt_ref[h * DK_A:(h + 1) * DK_A, :]
        v = v_ref[:, h * DV_A:(h + 1) * DV_A]
        m = m_sc[h:h + 1, 0:1]
        cst = c_sc[h]
        nst = n_sc[h]

        a_r = i_r - b_r
        logd = jnp.where(causal, b_c + a_r, -jnp.inf)
        inter = b_c + m
        m_t = jnp.maximum(inter, jnp.max(logd, axis=1, keepdims=True))
        dmat = jnp.exp(logd - m_t)
        e_int = jnp.exp(inter - m_t)
        s = (jnp.dot(q, kt, preferred_element_type=F32) * dmat).astype(BF16)
        num = (jnp.dot(s, v, preferred_element_type=F32)
               + e_int * jnp.dot(q, cst.astype(BF16), preferred_element_type=F32))
        den = (jnp.dot(s, ones_blk, preferred_element_type=F32)
               + e_int * jnp.dot(q, nst.astype(BF16), preferred_element_type=F32))[:, 0:1]
        h_ref[:, h * DV_A:(h + 1) * DV_A] = (num / jnp.maximum(jnp.abs(den), jnp.exp(-m_t))).astype(BF16)

        logw = bl + a_r
        m_new = jnp.maximum(bl + m, jnp.max(logw, axis=1, keepdims=True))
        w = jnp.exp(logw - m_new)
        decay = jnp.exp(bl + m - m_new)
        kw = (kt.astype(F32) * w).astype(BF16)
        c_sc[h] = decay * cst + jnp.dot(kw, v, preferred_element_type=F32)
        n_sc[h] = decay * nst + jnp.dot(kw, ones_blk, preferred_element_type=F32)
        m_sc[h:h + 1, :] = jnp.broadcast_to(m_new, (1, LANES))

    @pl.when(c == pl.num_programs(2) - 1)
    def _():
        c_out[0, 0] = c_sc[...]
        n_out[0, 0] = n_sc[...]
        m_out[0, 0] = m_sc[...]


def _mlstm_scan(q, kt, v, gr, *, row0, n_seq, seq_len, state=None):
    L = CHUNK_A
    nc = seq_len // L
    blk0 = row0 // L

    def loc_blk(b, d, c):
        return b * nc + c + d * (nc - 1 - 2 * c)

    def tok_blk(b, d, c):
        return blk0 + loc_blk(b, d, c)

    in_specs = [
        pl.BlockSpec((L, QK_A), lambda b, d, c: (tok_blk(b, d, c), 0)),
        pl.BlockSpec((QK_A, L), lambda b, d, c: (0, tok_blk(b, d, c))),
        pl.BlockSpec((L, V_A), lambda b, d, c: (tok_blk(b, d, c), 0)),
        pl.BlockSpec((4 * NH_A, L), lambda b, d, c: (0, tok_blk(b, d, c))),
    ]
    args = [q, kt, v, gr]
    if state is not None:
        in_specs += [
            pl.BlockSpec((1, 1, NH_A, DK_A, DV_A), lambda b, d, c: (b, d, 0, 0, 0)),
            pl.BlockSpec((1, 1, NH_A, DK_A, LANES), lambda b, d, c: (b, d, 0, 0, 0)),
            pl.BlockSpec((1, 1, SUBLANES, LANES), lambda b, d, c: (b, d, 0, 0)),
        ]
        args += list(state)
    return pl.pallas_call(
        functools.partial(_mlstm_scan_kernel, chunk=L, has_state=state is not None),
        name="mlstm_scan_seeded" if state is not None else "mlstm_scan",
        grid=(n_seq, 2, nc),
        in_specs=in_specs,
        out_specs=[
            pl.BlockSpec((None, L, V_A), lambda b, d, c: (d, loc_blk(b, d, c), 0)),
            pl.BlockSpec((1, 1, NH_A, DK_A, DV_A), lambda b, d, c: (b, d, 0, 0, 0)),
            pl.BlockSpec((1, 1, NH_A, DK_A, LANES), lambda b, d, c: (b, d, 0, 0, 0)),
            pl.BlockSpec((1, 1, SUBLANES, LANES), lambda b, d, c: (b, d, 0, 0)),
        ],
        out_shape=[
            jax.ShapeDtypeStruct((2, n_seq * seq_len, V_A), BF16),
            jax.ShapeDtypeStruct((n_seq, 2, NH_A, DK_A, DV_A), F32),
            jax.ShapeDtypeStruct((n_seq, 2, NH_A, DK_A, LANES), F32),
            jax.ShapeDtypeStruct((n_seq, 2, SUBLANES, LANES), F32),
        ],
        scratch_shapes=[
            pltpu.VMEM((NH_A, DK_A, DV_A), F32),
            pltpu.VMEM((NH_A, DK_A, LANES), F32),
            pltpu.VMEM((SUBLANES, LANES), F32),
        ],
        compiler_params=_params(("parallel", "parallel", "arbitrary")),
    )(*args)


def _out_a_kernel(hp_ref, hs_ref, so_ref, nw_ref, w_ref, xp_ref, xs_ref, pos_ref, mod_ref, lg_ref, lb_ref, o_ref, *,
                  n_prompt_tiles):
    is_prompt = pl.program_id(0) < n_prompt_tiles
    x = _embed_tile(xp_ref, xs_ref, pos_ref, n_prompt_tiles)
    y = jnp.where(is_prompt, hp_ref[0].astype(F32) + hp_ref[1].astype(F32),
                  hs_ref[0].astype(F32) + hs_ref[1].astype(F32))
    parts = []
    for h in range(NH_A):
        yh = y[:, h * DV_A:(h + 1) * DV_A]
        mu = jnp.mean(yh, axis=-1, keepdims=True)
        yc = yh - mu
        var = jnp.mean(yc * yc, axis=-1, keepdims=True)
        parts.append(yc * lax.rsqrt(var + EPS))
    yn = jnp.concatenate(parts, axis=-1) * nw_ref[...] * so_ref[...].astype(F32)
    out = jnp.dot(yn.astype(BF16), w_ref[...], preferred_element_type=F32)
    o_ref[...] = _layer_norm_rows(ALPHA * x + mod_ref[2:3, :] * out, lg_ref[...], lb_ref[...])


def _out_a(geo, h_prompt, h_sample, so, norm_w, w_out, x, mod_l, ln_g, ln_b):
    t = geo.t
    tb = TOK_TILE_L
    n_p = geo.t_prompt // tb
    full = lambda shape: pl.BlockSpec(shape, lambda i: (0,) * len(shape))
    return pl.pallas_call(
        functools.partial(_out_a_kernel, n_prompt_tiles=n_p),
        name="out_a",
        grid=(t // tb,),
        in_specs=[
            pl.BlockSpec((2, tb, V_A), lambda i: (0, jnp.minimum(i, n_p - 1), 0)),
            pl.BlockSpec((2, tb, V_A), lambda i: (0, jnp.maximum(i - n_p, 0), 0)),
            pl.BlockSpec((tb, V_A), lambda i: (i, 0)),
            full((1, V_A)), full((V_A, D)),
        ] + _embed_specs(geo, tb) + [
            pl.BlockSpec((None, 6, D), lambda i: (geo.cond_row(i, tb), 0, 0)),
            full((1, D)), full((1, D)),
        ],
        out_specs=pl.BlockSpec((tb, D), lambda i: (i, 0)),
        out_shape=jax.ShapeDtypeStruct((t, D), F32),
        compiler_params=_params(("parallel",)),
    )(h_prompt, h_sample, so, norm_w.reshape(1, V_A).astype(F32), w_out.astype(BF16), *x, mod_l,
      ln_g.reshape(1, D), ln_b.reshape(1, D))


def _proj_b_kernel(x_ref, mod_ref, w_ref, q_ref, pre_ref, v_ref, sg_ref):
    h = x_ref[...] * (1.0 + mod_ref[1:2, :]) + mod_ref[0:1, :]
    z = jnp.dot(h.astype(BF16), w_ref[...], preferred_element_type=F32)
    for hd in range(NH_B):
        lo = hd * DK_B
        qh = z[:, lo:lo + DK_B]
        q_ref[hd] = qh * jax.nn.sigmoid(qh)
        pre_ref[0, hd] = z[:, D + lo:D + lo + DK_B]
        pre_ref[1, hd] = z[:, 2 * D + lo:2 * D + lo + DK_B]
        v_ref[hd] = z[:, 3 * D + lo:3 * D + lo + DK_B].astype(BF16)
    g = z[:, 4 * D:]
    sg_ref[...] = (g * jax.nn.sigmoid(g)).astype(BF16)


def _proj_b(geo, x, mod_l, w_in):
    t = geo.t
    tb = TOK_TILE
    return pl.pallas_call(
        _proj_b_kernel,
        name="proj_b",
        grid=(t // tb,),
        in_specs=[
            pl.BlockSpec((tb, D), lambda i: (i, 0)),
            pl.BlockSpec((None, 6, D), lambda i: (geo.cond_row(i, tb), 0, 0)),
            pl.BlockSpec((D, 5 * D), lambda i: (0, 0)),
        ],
        out_specs=[
            pl.BlockSpec((NH_B, tb, DK_B), lambda i: (0, i, 0)),
            pl.BlockSpec((2, NH_B, tb, DK_B), lambda i: (0, 0, i, 0)),
            pl.BlockSpec((NH_B, tb, DK_B), lambda i: (0, i, 0)),
            pl.BlockSpec((tb, D), lambda i: (i, 0)),
        ],
        out_shape=[
            jax.ShapeDtypeStruct((NH_B, t, DK_B), F32),
            jax.ShapeDtypeStruct((2, NH_B, t, DK_B), F32),
            jax.ShapeDtypeStruct((NH_B, t, DK_B), BF16),
            jax.ShapeDtypeStruct((t, D), BF16),
        ],
        compiler_params=_params(("parallel",)),
    )(x, mod_l, w_in.astype(BF16))


CHUNK_B = 128
BAND = SUBLANES // 2
TN_DIMS = (((0,), (0,)), ((), ()))


def _hgrn_head(q, pre, lbv, v_bf, st, fwd):
    L = q.shape[0]
    sg = jax.nn.sigmoid(pre)
    f = lbv + (1.0 - lbv) * sg
    lf = jnp.log(f)
    kk = (1.0 - lbv) * (1.0 - sg)
    row = lax.broadcasted_iota(jnp.int32, (L, L), 0)
    col = lax.broadcasted_iota(jnp.int32, (L, L), 1)
    tri = ((row >= col) if fwd else (row <= col)).astype(BF16)
    b = _dot3(tri, lf)
    tpos = lax.broadcasted_iota(jnp.int32, (L, DK_B), 0)
    blk_bits = row ^ col
    lag = jnp.where(blk_bits < BAND, (row - col) if fwd else (col - row), -1)

    step = 1 if fwd else L - 1
    att = jnp.where(lag == 0, jnp.sum(q * kk, axis=1, keepdims=True), 0.0)
    f_r, kk_r, g = f, kk, f
    for dl in range(1, BAND):
        if dl > 1:
            f_r = pltpu.roll(f_r, step, 0)
            g = g * f_r
        kk_r = pltpu.roll(kk_r, step, 0)
        att = jnp.where(lag == dl, jnp.sum(q * kk_r * g, axis=1, keepdims=True), att)

    w = BAND
    while w < L:
        nb = L // (2 * w)
        b3 = b.reshape(nb, 2 * w, DK_B)
        edge = (b3[:, w - 1:w, :] if fwd else b3[:, w:w + 1, :])
        bmid = jnp.broadcast_to(edge, (nb, 2 * w, DK_B)).reshape(L, DK_B)
        second = (tpos & w) != 0
        t_side = second if fwd else jnp.logical_not(second)
        e = jnp.exp(jnp.where(t_side, b - bmid, bmid - b))
        qt = jnp.where(t_side, q * e, 0.0).astype(BF16)
        ks = jnp.where(t_side, 0.0, kk * e).astype(BF16)
        a = lax.dot_general(qt, ks, NT_DIMS, preferred_element_type=F32)
        att = att + jnp.where(blk_bits < 2 * w, a, 0.0)
        w *= 2
    o = jnp.dot(att.astype(BF16), v_bf, preferred_element_type=F32)

    bl = b[L - 1:L, :] if fwd else b[0:1, :]
    o = o + lax.dot_general((q * jnp.exp(b)).astype(BF16), st.astype(BF16), NT_DIMS, preferred_element_type=F32)
    kd = (kk * jnp.exp(bl - b)).astype(BF16)
    st_new = jnp.exp(bl) * st + lax.dot_general(v_bf, kd, TN_DIMS, preferred_element_type=F32)
    return o, st_new


def _hgrn_scan_kernel(*refs, has_state):
    if has_state:
        q_ref, pre_ref, v_ref, lb_ref, s0_ref, o_ref, s_out, st_sc = refs
    else:
        q_ref, pre_ref, v_ref, lb_ref, o_ref, s_out, st_sc = refs
    d = pl.program_id(1)
    c = pl.program_id(2)

    @pl.when(c == 0)
    def _():
        if has_state:
            for hd in range(NH_B):
                st_sc[hd] = s0_ref[0, 0, hd].T
        else:
            st_sc[...] = jnp.zeros_like(st_sc)

    def run(fwd):
        def head(hd, carry):
            o, st_new = _hgrn_head(q_ref[hd], pre_ref[hd], lb_ref[hd], v_ref[hd], st_sc[hd], fwd)
            o_ref[hd] = o.astype(BF16)
            st_sc[hd] = st_new
            return carry
        lax.fori_loop(0, NH_B, head, 0, unroll=8)

    @pl.when(d == 0)
    def _():
        run(True)

    @pl.when(d == 1)
    def _():
        run(False)

    @pl.when(c == pl.num_programs(2) - 1)
    def _():
        for hd in range(NH_B):
            s_out[0, 0, hd] = st_sc[hd].T


def _hgrn_scan(q, pre, v, lbd, *, row0, n_seq, seq_len, state=None):
    L = CHUNK_B
    nc = seq_len // L
    blk0 = row0 // L

    def loc_blk(b, d, c):
        return b * nc + c + d * (nc - 1 - 2 * c)

    def tok_blk(b, d, c):
        return blk0 + loc_blk(b, d, c)

    in_specs = [
        pl.BlockSpec((NH_B, L, DK_B), lambda b, d, c: (0, tok_blk(b, d, c), 0)),
        pl.BlockSpec((None, NH_B, L, DK_B), lambda b, d, c: (d, 0, tok_blk(b, d, c), 0)),
        pl.BlockSpec((NH_B, L, DK_B), lambda b, d, c: (0, tok_blk(b, d, c), 0)),
        pl.BlockSpec((None, NH_B, 1, DK_B), lambda b, d, c: (d, 0, 0, 0)),
    ]
    args = [q, pre, v, lbd]
    if state is not None:
        in_specs.append(pl.BlockSpec((1, 1, NH_B, DK_B, DK_B), lambda b, d, c: (b, d, 0, 0, 0)))
        args.append(state)
    return pl.pallas_call(
        functools.partial(_hgrn_scan_kernel, has_state=state is not None),
        name="hgrn_scan_seeded" if state is not None else "hgrn_scan",
        grid=(n_seq, 2, nc),
        in_specs=in_specs,
        out_specs=[
            pl.BlockSpec((None, NH_B, L, DK_B), lambda b, d, c: (d, 0, loc_blk(b, d, c), 0)),
            pl.BlockSpec((1, 1, NH_B, DK_B, DK_B), lambda b, d, c: (b, d, 0, 0, 0)),
        ],
        out_shape=[
            jax.ShapeDtypeStruct((2, NH_B, n_seq * seq_len, DK_B), BF16),
            jax.ShapeDtypeStruct((n_seq, 2, NH_B, DK_B, DK_B), F32),
        ],
        scratch_shapes=[pltpu.VMEM((NH_B, DK_B, DK_B), F32)],
        compiler_params=_params(("parallel", "parallel", "arbitrary")),
    )(*args)


def _out_b_kernel(op_ref, os_ref, sg_ref, nw_ref, w_ref, x_ref, mod_ref, lg_ref, lb_ref, out_ref, *, n_prompt_tiles):
    is_prompt = pl.program_id(0) < n_prompt_tiles
    parts = []
    for hd in range(NH_B):
        y = jnp.where(is_prompt, op_ref[0, hd].astype(F32) + op_ref[1, hd].astype(F32),
                      os_ref[0, hd].astype(F32) + os_ref[1, hd].astype(F32))
        parts.append(y * lax.rsqrt(jnp.mean(y * y, axis=-1, keepdims=True) + EPS))
    yn = jnp.concatenate(parts, axis=-1) * nw_ref[...] * sg_ref[...].astype(F32)
    out = jnp.dot(yn.astype(BF16), w_ref[...], preferred_element_type=F32)
    out_ref[...] = _layer_norm_rows(ALPHA * x_ref[...] + mod_ref[2:3, :] * out, lg_ref[...], lb_ref[...])


def _out_b(geo, o_prompt, o_sample, sg, norm_w, w_out, x, mod_l, ln_g, ln_b):
    t = geo.t
    tb = TOK_TILE_L
    n_p = geo.t_prompt // tb
    full = lambda shape: pl.BlockSpec(shape, lambda i: (0,) * len(shape))
    return pl.pallas_call(
        functools.partial(_out_b_kernel, n_prompt_tiles=n_p),
        name="out_b",
        grid=(t // tb,),
        in_specs=[
            pl.BlockSpec((2, NH_B, tb, DK_B), lambda i: (0, 0, jnp.minimum(i, n_p - 1), 0)),
            pl.BlockSpec((2, NH_B, tb, DK_B), lambda i: (0, 0, jnp.maximum(i - n_p, 0), 0)),
            pl.BlockSpec((tb, D), lambda i: (i, 0)),
            full((1, D)), full((D, D)),
            pl.BlockSpec((tb, D), lambda i: (i, 0)),
            pl.BlockSpec((None, 6, D), lambda i: (geo.cond_row(i, tb), 0, 0)),
            full((1, D)), full((1, D)),
        ],
        out_specs=pl.BlockSpec((tb, D), lambda i: (i, 0)),
        out_shape=jax.ShapeDtypeStruct((t, D), F32),
        compiler_params=_params(("parallel",)),
    )(o_prompt, o_sample, sg, norm_w.reshape(1, D).astype(F32), w_out.astype(BF16), x, mod_l,
      ln_g.reshape(1, D), ln_b.reshape(1, D))


N_STREAMS = 1
MOE_BLK = 1024
FFN_ROWS = MOE_BLK
U32 = jnp.uint32
ROW_WORDS = D // 2
CHUNK_W = 256
ROW_CHUNKS = ROW_WORDS // CHUNK_W
SC_WINDOW = 128


def _pack_rows(x):
    return pltpu.pack_elementwise([x[:, :ROW_WORDS], x[:, ROW_WORDS:]], packed_dtype=BF16)


def _unpack_rows(words):
    return jnp.concatenate([pltpu.unpack_elementwise(words, index=i, packed_dtype=BF16, unpacked_dtype=F32)
                            for i in range(2)], axis=1)


def _store_chunks(chunk_ref, x):
    words = _pack_rows(x)
    for c in range(ROW_CHUNKS):
        chunk_ref(c)[...] = words[:, c * CHUNK_W:(c + 1) * CHUNK_W]


def _load_chunks(chunk_ref, valid_rows=None):
    words = jnp.concatenate([chunk_ref(c)[...] for c in range(ROW_CHUNKS)], axis=1)
    if valid_rows is not None:
        row = lax.broadcasted_iota(jnp.int32, (words.shape[0], 1), 0)
        words = jnp.where(row < valid_rows, words, jnp.uint32(0))
    return _unpack_rows(words)


def _first_index(hit, iota, size, axis):
    return jnp.min(jnp.where(hit, iota, size), axis=axis, keepdims=True)


def _router_kernel(x_ref, mod_ref, wrt_ref, eb_ref, e_ref, w_ref, r_ref, cnt_ref, h_ref, cnt_sc):
    i = pl.program_id(0)
    tb = x_ref.shape[0]

    @pl.when(i == 0)
    def _():
        cnt_sc[...] = jnp.zeros_like(cnt_sc)

    h = x_ref[...] * (1.0 + mod_ref[4:5, :]) + mod_ref[3:4, :]
    _store_chunks(lambda c: h_ref.at[c], h)
    logits = lax.dot_general(wrt_ref[...], h, NT_DIMS, precision=HIGHEST, preferred_element_type=F32)
    scores = jax.nn.sigmoid(logits)
    sel = scores + eb_ref[...]

    g3 = sel.reshape(N_GROUPS, GROUP_SIZE, tb)
    io3 = lax.broadcasted_iota(jnp.int32, g3.shape, 1)
    m1 = jnp.max(g3, axis=1, keepdims=True)
    first = _first_index(g3 == m1, io3, GROUP_SIZE, 1)
    m2 = jnp.max(jnp.where(io3 == first, -jnp.inf, g3), axis=1, keepdims=True)
    gscore = (m1 + m2).reshape(N_GROUPS, tb)

    iog = lax.broadcasted_iota(jnp.int32, gscore.shape, 0)
    gmask = jnp.zeros(gscore.shape, F32)
    for _ in range(TOPK_GROUPS):
        gm = jnp.max(gscore, axis=0, keepdims=True)
        pick = iog == _first_index(gscore == gm, iog, N_GROUPS, 0)
        gmask = jnp.where(pick, 1.0, gmask)
        gscore = jnp.where(pick, -jnp.inf, gscore)
    emask = jnp.broadcast_to(gmask.reshape(N_GROUPS, 1, tb), (N_GROUPS, GROUP_SIZE, tb)).reshape(N_EXPERTS, tb)
    cand = jnp.where(emask > 0.0, sel, -jnp.inf)

    ioe = lax.broadcasted_iota(jnp.int32, cand.shape, 0)
    picks, wts = [], []
    onehot = jnp.zeros(cand.shape, F32)
    for _ in range(TOP_K):
        cm = jnp.max(cand, axis=0, keepdims=True)
        idx = _first_index(cand == cm, ioe, N_EXPERTS, 0)
        pick = ioe == idx
        picks.append(pick)
        wts.append(jnp.sum(jnp.where(pick, scores, 0.0), axis=0, keepdims=True))
        onehot = onehot + pick.astype(F32)
        cand = jnp.where(pick, -jnp.inf, cand)
        e_ref[pl.ds(len(picks) - 1, 1), :] = idx
    wsum = wts[0]
    for wk in wts[1:]:
        wsum = wsum + wk
    for k in range(TOP_K):
        w_ref[pl.ds(k, 1), :] = wts[k] / wsum * ROUTED_SCALE

    r_io = lax.broadcasted_iota(jnp.int32, (tb, tb), 0)
    c_io = lax.broadcasted_iota(jnp.int32, (tb, tb), 1)
    before = (r_io < c_io).astype(BF16)
    rank = cnt_sc[:, 0:1] + jnp.dot(onehot.astype(BF16), before, preferred_element_type=F32)
    for k in range(TOP_K):
        r_ref[pl.ds(k, 1), :] = jnp.sum(jnp.where(picks[k], rank, 0.0), axis=0, keepdims=True).astype(jnp.int32)
    cnt_sc[...] = cnt_sc[...] + jnp.sum(onehot, axis=1, keepdims=True)
    cnt_ref[...] = cnt_sc[...]


def _router(geo, x, mod_l, w_router, e_bias):
    t = geo.t
    tb = TOK_TILE_L
    full = lambda shape: pl.BlockSpec(shape, lambda i: (0,) * len(shape))
    e, w, r, cnt, h = pl.pallas_call(
        _router_kernel,
        name="router",
        grid=(t // tb,),
        in_specs=[
            pl.BlockSpec((tb, D), lambda i: (i, 0)),
            pl.BlockSpec((None, 6, D), lambda i: (geo.cond_row(i, tb), 0, 0)),
            full((N_EXPERTS, D)), full((N_EXPERTS, 1)),
        ],
        out_specs=[
            pl.BlockSpec((TOP_K, tb), lambda i: (0, i)),
            pl.BlockSpec((TOP_K, tb), lambda i: (0, i)),
            pl.BlockSpec((TOP_K, tb), lambda i: (0, i)),
            full((N_EXPERTS, LANES)),
            pl.BlockSpec((ROW_CHUNKS, tb, CHUNK_W), lambda i: (0, i, 0)),
        ],
        out_shape=[
            jax.ShapeDtypeStruct((TOP_K, t), jnp.int32),
            jax.ShapeDtypeStruct((TOP_K, t), F32),
            jax.ShapeDtypeStruct((TOP_K, t), jnp.int32),
            jax.ShapeDtypeStruct((N_EXPERTS, LANES), F32),
            jax.ShapeDtypeStruct((ROW_CHUNKS, t, CHUNK_W), U32),
        ],
        scratch_shapes=[pltpu.VMEM((N_EXPERTS, LANES), F32)],
        compiler_params=_params(("arbitrary",)),
    )(x, mod_l, w_router.T.astype(F32), e_bias.reshape(N_EXPERTS, 1).astype(F32))
    return e, w, r, cnt[:, 0].astype(jnp.int32), h


def _slot_kernel(pstart_ref, e_ref, r_ref, o_ref):
    e = e_ref[...]
    slot = r_ref[...]
    for x in range(N_EXPERTS):
        slot = slot + jnp.where(e == x, pstart_ref[x], 0)
    o_ref[...] = slot


def _slots(geo, pstart, top_e, rank):
    tb = math.gcd(SLOT_TILE, geo.t)
    return pl.pallas_call(
        _slot_kernel,
        name="slots",
        grid_spec=pltpu.PrefetchScalarGridSpec(
            num_scalar_prefetch=1,
            grid=(geo.t // tb,),
            in_specs=[pl.BlockSpec((TOP_K, tb), lambda i, p: (0, i)),
                      pl.BlockSpec((TOP_K, tb), lambda i, p: (0, i))],
            out_specs=pl.BlockSpec((TOP_K, tb), lambda i, p: (0, i)),
        ),
        out_shape=jax.ShapeDtypeStruct((TOP_K, geo.t), jnp.int32),
        compiler_params=_params(("parallel",)),
    )(pstart, top_e, rank)


def _block_meta_kernel(pstart_ref, counts_ref, pend_ref, e_ref, v_ref):
    row0 = lax.broadcasted_iota(jnp.int32, e_ref.shape, 1) * MOE_BLK
    blk_e = jnp.zeros(e_ref.shape, jnp.int32)
    for x in range(N_EXPERTS):
        blk_e = blk_e + jnp.where(pend_ref[x] <= row0, 1, 0)
    blk_e = jnp.minimum(blk_e, N_EXPERTS - 1)
    last = jnp.zeros(e_ref.shape, jnp.int32)
    for x in range(N_EXPERTS):
        last = last + jnp.where(blk_e == x, pstart_ref[x] + counts_ref[x], 0)
    e_ref[...] = blk_e
    v_ref[...] = jnp.clip(last - row0, 0, MOE_BLK)


def _block_meta(pstart, counts, pend, n_blocks):
    e, v = pl.pallas_call(
        _block_meta_kernel,
        name="block_meta",
        grid_spec=pltpu.PrefetchScalarGridSpec(
            num_scalar_prefetch=3,
            grid=(1,),
            in_specs=[],
            out_specs=[pl.BlockSpec((1, n_blocks), lambda i, a, b, c: (0, 0)),
                       pl.BlockSpec((1, n_blocks), lambda i, a, b, c: (0, 0))],
        ),
        out_shape=[jax.ShapeDtypeStruct((1, n_blocks), jnp.int32), jax.ShapeDtypeStruct((1, n_blocks), jnp.int32)],
        compiler_params=_params(("arbitrary",)),
    )(pstart, counts, pend)
    return e[0], v[0]


def _sc_mesh():
    return plsc.VectorSubcoreMesh(core_axis_name="core", subcore_axis_name="subcore")


def _sc_scatter(rows, idx, n_out, copies):
    n_src = rows.shape[0]
    n_idx = idx.shape[0]
    groups = ROW_CHUNKS
    win_per_group = n_src // groups // SC_WINDOW

    def idx_block(w, k):
        return (0, ((w // win_per_group) * copies + k) * win_per_group + w % win_per_group)

    @pl.kernel(out_type=jax.ShapeDtypeStruct((n_out, CHUNK_W), rows.dtype), mesh=_sc_mesh(), scratch_types=[],
               name="sc_dispatch")
    def scatter(x_hbm, i_hbm, o_hbm):
        def body(x_vmem, *i_vmems):
            for i_vmem in i_vmems:
                pltpu.sync_copy(x_vmem, o_hbm.at[i_vmem.at[0]])

        pltpu.emit_pipeline(
            body,
            grid=(n_src // SC_WINDOW,),
            in_specs=[pl.BlockSpec((SC_WINDOW, CHUNK_W), index_map=lambda w: (w, 0))]
            + [pl.BlockSpec((1, SC_WINDOW), index_map=functools.partial(idx_block, k=k)) for k in range(copies)],
            out_specs=[],
            core_axis_name=("core", "subcore"),
            dimension_semantics=(pltpu.PARALLEL,),
        )(x_hbm, *([i_hbm] * copies))

    return scatter(rows, idx.reshape(1, n_idx))


def _sc_gather(table, idx):
    n_idx = idx.shape[0]

    @pl.kernel(out_type=jax.ShapeDtypeStruct((n_idx, CHUNK_W), table.dtype), mesh=_sc_mesh(),
               name="sc_combine_gather")
    def gather(t_hbm, i_hbm, o_hbm):
        def body(i_vmem, o_vmem):
            pltpu.sync_copy(t_hbm.at[i_vmem.at[0]], o_vmem)

        pltpu.emit_pipeline(
            body,
            grid=(n_idx // SC_WINDOW,),
            in_specs=[pl.BlockSpec((1, SC_WINDOW), index_map=lambda w: (0, w))],
            out_specs=[pl.BlockSpec((SC_WINDOW, CHUNK_W), index_map=lambda w: (w, 0))],
            core_axis_name=("core", "subcore"),
            dimension_semantics=(pltpu.PARALLEL,),
        )(i_hbm, o_hbm)

    return gather(table, idx.reshape(1, n_idx))


def _ffn_kernel(blk_e_ref, blk_valid_ref, n_used_ref, xs_ref, wg_ref, wu_ref, wd_ref, y_ref, wg_sc, wu_sc, wd_sc):
    b = pl.program_id(0)
    used = b < n_used_ref[0]
    new_expert = (b == 0) | (blk_e_ref[b] != blk_e_ref[jnp.maximum(b - 1, 0)])

    @pl.when(used & new_expert)
    def _():
        wg_sc[...] = wg_ref[...].astype(BF16)
        wu_sc[...] = wu_ref[...].astype(BF16)
        wd_sc[...] = wd_ref[...].astype(BF16)

    @pl.when(used)
    def _():
        for r0 in range(0, MOE_BLK, FFN_ROWS):
            rows = pl.ds(r0, FFN_ROWS)
            x = _load_chunks(lambda c: xs_ref.at[c, rows], valid_rows=blk_valid_ref[b] - r0).astype(BF16)
            g = jnp.dot(x, wg_sc[...], preferred_element_type=F32)
            u = jnp.dot(x, wu_sc[...], preferred_element_type=F32)
            hmid = (g * jax.nn.sigmoid(g) * u).astype(BF16)
            _store_chunks(lambda c: y_ref.at[c, rows], jnp.dot(hmid, wd_sc[...], preferred_element_type=F32))

    @pl.when(jnp.logical_not(used))
    def _():
        y_ref[...] = jnp.zeros_like(y_ref)


def _ffn(xs, blk_e, blk_valid, n_used, layer, wg, wu, wd, n_blocks):
    def blk(b, be, bv, nu):
        return jnp.maximum(jnp.minimum(b, nu[0] - 1), 0)

    def w_idx(b, be, bv, nu):
        return (layer, be[blk(b, be, bv, nu)], 0, 0)

    return pl.pallas_call(
        _ffn_kernel,
        name="expert_ffn",
        grid_spec=pltpu.PrefetchScalarGridSpec(
            num_scalar_prefetch=3,
            grid=(n_blocks,),
            in_specs=[
                pl.BlockSpec((ROW_CHUNKS, MOE_BLK, CHUNK_W), lambda b, be, bv, nu: (0, blk(b, be, bv, nu), 0)),
                pl.BlockSpec((None, None, D, D_EXPERT), w_idx),
                pl.BlockSpec((None, None, D, D_EXPERT), w_idx),
                pl.BlockSpec((None, None, D_EXPERT, D), w_idx),
            ],
            out_specs=pl.BlockSpec((ROW_CHUNKS, MOE_BLK, CHUNK_W), lambda b, be, bv, nu: (0, b, 0)),
            scratch_shapes=[pltpu.VMEM((D, D_EXPERT), BF16), pltpu.VMEM((D, D_EXPERT), BF16),
                            pltpu.VMEM((D_EXPERT, D), BF16)],
        ),
        out_shape=jax.ShapeDtypeStruct(xs.shape, U32),
        compiler_params=_params(("arbitrary",)),
    )(blk_e, blk_valid, n_used, xs, wg, wu, wd)


def _combine_kernel(x_ref, mod_ref, wt_ref, y_ref, sg_ref, su_ref, sd_ref, lg_ref, lb_ref, *o_refs, n_prompt_tiles):
    x = x_ref[...]
    hb = (x * (1.0 + mod_ref[4:5, :]) + mod_ref[3:4, :]).astype(BF16)
    g = jnp.dot(hb, sg_ref[...], preferred_element_type=F32)
    u = jnp.dot(hb, su_ref[...], preferred_element_type=F32)
    ff = jnp.dot((g * jax.nn.sigmoid(g) * u).astype(BF16), sd_ref[...], preferred_element_type=F32)
    for k in range(TOP_K):
        ff = ff + _load_chunks(lambda c: y_ref.at[c, k]) * wt_ref[:, k:k + 1]
    out = _layer_norm_rows(ALPHA * x + mod_ref[5:6, :] * ff, lg_ref[...], lb_ref[...])
    if len(o_refs) == 1:
        o_refs[0][...] = out
    else:
        is_prompt = pl.program_id(0) < n_prompt_tiles

        @pl.when(is_prompt)
        def _():
            o_refs[0][...] = out

        @pl.when(jnp.logical_not(is_prompt))
        def _():
            o_refs[1][...] = out


def _combine(geo, x, mod_l, wt, ytok, sg, su, sd, ln_g, ln_b, split=False):
    tb = TOK_TILE_L
    n_p = geo.t_prompt // tb
    full = lambda shape: pl.BlockSpec(shape, lambda i: (0,) * len(shape))
    if split:
        out_specs = [pl.BlockSpec((tb, D), lambda i: (jnp.minimum(i, n_p - 1), 0)),
                     pl.BlockSpec((tb, D), lambda i: (jnp.maximum(i - n_p, 0), 0))]
        out_shape = [jax.ShapeDtypeStruct((geo.t_prompt, D), F32), jax.ShapeDtypeStruct((geo.t_sample, D), F32)]
    else:
        out_specs = pl.BlockSpec((tb, D), lambda i: (i, 0))
        out_shape = jax.ShapeDtypeStruct((geo.t, D), F32)
    return pl.pallas_call(
        functools.partial(_combine_kernel, n_prompt_tiles=n_p),
        name="combine",
        grid=(geo.t // tb,),
        in_specs=[
            pl.BlockSpec((tb, D), lambda i: (i, 0)),
            pl.BlockSpec((None, 6, D), lambda i: (geo.cond_row(i, tb), 0, 0)),
            pl.BlockSpec((tb, TOP_K), lambda i: (i, 0)),
            pl.BlockSpec((ROW_CHUNKS, TOP_K, tb, CHUNK_W), lambda i: (0, 0, i, 0)),
            full((D, D_EXPERT)), full((D, D_EXPERT)), full((D_EXPERT, D)), full((1, D)), full((1, D)),
        ],
        out_specs=out_specs,
        out_shape=out_shape,
        compiler_params=_params(("arbitrary",)),
    )(x, mod_l, wt, ytok, sg.astype(BF16), su.astype(BF16), sd.astype(BF16),
      ln_g.reshape(1, D), ln_b.reshape(1, D))


def _moe_stages(geo, x, mod_l, w_router, e_bias, layer, wg, wu, wd, sg, su, sd, ln_g, ln_b, split=False):
    t = geo.t
    top_e, w, rank, counts, h = _router(geo, x, mod_l, w_router, e_bias)
    n_blocks = (t * TOP_K) // MOE_BLK + N_EXPERTS
    n_rows = n_blocks * MOE_BLK
    padded = (counts + MOE_BLK - 1) // MOE_BLK * MOE_BLK
    pend = jnp.cumsum(padded)
    pstart = (pend - padded).astype(jnp.int32)
    blk_e, blk_valid = _block_meta(pstart, counts, pend.astype(jnp.int32), n_blocks)
    n_used = (pend[-1:] // MOE_BLK).astype(jnp.int32)
    slots = _slots(geo, pstart, top_e, rank)
    idx = (slots.reshape(1, TOP_K * t) + (jnp.arange(ROW_CHUNKS, dtype=jnp.int32) * n_rows)[:, None]).reshape(-1)
    xs = _sc_scatter(h.reshape(ROW_CHUNKS * t, CHUNK_W), idx, ROW_CHUNKS * n_rows, TOP_K)
    yield
    yb = _ffn(xs.reshape(ROW_CHUNKS, n_rows, CHUNK_W), blk_e, blk_valid, n_used, layer, wg, wu, wd, n_blocks)
    ytok = _sc_gather(yb.reshape(ROW_CHUNKS * n_rows, CHUNK_W), idx)
    yield
    return _combine(geo, x, mod_l, w.T, ytok.reshape(ROW_CHUNKS, TOP_K, t, CHUNK_W), sg, su, sd, ln_g, ln_b,
                    split=split)


def _run_interleaved(gens):
    results = [None] * len(gens)
    live = list(range(len(gens)))
    while live:
        for i in list(live):
            try:
                next(gens[i])
            except StopIteration as stop:
                results[i] = stop.value
                live.remove(i)
    return results


def _moe_layer(*args, **kwargs):
    return _run_interleaved([_moe_stages(*args, **kwargs)])[0]


def _pos_embed(rows):
    quarter = D // 4
    omega = 1.0 / (POS_BASE ** (jnp.arange(quarter, dtype=F32) / quarter))
    r, col = jnp.meshgrid(jnp.arange(rows, dtype=F32), jnp.arange(GRID_W, dtype=F32), indexing='ij')
    r = r.reshape(-1, 1) * omega
    col = col.reshape(-1, 1) * omega
    return jnp.concatenate([jnp.sin(r), jnp.cos(r), jnp.sin(col), jnp.cos(col)], axis=-1)


def _mlstm_layer(geo, x, mod_l, j, a_w_in, a_b_gates, a_norm, a_w_out, ln_g, ln_b,
                 state_C, state_n, state_m):
    q, kt, v, so, gr = _proj_a(geo, x, mod_l, a_w_in[j], a_b_gates[j])
    hp, c_p, n_p, m_p = _mlstm_scan(q, kt, v, gr, row0=0, n_seq=geo.n_prompt, seq_len=geo.prompt_len)
    ns = geo.n_sample
    n0 = jnp.pad(state_n[:, j].astype(F32)[..., None], ((0, 0),) * 4 + ((0, LANES - 1),))
    m0 = jnp.pad(state_m[:, j].astype(F32), ((0, 0), (0, 0), (0, SUBLANES - NH_A)))
    m0 = jnp.broadcast_to(m0[..., None], (ns, 2, SUBLANES, LANES))
    hs, _, _, _ = _mlstm_scan(q, kt, v, gr, row0=geo.t_prompt, n_seq=ns, seq_len=geo.sample_len,
                              state=(state_C[:, j].astype(F32), n0, m0))
    x1 = _out_a(geo, hp, hs, so, a_norm[j], a_w_out[j], x, mod_l, ln_g, ln_b)
    return x1, c_p, n_p[..., 0], m_p[:, :, :NH_A, 0]


def _hgrn_layer(geo, x, mod_l, j, lb_layer, b_w_in, b_norm, b_w_out, ln_g, ln_b, state_S):
    q, pre, v, sg = _proj_b(geo, x, mod_l, b_w_in[j])
    lbd = lb_layer.reshape(2, NH_B, 1, DK_B)
    op, s_p = _hgrn_scan(q, pre, v, lbd, row0=0, n_seq=geo.n_prompt, seq_len=geo.prompt_len)
    os_, _ = _hgrn_scan(q, pre, v, lbd, row0=geo.t_prompt, n_seq=geo.n_sample, seq_len=geo.sample_len,
                        state=state_S[:, j].astype(F32))
    x1 = _out_b(geo, op, os_, sg, b_norm[j], b_w_out[j], x, mod_l, ln_g, ln_b)
    return x1, s_p


def kernel(x_prompt, x_sample, state_mlstm_C, state_mlstm_n, state_mlstm_m, state_hgrn_S, c, c_ctx, w_mod, b_mod, ln_g, ln_b, a_w_in, a_b_gates, a_norm, a_w_out, b_w_in, b_lb, b_norm, b_w_out, w_router, e_bias, w_gate, w_up, w_down, ws_gate, ws_up, ws_down):
    bp, sp, _ = x_prompt.shape
    bs, ss, _ = x_sample.shape
    cond = jnp.zeros((COND_ROWS, D), F32).at[0].set(c_ctx).at[1:1 + bs].set(c)
    mod = _modulation(cond, w_mod, b_mod)
    x = (x_prompt.reshape(-1, D), x_sample.reshape(-1, D), _pos_embed(ss // GRID_W))
    sm = jax.nn.softmax(b_lb.astype(F32), axis=0)
    lb_all = jnp.cumsum(sm, axis=0) - sm[0]

    n_streams = N_STREAMS if bp % N_STREAMS == 0 and bs % N_STREAMS == 0 else 1
    pb, sb = bp // n_streams, bs // n_streams

    def stream(s):
        geo = Geometry(pb, sp, sb, ss, prompt0=s * pb, sample0=s * sb)
        rows = [0] + [1 + s * sb + b for b in range(sb)]
        mod_s = mod[:, jnp.array(rows + [0] * (COND_ROWS - len(rows)), jnp.int32)]
        seqs = slice(s * sb, (s + 1) * sb)
        x1, new_c, new_n, new_m = _mlstm_layer(geo, x, mod_s[0], 0, a_w_in, a_b_gates, a_norm, a_w_out,
                                               ln_g[0, 0], ln_b[0, 0], state_mlstm_C[seqs], state_mlstm_n[seqs],
                                               state_mlstm_m[seqs])
        x2 = yield from _moe_stages(geo, x1, mod_s[0], w_router[0], e_bias[0], 0, w_gate, w_up, w_down, ws_gate[0],
                                    ws_up[0], ws_down[0], ln_g[0, 1], ln_b[0, 1])
        x3, new_s = _hgrn_layer(geo, x2, mod_s[1], 0, lb_all[1], b_w_in, b_norm, b_w_out, ln_g[1, 0], ln_b[1, 0],
                                state_hgrn_S[seqs])
        y_p, y_s = yield from _moe_stages(geo, x3, mod_s[1], w_router[1], e_bias[1], 1, w_gate, w_up, w_down,
                                          ws_gate[1], ws_up[1], ws_down[1], ln_g[1, 1], ln_b[1, 1], split=True)
        return y_p.reshape(pb, sp, D), y_s.reshape(sb, ss, D), new_c[:, None], new_n[:, None], new_m[:, None], \
            new_s[:, None]

    outs = _run_interleaved([stream(s) for s in range(n_streams)])
    return tuple(jnp.concatenate(leaf, axis=0) for leaf in zip(*outs))
```

```python
import functools
import math

import jax
import jax.numpy as jnp
from jax import lax
from jax.experimental import pallas as pl
from jax.experimental.pallas import tpu as pltpu
from jax.experimental.pallas import tpu_sc as plsc

F32 = jnp.float32
BF16 = jnp.bfloat16
HIGHEST = lax.Precision.HIGHEST

D = 1024
DEPTH = 2
GRID_W = 64
POS_BASE = 10000.0
EPS = 1e-6
ALPHA = (2.0 * DEPTH) ** 0.25
NH_A, DK_A, DV_A = 4, 128, 256
QK_A, V_A = NH_A * DK_A, NH_A * DV_A
NH_B, DK_B = 8, 128
N_EXPERTS, TOP_K, N_GROUPS, TOPK_GROUPS = 64, 8, 8, 4
GROUP_SIZE = N_EXPERTS // N_GROUPS
D_EXPERT = D // 4
ROUTED_SCALE = 2.5

LANES = 128
SUBLANES = 8
COND_ROWS = 8
TOK_TILE = 256
TOK_TILE_L = 512
SLOT_TILE = 2048
CHUNK_A = 256
VMEM_LIMIT = 56 * 1024 * 1024

NT_DIMS = (((1,), (1,)), ((), ()))


def _params(sem):
    return pltpu.CompilerParams(dimension_semantics=sem, vmem_limit_bytes=VMEM_LIMIT)


def _split3(x):
    hi = x.astype(BF16)
    r = x - hi.astype(F32)
    mid = r.astype(BF16)
    lo = (r - mid.astype(F32)).astype(BF16)
    return hi, mid, lo


def _dot3(a_bf, x, transpose_side=None):
    hi, mid, lo = _split3(x)
    return (jnp.dot(a_bf, hi, preferred_element_type=F32)
            + jnp.dot(a_bf, mid, preferred_element_type=F32)
            + jnp.dot(a_bf, lo, preferred_element_type=F32))


def _dot3_r(x, a_bf):
    hi, mid, lo = _split3(x)
    return (jnp.dot(hi, a_bf, preferred_element_type=F32)
            + jnp.dot(mid, a_bf, preferred_element_type=F32)
            + jnp.dot(lo, a_bf, preferred_element_type=F32))


def _log_sigmoid(x):
    return jnp.minimum(x, 0.0) - jnp.log1p(jnp.exp(-jnp.abs(x)))


def _layer_norm_rows(x, g, b):
    mu = jnp.mean(x, axis=-1, keepdims=True)
    xc = x - mu
    var = jnp.mean(xc * xc, axis=-1, keepdims=True)
    return xc * lax.rsqrt(var + EPS) * g + b


class Geometry:
    def __init__(self, n_prompt, prompt_len, n_sample, sample_len, prompt0=0, sample0=0):
        self.n_prompt, self.prompt_len = n_prompt, prompt_len
        self.n_sample, self.sample_len = n_sample, sample_len
        self.prompt0, self.sample0 = prompt0, sample0
        self.t_prompt = n_prompt * prompt_len
        self.t_sample = n_sample * sample_len
        self.t = self.t_prompt + self.t_sample
        assert self.t_prompt % TOK_TILE_L == 0 and sample_len % TOK_TILE_L == 0
        assert (prompt0 * prompt_len) % TOK_TILE_L == 0
        assert n_sample + 1 <= COND_ROWS

    def cond_row(self, tile, tile_rows):
        n_p = self.t_prompt // tile_rows
        return jnp.where(tile < n_p, 0, 1 + (tile - n_p) // (self.sample_len // tile_rows))


def _mod_kernel(cond_ref, w_ref, b_ref, o_ref):
    c = cond_ref[...]
    s = c * jax.nn.sigmoid(c)
    o_ref[0, 0] = jnp.dot(s, w_ref[0], precision=HIGHEST, preferred_element_type=F32) + b_ref[0, 0]


def _modulation(cond, w_mod, b_mod):
    out = pl.pallas_call(
        _mod_kernel,
        name="modulation",
        grid=(DEPTH, 6),
        in_specs=[
            pl.BlockSpec((COND_ROWS, D), lambda l, j: (0, 0)),
            pl.BlockSpec((1, D, D), lambda l, j: (l, 0, j)),
            pl.BlockSpec((1, 1, 1, D), lambda l, j: (l, j, 0, 0)),
        ],
        out_specs=pl.BlockSpec((1, 1, COND_ROWS, D), lambda l, j: (l, j, 0, 0)),
        out_shape=jax.ShapeDtypeStruct((DEPTH, 6, COND_ROWS, D), F32),
        compiler_params=_params(("arbitrary", "arbitrary")),
    )(cond, w_mod, b_mod.reshape(DEPTH, 6, 1, D))
    return out.transpose(0, 2, 1, 3)


def _embed_specs(geo, tb):
    n_p = geo.t_prompt // tb
    per_seq = geo.sample_len // tb
    p0 = geo.prompt0 * geo.prompt_len // tb
    s0 = geo.sample0 * per_seq
    return [pl.BlockSpec((tb, D), lambda i: (p0 + jnp.minimum(i, n_p - 1), 0)),
            pl.BlockSpec((tb, D), lambda i: (s0 + jnp.maximum(i - n_p, 0), 0)),
            pl.BlockSpec((tb, D), lambda i: (jnp.maximum(i - n_p, 0) % per_seq, 0))]


def _embed_tile(xp_ref, xs_ref, pos_ref, n_prompt_tiles):
    return jnp.where(pl.program_id(0) < n_prompt_tiles, xp_ref[...], xs_ref[...] + pos_ref[...])


def _proj_a_kernel(xp_ref, xs_ref, pos_ref, mod_ref, wq_ref, wkt_ref, wvo_ref, wgt_ref, bgt_ref,
                   q_ref, kt_ref, v_ref, so_ref, gr_ref, *, n_prompt_tiles):
    x = _embed_tile(xp_ref, xs_ref, pos_ref, n_prompt_tiles)
    h = x * (1.0 + mod_ref[1:2, :]) + mod_ref[0:1, :]
    hb = h.astype(BF16)
    q_ref[...] = jnp.dot(hb, wq_ref[...], preferred_element_type=F32).astype(BF16)
    kt = lax.dot_general(wkt_ref[...], hb, NT_DIMS, preferred_element_type=F32)
    kt_ref[...] = (kt * (DK_A ** -0.5)).astype(BF16)
    vo = jnp.dot(hb, wvo_ref[...], preferred_element_type=F32)
    v_ref[...] = vo[:, :V_A].astype(BF16)
    so_ref[...] = jax.nn.sigmoid(vo[:, V_A:]).astype(BF16)
    gr_ref[...] = lax.dot_general(wgt_ref[...], h, NT_DIMS, precision=HIGHEST,
                                  preferred_element_type=F32) + bgt_ref[...]


def _proj_a(geo, x, mod_l, w_in, b_gates):
    t = geo.t
    n_gate = 4 * NH_A
    wq = w_in[:, :QK_A].astype(BF16)
    wkt = w_in[:, QK_A:2 * QK_A].T.astype(BF16)
    wvo = w_in[:, 2 * QK_A:2 * QK_A + 2 * V_A].astype(BF16)
    wg = w_in[:, 2 * QK_A + 2 * V_A:]
    bg = b_gates.reshape(n_gate).astype(F32)
    tb = TOK_TILE_L
    full = lambda shape: pl.BlockSpec(shape, lambda i: (0,) * len(shape))
    return pl.pallas_call(
        functools.partial(_proj_a_kernel, n_prompt_tiles=geo.t_prompt // tb),
        name="proj_a",
        grid=(t // tb,),
        in_specs=_embed_specs(geo, tb) + [
            pl.BlockSpec((None, 6, D), lambda i: (geo.cond_row(i, tb), 0, 0)),
            full((D, QK_A)), full((QK_A, D)), full((D, 2 * V_A)), full((n_gate, D)), full((n_gate, 1)),
        ],
        out_specs=[
            pl.BlockSpec((tb, QK_A), lambda i: (i, 0)),
            pl.BlockSpec((QK_A, tb), lambda i: (0, i)),
            pl.BlockSpec((tb, V_A), lambda i: (i, 0)),
            pl.BlockSpec((tb, V_A), lambda i: (i, 0)),
            pl.BlockSpec((n_gate, tb), lambda i: (0, i)),
        ],
        out_shape=[
            jax.ShapeDtypeStruct((t, QK_A), BF16),
            jax.ShapeDtypeStruct((QK_A, t), BF16),
            jax.ShapeDtypeStruct((t, V_A), BF16),
            jax.ShapeDtypeStruct((t, V_A), BF16),
            jax.ShapeDtypeStruct((n_gate, t), F32),
        ],
        compiler_params=_params(("parallel",)),
    )(*x, mod_l, wq, wkt, wvo, wg.T, bg.reshape(n_gate, 1))


def _mlstm_scan_kernel(*refs, chunk, has_state):
    if has_state:
        (q_ref, kt_ref, v_ref, gr_ref, c0_ref, n0_ref, m0_ref,
         h_ref, c_out, n_out, m_out, c_sc, n_sc, m_sc) = refs
    else:
        (q_ref, kt_ref, v_ref, gr_ref,
         h_ref, c_out, n_out, m_out, c_sc, n_sc, m_sc) = refs
    L = chunk
    d = pl.program_id(1)
    c = pl.program_id(2)
    fwd = d == 0

    @pl.when(c == 0)
    def _():
        if has_state:
            c_sc[...] = c0_ref[0, 0]
            n_sc[...] = n0_ref[0, 0]
            m_sc[...] = m0_ref[0, 0]
        else:
            c_sc[...] = jnp.zeros_like(c_sc)
            n_sc[...] = jnp.zeros_like(n_sc)
            m_sc[...] = jnp.zeros_like(m_sc)

    row = lax.broadcasted_iota(jnp.int32, (L, L), 0)
    col = lax.broadcasted_iota(jnp.int32, (L, L), 1)
    sgn = 1 - 2 * d
    causal = (row - col) * sgn >= 0
    tri_t = ((col - row) * sgn >= 0).astype(BF16)

    gr = gr_ref[...]
    br_all = _dot3_r(_log_sigmoid(gr), tri_t)
    bc_all = jnp.concatenate([br_all, jnp.zeros((LANES - br_all.shape[0], L), F32)], axis=0).T
    ones_blk = (lax.broadcasted_iota(jnp.int32, (L, LANES), 1) == 0).astype(BF16)

    for h in range(NH_A):
        b_c = jnp.where(fwd, bc_all[:, 4 + h:5 + h], bc_all[:, 12 + h:13 + h])
        b_r = jnp.where(fwd, br_all[4 + h:5 + h, :], br_all[12 + h:13 + h, :])
        i_r = jnp.where(fwd, gr[h:h + 1, :], gr[8 + h:9 + h, :])
        bl = jnp.where(fwd, b_r[:, L - 1:L], b_r[:, 0:1])
        q = q_ref[:, h * DK_A:(h + 1) * DK_A]
        kt = kt_ref[h * DK_A:(h + 1) * DK_A, :]
        v = v_ref[:, h * DV_A:(h + 1) * DV_A]
        m = m_sc[h:h + 1, 0:1]
        cst = c_sc[h]
        nst = n_sc[h]

        a_r = i_r - b_r
        logd = jnp.where(causal, b_c + a_r, -jnp.inf)
        inter = b_c + m
        m_t = jnp.maximum(inter, jnp.max(logd, axis=1, keepdims=True))
        dmat = jnp.exp(logd - m_t)
        e_int = jnp.exp(inter - m_t)
        s = (jnp.dot(q, kt, preferred_element_type=F32) * dmat).astype(BF16)
        num = (jnp.dot(s, v, preferred_element_type=F32)
               + e_int * jnp.dot(q, cst.astype(BF16), preferred_element_type=F32))
        den = (jnp.dot(s, ones_blk, preferred_element_type=F32)
               + e_int * jnp.dot(q, nst.astype(BF16), preferred_element_type=F32))[:, 0:1]
        h_ref[:, h * DV_A:(h + 1) * DV_A] = (num / jnp.maximum(jnp.abs(den), jnp.exp(-m_t))).astype(BF16)

        logw = bl + a_r
        m_new = jnp.maximum(bl + m, jnp.max(logw, axis=1, keepdims=True))
        w = jnp.exp(logw - m_new)
        decay = jnp.exp(bl + m - m_new)
        kw = (kt.astype(F32) * w).astype(BF16)
        c_sc[h] = decay * cst + jnp.dot(kw, v, preferred_element_type=F32)
        n_sc[h] = decay * nst + jnp.dot(kw, ones_blk, preferred_element_type=F32)
        m_sc[h:h + 1, :] = jnp.broadcast_to(m_new, (1, LANES))

    @pl.when(c == pl.num_programs(2) - 1)
    def _():
        c_out[0, 0] = c_sc[...]
        n_out[0, 0] = n_sc[...]
        m_out[0, 0] = m_sc[...]


def _mlstm_scan(q, kt, v, gr, *, row0, n_seq, seq_len, state=None):
    L = CHUNK_A
    nc = seq_len // L
    blk0 = row0 // L

    def loc_blk(b, d, c):
        return b * nc + c + d * (nc - 1 - 2 * c)

    def tok_blk(b, d, c):
        return blk0 + loc_blk(b, d, c)

    in_specs = [
        pl.BlockSpec((L, QK_A), lambda b, d, c: (tok_blk(b, d, c), 0)),
        pl.BlockSpec((QK_A, L), lambda b, d, c: (0, tok_blk(b, d, c))),
        pl.BlockSpec((L, V_A), lambda b, d, c: (tok_blk(b, d, c), 0)),
        pl.BlockSpec((4 * NH_A, L), lambda b, d, c: (0, tok_blk(b, d, c))),
    ]
    args = [q, kt, v, gr]
    if state is not None:
        in_specs += [
            pl.BlockSpec((1, 1, NH_A, DK_A, DV_A), lambda b, d, c: (b, d, 0, 0, 0)),
            pl.BlockSpec((1, 1, NH_A, DK_A, LANES), lambda b, d, c: (b, d, 0, 0, 0)),
            pl.BlockSpec((1, 1, SUBLANES, LANES), lambda b, d, c: (b, d, 0, 0)),
        ]
        args += list(state)
    return pl.pallas_call(
        functools.partial(_mlstm_scan_kernel, chunk=L, has_state=state is not None),
        name="mlstm_scan_seeded" if state is not None else "mlstm_scan",
        grid=(n_seq, 2, nc),
        in_specs=in_specs,
        out_specs=[
            pl.BlockSpec((None, L, V_A), lambda b, d, c: (d, loc_blk(b, d, c), 0)),
            pl.BlockSpec((1, 1, NH_A, DK_A, DV_A), lambda b, d, c: (b, d, 0, 0, 0)),
            pl.BlockSpec((1, 1, NH_A, DK_A, LANES), lambda b, d, c: (b, d, 0, 0, 0)),
            pl.BlockSpec((1, 1, SUBLANES, LANES), lambda b, d, c: (b, d, 0, 0)),
        ],
        out_shape=[
            jax.ShapeDtypeStruct((2, n_seq * seq_len, V_A), BF16),
            jax.ShapeDtypeStruct((n_seq, 2, NH_A, DK_A, DV_A), F32),
            jax.ShapeDtypeStruct((n_seq, 2, NH_A, DK_A, LANES), F32),
            jax.ShapeDtypeStruct((n_seq, 2, SUBLANES, LANES), F32),
        ],
        scratch_shapes=[
            pltpu.VMEM((NH_A, DK_A, DV_A), F32),
            pltpu.VMEM((NH_A, DK_A, LANES), F32),
            pltpu.VMEM((SUBLANES, LANES), F32),
        ],
        compiler_params=_params(("parallel", "parallel", "arbitrary")),
    )(*args)


def _out_a_kernel(hp_ref, hs_ref, so_ref, nw_ref, w_ref, xp_ref, xs_ref, pos_ref, mod_ref, lg_ref, lb_ref, o_ref, *,
                  n_prompt_tiles):
    is_prompt = pl.program_id(0) < n_prompt_tiles
    x = _embed_tile(xp_ref, xs_ref, pos_ref, n_prompt_tiles)
    y = jnp.where(is_prompt, hp_ref[0].astype(F32) + hp_ref[1].astype(F32),
                  hs_ref[0].astype(F32) + hs_ref[1].astype(F32))
    parts = []
    for h in range(NH_A):
        yh = y[:, h * DV_A:(h + 1) * DV_A]
        mu = jnp.mean(yh, axis=-1, keepdims=True)
        yc = yh - mu
        var = jnp.mean(yc * yc, axis=-1, keepdims=True)
        parts.append(yc * lax.rsqrt(var + EPS))
    yn = jnp.concatenate(parts, axis=-1) * nw_ref[...] * so_ref[...].astype(F32)
    out = jnp.dot(yn.astype(BF16), w_ref[...], preferred_element_type=F32)
    o_ref[...] = _layer_norm_rows(ALPHA * x + mod_ref[2:3, :] * out, lg_ref[...], lb_ref[...])


def _out_a(geo, h_prompt, h_sample, so, norm_w, w_out, x, mod_l, ln_g, ln_b):
    t = geo.t
    tb = TOK_TILE_L
    n_p = geo.t_prompt // tb
    full = lambda shape: pl.BlockSpec(shape, lambda i: (0,) * len(shape))
    return pl.pallas_call(
        functools.partial(_out_a_kernel, n_prompt_tiles=n_p),
        name="out_a",
        grid=(t // tb,),
        in_specs=[
            pl.BlockSpec((2, tb, V_A), lambda i: (0, jnp.minimum(i, n_p - 1), 0)),
            pl.BlockSpec((2, tb, V_A), lambda i: (0, jnp.maximum(i - n_p, 0), 0)),
            pl.BlockSpec((tb, V_A), lambda i: (i, 0)),
            full((1, V_A)), full((V_A, D)),
        ] + _embed_specs(geo, tb) + [
            pl.BlockSpec((None, 6, D), lambda i: (geo.cond_row(i, tb), 0, 0)),
            full((1, D)), full((1, D)),
        ],
        out_specs=pl.BlockSpec((tb, D), lambda i: (i, 0)),
        out_shape=jax.ShapeDtypeStruct((t, D), F32),
        compiler_params=_params(("parallel",)),
    )(h_prompt, h_sample, so, norm_w.reshape(1, V_A).astype(F32), w_out.astype(BF16), *x, mod_l,
      ln_g.reshape(1, D), ln_b.reshape(1, D))


def _proj_b_kernel(x_ref, mod_ref, w_ref, q_ref, pre_ref, v_ref, sg_ref):
    h = x_ref[...] * (1.0 + mod_ref[1:2, :]) + mod_ref[0:1, :]
    z = jnp.dot(h.astype(BF16), w_ref[...], preferred_element_type=F32)
    for hd in range(NH_B):
        lo = hd * DK_B
        qh = z[:, lo:lo + DK_B]
        q_ref[hd] = qh * jax.nn.sigmoid(qh)
        pre_ref[0, hd] = z[:, D + lo:D + lo + DK_B]
        pre_ref[1, hd] = z[:, 2 * D + lo:2 * D + lo + DK_B]
        v_ref[hd] = z[:, 3 * D + lo:3 * D + lo + DK_B].astype(BF16)
    g = z[:, 4 * D:]
    sg_ref[...] = (g * jax.nn.sigmoid(g)).astype(BF16)


def _proj_b(geo, x, mod_l, w_in):
    t = geo.t
    tb = TOK_TILE
    return pl.pallas_call(
        _proj_b_kernel,
        name="proj_b",
        grid=(t // tb,),
        in_specs=[
            pl.BlockSpec((tb, D), lambda i: (i, 0)),
            pl.BlockSpec((None, 6, D), lambda i: (geo.cond_row(i, tb), 0, 0)),
            pl.BlockSpec((D, 5 * D), lambda i: (0, 0)),
        ],
        out_specs=[
            pl.BlockSpec((NH_B, tb, DK_B), lambda i: (0, i, 0)),
            pl.BlockSpec((2, NH_B, tb, DK_B), lambda i: (0, 0, i, 0)),
            pl.BlockSpec((NH_B, tb, DK_B), lambda i: (0, i, 0)),
            pl.BlockSpec((tb, D), lambda i: (i, 0)),
        ],
        out_shape=[
            jax.ShapeDtypeStruct((NH_B, t, DK_B), F32),
            jax.ShapeDtypeStruct((2, NH_B, t, DK_B), F32),
            jax.ShapeDtypeStruct((NH_B, t, DK_B), BF16),
            jax.ShapeDtypeStruct((t, D), BF16),
        ],
        compiler_params=_params(("parallel",)),
    )(x, mod_l, w_in.astype(BF16))


CHUNK_B = 128
BAND = SUBLANES // 2
TN_DIMS = (((0,), (0,)), ((), ()))


def _hgrn_head(q, pre, lbv, v_bf, st, fwd):
    L = q.shape[0]
    sg = jax.nn.sigmoid(pre)
    f = lbv + (1.0 - lbv) * sg
    lf = jnp.log(f)
    kk = (1.0 - lbv) * (1.0 - sg)
    row = lax.broadcasted_iota(jnp.int32, (L, L), 0)
    col = lax.broadcasted_iota(jnp.int32, (L, L), 1)
    tri = ((row >= col) if fwd else (row <= col)).astype(BF16)
    b = _dot3(tri, lf)
    tpos = lax.broadcasted_iota(jnp.int32, (L, DK_B), 0)
    blk_bits = row ^ col
    lag = jnp.where(blk_bits < BAND, (row - col) if fwd else (col - row), -1)

    step = 1 if fwd else L - 1
    att = jnp.where(lag == 0, jnp.sum(q * kk, axis=1, keepdims=True), 0.0)
    f_r, kk_r, g = f, kk, f
    for dl in range(1, BAND):
        if dl > 1:
            f_r = pltpu.roll(f_r, step, 0)
            g = g * f_r
        kk_r = pltpu.roll(kk_r, step, 0)
        att = jnp.where(lag == dl, jnp.sum(q * kk_r * g, axis=1, keepdims=True), att)

    w = BAND
    while w < L:
        nb = L // (2 * w)
        b3 = b.reshape(nb, 2 * w, DK_B)
        edge = (b3[:, w - 1:w, :] if fwd else b3[:, w:w + 1, :])
        bmid = jnp.broadcast_to(edge, (nb, 2 * w, DK_B)).reshape(L, DK_B)
        second = (tpos & w) != 0
        t_side = second if fwd else jnp.logical_not(second)
        e = jnp.exp(jnp.where(t_side, b - bmid, bmid - b))
        qt = jnp.where(t_side, q * e, 0.0).astype(BF16)
        ks = jnp.where(t_side, 0.0, kk * e).astype(BF16)
        a = lax.dot_general(qt, ks, NT_DIMS, preferred_element_type=F32)
        att = att + jnp.where(blk_bits < 2 * w, a, 0.0)
        w *= 2
    o = jnp.dot(att.astype(BF16), v_bf, preferred_element_type=F32)

    bl = b[L - 1:L, :] if fwd else b[0:1, :]
    o = o + lax.dot_general((q * jnp.exp(b)).astype(BF16), st.astype(BF16), NT_DIMS, preferred_element_type=F32)
    kd = (kk * jnp.exp(bl - b)).astype(BF16)
    st_new = jnp.exp(bl) * st + lax.dot_general(v_bf, kd, TN_DIMS, preferred_element_type=F32)
    return o, st_new


def _hgrn_scan_kernel(*refs, has_state):
    if has_state:
        q_ref, pre_ref, v_ref, lb_ref, s0_ref, o_ref, s_out, st_sc = refs
    else:
        q_ref, pre_ref, v_ref, lb_ref, o_ref, s_out, st_sc = refs
    d = pl.program_id(1)
    c = pl.program_id(2)

    @pl.when(c == 0)
    def _():
        if has_state:
            for hd in range(NH_B):
                st_sc[hd] = s0_ref[0, 0, hd].T
        else:
            st_sc[...] = jnp.zeros_like(st_sc)

    def run(fwd):
        def head(hd, carry):
            o, st_new = _hgrn_head(q_ref[hd], pre_ref[hd], lb_ref[hd], v_ref[hd], st_sc[hd], fwd)
            o_ref[hd] = o.astype(BF16)
            st_sc[hd] = st_new
            return carry
        lax.fori_loop(0, NH_B, head, 0, unroll=8)

    @pl.when(d == 0)
    def _():
        run(True)

    @pl.when(d == 1)
    def _():
        run(False)

    @pl.when(c == pl.num_programs(2) - 1)
    def _():
        for hd in range(NH_B):
            s_out[0, 0, hd] = st_sc[hd].T


def _hgrn_scan(q, pre, v, lbd, *, row0, n_seq, seq_len, state=None):
    L = CHUNK_B
    nc = seq_len // L
    blk0 = row0 // L

    def loc_blk(b, d, c):
        return b * nc + c + d * (nc - 1 - 2 * c)

    def tok_blk(b, d, c):
        return blk0 + loc_blk(b, d, c)

    in_specs = [
        pl.BlockSpec((NH_B, L, DK_B), lambda b, d, c: (0, tok_blk(b, d, c), 0)),
        pl.BlockSpec((None, NH_B, L, DK_B), lambda b, d, c: (d, 0, tok_blk(b, d, c), 0)),
        pl.BlockSpec((NH_B, L, DK_B), lambda b, d, c: (0, tok_blk(b, d, c), 0)),
        pl.BlockSpec((None, NH_B, 1, DK_B), lambda b, d, c: (d, 0, 0, 0)),
    ]
    args = [q, pre, v, lbd]
    if state is not None:
        in_specs.append(pl.BlockSpec((1, 1, NH_B, DK_B, DK_B), lambda b, d, c: (b, d, 0, 0, 0)))
        args.append(state)
    return pl.pallas_call(
        functools.partial(_hgrn_scan_kernel, has_state=state is not None),
        name="hgrn_scan_seeded" if state is not None else "hgrn_scan",
        grid=(n_seq, 2, nc),
        in_specs=in_specs,
        out_specs=[
            pl.BlockSpec((None, NH_B, L, DK_B), lambda b, d, c: (d, 0, loc_blk(b, d, c), 0)),
            pl.BlockSpec((1, 1, NH_B, DK_B, DK_B), lambda b, d, c: (b, d, 0, 0, 0)),
        ],
        out_shape=[
            jax.ShapeDtypeStruct((2, NH_B, n_seq * seq_len, DK_B), BF16),
            jax.ShapeDtypeStruct((n_seq, 2, NH_B, DK_B, DK_B), F32),
        ],
        scratch_shapes=[pltpu.VMEM((NH_B, DK_B, DK_B), F32)],
        compiler_params=_params(("parallel", "parallel", "arbitrary")),
    )(*args)


def _out_b_kernel(op_ref, os_ref, sg_ref, nw_ref, w_ref, x_ref, mod_ref, lg_ref, lb_ref, out_ref, *, n_prompt_tiles):
    is_prompt = pl.program_id(0) < n_prompt_tiles
    parts = []
    for hd in range(NH_B):
        y = jnp.where(is_prompt, op_ref[0, hd].astype(F32) + op_ref[1, hd].astype(F32),
                      os_ref[0, hd].astype(F32) + os_ref[1, hd].astype(F32))
        parts.append(y * lax.rsqrt(jnp.mean(y * y, axis=-1, keepdims=True) + EPS))
    yn = jnp.concatenate(parts, axis=-1) * nw_ref[...] * sg_ref[...].astype(F32)
    out = jnp.dot(yn.astype(BF16), w_ref[...], preferred_element_type=F32)
    out_ref[...] = _layer_norm_rows(ALPHA * x_ref[...] + mod_ref[2:3, :] * out, lg_ref[...], lb_ref[...])


def _out_b(geo, o_prompt, o_sample, sg, norm_w, w_out, x, mod_l, ln_g, ln_b):
    t = geo.t
    tb = TOK_TILE_L
    n_p = geo.t_prompt // tb
    full = lambda shape: pl.BlockSpec(shape, lambda i: (0,) * len(shape))
    return pl.pallas_call(
        functools.partial(_out_b_kernel, n_prompt_tiles=n_p),
        name="out_b",
        grid=(t // tb,),
        in_specs=[
            pl.BlockSpec((2, NH_B, tb, DK_B), lambda i: (0, 0, jnp.minimum(i, n_p - 1), 0)),
            pl.BlockSpec((2, NH_B, tb, DK_B), lambda i: (0, 0, jnp.maximum(i - n_p, 0), 0)),
            pl.BlockSpec((tb, D), lambda i: (i, 0)),
            full((1, D)), full((D, D)),
            pl.BlockSpec((tb, D), lambda i: (i, 0)),
            pl.BlockSpec((None, 6, D), lambda i: (geo.cond_row(i, tb), 0, 0)),
            full((1, D)), full((1, D)),
        ],
        out_specs=pl.BlockSpec((tb, D), lambda i: (i, 0)),
        out_shape=jax.ShapeDtypeStruct((t, D), F32),
        compiler_params=_params(("parallel",)),
    )(o_prompt, o_sample, sg, norm_w.reshape(1, D).astype(F32), w_out.astype(BF16), x, mod_l,
      ln_g.reshape(1, D), ln_b.reshape(1, D))


COMBINE_PARTS = 2
N_STREAMS = 1
MOE_BLK = 1024
FFN_ROWS = MOE_BLK
U32 = jnp.uint32
ROW_WORDS = D // 2
CHUNK_W = 256
ROW_CHUNKS = ROW_WORDS // CHUNK_W
SC_WINDOW = 128


def _pack_rows(x):
    return pltpu.pack_elementwise([x[:, :ROW_WORDS], x[:, ROW_WORDS:]], packed_dtype=BF16)


def _unpack_rows(words):
    return jnp.concatenate([pltpu.unpack_elementwise(words, index=i, packed_dtype=BF16, unpacked_dtype=F32)
                            for i in range(2)], axis=1)


def _store_chunks(chunk_ref, x):
    words = _pack_rows(x)
    for c in range(ROW_CHUNKS):
        chunk_ref(c)[...] = words[:, c * CHUNK_W:(c + 1) * CHUNK_W]


def _load_chunks(chunk_ref, valid_rows=None):
    words = jnp.concatenate([chunk_ref(c)[...] for c in range(ROW_CHUNKS)], axis=1)
    if valid_rows is not None:
        row = lax.broadcasted_iota(jnp.int32, (words.shape[0], 1), 0)
        words = jnp.where(row < valid_rows, words, jnp.uint32(0))
    return _unpack_rows(words)


def _first_index(hit, iota, size, axis):
    return jnp.min(jnp.where(hit, iota, size), axis=axis, keepdims=True)


def _router_kernel(x_ref, mod_ref, wrt_ref, eb_ref, e_ref, w_ref, r_ref, cnt_ref, h_ref, cnt_sc):
    i = pl.program_id(0)
    tb = x_ref.shape[0]

    @pl.when(i == 0)
    def _():
        cnt_sc[...] = jnp.zeros_like(cnt_sc)

    h = x_ref[...] * (1.0 + mod_ref[4:5, :]) + mod_ref[3:4, :]
    _store_chunks(lambda c: h_ref.at[c], h)
    logits = lax.dot_general(wrt_ref[...], h, NT_DIMS, precision=HIGHEST, preferred_element_type=F32)
    scores = jax.nn.sigmoid(logits)
    sel = scores + eb_ref[...]

    g3 = sel.reshape(N_GROUPS, GROUP_SIZE, tb)
    io3 = lax.broadcasted_iota(jnp.int32, g3.shape, 1)
    m1 = jnp.max(g3, axis=1, keepdims=True)
    first = _first_index(g3 == m1, io3, GROUP_SIZE, 1)
    m2 = jnp.max(jnp.where(io3 == first, -jnp.inf, g3), axis=1, keepdims=True)
    gscore = (m1 + m2).reshape(N_GROUPS, tb)

    iog = lax.broadcasted_iota(jnp.int32, gscore.shape, 0)
    gmask = jnp.zeros(gscore.shape, F32)
    for _ in range(TOPK_GROUPS):
        gm = jnp.max(gscore, axis=0, keepdims=True)
        pick = iog == _first_index(gscore == gm, iog, N_GROUPS, 0)
        gmask = jnp.where(pick, 1.0, gmask)
        gscore = jnp.where(pick, -jnp.inf, gscore)
    emask = jnp.broadcast_to(gmask.reshape(N_GROUPS, 1, tb), (N_GROUPS, GROUP_SIZE, tb)).reshape(N_EXPERTS, tb)
    cand = jnp.where(emask > 0.0, sel, -jnp.inf)

    ioe = lax.broadcasted_iota(jnp.int32, cand.shape, 0)
    picks, wts = [], []
    onehot = jnp.zeros(cand.shape, F32)
    for _ in range(TOP_K):
        cm = jnp.max(cand, axis=0, keepdims=True)
        idx = _first_index(cand == cm, ioe, N_EXPERTS, 0)
        pick = ioe == idx
        picks.append(pick)
        wts.append(jnp.sum(jnp.where(pick, scores, 0.0), axis=0, keepdims=True))
        onehot = onehot + pick.astype(F32)
        cand = jnp.where(pick, -jnp.inf, cand)
        e_ref[pl.ds(len(picks) - 1, 1), :] = idx
    wsum = wts[0]
    for wk in wts[1:]:
        wsum = wsum + wk
    for k in range(TOP_K):
        w_ref[pl.ds(k, 1), :] = wts[k] / wsum * ROUTED_SCALE

    r_io = lax.broadcasted_iota(jnp.int32, (tb, tb), 0)
    c_io = lax.broadcasted_iota(jnp.int32, (tb, tb), 1)
    before = (r_io < c_io).astype(BF16)
    rank = cnt_sc[:, 0:1] + jnp.dot(onehot.astype(BF16), before, preferred_element_type=F32)
    for k in range(TOP_K):
        r_ref[pl.ds(k, 1), :] = jnp.sum(jnp.where(picks[k], rank, 0.0), axis=0, keepdims=True).astype(jnp.int32)
    cnt_sc[...] = cnt_sc[...] + jnp.sum(onehot, axis=1, keepdims=True)
    cnt_ref[...] = cnt_sc[...]


def _router(geo, x, mod_l, w_router, e_bias):
    t = geo.t
    tb = TOK_TILE_L
    full = lambda shape: pl.BlockSpec(shape, lambda i: (0,) * len(shape))
    e, w, r, cnt, h = pl.pallas_call(
        _router_kernel,
        name="router",
        grid=(t // tb,),
        in_specs=[
            pl.BlockSpec((tb, D), lambda i: (i, 0)),
            pl.BlockSpec((None, 6, D), lambda i: (geo.cond_row(i, tb), 0, 0)),
            full((N_EXPERTS, D)), full((N_EXPERTS, 1)),
        ],
        out_specs=[
            pl.BlockSpec((TOP_K, tb), lambda i: (0, i)),
            pl.BlockSpec((TOP_K, tb), lambda i: (0, i)),
            pl.BlockSpec((TOP_K, tb), lambda i: (0, i)),
            full((N_EXPERTS, LANES)),
            pl.BlockSpec((ROW_CHUNKS, tb, CHUNK_W), lambda i: (0, i, 0)),
        ],
        out_shape=[
            jax.ShapeDtypeStruct((TOP_K, t), jnp.int32),
            jax.ShapeDtypeStruct((TOP_K, t), F32),
            jax.ShapeDtypeStruct((TOP_K, t), jnp.int32),
            jax.ShapeDtypeStruct((N_EXPERTS, LANES), F32),
            jax.ShapeDtypeStruct((ROW_CHUNKS, t, CHUNK_W), U32),
        ],
        scratch_shapes=[pltpu.VMEM((N_EXPERTS, LANES), F32)],
        compiler_params=_params(("arbitrary",)),
    )(x, mod_l, w_router.T.astype(F32), e_bias.reshape(N_EXPERTS, 1).astype(F32))
    return e, w, r, cnt[:, 0].astype(jnp.int32), h


def _slot_kernel(pstart_ref, e_ref, r_ref, o_ref):
    e = e_ref[...]
    slot = r_ref[...]
    for x in range(N_EXPERTS):
        slot = slot + jnp.where(e == x, pstart_ref[x], 0)
    o_ref[...] = slot


def _slots(geo, pstart, top_e, rank):
    tb = math.gcd(SLOT_TILE, geo.t)
    return pl.pallas_call(
        _slot_kernel,
        name="slots",
        grid_spec=pltpu.PrefetchScalarGridSpec(
            num_scalar_prefetch=1,
            grid=(geo.t // tb,),
            in_specs=[pl.BlockSpec((TOP_K, tb), lambda i, p: (0, i)),
                      pl.BlockSpec((TOP_K, tb), lambda i, p: (0, i))],
            out_specs=pl.BlockSpec((TOP_K, tb), lambda i, p: (0, i)),
        ),
        out_shape=jax.ShapeDtypeStruct((TOP_K, geo.t), jnp.int32),
        compiler_params=_params(("parallel",)),
    )(pstart, top_e, rank)


def _block_meta_kernel(pstart_ref, counts_ref, pend_ref, e_ref, v_ref):
    row0 = lax.broadcasted_iota(jnp.int32, e_ref.shape, 1) * MOE_BLK
    blk_e = jnp.zeros(e_ref.shape, jnp.int32)
    for x in range(N_EXPERTS):
        blk_e = blk_e + jnp.where(pend_ref[x] <= row0, 1, 0)
    blk_e = jnp.minimum(blk_e, N_EXPERTS - 1)
    last = jnp.zeros(e_ref.shape, jnp.int32)
    for x in range(N_EXPERTS):
        last = last + jnp.where(blk_e == x, pstart_ref[x] + counts_ref[x], 0)
    e_ref[...] = blk_e
    v_ref[...] = jnp.clip(last - row0, 0, MOE_BLK)


def _block_meta(pstart, counts, pend, n_blocks):
    e, v = pl.pallas_call(
        _block_meta_kernel,
        name="block_meta",
        grid_spec=pltpu.PrefetchScalarGridSpec(
            num_scalar_prefetch=3,
            grid=(1,),
            in_specs=[],
            out_specs=[pl.BlockSpec((1, n_blocks), lambda i, a, b, c: (0, 0)),
                       pl.BlockSpec((1, n_blocks), lambda i, a, b, c: (0, 0))],
        ),
        out_shape=[jax.ShapeDtypeStruct((1, n_blocks), jnp.int32), jax.ShapeDtypeStruct((1, n_blocks), jnp.int32)],
        compiler_params=_params(("arbitrary",)),
    )(pstart, counts, pend)
    return e[0], v[0]


def _sc_mesh():
    return plsc.VectorSubcoreMesh(core_axis_name="core", subcore_axis_name="subcore")


def _sc_scatter(rows, idx, n_out, copies):
    n_src = rows.shape[0]
    n_idx = idx.shape[0]
    groups = ROW_CHUNKS
    win_per_group = n_src // groups // SC_WINDOW

    def idx_block(w, k):
        return (0, ((w // win_per_group) * copies + k) * win_per_group + w % win_per_group)

    @pl.kernel(out_type=jax.ShapeDtypeStruct((n_out, CHUNK_W), rows.dtype), mesh=_sc_mesh(), scratch_types=[],
               name="sc_dispatch")
    def scatter(x_hbm, i_hbm, o_hbm):
        def body(x_vmem, *i_vmems):
            for i_vmem in i_vmems:
                pltpu.sync_copy(x_vmem, o_hbm.at[i_vmem.at[0]])

        pltpu.emit_pipeline(
            body,
            grid=(n_src // SC_WINDOW,),
            in_specs=[pl.BlockSpec((SC_WINDOW, CHUNK_W), index_map=lambda w: (w, 0))]
            + [pl.BlockSpec((1, SC_WINDOW), index_map=functools.partial(idx_block, k=k)) for k in range(copies)],
            out_specs=[],
            core_axis_name=("core", "subcore"),
            dimension_semantics=(pltpu.PARALLEL,),
        )(x_hbm, *([i_hbm] * copies))

    return scatter(rows, idx.reshape(1, n_idx))


def _sc_gather(table, idx):
    n_idx = idx.shape[0]

    @pl.kernel(out_type=jax.ShapeDtypeStruct((n_idx, CHUNK_W), table.dtype), mesh=_sc_mesh(),
               name="sc_combine_gather")
    def gather(t_hbm, i_hbm, o_hbm):
        def body(i_vmem, o_vmem):
            pltpu.sync_copy(t_hbm.at[i_vmem.at[0]], o_vmem)

        pltpu.emit_pipeline(
            body,
            grid=(n_idx // SC_WINDOW,),
            in_specs=[pl.BlockSpec((1, SC_WINDOW), index_map=lambda w: (0, w))],
            out_specs=[pl.BlockSpec((SC_WINDOW, CHUNK_W), index_map=lambda w: (w, 0))],
            core_axis_name=("core", "subcore"),
            dimension_semantics=(pltpu.PARALLEL,),
        )(i_hbm, o_hbm)

    return gather(table, idx.reshape(1, n_idx))


def _ffn_kernel(blk_e_ref, blk_valid_ref, n_used_ref, xs_ref, wg_ref, wu_ref, wd_ref, y_ref, wg_sc, wu_sc, wd_sc):
    b = pl.program_id(0)
    used = b < n_used_ref[0]
    new_expert = (b == 0) | (blk_e_ref[b] != blk_e_ref[jnp.maximum(b - 1, 0)])

    @pl.when(used & new_expert)
    def _():
        wg_sc[...] = wg_ref[...].astype(BF16)
        wu_sc[...] = wu_ref[...].astype(BF16)
        wd_sc[...] = wd_ref[...].astype(BF16)

    @pl.when(used)
    def _():
        for r0 in range(0, MOE_BLK, FFN_ROWS):
            rows = pl.ds(r0, FFN_ROWS)
            x = _load_chunks(lambda c: xs_ref.at[c, rows], valid_rows=blk_valid_ref[b] - r0).astype(BF16)
            g = jnp.dot(x, wg_sc[...], preferred_element_type=F32)
            u = jnp.dot(x, wu_sc[...], preferred_element_type=F32)
            hmid = (g * jax.nn.sigmoid(g) * u).astype(BF16)
            _store_chunks(lambda c: y_ref.at[c, rows], jnp.dot(hmid, wd_sc[...], preferred_element_type=F32))

    @pl.when(jnp.logical_not(used))
    def _():
        y_ref[...] = jnp.zeros_like(y_ref)


def _ffn(xs, blk_e, blk_valid, n_used, layer, wg, wu, wd, n_blocks):
    def blk(b, be, bv, nu):
        return jnp.maximum(jnp.minimum(b, nu[0] - 1), 0)

    def w_idx(b, be, bv, nu):
        return (layer, be[blk(b, be, bv, nu)], 0, 0)

    return pl.pallas_call(
        _ffn_kernel,
        name="expert_ffn",
        grid_spec=pltpu.PrefetchScalarGridSpec(
            num_scalar_prefetch=3,
            grid=(n_blocks,),
            in_specs=[
                pl.BlockSpec((ROW_CHUNKS, MOE_BLK, CHUNK_W), lambda b, be, bv, nu: (0, blk(b, be, bv, nu), 0)),
                pl.BlockSpec((None, None, D, D_EXPERT), w_idx),
                pl.BlockSpec((None, None, D, D_EXPERT), w_idx),
                pl.BlockSpec((None, None, D_EXPERT, D), w_idx),
            ],
            out_specs=pl.BlockSpec((ROW_CHUNKS, MOE_BLK, CHUNK_W), lambda b, be, bv, nu: (0, b, 0)),
            scratch_shapes=[pltpu.VMEM((D, D_EXPERT), BF16), pltpu.VMEM((D, D_EXPERT), BF16),
                            pltpu.VMEM((D_EXPERT, D), BF16)],
        ),
        out_shape=jax.ShapeDtypeStruct(xs.shape, U32),
        compiler_params=_params(("arbitrary",)),
    )(blk_e, blk_valid, n_used, xs, wg, wu, wd)


def _combine_kernel(x_ref, mod_ref, wt_ref, y_ref, sg_ref, su_ref, sd_ref, lg_ref, lb_ref, *rest, n_prompt_tiles,
                    tile0, n_tiles, n_out):
    o_refs = rest[len(rest) - n_out:]
    x = x_ref[...]
    hb = (x * (1.0 + mod_ref[4:5, :]) + mod_ref[3:4, :]).astype(BF16)
    g = jnp.dot(hb, sg_ref[...], preferred_element_type=F32)
    u = jnp.dot(hb, su_ref[...], preferred_element_type=F32)
    ff = jnp.dot((g * jax.nn.sigmoid(g) * u).astype(BF16), sd_ref[...], preferred_element_type=F32)
    for k in range(TOP_K):
        ff = ff + _load_chunks(lambda c: y_ref.at[c, k]) * wt_ref[:, k:k + 1]
    out = _layer_norm_rows(ALPHA * x + mod_ref[5:6, :] * ff, lg_ref[...], lb_ref[...])
    if len(o_refs) == 1:
        o_refs[0][...] = out
    else:
        is_prompt = tile0 + pl.program_id(0) < n_prompt_tiles
        if tile0 < n_prompt_tiles:
            @pl.when(is_prompt)
            def _():
                o_refs[0][...] = out

        if tile0 + n_tiles > n_prompt_tiles:
            @pl.when(jnp.logical_not(is_prompt))
            def _():
                o_refs[1][...] = out


def _combine(geo, x, mod_l, wt, ytok, sg, su, sd, ln_g, ln_b, split=False, tok0=0, prev=None):
    tb = TOK_TILE_L
    n_p = geo.t_prompt // tb
    tile0 = tok0 // tb
    n_tiles = ytok.shape[2] // tb
    full = lambda shape: pl.BlockSpec(shape, lambda i: (0,) * len(shape))
    if split:
        through = pl.BlockSpec(memory_space=pl.ANY)
        out_specs = [pl.BlockSpec((tb, D), lambda i: (jnp.minimum(tile0 + i, n_p - 1), 0)) if tile0 < n_p
                     else through,
                     pl.BlockSpec((tb, D), lambda i: (jnp.maximum(tile0 + i - n_p, 0), 0))
                     if tile0 + n_tiles > n_p else through]
        out_shape = [jax.ShapeDtypeStruct((geo.t_prompt, D), F32), jax.ShapeDtypeStruct((geo.t_sample, D), F32)]
    else:
        out_specs = [pl.BlockSpec((tb, D), lambda i: (tile0 + i, 0))]
        out_shape = [jax.ShapeDtypeStruct((geo.t, D), F32)]
    n_out = len(out_shape)
    prev = list(prev) if prev is not None else []
    n_in = 9
    outs = pl.pallas_call(
        functools.partial(_combine_kernel, n_prompt_tiles=n_p, tile0=tile0, n_tiles=n_tiles, n_out=n_out),
        name="combine",
        grid=(n_tiles,),
        in_specs=[
            pl.BlockSpec((tb, D), lambda i: (tile0 + i, 0)),
            pl.BlockSpec((None, 6, D), lambda i: (geo.cond_row(tile0 + i, tb), 0, 0)),
            pl.BlockSpec((tb, TOP_K), lambda i: (tile0 + i, 0)),
            pl.BlockSpec((ROW_CHUNKS, TOP_K, tb, CHUNK_W), lambda i: (0, 0, i, 0)),
            full((D, D_EXPERT)), full((D, D_EXPERT)), full((D_EXPERT, D)), full((1, D)), full((1, D)),
        ] + [pl.BlockSpec(memory_space=pl.ANY)] * len(prev),
        out_specs=out_specs,
        out_shape=out_shape,
        input_output_aliases={n_in + j: j for j in range(len(prev))},
        compiler_params=_params(("arbitrary",)),
    )(x, mod_l, wt, ytok, sg.astype(BF16), su.astype(BF16), sd.astype(BF16),
      ln_g.reshape(1, D), ln_b.reshape(1, D), *prev)
    return outs if split else outs[0]


def _moe_stages(geo, x, mod_l, w_router, e_bias, layer, wg, wu, wd, sg, su, sd, ln_g, ln_b, split=False):
    t = geo.t
    top_e, w, rank, counts, h = _router(geo, x, mod_l, w_router, e_bias)
    n_blocks = (t * TOP_K) // MOE_BLK + N_EXPERTS
    n_rows = n_blocks * MOE_BLK
    padded = (counts + MOE_BLK - 1) // MOE_BLK * MOE_BLK
    pend = jnp.cumsum(padded)
    pstart = (pend - padded).astype(jnp.int32)
    blk_e, blk_valid = _block_meta(pstart, counts, pend.astype(jnp.int32), n_blocks)
    n_used = (pend[-1:] // MOE_BLK).astype(jnp.int32)
    slots = _slots(geo, pstart, top_e, rank)
    idx = (slots.reshape(1, TOP_K * t) + (jnp.arange(ROW_CHUNKS, dtype=jnp.int32) * n_rows)[:, None]).reshape(-1)
    xs = _sc_scatter(h.reshape(ROW_CHUNKS * t, CHUNK_W), idx, ROW_CHUNKS * n_rows, TOP_K)
    yield
    yb = _ffn(xs.reshape(ROW_CHUNKS, n_rows, CHUNK_W), blk_e, blk_valid, n_used, layer, wg, wu, wd, n_blocks)
    yb_rows = yb.reshape(ROW_CHUNKS * n_rows, CHUNK_W)
    idx3 = idx.reshape(ROW_CHUNKS, TOP_K, t)
    part = t // COMBINE_PARTS
    assert part % TOK_TILE_L == 0
    gathered = [_sc_gather(yb_rows, idx3[:, :, p * part:(p + 1) * part].reshape(-1)) for p in range(COMBINE_PARTS)]
    yield
    out = None
    for p in range(COMBINE_PARTS):
        prev = None if out is None else (out if split else [out])
        out = _combine(geo, x, mod_l, w.T, gathered[p].reshape(ROW_CHUNKS, TOP_K, part, CHUNK_W), sg, su, sd,
                       ln_g, ln_b, split=split, tok0=p * part, prev=prev)
    return out


def _run_interleaved(gens):
    results = [None] * len(gens)
    live = list(range(len(gens)))
    while live:
        for i in list(live):
            try:
                next(gens[i])
            except StopIteration as stop:
                results[i] = stop.value
                live.remove(i)
    return results


def _moe_layer(*args, **kwargs):
    return _run_interleaved([_moe_stages(*args, **kwargs)])[0]


def _pos_embed(rows):
    quarter = D // 4
    omega = 1.0 / (POS_BASE ** (jnp.arange(quarter, dtype=F32) / quarter))
    r, col = jnp.meshgrid(jnp.arange(rows, dtype=F32), jnp.arange(GRID_W, dtype=F32), indexing='ij')
    r = r.reshape(-1, 1) * omega
    col = col.reshape(-1, 1) * omega
    return jnp.concatenate([jnp.sin(r), jnp.cos(r), jnp.sin(col), jnp.cos(col)], axis=-1)


def _mlstm_layer(geo, x, mod_l, j, a_w_in, a_b_gates, a_norm, a_w_out, ln_g, ln_b,
                 state_C, state_n, state_m):
    q, kt, v, so, gr = _proj_a(geo, x, mod_l, a_w_in[j], a_b_gates[j])
    hp, c_p, n_p, m_p = _mlstm_scan(q, kt, v, gr, row0=0, n_seq=geo.n_prompt, seq_len=geo.prompt_len)
    ns = geo.n_sample
    n0 = jnp.pad(state_n[:, j].astype(F32)[..., None], ((0, 0),) * 4 + ((0, LANES - 1),))
    m0 = jnp.pad(state_m[:, j].astype(F32), ((0, 0), (0, 0), (0, SUBLANES - NH_A)))
    m0 = jnp.broadcast_to(m0[..., None], (ns, 2, SUBLANES, LANES))
    hs, _, _, _ = _mlstm_scan(q, kt, v, gr, row0=geo.t_prompt, n_seq=ns, seq_len=geo.sample_len,
                              state=(state_C[:, j].astype(F32), n0, m0))
    x1 = _out_a(geo, hp, hs, so, a_norm[j], a_w_out[j], x, mod_l, ln_g, ln_b)
    return x1, c_p, n_p[..., 0], m_p[:, :, :NH_A, 0]


def _hgrn_layer(geo, x, mod_l, j, lb_layer, b_w_in, b_norm, b_w_out, ln_g, ln_b, state_S):
    q, pre, v, sg = _proj_b(geo, x, mod_l, b_w_in[j])
    lbd = lb_layer.reshape(2, NH_B, 1, DK_B)
    op, s_p = _hgrn_scan(q, pre, v, lbd, row0=0, n_seq=geo.n_prompt, seq_len=geo.prompt_len)
    os_, _ = _hgrn_scan(q, pre, v, lbd, row0=geo.t_prompt, n_seq=geo.n_sample, seq_len=geo.sample_len,
                        state=state_S[:, j].astype(F32))
    x1 = _out_b(geo, op, os_, sg, b_norm[j], b_w_out[j], x, mod_l, ln_g, ln_b)
    return x1, s_p


def kernel(x_prompt, x_sample, state_mlstm_C, state_mlstm_n, state_mlstm_m, state_hgrn_S, c, c_ctx, w_mod, b_mod, ln_g, ln_b, a_w_in, a_b_gates, a_norm, a_w_out, b_w_in, b_lb, b_norm, b_w_out, w_router, e_bias, w_gate, w_up, w_down, ws_gate, ws_up, ws_down):
    bp, sp, _ = x_prompt.shape
    bs, ss, _ = x_sample.shape
    cond = jnp.zeros((COND_ROWS, D), F32).at[0].set(c_ctx).at[1:1 + bs].set(c)
    mod = _modulation(cond, w_mod, b_mod)
    x = (x_prompt.reshape(-1, D), x_sample.reshape(-1, D), _pos_embed(ss // GRID_W))
    sm = jax.nn.softmax(b_lb.astype(F32), axis=0)
    lb_all = jnp.cumsum(sm, axis=0) - sm[0]

    n_streams = N_STREAMS if bp % N_STREAMS == 0 and bs % N_STREAMS == 0 else 1
    pb, sb = bp // n_streams, bs // n_streams

    def stream(s):
        geo = Geometry(pb, sp, sb, ss, prompt0=s * pb, sample0=s * sb)
        rows = [0] + [1 + s * sb + b for b in range(sb)]
        mod_s = mod[:, jnp.array(rows + [0] * (COND_ROWS - len(rows)), jnp.int32)]
        seqs = slice(s * sb, (s + 1) * sb)
        x1, new_c, new_n, new_m = _mlstm_layer(geo, x, mod_s[0], 0, a_w_in, a_b_gates, a_norm, a_w_out,
                                               ln_g[0, 0], ln_b[0, 0], state_mlstm_C[seqs], state_mlstm_n[seqs],
                                               state_mlstm_m[seqs])
        x2 = yield from _moe_stages(geo, x1, mod_s[0], w_router[0], e_bias[0], 0, w_gate, w_up, w_down, ws_gate[0],
                                    ws_up[0], ws_down[0], ln_g[0, 1], ln_b[0, 1])
        x3, new_s = _hgrn_layer(geo, x2, mod_s[1], 0, lb_all[1], b_w_in, b_norm, b_w_out, ln_g[1, 0], ln_b[1, 0],
                                state_hgrn_S[seqs])
        y_p, y_s = yield from _moe_stages(geo, x3, mod_s[1], w_router[1], e_bias[1], 1, w_gate, w_up, w_down,
                                          ws_gate[1], ws_up[1], ws_down[1], ln_g[1, 1], ln_b[1, 1], split=True)
        return y_p.reshape(pb, sp, D), y_s.reshape(sb, ss, D), new_c[:, None], new_n[:, None], new_m[:, None], \
            new_s[:, None]

    outs = _run_interleaved([stream(s) for s in range(n_streams)])
    return tuple(jnp.concatenate(leaf, axis=0) for leaf in zip(*outs))
```

```python
import functools
import math

import jax
import jax.numpy as jnp
from jax import lax
from jax.experimental import pallas as pl
from jax.experimental.pallas import tpu as pltpu
from jax.experimental.pallas import tpu_sc as plsc

F32 = jnp.float32
BF16 = jnp.bfloat16
HIGHEST = lax.Precision.HIGHEST

D = 1024
DEPTH = 2
GRID_W = 64
POS_BASE = 10000.0
EPS = 1e-6
ALPHA = (2.0 * DEPTH) ** 0.25
NH_A, DK_A, DV_A = 4, 128, 256
QK_A, V_A = NH_A * DK_A, NH_A * DV_A
NH_B, DK_B = 8, 128
N_EXPERTS, TOP_K, N_GROUPS, TOPK_GROUPS = 64, 8, 8, 4
GROUP_SIZE = N_EXPERTS // N_GROUPS
D_EXPERT = D // 4
ROUTED_SCALE = 2.5

LANES = 128
SUBLANES = 8
COND_ROWS = 8
TOK_TILE = 256
TOK_TILE_L = 512
SLOT_TILE = 2048
CHUNK_A = 256
VMEM_LIMIT = 56 * 1024 * 1024

NT_DIMS = (((1,), (1,)), ((), ()))


def _params(sem):
    return pltpu.CompilerParams(dimension_semantics=sem, vmem_limit_bytes=VMEM_LIMIT)


def _split3(x):
    hi = x.astype(BF16)
    r = x - hi.astype(F32)
    mid = r.astype(BF16)
    lo = (r - mid.astype(F32)).astype(BF16)
    return hi, mid, lo


def _dot3(a_bf, x):
    hi, mid, lo = _split3(x)
    return (jnp.dot(a_bf, hi, preferred_element_type=F32)
            + jnp.dot(a_bf, mid, preferred_element_type=F32)
            + jnp.dot(a_bf, lo, preferred_element_type=F32))


def _dot3_r(x, a_bf):
    hi, mid, lo = _split3(x)
    return (jnp.dot(hi, a_bf, preferred_element_type=F32)
            + jnp.dot(mid, a_bf, preferred_element_type=F32)
            + jnp.dot(lo, a_bf, preferred_element_type=F32))


def _log_sigmoid(x):
    return jnp.minimum(x, 0.0) - jnp.log1p(jnp.exp(-jnp.abs(x)))


def _layer_norm_rows(x, g, b):
    mu = jnp.mean(x, axis=-1, keepdims=True)
    xc = x - mu
    var = jnp.mean(xc * xc, axis=-1, keepdims=True)
    return xc * lax.rsqrt(var + EPS) * g + b


class Geometry:
    def __init__(self, n_prompt, prompt_len, n_sample, sample_len):
        self.n_prompt, self.prompt_len = n_prompt, prompt_len
        self.n_sample, self.sample_len = n_sample, sample_len
        self.t_prompt = n_prompt * prompt_len
        self.t_sample = n_sample * sample_len
        self.t = self.t_prompt + self.t_sample
        assert self.t_prompt % TOK_TILE_L == 0 and sample_len % TOK_TILE_L == 0
        assert n_sample + 1 <= COND_ROWS

    def cond_row(self, tile, tile_rows):
        n_p = self.t_prompt // tile_rows
        return jnp.where(tile < n_p, 0, 1 + (tile - n_p) // (self.sample_len // tile_rows))


def _mod_kernel(cond_ref, w_ref, b_ref, o_ref):
    c = cond_ref[...]
    s = c * jax.nn.sigmoid(c)
    o_ref[0, 0] = jnp.dot(s, w_ref[0], precision=HIGHEST, preferred_element_type=F32) + b_ref[0, 0]


def _modulation(cond, w_mod, b_mod):
    out = pl.pallas_call(
        _mod_kernel,
        name="modulation",
        grid=(DEPTH, 6),
        in_specs=[
            pl.BlockSpec((COND_ROWS, D), lambda l, j: (0, 0)),
            pl.BlockSpec((1, D, D), lambda l, j: (l, 0, j)),
            pl.BlockSpec((1, 1, 1, D), lambda l, j: (l, j, 0, 0)),
        ],
        out_specs=pl.BlockSpec((1, 1, COND_ROWS, D), lambda l, j: (l, j, 0, 0)),
        out_shape=jax.ShapeDtypeStruct((DEPTH, 6, COND_ROWS, D), F32),
        compiler_params=_params(("arbitrary", "arbitrary")),
    )(cond, w_mod, b_mod.reshape(DEPTH, 6, 1, D))
    return out.transpose(0, 2, 1, 3)


def _embed_specs(geo, tb):
    n_p = geo.t_prompt // tb
    per_seq = geo.sample_len // tb
    return [pl.BlockSpec((tb, D), lambda i: (jnp.minimum(i, n_p - 1), 0)),
            pl.BlockSpec((tb, D), lambda i: (jnp.maximum(i - n_p, 0), 0)),
            pl.BlockSpec((tb, D), lambda i: (jnp.maximum(i - n_p, 0) % per_seq, 0))]


def _embed_tile(xp_ref, xs_ref, pos_ref, n_prompt_tiles):
    return jnp.where(pl.program_id(0) < n_prompt_tiles, xp_ref[...], xs_ref[...] + pos_ref[...])


def _proj_a_kernel(xp_ref, xs_ref, pos_ref, mod_ref, wq_ref, wkt_ref, wvo_ref, wgt_ref, bgt_ref,
                   q_ref, kt_ref, v_ref, so_ref, gr_ref, *, n_prompt_tiles):
    x = _embed_tile(xp_ref, xs_ref, pos_ref, n_prompt_tiles)
    h = x * (1.0 + mod_ref[1:2, :]) + mod_ref[0:1, :]
    hb = h.astype(BF16)
    q_ref[...] = jnp.dot(hb, wq_ref[...], preferred_element_type=F32).astype(BF16)
    kt = lax.dot_general(wkt_ref[...], hb, NT_DIMS, preferred_element_type=F32)
    kt_ref[...] = (kt * (DK_A ** -0.5)).astype(BF16)
    vo = jnp.dot(hb, wvo_ref[...], preferred_element_type=F32)
    v_ref[...] = vo[:, :V_A].astype(BF16)
    so_ref[...] = jax.nn.sigmoid(vo[:, V_A:]).astype(BF16)
    gr_ref[...] = lax.dot_general(wgt_ref[...], h, NT_DIMS, precision=HIGHEST,
                                  preferred_element_type=F32) + bgt_ref[...]


def _proj_a(geo, x, mod_l, w_in, b_gates):
    t = geo.t
    n_gate = 4 * NH_A
    wq = w_in[:, :QK_A].astype(BF16)
    wkt = w_in[:, QK_A:2 * QK_A].T.astype(BF16)
    wvo = w_in[:, 2 * QK_A:2 * QK_A + 2 * V_A].astype(BF16)
    wg = w_in[:, 2 * QK_A + 2 * V_A:]
    bg = b_gates.reshape(n_gate).astype(F32)
    tb = TOK_TILE_L
    full = lambda shape: pl.BlockSpec(shape, lambda i: (0,) * len(shape))
    return pl.pallas_call(
        functools.partial(_proj_a_kernel, n_prompt_tiles=geo.t_prompt // tb),
        name="proj_a",
        grid=(t // tb,),
        in_specs=_embed_specs(geo, tb) + [
            pl.BlockSpec((None, 6, D), lambda i: (geo.cond_row(i, tb), 0, 0)),
            full((D, QK_A)), full((QK_A, D)), full((D, 2 * V_A)), full((n_gate, D)), full((n_gate, 1)),
        ],
        out_specs=[
            pl.BlockSpec((tb, QK_A), lambda i: (i, 0)),
            pl.BlockSpec((QK_A, tb), lambda i: (0, i)),
            pl.BlockSpec((tb, V_A), lambda i: (i, 0)),
            pl.BlockSpec((tb, V_A), lambda i: (i, 0)),
            pl.BlockSpec((n_gate, tb), lambda i: (0, i)),
        ],
        out_shape=[
            jax.ShapeDtypeStruct((t, QK_A), BF16),
            jax.ShapeDtypeStruct((QK_A, t), BF16),
            jax.ShapeDtypeStruct((t, V_A), BF16),
            jax.ShapeDtypeStruct((t, V_A), BF16),
            jax.ShapeDtypeStruct((n_gate, t), F32),
        ],
        compiler_params=_params(("parallel",)),
    )(*x, mod_l, wq, wkt, wvo, wg.T, bg.reshape(n_gate, 1))


def _mlstm_scan_kernel(*refs, chunk, has_state):
    if has_state:
        (q_ref, kt_ref, v_ref, gr_ref, c0_ref, n0_ref, m0_ref,
         h_ref, c_out, n_out, m_out, c_sc, n_sc, m_sc) = refs
    else:
        (q_ref, kt_ref, v_ref, gr_ref,
         h_ref, c_out, n_out, m_out, c_sc, n_sc, m_sc) = refs
    L = chunk
    d = pl.program_id(1)
    c = pl.program_id(2)
    fwd = d == 0

    @pl.when(c == 0)
    def _():
        if has_state:
            c_sc[...] = c0_ref[0, 0]
            n_sc[...] = n0_ref[0, 0]
            m_sc[...] = m0_ref[0, 0]
        else:
            c_sc[...] = jnp.zeros_like(c_sc)
            n_sc[...] = jnp.zeros_like(n_sc)
            m_sc[...] = jnp.zeros_like(m_sc)

    row = lax.broadcasted_iota(jnp.int32, (L, L), 0)
    col = lax.broadcasted_iota(jnp.int32, (L, L), 1)
    sgn = 1 - 2 * d
    causal = (row - col) * sgn >= 0
    tri_t = ((col - row) * sgn >= 0).astype(BF16)

    gr = gr_ref[...]
    br_all = _dot3_r(_log_sigmoid(gr), tri_t)
    bc_all = jnp.concatenate([br_all, jnp.zeros((LANES - br_all.shape[0], L), F32)], axis=0).T
    ones_blk = (lax.broadcasted_iota(jnp.int32, (L, LANES), 1) == 0).astype(BF16)

    for h in range(NH_A):
        b_c = jnp.where(fwd, bc_all[:, 4 + h:5 + h], bc_all[:, 12 + h:13 + h])
        b_r = jnp.where(fwd, br_all[4 + h:5 + h, :], br_all[12 + h:13 + h, :])
        i_r = jnp.where(fwd, gr[h:h + 1, :], gr[8 + h:9 + h, :])
        bl = jnp.where(fwd, b_r[:, L - 1:L], b_r[:, 0:1])
        q = q_ref[:, h * DK_A:(h + 1) * DK_A]
        kt = kt_ref[h * DK_A:(h + 1) * DK_A, :]
        v = v_ref[:, h * DV_A:(h + 1) * DV_A]
        m = m_sc[h:h + 1, 0:1]
        cst = c_sc[h]
        nst = n_sc[h]

        a_r = i_r - b_r
        logd = jnp.where(causal, b_c + a_r, -jnp.inf)
        inter = b_c + m
        m_t = jnp.maximum(inter, jnp.max(logd, axis=1, keepdims=True))
        dmat = jnp.exp(logd - m_t)
        e_int = jnp.exp(inter - m_t)
        s = (jnp.dot(q, kt, preferred_element_type=F32) * dmat).astype(BF16)
        num = (jnp.dot(s, v, preferred_element_type=F32)
               + e_int * jnp.dot(q, cst.astype(BF16), preferred_element_type=F32))
        den = (jnp.dot(s, ones_blk, preferred_element_type=F32)
               + e_int * jnp.dot(q, nst.astype(BF16), preferred_element_type=F32))[:, 0:1]
        h_ref[:, h * DV_A:(h + 1) * DV_A] = (num / jnp.maximum(jnp.abs(den), jnp.exp(-m_t))).astype(BF16)

        logw = bl + a_r
        m_new = jnp.maximum(bl + m, jnp.max(logw, axis=1, keepdims=True))
        w = jnp.exp(logw - m_new)
        decay = jnp.exp(bl + m - m_new)
        kw = (kt.astype(F32) * w).astype(BF16)
        c_sc[h] = decay * cst + jnp.dot(kw, v, preferred_element_type=F32)
        n_sc[h] = decay * nst + jnp.dot(kw, ones_blk, preferred_element_type=F32)
        m_sc[h:h + 1, :] = jnp.broadcast_to(m_new, (1, LANES))

    @pl.when(c == pl.num_programs(2) - 1)
    def _():
        c_out[0, 0] = c_sc[...]
        n_out[0, 0] = n_sc[...]
        m_out[0, 0] = m_sc[...]


def _mlstm_scan(q, kt, v, gr, *, row0, n_seq, seq_len, state=None):
    L = CHUNK_A
    nc = seq_len // L
    blk0 = row0 // L

    def loc_blk(b, d, c):
        return b * nc + c + d * (nc - 1 - 2 * c)

    def tok_blk(b, d, c):
        return blk0 + loc_blk(b, d, c)

    in_specs = [
        pl.BlockSpec((L, QK_A), lambda b, d, c: (tok_blk(b, d, c), 0)),
        pl.BlockSpec((QK_A, L), lambda b, d, c: (0, tok_blk(b, d, c))),
        pl.BlockSpec((L, V_A), lambda b, d, c: (tok_blk(b, d, c), 0)),
        pl.BlockSpec((4 * NH_A, L), lambda b, d, c: (0, tok_blk(b, d, c))),
    ]
    args = [q, kt, v, gr]
    if state is not None:
        in_specs += [
            pl.BlockSpec((1, 1, NH_A, DK_A, DV_A), lambda b, d, c: (b, d, 0, 0, 0)),
            pl.BlockSpec((1, 1, NH_A, DK_A, LANES), lambda b, d, c: (b, d, 0, 0, 0)),
            pl.BlockSpec((1, 1, SUBLANES, LANES), lambda b, d, c: (b, d, 0, 0)),
        ]
        args += list(state)
    return pl.pallas_call(
        functools.partial(_mlstm_scan_kernel, chunk=L, has_state=state is not None),
        name="mlstm_scan_seeded" if state is not None else "mlstm_scan",
        grid=(n_seq, 2, nc),
        in_specs=in_specs,
        out_specs=[
            pl.BlockSpec((None, L, V_A), lambda b, d, c: (d, loc_blk(b, d, c), 0)),
            pl.BlockSpec((1, 1, NH_A, DK_A, DV_A), lambda b, d, c: (b, d, 0, 0, 0)),
            pl.BlockSpec((1, 1, NH_A, DK_A, LANES), lambda b, d, c: (b, d, 0, 0, 0)),
            pl.BlockSpec((1, 1, SUBLANES, LANES), lambda b, d, c: (b, d, 0, 0)),
        ],
        out_shape=[
            jax.ShapeDtypeStruct((2, n_seq * seq_len, V_A), BF16),
            jax.ShapeDtypeStruct((n_seq, 2, NH_A, DK_A, DV_A), F32),
            jax.ShapeDtypeStruct((n_seq, 2, NH_A, DK_A, LANES), F32),
            jax.ShapeDtypeStruct((n_seq, 2, SUBLANES, LANES), F32),
        ],
        scratch_shapes=[
            pltpu.VMEM((NH_A, DK_A, DV_A), F32),
            pltpu.VMEM((NH_A, DK_A, LANES), F32),
            pltpu.VMEM((SUBLANES, LANES), F32),
        ],
        compiler_params=_params(("parallel", "parallel", "arbitrary")),
    )(*args)


def _out_a_kernel(hp_ref, hs_ref, so_ref, nw_ref, w_ref, xp_ref, xs_ref, pos_ref, mod_ref, lg_ref, lb_ref,
                  wrt_ref, eb_ref, o_ref, *route_refs, n_prompt_tiles):
    is_prompt = pl.program_id(0) < n_prompt_tiles
    x = _embed_tile(xp_ref, xs_ref, pos_ref, n_prompt_tiles)
    y = jnp.where(is_prompt, hp_ref[0].astype(F32) + hp_ref[1].astype(F32),
                  hs_ref[0].astype(F32) + hs_ref[1].astype(F32))
    parts = []
    for h in range(NH_A):
        yh = y[:, h * DV_A:(h + 1) * DV_A]
        mu = jnp.mean(yh, axis=-1, keepdims=True)
        yc = yh - mu
        var = jnp.mean(yc * yc, axis=-1, keepdims=True)
        parts.append(yc * lax.rsqrt(var + EPS))
    yn = jnp.concatenate(parts, axis=-1) * nw_ref[...] * so_ref[...].astype(F32)
    out = jnp.dot(yn.astype(BF16), w_ref[...], preferred_element_type=F32)
    x1 = _layer_norm_rows(ALPHA * x + mod_ref[2:3, :] * out, lg_ref[...], lb_ref[...])
    o_ref[...] = x1
    _route_tile(x1, mod_ref, wrt_ref, eb_ref, *route_refs)


def _out_a(geo, h_prompt, h_sample, so, norm_w, w_out, x, mod_l, ln_g, ln_b, w_router, e_bias):
    t = geo.t
    tb = TOK_TILE_L
    n_p = geo.t_prompt // tb
    full = lambda shape: pl.BlockSpec(shape, lambda i: (0,) * len(shape))
    rio = _RouterIO(t, tb, w_router, e_bias)
    outs = pl.pallas_call(
        functools.partial(_out_a_kernel, n_prompt_tiles=n_p),
        name="out_a",
        grid=(t // tb,),
        in_specs=[
            pl.BlockSpec((2, tb, V_A), lambda i: (0, jnp.minimum(i, n_p - 1), 0)),
            pl.BlockSpec((2, tb, V_A), lambda i: (0, jnp.maximum(i - n_p, 0), 0)),
            pl.BlockSpec((tb, V_A), lambda i: (i, 0)),
            full((1, V_A)), full((V_A, D)),
        ] + _embed_specs(geo, tb) + [
            pl.BlockSpec((None, 6, D), lambda i: (geo.cond_row(i, tb), 0, 0)),
            full((1, D)), full((1, D)),
        ] + rio.in_specs,
        out_specs=[pl.BlockSpec((tb, D), lambda i: (i, 0))] + rio.out_specs,
        out_shape=[jax.ShapeDtypeStruct((t, D), F32)] + rio.out_shape,
        scratch_shapes=rio.scratch,
        compiler_params=_params(("arbitrary",)),
    )(h_prompt, h_sample, so, norm_w.reshape(1, V_A).astype(F32), w_out.astype(BF16), *x, mod_l,
      ln_g.reshape(1, D), ln_b.reshape(1, D), *rio.inputs)
    return outs[0], _RouterIO.unpack(outs[1:])


def _proj_b_kernel(x_ref, mod_ref, w_ref, q_ref, pre_ref, v_ref, sg_ref):
    h = x_ref[...] * (1.0 + mod_ref[1:2, :]) + mod_ref[0:1, :]
    z = jnp.dot(h.astype(BF16), w_ref[...], preferred_element_type=F32)
    for hd in range(NH_B):
        lo = hd * DK_B
        qh = z[:, lo:lo + DK_B]
        q_ref[hd] = qh * jax.nn.sigmoid(qh)
        pre_ref[0, hd] = z[:, D + lo:D + lo + DK_B]
        pre_ref[1, hd] = z[:, 2 * D + lo:2 * D + lo + DK_B]
        v_ref[hd] = z[:, 3 * D + lo:3 * D + lo + DK_B].astype(BF16)
    g = z[:, 4 * D:]
    sg_ref[...] = (g * jax.nn.sigmoid(g)).astype(BF16)


def _proj_b(geo, x, mod_l, w_in):
    t = geo.t
    tb = TOK_TILE
    return pl.pallas_call(
        _proj_b_kernel,
        name="proj_b",
        grid=(t // tb,),
        in_specs=[
            pl.BlockSpec((tb, D), lambda i: (i, 0)),
            pl.BlockSpec((None, 6, D), lambda i: (geo.cond_row(i, tb), 0, 0)),
            pl.BlockSpec((D, 5 * D), lambda i: (0, 0)),
        ],
        out_specs=[
            pl.BlockSpec((NH_B, tb, DK_B), lambda i: (0, i, 0)),
            pl.BlockSpec((2, NH_B, tb, DK_B), lambda i: (0, 0, i, 0)),
            pl.BlockSpec((NH_B, tb, DK_B), lambda i: (0, i, 0)),
            pl.BlockSpec((tb, D), lambda i: (i, 0)),
        ],
        out_shape=[
            jax.ShapeDtypeStruct((NH_B, t, DK_B), F32),
            jax.ShapeDtypeStruct((2, NH_B, t, DK_B), F32),
            jax.ShapeDtypeStruct((NH_B, t, DK_B), BF16),
            jax.ShapeDtypeStruct((t, D), BF16),
        ],
        compiler_params=_params(("parallel",)),
    )(x, mod_l, w_in.astype(BF16))


CHUNK_B = 128
BAND = SUBLANES // 2
TN_DIMS = (((0,), (0,)), ((), ()))


def _hgrn_head(q, pre, lbv, v_bf, st, fwd):
    L = q.shape[0]
    sg = jax.nn.sigmoid(pre)
    f = lbv + (1.0 - lbv) * sg
    lf = jnp.log(f)
    kk = (1.0 - lbv) * (1.0 - sg)
    row = lax.broadcasted_iota(jnp.int32, (L, L), 0)
    col = lax.broadcasted_iota(jnp.int32, (L, L), 1)
    tri = ((row >= col) if fwd else (row <= col)).astype(BF16)
    b = _dot3(tri, lf)
    tpos = lax.broadcasted_iota(jnp.int32, (L, DK_B), 0)
    blk_bits = row ^ col
    lag = jnp.where(blk_bits < BAND, (row - col) if fwd else (col - row), -1)

    step = 1 if fwd else L - 1
    att = jnp.where(lag == 0, jnp.sum(q * kk, axis=1, keepdims=True), 0.0)
    f_r, kk_r, g = f, kk, f
    for dl in range(1, BAND):
        if dl > 1:
            f_r = pltpu.roll(f_r, step, 0)
            g = g * f_r
        kk_r = pltpu.roll(kk_r, step, 0)
        att = jnp.where(lag == dl, jnp.sum(q * kk_r * g, axis=1, keepdims=True), att)

    w = BAND
    while w < L:
        nb = L // (2 * w)
        b3 = b.reshape(nb, 2 * w, DK_B)
        edge = (b3[:, w - 1:w, :] if fwd else b3[:, w:w + 1, :])
        bmid = jnp.broadcast_to(edge, (nb, 2 * w, DK_B)).reshape(L, DK_B)
        second = (tpos & w) != 0
        t_side = second if fwd else jnp.logical_not(second)
        e = jnp.exp(jnp.where(t_side, b - bmid, bmid - b))
        qt = jnp.where(t_side, q * e, 0.0).astype(BF16)
        ks = jnp.where(t_side, 0.0, kk * e).astype(BF16)
        a = lax.dot_general(qt, ks, NT_DIMS, preferred_element_type=F32)
        att = att + jnp.where(blk_bits < 2 * w, a, 0.0)
        w *= 2
    o = jnp.dot(att.astype(BF16), v_bf, preferred_element_type=F32)

    bl = b[L - 1:L, :] if fwd else b[0:1, :]
    o = o + lax.dot_general((q * jnp.exp(b)).astype(BF16), st.astype(BF16), NT_DIMS, preferred_element_type=F32)
    kd = (kk * jnp.exp(bl - b)).astype(BF16)
    st_new = jnp.exp(bl) * st + lax.dot_general(v_bf, kd, TN_DIMS, preferred_element_type=F32)
    return o, st_new


def _hgrn_scan_kernel(*refs, has_state):
    if has_state:
        q_ref, pre_ref, v_ref, lb_ref, s0_ref, o_ref, s_out, st_sc = refs
    else:
        q_ref, pre_ref, v_ref, lb_ref, o_ref, s_out, st_sc = refs
    d = pl.program_id(1)
    c = pl.program_id(2)

    @pl.when(c == 0)
    def _():
        if has_state:
            for hd in range(NH_B):
                st_sc[hd] = s0_ref[0, 0, hd].T
        else:
            st_sc[...] = jnp.zeros_like(st_sc)

    def run(fwd):
        def head(hd, carry):
            o, st_new = _hgrn_head(q_ref[hd], pre_ref[hd], lb_ref[hd], v_ref[hd], st_sc[hd], fwd)
            o_ref[hd] = o.astype(BF16)
            st_sc[hd] = st_new
            return carry
        lax.fori_loop(0, NH_B, head, 0, unroll=8)

    @pl.when(d == 0)
    def _():
        run(True)

    @pl.when(d == 1)
    def _():
        run(False)

    @pl.when(c == pl.num_programs(2) - 1)
    def _():
        for hd in range(NH_B):
            s_out[0, 0, hd] = st_sc[hd].T


def _hgrn_scan(q, pre, v, lbd, *, row0, n_seq, seq_len, state=None):
    L = CHUNK_B
    nc = seq_len // L
    blk0 = row0 // L

    def loc_blk(b, d, c):
        return b * nc + c + d * (nc - 1 - 2 * c)

    def tok_blk(b, d, c):
        return blk0 + loc_blk(b, d, c)

    in_specs = [
        pl.BlockSpec((NH_B, L, DK_B), lambda b, d, c: (0, tok_blk(b, d, c), 0)),
        pl.BlockSpec((None, NH_B, L, DK_B), lambda b, d, c: (d, 0, tok_blk(b, d, c), 0)),
        pl.BlockSpec((NH_B, L, DK_B), lambda b, d, c: (0, tok_blk(b, d, c), 0)),
        pl.BlockSpec((None, NH_B, 1, DK_B), lambda b, d, c: (d, 0, 0, 0)),
    ]
    args = [q, pre, v, lbd]
    if state is not None:
        in_specs.append(pl.BlockSpec((1, 1, NH_B, DK_B, DK_B), lambda b, d, c: (b, d, 0, 0, 0)))
        args.append(state)
    return pl.pallas_call(
        functools.partial(_hgrn_scan_kernel, has_state=state is not None),
        name="hgrn_scan_seeded" if state is not None else "hgrn_scan",
        grid=(n_seq, 2, nc),
        in_specs=in_specs,
        out_specs=[
            pl.BlockSpec((None, NH_B, L, DK_B), lambda b, d, c: (d, 0, loc_blk(b, d, c), 0)),
            pl.BlockSpec((1, 1, NH_B, DK_B, DK_B), lambda b, d, c: (b, d, 0, 0, 0)),
        ],
        out_shape=[
            jax.ShapeDtypeStruct((2, NH_B, n_seq * seq_len, DK_B), BF16),
            jax.ShapeDtypeStruct((n_seq, 2, NH_B, DK_B, DK_B), F32),
        ],
        scratch_shapes=[pltpu.VMEM((NH_B, DK_B, DK_B), F32)],
        compiler_params=_params(("parallel", "parallel", "arbitrary")),
    )(*args)


def _out_b_kernel(op_ref, os_ref, sg_ref, nw_ref, w_ref, x_ref, mod_ref, lg_ref, lb_ref, wrt_ref, eb_ref,
                  out_ref, *route_refs, n_prompt_tiles):
    is_prompt = pl.program_id(0) < n_prompt_tiles
    parts = []
    for hd in range(NH_B):
        y = jnp.where(is_prompt, op_ref[0, hd].astype(F32) + op_ref[1, hd].astype(F32),
                      os_ref[0, hd].astype(F32) + os_ref[1, hd].astype(F32))
        parts.append(y * lax.rsqrt(jnp.mean(y * y, axis=-1, keepdims=True) + EPS))
    yn = jnp.concatenate(parts, axis=-1) * nw_ref[...] * sg_ref[...].astype(F32)
    out = jnp.dot(yn.astype(BF16), w_ref[...], preferred_element_type=F32)
    x1 = _layer_norm_rows(ALPHA * x_ref[...] + mod_ref[2:3, :] * out, lg_ref[...], lb_ref[...])
    out_ref[...] = x1
    _route_tile(x1, mod_ref, wrt_ref, eb_ref, *route_refs)


def _out_b(geo, o_prompt, o_sample, sg, norm_w, w_out, x, mod_l, ln_g, ln_b, w_router, e_bias):
    t = geo.t
    tb = TOK_TILE_L
    n_p = geo.t_prompt // tb
    full = lambda shape: pl.BlockSpec(shape, lambda i: (0,) * len(shape))
    rio = _RouterIO(t, tb, w_router, e_bias)
    outs = pl.pallas_call(
        functools.partial(_out_b_kernel, n_prompt_tiles=n_p),
        name="out_b",
        grid=(t // tb,),
        in_specs=[
            pl.BlockSpec((2, NH_B, tb, DK_B), lambda i: (0, 0, jnp.minimum(i, n_p - 1), 0)),
            pl.BlockSpec((2, NH_B, tb, DK_B), lambda i: (0, 0, jnp.maximum(i - n_p, 0), 0)),
            pl.BlockSpec((tb, D), lambda i: (i, 0)),
            full((1, D)), full((D, D)),
            pl.BlockSpec((tb, D), lambda i: (i, 0)),
            pl.BlockSpec((None, 6, D), lambda i: (geo.cond_row(i, tb), 0, 0)),
            full((1, D)), full((1, D)),
        ] + rio.in_specs,
        out_specs=[pl.BlockSpec((tb, D), lambda i: (i, 0))] + rio.out_specs,
        out_shape=[jax.ShapeDtypeStruct((t, D), F32)] + rio.out_shape,
        scratch_shapes=rio.scratch,
        compiler_params=_params(("arbitrary",)),
    )(o_prompt, o_sample, sg, norm_w.reshape(1, D).astype(F32), w_out.astype(BF16), x, mod_l,
      ln_g.reshape(1, D), ln_b.reshape(1, D), *rio.inputs)
    return outs[0], _RouterIO.unpack(outs[1:])


MOE_BLK = 1024
U32 = jnp.uint32
ROW_WORDS = D // 2
CHUNK_W = 256
ROW_CHUNKS = ROW_WORDS // CHUNK_W
SC_WINDOW = 128


def _pack_rows(x):
    return pltpu.pack_elementwise([x[:, :ROW_WORDS], x[:, ROW_WORDS:]], packed_dtype=BF16)


def _unpack_rows(words):
    return jnp.concatenate([pltpu.unpack_elementwise(words, index=i, packed_dtype=BF16, unpacked_dtype=F32)
                            for i in range(2)], axis=1)


def _store_chunks(chunk_ref, x):
    words = _pack_rows(x)
    for c in range(ROW_CHUNKS):
        chunk_ref(c)[...] = words[:, c * CHUNK_W:(c + 1) * CHUNK_W]


def _load_chunks(chunk_ref, valid_rows=None):
    words = jnp.concatenate([chunk_ref(c)[...] for c in range(ROW_CHUNKS)], axis=1)
    if valid_rows is not None:
        row = lax.broadcasted_iota(jnp.int32, (words.shape[0], 1), 0)
        words = jnp.where(row < valid_rows, words, jnp.uint32(0))
    return _unpack_rows(words)


def _first_index(hit, iota, size, axis):
    return jnp.min(jnp.where(hit, iota, size), axis=axis, keepdims=True)


def _route_tile(x, mod_ref, wrt_ref, eb_ref, e_ref, w_ref, r_ref, cnt_ref, h_ref, cnt_sc):
    i = pl.program_id(0)
    tb = x.shape[0]

    @pl.when(i == 0)
    def _():
        cnt_sc[...] = jnp.zeros_like(cnt_sc)

    h = x * (1.0 + mod_ref[4:5, :]) + mod_ref[3:4, :]
    _store_chunks(lambda c: h_ref.at[c], h)
    logits = lax.dot_general(wrt_ref[...], h, NT_DIMS, precision=HIGHEST, preferred_element_type=F32)
    scores = jax.nn.sigmoid(logits)
    sel = scores + eb_ref[...]

    g3 = sel.reshape(N_GROUPS, GROUP_SIZE, tb)
    io3 = lax.broadcasted_iota(jnp.int32, g3.shape, 1)
    m1 = jnp.max(g3, axis=1, keepdims=True)
    first = _first_index(g3 == m1, io3, GROUP_SIZE, 1)
    m2 = jnp.max(jnp.where(io3 == first, -jnp.inf, g3), axis=1, keepdims=True)
    gscore = (m1 + m2).reshape(N_GROUPS, tb)

    iog = lax.broadcasted_iota(jnp.int32, gscore.shape, 0)
    gmask = jnp.zeros(gscore.shape, F32)
    for _ in range(TOPK_GROUPS):
        gm = jnp.max(gscore, axis=0, keepdims=True)
        pick = iog == _first_index(gscore == gm, iog, N_GROUPS, 0)
        gmask = jnp.where(pick, 1.0, gmask)
        gscore = jnp.where(pick, -jnp.inf, gscore)
    emask = jnp.broadcast_to(gmask.reshape(N_GROUPS, 1, tb), (N_GROUPS, GROUP_SIZE, tb)).reshape(N_EXPERTS, tb)
    cand = jnp.where(emask > 0.0, sel, -jnp.inf)

    ioe = lax.broadcasted_iota(jnp.int32, cand.shape, 0)
    picks, wts = [], []
    onehot = jnp.zeros(cand.shape, F32)
    for _ in range(TOP_K):
        cm = jnp.max(cand, axis=0, keepdims=True)
        idx = _first_index(cand == cm, ioe, N_EXPERTS, 0)
        pick = ioe == idx
        picks.append(pick)
        wts.append(jnp.sum(jnp.where(pick, scores, 0.0), axis=0, keepdims=True))
        onehot = onehot + pick.astype(F32)
        cand = jnp.where(pick, -jnp.inf, cand)
        e_ref[pl.ds(len(picks) - 1, 1), :] = idx
    wsum = wts[0]
    for wk in wts[1:]:
        wsum = wsum + wk
    for k in range(TOP_K):
        w_ref[pl.ds(k, 1), :] = wts[k] / wsum * ROUTED_SCALE

    r_io = lax.broadcasted_iota(jnp.int32, (tb, tb), 0)
    c_io = lax.broadcasted_iota(jnp.int32, (tb, tb), 1)
    before = (r_io < c_io).astype(BF16)
    rank = cnt_sc[:, 0:1] + jnp.dot(onehot.astype(BF16), before, preferred_element_type=F32)
    for k in range(TOP_K):
        r_ref[pl.ds(k, 1), :] = jnp.sum(jnp.where(picks[k], rank, 0.0), axis=0, keepdims=True).astype(jnp.int32)
    cnt_sc[...] = cnt_sc[...] + jnp.sum(onehot, axis=1, keepdims=True)
    cnt_ref[...] = cnt_sc[...]


class _RouterIO:
    def __init__(self, t, tb, w_router, e_bias):
        full = lambda shape: pl.BlockSpec(shape, lambda i: (0,) * len(shape))
        self.inputs = [w_router.T.astype(F32), e_bias.reshape(N_EXPERTS, 1).astype(F32)]
        self.in_specs = [full((N_EXPERTS, D)), full((N_EXPERTS, 1))]
        self.out_specs = [
            pl.BlockSpec((TOP_K, tb), lambda i: (0, i)),
            pl.BlockSpec((TOP_K, tb), lambda i: (0, i)),
            pl.BlockSpec((TOP_K, tb), lambda i: (0, i)),
            full((N_EXPERTS, LANES)),
            pl.BlockSpec((ROW_CHUNKS, tb, CHUNK_W), lambda i: (0, i, 0)),
        ]
        self.out_shape = [
            jax.ShapeDtypeStruct((TOP_K, t), jnp.int32),
            jax.ShapeDtypeStruct((TOP_K, t), F32),
            jax.ShapeDtypeStruct((TOP_K, t), jnp.int32),
            jax.ShapeDtypeStruct((N_EXPERTS, LANES), F32),
            jax.ShapeDtypeStruct((ROW_CHUNKS, t, CHUNK_W), U32),
        ]
        self.scratch = [pltpu.VMEM((N_EXPERTS, LANES), F32)]

    @staticmethod
    def unpack(outs):
        e, w, r, cnt, h = outs
        return e, w, r, cnt[:, 0].astype(jnp.int32), h


def _slot_kernel(pstart_ref, e_ref, r_ref, o_ref):
    e = e_ref[...]
    slot = r_ref[...]
    for x in range(N_EXPERTS):
        slot = slot + jnp.where(e == x, pstart_ref[x], 0)
    o_ref[...] = slot


def _slots(geo, pstart, top_e, rank):
    tb = math.gcd(SLOT_TILE, geo.t)
    return pl.pallas_call(
        _slot_kernel,
        name="slots",
        grid_spec=pltpu.PrefetchScalarGridSpec(
            num_scalar_prefetch=1,
            grid=(geo.t // tb,),
            in_specs=[pl.BlockSpec((TOP_K, tb), lambda i, p: (0, i)),
                      pl.BlockSpec((TOP_K, tb), lambda i, p: (0, i))],
            out_specs=pl.BlockSpec((TOP_K, tb), lambda i, p: (0, i)),
        ),
        out_shape=jax.ShapeDtypeStruct((TOP_K, geo.t), jnp.int32),
        compiler_params=_params(("parallel",)),
    )(pstart, top_e, rank)


def _block_meta_kernel(pstart_ref, counts_ref, pend_ref, e_ref, v_ref):
    row0 = lax.broadcasted_iota(jnp.int32, e_ref.shape, 1) * MOE_BLK
    blk_e = jnp.zeros(e_ref.shape, jnp.int32)
    for x in range(N_EXPERTS):
        blk_e = blk_e + jnp.where(pend_ref[x] <= row0, 1, 0)
    blk_e = jnp.minimum(blk_e, N_EXPERTS - 1)
    last = jnp.zeros(e_ref.shape, jnp.int32)
    for x in range(N_EXPERTS):
        last = last + jnp.where(blk_e == x, pstart_ref[x] + counts_ref[x], 0)
    e_ref[...] = blk_e
    v_ref[...] = jnp.clip(last - row0, 0, MOE_BLK)


def _block_meta(pstart, counts, pend, n_blocks):
    e, v = pl.pallas_call(
        _block_meta_kernel,
        name="block_meta",
        grid_spec=pltpu.PrefetchScalarGridSpec(
            num_scalar_prefetch=3,
            grid=(1,),
            in_specs=[],
            out_specs=[pl.BlockSpec((1, n_blocks), lambda i, a, b, c: (0, 0)),
                       pl.BlockSpec((1, n_blocks), lambda i, a, b, c: (0, 0))],
        ),
        out_shape=[jax.ShapeDtypeStruct((1, n_blocks), jnp.int32), jax.ShapeDtypeStruct((1, n_blocks), jnp.int32)],
        compiler_params=_params(("arbitrary",)),
    )(pstart, counts, pend)
    return e[0], v[0]


def _sc_mesh():
    return plsc.VectorSubcoreMesh(core_axis_name="core", subcore_axis_name="subcore")


def _sc_scatter(rows, idx, n_out, copies):
    n_src = rows.shape[0]
    n_idx = idx.shape[0]
    groups = ROW_CHUNKS
    win_per_group = n_src // groups // SC_WINDOW

    def idx_block(w, k):
        return (0, ((w // win_per_group) * copies + k) * win_per_group + w % win_per_group)

    @pl.kernel(out_type=jax.ShapeDtypeStruct((n_out, CHUNK_W), rows.dtype), mesh=_sc_mesh(), scratch_types=[],
               name="sc_dispatch")
    def scatter(x_hbm, i_hbm, o_hbm):
        def body(x_vmem, *i_vmems):
            for i_vmem in i_vmems:
                pltpu.sync_copy(x_vmem, o_hbm.at[i_vmem.at[0]])

        pltpu.emit_pipeline(
            body,
            grid=(n_src // SC_WINDOW,),
            in_specs=[pl.BlockSpec((SC_WINDOW, CHUNK_W), index_map=lambda w: (w, 0))]
            + [pl.BlockSpec((1, SC_WINDOW), index_map=functools.partial(idx_block, k=k)) for k in range(copies)],
            out_specs=[],
            core_axis_name=("core", "subcore"),
            dimension_semantics=(pltpu.PARALLEL,),
        )(x_hbm, *([i_hbm] * copies))

    return scatter(rows, idx.reshape(1, n_idx))


def _sc_gather(table, idx):
    n_idx = idx.shape[0]

    @pl.kernel(out_type=jax.ShapeDtypeStruct((n_idx, CHUNK_W), table.dtype), mesh=_sc_mesh(),
               name="sc_combine_gather")
    def gather(t_hbm, i_hbm, o_hbm):
        def body(i_vmem, o_vmem):
            pltpu.sync_copy(t_hbm.at[i_vmem.at[0]], o_vmem)

        pltpu.emit_pipeline(
            body,
            grid=(n_idx // SC_WINDOW,),
            in_specs=[pl.BlockSpec((1, SC_WINDOW), index_map=lambda w: (0, w))],
            out_specs=[pl.BlockSpec((SC_WINDOW, CHUNK_W), index_map=lambda w: (w, 0))],
            core_axis_name=("core", "subcore"),
            dimension_semantics=(pltpu.PARALLEL,),
        )(i_hbm, o_hbm)

    return gather(table, idx.reshape(1, n_idx))


def _ffn_kernel(blk_e_ref, blk_valid_ref, n_used_ref, xs_ref, wg_ref, wu_ref, wd_ref, y_ref, wg_sc, wu_sc, wd_sc):
    b = pl.program_id(0)
    used = b < n_used_ref[0]
    new_expert = (b == 0) | (blk_e_ref[b] != blk_e_ref[jnp.maximum(b - 1, 0)])

    @pl.when(used & new_expert)
    def _():
        wg_sc[...] = wg_ref[...].astype(BF16)
        wu_sc[...] = wu_ref[...].astype(BF16)
        wd_sc[...] = wd_ref[...].astype(BF16)

    @pl.when(used)
    def _():
        x = _load_chunks(lambda c: xs_ref.at[c], valid_rows=blk_valid_ref[b]).astype(BF16)
        g = jnp.dot(x, wg_sc[...], preferred_element_type=F32)
        u = jnp.dot(x, wu_sc[...], preferred_element_type=F32)
        hmid = (g * jax.nn.sigmoid(g) * u).astype(BF16)
        _store_chunks(lambda c: y_ref.at[c], jnp.dot(hmid, wd_sc[...], preferred_element_type=F32))

    @pl.when(jnp.logical_not(used))
    def _():
        y_ref[...] = jnp.zeros_like(y_ref)


def _ffn(xs, blk_e, blk_valid, n_used, layer, wg, wu, wd, n_blocks):
    def blk(b, be, bv, nu):
        return jnp.maximum(jnp.minimum(b, nu[0] - 1), 0)

    def w_idx(b, be, bv, nu):
        return (layer, be[blk(b, be, bv, nu)], 0, 0)

    return pl.pallas_call(
        _ffn_kernel,
        name="expert_ffn",
        grid_spec=pltpu.PrefetchScalarGridSpec(
            num_scalar_prefetch=3,
            grid=(n_blocks,),
            in_specs=[
                pl.BlockSpec((ROW_CHUNKS, MOE_BLK, CHUNK_W), lambda b, be, bv, nu: (0, blk(b, be, bv, nu), 0)),
                pl.BlockSpec((None, None, D, D_EXPERT), w_idx),
                pl.BlockSpec((None, None, D, D_EXPERT), w_idx),
                pl.BlockSpec((None, None, D_EXPERT, D), w_idx),
            ],
            out_specs=pl.BlockSpec((ROW_CHUNKS, MOE_BLK, CHUNK_W), lambda b, be, bv, nu: (0, b, 0)),
            scratch_shapes=[pltpu.VMEM((D, D_EXPERT), BF16), pltpu.VMEM((D, D_EXPERT), BF16),
                            pltpu.VMEM((D_EXPERT, D), BF16)],
        ),
        out_shape=jax.ShapeDtypeStruct(xs.shape, U32),
        compiler_params=_params(("arbitrary",)),
    )(blk_e, blk_valid, n_used, xs, wg, wu, wd)


def _combine_kernel(x_ref, mod_ref, wt_ref, y_ref, sg_ref, su_ref, sd_ref, lg_ref, lb_ref, *o_refs, n_prompt_tiles):
    x = x_ref[...]
    hb = (x * (1.0 + mod_ref[4:5, :]) + mod_ref[3:4, :]).astype(BF16)
    g = jnp.dot(hb, sg_ref[...], preferred_element_type=F32)
    u = jnp.dot(hb, su_ref[...], preferred_element_type=F32)
    ff = jnp.dot((g * jax.nn.sigmoid(g) * u).astype(BF16), sd_ref[...], preferred_element_type=F32)
    for k in range(TOP_K):
        ff = ff + _load_chunks(lambda c: y_ref.at[c, k]) * wt_ref[:, k:k + 1]
    out = _layer_norm_rows(ALPHA * x + mod_ref[5:6, :] * ff, lg_ref[...], lb_ref[...])
    if len(o_refs) == 1:
        o_refs[0][...] = out
    else:
        is_prompt = pl.program_id(0) < n_prompt_tiles

        @pl.when(is_prompt)
        def _():
            o_refs[0][...] = out

        @pl.when(jnp.logical_not(is_prompt))
        def _():
            o_refs[1][...] = out


def _combine(geo, x, mod_l, wt, ytok, sg, su, sd, ln_g, ln_b, split=False):
    tb = TOK_TILE_L
    n_p = geo.t_prompt // tb
    full = lambda shape: pl.BlockSpec(shape, lambda i: (0,) * len(shape))
    if split:
        out_specs = [pl.BlockSpec((tb, D), lambda i: (jnp.minimum(i, n_p - 1), 0)),
                     pl.BlockSpec((tb, D), lambda i: (jnp.maximum(i - n_p, 0), 0))]
        out_shape = [jax.ShapeDtypeStruct((geo.t_prompt, D), F32), jax.ShapeDtypeStruct((geo.t_sample, D), F32)]
    else:
        out_specs = pl.BlockSpec((tb, D), lambda i: (i, 0))
        out_shape = jax.ShapeDtypeStruct((geo.t, D), F32)
    return pl.pallas_call(
        functools.partial(_combine_kernel, n_prompt_tiles=n_p),
        name="combine",
        grid=(geo.t // tb,),
        in_specs=[
            pl.BlockSpec((tb, D), lambda i: (i, 0)),
            pl.BlockSpec((None, 6, D), lambda i: (geo.cond_row(i, tb), 0, 0)),
            pl.BlockSpec((tb, TOP_K), lambda i: (i, 0)),
            pl.BlockSpec((ROW_CHUNKS, TOP_K, tb, CHUNK_W), lambda i: (0, 0, i, 0)),
            full((D, D_EXPERT)), full((D, D_EXPERT)), full((D_EXPERT, D)), full((1, D)), full((1, D)),
        ],
        out_specs=out_specs,
        out_shape=out_shape,
        compiler_params=_params(("arbitrary",)),
    )(x, mod_l, wt, ytok, sg.astype(BF16), su.astype(BF16), sd.astype(BF16),
      ln_g.reshape(1, D), ln_b.reshape(1, D))


def _moe_layer(geo, x, routing, mod_l, layer, wg, wu, wd, sg, su, sd, ln_g, ln_b, split=False):
    t = geo.t
    top_e, w, rank, counts, h = routing
    n_blocks = (t * TOP_K) // MOE_BLK + N_EXPERTS
    n_rows = n_blocks * MOE_BLK
    padded = (counts + MOE_BLK - 1) // MOE_BLK * MOE_BLK
    pend = jnp.cumsum(padded)
    pstart = (pend - padded).astype(jnp.int32)
    blk_e, blk_valid = _block_meta(pstart, counts, pend.astype(jnp.int32), n_blocks)
    n_used = (pend[-1:] // MOE_BLK).astype(jnp.int32)
    slots = _slots(geo, pstart, top_e, rank)
    idx = (slots.reshape(1, TOP_K * t) + (jnp.arange(ROW_CHUNKS, dtype=jnp.int32) * n_rows)[:, None]).reshape(-1)
    xs = _sc_scatter(h.reshape(ROW_CHUNKS * t, CHUNK_W), idx, ROW_CHUNKS * n_rows, TOP_K)
    yb = _ffn(xs.reshape(ROW_CHUNKS, n_rows, CHUNK_W), blk_e, blk_valid, n_used, layer, wg, wu, wd, n_blocks)
    ytok = _sc_gather(yb.reshape(ROW_CHUNKS * n_rows, CHUNK_W), idx)
    return _combine(geo, x, mod_l, w.T, ytok.reshape(ROW_CHUNKS, TOP_K, t, CHUNK_W), sg, su, sd, ln_g, ln_b,
                    split=split)


def _pos_embed(rows):
    quarter = D // 4
    omega = 1.0 / (POS_BASE ** (jnp.arange(quarter, dtype=F32) / quarter))
    r = jnp.arange(rows, dtype=F32)[:, None] * omega
    col = jnp.arange(GRID_W, dtype=F32)[:, None] * omega
    row_part = jnp.concatenate([jnp.sin(r), jnp.cos(r)], axis=-1)[:, None, :]
    col_part = jnp.concatenate([jnp.sin(col), jnp.cos(col)], axis=-1)[None, :, :]
    shape = (rows, GRID_W, 2 * quarter)
    return jnp.concatenate([jnp.broadcast_to(row_part, shape), jnp.broadcast_to(col_part, shape)],
                           axis=-1).reshape(rows * GRID_W, D)


def _mlstm_layer(geo, x, mod_l, j, a_w_in, a_b_gates, a_norm, a_w_out, ln_g, ln_b,
                 state_C, state_n, state_m, w_router, e_bias):
    q, kt, v, so, gr = _proj_a(geo, x, mod_l, a_w_in[j], a_b_gates[j])
    hp, c_p, n_p, m_p = _mlstm_scan(q, kt, v, gr, row0=0, n_seq=geo.n_prompt, seq_len=geo.prompt_len)
    ns = geo.n_sample
    n0 = jnp.pad(state_n[:, j].astype(F32)[..., None], ((0, 0),) * 4 + ((0, LANES - 1),))
    m0 = jnp.pad(state_m[:, j].astype(F32), ((0, 0), (0, 0), (0, SUBLANES - NH_A)))
    m0 = jnp.broadcast_to(m0[..., None], (ns, 2, SUBLANES, LANES))
    hs, _, _, _ = _mlstm_scan(q, kt, v, gr, row0=geo.t_prompt, n_seq=ns, seq_len=geo.sample_len,
                              state=(state_C[:, j].astype(F32), n0, m0))
    x1, routing = _out_a(geo, hp, hs, so, a_norm[j], a_w_out[j], x, mod_l, ln_g, ln_b, w_router, e_bias)
    return x1, routing, c_p, n_p[..., 0], m_p[:, :, :NH_A, 0]


def _hgrn_layer(geo, x, mod_l, j, lb_layer, b_w_in, b_norm, b_w_out, ln_g, ln_b, state_S, w_router, e_bias):
    q, pre, v, sg = _proj_b(geo, x, mod_l, b_w_in[j])
    lbd = lb_layer.reshape(2, NH_B, 1, DK_B)
    op, s_p = _hgrn_scan(q, pre, v, lbd, row0=0, n_seq=geo.n_prompt, seq_len=geo.prompt_len)
    os_, _ = _hgrn_scan(q, pre, v, lbd, row0=geo.t_prompt, n_seq=geo.n_sample, seq_len=geo.sample_len,
                        state=state_S[:, j].astype(F32))
    x1, routing = _out_b(geo, op, os_, sg, b_norm[j], b_w_out[j], x, mod_l, ln_g, ln_b, w_router, e_bias)
    return x1, routing, s_p


def kernel(x_prompt, x_sample, state_mlstm_C, state_mlstm_n, state_mlstm_m, state_hgrn_S, c, c_ctx, w_mod, b_mod, ln_g, ln_b, a_w_in, a_b_gates, a_norm, a_w_out, b_w_in, b_lb, b_norm, b_w_out, w_router, e_bias, w_gate, w_up, w_down, ws_gate, ws_up, ws_down):
    bp, sp, _ = x_prompt.shape
    bs, ss, _ = x_sample.shape
    cond = jnp.zeros((COND_ROWS, D), F32).at[0].set(c_ctx).at[1:1 + bs].set(c)
    mod = _modulation(cond, w_mod, b_mod)
    x = (x_prompt.reshape(-1, D), x_sample.reshape(-1, D), _pos_embed(ss // GRID_W))
    sm = jax.nn.softmax(b_lb.astype(F32), axis=0)
    lb_all = jnp.cumsum(sm, axis=0) - sm[0]
    geo = Geometry(bp, sp, bs, ss)
    x1, routing, new_c, new_n, new_m = _mlstm_layer(geo, x, mod[0], 0, a_w_in, a_b_gates, a_norm, a_w_out,
                                                    ln_g[0, 0], ln_b[0, 0], state_mlstm_C, state_mlstm_n,
                                                    state_mlstm_m, w_router[0], e_bias[0])
    x2 = _moe_layer(geo, x1, routing, mod[0], 0, w_gate, w_up, w_down, ws_gate[0], ws_up[0], ws_down[0],
                    ln_g[0, 1], ln_b[0, 1])
    x3, routing, new_s = _hgrn_layer(geo, x2, mod[1], 0, lb_all[1], b_w_in, b_norm, b_w_out, ln_g[1, 0], ln_b[1, 0],
                                     state_hgrn_S, w_router[1], e_bias[1])
    y_p, y_s = _moe_layer(geo, x3, routing, mod[1], 1, w_gate, w_up, w_down, ws_gate[1], ws_up[1], ws_down[1],
                          ln_g[1, 1], ln_b[1, 1], split=True)
    return (y_p.reshape(bp, sp, D), y_s.reshape(bs, ss, D), new_c[:, None], new_n[:, None], new_m[:, None],
            new_s[:, None])
```

```python
import functools
import math

import jax
import jax.numpy as jnp
from jax import lax
from jax.experimental import pallas as pl
from jax.experimental.pallas import tpu as pltpu
from jax.experimental.pallas import tpu_sc as plsc

F32 = jnp.float32
BF16 = jnp.bfloat16
HIGHEST = lax.Precision.HIGHEST

D = 1024
DEPTH = 2
GRID_W = 64
POS_BASE = 10000.0
EPS = 1e-6
ALPHA = (2.0 * DEPTH) ** 0.25
NH_A, DK_A, DV_A = 4, 128, 256
QK_A, V_A = NH_A * DK_A, NH_A * DV_A
NH_B, DK_B = 8, 128
N_EXPERTS, TOP_K, N_GROUPS, TOPK_GROUPS = 64, 8, 8, 4
GROUP_SIZE = N_EXPERTS // N_GROUPS
D_EXPERT = D // 4
ROUTED_SCALE = 2.5

LANES = 128
SUBLANES = 8
COND_ROWS = 8
TOK_TILE = 256
TOK_TILE_L = 512
SLOT_TILE = 2048
CHUNK_A = 256
VMEM_BYTES_V7X = 64 * 1024 * 1024
VMEM_LIMIT = VMEM_BYTES_V7X - 8 * 1024 * 1024

NT_DIMS = (((1,), (1,)), ((), ()))


def _params(sem):
    return pltpu.CompilerParams(dimension_semantics=sem, vmem_limit_bytes=VMEM_LIMIT)


def _split3(x):
    hi = x.astype(BF16)
    r = x - hi.astype(F32)
    mid = r.astype(BF16)
    lo = (r - mid.astype(F32)).astype(BF16)
    return hi, mid, lo


def _dot3(a_bf, x):
    hi, mid, lo = _split3(x)
    return (jnp.dot(a_bf, hi, preferred_element_type=F32)
            + jnp.dot(a_bf, mid, preferred_element_type=F32)
            + jnp.dot(a_bf, lo, preferred_element_type=F32))


def _dot3_r(x, a_bf):
    hi, mid, lo = _split3(x)
    return (jnp.dot(hi, a_bf, preferred_element_type=F32)
            + jnp.dot(mid, a_bf, preferred_element_type=F32)
            + jnp.dot(lo, a_bf, preferred_element_type=F32))


def _log_sigmoid(x):
    return jnp.minimum(x, 0.0) - jnp.log1p(jnp.exp(-jnp.abs(x)))


def _layer_norm_rows(x, g, b):
    mu = jnp.mean(x, axis=-1, keepdims=True)
    xc = x - mu
    var = jnp.mean(xc * xc, axis=-1, keepdims=True)
    return xc * lax.rsqrt(var + EPS) * g + b


class Geometry:
    def __init__(self, n_prompt, prompt_len, n_sample, sample_len):
        self.n_prompt, self.prompt_len = n_prompt, prompt_len
        self.n_sample, self.sample_len = n_sample, sample_len
        self.t_prompt = n_prompt * prompt_len
        self.t_sample = n_sample * sample_len
        self.t = self.t_prompt + self.t_sample
        assert self.t_prompt % TOK_TILE_L == 0 and sample_len % TOK_TILE_L == 0
        assert n_sample + 1 <= COND_ROWS

    def cond_row(self, tile, tile_rows):
        n_p = self.t_prompt // tile_rows
        return jnp.where(tile < n_p, 0, 1 + (tile - n_p) // (self.sample_len // tile_rows))


def _mod_kernel(cond_ref, w_ref, b_ref, o_ref):
    c = cond_ref[...]
    s = c * jax.nn.sigmoid(c)
    o_ref[0, 0] = jnp.dot(s, w_ref[0], precision=HIGHEST, preferred_element_type=F32) + b_ref[0, 0]


def _modulation(cond, w_mod, b_mod):
    out = pl.pallas_call(
        _mod_kernel,
        name="modulation",
        grid=(DEPTH, 6),
        in_specs=[
            pl.BlockSpec((COND_ROWS, D), lambda l, j: (0, 0)),
            pl.BlockSpec((1, D, D), lambda l, j: (l, 0, j)),
            pl.BlockSpec((1, 1, 1, D), lambda l, j: (l, j, 0, 0)),
        ],
        out_specs=pl.BlockSpec((1, 1, COND_ROWS, D), lambda l, j: (l, j, 0, 0)),
        out_shape=jax.ShapeDtypeStruct((DEPTH, 6, COND_ROWS, D), F32),
        compiler_params=_params(("arbitrary", "arbitrary")),
    )(cond, w_mod, b_mod.reshape(DEPTH, 6, 1, D))
    return out.transpose(0, 2, 1, 3)


def _embed_specs(geo, tb):
    n_p = geo.t_prompt // tb
    per_seq = geo.sample_len // tb
    return [pl.BlockSpec((tb, D), lambda i: (jnp.minimum(i, n_p - 1), 0)),
            pl.BlockSpec((tb, D), lambda i: (jnp.maximum(i - n_p, 0), 0)),
            pl.BlockSpec((tb, D), lambda i: (jnp.maximum(i - n_p, 0) % per_seq, 0))]


def _embed_tile(xp_ref, xs_ref, pos_ref, n_prompt_tiles):
    return jnp.where(pl.program_id(0) < n_prompt_tiles, xp_ref[...], xs_ref[...] + pos_ref[...])


def _proj_a_kernel(xp_ref, xs_ref, pos_ref, mod_ref, wq_ref, wkt_ref, wvo_ref, wgt_ref, bgt_ref,
                   q_ref, kt_ref, v_ref, so_ref, gr_ref, *, n_prompt_tiles):
    x = _embed_tile(xp_ref, xs_ref, pos_ref, n_prompt_tiles)
    h = x * (1.0 + mod_ref[1:2, :]) + mod_ref[0:1, :]
    hb = h.astype(BF16)
    q_ref[...] = jnp.dot(hb, wq_ref[...], preferred_element_type=F32).astype(BF16)
    kt = lax.dot_general(wkt_ref[...], hb, NT_DIMS, preferred_element_type=F32)
    kt_ref[...] = (kt * (DK_A ** -0.5)).astype(BF16)
    vo = jnp.dot(hb, wvo_ref[...], preferred_element_type=F32)
    v_ref[...] = vo[:, :V_A].astype(BF16)
    so_ref[...] = jax.nn.sigmoid(vo[:, V_A:]).astype(BF16)
    gr_ref[...] = lax.dot_general(wgt_ref[...], h, NT_DIMS, precision=HIGHEST,
                                  preferred_element_type=F32) + bgt_ref[...]


def _proj_a(geo, x, mod_l, w_in, b_gates):
    t = geo.t
    n_gate = 4 * NH_A
    wq = w_in[:, :QK_A].astype(BF16)
    wkt = w_in[:, QK_A:2 * QK_A].T.astype(BF16)
    wvo = w_in[:, 2 * QK_A:2 * QK_A + 2 * V_A].astype(BF16)
    wg = w_in[:, 2 * QK_A + 2 * V_A:]
    bg = b_gates.reshape(n_gate).astype(F32)
    tb = TOK_TILE_L
    full = lambda shape: pl.BlockSpec(shape, lambda i: (0,) * len(shape))
    return pl.pallas_call(
        functools.partial(_proj_a_kernel, n_prompt_tiles=geo.t_prompt // tb),
        name="proj_a",
        grid=(t // tb,),
        in_specs=_embed_specs(geo, tb) + [
            pl.BlockSpec((None, 6, D), lambda i: (geo.cond_row(i, tb), 0, 0)),
            full((D, QK_A)), full((QK_A, D)), full((D, 2 * V_A)), full((n_gate, D)), full((n_gate, 1)),
        ],
        out_specs=[
            pl.BlockSpec((tb, QK_A), lambda i: (i, 0)),
            pl.BlockSpec((QK_A, tb), lambda i: (0, i)),
            pl.BlockSpec((tb, V_A), lambda i: (i, 0)),
            pl.BlockSpec((tb, V_A), lambda i: (i, 0)),
            pl.BlockSpec((n_gate, tb), lambda i: (0, i)),
        ],
        out_shape=[
            jax.ShapeDtypeStruct((t, QK_A), BF16),
            jax.ShapeDtypeStruct((QK_A, t), BF16),
            jax.ShapeDtypeStruct((t, V_A), BF16),
            jax.ShapeDtypeStruct((t, V_A), BF16),
            jax.ShapeDtypeStruct((n_gate, t), F32),
        ],
        compiler_params=_params(("parallel",)),
    )(*x, mod_l, wq, wkt, wvo, wg.T, bg.reshape(n_gate, 1))


def _mlstm_scan_kernel(*refs, chunk, has_state):
    if has_state:
        (q_ref, kt_ref, v_ref, gr_ref, c0_ref, n0_ref, m0_ref,
         h_ref, c_out, n_out, m_out, c_sc, n_sc, m_sc) = refs
    else:
        (q_ref, kt_ref, v_ref, gr_ref,
         h_ref, c_out, n_out, m_out, c_sc, n_sc, m_sc) = refs
    L = chunk
    d = pl.program_id(1)
    c = pl.program_id(2)
    fwd = d == 0

    @pl.when(c == 0)
    def _():
        if has_state:
            c_sc[...] = c0_ref[0, 0]
            n_sc[...] = n0_ref[0, 0]
            m_sc[...] = m0_ref[0, 0]
        else:
            c_sc[...] = jnp.zeros_like(c_sc)
            n_sc[...] = jnp.zeros_like(n_sc)
            m_sc[...] = jnp.zeros_like(m_sc)

    row = lax.broadcasted_iota(jnp.int32, (L, L), 0)
    col = lax.broadcasted_iota(jnp.int32, (L, L), 1)
    sgn = 1 - 2 * d
    causal = (row - col) * sgn >= 0
    tri_t = ((col - row) * sgn >= 0).astype(BF16)

    gr = gr_ref[...]
    br_all = _dot3_r(_log_sigmoid(gr), tri_t)
    bc_all = jnp.concatenate([br_all, jnp.zeros((LANES - br_all.shape[0], L), F32)], axis=0).T
    ones_blk = (lax.broadcasted_iota(jnp.int32, (L, LANES), 1) == 0).astype(BF16)

    def gate_row(direction, gate, head):
        return (direction * 2 + gate) * NH_A + head

    for h in range(NH_A):
        ff, fb = gate_row(0, 1, h), gate_row(1, 1, h)
        gi, gb = gate_row(0, 0, h), gate_row(1, 0, h)
        b_c = jnp.where(fwd, bc_all[:, ff:ff + 1], bc_all[:, fb:fb + 1])
        b_r = jnp.where(fwd, br_all[ff:ff + 1, :], br_all[fb:fb + 1, :])
        i_r = jnp.where(fwd, gr[gi:gi + 1, :], gr[gb:gb + 1, :])
        bl = jnp.where(fwd, b_r[:, L - 1:L], b_r[:, 0:1])
        q = q_ref[:, h * DK_A:(h + 1) * DK_A]
        kt = kt_ref[h * DK_A:(h + 1) * DK_A, :]
        v = v_ref[:, h * DV_A:(h + 1) * DV_A]
        m = m_sc[h:h + 1, 0:1]
        cst = c_sc[h]
        nst = n_sc[h]

        a_r = i_r - b_r
        logd = jnp.where(causal, b_c + a_r, -jnp.inf)
        inter = b_c + m
        m_t = jnp.maximum(inter, jnp.max(logd, axis=1, keepdims=True))
        dmat = jnp.exp(logd - m_t)
        e_int = jnp.exp(inter - m_t)
        s = (jnp.dot(q, kt, preferred_element_type=F32) * dmat).astype(BF16)
        num = (jnp.dot(s, v, preferred_element_type=F32)
               + e_int * jnp.dot(q, cst.astype(BF16), preferred_element_type=F32))
        den = (jnp.dot(s, ones_blk, preferred_element_type=F32)
               + e_int * jnp.dot(q, nst.astype(BF16), preferred_element_type=F32))[:, 0:1]
        h_ref[:, h * DV_A:(h + 1) * DV_A] = (num / jnp.maximum(jnp.abs(den), jnp.exp(-m_t))).astype(BF16)

        logw = bl + a_r
        m_new = jnp.maximum(bl + m, jnp.max(logw, axis=1, keepdims=True))
        w = jnp.exp(logw - m_new)
        decay = jnp.exp(bl + m - m_new)
        kw = (kt.astype(F32) * w).astype(BF16)
        c_sc[h] = decay * cst + jnp.dot(kw, v, preferred_element_type=F32)
        n_sc[h] = decay * nst + jnp.dot(kw, ones_blk, preferred_element_type=F32)
        m_sc[h:h + 1, :] = jnp.broadcast_to(m_new, (1, LANES))

    @pl.when(c == pl.num_programs(2) - 1)
    def _():
        c_out[0, 0] = c_sc[...]
        n_out[0, 0] = n_sc[...]
        m_out[0, 0] = m_sc[...]


def _mlstm_scan(q, kt, v, gr, *, row0, n_seq, seq_len, state=None):
    L = CHUNK_A
    nc = seq_len // L
    blk0 = row0 // L

    def loc_blk(b, d, c):
        return b * nc + c + d * (nc - 1 - 2 * c)

    def tok_blk(b, d, c):
        return blk0 + loc_blk(b, d, c)

    in_specs = [
        pl.BlockSpec((L, QK_A), lambda b, d, c: (tok_blk(b, d, c), 0)),
        pl.BlockSpec((QK_A, L), lambda b, d, c: (0, tok_blk(b, d, c))),
        pl.BlockSpec((L, V_A), lambda b, d, c: (tok_blk(b, d, c), 0)),
        pl.BlockSpec((4 * NH_A, L), lambda b, d, c: (0, tok_blk(b, d, c))),
    ]
    args = [q, kt, v, gr]
    if state is not None:
        in_specs += [
            pl.BlockSpec((1, 1, NH_A, DK_A, DV_A), lambda b, d, c: (b, d, 0, 0, 0)),
            pl.BlockSpec((1, 1, NH_A, DK_A, LANES), lambda b, d, c: (b, d, 0, 0, 0)),
            pl.BlockSpec((1, 1, SUBLANES, LANES), lambda b, d, c: (b, d, 0, 0)),
        ]
        args += list(state)
    return pl.pallas_call(
        functools.partial(_mlstm_scan_kernel, chunk=L, has_state=state is not None),
        name="mlstm_scan_seeded" if state is not None else "mlstm_scan",
        grid=(n_seq, 2, nc),
        in_specs=in_specs,
        out_specs=[
            pl.BlockSpec((None, L, V_A), lambda b, d, c: (d, loc_blk(b, d, c), 0)),
            pl.BlockSpec((1, 1, NH_A, DK_A, DV_A), lambda b, d, c: (b, d, 0, 0, 0)),
            pl.BlockSpec((1, 1, NH_A, DK_A, LANES), lambda b, d, c: (b, d, 0, 0, 0)),
            pl.BlockSpec((1, 1, SUBLANES, LANES), lambda b, d, c: (b, d, 0, 0)),
        ],
        out_shape=[
            jax.ShapeDtypeStruct((2, n_seq * seq_len, V_A), BF16),
            jax.ShapeDtypeStruct((n_seq, 2, NH_A, DK_A, DV_A), F32),
            jax.ShapeDtypeStruct((n_seq, 2, NH_A, DK_A, LANES), F32),
            jax.ShapeDtypeStruct((n_seq, 2, SUBLANES, LANES), F32),
        ],
        scratch_shapes=[
            pltpu.VMEM((NH_A, DK_A, DV_A), F32),
            pltpu.VMEM((NH_A, DK_A, LANES), F32),
            pltpu.VMEM((SUBLANES, LANES), F32),
        ],
        compiler_params=_params(("parallel", "parallel", "arbitrary")),
    )(*args)


def _out_a_kernel(hp_ref, hs_ref, so_ref, nw_ref, w_ref, xp_ref, xs_ref, pos_ref, mod_ref, lg_ref, lb_ref,
                  wrt_ref, eb_ref, o_ref, *route_refs, n_prompt_tiles):
    is_prompt = pl.program_id(0) < n_prompt_tiles
    x = _embed_tile(xp_ref, xs_ref, pos_ref, n_prompt_tiles)
    y = jnp.where(is_prompt, hp_ref[0].astype(F32) + hp_ref[1].astype(F32),
                  hs_ref[0].astype(F32) + hs_ref[1].astype(F32))
    parts = []
    for h in range(NH_A):
        yh = y[:, h * DV_A:(h + 1) * DV_A]
        mu = jnp.mean(yh, axis=-1, keepdims=True)
        yc = yh - mu
        var = jnp.mean(yc * yc, axis=-1, keepdims=True)
        parts.append(yc * lax.rsqrt(var + EPS))
    yn = jnp.concatenate(parts, axis=-1) * nw_ref[...] * so_ref[...].astype(F32)
    out = jnp.dot(yn.astype(BF16), w_ref[...], preferred_element_type=F32)
    x1 = _layer_norm_rows(ALPHA * x + mod_ref[2:3, :] * out, lg_ref[...], lb_ref[...])
    o_ref[...] = x1
    _route_tile(x1, mod_ref, wrt_ref, eb_ref, *route_refs)


def _out_a(geo, h_prompt, h_sample, so, norm_w, w_out, x, mod_l, ln_g, ln_b, w_router, e_bias):
    t = geo.t
    tb = TOK_TILE_L
    n_p = geo.t_prompt // tb
    full = lambda shape: pl.BlockSpec(shape, lambda i: (0,) * len(shape))
    rio = _RouterIO(t, tb, w_router, e_bias)
    outs = pl.pallas_call(
        functools.partial(_out_a_kernel, n_prompt_tiles=n_p),
        name="out_a",
        grid=(t // tb,),
        in_specs=[
            pl.BlockSpec((2, tb, V_A), lambda i: (0, jnp.minimum(i, n_p - 1), 0)),
            pl.BlockSpec((2, tb, V_A), lambda i: (0, jnp.maximum(i - n_p, 0), 0)),
            pl.BlockSpec((tb, V_A), lambda i: (i, 0)),
            full((1, V_A)), full((V_A, D)),
        ] + _embed_specs(geo, tb) + [
            pl.BlockSpec((None, 6, D), lambda i: (geo.cond_row(i, tb), 0, 0)),
            full((1, D)), full((1, D)),
        ] + rio.in_specs,
        out_specs=[pl.BlockSpec((tb, D), lambda i: (i, 0))] + rio.out_specs,
        out_shape=[jax.ShapeDtypeStruct((t, D), F32)] + rio.out_shape,
        scratch_shapes=rio.scratch,
        compiler_params=_params(("arbitrary",)),
    )(h_prompt, h_sample, so, norm_w.reshape(1, V_A).astype(F32), w_out.astype(BF16), *x, mod_l,
      ln_g.reshape(1, D), ln_b.reshape(1, D), *rio.inputs)
    return outs[0], _RouterIO.unpack(outs[1:])


def _proj_b_kernel(x_ref, mod_ref, w_ref, q_ref, pre_ref, v_ref, sg_ref):
    h = x_ref[...] * (1.0 + mod_ref[1:2, :]) + mod_ref[0:1, :]
    z = jnp.dot(h.astype(BF16), w_ref[...], preferred_element_type=F32)
    for hd in range(NH_B):
        lo = hd * DK_B
        qh = z[:, lo:lo + DK_B]
        q_ref[hd] = qh * jax.nn.sigmoid(qh)
        pre_ref[0, hd] = z[:, D + lo:D + lo + DK_B]
        pre_ref[1, hd] = z[:, 2 * D + lo:2 * D + lo + DK_B]
        v_ref[hd] = z[:, 3 * D + lo:3 * D + lo + DK_B].astype(BF16)
    g = z[:, 4 * D:]
    sg_ref[...] = (g * jax.nn.sigmoid(g)).astype(BF16)


def _proj_b(geo, x, mod_l, w_in):
    t = geo.t
    tb = TOK_TILE
    return pl.pallas_call(
        _proj_b_kernel,
        name="proj_b",
        grid=(t // tb,),
        in_specs=[
            pl.BlockSpec((tb, D), lambda i: (i, 0)),
            pl.BlockSpec((None, 6, D), lambda i: (geo.cond_row(i, tb), 0, 0)),
            pl.BlockSpec((D, 5 * D), lambda i: (0, 0)),
        ],
        out_specs=[
            pl.BlockSpec((NH_B, tb, DK_B), lambda i: (0, i, 0)),
            pl.BlockSpec((2, NH_B, tb, DK_B), lambda i: (0, 0, i, 0)),
            pl.BlockSpec((NH_B, tb, DK_B), lambda i: (0, i, 0)),
            pl.BlockSpec((tb, D), lambda i: (i, 0)),
        ],
        out_shape=[
            jax.ShapeDtypeStruct((NH_B, t, DK_B), F32),
            jax.ShapeDtypeStruct((2, NH_B, t, DK_B), F32),
            jax.ShapeDtypeStruct((NH_B, t, DK_B), BF16),
            jax.ShapeDtypeStruct((t, D), BF16),
        ],
        compiler_params=_params(("parallel",)),
    )(x, mod_l, w_in.astype(BF16))


CHUNK_B = 128
BAND = SUBLANES // 2
TN_DIMS = (((0,), (0,)), ((), ()))


def _hgrn_head(q, pre, lbv, v_bf, st, fwd):
    L = q.shape[0]
    sg = jax.nn.sigmoid(pre)
    f = lbv + (1.0 - lbv) * sg
    lf = jnp.log(f)
    kk = (1.0 - lbv) * (1.0 - sg)
    row = lax.broadcasted_iota(jnp.int32, (L, L), 0)
    col = lax.broadcasted_iota(jnp.int32, (L, L), 1)
    tri = ((row >= col) if fwd else (row <= col)).astype(BF16)
    b = _dot3(tri, lf)
    tpos = lax.broadcasted_iota(jnp.int32, (L, DK_B), 0)
    blk_bits = row ^ col
    lag = jnp.where(blk_bits < BAND, (row - col) if fwd else (col - row), -1)

    step = 1 if fwd else L - 1
    att = jnp.where(lag == 0, jnp.sum(q * kk, axis=1, keepdims=True), 0.0)
    f_r, kk_r, g = f, kk, f
    for dl in range(1, BAND):
        if dl > 1:
            f_r = pltpu.roll(f_r, step, 0)
            g = g * f_r
        kk_r = pltpu.roll(kk_r, step, 0)
        att = jnp.where(lag == dl, jnp.sum(q * kk_r * g, axis=1, keepdims=True), att)

    w = BAND
    while w < L:
        nb = L // (2 * w)
        b3 = b.reshape(nb, 2 * w, DK_B)
        edge = (b3[:, w - 1:w, :] if fwd else b3[:, w:w + 1, :])
        bmid = jnp.broadcast_to(edge, (nb, 2 * w, DK_B)).reshape(L, DK_B)
        second = (tpos & w) != 0
        t_side = second if fwd else jnp.logical_not(second)
        e = jnp.exp(jnp.where(t_side, b - bmid, bmid - b))
        qt = jnp.where(t_side, q * e, 0.0).astype(BF16)
        ks = jnp.where(t_side, 0.0, kk * e).astype(BF16)
        a = lax.dot_general(qt, ks, NT_DIMS, preferred_element_type=F32)
        att = att + jnp.where(blk_bits < 2 * w, a, 0.0)
        w *= 2
    o = jnp.dot(att.astype(BF16), v_bf, preferred_element_type=F32)

    bl = b[L - 1:L, :] if fwd else b[0:1, :]
    o = o + lax.dot_general((q * jnp.exp(b)).astype(BF16), st.astype(BF16), NT_DIMS, preferred_element_type=F32)
    kd = (kk * jnp.exp(bl - b)).astype(BF16)
    st_new = jnp.exp(bl) * st + lax.dot_general(v_bf, kd, TN_DIMS, preferred_element_type=F32)
    return o, st_new


def _hgrn_scan_kernel(*refs, has_state):
    if has_state:
        q_ref, pre_ref, v_ref, lb_ref, s0_ref, o_ref, s_out, st_sc = refs
    else:
        q_ref, pre_ref, v_ref, lb_ref, o_ref, s_out, st_sc = refs
    d = pl.program_id(1)
    c = pl.program_id(2)

    @pl.when(c == 0)
    def _():
        if has_state:
            for hd in range(NH_B):
                st_sc[hd] = s0_ref[0, 0, hd].T
        else:
            st_sc[...] = jnp.zeros_like(st_sc)

    def run(fwd):
        def head(hd, carry):
            o, st_new = _hgrn_head(q_ref[hd], pre_ref[hd], lb_ref[hd], v_ref[hd], st_sc[hd], fwd)
            o_ref[hd] = o.astype(BF16)
            st_sc[hd] = st_new
            return carry
        lax.fori_loop(0, NH_B, head, 0, unroll=8)

    @pl.when(d == 0)
    def _():
        run(True)

    @pl.when(d == 1)
    def _():
        run(False)

    @pl.when(c == pl.num_programs(2) - 1)
    def _():
        for hd in range(NH_B):
            s_out[0, 0, hd] = st_sc[hd].T


def _hgrn_scan(q, pre, v, lbd, *, row0, n_seq, seq_len, state=None):
    L = CHUNK_B
    nc = seq_len // L
    blk0 = row0 // L

    def loc_blk(b, d, c):
        return b * nc + c + d * (nc - 1 - 2 * c)

    def tok_blk(b, d, c):
        return blk0 + loc_blk(b, d, c)

    in_specs = [
        pl.BlockSpec((NH_B, L, DK_B), lambda b, d, c: (0, tok_blk(b, d, c), 0)),
        pl.BlockSpec((None, NH_B, L, DK_B), lambda b, d, c: (d, 0, tok_blk(b, d, c), 0)),
        pl.BlockSpec((NH_B, L, DK_B), lambda b, d, c: (0, tok_blk(b, d, c), 0)),
        pl.BlockSpec((None, NH_B, 1, DK_B), lambda b, d, c: (d, 0, 0, 0)),
    ]
    args = [q, pre, v, lbd]
    if state is not None:
        in_specs.append(pl.BlockSpec((1, 1, NH_B, DK_B, DK_B), lambda b, d, c: (b, d, 0, 0, 0)))
        args.append(state)
    return pl.pallas_call(
        functools.partial(_hgrn_scan_kernel, has_state=state is not None),
        name="hgrn_scan_seeded" if state is not None else "hgrn_scan",
        grid=(n_seq, 2, nc),
        in_specs=in_specs,
        out_specs=[
            pl.BlockSpec((None, NH_B, L, DK_B), lambda b, d, c: (d, 0, loc_blk(b, d, c), 0)),
            pl.BlockSpec((1, 1, NH_B, DK_B, DK_B), lambda b, d, c: (b, d, 0, 0, 0)),
        ],
        out_shape=[
            jax.ShapeDtypeStruct((2, NH_B, n_seq * seq_len, DK_B), BF16),
            jax.ShapeDtypeStruct((n_seq, 2, NH_B, DK_B, DK_B), F32),
        ],
        scratch_shapes=[pltpu.VMEM((NH_B, DK_B, DK_B), F32)],
        compiler_params=_params(("parallel", "parallel", "arbitrary")),
    )(*args)


def _out_b_kernel(op_ref, os_ref, sg_ref, nw_ref, w_ref, x_ref, mod_ref, lg_ref, lb_ref, wrt_ref, eb_ref,
                  out_ref, *route_refs, n_prompt_tiles):
    is_prompt = pl.program_id(0) < n_prompt_tiles
    parts = []
    for hd in range(NH_B):
        y = jnp.where(is_prompt, op_ref[0, hd].astype(F32) + op_ref[1, hd].astype(F32),
                      os_ref[0, hd].astype(F32) + os_ref[1, hd].astype(F32))
        parts.append(y * lax.rsqrt(jnp.mean(y * y, axis=-1, keepdims=True) + EPS))
    yn = jnp.concatenate(parts, axis=-1) * nw_ref[...] * sg_ref[...].astype(F32)
    out = jnp.dot(yn.astype(BF16), w_ref[...], preferred_element_type=F32)
    x1 = _layer_norm_rows(ALPHA * x_ref[...] + mod_ref[2:3, :] * out, lg_ref[...], lb_ref[...])
    out_ref[...] = x1
    _route_tile(x1, mod_ref, wrt_ref, eb_ref, *route_refs)


def _out_b(geo, o_prompt, o_sample, sg, norm_w, w_out, x, mod_l, ln_g, ln_b, w_router, e_bias):
    t = geo.t
    tb = TOK_TILE_L
    n_p = geo.t_prompt // tb
    full = lambda shape: pl.BlockSpec(shape, lambda i: (0,) * len(shape))
    rio = _RouterIO(t, tb, w_router, e_bias)
    outs = pl.pallas_call(
        functools.partial(_out_b_kernel, n_prompt_tiles=n_p),
        name="out_b",
        grid=(t // tb,),
        in_specs=[
            pl.BlockSpec((2, NH_B, tb, DK_B), lambda i: (0, 0, jnp.minimum(i, n_p - 1), 0)),
            pl.BlockSpec((2, NH_B, tb, DK_B), lambda i: (0, 0, jnp.maximum(i - n_p, 0), 0)),
            pl.BlockSpec((tb, D), lambda i: (i, 0)),
            full((1, D)), full((D, D)),
            pl.BlockSpec((tb, D), lambda i: (i, 0)),
            pl.BlockSpec((None, 6, D), lambda i: (geo.cond_row(i, tb), 0, 0)),
            full((1, D)), full((1, D)),
        ] + rio.in_specs,
        out_specs=[pl.BlockSpec((tb, D), lambda i: (i, 0))] + rio.out_specs,
        out_shape=[jax.ShapeDtypeStruct((t, D), F32)] + rio.out_shape,
        scratch_shapes=rio.scratch,
        compiler_params=_params(("arbitrary",)),
    )(o_prompt, o_sample, sg, norm_w.reshape(1, D).astype(F32), w_out.astype(BF16), x, mod_l,
      ln_g.reshape(1, D), ln_b.reshape(1, D), *rio.inputs)
    return outs[0], _RouterIO.unpack(outs[1:])


MOE_BLK = 1024
U32 = jnp.uint32
ROW_WORDS = D // 2
CHUNK_W = 256
ROW_CHUNKS = ROW_WORDS // CHUNK_W
SC_WINDOW = 128


def _pack_rows(x):
    return pltpu.pack_elementwise([x[:, :ROW_WORDS], x[:, ROW_WORDS:]], packed_dtype=BF16)


def _unpack_rows(words):
    return jnp.concatenate([pltpu.unpack_elementwise(words, index=i, packed_dtype=BF16, unpacked_dtype=F32)
                            for i in range(2)], axis=1)


def _store_chunks(chunk_ref, x):
    words = _pack_rows(x)
    for c in range(ROW_CHUNKS):
        chunk_ref(c)[...] = words[:, c * CHUNK_W:(c + 1) * CHUNK_W]


def _load_chunks(chunk_ref, valid_rows=None):
    words = jnp.concatenate([chunk_ref(c)[...] for c in range(ROW_CHUNKS)], axis=1)
    if valid_rows is not None:
        row = lax.broadcasted_iota(jnp.int32, (words.shape[0], 1), 0)
        words = jnp.where(row < valid_rows, words, jnp.uint32(0))
    return _unpack_rows(words)


def _first_index(hit, iota, size, axis):
    return jnp.min(jnp.where(hit, iota, size), axis=axis, keepdims=True)


def _route_tile(x, mod_ref, wrt_ref, eb_ref, e_ref, w_ref, r_ref, cnt_ref, h_ref, cnt_sc):
    i = pl.program_id(0)
    tb = x.shape[0]

    @pl.when(i == 0)
    def _():
        cnt_sc[...] = jnp.zeros_like(cnt_sc)

    h = x * (1.0 + mod_ref[4:5, :]) + mod_ref[3:4, :]
    _store_chunks(lambda c: h_ref.at[c], h)
    logits = lax.dot_general(wrt_ref[...], h, NT_DIMS, precision=HIGHEST, preferred_element_type=F32)
    scores = jax.nn.sigmoid(logits)
    sel = scores + eb_ref[...]

    g3 = sel.reshape(N_GROUPS, GROUP_SIZE, tb)
    io3 = lax.broadcasted_iota(jnp.int32, g3.shape, 1)
    m1 = jnp.max(g3, axis=1, keepdims=True)
    first = _first_index(g3 == m1, io3, GROUP_SIZE, 1)
    m2 = jnp.max(jnp.where(io3 == first, -jnp.inf, g3), axis=1, keepdims=True)
    gscore = (m1 + m2).reshape(N_GROUPS, tb)

    iog = lax.broadcasted_iota(jnp.int32, gscore.shape, 0)
    gmask = jnp.zeros(gscore.shape, F32)
    for _ in range(TOPK_GROUPS):
        gm = jnp.max(gscore, axis=0, keepdims=True)
        pick = iog == _first_index(gscore == gm, iog, N_GROUPS, 0)
        gmask = jnp.where(pick, 1.0, gmask)
        gscore = jnp.where(pick, -jnp.inf, gscore)
    emask = jnp.broadcast_to(gmask.reshape(N_GROUPS, 1, tb), (N_GROUPS, GROUP_SIZE, tb)).reshape(N_EXPERTS, tb)
    cand = jnp.where(emask > 0.0, sel, -jnp.inf)

    ioe = lax.broadcasted_iota(jnp.int32, cand.shape, 0)
    picks, wts = [], []
    onehot = jnp.zeros(cand.shape, F32)
    for _ in range(TOP_K):
        cm = jnp.max(cand, axis=0, keepdims=True)
        idx = _first_index(cand == cm, ioe, N_EXPERTS, 0)
        pick = ioe == idx
        picks.append(pick)
        wts.append(jnp.sum(jnp.where(pick, scores, 0.0), axis=0, keepdims=True))
        onehot = onehot + pick.astype(F32)
        cand = jnp.where(pick, -jnp.inf, cand)
        e_ref[pl.ds(len(picks) - 1, 1), :] = idx
    wsum = wts[0]
    for wk in wts[1:]:
        wsum = wsum + wk
    for k in range(TOP_K):
        w_ref[pl.ds(k, 1), :] = wts[k] / wsum * ROUTED_SCALE

    r_io = lax.broadcasted_iota(jnp.int32, (tb, tb), 0)
    c_io = lax.broadcasted_iota(jnp.int32, (tb, tb), 1)
    before = (r_io < c_io).astype(BF16)
    rank = cnt_sc[:, 0:1] + jnp.dot(onehot.astype(BF16), before, preferred_element_type=F32)
    for k in range(TOP_K):
        r_ref[pl.ds(k, 1), :] = jnp.sum(jnp.where(picks[k], rank, 0.0), axis=0, keepdims=True).astype(jnp.int32)
    cnt_sc[...] = cnt_sc[...] + jnp.sum(onehot, axis=1, keepdims=True)
    cnt_ref[...] = cnt_sc[...]


class _RouterIO:
    def __init__(self, t, tb, w_router, e_bias):
        full = lambda shape: pl.BlockSpec(shape, lambda i: (0,) * len(shape))
        self.inputs = [w_router.T.astype(F32), e_bias.reshape(N_EXPERTS, 1).astype(F32)]
        self.in_specs = [full((N_EXPERTS, D)), full((N_EXPERTS, 1))]
        self.out_specs = [
            pl.BlockSpec((TOP_K, tb), lambda i: (0, i)),
            pl.BlockSpec((TOP_K, tb), lambda i: (0, i)),
            pl.BlockSpec((TOP_K, tb), lambda i: (0, i)),
            full((N_EXPERTS, LANES)),
            pl.BlockSpec((ROW_CHUNKS, tb, CHUNK_W), lambda i: (0, i, 0)),
        ]
        self.out_shape = [
            jax.ShapeDtypeStruct((TOP_K, t), jnp.int32),
            jax.ShapeDtypeStruct((TOP_K, t), F32),
            jax.ShapeDtypeStruct((TOP_K, t), jnp.int32),
            jax.ShapeDtypeStruct((N_EXPERTS, LANES), F32),
            jax.ShapeDtypeStruct((ROW_CHUNKS, t, CHUNK_W), U32),
        ]
        self.scratch = [pltpu.VMEM((N_EXPERTS, LANES), F32)]

    @staticmethod
    def unpack(outs):
        e, w, r, cnt, h = outs
        return e, w, r, cnt[:, 0].astype(jnp.int32), h


def _slot_kernel(pstart_ref, e_ref, r_ref, o_ref):
    e = e_ref[...]
    slot = r_ref[...]
    for x in range(N_EXPERTS):
        slot = slot + jnp.where(e == x, pstart_ref[x], 0)
    o_ref[...] = slot


def _slots(geo, pstart, top_e, rank):
    tb = math.gcd(SLOT_TILE, geo.t)
    return pl.pallas_call(
        _slot_kernel,
        name="slots",
        grid_spec=pltpu.PrefetchScalarGridSpec(
            num_scalar_prefetch=1,
            grid=(geo.t // tb,),
            in_specs=[pl.BlockSpec((TOP_K, tb), lambda i, p: (0, i)),
                      pl.BlockSpec((TOP_K, tb), lambda i, p: (0, i))],
            out_specs=pl.BlockSpec((TOP_K, tb), lambda i, p: (0, i)),
        ),
        out_shape=jax.ShapeDtypeStruct((TOP_K, geo.t), jnp.int32),
        compiler_params=_params(("parallel",)),
    )(pstart, top_e, rank)


def _block_meta_kernel(pstart_ref, counts_ref, pend_ref, e_ref, v_ref):
    row0 = lax.broadcasted_iota(jnp.int32, e_ref.shape, 1) * MOE_BLK
    blk_e = jnp.zeros(e_ref.shape, jnp.int32)
    for x in range(N_EXPERTS):
        blk_e = blk_e + jnp.where(pend_ref[x] <= row0, 1, 0)
    blk_e = jnp.minimum(blk_e, N_EXPERTS - 1)
    last = jnp.zeros(e_ref.shape, jnp.int32)
    for x in range(N_EXPERTS):
        last = last + jnp.where(blk_e == x, pstart_ref[x] + counts_ref[x], 0)
    e_ref[...] = blk_e
    v_ref[...] = jnp.clip(last - row0, 0, MOE_BLK)


def _block_meta(pstart, counts, pend, n_blocks):
    e, v = pl.pallas_call(
        _block_meta_kernel,
        name="block_meta",
        grid_spec=pltpu.PrefetchScalarGridSpec(
            num_scalar_prefetch=3,
            grid=(1,),
            in_specs=[],
            out_specs=[pl.BlockSpec((1, n_blocks), lambda i, a, b, c: (0, 0)),
                       pl.BlockSpec((1, n_blocks), lambda i, a, b, c: (0, 0))],
        ),
        out_shape=[jax.ShapeDtypeStruct((1, n_blocks), jnp.int32), jax.ShapeDtypeStruct((1, n_blocks), jnp.int32)],
        compiler_params=_params(("arbitrary",)),
    )(pstart, counts, pend)
    return e[0], v[0]


def _sc_mesh():
    return plsc.VectorSubcoreMesh(core_axis_name="core", subcore_axis_name="subcore")


def _sc_scatter(rows, idx, n_out, copies):
    n_src = rows.shape[0]
    n_idx = idx.shape[0]
    groups = ROW_CHUNKS
    win_per_group = n_src // groups // SC_WINDOW

    def idx_block(w, k):
        return (0, ((w // win_per_group) * copies + k) * win_per_group + w % win_per_group)

    @pl.kernel(out_type=jax.ShapeDtypeStruct((n_out, CHUNK_W), rows.dtype), mesh=_sc_mesh(), scratch_types=[],
               name="sc_dispatch")
    def scatter(x_hbm, i_hbm, o_hbm):
        def body(x_vmem, *i_vmems):
            for i_vmem in i_vmems:
                pltpu.sync_copy(x_vmem, o_hbm.at[i_vmem.at[0]])

        pltpu.emit_pipeline(
            body,
            grid=(n_src // SC_WINDOW,),
            in_specs=[pl.BlockSpec((SC_WINDOW, CHUNK_W), index_map=lambda w: (w, 0))]
            + [pl.BlockSpec((1, SC_WINDOW), index_map=functools.partial(idx_block, k=k)) for k in range(copies)],
            out_specs=[],
            core_axis_name=("core", "subcore"),
            dimension_semantics=(pltpu.PARALLEL,),
        )(x_hbm, *([i_hbm] * copies))

    return scatter(rows, idx.reshape(1, n_idx))


def _sc_gather(table, idx):
    n_idx = idx.shape[0]

    @pl.kernel(out_type=jax.ShapeDtypeStruct((n_idx, CHUNK_W), table.dtype), mesh=_sc_mesh(),
               name="sc_combine_gather")
    def gather(t_hbm, i_hbm, o_hbm):
        def body(i_vmem, o_vmem):
            pltpu.sync_copy(t_hbm.at[i_vmem.at[0]], o_vmem)

        pltpu.emit_pipeline(
            body,
            grid=(n_idx // SC_WINDOW,),
            in_specs=[pl.BlockSpec((1, SC_WINDOW), index_map=lambda w: (0, w))],
            out_specs=[pl.BlockSpec((SC_WINDOW, CHUNK_W), index_map=lambda w: (w, 0))],
            core_axis_name=("core", "subcore"),
            dimension_semantics=(pltpu.PARALLEL,),
        )(i_hbm, o_hbm)

    return gather(table, idx.reshape(1, n_idx))


def _ffn_kernel(blk_e_ref, blk_valid_ref, n_used_ref, xs_ref, wg_ref, wu_ref, wd_ref, y_ref, wg_sc, wu_sc, wd_sc):
    b = pl.program_id(0)
    used = b < n_used_ref[0]
    new_expert = (b == 0) | (blk_e_ref[b] != blk_e_ref[jnp.maximum(b - 1, 0)])

    @pl.when(used & new_expert)
    def _():
        wg_sc[...] = wg_ref[...].astype(BF16)
        wu_sc[...] = wu_ref[...].astype(BF16)
        wd_sc[...] = wd_ref[...].astype(BF16)

    @pl.when(used)
    def _():
        x = _load_chunks(lambda c: xs_ref.at[c], valid_rows=blk_valid_ref[b]).astype(BF16)
        g = jnp.dot(x, wg_sc[...], preferred_element_type=F32)
        u = jnp.dot(x, wu_sc[...], preferred_element_type=F32)
        hmid = (g * jax.nn.sigmoid(g) * u).astype(BF16)
        _store_chunks(lambda c: y_ref.at[c], jnp.dot(hmid, wd_sc[...], preferred_element_type=F32))

    @pl.when(jnp.logical_not(used))
    def _():
        y_ref[...] = jnp.zeros_like(y_ref)


def _ffn(xs, blk_e, blk_valid, n_used, layer, wg, wu, wd, n_blocks):
    def blk(b, be, bv, nu):
        return jnp.maximum(jnp.minimum(b, nu[0] - 1), 0)

    def w_idx(b, be, bv, nu):
        return (layer, be[blk(b, be, bv, nu)], 0, 0)

    return pl.pallas_call(
        _ffn_kernel,
        name="expert_ffn",
        grid_spec=pltpu.PrefetchScalarGridSpec(
            num_scalar_prefetch=3,
            grid=(n_blocks,),
            in_specs=[
                pl.BlockSpec((ROW_CHUNKS, MOE_BLK, CHUNK_W), lambda b, be, bv, nu: (0, blk(b, be, bv, nu), 0)),
                pl.BlockSpec((None, None, D, D_EXPERT), w_idx),
                pl.BlockSpec((None, None, D, D_EXPERT), w_idx),
                pl.BlockSpec((None, None, D_EXPERT, D), w_idx),
            ],
            out_specs=pl.BlockSpec((ROW_CHUNKS, MOE_BLK, CHUNK_W), lambda b, be, bv, nu: (0, b, 0)),
            scratch_shapes=[pltpu.VMEM((D, D_EXPERT), BF16), pltpu.VMEM((D, D_EXPERT), BF16),
                            pltpu.VMEM((D_EXPERT, D), BF16)],
        ),
        out_shape=jax.ShapeDtypeStruct(xs.shape, U32),
        compiler_params=_params(("arbitrary",)),
    )(blk_e, blk_valid, n_used, xs, wg, wu, wd)


def _combine_kernel(x_ref, mod_ref, wt_ref, y_ref, sg_ref, su_ref, sd_ref, lg_ref, lb_ref, *o_refs, n_prompt_tiles):
    x = x_ref[...]
    hb = (x * (1.0 + mod_ref[4:5, :]) + mod_ref[3:4, :]).astype(BF16)
    g = jnp.dot(hb, sg_ref[...], preferred_element_type=F32)
    u = jnp.dot(hb, su_ref[...], preferred_element_type=F32)
    ff = jnp.dot((g * jax.nn.sigmoid(g) * u).astype(BF16), sd_ref[...], preferred_element_type=F32)
    for k in range(TOP_K):
        ff = ff + _load_chunks(lambda c: y_ref.at[c, k]) * wt_ref[:, k:k + 1]
    out = _layer_norm_rows(ALPHA * x + mod_ref[5:6, :] * ff, lg_ref[...], lb_ref[...])
    if len(o_refs) == 1:
        o_refs[0][...] = out
    else:
        is_prompt = pl.program_id(0) < n_prompt_tiles

        @pl.when(is_prompt)
        def _():
            o_refs[0][...] = out

        @pl.when(jnp.logical_not(is_prompt))
        def _():
            o_refs[1][...] = out


def _combine(geo, x, mod_l, wt, ytok, sg, su, sd, ln_g, ln_b, split=False):
    tb = TOK_TILE_L
    n_p = geo.t_prompt // tb
    full = lambda shape: pl.BlockSpec(shape, lambda i: (0,) * len(shape))
    if split:
        out_specs = [pl.BlockSpec((tb, D), lambda i: (jnp.minimum(i, n_p - 1), 0)),
                     pl.BlockSpec((tb, D), lambda i: (jnp.maximum(i - n_p, 0), 0))]
        out_shape = [jax.ShapeDtypeStruct((geo.t_prompt, D), F32), jax.ShapeDtypeStruct((geo.t_sample, D), F32)]
    else:
        out_specs = pl.BlockSpec((tb, D), lambda i: (i, 0))
        out_shape = jax.ShapeDtypeStruct((geo.t, D), F32)
    return pl.pallas_call(
        functools.partial(_combine_kernel, n_prompt_tiles=n_p),
        name="combine",
        grid=(geo.t // tb,),
        in_specs=[
            pl.BlockSpec((tb, D), lambda i: (i, 0)),
            pl.BlockSpec((None, 6, D), lambda i: (geo.cond_row(i, tb), 0, 0)),
            pl.BlockSpec((tb, TOP_K), lambda i: (i, 0)),
            pl.BlockSpec((ROW_CHUNKS, TOP_K, tb, CHUNK_W), lambda i: (0, 0, i, 0)),
            full((D, D_EXPERT)), full((D, D_EXPERT)), full((D_EXPERT, D)), full((1, D)), full((1, D)),
        ],
        out_specs=out_specs,
        out_shape=out_shape,
        compiler_params=_params(("arbitrary",)),
    )(x, mod_l, wt, ytok, sg.astype(BF16), su.astype(BF16), sd.astype(BF16),
      ln_g.reshape(1, D), ln_b.reshape(1, D))


def _moe_layer(geo, x, routing, mod_l, layer, wg, wu, wd, sg, su, sd, ln_g, ln_b, split=False):
    t = geo.t
    top_e, w, rank, counts, h = routing
    n_blocks = (t * TOP_K) // MOE_BLK + N_EXPERTS
    n_rows = n_blocks * MOE_BLK
    padded = (counts + MOE_BLK - 1) // MOE_BLK * MOE_BLK
    pend = jnp.cumsum(padded)
    pstart = (pend - padded).astype(jnp.int32)
    blk_e, blk_valid = _block_meta(pstart, counts, pend.astype(jnp.int32), n_blocks)
    n_used = (pend[-1:] // MOE_BLK).astype(jnp.int32)
    slots = _slots(geo, pstart, top_e, rank)
    idx = (slots.reshape(1, TOP_K * t) + (jnp.arange(ROW_CHUNKS, dtype=jnp.int32) * n_rows)[:, None]).reshape(-1)
    xs = _sc_scatter(h.reshape(ROW_CHUNKS * t, CHUNK_W), idx, ROW_CHUNKS * n_rows, TOP_K)
    yb = _ffn(xs.reshape(ROW_CHUNKS, n_rows, CHUNK_W), blk_e, blk_valid, n_used, layer, wg, wu, wd, n_blocks)
    ytok = _sc_gather(yb.reshape(ROW_CHUNKS * n_rows, CHUNK_W), idx)
    return _combine(geo, x, mod_l, w.T, ytok.reshape(ROW_CHUNKS, TOP_K, t, CHUNK_W), sg, su, sd, ln_g, ln_b,
                    split=split)


def _pos_embed(rows):
    quarter = D // 4
    omega = 1.0 / (POS_BASE ** (jnp.arange(quarter, dtype=F32) / quarter))
    r = jnp.arange(rows, dtype=F32)[:, None] * omega
    col = jnp.arange(GRID_W, dtype=F32)[:, None] * omega
    row_part = jnp.concatenate([jnp.sin(r), jnp.cos(r)], axis=-1)[:, None, :]
    col_part = jnp.concatenate([jnp.sin(col), jnp.cos(col)], axis=-1)[None, :, :]
    shape = (rows, GRID_W, 2 * quarter)
    return jnp.concatenate([jnp.broadcast_to(row_part, shape), jnp.broadcast_to(col_part, shape)],
                           axis=-1).reshape(rows * GRID_W, D)


def _mlstm_layer(geo, x, mod_l, j, a_w_in, a_b_gates, a_norm, a_w_out, ln_g, ln_b,
                 state_C, state_n, state_m, w_router, e_bias):
    q, kt, v, so, gr = _proj_a(geo, x, mod_l, a_w_in[j], a_b_gates[j])
    hp, c_p, n_p, m_p = _mlstm_scan(q, kt, v, gr, row0=0, n_seq=geo.n_prompt, seq_len=geo.prompt_len)
    ns = geo.n_sample
    n0 = jnp.pad(state_n[:, j].astype(F32)[..., None], ((0, 0),) * 4 + ((0, LANES - 1),))
    m0 = jnp.pad(state_m[:, j].astype(F32), ((0, 0), (0, 0), (0, SUBLANES - NH_A)))
    m0 = jnp.broadcast_to(m0[..., None], (ns, 2, SUBLANES, LANES))
    hs, _, _, _ = _mlstm_scan(q, kt, v, gr, row0=geo.t_prompt, n_seq=ns, seq_len=geo.sample_len,
                              state=(state_C[:, j].astype(F32), n0, m0))
    x1, routing = _out_a(geo, hp, hs, so, a_norm[j], a_w_out[j], x, mod_l, ln_g, ln_b, w_router, e_bias)
    return x1, routing, c_p, n_p[..., 0], m_p[:, :, :NH_A, 0]


def _hgrn_layer(geo, x, mod_l, j, lb_layer, b_w_in, b_norm, b_w_out, ln_g, ln_b, state_S, w_router, e_bias):
    q, pre, v, sg = _proj_b(geo, x, mod_l, b_w_in[j])
    lbd = lb_layer.reshape(2, NH_B, 1, DK_B)
    op, s_p = _hgrn_scan(q, pre, v, lbd, row0=0, n_seq=geo.n_prompt, seq_len=geo.prompt_len)
    os_, _ = _hgrn_scan(q, pre, v, lbd, row0=geo.t_prompt, n_seq=geo.n_sample, seq_len=geo.sample_len,
                        state=state_S[:, j].astype(F32))
    x1, routing = _out_b(geo, op, os_, sg, b_norm[j], b_w_out[j], x, mod_l, ln_g, ln_b, w_router, e_bias)
    return x1, routing, s_p


def kernel(x_prompt, x_sample, state_mlstm_C, state_mlstm_n, state_mlstm_m, state_hgrn_S, c, c_ctx, w_mod, b_mod, ln_g, ln_b, a_w_in, a_b_gates, a_norm, a_w_out, b_w_in, b_lb, b_norm, b_w_out, w_router, e_bias, w_gate, w_up, w_down, ws_gate, ws_up, ws_down):
    bp, sp, _ = x_prompt.shape
    bs, ss, _ = x_sample.shape
    cond = jnp.zeros((COND_ROWS, D), F32).at[0].set(c_ctx).at[1:1 + bs].set(c)
    mod = _modulation(cond, w_mod, b_mod)
    x = (x_prompt.reshape(-1, D), x_sample.reshape(-1, D), _pos_embed(ss // GRID_W))
    sm = jax.nn.softmax(b_lb.astype(F32), axis=0)
    lb_all = jnp.cumsum(sm, axis=0) - sm[0]
    geo = Geometry(bp, sp, bs, ss)
    x1, routing, new_c, new_n, new_m = _mlstm_layer(geo, x, mod[0], 0, a_w_in, a_b_gates, a_norm, a_w_out,
                                                    ln_g[0, 0], ln_b[0, 0], state_mlstm_C, state_mlstm_n,
                                                    state_mlstm_m, w_router[0], e_bias[0])
    x2 = _moe_layer(geo, x1, routing, mod[0], 0, w_gate, w_up, w_down, ws_gate[0], ws_up[0], ws_down[0],
                    ln_g[0, 1], ln_b[0, 1])
    x3, routing, new_s = _hgrn_layer(geo, x2, mod[1], 0, lb_all[1], b_w_in, b_norm, b_w_out, ln_g[1, 0], ln_b[1, 0],
                                     state_hgrn_S, w_router[1], e_bias[1])
    y_p, y_s = _moe_layer(geo, x3, routing, mod[1], 1, w_gate, w_up, w_down, ws_gate[1], ws_up[1], ws_down[1],
                          ln_g[1, 1], ln_b[1, 1], split=True)
    return (y_p.reshape(bp, sp, D), y_s.reshape(bs, ss, D), new_c[:, None], new_n[:, None], new_m[:, None],
            new_s[:, None])
```

```python
import functools
import math

import jax
import jax.numpy as jnp
from jax import lax
from jax.experimental import pallas as pl
from jax.experimental.pallas import tpu as pltpu
from jax.experimental.pallas import tpu_sc as plsc

F32 = jnp.float32
BF16 = jnp.bfloat16
HIGHEST = lax.Precision.HIGHEST

D = 1024
DEPTH = 2
GRID_W = 64
POS_BASE = 10000.0
EPS = 1e-6
ALPHA = (2.0 * DEPTH) ** 0.25
NH_A, DK_A, DV_A = 4, 128, 256
QK_A, V_A = NH_A * DK_A, NH_A * DV_A
NH_B, DK_B = 8, 128
N_EXPERTS, TOP_K, N_GROUPS, TOPK_GROUPS = 64, 8, 8, 4
GROUP_SIZE = N_EXPERTS // N_GROUPS
D_EXPERT = D // 4
ROUTED_SCALE = 2.5

LANES = 128
SUBLANES = 8
COND_ROWS = 8
TOK_TILE = 256
TOK_TILE_L = 512
SLOT_TILE = 2048
CHUNK_A = 256
VMEM_BYTES_V7X = 64 * 1024 * 1024
VMEM_LIMIT = VMEM_BYTES_V7X - 8 * 1024 * 1024

NT_DIMS = (((1,), (1,)), ((), ()))


def _params(sem):
    return pltpu.CompilerParams(dimension_semantics=sem, vmem_limit_bytes=VMEM_LIMIT)


def _split3(x):
    hi = x.astype(BF16)
    r = x - hi.astype(F32)
    mid = r.astype(BF16)
    lo = (r - mid.astype(F32)).astype(BF16)
    return hi, mid, lo


def _dot3(a_bf, x):
    hi, mid, lo = _split3(x)
    return (jnp.dot(a_bf, hi, preferred_element_type=F32)
            + jnp.dot(a_bf, mid, preferred_element_type=F32)
            + jnp.dot(a_bf, lo, preferred_element_type=F32))


def _dot3_r(x, a_bf):
    hi, mid, lo = _split3(x)
    return (jnp.dot(hi, a_bf, preferred_element_type=F32)
            + jnp.dot(mid, a_bf, preferred_element_type=F32)
            + jnp.dot(lo, a_bf, preferred_element_type=F32))


def _log_sigmoid(x):
    return jnp.minimum(x, 0.0) - jnp.log1p(jnp.exp(-jnp.abs(x)))


def _layer_norm_rows(x, g, b):
    mu = jnp.mean(x, axis=-1, keepdims=True)
    xc = x - mu
    var = jnp.mean(xc * xc, axis=-1, keepdims=True)
    return xc * lax.rsqrt(var + EPS) * g + b


class Geometry:
    def __init__(self, n_prompt, prompt_len, n_sample, sample_len):
        self.n_prompt, self.prompt_len = n_prompt, prompt_len
        self.n_sample, self.sample_len = n_sample, sample_len
        self.t_prompt = n_prompt * prompt_len
        self.t_sample = n_sample * sample_len
        self.t = self.t_prompt + self.t_sample
        assert self.t_prompt % TOK_TILE_L == 0 and sample_len % TOK_TILE_L == 0
        assert n_sample + 1 <= COND_ROWS

    def cond_row(self, tile, tile_rows):
        n_p = self.t_prompt // tile_rows
        return jnp.where(tile < n_p, 0, 1 + (tile - n_p) // (self.sample_len // tile_rows))


def _mod_kernel(cond_ref, w_ref, b_ref, o_ref):
    c = cond_ref[...]
    s = c * jax.nn.sigmoid(c)
    o_ref[0, 0] = jnp.dot(s, w_ref[0], precision=HIGHEST, preferred_element_type=F32) + b_ref[0, 0]


def _modulation(cond, w_mod, b_mod):
    out = pl.pallas_call(
        _mod_kernel,
        name="modulation",
        grid=(DEPTH, 6),
        in_specs=[
            pl.BlockSpec((COND_ROWS, D), lambda l, j: (0, 0)),
            pl.BlockSpec((1, D, D), lambda l, j: (l, 0, j)),
            pl.BlockSpec((1, 1, 1, D), lambda l, j: (l, j, 0, 0)),
        ],
        out_specs=pl.BlockSpec((1, 1, COND_ROWS, D), lambda l, j: (l, j, 0, 0)),
        out_shape=jax.ShapeDtypeStruct((DEPTH, 6, COND_ROWS, D), F32),
        compiler_params=_params(("arbitrary", "arbitrary")),
    )(cond, w_mod, b_mod.reshape(DEPTH, 6, 1, D))
    return out.transpose(0, 2, 1, 3)


def _embed_specs(geo, tb):
    n_p = geo.t_prompt // tb
    per_seq = geo.sample_len // tb
    return [pl.BlockSpec((tb, D), lambda i: (jnp.minimum(i, n_p - 1), 0)),
            pl.BlockSpec((tb, D), lambda i: (jnp.maximum(i - n_p, 0), 0)),
            pl.BlockSpec((tb, D), lambda i: (jnp.maximum(i - n_p, 0) % per_seq, 0))]


def _embed_tile(xp_ref, xs_ref, pos_ref, n_prompt_tiles):
    return jnp.where(pl.program_id(0) < n_prompt_tiles, xp_ref[...], xs_ref[...] + pos_ref[...])


def _proj_a_kernel(xp_ref, xs_ref, pos_ref, mod_ref, wq_ref, wkt_ref, wvo_ref, wgt_ref, bgt_ref,
                   q_ref, kt_ref, v_ref, so_ref, gr_ref, *, n_prompt_tiles):
    x = _embed_tile(xp_ref, xs_ref, pos_ref, n_prompt_tiles)
    h = x * (1.0 + mod_ref[1:2, :]) + mod_ref[0:1, :]
    hb = h.astype(BF16)
    q_ref[...] = jnp.dot(hb, wq_ref[...], preferred_element_type=F32).astype(BF16)
    kt = lax.dot_general(wkt_ref[...], hb, NT_DIMS, preferred_element_type=F32)
    kt_ref[...] = (kt * (DK_A ** -0.5)).astype(BF16)
    vo = jnp.dot(hb, wvo_ref[...], preferred_element_type=F32)
    v_ref[...] = vo[:, :V_A].astype(BF16)
    so_ref[...] = jax.nn.sigmoid(vo[:, V_A:]).astype(BF16)
    gr_ref[...] = lax.dot_general(wgt_ref[...], h, NT_DIMS, precision=HIGHEST,
                                  preferred_element_type=F32) + bgt_ref[...]


def _proj_a(geo, x, mod_l, w_in, b_gates):
    t = geo.t
    n_gate = 4 * NH_A
    wq = w_in[:, :QK_A].astype(BF16)
    wkt = w_in[:, QK_A:2 * QK_A].T.astype(BF16)
    wvo = w_in[:, 2 * QK_A:2 * QK_A + 2 * V_A].astype(BF16)
    wg = w_in[:, 2 * QK_A + 2 * V_A:]
    bg = b_gates.reshape(n_gate).astype(F32)
    tb = TOK_TILE_L
    full = lambda shape: pl.BlockSpec(shape, lambda i: (0,) * len(shape))
    return pl.pallas_call(
        functools.partial(_proj_a_kernel, n_prompt_tiles=geo.t_prompt // tb),
        name="proj_a",
        grid=(t // tb,),
        in_specs=_embed_specs(geo, tb) + [
            pl.BlockSpec((None, 6, D), lambda i: (geo.cond_row(i, tb), 0, 0)),
            full((D, QK_A)), full((QK_A, D)), full((D, 2 * V_A)), full((n_gate, D)), full((n_gate, 1)),
        ],
        out_specs=[
            pl.BlockSpec((tb, QK_A), lambda i: (i, 0)),
            pl.BlockSpec((QK_A, tb), lambda i: (0, i)),
            pl.BlockSpec((tb, V_A), lambda i: (i, 0)),
            pl.BlockSpec((tb, V_A), lambda i: (i, 0)),
            pl.BlockSpec((n_gate, tb), lambda i: (0, i)),
        ],
        out_shape=[
            jax.ShapeDtypeStruct((t, QK_A), BF16),
            jax.ShapeDtypeStruct((QK_A, t), BF16),
            jax.ShapeDtypeStruct((t, V_A), BF16),
            jax.ShapeDtypeStruct((t, V_A), BF16),
            jax.ShapeDtypeStruct((n_gate, t), F32),
        ],
        compiler_params=_params(("parallel",)),
    )(*x, mod_l, wq, wkt, wvo, wg.T, bg.reshape(n_gate, 1))


def _mlstm_scan_kernel(*refs, chunk, has_state):
    if has_state:
        (q_ref, kt_ref, v_ref, gr_ref, c0_ref, n0_ref, m0_ref,
         h_ref, c_out, n_out, m_out, c_sc, n_sc, m_sc) = refs
    else:
        (q_ref, kt_ref, v_ref, gr_ref,
         h_ref, c_out, n_out, m_out, c_sc, n_sc, m_sc) = refs
    L = chunk
    d = pl.program_id(1)
    c = pl.program_id(2)
    fwd = d == 0

    @pl.when(c == 0)
    def _():
        if has_state:
            c_sc[...] = c0_ref[0, 0]
            n_sc[...] = n0_ref[0, 0]
            m_sc[...] = m0_ref[0, 0]
        else:
            c_sc[...] = jnp.zeros_like(c_sc)
            n_sc[...] = jnp.zeros_like(n_sc)
            m_sc[...] = jnp.zeros_like(m_sc)

    row = lax.broadcasted_iota(jnp.int32, (L, L), 0)
    col = lax.broadcasted_iota(jnp.int32, (L, L), 1)
    sgn = 1 - 2 * d
    causal = (row - col) * sgn >= 0
    tri_t = ((col - row) * sgn >= 0).astype(BF16)

    gr = gr_ref[...]
    br_all = _dot3_r(_log_sigmoid(gr), tri_t)
    bc_all = jnp.concatenate([br_all, jnp.zeros((LANES - br_all.shape[0], L), F32)], axis=0).T
    ones_blk = (lax.broadcasted_iota(jnp.int32, (L, LANES), 1) == 0).astype(BF16)

    def gate_row(direction, gate, head):
        return (direction * 2 + gate) * NH_A + head

    for h in range(NH_A):
        ff, fb = gate_row(0, 1, h), gate_row(1, 1, h)
        gi, gb = gate_row(0, 0, h), gate_row(1, 0, h)
        b_c = jnp.where(fwd, bc_all[:, ff:ff + 1], bc_all[:, fb:fb + 1])
        b_r = jnp.where(fwd, br_all[ff:ff + 1, :], br_all[fb:fb + 1, :])
        i_r = jnp.where(fwd, gr[gi:gi + 1, :], gr[gb:gb + 1, :])
        bl = jnp.where(fwd, b_r[:, L - 1:L], b_r[:, 0:1])
        q = q_ref[:, h * DK_A:(h + 1) * DK_A]
        kt = kt_ref[h * DK_A:(h + 1) * DK_A, :]
        v = v_ref[:, h * DV_A:(h + 1) * DV_A]
        m = m_sc[h:h + 1, 0:1]
        cst = c_sc[h]
        nst = n_sc[h]

        a_r = i_r - b_r
        logd = jnp.where(causal, b_c + a_r, -jnp.inf)
        inter = b_c + m
        m_t = jnp.maximum(inter, jnp.max(logd, axis=1, keepdims=True))
        dmat = jnp.exp(logd - m_t)
        e_int = jnp.exp(inter - m_t)
        s = (jnp.dot(q, kt, preferred_element_type=F32) * dmat).astype(BF16)
        num = (jnp.dot(s, v, preferred_element_type=F32)
               + e_int * jnp.dot(q, cst.astype(BF16), preferred_element_type=F32))
        den = (jnp.dot(s, ones_blk, preferred_element_type=F32)
               + e_int * jnp.dot(q, nst.astype(BF16), preferred_element_type=F32))[:, 0:1]
        h_ref[:, h * DV_A:(h + 1) * DV_A] = (num / jnp.maximum(jnp.abs(den), jnp.exp(-m_t))).astype(BF16)

        logw = bl + a_r
        m_new = jnp.maximum(bl + m, jnp.max(logw, axis=1, keepdims=True))
        w = jnp.exp(logw - m_new)
        decay = jnp.exp(bl + m - m_new)
        kw = (kt.astype(F32) * w).astype(BF16)
        c_sc[h] = decay * cst + jnp.dot(kw, v, preferred_element_type=F32)
        n_sc[h] = decay * nst + jnp.dot(kw, ones_blk, preferred_element_type=F32)
        m_sc[h:h + 1, :] = jnp.broadcast_to(m_new, (1, LANES))

    @pl.when(c == pl.num_programs(2) - 1)
    def _():
        c_out[0, 0] = c_sc[...]
        n_out[0, 0] = n_sc[...]
        m_out[0, 0] = m_sc[...]


def _mlstm_scan(q, kt, v, gr, *, row0, n_seq, seq_len, state=None):
    L = CHUNK_A
    nc = seq_len // L
    blk0 = row0 // L

    def loc_blk(b, d, c):
        return b * nc + c + d * (nc - 1 - 2 * c)

    def tok_blk(b, d, c):
        return blk0 + loc_blk(b, d, c)

    in_specs = [
        pl.BlockSpec((L, QK_A), lambda b, d, c: (tok_blk(b, d, c), 0)),
        pl.BlockSpec((QK_A, L), lambda b, d, c: (0, tok_blk(b, d, c))),
        pl.BlockSpec((L, V_A), lambda b, d, c: (tok_blk(b, d, c), 0)),
        pl.BlockSpec((4 * NH_A, L), lambda b, d, c: (0, tok_blk(b, d, c))),
    ]
    args = [q, kt, v, gr]
    if state is not None:
        in_specs += [
            pl.BlockSpec((1, 1, NH_A, DK_A, DV_A), lambda b, d, c: (b, d, 0, 0, 0)),
            pl.BlockSpec((1, 1, NH_A, DK_A, LANES), lambda b, d, c: (b, d, 0, 0, 0)),
            pl.BlockSpec((1, 1, SUBLANES, LANES), lambda b, d, c: (b, d, 0, 0)),
        ]
        args += list(state)
    return pl.pallas_call(
        functools.partial(_mlstm_scan_kernel, chunk=L, has_state=state is not None),
        name="mlstm_scan_seeded" if state is not None else "mlstm_scan",
        grid=(n_seq, 2, nc),
        in_specs=in_specs,
        out_specs=[
            pl.BlockSpec((None, L, V_A), lambda b, d, c: (d, loc_blk(b, d, c), 0)),
            pl.BlockSpec((1, 1, NH_A, DK_A, DV_A), lambda b, d, c: (b, d, 0, 0, 0)),
            pl.BlockSpec((1, 1, NH_A, DK_A, LANES), lambda b, d, c: (b, d, 0, 0, 0)),
            pl.BlockSpec((1, 1, SUBLANES, LANES), lambda b, d, c: (b, d, 0, 0)),
        ],
        out_shape=[
            jax.ShapeDtypeStruct((2, n_seq * seq_len, V_A), BF16),
            jax.ShapeDtypeStruct((n_seq, 2, NH_A, DK_A, DV_A), F32),
            jax.ShapeDtypeStruct((n_seq, 2, NH_A, DK_A, LANES), F32),
            jax.ShapeDtypeStruct((n_seq, 2, SUBLANES, LANES), F32),
        ],
        scratch_shapes=[
            pltpu.VMEM((NH_A, DK_A, DV_A), F32),
            pltpu.VMEM((NH_A, DK_A, LANES), F32),
            pltpu.VMEM((SUBLANES, LANES), F32),
        ],
        compiler_params=_params(("parallel", "parallel", "arbitrary")),
    )(*args)


def _out_a_kernel(hp_ref, hs_ref, so_ref, nw_ref, w_ref, xp_ref, xs_ref, pos_ref, mod_ref, lg_ref, lb_ref,
                  wrt_ref, eb_ref, o_ref, *route_refs, n_prompt_tiles):
    is_prompt = pl.program_id(0) < n_prompt_tiles
    x = _embed_tile(xp_ref, xs_ref, pos_ref, n_prompt_tiles)
    y = jnp.where(is_prompt, hp_ref[0].astype(F32) + hp_ref[1].astype(F32),
                  hs_ref[0].astype(F32) + hs_ref[1].astype(F32))
    parts = []
    for h in range(NH_A):
        yh = y[:, h * DV_A:(h + 1) * DV_A]
        mu = jnp.mean(yh, axis=-1, keepdims=True)
        yc = yh - mu
        var = jnp.mean(yc * yc, axis=-1, keepdims=True)
        parts.append(yc * lax.rsqrt(var + EPS))
    yn = jnp.concatenate(parts, axis=-1) * nw_ref[...] * so_ref[...].astype(F32)
    out = jnp.dot(yn.astype(BF16), w_ref[...], preferred_element_type=F32)
    x1 = _layer_norm_rows(ALPHA * x + mod_ref[2:3, :] * out, lg_ref[...], lb_ref[...])
    o_ref[...] = x1
    _route_tile(x1, mod_ref, wrt_ref, eb_ref, *route_refs)


def _out_a(geo, h_prompt, h_sample, so, norm_w, w_out, x, mod_l, ln_g, ln_b, w_router, e_bias):
    t = geo.t
    tb = TOK_TILE_L
    n_p = geo.t_prompt // tb
    full = lambda shape: pl.BlockSpec(shape, lambda i: (0,) * len(shape))
    rio = _RouterIO(t, tb, w_router, e_bias)
    outs = pl.pallas_call(
        functools.partial(_out_a_kernel, n_prompt_tiles=n_p),
        name="out_a",
        grid=(t // tb,),
        in_specs=[
            pl.BlockSpec((2, tb, V_A), lambda i: (0, jnp.minimum(i, n_p - 1), 0)),
            pl.BlockSpec((2, tb, V_A), lambda i: (0, jnp.maximum(i - n_p, 0), 0)),
            pl.BlockSpec((tb, V_A), lambda i: (i, 0)),
            full((1, V_A)), full((V_A, D)),
        ] + _embed_specs(geo, tb) + [
            pl.BlockSpec((None, 6, D), lambda i: (geo.cond_row(i, tb), 0, 0)),
            full((1, D)), full((1, D)),
        ] + rio.in_specs,
        out_specs=[pl.BlockSpec((tb, D), lambda i: (i, 0))] + rio.out_specs,
        out_shape=[jax.ShapeDtypeStruct((t, D), F32)] + rio.out_shape,
        scratch_shapes=rio.scratch,
        compiler_params=_params(("arbitrary",)),
    )(h_prompt, h_sample, so, norm_w.reshape(1, V_A).astype(F32), w_out.astype(BF16), *x, mod_l,
      ln_g.reshape(1, D), ln_b.reshape(1, D), *rio.inputs)
    return outs[0], _RouterIO.unpack(outs[1:])


def _proj_b_kernel(x_ref, mod_ref, w_ref, q_ref, pre_ref, v_ref, sg_ref):
    h = x_ref[...] * (1.0 + mod_ref[1:2, :]) + mod_ref[0:1, :]
    z = jnp.dot(h.astype(BF16), w_ref[...], preferred_element_type=F32)
    for hd in range(NH_B):
        lo = hd * DK_B
        qh = z[:, lo:lo + DK_B]
        q_ref[hd] = qh * jax.nn.sigmoid(qh)
        pre_ref[0, hd] = z[:, D + lo:D + lo + DK_B]
        pre_ref[1, hd] = z[:, 2 * D + lo:2 * D + lo + DK_B]
        v_ref[hd] = z[:, 3 * D + lo:3 * D + lo + DK_B].astype(BF16)
    g = z[:, 4 * D:]
    sg_ref[...] = (g * jax.nn.sigmoid(g)).astype(BF16)


def _proj_b(geo, x, mod_l, w_in):
    t = geo.t
    tb = TOK_TILE
    return pl.pallas_call(
        _proj_b_kernel,
        name="proj_b",
        grid=(t // tb,),
        in_specs=[
            pl.BlockSpec((tb, D), lambda i: (i, 0)),
            pl.BlockSpec((None, 6, D), lambda i: (geo.cond_row(i, tb), 0, 0)),
            pl.BlockSpec((D, 5 * D), lambda i: (0, 0)),
        ],
        out_specs=[
            pl.BlockSpec((NH_B, tb, DK_B), lambda i: (0, i, 0)),
            pl.BlockSpec((2, NH_B, tb, DK_B), lambda i: (0, 0, i, 0)),
            pl.BlockSpec((NH_B, tb, DK_B), lambda i: (0, i, 0)),
            pl.BlockSpec((tb, D), lambda i: (i, 0)),
        ],
        out_shape=[
            jax.ShapeDtypeStruct((NH_B, t, DK_B), F32),
            jax.ShapeDtypeStruct((2, NH_B, t, DK_B), F32),
            jax.ShapeDtypeStruct((NH_B, t, DK_B), BF16),
            jax.ShapeDtypeStruct((t, D), BF16),
        ],
        compiler_params=_params(("parallel",)),
    )(x, mod_l, w_in.astype(BF16))


CHUNK_B = 128
BAND = SUBLANES // 2
TN_DIMS = (((0,), (0,)), ((), ()))


def _hgrn_head(q, pre, lbv, v_bf, st, fwd):
    L = q.shape[0]
    sg = jax.nn.sigmoid(pre)
    f = lbv + (1.0 - lbv) * sg
    lf = jnp.log(f)
    kk = (1.0 - lbv) * (1.0 - sg)
    row = lax.broadcasted_iota(jnp.int32, (L, L), 0)
    col = lax.broadcasted_iota(jnp.int32, (L, L), 1)
    tri = ((row >= col) if fwd else (row <= col)).astype(BF16)
    b = _dot3(tri, lf)
    tpos = lax.broadcasted_iota(jnp.int32, (L, DK_B), 0)
    blk_bits = row ^ col
    lag = jnp.where(blk_bits < BAND, (row - col) if fwd else (col - row), -1)

    step = 1 if fwd else L - 1
    att = jnp.where(lag == 0, jnp.sum(q * kk, axis=1, keepdims=True), 0.0)
    f_r, kk_r, g = f, kk, f
    for dl in range(1, BAND):
        if dl > 1:
            f_r = pltpu.roll(f_r, step, 0)
            g = g * f_r
        kk_r = pltpu.roll(kk_r, step, 0)
        att = jnp.where(lag == dl, jnp.sum(q * kk_r * g, axis=1, keepdims=True), att)

    w = BAND
    while w < L:
        nb = L // (2 * w)
        b3 = b.reshape(nb, 2 * w, DK_B)
        edge = (b3[:, w - 1:w, :] if fwd else b3[:, w:w + 1, :])
        bmid = jnp.broadcast_to(edge, (nb, 2 * w, DK_B)).reshape(L, DK_B)
        second = (tpos & w) != 0
        t_side = second if fwd else jnp.logical_not(second)
        e = jnp.exp(jnp.where(t_side, b - bmid, bmid - b))
        qt = jnp.where(t_side, q * e, 0.0).astype(BF16)
        ks = jnp.where(t_side, 0.0, kk * e).astype(BF16)
        a = lax.dot_general(qt, ks, NT_DIMS, preferred_element_type=F32)
        att = att + jnp.where(blk_bits < 2 * w, a, 0.0)
        w *= 2
    o = jnp.dot(att.astype(BF16), v_bf, preferred_element_type=F32)

    bl = b[L - 1:L, :] if fwd else b[0:1, :]
    o = o + lax.dot_general((q * jnp.exp(b)).astype(BF16), st.astype(BF16), NT_DIMS, preferred_element_type=F32)
    kd = (kk * jnp.exp(bl - b)).astype(BF16)
    st_new = jnp.exp(bl) * st + lax.dot_general(v_bf, kd, TN_DIMS, preferred_element_type=F32)
    return o, st_new


def _hgrn_scan_kernel(*refs, has_state):
    if has_state:
        q_ref, pre_ref, v_ref, lb_ref, s0_ref, o_ref, s_out, st_sc = refs
    else:
        q_ref, pre_ref, v_ref, lb_ref, o_ref, s_out, st_sc = refs
    d = pl.program_id(1)
    c = pl.program_id(2)

    @pl.when(c == 0)
    def _():
        if has_state:
            for hd in range(NH_B):
                st_sc[hd] = s0_ref[0, 0, hd].T
        else:
            st_sc[...] = jnp.zeros_like(st_sc)

    def run(fwd):
        def head(hd, carry):
            o, st_new = _hgrn_head(q_ref[hd], pre_ref[hd], lb_ref[hd], v_ref[hd], st_sc[hd], fwd)
            o_ref[hd] = o.astype(BF16)
            st_sc[hd] = st_new
            return carry
        lax.fori_loop(0, NH_B, head, 0, unroll=8)

    @pl.when(d == 0)
    def _():
        run(True)

    @pl.when(d == 1)
    def _():
        run(False)

    @pl.when(c == pl.num_programs(2) - 1)
    def _():
        for hd in range(NH_B):
            s_out[0, 0, hd] = st_sc[hd].T


def _hgrn_scan(q, pre, v, lbd, *, row0, n_seq, seq_len, state=None):
    L = CHUNK_B
    nc = seq_len // L
    blk0 = row0 // L

    def loc_blk(b, d, c):
        return b * nc + c + d * (nc - 1 - 2 * c)

    def tok_blk(b, d, c):
        return blk0 + loc_blk(b, d, c)

    in_specs = [
        pl.BlockSpec((NH_B, L, DK_B), lambda b, d, c: (0, tok_blk(b, d, c), 0)),
        pl.BlockSpec((None, NH_B, L, DK_B), lambda b, d, c: (d, 0, tok_blk(b, d, c), 0)),
        pl.BlockSpec((NH_B, L, DK_B), lambda b, d, c: (0, tok_blk(b, d, c), 0)),
        pl.BlockSpec((None, NH_B, 1, DK_B), lambda b, d, c: (d, 0, 0, 0)),
    ]
    args = [q, pre, v, lbd]
    if state is not None:
        in_specs.append(pl.BlockSpec((1, 1, NH_B, DK_B, DK_B), lambda b, d, c: (b, d, 0, 0, 0)))
        args.append(state)
    return pl.pallas_call(
        functools.partial(_hgrn_scan_kernel, has_state=state is not None),
        name="hgrn_scan_seeded" if state is not None else "hgrn_scan",
        grid=(n_seq, 2, nc),
        in_specs=in_specs,
        out_specs=[
            pl.BlockSpec((None, NH_B, L, DK_B), lambda b, d, c: (d, 0, loc_blk(b, d, c), 0)),
            pl.BlockSpec((1, 1, NH_B, DK_B, DK_B), lambda b, d, c: (b, d, 0, 0, 0)),
        ],
        out_shape=[
            jax.ShapeDtypeStruct((2, NH_B, n_seq * seq_len, DK_B), BF16),
            jax.ShapeDtypeStruct((n_seq, 2, NH_B, DK_B, DK_B), F32),
        ],
        scratch_shapes=[pltpu.VMEM((NH_B, DK_B, DK_B), F32)],
        compiler_params=_params(("parallel", "parallel", "arbitrary")),
    )(*args)


def _out_b_kernel(op_ref, os_ref, sg_ref, nw_ref, w_ref, x_ref, mod_ref, lg_ref, lb_ref, wrt_ref, eb_ref,
                  out_ref, *route_refs, n_prompt_tiles):
    is_prompt = pl.program_id(0) < n_prompt_tiles
    parts = []
    for hd in range(NH_B):
        y = jnp.where(is_prompt, op_ref[0, hd].astype(F32) + op_ref[1, hd].astype(F32),
                      os_ref[0, hd].astype(F32) + os_ref[1, hd].astype(F32))
        parts.append(y * lax.rsqrt(jnp.mean(y * y, axis=-1, keepdims=True) + EPS))
    yn = jnp.concatenate(parts, axis=-1) * nw_ref[...] * sg_ref[...].astype(F32)
    out = jnp.dot(yn.astype(BF16), w_ref[...], preferred_element_type=F32)
    x1 = _layer_norm_rows(ALPHA * x_ref[...] + mod_ref[2:3, :] * out, lg_ref[...], lb_ref[...])
    out_ref[...] = x1
    _route_tile(x1, mod_ref, wrt_ref, eb_ref, *route_refs)


def _out_b(geo, o_prompt, o_sample, sg, norm_w, w_out, x, mod_l, ln_g, ln_b, w_router, e_bias):
    t = geo.t
    tb = TOK_TILE_L
    n_p = geo.t_prompt // tb
    full = lambda shape: pl.BlockSpec(shape, lambda i: (0,) * len(shape))
    rio = _RouterIO(t, tb, w_router, e_bias)
    outs = pl.pallas_call(
        functools.partial(_out_b_kernel, n_prompt_tiles=n_p),
        name="out_b",
        grid=(t // tb,),
        in_specs=[
            pl.BlockSpec((2, NH_B, tb, DK_B), lambda i: (0, 0, jnp.minimum(i, n_p - 1), 0)),
            pl.BlockSpec((2, NH_B, tb, DK_B), lambda i: (0, 0, jnp.maximum(i - n_p, 0), 0)),
            pl.BlockSpec((tb, D), lambda i: (i, 0)),
            full((1, D)), full((D, D)),
            pl.BlockSpec((tb, D), lambda i: (i, 0)),
            pl.BlockSpec((None, 6, D), lambda i: (geo.cond_row(i, tb), 0, 0)),
            full((1, D)), full((1, D)),
        ] + rio.in_specs,
        out_specs=[pl.BlockSpec((tb, D), lambda i: (i, 0))] + rio.out_specs,
        out_shape=[jax.ShapeDtypeStruct((t, D), F32)] + rio.out_shape,
        scratch_shapes=rio.scratch,
        compiler_params=_params(("arbitrary",)),
    )(o_prompt, o_sample, sg, norm_w.reshape(1, D).astype(F32), w_out.astype(BF16), x, mod_l,
      ln_g.reshape(1, D), ln_b.reshape(1, D), *rio.inputs)
    return outs[0], _RouterIO.unpack(outs[1:])


MOE_BLK = 1024
FFN_ROWS = 512
U32 = jnp.uint32
ROW_WORDS = D // 2
CHUNK_W = 256
ROW_CHUNKS = ROW_WORDS // CHUNK_W
SC_WINDOW = 128


def _pack_rows(x):
    return pltpu.pack_elementwise([x[:, :ROW_WORDS], x[:, ROW_WORDS:]], packed_dtype=BF16)


def _unpack_rows(words):
    return jnp.concatenate([pltpu.unpack_elementwise(words, index=i, packed_dtype=BF16, unpacked_dtype=F32)
                            for i in range(2)], axis=1)


def _store_chunks(chunk_ref, x):
    words = _pack_rows(x)
    for c in range(ROW_CHUNKS):
        chunk_ref(c)[...] = words[:, c * CHUNK_W:(c + 1) * CHUNK_W]


def _load_chunks(chunk_ref, valid_rows=None):
    words = jnp.concatenate([chunk_ref(c)[...] for c in range(ROW_CHUNKS)], axis=1)
    if valid_rows is not None:
        row = lax.broadcasted_iota(jnp.int32, (words.shape[0], 1), 0)
        words = jnp.where(row < valid_rows, words, jnp.uint32(0))
    return _unpack_rows(words)


def _first_index(hit, iota, size, axis):
    return jnp.min(jnp.where(hit, iota, size), axis=axis, keepdims=True)


def _route_tile(x, mod_ref, wrt_ref, eb_ref, e_ref, w_ref, r_ref, cnt_ref, h_ref, cnt_sc):
    i = pl.program_id(0)
    tb = x.shape[0]

    @pl.when(i == 0)
    def _():
        cnt_sc[...] = jnp.zeros_like(cnt_sc)

    h = x * (1.0 + mod_ref[4:5, :]) + mod_ref[3:4, :]
    _store_chunks(lambda c: h_ref.at[c], h)
    logits = lax.dot_general(wrt_ref[...], h, NT_DIMS, precision=HIGHEST, preferred_element_type=F32)
    scores = jax.nn.sigmoid(logits)
    sel = scores + eb_ref[...]

    g3 = sel.reshape(N_GROUPS, GROUP_SIZE, tb)
    io3 = lax.broadcasted_iota(jnp.int32, g3.shape, 1)
    m1 = jnp.max(g3, axis=1, keepdims=True)
    first = _first_index(g3 == m1, io3, GROUP_SIZE, 1)
    m2 = jnp.max(jnp.where(io3 == first, -jnp.inf, g3), axis=1, keepdims=True)
    gscore = (m1 + m2).reshape(N_GROUPS, tb)

    iog = lax.broadcasted_iota(jnp.int32, gscore.shape, 0)
    gmask = jnp.zeros(gscore.shape, F32)
    for _ in range(TOPK_GROUPS):
        gm = jnp.max(gscore, axis=0, keepdims=True)
        pick = iog == _first_index(gscore == gm, iog, N_GROUPS, 0)
        gmask = jnp.where(pick, 1.0, gmask)
        gscore = jnp.where(pick, -jnp.inf, gscore)
    emask = jnp.broadcast_to(gmask.reshape(N_GROUPS, 1, tb), (N_GROUPS, GROUP_SIZE, tb)).reshape(N_EXPERTS, tb)
    cand = jnp.where(emask > 0.0, sel, -jnp.inf)

    ioe = lax.broadcasted_iota(jnp.int32, cand.shape, 0)
    picks, wts = [], []
    onehot = jnp.zeros(cand.shape, F32)
    for _ in range(TOP_K):
        cm = jnp.max(cand, axis=0, keepdims=True)
        idx = _first_index(cand == cm, ioe, N_EXPERTS, 0)
        pick = ioe == idx
        picks.append(pick)
        wts.append(jnp.sum(jnp.where(pick, scores, 0.0), axis=0, keepdims=True))
        onehot = onehot + pick.astype(F32)
        cand = jnp.where(pick, -jnp.inf, cand)
        e_ref[pl.ds(len(picks) - 1, 1), :] = idx
    wsum = wts[0]
    for wk in wts[1:]:
        wsum = wsum + wk
    for k in range(TOP_K):
        w_ref[pl.ds(k, 1), :] = wts[k] / wsum * ROUTED_SCALE

    r_io = lax.broadcasted_iota(jnp.int32, (tb, tb), 0)
    c_io = lax.broadcasted_iota(jnp.int32, (tb, tb), 1)
    before = (r_io < c_io).astype(BF16)
    rank = cnt_sc[:, 0:1] + jnp.dot(onehot.astype(BF16), before, preferred_element_type=F32)
    for k in range(TOP_K):
        r_ref[pl.ds(k, 1), :] = jnp.sum(jnp.where(picks[k], rank, 0.0), axis=0, keepdims=True).astype(jnp.int32)
    cnt_sc[...] = cnt_sc[...] + jnp.sum(onehot, axis=1, keepdims=True)
    cnt_ref[...] = cnt_sc[...]


class _RouterIO:
    def __init__(self, t, tb, w_router, e_bias):
        full = lambda shape: pl.BlockSpec(shape, lambda i: (0,) * len(shape))
        self.inputs = [w_router.T.astype(F32), e_bias.reshape(N_EXPERTS, 1).astype(F32)]
        self.in_specs = [full((N_EXPERTS, D)), full((N_EXPERTS, 1))]
        self.out_specs = [
            pl.BlockSpec((TOP_K, tb), lambda i: (0, i)),
            pl.BlockSpec((TOP_K, tb), lambda i: (0, i)),
            pl.BlockSpec((TOP_K, tb), lambda i: (0, i)),
            full((N_EXPERTS, LANES)),
            pl.BlockSpec((ROW_CHUNKS, tb, CHUNK_W), lambda i: (0, i, 0)),
        ]
        self.out_shape = [
            jax.ShapeDtypeStruct((TOP_K, t), jnp.int32),
            jax.ShapeDtypeStruct((TOP_K, t), F32),
            jax.ShapeDtypeStruct((TOP_K, t), jnp.int32),
            jax.ShapeDtypeStruct((N_EXPERTS, LANES), F32),
            jax.ShapeDtypeStruct((ROW_CHUNKS, t, CHUNK_W), U32),
        ]
        self.scratch = [pltpu.VMEM((N_EXPERTS, LANES), F32)]

    @staticmethod
    def unpack(outs):
        e, w, r, cnt, h = outs
        return e, w, r, cnt[:, 0].astype(jnp.int32), h


def _slot_kernel(pstart_ref, e_ref, r_ref, o_ref):
    e = e_ref[...]
    slot = r_ref[...]
    for x in range(N_EXPERTS):
        slot = slot + jnp.where(e == x, pstart_ref[x], 0)
    o_ref[...] = slot


def _slots(geo, pstart, top_e, rank):
    tb = math.gcd(SLOT_TILE, geo.t)
    return pl.pallas_call(
        _slot_kernel,
        name="slots",
        grid_spec=pltpu.PrefetchScalarGridSpec(
            num_scalar_prefetch=1,
            grid=(geo.t // tb,),
            in_specs=[pl.BlockSpec((TOP_K, tb), lambda i, p: (0, i)),
                      pl.BlockSpec((TOP_K, tb), lambda i, p: (0, i))],
            out_specs=pl.BlockSpec((TOP_K, tb), lambda i, p: (0, i)),
        ),
        out_shape=jax.ShapeDtypeStruct((TOP_K, geo.t), jnp.int32),
        compiler_params=_params(("parallel",)),
    )(pstart, top_e, rank)


def _block_meta_kernel(pstart_ref, counts_ref, pend_ref, e_ref, v_ref):
    row0 = lax.broadcasted_iota(jnp.int32, e_ref.shape, 1) * MOE_BLK
    blk_e = jnp.zeros(e_ref.shape, jnp.int32)
    for x in range(N_EXPERTS):
        blk_e = blk_e + jnp.where(pend_ref[x] <= row0, 1, 0)
    blk_e = jnp.minimum(blk_e, N_EXPERTS - 1)
    last = jnp.zeros(e_ref.shape, jnp.int32)
    for x in range(N_EXPERTS):
        last = last + jnp.where(blk_e == x, pstart_ref[x] + counts_ref[x], 0)
    e_ref[...] = blk_e
    v_ref[...] = jnp.clip(last - row0, 0, MOE_BLK)


def _block_meta(pstart, counts, pend, n_blocks):
    e, v = pl.pallas_call(
        _block_meta_kernel,
        name="block_meta",
        grid_spec=pltpu.PrefetchScalarGridSpec(
            num_scalar_prefetch=3,
            grid=(1,),
            in_specs=[],
            out_specs=[pl.BlockSpec((1, n_blocks), lambda i, a, b, c: (0, 0)),
                       pl.BlockSpec((1, n_blocks), lambda i, a, b, c: (0, 0))],
        ),
        out_shape=[jax.ShapeDtypeStruct((1, n_blocks), jnp.int32), jax.ShapeDtypeStruct((1, n_blocks), jnp.int32)],
        compiler_params=_params(("arbitrary",)),
    )(pstart, counts, pend)
    return e[0], v[0]


def _sc_mesh():
    return plsc.VectorSubcoreMesh(core_axis_name="core", subcore_axis_name="subcore")


def _sc_scatter(rows, idx, n_out, copies):
    n_src = rows.shape[0]
    n_idx = idx.shape[0]
    groups = ROW_CHUNKS
    win_per_group = n_src // groups // SC_WINDOW

    def idx_block(w, k):
        return (0, ((w // win_per_group) * copies + k) * win_per_group + w % win_per_group)

    @pl.kernel(out_type=jax.ShapeDtypeStruct((n_out, CHUNK_W), rows.dtype), mesh=_sc_mesh(), scratch_types=[],
               name="sc_dispatch")
    def scatter(x_hbm, i_hbm, o_hbm):
        def body(x_vmem, *i_vmems):
            for i_vmem in i_vmems:
                pltpu.sync_copy(x_vmem, o_hbm.at[i_vmem.at[0]])

        pltpu.emit_pipeline(
            body,
            grid=(n_src // SC_WINDOW,),
            in_specs=[pl.BlockSpec((SC_WINDOW, CHUNK_W), index_map=lambda w: (w, 0))]
            + [pl.BlockSpec((1, SC_WINDOW), index_map=functools.partial(idx_block, k=k)) for k in range(copies)],
            out_specs=[],
            core_axis_name=("core", "subcore"),
            dimension_semantics=(pltpu.PARALLEL,),
        )(x_hbm, *([i_hbm] * copies))

    return scatter(rows, idx.reshape(1, n_idx))


def _sc_gather(table, idx):
    n_idx = idx.shape[0]

    @pl.kernel(out_type=jax.ShapeDtypeStruct((n_idx, CHUNK_W), table.dtype), mesh=_sc_mesh(),
               name="sc_combine_gather")
    def gather(t_hbm, i_hbm, o_hbm):
        def body(i_vmem, o_vmem):
            pltpu.sync_copy(t_hbm.at[i_vmem.at[0]], o_vmem)

        pltpu.emit_pipeline(
            body,
            grid=(n_idx // SC_WINDOW,),
            in_specs=[pl.BlockSpec((1, SC_WINDOW), index_map=lambda w: (0, w))],
            out_specs=[pl.BlockSpec((SC_WINDOW, CHUNK_W), index_map=lambda w: (w, 0))],
            core_axis_name=("core", "subcore"),
            dimension_semantics=(pltpu.PARALLEL,),
        )(i_hbm, o_hbm)

    return gather(table, idx.reshape(1, n_idx))


def _ffn_kernel(blk_e_ref, blk_valid_ref, n_used_ref, xs_ref, wg_ref, wu_ref, wd_ref, y_ref, wg_sc, wu_sc, wd_sc):
    b = pl.program_id(0)
    used = b < n_used_ref[0]
    new_expert = (b == 0) | (blk_e_ref[b] != blk_e_ref[jnp.maximum(b - 1, 0)])

    @pl.when(used & new_expert)
    def _():
        wg_sc[...] = wg_ref[...].astype(BF16)
        wu_sc[...] = wu_ref[...].astype(BF16)
        wd_sc[...] = wd_ref[...].astype(BF16)

    @pl.when(used)
    def _():
        for r0 in range(0, MOE_BLK, FFN_ROWS):
            rows = pl.ds(r0, FFN_ROWS)
            x = _load_chunks(lambda c: xs_ref.at[c, rows], valid_rows=blk_valid_ref[b] - r0).astype(BF16)
            g = jnp.dot(x, wg_sc[...], preferred_element_type=F32)
            u = jnp.dot(x, wu_sc[...], preferred_element_type=F32)
            hmid = (g * jax.nn.sigmoid(g) * u).astype(BF16)
            _store_chunks(lambda c: y_ref.at[c, rows], jnp.dot(hmid, wd_sc[...], preferred_element_type=F32))

    @pl.when(jnp.logical_not(used))
    def _():
        y_ref[...] = jnp.zeros_like(y_ref)


def _ffn(xs, blk_e, blk_valid, n_used, layer, wg, wu, wd, n_blocks):
    def blk(b, be, bv, nu):
        return jnp.maximum(jnp.minimum(b, nu[0] - 1), 0)

    def w_idx(b, be, bv, nu):
        return (layer, be[blk(b, be, bv, nu)], 0, 0)

    return pl.pallas_call(
        _ffn_kernel,
        name="expert_ffn",
        grid_spec=pltpu.PrefetchScalarGridSpec(
            num_scalar_prefetch=3,
            grid=(n_blocks,),
            in_specs=[
                pl.BlockSpec((ROW_CHUNKS, MOE_BLK, CHUNK_W), lambda b, be, bv, nu: (0, blk(b, be, bv, nu), 0)),
                pl.BlockSpec((None, None, D, D_EXPERT), w_idx),
                pl.BlockSpec((None, None, D, D_EXPERT), w_idx),
                pl.BlockSpec((None, None, D_EXPERT, D), w_idx),
            ],
            out_specs=pl.BlockSpec((ROW_CHUNKS, MOE_BLK, CHUNK_W), lambda b, be, bv, nu: (0, b, 0)),
            scratch_shapes=[pltpu.VMEM((D, D_EXPERT), BF16), pltpu.VMEM((D, D_EXPERT), BF16),
                            pltpu.VMEM((D_EXPERT, D), BF16)],
        ),
        out_shape=jax.ShapeDtypeStruct(xs.shape, U32),
        compiler_params=_params(("arbitrary",)),
    )(blk_e, blk_valid, n_used, xs, wg, wu, wd)


def _combine_kernel(x_ref, mod_ref, wt_ref, y_ref, sg_ref, su_ref, sd_ref, lg_ref, lb_ref, *o_refs, n_prompt_tiles):
    x = x_ref[...]
    hb = (x * (1.0 + mod_ref[4:5, :]) + mod_ref[3:4, :]).astype(BF16)
    g = jnp.dot(hb, sg_ref[...], preferred_element_type=F32)
    u = jnp.dot(hb, su_ref[...], preferred_element_type=F32)
    ff = jnp.dot((g * jax.nn.sigmoid(g) * u).astype(BF16), sd_ref[...], preferred_element_type=F32)
    for k in range(TOP_K):
        ff = ff + _load_chunks(lambda c: y_ref.at[c, k]) * wt_ref[:, k:k + 1]
    out = _layer_norm_rows(ALPHA * x + mod_ref[5:6, :] * ff, lg_ref[...], lb_ref[...])
    if len(o_refs) == 1:
        o_refs[0][...] = out
    else:
        is_prompt = pl.program_id(0) < n_prompt_tiles

        @pl.when(is_prompt)
        def _():
            o_refs[0][...] = out

        @pl.when(jnp.logical_not(is_prompt))
        def _():
            o_refs[1][...] = out


def _combine(geo, x, mod_l, wt, ytok, sg, su, sd, ln_g, ln_b, split=False):
    tb = TOK_TILE_L
    n_p = geo.t_prompt // tb
    full = lambda shape: pl.BlockSpec(shape, lambda i: (0,) * len(shape))
    if split:
        out_specs = [pl.BlockSpec((tb, D), lambda i: (jnp.minimum(i, n_p - 1), 0)),
                     pl.BlockSpec((tb, D), lambda i: (jnp.maximum(i - n_p, 0), 0))]
        out_shape = [jax.ShapeDtypeStruct((geo.t_prompt, D), F32), jax.ShapeDtypeStruct((geo.t_sample, D), F32)]
    else:
        out_specs = pl.BlockSpec((tb, D), lambda i: (i, 0))
        out_shape = jax.ShapeDtypeStruct((geo.t, D), F32)
    return pl.pallas_call(
        functools.partial(_combine_kernel, n_prompt_tiles=n_p),
        name="combine",
        grid=(geo.t // tb,),
        in_specs=[
            pl.BlockSpec((tb, D), lambda i: (i, 0)),
            pl.BlockSpec((None, 6, D), lambda i: (geo.cond_row(i, tb), 0, 0)),
            pl.BlockSpec((tb, TOP_K), lambda i: (i, 0)),
            pl.BlockSpec((ROW_CHUNKS, TOP_K, tb, CHUNK_W), lambda i: (0, 0, i, 0)),
            full((D, D_EXPERT)), full((D, D_EXPERT)), full((D_EXPERT, D)), full((1, D)), full((1, D)),
        ],
        out_specs=out_specs,
        out_shape=out_shape,
        compiler_params=_params(("arbitrary",)),
    )(x, mod_l, wt, ytok, sg.astype(BF16), su.astype(BF16), sd.astype(BF16),
      ln_g.reshape(1, D), ln_b.reshape(1, D))


def _moe_layer(geo, x, routing, mod_l, layer, wg, wu, wd, sg, su, sd, ln_g, ln_b, split=False):
    t = geo.t
    top_e, w, rank, counts, h = routing
    n_blocks = (t * TOP_K) // MOE_BLK + N_EXPERTS
    n_rows = n_blocks * MOE_BLK
    padded = (counts + MOE_BLK - 1) // MOE_BLK * MOE_BLK
    pend = jnp.cumsum(padded)
    pstart = (pend - padded).astype(jnp.int32)
    blk_e, blk_valid = _block_meta(pstart, counts, pend.astype(jnp.int32), n_blocks)
    n_used = (pend[-1:] // MOE_BLK).astype(jnp.int32)
    slots = _slots(geo, pstart, top_e, rank)
    idx = (slots.reshape(1, TOP_K * t) + (jnp.arange(ROW_CHUNKS, dtype=jnp.int32) * n_rows)[:, None]).reshape(-1)
    xs = _sc_scatter(h.reshape(ROW_CHUNKS * t, CHUNK_W), idx, ROW_CHUNKS * n_rows, TOP_K)
    yb = _ffn(xs.reshape(ROW_CHUNKS, n_rows, CHUNK_W), blk_e, blk_valid, n_used, layer, wg, wu, wd, n_blocks)
    ytok = _sc_gather(yb.reshape(ROW_CHUNKS * n_rows, CHUNK_W), idx)
    return _combine(geo, x, mod_l, w.T, ytok.reshape(ROW_CHUNKS, TOP_K, t, CHUNK_W), sg, su, sd, ln_g, ln_b,
                    split=split)


def _pos_embed(rows):
    quarter = D // 4
    omega = 1.0 / (POS_BASE ** (jnp.arange(quarter, dtype=F32) / quarter))
    r = jnp.arange(rows, dtype=F32)[:, None] * omega
    col = jnp.arange(GRID_W, dtype=F32)[:, None] * omega
    row_part = jnp.concatenate([jnp.sin(r), jnp.cos(r)], axis=-1)[:, None, :]
    col_part = jnp.concatenate([jnp.sin(col), jnp.cos(col)], axis=-1)[None, :, :]
    shape = (rows, GRID_W, 2 * quarter)
    return jnp.concatenate([jnp.broadcast_to(row_part, shape), jnp.broadcast_to(col_part, shape)],
                           axis=-1).reshape(rows * GRID_W, D)


def _mlstm_layer(geo, x, mod_l, j, a_w_in, a_b_gates, a_norm, a_w_out, ln_g, ln_b,
                 state_C, state_n, state_m, w_router, e_bias):
    q, kt, v, so, gr = _proj_a(geo, x, mod_l, a_w_in[j], a_b_gates[j])
    hp, c_p, n_p, m_p = _mlstm_scan(q, kt, v, gr, row0=0, n_seq=geo.n_prompt, seq_len=geo.prompt_len)
    ns = geo.n_sample
    n0 = jnp.pad(state_n[:, j].astype(F32)[..., None], ((0, 0),) * 4 + ((0, LANES - 1),))
    m0 = jnp.pad(state_m[:, j].astype(F32), ((0, 0), (0, 0), (0, SUBLANES - NH_A)))
    m0 = jnp.broadcast_to(m0[..., None], (ns, 2, SUBLANES, LANES))
    hs, _, _, _ = _mlstm_scan(q, kt, v, gr, row0=geo.t_prompt, n_seq=ns, seq_len=geo.sample_len,
                              state=(state_C[:, j].astype(F32), n0, m0))
    x1, routing = _out_a(geo, hp, hs, so, a_norm[j], a_w_out[j], x, mod_l, ln_g, ln_b, w_router, e_bias)
    return x1, routing, c_p, n_p[..., 0], m_p[:, :, :NH_A, 0]


def _hgrn_layer(geo, x, mod_l, j, lb_layer, b_w_in, b_norm, b_w_out, ln_g, ln_b, state_S, w_router, e_bias):
    q, pre, v, sg = _proj_b(geo, x, mod_l, b_w_in[j])
    lbd = lb_layer.reshape(2, NH_B, 1, DK_B)
    op, s_p = _hgrn_scan(q, pre, v, lbd, row0=0, n_seq=geo.n_prompt, seq_len=geo.prompt_len)
    os_, _ = _hgrn_scan(q, pre, v, lbd, row0=geo.t_prompt, n_seq=geo.n_sample, seq_len=geo.sample_len,
                        state=state_S[:, j].astype(F32))
    x1, routing = _out_b(geo, op, os_, sg, b_norm[j], b_w_out[j], x, mod_l, ln_g, ln_b, w_router, e_bias)
    return x1, routing, s_p


def kernel(x_prompt, x_sample, state_mlstm_C, state_mlstm_n, state_mlstm_m, state_hgrn_S, c, c_ctx, w_mod, b_mod, ln_g, ln_b, a_w_in, a_b_gates, a_norm, a_w_out, b_w_in, b_lb, b_norm, b_w_out, w_router, e_bias, w_gate, w_up, w_down, ws_gate, ws_up, ws_down):
    bp, sp, _ = x_prompt.shape
    bs, ss, _ = x_sample.shape
    cond = jnp.zeros((COND_ROWS, D), F32).at[0].set(c_ctx).at[1:1 + bs].set(c)
    mod = _modulation(cond, w_mod, b_mod)
    x = (x_prompt.reshape(-1, D), x_sample.reshape(-1, D), _pos_embed(ss // GRID_W))
    sm = jax.nn.softmax(b_lb.astype(F32), axis=0)
    lb_all = jnp.cumsum(sm, axis=0) - sm[0]
    geo = Geometry(bp, sp, bs, ss)
    x1, routing, new_c, new_n, new_m = _mlstm_layer(geo, x, mod[0], 0, a_w_in, a_b_gates, a_norm, a_w_out,
                                                    ln_g[0, 0], ln_b[0, 0], state_mlstm_C, state_mlstm_n,
                                                    state_mlstm_m, w_router[0], e_bias[0])
    x2 = _moe_layer(geo, x1, routing, mod[0], 0, w_gate, w_up, w_down, ws_gate[0], ws_up[0], ws_down[0],
                    ln_g[0, 1], ln_b[0, 1])
    x3, routing, new_s = _hgrn_layer(geo, x2, mod[1], 0, lb_all[1], b_w_in, b_norm, b_w_out, ln_g[1, 0], ln_b[1, 0],
                                     state_hgrn_S, w_router[1], e_bias[1])
    y_p, y_s = _moe_layer(geo, x3, routing, mod[1], 1, w_gate, w_up, w_down, ws_gate[1], ws_up[1], ws_down[1],
                          ln_g[1, 1], ln_b[1, 1], split=True)
    return (y_p.reshape(bp, sp, D), y_s.reshape(bs, ss, D), new_c[:, None], new_n[:, None], new_m[:, None],
            new_s[:, None])
```

```python
import functools
import math

import jax
import jax.numpy as jnp
from jax import lax
from jax.experimental import pallas as pl
from jax.experimental.pallas import tpu as pltpu
from jax.experimental.pallas import tpu_sc as plsc

F32 = jnp.float32
BF16 = jnp.bfloat16
HIGHEST = lax.Precision.HIGHEST

D = 1024
DEPTH = 2
GRID_W = 64
POS_BASE = 10000.0
EPS = 1e-6
ALPHA = (2.0 * DEPTH) ** 0.25
NH_A, DK_A, DV_A = 4, 128, 256
QK_A, V_A = NH_A * DK_A, NH_A * DV_A
NH_B, DK_B = 8, 128
N_EXPERTS, TOP_K, N_GROUPS, TOPK_GROUPS = 64, 8, 8, 4
GROUP_SIZE = N_EXPERTS // N_GROUPS
D_EXPERT = D // 4
ROUTED_SCALE = 2.5

LANES = 128
SUBLANES = 8
COND_ROWS = 8
TOK_TILE = 256
TOK_TILE_L = 512
SLOT_TILE = 2048
CHUNK_A = 256
VMEM_BYTES_V7X = 64 * 1024 * 1024
VMEM_LIMIT = VMEM_BYTES_V7X - 8 * 1024 * 1024

NT_DIMS = (((1,), (1,)), ((), ()))


def _params(sem):
    return pltpu.CompilerParams(dimension_semantics=sem, vmem_limit_bytes=VMEM_LIMIT)


def _split3(x):
    hi = x.astype(BF16)
    r = x - hi.astype(F32)
    mid = r.astype(BF16)
    lo = (r - mid.astype(F32)).astype(BF16)
    return hi, mid, lo


def _dot3(a_bf, x):
    hi, mid, lo = _split3(x)
    return (jnp.dot(a_bf, hi, preferred_element_type=F32)
            + jnp.dot(a_bf, mid, preferred_element_type=F32)
            + jnp.dot(a_bf, lo, preferred_element_type=F32))


def _dot3_r(x, a_bf):
    hi, mid, lo = _split3(x)
    return (jnp.dot(hi, a_bf, preferred_element_type=F32)
            + jnp.dot(mid, a_bf, preferred_element_type=F32)
            + jnp.dot(lo, a_bf, preferred_element_type=F32))


def _log_sigmoid(x):
    return jnp.minimum(x, 0.0) - jnp.log1p(jnp.exp(-jnp.abs(x)))


def _layer_norm_rows(x, g, b):
    mu = jnp.mean(x, axis=-1, keepdims=True)
    xc = x - mu
    var = jnp.mean(xc * xc, axis=-1, keepdims=True)
    return xc * lax.rsqrt(var + EPS) * g + b


class Geometry:
    def __init__(self, n_prompt, prompt_len, n_sample, sample_len):
        self.n_prompt, self.prompt_len = n_prompt, prompt_len
        self.n_sample, self.sample_len = n_sample, sample_len
        self.t_prompt = n_prompt * prompt_len
        self.t_sample = n_sample * sample_len
        self.t = self.t_prompt + self.t_sample
        assert self.t_prompt % TOK_TILE_L == 0 and sample_len % TOK_TILE_L == 0
        assert n_sample + 1 <= COND_ROWS

    def cond_row(self, tile, tile_rows):
        n_p = self.t_prompt // tile_rows
        return jnp.where(tile < n_p, 0, 1 + (tile - n_p) // (self.sample_len // tile_rows))


def _mod_kernel(cond_ref, w_ref, b_ref, o_ref):
    c = cond_ref[...]
    s = c * jax.nn.sigmoid(c)
    o_ref[0, 0] = jnp.dot(s, w_ref[0], precision=HIGHEST, preferred_element_type=F32) + b_ref[0, 0]


def _modulation(cond, w_mod, b_mod):
    out = pl.pallas_call(
        _mod_kernel,
        name="modulation",
        grid=(DEPTH, 6),
        in_specs=[
            pl.BlockSpec((COND_ROWS, D), lambda l, j: (0, 0)),
            pl.BlockSpec((1, D, D), lambda l, j: (l, 0, j)),
            pl.BlockSpec((1, 1, 1, D), lambda l, j: (l, j, 0, 0)),
        ],
        out_specs=pl.BlockSpec((1, 1, COND_ROWS, D), lambda l, j: (l, j, 0, 0)),
        out_shape=jax.ShapeDtypeStruct((DEPTH, 6, COND_ROWS, D), F32),
        compiler_params=_params(("arbitrary", "arbitrary")),
    )(cond, w_mod, b_mod.reshape(DEPTH, 6, 1, D))
    return out.transpose(0, 2, 1, 3)


def _embed_specs(geo, tb):
    n_p = geo.t_prompt // tb
    per_seq = geo.sample_len // tb
    return [pl.BlockSpec((tb, D), lambda i: (jnp.minimum(i, n_p - 1), 0)),
            pl.BlockSpec((tb, D), lambda i: (jnp.maximum(i - n_p, 0), 0)),
            pl.BlockSpec((tb, D), lambda i: (jnp.maximum(i - n_p, 0) % per_seq, 0))]


def _embed_tile(xp_ref, xs_ref, pos_ref, n_prompt_tiles):
    return jnp.where(pl.program_id(0) < n_prompt_tiles, xp_ref[...], xs_ref[...] + pos_ref[...])


def _proj_a_kernel(xp_ref, xs_ref, pos_ref, mod_ref, wq_ref, wkt_ref, wvo_ref, wgt_ref, bgt_ref,
                   q_ref, kt_ref, v_ref, so_ref, gr_ref, *, n_prompt_tiles):
    x = _embed_tile(xp_ref, xs_ref, pos_ref, n_prompt_tiles)
    h = x * (1.0 + mod_ref[1:2, :]) + mod_ref[0:1, :]
    hb = h.astype(BF16)
    q_ref[...] = jnp.dot(hb, wq_ref[...], preferred_element_type=F32).astype(BF16)
    kt = lax.dot_general(wkt_ref[...], hb, NT_DIMS, preferred_element_type=F32)
    kt_ref[...] = (kt * (DK_A ** -0.5)).astype(BF16)
    vo = jnp.dot(hb, wvo_ref[...], preferred_element_type=F32)
    v_ref[...] = vo[:, :V_A].astype(BF16)
    so_ref[...] = jax.nn.sigmoid(vo[:, V_A:]).astype(BF16)
    gr_ref[...] = lax.dot_general(wgt_ref[...], h, NT_DIMS, precision=HIGHEST,
                                  preferred_element_type=F32) + bgt_ref[...]


def _proj_a(geo, x, mod_l, w_in, b_gates):
    t = geo.t
    n_gate = 4 * NH_A
    wq = w_in[:, :QK_A].astype(BF16)
    wkt = w_in[:, QK_A:2 * QK_A].T.astype(BF16)
    wvo = w_in[:, 2 * QK_A:2 * QK_A + 2 * V_A].astype(BF16)
    wg = w_in[:, 2 * QK_A + 2 * V_A:]
    bg = b_gates.reshape(n_gate).astype(F32)
    tb = TOK_TILE_L
    full = lambda shape: pl.BlockSpec(shape, lambda i: (0,) * len(shape))
    return pl.pallas_call(
        functools.partial(_proj_a_kernel, n_prompt_tiles=geo.t_prompt // tb),
        name="proj_a",
        grid=(t // tb,),
        in_specs=_embed_specs(geo, tb) + [
            pl.BlockSpec((None, 6, D), lambda i: (geo.cond_row(i, tb), 0, 0)),
            full((D, QK_A)), full((QK_A, D)), full((D, 2 * V_A)), full((n_gate, D)), full((n_gate, 1)),
        ],
        out_specs=[
            pl.BlockSpec((tb, QK_A), lambda i: (i, 0)),
            pl.BlockSpec((QK_A, tb), lambda i: (0, i)),
            pl.BlockSpec((tb, V_A), lambda i: (i, 0)),
            pl.BlockSpec((tb, V_A), lambda i: (i, 0)),
            pl.BlockSpec((n_gate, tb), lambda i: (0, i)),
        ],
        out_shape=[
            jax.ShapeDtypeStruct((t, QK_A), BF16),
            jax.ShapeDtypeStruct((QK_A, t), BF16),
            jax.ShapeDtypeStruct((t, V_A), BF16),
            jax.ShapeDtypeStruct((t, V_A), BF16),
            jax.ShapeDtypeStruct((n_gate, t), F32),
        ],
        compiler_params=_params(("parallel",)),
    )(*x, mod_l, wq, wkt, wvo, wg.T, bg.reshape(n_gate, 1))


def _mlstm_scan_kernel(*refs, chunk, has_state):
    if has_state:
        (q_ref, kt_ref, v_ref, gr_ref, c0_ref, n0_ref, m0_ref,
         h_ref, c_out, n_out, m_out, c_sc, n_sc, m_sc) = refs
    else:
        (q_ref, kt_ref, v_ref, gr_ref,
         h_ref, c_out, n_out, m_out, c_sc, n_sc, m_sc) = refs
    L = chunk
    d = pl.program_id(1)
    c = pl.program_id(2)
    fwd = d == 0

    @pl.when(c == 0)
    def _():
        if has_state:
            c_sc[...] = c0_ref[0, 0]
            n_sc[...] = n0_ref[0, 0]
            m_sc[...] = m0_ref[0, 0]
        else:
            c_sc[...] = jnp.zeros_like(c_sc)
            n_sc[...] = jnp.zeros_like(n_sc)
            m_sc[...] = jnp.zeros_like(m_sc)

    row = lax.broadcasted_iota(jnp.int32, (L, L), 0)
    col = lax.broadcasted_iota(jnp.int32, (L, L), 1)
    sgn = 1 - 2 * d
    causal = (row - col) * sgn >= 0
    tri_t = ((col - row) * sgn >= 0).astype(BF16)

    gr = gr_ref[...]
    br_all = _dot3_r(_log_sigmoid(gr), tri_t)
    bc_all = jnp.concatenate([br_all, jnp.zeros((LANES - br_all.shape[0], L), F32)], axis=0).T
    ones_blk = (lax.broadcasted_iota(jnp.int32, (L, LANES), 1) == 0).astype(BF16)

    def gate_row(direction, gate, head):
        return (direction * 2 + gate) * NH_A + head

    for h in range(NH_A):
        ff, fb = gate_row(0, 1, h), gate_row(1, 1, h)
        gi, gb = gate_row(0, 0, h), gate_row(1, 0, h)
        b_c = jnp.where(fwd, bc_all[:, ff:ff + 1], bc_all[:, fb:fb + 1])
        b_r = jnp.where(fwd, br_all[ff:ff + 1, :], br_all[fb:fb + 1, :])
        i_r = jnp.where(fwd, gr[gi:gi + 1, :], gr[gb:gb + 1, :])
        bl = jnp.where(fwd, b_r[:, L - 1:L], b_r[:, 0:1])
        q = q_ref[:, h * DK_A:(h + 1) * DK_A]
        kt = kt_ref[h * DK_A:(h + 1) * DK_A, :]
        v = v_ref[:, h * DV_A:(h + 1) * DV_A]
        m = m_sc[h:h + 1, 0:1]
        cst = c_sc[h]
        nst = n_sc[h]

        a_r = i_r - b_r
        logd = jnp.where(causal, b_c + a_r, -jnp.inf)
        inter = b_c + m
        m_t = jnp.maximum(inter, jnp.max(logd, axis=1, keepdims=True))
        dmat = jnp.exp(logd - m_t)
        e_int = jnp.exp(inter - m_t)
        s = (jnp.dot(q, kt, preferred_element_type=F32) * dmat).astype(BF16)
        num = (jnp.dot(s, v, preferred_element_type=F32)
               + e_int * jnp.dot(q, cst.astype(BF16), preferred_element_type=F32))
        den = (jnp.dot(s, ones_blk, preferred_element_type=F32)
               + e_int * jnp.dot(q, nst.astype(BF16), preferred_element_type=F32))[:, 0:1]
        h_ref[:, h * DV_A:(h + 1) * DV_A] = (num / jnp.maximum(jnp.abs(den), jnp.exp(-m_t))).astype(BF16)

        logw = bl + a_r
        m_new = jnp.maximum(bl + m, jnp.max(logw, axis=1, keepdims=True))
        w = jnp.exp(logw - m_new)
        decay = jnp.exp(bl + m - m_new)
        kw = (kt.astype(F32) * w).astype(BF16)
        c_sc[h] = decay * cst + jnp.dot(kw, v, preferred_element_type=F32)
        n_sc[h] = decay * nst + jnp.dot(kw, ones_blk, preferred_element_type=F32)
        m_sc[h:h + 1, :] = jnp.broadcast_to(m_new, (1, LANES))

    @pl.when(c == pl.num_programs(2) - 1)
    def _():
        c_out[0, 0] = c_sc[...]
        n_out[0, 0] = n_sc[...]
        m_out[0, 0] = m_sc[...]


def _mlstm_scan(q, kt, v, gr, *, row0, n_seq, seq_len, state=None):
    L = CHUNK_A
    nc = seq_len // L
    blk0 = row0 // L

    def loc_blk(b, d, c):
        return b * nc + c + d * (nc - 1 - 2 * c)

    def tok_blk(b, d, c):
        return blk0 + loc_blk(b, d, c)

    in_specs = [
        pl.BlockSpec((L, QK_A), lambda b, d, c: (tok_blk(b, d, c), 0)),
        pl.BlockSpec((QK_A, L), lambda b, d, c: (0, tok_blk(b, d, c))),
        pl.BlockSpec((L, V_A), lambda b, d, c: (tok_blk(b, d, c), 0)),
        pl.BlockSpec((4 * NH_A, L), lambda b, d, c: (0, tok_blk(b, d, c))),
    ]
    args = [q, kt, v, gr]
    if state is not None:
        in_specs += [
            pl.BlockSpec((1, 1, NH_A, DK_A, DV_A), lambda b, d, c: (b, d, 0, 0, 0)),
            pl.BlockSpec((1, 1, NH_A, DK_A, LANES), lambda b, d, c: (b, d, 0, 0, 0)),
            pl.BlockSpec((1, 1, SUBLANES, LANES), lambda b, d, c: (b, d, 0, 0)),
        ]
        args += list(state)
    return pl.pallas_call(
        functools.partial(_mlstm_scan_kernel, chunk=L, has_state=state is not None),
        name="mlstm_scan_seeded" if state is not None else "mlstm_scan",
        grid=(n_seq, 2, nc),
        in_specs=in_specs,
        out_specs=[
            pl.BlockSpec((None, L, V_A), lambda b, d, c: (d, loc_blk(b, d, c), 0)),
            pl.BlockSpec((1, 1, NH_A, DK_A, DV_A), lambda b, d, c: (b, d, 0, 0, 0)),
            pl.BlockSpec((1, 1, NH_A, DK_A, LANES), lambda b, d, c: (b, d, 0, 0, 0)),
            pl.BlockSpec((1, 1, SUBLANES, LANES), lambda b, d, c: (b, d, 0, 0)),
        ],
        out_shape=[
            jax.ShapeDtypeStruct((2, n_seq * seq_len, V_A), BF16),
            jax.ShapeDtypeStruct((n_seq, 2, NH_A, DK_A, DV_A), F32),
            jax.ShapeDtypeStruct((n_seq, 2, NH_A, DK_A, LANES), F32),
            jax.ShapeDtypeStruct((n_seq, 2, SUBLANES, LANES), F32),
        ],
        scratch_shapes=[
            pltpu.VMEM((NH_A, DK_A, DV_A), F32),
            pltpu.VMEM((NH_A, DK_A, LANES), F32),
            pltpu.VMEM((SUBLANES, LANES), F32),
        ],
        compiler_params=_params(("parallel", "parallel", "arbitrary")),
    )(*args)


def _out_a_kernel(hp_ref, hs_ref, so_ref, nw_ref, w_ref, xp_ref, xs_ref, pos_ref, mod_ref, lg_ref, lb_ref,
                  wrt_ref, eb_ref, o_ref, *route_refs, n_prompt_tiles):
    is_prompt = pl.program_id(0) < n_prompt_tiles
    x = _embed_tile(xp_ref, xs_ref, pos_ref, n_prompt_tiles)
    y = jnp.where(is_prompt, hp_ref[0].astype(F32) + hp_ref[1].astype(F32),
                  hs_ref[0].astype(F32) + hs_ref[1].astype(F32))
    parts = []
    for h in range(NH_A):
        yh = y[:, h * DV_A:(h + 1) * DV_A]
        mu = jnp.mean(yh, axis=-1, keepdims=True)
        yc = yh - mu
        var = jnp.mean(yc * yc, axis=-1, keepdims=True)
        parts.append(yc * lax.rsqrt(var + EPS))
    yn = jnp.concatenate(parts, axis=-1) * nw_ref[...] * so_ref[...].astype(F32)
    out = jnp.dot(yn.astype(BF16), w_ref[...], preferred_element_type=F32)
    x1 = _layer_norm_rows(ALPHA * x + mod_ref[2:3, :] * out, lg_ref[...], lb_ref[...])
    o_ref[...] = x1
    _route_tile(x1, mod_ref, wrt_ref, eb_ref, *route_refs)


def _out_a(geo, h_prompt, h_sample, so, norm_w, w_out, x, mod_l, ln_g, ln_b, w_router, e_bias):
    t = geo.t
    tb = TOK_TILE_L
    n_p = geo.t_prompt // tb
    full = lambda shape: pl.BlockSpec(shape, lambda i: (0,) * len(shape))
    rio = _RouterIO(t, tb, w_router, e_bias)
    outs = pl.pallas_call(
        functools.partial(_out_a_kernel, n_prompt_tiles=n_p),
        name="out_a",
        grid=(t // tb,),
        in_specs=[
            pl.BlockSpec((2, tb, V_A), lambda i: (0, jnp.minimum(i, n_p - 1), 0)),
            pl.BlockSpec((2, tb, V_A), lambda i: (0, jnp.maximum(i - n_p, 0), 0)),
            pl.BlockSpec((tb, V_A), lambda i: (i, 0)),
            full((1, V_A)), full((V_A, D)),
        ] + _embed_specs(geo, tb) + [
            pl.BlockSpec((None, 6, D), lambda i: (geo.cond_row(i, tb), 0, 0)),
            full((1, D)), full((1, D)),
        ] + rio.in_specs,
        out_specs=[pl.BlockSpec((tb, D), lambda i: (i, 0))] + rio.out_specs,
        out_shape=[jax.ShapeDtypeStruct((t, D), F32)] + rio.out_shape,
        scratch_shapes=rio.scratch,
        compiler_params=_params(("arbitrary",)),
    )(h_prompt, h_sample, so, norm_w.reshape(1, V_A).astype(F32), w_out.astype(BF16), *x, mod_l,
      ln_g.reshape(1, D), ln_b.reshape(1, D), *rio.inputs)
    return outs[0], _RouterIO.unpack(outs[1:])


def _proj_b_tile(x, mod_ref, w_ref, q_ref, pre_ref, v_ref, sg_ref):
    h = x * (1.0 + mod_ref[1:2, :]) + mod_ref[0:1, :]
    z = jnp.dot(h.astype(BF16), w_ref[...], preferred_element_type=F32)
    for hd in range(NH_B):
        lo = hd * DK_B
        qh = z[:, lo:lo + DK_B]
        q_ref[hd] = qh * jax.nn.sigmoid(qh)
        pre_ref[0, hd] = z[:, D + lo:D + lo + DK_B]
        pre_ref[1, hd] = z[:, 2 * D + lo:2 * D + lo + DK_B]
        v_ref[hd] = z[:, 3 * D + lo:3 * D + lo + DK_B].astype(BF16)
    g = z[:, 4 * D:]
    sg_ref[...] = (g * jax.nn.sigmoid(g)).astype(BF16)


class _ProjBIO:
    def __init__(self, geo, tb, mod_l, w_in):
        t = geo.t
        self.inputs = [mod_l, w_in.astype(BF16)]
        self.in_specs = [
            pl.BlockSpec((None, 6, D), lambda i: (geo.cond_row(i, tb), 0, 0)),
            pl.BlockSpec((D, 5 * D), lambda i: (0, 0), pipeline_mode=pl.Buffered(1)),
        ]
        self.out_specs = [
            pl.BlockSpec((NH_B, tb, DK_B), lambda i: (0, i, 0)),
            pl.BlockSpec((2, NH_B, tb, DK_B), lambda i: (0, 0, i, 0)),
            pl.BlockSpec((NH_B, tb, DK_B), lambda i: (0, i, 0)),
            pl.BlockSpec((tb, D), lambda i: (i, 0)),
        ]
        self.out_shape = [
            jax.ShapeDtypeStruct((NH_B, t, DK_B), F32),
            jax.ShapeDtypeStruct((2, NH_B, t, DK_B), F32),
            jax.ShapeDtypeStruct((NH_B, t, DK_B), BF16),
            jax.ShapeDtypeStruct((t, D), BF16),
        ]


CHUNK_B = 128
BAND = SUBLANES // 2
TN_DIMS = (((0,), (0,)), ((), ()))


def _hgrn_head(q, pre, lbv, v_bf, st, fwd):
    L = q.shape[0]
    sg = jax.nn.sigmoid(pre)
    f = lbv + (1.0 - lbv) * sg
    lf = jnp.log(f)
    kk = (1.0 - lbv) * (1.0 - sg)
    row = lax.broadcasted_iota(jnp.int32, (L, L), 0)
    col = lax.broadcasted_iota(jnp.int32, (L, L), 1)
    tri = ((row >= col) if fwd else (row <= col)).astype(BF16)
    b = _dot3(tri, lf)
    tpos = lax.broadcasted_iota(jnp.int32, (L, DK_B), 0)
    blk_bits = row ^ col
    lag = jnp.where(blk_bits < BAND, (row - col) if fwd else (col - row), -1)

    step = 1 if fwd else L - 1
    att = jnp.where(lag == 0, jnp.sum(q * kk, axis=1, keepdims=True), 0.0)
    f_r, kk_r, g = f, kk, f
    for dl in range(1, BAND):
        if dl > 1:
            f_r = pltpu.roll(f_r, step, 0)
            g = g * f_r
        kk_r = pltpu.roll(kk_r, step, 0)
        att = jnp.where(lag == dl, jnp.sum(q * kk_r * g, axis=1, keepdims=True), att)

    w = BAND
    while w < L:
        nb = L // (2 * w)
        b3 = b.reshape(nb, 2 * w, DK_B)
        edge = (b3[:, w - 1:w, :] if fwd else b3[:, w:w + 1, :])
        bmid = jnp.broadcast_to(edge, (nb, 2 * w, DK_B)).reshape(L, DK_B)
        second = (tpos & w) != 0
        t_side = second if fwd else jnp.logical_not(second)
        e = jnp.exp(jnp.where(t_side, b - bmid, bmid - b))
        qt = jnp.where(t_side, q * e, 0.0).astype(BF16)
        ks = jnp.where(t_side, 0.0, kk * e).astype(BF16)
        a = lax.dot_general(qt, ks, NT_DIMS, preferred_element_type=F32)
        att = att + jnp.where(blk_bits < 2 * w, a, 0.0)
        w *= 2
    o = jnp.dot(att.astype(BF16), v_bf, preferred_element_type=F32)

    bl = b[L - 1:L, :] if fwd else b[0:1, :]
    o = o + lax.dot_general((q * jnp.exp(b)).astype(BF16), st.astype(BF16), NT_DIMS, preferred_element_type=F32)
    kd = (kk * jnp.exp(bl - b)).astype(BF16)
    st_new = jnp.exp(bl) * st + lax.dot_general(v_bf, kd, TN_DIMS, preferred_element_type=F32)
    return o, st_new


def _hgrn_scan_kernel(*refs, has_state):
    if has_state:
        q_ref, pre_ref, v_ref, lb_ref, s0_ref, o_ref, s_out, st_sc = refs
    else:
        q_ref, pre_ref, v_ref, lb_ref, o_ref, s_out, st_sc = refs
    d = pl.program_id(1)
    c = pl.program_id(2)

    @pl.when(c == 0)
    def _():
        if has_state:
            for hd in range(NH_B):
                st_sc[hd] = s0_ref[0, 0, hd].T
        else:
            st_sc[...] = jnp.zeros_like(st_sc)

    def run(fwd):
        def head(hd, carry):
            o, st_new = _hgrn_head(q_ref[hd], pre_ref[hd], lb_ref[hd], v_ref[hd], st_sc[hd], fwd)
            o_ref[hd] = o.astype(BF16)
            st_sc[hd] = st_new
            return carry
        lax.fori_loop(0, NH_B, head, 0, unroll=8)

    @pl.when(d == 0)
    def _():
        run(True)

    @pl.when(d == 1)
    def _():
        run(False)

    @pl.when(c == pl.num_programs(2) - 1)
    def _():
        for hd in range(NH_B):
            s_out[0, 0, hd] = st_sc[hd].T


def _hgrn_scan(q, pre, v, lbd, *, row0, n_seq, seq_len, state=None):
    L = CHUNK_B
    nc = seq_len // L
    blk0 = row0 // L

    def loc_blk(b, d, c):
        return b * nc + c + d * (nc - 1 - 2 * c)

    def tok_blk(b, d, c):
        return blk0 + loc_blk(b, d, c)

    in_specs = [
        pl.BlockSpec((NH_B, L, DK_B), lambda b, d, c: (0, tok_blk(b, d, c), 0)),
        pl.BlockSpec((None, NH_B, L, DK_B), lambda b, d, c: (d, 0, tok_blk(b, d, c), 0)),
        pl.BlockSpec((NH_B, L, DK_B), lambda b, d, c: (0, tok_blk(b, d, c), 0)),
        pl.BlockSpec((None, NH_B, 1, DK_B), lambda b, d, c: (d, 0, 0, 0)),
    ]
    args = [q, pre, v, lbd]
    if state is not None:
        in_specs.append(pl.BlockSpec((1, 1, NH_B, DK_B, DK_B), lambda b, d, c: (b, d, 0, 0, 0)))
        args.append(state)
    return pl.pallas_call(
        functools.partial(_hgrn_scan_kernel, has_state=state is not None),
        name="hgrn_scan_seeded" if state is not None else "hgrn_scan",
        grid=(n_seq, 2, nc),
        in_specs=in_specs,
        out_specs=[
            pl.BlockSpec((None, NH_B, L, DK_B), lambda b, d, c: (d, 0, loc_blk(b, d, c), 0)),
            pl.BlockSpec((1, 1, NH_B, DK_B, DK_B), lambda b, d, c: (b, d, 0, 0, 0)),
        ],
        out_shape=[
            jax.ShapeDtypeStruct((2, NH_B, n_seq * seq_len, DK_B), BF16),
            jax.ShapeDtypeStruct((n_seq, 2, NH_B, DK_B, DK_B), F32),
        ],
        scratch_shapes=[pltpu.VMEM((NH_B, DK_B, DK_B), F32)],
        compiler_params=_params(("parallel", "parallel", "arbitrary")),
    )(*args)


def _out_b_kernel(op_ref, os_ref, sg_ref, nw_ref, w_ref, x_ref, mod_ref, lg_ref, lb_ref, wrt_ref, eb_ref,
                  out_ref, *route_refs, n_prompt_tiles):
    is_prompt = pl.program_id(0) < n_prompt_tiles
    parts = []
    for hd in range(NH_B):
        y = jnp.where(is_prompt, op_ref[0, hd].astype(F32) + op_ref[1, hd].astype(F32),
                      os_ref[0, hd].astype(F32) + os_ref[1, hd].astype(F32))
        parts.append(y * lax.rsqrt(jnp.mean(y * y, axis=-1, keepdims=True) + EPS))
    yn = jnp.concatenate(parts, axis=-1) * nw_ref[...] * sg_ref[...].astype(F32)
    out = jnp.dot(yn.astype(BF16), w_ref[...], preferred_element_type=F32)
    x1 = _layer_norm_rows(ALPHA * x_ref[...] + mod_ref[2:3, :] * out, lg_ref[...], lb_ref[...])
    out_ref[...] = x1
    _route_tile(x1, mod_ref, wrt_ref, eb_ref, *route_refs)


def _out_b(geo, o_prompt, o_sample, sg, norm_w, w_out, x, mod_l, ln_g, ln_b, w_router, e_bias):
    t = geo.t
    tb = TOK_TILE_L
    n_p = geo.t_prompt // tb
    full = lambda shape: pl.BlockSpec(shape, lambda i: (0,) * len(shape))
    rio = _RouterIO(t, tb, w_router, e_bias)
    outs = pl.pallas_call(
        functools.partial(_out_b_kernel, n_prompt_tiles=n_p),
        name="out_b",
        grid=(t // tb,),
        in_specs=[
            pl.BlockSpec((2, NH_B, tb, DK_B), lambda i: (0, 0, jnp.minimum(i, n_p - 1), 0)),
            pl.BlockSpec((2, NH_B, tb, DK_B), lambda i: (0, 0, jnp.maximum(i - n_p, 0), 0)),
            pl.BlockSpec((tb, D), lambda i: (i, 0)),
            full((1, D)), full((D, D)),
            pl.BlockSpec((tb, D), lambda i: (i, 0)),
            pl.BlockSpec((None, 6, D), lambda i: (geo.cond_row(i, tb), 0, 0)),
            full((1, D)), full((1, D)),
        ] + rio.in_specs,
        out_specs=[pl.BlockSpec((tb, D), lambda i: (i, 0))] + rio.out_specs,
        out_shape=[jax.ShapeDtypeStruct((t, D), F32)] + rio.out_shape,
        scratch_shapes=rio.scratch,
        compiler_params=_params(("arbitrary",)),
    )(o_prompt, o_sample, sg, norm_w.reshape(1, D).astype(F32), w_out.astype(BF16), x, mod_l,
      ln_g.reshape(1, D), ln_b.reshape(1, D), *rio.inputs)
    return outs[0], _RouterIO.unpack(outs[1:])


MOE_BLK = 1024
U32 = jnp.uint32
ROW_WORDS = D // 2
CHUNK_W = 256
ROW_CHUNKS = ROW_WORDS // CHUNK_W
SC_WINDOW = 128


def _pack_rows(x):
    return pltpu.pack_elementwise([x[:, :ROW_WORDS], x[:, ROW_WORDS:]], packed_dtype=BF16)


def _unpack_rows(words):
    return jnp.concatenate([pltpu.unpack_elementwise(words, index=i, packed_dtype=BF16, unpacked_dtype=F32)
                            for i in range(2)], axis=1)


def _store_chunks(chunk_ref, x):
    words = _pack_rows(x)
    for c in range(ROW_CHUNKS):
        chunk_ref(c)[...] = words[:, c * CHUNK_W:(c + 1) * CHUNK_W]


def _load_chunks(chunk_ref, valid_rows=None):
    words = jnp.concatenate([chunk_ref(c)[...] for c in range(ROW_CHUNKS)], axis=1)
    if valid_rows is not None:
        row = lax.broadcasted_iota(jnp.int32, (words.shape[0], 1), 0)
        words = jnp.where(row < valid_rows, words, jnp.uint32(0))
    return _unpack_rows(words)


def _first_index(hit, iota, size, axis):
    return jnp.min(jnp.where(hit, iota, size), axis=axis, keepdims=True)


def _route_tile(x, mod_ref, wrt_ref, eb_ref, e_ref, w_ref, r_ref, cnt_ref, h_ref, cnt_sc):
    i = pl.program_id(0)
    tb = x.shape[0]

    @pl.when(i == 0)
    def _():
        cnt_sc[...] = jnp.zeros_like(cnt_sc)

    h = x * (1.0 + mod_ref[4:5, :]) + mod_ref[3:4, :]
    _store_chunks(lambda c: h_ref.at[c], h)
    logits = lax.dot_general(wrt_ref[...], h, NT_DIMS, precision=HIGHEST, preferred_element_type=F32)
    scores = jax.nn.sigmoid(logits)
    sel = scores + eb_ref[...]

    g3 = sel.reshape(N_GROUPS, GROUP_SIZE, tb)
    io3 = lax.broadcasted_iota(jnp.int32, g3.shape, 1)
    m1 = jnp.max(g3, axis=1, keepdims=True)
    first = _first_index(g3 == m1, io3, GROUP_SIZE, 1)
    m2 = jnp.max(jnp.where(io3 == first, -jnp.inf, g3), axis=1, keepdims=True)
    gscore = (m1 + m2).reshape(N_GROUPS, tb)

    iog = lax.broadcasted_iota(jnp.int32, gscore.shape, 0)
    gmask = jnp.zeros(gscore.shape, F32)
    for _ in range(TOPK_GROUPS):
        gm = jnp.max(gscore, axis=0, keepdims=True)
        pick = iog == _first_index(gscore == gm, iog, N_GROUPS, 0)
        gmask = jnp.where(pick, 1.0, gmask)
        gscore = jnp.where(pick, -jnp.inf, gscore)
    emask = jnp.broadcast_to(gmask.reshape(N_GROUPS, 1, tb), (N_GROUPS, GROUP_SIZE, tb)).reshape(N_EXPERTS, tb)
    cand = jnp.where(emask > 0.0, sel, -jnp.inf)

    ioe = lax.broadcasted_iota(jnp.int32, cand.shape, 0)
    picks, wts = [], []
    onehot = jnp.zeros(cand.shape, F32)
    for _ in range(TOP_K):
        cm = jnp.max(cand, axis=0, keepdims=True)
        idx = _first_index(cand == cm, ioe, N_EXPERTS, 0)
        pick = ioe == idx
        picks.append(pick)
        wts.append(jnp.sum(jnp.where(pick, scores, 0.0), axis=0, keepdims=True))
        onehot = onehot + pick.astype(F32)
        cand = jnp.where(pick, -jnp.inf, cand)
        e_ref[pl.ds(len(picks) - 1, 1), :] = idx
    wsum = wts[0]
    for wk in wts[1:]:
        wsum = wsum + wk
    for k in range(TOP_K):
        w_ref[pl.ds(k, 1), :] = wts[k] / wsum * ROUTED_SCALE

    r_io = lax.broadcasted_iota(jnp.int32, (tb, tb), 0)
    c_io = lax.broadcasted_iota(jnp.int32, (tb, tb), 1)
    before = (r_io < c_io).astype(BF16)
    rank = cnt_sc[:, 0:1] + jnp.dot(onehot.astype(BF16), before, preferred_element_type=F32)
    for k in range(TOP_K):
        r_ref[pl.ds(k, 1), :] = jnp.sum(jnp.where(picks[k], rank, 0.0), axis=0, keepdims=True).astype(jnp.int32)
    cnt_sc[...] = cnt_sc[...] + jnp.sum(onehot, axis=1, keepdims=True)
    cnt_ref[...] = cnt_sc[...]


class _RouterIO:
    def __init__(self, t, tb, w_router, e_bias):
        full = lambda shape: pl.BlockSpec(shape, lambda i: (0,) * len(shape))
        self.inputs = [w_router.T.astype(F32), e_bias.reshape(N_EXPERTS, 1).astype(F32)]
        self.in_specs = [full((N_EXPERTS, D)), full((N_EXPERTS, 1))]
        self.out_specs = [
            pl.BlockSpec((TOP_K, tb), lambda i: (0, i)),
            pl.BlockSpec((TOP_K, tb), lambda i: (0, i)),
            pl.BlockSpec((TOP_K, tb), lambda i: (0, i)),
            full((N_EXPERTS, LANES)),
            pl.BlockSpec((ROW_CHUNKS, tb, CHUNK_W), lambda i: (0, i, 0)),
        ]
        self.out_shape = [
            jax.ShapeDtypeStruct((TOP_K, t), jnp.int32),
            jax.ShapeDtypeStruct((TOP_K, t), F32),
            jax.ShapeDtypeStruct((TOP_K, t), jnp.int32),
            jax.ShapeDtypeStruct((N_EXPERTS, LANES), F32),
            jax.ShapeDtypeStruct((ROW_CHUNKS, t, CHUNK_W), U32),
        ]
        self.scratch = [pltpu.VMEM((N_EXPERTS, LANES), F32)]

    @staticmethod
    def unpack(outs):
        e, w, r, cnt, h = outs
        return e, w, r, cnt[:, 0].astype(jnp.int32), h


def _slot_kernel(pstart_ref, e_ref, r_ref, o_ref):
    e = e_ref[...]
    slot = r_ref[...]
    for x in range(N_EXPERTS):
        slot = slot + jnp.where(e == x, pstart_ref[x], 0)
    o_ref[...] = slot


def _slots(geo, pstart, top_e, rank):
    tb = math.gcd(SLOT_TILE, geo.t)
    return pl.pallas_call(
        _slot_kernel,
        name="slots",
        grid_spec=pltpu.PrefetchScalarGridSpec(
            num_scalar_prefetch=1,
            grid=(geo.t // tb,),
            in_specs=[pl.BlockSpec((TOP_K, tb), lambda i, p: (0, i)),
                      pl.BlockSpec((TOP_K, tb), lambda i, p: (0, i))],
            out_specs=pl.BlockSpec((TOP_K, tb), lambda i, p: (0, i)),
        ),
        out_shape=jax.ShapeDtypeStruct((TOP_K, geo.t), jnp.int32),
        compiler_params=_params(("parallel",)),
    )(pstart, top_e, rank)


def _block_meta_kernel(pstart_ref, counts_ref, pend_ref, e_ref, v_ref):
    row0 = lax.broadcasted_iota(jnp.int32, e_ref.shape, 1) * MOE_BLK
    blk_e = jnp.zeros(e_ref.shape, jnp.int32)
    for x in range(N_EXPERTS):
        blk_e = blk_e + jnp.where(pend_ref[x] <= row0, 1, 0)
    blk_e = jnp.minimum(blk_e, N_EXPERTS - 1)
    last = jnp.zeros(e_ref.shape, jnp.int32)
    for x in range(N_EXPERTS):
        last = last + jnp.where(blk_e == x, pstart_ref[x] + counts_ref[x], 0)
    e_ref[...] = blk_e
    v_ref[...] = jnp.clip(last - row0, 0, MOE_BLK)


def _block_meta(pstart, counts, pend, n_blocks):
    e, v = pl.pallas_call(
        _block_meta_kernel,
        name="block_meta",
        grid_spec=pltpu.PrefetchScalarGridSpec(
            num_scalar_prefetch=3,
            grid=(1,),
            in_specs=[],
            out_specs=[pl.BlockSpec((1, n_blocks), lambda i, a, b, c: (0, 0)),
                       pl.BlockSpec((1, n_blocks), lambda i, a, b, c: (0, 0))],
        ),
        out_shape=[jax.ShapeDtypeStruct((1, n_blocks), jnp.int32), jax.ShapeDtypeStruct((1, n_blocks), jnp.int32)],
        compiler_params=_params(("arbitrary",)),
    )(pstart, counts, pend)
    return e[0], v[0]


def _sc_mesh():
    return plsc.VectorSubcoreMesh(core_axis_name="core", subcore_axis_name="subcore")


def _sc_scatter(rows, idx, n_out, copies):
    n_src = rows.shape[0]
    n_idx = idx.shape[0]
    groups = ROW_CHUNKS
    win_per_group = n_src // groups // SC_WINDOW

    def idx_block(w, k):
        return (0, ((w // win_per_group) * copies + k) * win_per_group + w % win_per_group)

    @pl.kernel(out_type=jax.ShapeDtypeStruct((n_out, CHUNK_W), rows.dtype), mesh=_sc_mesh(), scratch_types=[],
               name="sc_dispatch")
    def scatter(x_hbm, i_hbm, o_hbm):
        def body(x_vmem, *i_vmems):
            for i_vmem in i_vmems:
                pltpu.sync_copy(x_vmem, o_hbm.at[i_vmem.at[0]])

        pltpu.emit_pipeline(
            body,
            grid=(n_src // SC_WINDOW,),
            in_specs=[pl.BlockSpec((SC_WINDOW, CHUNK_W), index_map=lambda w: (w, 0))]
            + [pl.BlockSpec((1, SC_WINDOW), index_map=functools.partial(idx_block, k=k)) for k in range(copies)],
            out_specs=[],
            core_axis_name=("core", "subcore"),
            dimension_semantics=(pltpu.PARALLEL,),
        )(x_hbm, *([i_hbm] * copies))

    return scatter(rows, idx.reshape(1, n_idx))


def _sc_gather(table, idx):
    n_idx = idx.shape[0]

    @pl.kernel(out_type=jax.ShapeDtypeStruct((n_idx, CHUNK_W), table.dtype), mesh=_sc_mesh(),
               name="sc_combine_gather")
    def gather(t_hbm, i_hbm, o_hbm):
        def body(i_vmem, o_vmem):
            pltpu.sync_copy(t_hbm.at[i_vmem.at[0]], o_vmem)

        pltpu.emit_pipeline(
            body,
            grid=(n_idx // SC_WINDOW,),
            in_specs=[pl.BlockSpec((1, SC_WINDOW), index_map=lambda w: (0, w))],
            out_specs=[pl.BlockSpec((SC_WINDOW, CHUNK_W), index_map=lambda w: (w, 0))],
            core_axis_name=("core", "subcore"),
            dimension_semantics=(pltpu.PARALLEL,),
        )(i_hbm, o_hbm)

    return gather(table, idx.reshape(1, n_idx))


def _ffn_kernel(blk_e_ref, blk_valid_ref, n_used_ref, xs_ref, wg_ref, wu_ref, wd_ref, y_ref, wg_sc, wu_sc, wd_sc):
    b = pl.program_id(0)
    used = b < n_used_ref[0]
    new_expert = (b == 0) | (blk_e_ref[b] != blk_e_ref[jnp.maximum(b - 1, 0)])

    @pl.when(used & new_expert)
    def _():
        wg_sc[...] = wg_ref[...].astype(BF16)
        wu_sc[...] = wu_ref[...].astype(BF16)
        wd_sc[...] = wd_ref[...].astype(BF16)

    @pl.when(used)
    def _():
        x = _load_chunks(lambda c: xs_ref.at[c], valid_rows=blk_valid_ref[b]).astype(BF16)
        g = jnp.dot(x, wg_sc[...], preferred_element_type=F32)
        u = jnp.dot(x, wu_sc[...], preferred_element_type=F32)
        hmid = (g * jax.nn.sigmoid(g) * u).astype(BF16)
        _store_chunks(lambda c: y_ref.at[c], jnp.dot(hmid, wd_sc[...], preferred_element_type=F32))

    @pl.when(jnp.logical_not(used))
    def _():
        y_ref[...] = jnp.zeros_like(y_ref)


def _ffn(xs, blk_e, blk_valid, n_used, layer, wg, wu, wd, n_blocks):
    def blk(b, be, bv, nu):
        return jnp.maximum(jnp.minimum(b, nu[0] - 1), 0)

    def w_idx(b, be, bv, nu):
        return (layer, be[blk(b, be, bv, nu)], 0, 0)

    return pl.pallas_call(
        _ffn_kernel,
        name="expert_ffn",
        grid_spec=pltpu.PrefetchScalarGridSpec(
            num_scalar_prefetch=3,
            grid=(n_blocks,),
            in_specs=[
                pl.BlockSpec((ROW_CHUNKS, MOE_BLK, CHUNK_W), lambda b, be, bv, nu: (0, blk(b, be, bv, nu), 0)),
                pl.BlockSpec((None, None, D, D_EXPERT), w_idx),
                pl.BlockSpec((None, None, D, D_EXPERT), w_idx),
                pl.BlockSpec((None, None, D_EXPERT, D), w_idx),
            ],
            out_specs=pl.BlockSpec((ROW_CHUNKS, MOE_BLK, CHUNK_W), lambda b, be, bv, nu: (0, b, 0)),
            scratch_shapes=[pltpu.VMEM((D, D_EXPERT), BF16), pltpu.VMEM((D, D_EXPERT), BF16),
                            pltpu.VMEM((D_EXPERT, D), BF16)],
        ),
        out_shape=jax.ShapeDtypeStruct(xs.shape, U32),
        compiler_params=_params(("arbitrary",)),
    )(blk_e, blk_valid, n_used, xs, wg, wu, wd)


def _combine_kernel(x_ref, mod_ref, wt_ref, y_ref, sg_ref, su_ref, sd_ref, lg_ref, lb_ref, *rest, n_prompt_tiles,
                    n_out, fuse_proj_b):
    n_proj_in = 2 if fuse_proj_b else 0
    proj_in, o_refs, proj_out = rest[:n_proj_in], rest[n_proj_in:n_proj_in + n_out], rest[n_proj_in + n_out:]
    x = x_ref[...]
    hb = (x * (1.0 + mod_ref[4:5, :]) + mod_ref[3:4, :]).astype(BF16)
    g = jnp.dot(hb, sg_ref[...], preferred_element_type=F32)
    u = jnp.dot(hb, su_ref[...], preferred_element_type=F32)
    ff = jnp.dot((g * jax.nn.sigmoid(g) * u).astype(BF16), sd_ref[...], preferred_element_type=F32)
    for k in range(TOP_K):
        ff = ff + _load_chunks(lambda c: y_ref.at[c, k]) * wt_ref[:, k:k + 1]
    out = _layer_norm_rows(ALPHA * x + mod_ref[5:6, :] * ff, lg_ref[...], lb_ref[...])
    if len(o_refs) == 1:
        o_refs[0][...] = out
    else:
        is_prompt = pl.program_id(0) < n_prompt_tiles

        @pl.when(is_prompt)
        def _():
            o_refs[0][...] = out

        @pl.when(jnp.logical_not(is_prompt))
        def _():
            o_refs[1][...] = out

    if fuse_proj_b:
        _proj_b_tile(out, *proj_in, *proj_out)


def _combine(geo, x, mod_l, wt, ytok, sg, su, sd, ln_g, ln_b, split=False, next_proj_b=None):
    tb = TOK_TILE if next_proj_b is not None else TOK_TILE_L
    n_p = geo.t_prompt // tb
    full = lambda shape: pl.BlockSpec(shape, lambda i: (0,) * len(shape))
    if split:
        out_specs = [pl.BlockSpec((tb, D), lambda i: (jnp.minimum(i, n_p - 1), 0)),
                     pl.BlockSpec((tb, D), lambda i: (jnp.maximum(i - n_p, 0), 0))]
        out_shape = [jax.ShapeDtypeStruct((geo.t_prompt, D), F32), jax.ShapeDtypeStruct((geo.t_sample, D), F32)]
    else:
        out_specs = [pl.BlockSpec((tb, D), lambda i: (i, 0))]
        out_shape = [jax.ShapeDtypeStruct((geo.t, D), F32)]
    n_out = len(out_shape)
    pio = _ProjBIO(geo, tb, *next_proj_b) if next_proj_b is not None else None
    outs = pl.pallas_call(
        functools.partial(_combine_kernel, n_prompt_tiles=n_p, n_out=n_out, fuse_proj_b=pio is not None),
        name="combine",
        grid=(geo.t // tb,),
        in_specs=[
            pl.BlockSpec((tb, D), lambda i: (i, 0)),
            pl.BlockSpec((None, 6, D), lambda i: (geo.cond_row(i, tb), 0, 0)),
            pl.BlockSpec((tb, TOP_K), lambda i: (i, 0)),
            pl.BlockSpec((ROW_CHUNKS, TOP_K, tb, CHUNK_W), lambda i: (0, 0, i, 0)),
            full((D, D_EXPERT)), full((D, D_EXPERT)), full((D_EXPERT, D)), full((1, D)), full((1, D)),
        ] + (pio.in_specs if pio else []),
        out_specs=out_specs + (pio.out_specs if pio else []),
        out_shape=out_shape + (pio.out_shape if pio else []),
        compiler_params=_params(("arbitrary",)),
    )(x, mod_l, wt, ytok, sg.astype(BF16), su.astype(BF16), sd.astype(BF16),
      ln_g.reshape(1, D), ln_b.reshape(1, D), *(pio.inputs if pio else []))
    layer_out = tuple(outs[:n_out]) if split else outs[0]
    return (layer_out, tuple(outs[n_out:])) if pio else layer_out


def _moe_layer(geo, x, routing, mod_l, layer, wg, wu, wd, sg, su, sd, ln_g, ln_b, split=False, next_proj_b=None):
    t = geo.t
    top_e, w, rank, counts, h = routing
    n_blocks = (t * TOP_K) // MOE_BLK + N_EXPERTS
    n_rows = n_blocks * MOE_BLK
    padded = (counts + MOE_BLK - 1) // MOE_BLK * MOE_BLK
    pend = jnp.cumsum(padded)
    pstart = (pend - padded).astype(jnp.int32)
    blk_e, blk_valid = _block_meta(pstart, counts, pend.astype(jnp.int32), n_blocks)
    n_used = (pend[-1:] // MOE_BLK).astype(jnp.int32)
    slots = _slots(geo, pstart, top_e, rank)
    idx = (slots.reshape(1, TOP_K * t) + (jnp.arange(ROW_CHUNKS, dtype=jnp.int32) * n_rows)[:, None]).reshape(-1)
    xs = _sc_scatter(h.reshape(ROW_CHUNKS * t, CHUNK_W), idx, ROW_CHUNKS * n_rows, TOP_K)
    yb = _ffn(xs.reshape(ROW_CHUNKS, n_rows, CHUNK_W), blk_e, blk_valid, n_used, layer, wg, wu, wd, n_blocks)
    ytok = _sc_gather(yb.reshape(ROW_CHUNKS * n_rows, CHUNK_W), idx)
    return _combine(geo, x, mod_l, w.T, ytok.reshape(ROW_CHUNKS, TOP_K, t, CHUNK_W), sg, su, sd, ln_g, ln_b,
                    split=split, next_proj_b=next_proj_b)


def _pos_embed(rows):
    quarter = D // 4
    omega = 1.0 / (POS_BASE ** (jnp.arange(quarter, dtype=F32) / quarter))
    r = jnp.arange(rows, dtype=F32)[:, None] * omega
    col = jnp.arange(GRID_W, dtype=F32)[:, None] * omega
    row_part = jnp.concatenate([jnp.sin(r), jnp.cos(r)], axis=-1)[:, None, :]
    col_part = jnp.concatenate([jnp.sin(col), jnp.cos(col)], axis=-1)[None, :, :]
    shape = (rows, GRID_W, 2 * quarter)
    return jnp.concatenate([jnp.broadcast_to(row_part, shape), jnp.broadcast_to(col_part, shape)],
                           axis=-1).reshape(rows * GRID_W, D)


def _mlstm_layer(geo, x, mod_l, j, a_w_in, a_b_gates, a_norm, a_w_out, ln_g, ln_b,
                 state_C, state_n, state_m, w_router, e_bias):
    q, kt, v, so, gr = _proj_a(geo, x, mod_l, a_w_in[j], a_b_gates[j])
    hp, c_p, n_p, m_p = _mlstm_scan(q, kt, v, gr, row0=0, n_seq=geo.n_prompt, seq_len=geo.prompt_len)
    ns = geo.n_sample
    n0 = jnp.pad(state_n[:, j].astype(F32)[..., None], ((0, 0),) * 4 + ((0, LANES - 1),))
    m0 = jnp.pad(state_m[:, j].astype(F32), ((0, 0), (0, 0), (0, SUBLANES - NH_A)))
    m0 = jnp.broadcast_to(m0[..., None], (ns, 2, SUBLANES, LANES))
    hs, _, _, _ = _mlstm_scan(q, kt, v, gr, row0=geo.t_prompt, n_seq=ns, seq_len=geo.sample_len,
                              state=(state_C[:, j].astype(F32), n0, m0))
    x1, routing = _out_a(geo, hp, hs, so, a_norm[j], a_w_out[j], x, mod_l, ln_g, ln_b, w_router, e_bias)
    return x1, routing, c_p, n_p[..., 0], m_p[:, :, :NH_A, 0]


def _hgrn_layer(geo, x, proj, mod_l, j, lb_layer, b_norm, b_w_out, ln_g, ln_b, state_S, w_router, e_bias):
    q, pre, v, sg = proj
    lbd = lb_layer.reshape(2, NH_B, 1, DK_B)
    op, s_p = _hgrn_scan(q, pre, v, lbd, row0=0, n_seq=geo.n_prompt, seq_len=geo.prompt_len)
    os_, _ = _hgrn_scan(q, pre, v, lbd, row0=geo.t_prompt, n_seq=geo.n_sample, seq_len=geo.sample_len,
                        state=state_S[:, j].astype(F32))
    x1, routing = _out_b(geo, op, os_, sg, b_norm[j], b_w_out[j], x, mod_l, ln_g, ln_b, w_router, e_bias)
    return x1, routing, s_p


def kernel(x_prompt, x_sample, state_mlstm_C, state_mlstm_n, state_mlstm_m, state_hgrn_S, c, c_ctx, w_mod, b_mod, ln_g, ln_b, a_w_in, a_b_gates, a_norm, a_w_out, b_w_in, b_lb, b_norm, b_w_out, w_router, e_bias, w_gate, w_up, w_down, ws_gate, ws_up, ws_down):
    bp, sp, _ = x_prompt.shape
    bs, ss, _ = x_sample.shape
    cond = jnp.zeros((COND_ROWS, D), F32).at[0].set(c_ctx).at[1:1 + bs].set(c)
    mod = _modulation(cond, w_mod, b_mod)
    x = (x_prompt.reshape(-1, D), x_sample.reshape(-1, D), _pos_embed(ss // GRID_W))
    sm = jax.nn.softmax(b_lb.astype(F32), axis=0)
    lb_all = jnp.cumsum(sm, axis=0) - sm[0]
    geo = Geometry(bp, sp, bs, ss)
    x1, routing, new_c, new_n, new_m = _mlstm_layer(geo, x, mod[0], 0, a_w_in, a_b_gates, a_norm, a_w_out,
                                                    ln_g[0, 0], ln_b[0, 0], state_mlstm_C, state_mlstm_n,
                                                    state_mlstm_m, w_router[0], e_bias[0])
    x2, proj = _moe_layer(geo, x1, routing, mod[0], 0, w_gate, w_up, w_down, ws_gate[0], ws_up[0], ws_down[0],
                          ln_g[0, 1], ln_b[0, 1], next_proj_b=(mod[1], b_w_in[0]))
    x3, routing, new_s = _hgrn_layer(geo, x2, proj, mod[1], 0, lb_all[1], b_norm, b_w_out, ln_g[1, 0], ln_b[1, 0],
                                     state_hgrn_S, w_router[1], e_bias[1])
    y_p, y_s = _moe_layer(geo, x3, routing, mod[1], 1, w_gate, w_up, w_down, ws_gate[1], ws_up[1], ws_down[1],
                          ln_g[1, 1], ln_b[1, 1], split=True)
    return (y_p.reshape(bp, sp, D), y_s.reshape(bs, ss, D), new_c[:, None], new_n[:, None], new_m[:, None],
            new_s[:, None])
```

```python
import functools
import math

import jax
import jax.numpy as jnp
from jax import lax
from jax.experimental import pallas as pl
from jax.experimental.pallas import tpu as pltpu
from jax.experimental.pallas import tpu_sc as plsc

F32 = jnp.float32
BF16 = jnp.bfloat16
HIGHEST = lax.Precision.HIGHEST

D = 1024
DEPTH = 2
GRID_W = 64
POS_BASE = 10000.0
EPS = 1e-6
ALPHA = (2.0 * DEPTH) ** 0.25
NH_A, DK_A, DV_A = 4, 128, 256
QK_A, V_A = NH_A * DK_A, NH_A * DV_A
NH_B, DK_B = 8, 128
N_EXPERTS, TOP_K, N_GROUPS, TOPK_GROUPS = 64, 8, 8, 4
GROUP_SIZE = N_EXPERTS // N_GROUPS
D_EXPERT = D // 4
ROUTED_SCALE = 2.5

LANES = 128
SUBLANES = 8
COND_ROWS = 8
TOK_TILE = 256
TOK_TILE_L = 512
SLOT_TILE = 2048
CHUNK_A = 256
VMEM_BYTES_V7X = 64 * 1024 * 1024
VMEM_LIMIT = VMEM_BYTES_V7X - 8 * 1024 * 1024

NT_DIMS = (((1,), (1,)), ((), ()))


def _params(sem):
    return pltpu.CompilerParams(dimension_semantics=sem, vmem_limit_bytes=VMEM_LIMIT)


def _split3(x):
    hi = x.astype(BF16)
    r = x - hi.astype(F32)
    mid = r.astype(BF16)
    lo = (r - mid.astype(F32)).astype(BF16)
    return hi, mid, lo


def _dot3(a_bf, x):
    hi, mid, lo = _split3(x)
    return (jnp.dot(a_bf, hi, preferred_element_type=F32)
            + jnp.dot(a_bf, mid, preferred_element_type=F32)
            + jnp.dot(a_bf, lo, preferred_element_type=F32))


def _dot3_r(x, a_bf):
    hi, mid, lo = _split3(x)
    return (jnp.dot(hi, a_bf, preferred_element_type=F32)
            + jnp.dot(mid, a_bf, preferred_element_type=F32)
            + jnp.dot(lo, a_bf, preferred_element_type=F32))


def _log_sigmoid(x):
    return jnp.minimum(x, 0.0) - jnp.log1p(jnp.exp(-jnp.abs(x)))


def _layer_norm_rows(x, g, b):
    mu = jnp.mean(x, axis=-1, keepdims=True)
    xc = x - mu
    var = jnp.mean(xc * xc, axis=-1, keepdims=True)
    return xc * lax.rsqrt(var + EPS) * g + b


class Geometry:
    def __init__(self, n_prompt, prompt_len, n_sample, sample_len):
        self.n_prompt, self.prompt_len = n_prompt, prompt_len
        self.n_sample, self.sample_len = n_sample, sample_len
        self.t_prompt = n_prompt * prompt_len
        self.t_sample = n_sample * sample_len
        self.t = self.t_prompt + self.t_sample
        assert self.t_prompt % TOK_TILE_L == 0 and sample_len % TOK_TILE_L == 0
        assert n_sample + 1 <= COND_ROWS

    def cond_row(self, tile, tile_rows):
        n_p = self.t_prompt // tile_rows
        return jnp.where(tile < n_p, 0, 1 + (tile - n_p) // (self.sample_len // tile_rows))


def _mod_kernel(cond_ref, w_ref, b_ref, o_ref):
    c = cond_ref[...]
    s = c * jax.nn.sigmoid(c)
    o_ref[0, 0] = jnp.dot(s, w_ref[0], precision=HIGHEST, preferred_element_type=F32) + b_ref[0, 0]


def _modulation(cond, w_mod, b_mod):
    out = pl.pallas_call(
        _mod_kernel,
        name="modulation",
        grid=(DEPTH, 6),
        in_specs=[
            pl.BlockSpec((COND_ROWS, D), lambda l, j: (0, 0)),
            pl.BlockSpec((1, D, D), lambda l, j: (l, 0, j)),
            pl.BlockSpec((1, 1, 1, D), lambda l, j: (l, j, 0, 0)),
        ],
        out_specs=pl.BlockSpec((1, 1, COND_ROWS, D), lambda l, j: (l, j, 0, 0)),
        out_shape=jax.ShapeDtypeStruct((DEPTH, 6, COND_ROWS, D), F32),
        compiler_params=_params(("arbitrary", "arbitrary")),
    )(cond, w_mod, b_mod.reshape(DEPTH, 6, 1, D))
    return out.transpose(0, 2, 1, 3)


def _embed_specs(geo, tb):
    n_p = geo.t_prompt // tb
    per_seq = geo.sample_len // tb
    return [pl.BlockSpec((tb, D), lambda i: (jnp.minimum(i, n_p - 1), 0)),
            pl.BlockSpec((tb, D), lambda i: (jnp.maximum(i - n_p, 0), 0)),
            pl.BlockSpec((tb, D), lambda i: (jnp.maximum(i - n_p, 0) % per_seq, 0))]


def _embed_tile(xp_ref, xs_ref, pos_ref, n_prompt_tiles):
    return jnp.where(pl.program_id(0) < n_prompt_tiles, xp_ref[...], xs_ref[...] + pos_ref[...])


def _proj_a_kernel(xp_ref, xs_ref, pos_ref, mod_ref, wq_ref, wkt_ref, wvo_ref, wgt_ref, bgt_ref,
                   q_ref, kt_ref, v_ref, so_ref, gr_ref, *, n_prompt_tiles):
    x = _embed_tile(xp_ref, xs_ref, pos_ref, n_prompt_tiles)
    h = x * (1.0 + mod_ref[1:2, :]) + mod_ref[0:1, :]
    hb = h.astype(BF16)
    q_ref[...] = jnp.dot(hb, wq_ref[...], preferred_element_type=F32).astype(BF16)
    kt = lax.dot_general(wkt_ref[...], hb, NT_DIMS, preferred_element_type=F32)
    kt_ref[...] = (kt * (DK_A ** -0.5)).astype(BF16)
    vo = jnp.dot(hb, wvo_ref[...], preferred_element_type=F32)
    v_ref[...] = vo[:, :V_A].astype(BF16)
    so_ref[...] = jax.nn.sigmoid(vo[:, V_A:]).astype(BF16)
    gr_ref[...] = lax.dot_general(wgt_ref[...], h, NT_DIMS, precision=HIGHEST,
                                  preferred_element_type=F32) + bgt_ref[...]


def _proj_a(geo, x, mod_l, w_in, b_gates):
    t = geo.t
    n_gate = 4 * NH_A
    wq = w_in[:, :QK_A].astype(BF16)
    wkt = w_in[:, QK_A:2 * QK_A].T.astype(BF16)
    wvo = w_in[:, 2 * QK_A:2 * QK_A + 2 * V_A].astype(BF16)
    wg = w_in[:, 2 * QK_A + 2 * V_A:]
    bg = b_gates.reshape(n_gate).astype(F32)
    tb = TOK_TILE_L
    full = lambda shape: pl.BlockSpec(shape, lambda i: (0,) * len(shape))
    return pl.pallas_call(
        functools.partial(_proj_a_kernel, n_prompt_tiles=geo.t_prompt // tb),
        name="proj_a",
        grid=(t // tb,),
        in_specs=_embed_specs(geo, tb) + [
            pl.BlockSpec((None, 6, D), lambda i: (geo.cond_row(i, tb), 0, 0)),
            full((D, QK_A)), full((QK_A, D)), full((D, 2 * V_A)), full((n_gate, D)), full((n_gate, 1)),
        ],
        out_specs=[
            pl.BlockSpec((tb, QK_A), lambda i: (i, 0)),
            pl.BlockSpec((QK_A, tb), lambda i: (0, i)),
            pl.BlockSpec((tb, V_A), lambda i: (i, 0)),
            pl.BlockSpec((tb, V_A), lambda i: (i, 0)),
            pl.BlockSpec((n_gate, tb), lambda i: (0, i)),
        ],
        out_shape=[
            jax.ShapeDtypeStruct((t, QK_A), BF16),
            jax.ShapeDtypeStruct((QK_A, t), BF16),
            jax.ShapeDtypeStruct((t, V_A), BF16),
            jax.ShapeDtypeStruct((t, V_A), BF16),
            jax.ShapeDtypeStruct((n_gate, t), F32),
        ],
        compiler_params=_params(("parallel",)),
    )(*x, mod_l, wq, wkt, wvo, wg.T, bg.reshape(n_gate, 1))


def _mlstm_scan_kernel(*refs, chunk, has_state):
    if has_state:
        (q_ref, kt_ref, v_ref, gr_ref, c0_ref, n0_ref, m0_ref,
         h_ref, c_out, n_out, m_out, c_sc, n_sc, m_sc) = refs
    else:
        (q_ref, kt_ref, v_ref, gr_ref,
         h_ref, c_out, n_out, m_out, c_sc, n_sc, m_sc) = refs
    L = chunk
    d = pl.program_id(1)
    c = pl.program_id(2)
    fwd = d == 0

    @pl.when(c == 0)
    def _():
        if has_state:
            c_sc[...] = c0_ref[0, 0]
            n_sc[...] = n0_ref[0, 0]
            m_sc[...] = m0_ref[0, 0]
        else:
            c_sc[...] = jnp.zeros_like(c_sc)
            n_sc[...] = jnp.zeros_like(n_sc)
            m_sc[...] = jnp.zeros_like(m_sc)

    row = lax.broadcasted_iota(jnp.int32, (L, L), 0)
    col = lax.broadcasted_iota(jnp.int32, (L, L), 1)
    sgn = 1 - 2 * d
    causal = (row - col) * sgn >= 0
    tri_t = ((col - row) * sgn >= 0).astype(BF16)

    gr = gr_ref[...]
    br_all = _dot3_r(_log_sigmoid(gr), tri_t)
    bc_all = jnp.concatenate([br_all, jnp.zeros((LANES - br_all.shape[0], L), F32)], axis=0).T
    ones_blk = (lax.broadcasted_iota(jnp.int32, (L, LANES), 1) == 0).astype(BF16)

    def gate_row(direction, gate, head):
        return (direction * 2 + gate) * NH_A + head

    for h in range(NH_A):
        ff, fb = gate_row(0, 1, h), gate_row(1, 1, h)
        gi, gb = gate_row(0, 0, h), gate_row(1, 0, h)
        b_c = jnp.where(fwd, bc_all[:, ff:ff + 1], bc_all[:, fb:fb + 1])
        b_r = jnp.where(fwd, br_all[ff:ff + 1, :], br_all[fb:fb + 1, :])
        i_r = jnp.where(fwd, gr[gi:gi + 1, :], gr[gb:gb + 1, :])
        bl = jnp.where(fwd, b_r[:, L - 1:L], b_r[:, 0:1])
        q = q_ref[:, h * DK_A:(h + 1) * DK_A]
        kt = kt_ref[h * DK_A:(h + 1) * DK_A, :]
        v = v_ref[:, h * DV_A:(h + 1) * DV_A]
        m = m_sc[h:h + 1, 0:1]
        cst = c_sc[h]
        nst = n_sc[h]

        a_r = i_r - b_r
        logd = jnp.where(causal, b_c + a_r, -jnp.inf)
        inter = b_c + m
        m_t = jnp.maximum(inter, jnp.max(logd, axis=1, keepdims=True))
        dmat = jnp.exp(logd - m_t)
        e_int = jnp.exp(inter - m_t)
        s = (jnp.dot(q, kt, preferred_element_type=F32) * dmat).astype(BF16)
        num = (jnp.dot(s, v, preferred_element_type=F32)
               + e_int * jnp.dot(q, cst.astype(BF16), preferred_element_type=F32))
        den = (jnp.dot(s, ones_blk, preferred_element_type=F32)
               + e_int * jnp.dot(q, nst.astype(BF16), preferred_element_type=F32))[:, 0:1]
        h_ref[:, h * DV_A:(h + 1) * DV_A] = (num / jnp.maximum(jnp.abs(den), jnp.exp(-m_t))).astype(BF16)

        logw = bl + a_r
        m_new = jnp.maximum(bl + m, jnp.max(logw, axis=1, keepdims=True))
        w = jnp.exp(logw - m_new)
        decay = jnp.exp(bl + m - m_new)
        kw = (kt.astype(F32) * w).astype(BF16)
        c_sc[h] = decay * cst + jnp.dot(kw, v, preferred_element_type=F32)
        n_sc[h] = decay * nst + jnp.dot(kw, ones_blk, preferred_element_type=F32)
        m_sc[h:h + 1, :] = jnp.broadcast_to(m_new, (1, LANES))

    @pl.when(c == pl.num_programs(2) - 1)
    def _():
        c_out[0, 0] = c_sc[...]
        n_out[0, 0] = n_sc[...]
        m_out[0, 0] = m_sc[...]


def _mlstm_scan(q, kt, v, gr, *, row0, n_seq, seq_len, state=None):
    L = CHUNK_A
    nc = seq_len // L
    blk0 = row0 // L

    def loc_blk(b, d, c):
        return b * nc + c + d * (nc - 1 - 2 * c)

    def tok_blk(b, d, c):
        return blk0 + loc_blk(b, d, c)

    in_specs = [
        pl.BlockSpec((L, QK_A), lambda b, d, c: (tok_blk(b, d, c), 0)),
        pl.BlockSpec((QK_A, L), lambda b, d, c: (0, tok_blk(b, d, c))),
        pl.BlockSpec((L, V_A), lambda b, d, c: (tok_blk(b, d, c), 0)),
        pl.BlockSpec((4 * NH_A, L), lambda b, d, c: (0, tok_blk(b, d, c))),
    ]
    args = [q, kt, v, gr]
    if state is not None:
        in_specs += [
            pl.BlockSpec((1, 1, NH_A, DK_A, DV_A), lambda b, d, c: (b, d, 0, 0, 0)),
            pl.BlockSpec((1, 1, NH_A, DK_A, LANES), lambda b, d, c: (b, d, 0, 0, 0)),
            pl.BlockSpec((1, 1, SUBLANES, LANES), lambda b, d, c: (b, d, 0, 0)),
        ]
        args += list(state)
    return pl.pallas_call(
        functools.partial(_mlstm_scan_kernel, chunk=L, has_state=state is not None),
        name="mlstm_scan_seeded" if state is not None else "mlstm_scan",
        grid=(n_seq, 2, nc),
        in_specs=in_specs,
        out_specs=[
            pl.BlockSpec((None, L, V_A), lambda b, d, c: (d, loc_blk(b, d, c), 0)),
            pl.BlockSpec((1, 1, NH_A, DK_A, DV_A), lambda b, d, c: (b, d, 0, 0, 0)),
            pl.BlockSpec((1, 1, NH_A, DK_A, LANES), lambda b, d, c: (b, d, 0, 0, 0)),
            pl.BlockSpec((1, 1, SUBLANES, LANES), lambda b, d, c: (b, d, 0, 0)),
        ],
        out_shape=[
            jax.ShapeDtypeStruct((2, n_seq * seq_len, V_A), BF16),
            jax.ShapeDtypeStruct((n_seq, 2, NH_A, DK_A, DV_A), F32),
            jax.ShapeDtypeStruct((n_seq, 2, NH_A, DK_A, LANES), F32),
            jax.ShapeDtypeStruct((n_seq, 2, SUBLANES, LANES), F32),
        ],
        scratch_shapes=[
            pltpu.VMEM((NH_A, DK_A, DV_A), F32),
            pltpu.VMEM((NH_A, DK_A, LANES), F32),
            pltpu.VMEM((SUBLANES, LANES), F32),
        ],
        compiler_params=_params(("parallel", "parallel", "arbitrary")),
    )(*args)


def _out_a_kernel(hp_ref, hs_ref, so_ref, nw_ref, w_ref, xp_ref, xs_ref, pos_ref, mod_ref, lg_ref, lb_ref,
                  wrt_ref, eb_ref, o_ref, *route_refs, n_prompt_tiles):
    is_prompt = pl.program_id(0) < n_prompt_tiles
    x = _embed_tile(xp_ref, xs_ref, pos_ref, n_prompt_tiles)
    y = jnp.where(is_prompt, hp_ref[0].astype(F32) + hp_ref[1].astype(F32),
                  hs_ref[0].astype(F32) + hs_ref[1].astype(F32))
    parts = []
    for h in range(NH_A):
        yh = y[:, h * DV_A:(h + 1) * DV_A]
        mu = jnp.mean(yh, axis=-1, keepdims=True)
        yc = yh - mu
        var = jnp.mean(yc * yc, axis=-1, keepdims=True)
        parts.append(yc * lax.rsqrt(var + EPS))
    yn = jnp.concatenate(parts, axis=-1) * nw_ref[...] * so_ref[...].astype(F32)
    out = jnp.dot(yn.astype(BF16), w_ref[...], preferred_element_type=F32)
    x1 = _layer_norm_rows(ALPHA * x + mod_ref[2:3, :] * out, lg_ref[...], lb_ref[...])
    o_ref[...] = x1
    _route_tile(x1, mod_ref, wrt_ref, eb_ref, *route_refs)


def _out_a(geo, h_prompt, h_sample, so, norm_w, w_out, x, mod_l, ln_g, ln_b, w_router, e_bias):
    t = geo.t
    tb = TOK_TILE_L
    n_p = geo.t_prompt // tb
    full = lambda shape: pl.BlockSpec(shape, lambda i: (0,) * len(shape))
    rio = _RouterIO(t, tb, w_router, e_bias)
    outs = pl.pallas_call(
        functools.partial(_out_a_kernel, n_prompt_tiles=n_p),
        name="out_a",
        grid=(t // tb,),
        in_specs=[
            pl.BlockSpec((2, tb, V_A), lambda i: (0, jnp.minimum(i, n_p - 1), 0)),
            pl.BlockSpec((2, tb, V_A), lambda i: (0, jnp.maximum(i - n_p, 0), 0)),
            pl.BlockSpec((tb, V_A), lambda i: (i, 0)),
            full((1, V_A)), full((V_A, D)),
        ] + _embed_specs(geo, tb) + [
            pl.BlockSpec((None, 6, D), lambda i: (geo.cond_row(i, tb), 0, 0)),
            full((1, D)), full((1, D)),
        ] + rio.in_specs,
        out_specs=[pl.BlockSpec((tb, D), lambda i: (i, 0))] + rio.out_specs,
        out_shape=[jax.ShapeDtypeStruct((t, D), F32)] + rio.out_shape,
        scratch_shapes=rio.scratch,
        compiler_params=_params(("arbitrary",)),
    )(h_prompt, h_sample, so, norm_w.reshape(1, V_A).astype(F32), w_out.astype(BF16), *x, mod_l,
      ln_g.reshape(1, D), ln_b.reshape(1, D), *rio.inputs)
    return outs[0], _RouterIO.unpack(outs[1:])


def _proj_b_kernel(x_ref, mod_ref, w_ref, q_ref, pre_ref, v_ref, sg_ref):
    h = x_ref[...] * (1.0 + mod_ref[1:2, :]) + mod_ref[0:1, :]
    z = jnp.dot(h.astype(BF16), w_ref[...], preferred_element_type=F32)
    for hd in range(NH_B):
        lo = hd * DK_B
        qh = z[:, lo:lo + DK_B]
        q_ref[hd] = qh * jax.nn.sigmoid(qh)
        pre_ref[0, hd] = z[:, D + lo:D + lo + DK_B]
        pre_ref[1, hd] = z[:, 2 * D + lo:2 * D + lo + DK_B]
        v_ref[hd] = z[:, 3 * D + lo:3 * D + lo + DK_B].astype(BF16)
    g = z[:, 4 * D:]
    sg_ref[...] = (g * jax.nn.sigmoid(g)).astype(BF16)


def _proj_b(geo, x, mod_l, w_in):
    t = geo.t
    tb = TOK_TILE
    return pl.pallas_call(
        _proj_b_kernel,
        name="proj_b",
        grid=(t // tb,),
        in_specs=[
            pl.BlockSpec((tb, D), lambda i: (i, 0)),
            pl.BlockSpec((None, 6, D), lambda i: (geo.cond_row(i, tb), 0, 0)),
            pl.BlockSpec((D, 5 * D), lambda i: (0, 0)),
        ],
        out_specs=[
            pl.BlockSpec((NH_B, tb, DK_B), lambda i: (0, i, 0)),
            pl.BlockSpec((2, NH_B, tb, DK_B), lambda i: (0, 0, i, 0)),
            pl.BlockSpec((NH_B, tb, DK_B), lambda i: (0, i, 0)),
            pl.BlockSpec((tb, D), lambda i: (i, 0)),
        ],
        out_shape=[
            jax.ShapeDtypeStruct((NH_B, t, DK_B), F32),
            jax.ShapeDtypeStruct((2, NH_B, t, DK_B), F32),
            jax.ShapeDtypeStruct((NH_B, t, DK_B), BF16),
            jax.ShapeDtypeStruct((t, D), BF16),
        ],
        compiler_params=_params(("parallel",)),
    )(x, mod_l, w_in.astype(BF16))


CHUNK_B = 256
BAND = SUBLANES // 2
TN_DIMS = (((0,), (0,)), ((), ()))


def _hgrn_head(q, pre, lbv, v_bf, st, fwd):
    L = q.shape[0]
    sg = jax.nn.sigmoid(pre)
    f = lbv + (1.0 - lbv) * sg
    lf = jnp.log(f)
    kk = (1.0 - lbv) * (1.0 - sg)
    row = lax.broadcasted_iota(jnp.int32, (L, L), 0)
    col = lax.broadcasted_iota(jnp.int32, (L, L), 1)
    tri = ((row >= col) if fwd else (row <= col)).astype(BF16)
    b = _dot3(tri, lf)
    tpos = lax.broadcasted_iota(jnp.int32, (L, DK_B), 0)
    blk_bits = row ^ col
    lag = jnp.where(blk_bits < BAND, (row - col) if fwd else (col - row), -1)

    step = 1 if fwd else L - 1
    att = jnp.where(lag == 0, jnp.sum(q * kk, axis=1, keepdims=True), 0.0)
    f_r, kk_r, g = f, kk, f
    for dl in range(1, BAND):
        if dl > 1:
            f_r = pltpu.roll(f_r, step, 0)
            g = g * f_r
        kk_r = pltpu.roll(kk_r, step, 0)
        att = jnp.where(lag == dl, jnp.sum(q * kk_r * g, axis=1, keepdims=True), att)

    w = BAND
    while w < L:
        nb = L // (2 * w)
        b3 = b.reshape(nb, 2 * w, DK_B)
        edge = (b3[:, w - 1:w, :] if fwd else b3[:, w:w + 1, :])
        bmid = jnp.broadcast_to(edge, (nb, 2 * w, DK_B)).reshape(L, DK_B)
        second = (tpos & w) != 0
        t_side = second if fwd else jnp.logical_not(second)
        e = jnp.exp(jnp.where(t_side, b - bmid, bmid - b))
        qt = jnp.where(t_side, q * e, 0.0).astype(BF16)
        ks = jnp.where(t_side, 0.0, kk * e).astype(BF16)
        a = lax.dot_general(qt, ks, NT_DIMS, preferred_element_type=F32)
        att = att + jnp.where(blk_bits < 2 * w, a, 0.0)
        w *= 2
    o = jnp.dot(att.astype(BF16), v_bf, preferred_element_type=F32)

    bl = b[L - 1:L, :] if fwd else b[0:1, :]
    o = o + lax.dot_general((q * jnp.exp(b)).astype(BF16), st.astype(BF16), NT_DIMS, preferred_element_type=F32)
    kd = (kk * jnp.exp(bl - b)).astype(BF16)
    st_new = jnp.exp(bl) * st + lax.dot_general(v_bf, kd, TN_DIMS, preferred_element_type=F32)
    return o, st_new


def _hgrn_scan_kernel(*refs, has_state):
    if has_state:
        q_ref, pre_ref, v_ref, lb_ref, s0_ref, o_ref, s_out, st_sc = refs
    else:
        q_ref, pre_ref, v_ref, lb_ref, o_ref, s_out, st_sc = refs
    d = pl.program_id(1)
    c = pl.program_id(2)

    @pl.when(c == 0)
    def _():
        if has_state:
            for hd in range(NH_B):
                st_sc[hd] = s0_ref[0, 0, hd].T
        else:
            st_sc[...] = jnp.zeros_like(st_sc)

    def run(fwd):
        def head(hd, carry):
            o, st_new = _hgrn_head(q_ref[hd], pre_ref[hd], lb_ref[hd], v_ref[hd], st_sc[hd], fwd)
            o_ref[hd] = o.astype(BF16)
            st_sc[hd] = st_new
            return carry
        lax.fori_loop(0, NH_B, head, 0, unroll=8)

    @pl.when(d == 0)
    def _():
        run(True)

    @pl.when(d == 1)
    def _():
        run(False)

    @pl.when(c == pl.num_programs(2) - 1)
    def _():
        for hd in range(NH_B):
            s_out[0, 0, hd] = st_sc[hd].T


def _hgrn_scan(q, pre, v, lbd, *, row0, n_seq, seq_len, state=None):
    L = CHUNK_B
    nc = seq_len // L
    blk0 = row0 // L

    def loc_blk(b, d, c):
        return b * nc + c + d * (nc - 1 - 2 * c)

    def tok_blk(b, d, c):
        return blk0 + loc_blk(b, d, c)

    in_specs = [
        pl.BlockSpec((NH_B, L, DK_B), lambda b, d, c: (0, tok_blk(b, d, c), 0)),
        pl.BlockSpec((None, NH_B, L, DK_B), lambda b, d, c: (d, 0, tok_blk(b, d, c), 0)),
        pl.BlockSpec((NH_B, L, DK_B), lambda b, d, c: (0, tok_blk(b, d, c), 0)),
        pl.BlockSpec((None, NH_B, 1, DK_B), lambda b, d, c: (d, 0, 0, 0)),
    ]
    args = [q, pre, v, lbd]
    if state is not None:
        in_specs.append(pl.BlockSpec((1, 1, NH_B, DK_B, DK_B), lambda b, d, c: (b, d, 0, 0, 0)))
        args.append(state)
    return pl.pallas_call(
        functools.partial(_hgrn_scan_kernel, has_state=state is not None),
        name="hgrn_scan_seeded" if state is not None else "hgrn_scan",
        grid=(n_seq, 2, nc),
        in_specs=in_specs,
        out_specs=[
            pl.BlockSpec((None, NH_B, L, DK_B), lambda b, d, c: (d, 0, loc_blk(b, d, c), 0)),
            pl.BlockSpec((1, 1, NH_B, DK_B, DK_B), lambda b, d, c: (b, d, 0, 0, 0)),
        ],
        out_shape=[
            jax.ShapeDtypeStruct((2, NH_B, n_seq * seq_len, DK_B), BF16),
            jax.ShapeDtypeStruct((n_seq, 2, NH_B, DK_B, DK_B), F32),
        ],
        scratch_shapes=[pltpu.VMEM((NH_B, DK_B, DK_B), F32)],
        compiler_params=_params(("parallel", "parallel", "arbitrary")),
    )(*args)


def _out_b_kernel(op_ref, os_ref, sg_ref, nw_ref, w_ref, x_ref, mod_ref, lg_ref, lb_ref, wrt_ref, eb_ref,
                  out_ref, *route_refs, n_prompt_tiles):
    is_prompt = pl.program_id(0) < n_prompt_tiles
    parts = []
    for hd in range(NH_B):
        y = jnp.where(is_prompt, op_ref[0, hd].astype(F32) + op_ref[1, hd].astype(F32),
                      os_ref[0, hd].astype(F32) + os_ref[1, hd].astype(F32))
        parts.append(y * lax.rsqrt(jnp.mean(y * y, axis=-1, keepdims=True) + EPS))
    yn = jnp.concatenate(parts, axis=-1) * nw_ref[...] * sg_ref[...].astype(F32)
    out = jnp.dot(yn.astype(BF16), w_ref[...], preferred_element_type=F32)
    x1 = _layer_norm_rows(ALPHA * x_ref[...] + mod_ref[2:3, :] * out, lg_ref[...], lb_ref[...])
    out_ref[...] = x1
    _route_tile(x1, mod_ref, wrt_ref, eb_ref, *route_refs)


def _out_b(geo, o_prompt, o_sample, sg, norm_w, w_out, x, mod_l, ln_g, ln_b, w_router, e_bias):
    t = geo.t
    tb = TOK_TILE_L
    n_p = geo.t_prompt // tb
    full = lambda shape: pl.BlockSpec(shape, lambda i: (0,) * len(shape))
    rio = _RouterIO(t, tb, w_router, e_bias)
    outs = pl.pallas_call(
        functools.partial(_out_b_kernel, n_prompt_tiles=n_p),
        name="out_b",
        grid=(t // tb,),
        in_specs=[
            pl.BlockSpec((2, NH_B, tb, DK_B), lambda i: (0, 0, jnp.minimum(i, n_p - 1), 0)),
            pl.BlockSpec((2, NH_B, tb, DK_B), lambda i: (0, 0, jnp.maximum(i - n_p, 0), 0)),
            pl.BlockSpec((tb, D), lambda i: (i, 0)),
            full((1, D)), full((D, D)),
            pl.BlockSpec((tb, D), lambda i: (i, 0)),
            pl.BlockSpec((None, 6, D), lambda i: (geo.cond_row(i, tb), 0, 0)),
            full((1, D)), full((1, D)),
        ] + rio.in_specs,
        out_specs=[pl.BlockSpec((tb, D), lambda i: (i, 0))] + rio.out_specs,
        out_shape=[jax.ShapeDtypeStruct((t, D), F32)] + rio.out_shape,
        scratch_shapes=rio.scratch,
        compiler_params=_params(("arbitrary",)),
    )(o_prompt, o_sample, sg, norm_w.reshape(1, D).astype(F32), w_out.astype(BF16), x, mod_l,
      ln_g.reshape(1, D), ln_b.reshape(1, D), *rio.inputs)
    return outs[0], _RouterIO.unpack(outs[1:])


MOE_BLK = 1024
U32 = jnp.uint32
ROW_WORDS = D // 2
CHUNK_W = 256
ROW_CHUNKS = ROW_WORDS // CHUNK_W
SC_WINDOW = 128


def _pack_rows(x):
    return pltpu.pack_elementwise([x[:, :ROW_WORDS], x[:, ROW_WORDS:]], packed_dtype=BF16)


def _unpack_rows(words):
    return jnp.concatenate([pltpu.unpack_elementwise(words, index=i, packed_dtype=BF16, unpacked_dtype=F32)
                            for i in range(2)], axis=1)


def _store_chunks(chunk_ref, x):
    words = _pack_rows(x)
    for c in range(ROW_CHUNKS):
        chunk_ref(c)[...] = words[:, c * CHUNK_W:(c + 1) * CHUNK_W]


def _load_chunks(chunk_ref, valid_rows=None):
    words = jnp.concatenate([chunk_ref(c)[...] for c in range(ROW_CHUNKS)], axis=1)
    if valid_rows is not None:
        row = lax.broadcasted_iota(jnp.int32, (words.shape[0], 1), 0)
        words = jnp.where(row < valid_rows, words, jnp.uint32(0))
    return _unpack_rows(words)


def _first_index(hit, iota, size, axis):
    return jnp.min(jnp.where(hit, iota, size), axis=axis, keepdims=True)


def _route_tile(x, mod_ref, wrt_ref, eb_ref, e_ref, w_ref, r_ref, cnt_ref, h_ref, cnt_sc):
    i = pl.program_id(0)
    tb = x.shape[0]

    @pl.when(i == 0)
    def _():
        cnt_sc[...] = jnp.zeros_like(cnt_sc)

    h = x * (1.0 + mod_ref[4:5, :]) + mod_ref[3:4, :]
    _store_chunks(lambda c: h_ref.at[c], h)
    logits = lax.dot_general(wrt_ref[...], h, NT_DIMS, precision=HIGHEST, preferred_element_type=F32)
    scores = jax.nn.sigmoid(logits)
    sel = scores + eb_ref[...]

    g3 = sel.reshape(N_GROUPS, GROUP_SIZE, tb)
    io3 = lax.broadcasted_iota(jnp.int32, g3.shape, 1)
    m1 = jnp.max(g3, axis=1, keepdims=True)
    first = _first_index(g3 == m1, io3, GROUP_SIZE, 1)
    m2 = jnp.max(jnp.where(io3 == first, -jnp.inf, g3), axis=1, keepdims=True)
    gscore = (m1 + m2).reshape(N_GROUPS, tb)

    iog = lax.broadcasted_iota(jnp.int32, gscore.shape, 0)
    gmask = jnp.zeros(gscore.shape, F32)
    for _ in range(TOPK_GROUPS):
        gm = jnp.max(gscore, axis=0, keepdims=True)
        pick = iog == _first_index(gscore == gm, iog, N_GROUPS, 0)
        gmask = jnp.where(pick, 1.0, gmask)
        gscore = jnp.where(pick, -jnp.inf, gscore)
    emask = jnp.broadcast_to(gmask.reshape(N_GROUPS, 1, tb), (N_GROUPS, GROUP_SIZE, tb)).reshape(N_EXPERTS, tb)
    cand = jnp.where(emask > 0.0, sel, -jnp.inf)

    ioe = lax.broadcasted_iota(jnp.int32, cand.shape, 0)
    picks, wts = [], []
    onehot = jnp.zeros(cand.shape, F32)
    for _ in range(TOP_K):
        cm = jnp.max(cand, axis=0, keepdims=True)
        idx = _first_index(cand == cm, ioe, N_EXPERTS, 0)
        pick = ioe == idx
        picks.append(pick)
        wts.append(jnp.sum(jnp.where(pick, scores, 0.0), axis=0, keepdims=True))
        onehot = onehot + pick.astype(F32)
        cand = jnp.where(pick, -jnp.inf, cand)
        e_ref[pl.ds(len(picks) - 1, 1), :] = idx
    wsum = wts[0]
    for wk in wts[1:]:
        wsum = wsum + wk
    for k in range(TOP_K):
        w_ref[pl.ds(k, 1), :] = wts[k] / wsum * ROUTED_SCALE

    r_io = lax.broadcasted_iota(jnp.int32, (tb, tb), 0)
    c_io = lax.broadcasted_iota(jnp.int32, (tb, tb), 1)
    before = (r_io < c_io).astype(BF16)
    rank = cnt_sc[:, 0:1] + jnp.dot(onehot.astype(BF16), before, preferred_element_type=F32)
    for k in range(TOP_K):
        r_ref[pl.ds(k, 1), :] = jnp.sum(jnp.where(picks[k], rank, 0.0), axis=0, keepdims=True).astype(jnp.int32)
    cnt_sc[...] = cnt_sc[...] + jnp.sum(onehot, axis=1, keepdims=True)
    cnt_ref[...] = cnt_sc[...]


class _RouterIO:
    def __init__(self, t, tb, w_router, e_bias):
        full = lambda shape: pl.BlockSpec(shape, lambda i: (0,) * len(shape))
        self.inputs = [w_router.T.astype(F32), e_bias.reshape(N_EXPERTS, 1).astype(F32)]
        self.in_specs = [full((N_EXPERTS, D)), full((N_EXPERTS, 1))]
        self.out_specs = [
            pl.BlockSpec((TOP_K, tb), lambda i: (0, i)),
            pl.BlockSpec((TOP_K, tb), lambda i: (0, i)),
            pl.BlockSpec((TOP_K, tb), lambda i: (0, i)),
            full((N_EXPERTS, LANES)),
            pl.BlockSpec((ROW_CHUNKS, tb, CHUNK_W), lambda i: (0, i, 0)),
        ]
        self.out_shape = [
            jax.ShapeDtypeStruct((TOP_K, t), jnp.int32),
            jax.ShapeDtypeStruct((TOP_K, t), F32),
            jax.ShapeDtypeStruct((TOP_K, t), jnp.int32),
            jax.ShapeDtypeStruct((N_EXPERTS, LANES), F32),
            jax.ShapeDtypeStruct((ROW_CHUNKS, t, CHUNK_W), U32),
        ]
        self.scratch = [pltpu.VMEM((N_EXPERTS, LANES), F32)]

    @staticmethod
    def unpack(outs):
        e, w, r, cnt, h = outs
        return e, w, r, cnt[:, 0].astype(jnp.int32), h


def _slot_kernel(pstart_ref, e_ref, r_ref, o_ref):
    e = e_ref[...]
    slot = r_ref[...]
    for x in range(N_EXPERTS):
        slot = slot + jnp.where(e == x, pstart_ref[x], 0)
    o_ref[...] = slot


def _slots(geo, pstart, top_e, rank):
    tb = math.gcd(SLOT_TILE, geo.t)
    return pl.pallas_call(
        _slot_kernel,
        name="slots",
        grid_spec=pltpu.PrefetchScalarGridSpec(
            num_scalar_prefetch=1,
            grid=(geo.t // tb,),
            in_specs=[pl.BlockSpec((TOP_K, tb), lambda i, p: (0, i)),
                      pl.BlockSpec((TOP_K, tb), lambda i, p: (0, i))],
            out_specs=pl.BlockSpec((TOP_K, tb), lambda i, p: (0, i)),
        ),
        out_shape=jax.ShapeDtypeStruct((TOP_K, geo.t), jnp.int32),
        compiler_params=_params(("parallel",)),
    )(pstart, top_e, rank)


def _block_meta_kernel(pstart_ref, counts_ref, pend_ref, e_ref, v_ref):
    row0 = lax.broadcasted_iota(jnp.int32, e_ref.shape, 1) * MOE_BLK
    blk_e = jnp.zeros(e_ref.shape, jnp.int32)
    for x in range(N_EXPERTS):
        blk_e = blk_e + jnp.where(pend_ref[x] <= row0, 1, 0)
    blk_e = jnp.minimum(blk_e, N_EXPERTS - 1)
    last = jnp.zeros(e_ref.shape, jnp.int32)
    for x in range(N_EXPERTS):
        last = last + jnp.where(blk_e == x, pstart_ref[x] + counts_ref[x], 0)
    e_ref[...] = blk_e
    v_ref[...] = jnp.clip(last - row0, 0, MOE_BLK)


def _block_meta(pstart, counts, pend, n_blocks):
    e, v = pl.pallas_call(
        _block_meta_kernel,
        name="block_meta",
        grid_spec=pltpu.PrefetchScalarGridSpec(
            num_scalar_prefetch=3,
            grid=(1,),
            in_specs=[],
            out_specs=[pl.BlockSpec((1, n_blocks), lambda i, a, b, c: (0, 0)),
                       pl.BlockSpec((1, n_blocks), lambda i, a, b, c: (0, 0))],
        ),
        out_shape=[jax.ShapeDtypeStruct((1, n_blocks), jnp.int32), jax.ShapeDtypeStruct((1, n_blocks), jnp.int32)],
        compiler_params=_params(("arbitrary",)),
    )(pstart, counts, pend)
    return e[0], v[0]


def _sc_mesh():
    return plsc.VectorSubcoreMesh(core_axis_name="core", subcore_axis_name="subcore")


def _sc_scatter(rows, idx, n_out, copies):
    n_src = rows.shape[0]
    n_idx = idx.shape[0]
    groups = ROW_CHUNKS
    win_per_group = n_src // groups // SC_WINDOW

    def idx_block(w, k):
        return (0, ((w // win_per_group) * copies + k) * win_per_group + w % win_per_group)

    @pl.kernel(out_type=jax.ShapeDtypeStruct((n_out, CHUNK_W), rows.dtype), mesh=_sc_mesh(), scratch_types=[],
               name="sc_dispatch")
    def scatter(x_hbm, i_hbm, o_hbm):
        def body(x_vmem, *i_vmems):
            for i_vmem in i_vmems:
                pltpu.sync_copy(x_vmem, o_hbm.at[i_vmem.at[0]])

        pltpu.emit_pipeline(
            body,
            grid=(n_src // SC_WINDOW,),
            in_specs=[pl.BlockSpec((SC_WINDOW, CHUNK_W), index_map=lambda w: (w, 0))]
            + [pl.BlockSpec((1, SC_WINDOW), index_map=functools.partial(idx_block, k=k)) for k in range(copies)],
            out_specs=[],
            core_axis_name=("core", "subcore"),
            dimension_semantics=(pltpu.PARALLEL,),
        )(x_hbm, *([i_hbm] * copies))

    return scatter(rows, idx.reshape(1, n_idx))


def _sc_gather(table, idx):
    n_idx = idx.shape[0]

    @pl.kernel(out_type=jax.ShapeDtypeStruct((n_idx, CHUNK_W), table.dtype), mesh=_sc_mesh(),
               name="sc_combine_gather")
    def gather(t_hbm, i_hbm, o_hbm):
        def body(i_vmem, o_vmem):
            pltpu.sync_copy(t_hbm.at[i_vmem.at[0]], o_vmem)

        pltpu.emit_pipeline(
            body,
            grid=(n_idx // SC_WINDOW,),
            in_specs=[pl.BlockSpec((1, SC_WINDOW), index_map=lambda w: (0, w))],
            out_specs=[pl.BlockSpec((SC_WINDOW, CHUNK_W), index_map=lambda w: (w, 0))],
            core_axis_name=("core", "subcore"),
            dimension_semantics=(pltpu.PARALLEL,),
        )(i_hbm, o_hbm)

    return gather(table, idx.reshape(1, n_idx))


def _ffn_kernel(blk_e_ref, blk_valid_ref, n_used_ref, xs_ref, wg_ref, wu_ref, wd_ref, y_ref, wg_sc, wu_sc, wd_sc):
    b = pl.program_id(0)
    used = b < n_used_ref[0]
    new_expert = (b == 0) | (blk_e_ref[b] != blk_e_ref[jnp.maximum(b - 1, 0)])

    @pl.when(used & new_expert)
    def _():
        wg_sc[...] = wg_ref[...].astype(BF16)
        wu_sc[...] = wu_ref[...].astype(BF16)
        wd_sc[...] = wd_ref[...].astype(BF16)

    @pl.when(used)
    def _():
        x = _load_chunks(lambda c: xs_ref.at[c], valid_rows=blk_valid_ref[b]).astype(BF16)
        g = jnp.dot(x, wg_sc[...], preferred_element_type=F32)
        u = jnp.dot(x, wu_sc[...], preferred_element_type=F32)
        hmid = (g * jax.nn.sigmoid(g) * u).astype(BF16)
        _store_chunks(lambda c: y_ref.at[c], jnp.dot(hmid, wd_sc[...], preferred_element_type=F32))

    @pl.when(jnp.logical_not(used))
    def _():
        y_ref[...] = jnp.zeros_like(y_ref)


def _ffn(xs, blk_e, blk_valid, n_used, layer, wg, wu, wd, n_blocks):
    def blk(b, be, bv, nu):
        return jnp.maximum(jnp.minimum(b, nu[0] - 1), 0)

    def w_idx(b, be, bv, nu):
        return (layer, be[blk(b, be, bv, nu)], 0, 0)

    return pl.pallas_call(
        _ffn_kernel,
        name="expert_ffn",
        grid_spec=pltpu.PrefetchScalarGridSpec(
            num_scalar_prefetch=3,
            grid=(n_blocks,),
            in_specs=[
                pl.BlockSpec((ROW_CHUNKS, MOE_BLK, CHUNK_W), lambda b, be, bv, nu: (0, blk(b, be, bv, nu), 0)),
                pl.BlockSpec((None, None, D, D_EXPERT), w_idx),
                pl.BlockSpec((None, None, D, D_EXPERT), w_idx),
                pl.BlockSpec((None, None, D_EXPERT, D), w_idx),
            ],
            out_specs=pl.BlockSpec((ROW_CHUNKS, MOE_BLK, CHUNK_W), lambda b, be, bv, nu: (0, b, 0)),
            scratch_shapes=[pltpu.VMEM((D, D_EXPERT), BF16), pltpu.VMEM((D, D_EXPERT), BF16),
                            pltpu.VMEM((D_EXPERT, D), BF16)],
        ),
        out_shape=jax.ShapeDtypeStruct(xs.shape, U32),
        compiler_params=_params(("arbitrary",)),
    )(blk_e, blk_valid, n_used, xs, wg, wu, wd)


def _combine_kernel(x_ref, mod_ref, wt_ref, y_ref, sg_ref, su_ref, sd_ref, lg_ref, lb_ref, *o_refs, n_prompt_tiles):
    x = x_ref[...]
    hb = (x * (1.0 + mod_ref[4:5, :]) + mod_ref[3:4, :]).astype(BF16)
    g = jnp.dot(hb, sg_ref[...], preferred_element_type=F32)
    u = jnp.dot(hb, su_ref[...], preferred_element_type=F32)
    ff = jnp.dot((g * jax.nn.sigmoid(g) * u).astype(BF16), sd_ref[...], preferred_element_type=F32)
    for k in range(TOP_K):
        ff = ff + _load_chunks(lambda c: y_ref.at[c, k]) * wt_ref[:, k:k + 1]
    out = _layer_norm_rows(ALPHA * x + mod_ref[5:6, :] * ff, lg_ref[...], lb_ref[...])
    if len(o_refs) == 1:
        o_refs[0][...] = out
    else:
        is_prompt = pl.program_id(0) < n_prompt_tiles

        @pl.when(is_prompt)
        def _():
            o_refs[0][...] = out

        @pl.when(jnp.logical_not(is_prompt))
        def _():
            o_refs[1][...] = out


def _combine(geo, x, mod_l, wt, ytok, sg, su, sd, ln_g, ln_b, split=False):
    tb = TOK_TILE_L
    n_p = geo.t_prompt // tb
    full = lambda shape: pl.BlockSpec(shape, lambda i: (0,) * len(shape))
    if split:
        out_specs = [pl.BlockSpec((tb, D), lambda i: (jnp.minimum(i, n_p - 1), 0)),
                     pl.BlockSpec((tb, D), lambda i: (jnp.maximum(i - n_p, 0), 0))]
        out_shape = [jax.ShapeDtypeStruct((geo.t_prompt, D), F32), jax.ShapeDtypeStruct((geo.t_sample, D), F32)]
    else:
        out_specs = pl.BlockSpec((tb, D), lambda i: (i, 0))
        out_shape = jax.ShapeDtypeStruct((geo.t, D), F32)
    return pl.pallas_call(
        functools.partial(_combine_kernel, n_prompt_tiles=n_p),
        name="combine",
        grid=(geo.t // tb,),
        in_specs=[
            pl.BlockSpec((tb, D), lambda i: (i, 0)),
            pl.BlockSpec((None, 6, D), lambda i: (geo.cond_row(i, tb), 0, 0)),
            pl.BlockSpec((tb, TOP_K), lambda i: (i, 0)),
            pl.BlockSpec((ROW_CHUNKS, TOP_K, tb, CHUNK_W), lambda i: (0, 0, i, 0)),
            full((D, D_EXPERT)), full((D, D_EXPERT)), full((D_EXPERT, D)), full((1, D)), full((1, D)),
        ],
        out_specs=out_specs,
        out_shape=out_shape,
        compiler_params=_params(("arbitrary",)),
    )(x, mod_l, wt, ytok, sg.astype(BF16), su.astype(BF16), sd.astype(BF16),
      ln_g.reshape(1, D), ln_b.reshape(1, D))


def _moe_layer(geo, x, routing, mod_l, layer, wg, wu, wd, sg, su, sd, ln_g, ln_b, split=False):
    t = geo.t
    top_e, w, rank, counts, h = routing
    n_blocks = (t * TOP_K) // MOE_BLK + N_EXPERTS
    n_rows = n_blocks * MOE_BLK
    padded = (counts + MOE_BLK - 1) // MOE_BLK * MOE_BLK
    pend = jnp.cumsum(padded)
    pstart = (pend - padded).astype(jnp.int32)
    blk_e, blk_valid = _block_meta(pstart, counts, pend.astype(jnp.int32), n_blocks)
    n_used = (pend[-1:] // MOE_BLK).astype(jnp.int32)
    slots = _slots(geo, pstart, top_e, rank)
    idx = (slots.reshape(1, TOP_K * t) + (jnp.arange(ROW_CHUNKS, dtype=jnp.int32) * n_rows)[:, None]).reshape(-1)
    xs = _sc_scatter(h.reshape(ROW_CHUNKS * t, CHUNK_W), idx, ROW_CHUNKS * n_rows, TOP_K)
    yb = _ffn(xs.reshape(ROW_CHUNKS, n_rows, CHUNK_W), blk_e, blk_valid, n_used, layer, wg, wu, wd, n_blocks)
    ytok = _sc_gather(yb.reshape(ROW_CHUNKS * n_rows, CHUNK_W), idx)
    return _combine(geo, x, mod_l, w.T, ytok.reshape(ROW_CHUNKS, TOP_K, t, CHUNK_W), sg, su, sd, ln_g, ln_b,
                    split=split)


def _pos_embed(rows):
    quarter = D // 4
    omega = 1.0 / (POS_BASE ** (jnp.arange(quarter, dtype=F32) / quarter))
    r = jnp.arange(rows, dtype=F32)[:, None] * omega
    col = jnp.arange(GRID_W, dtype=F32)[:, None] * omega
    row_part = jnp.concatenate([jnp.sin(r), jnp.cos(r)], axis=-1)[:, None, :]
    col_part = jnp.concatenate([jnp.sin(col), jnp.cos(col)], axis=-1)[None, :, :]
    shape = (rows, GRID_W, 2 * quarter)
    return jnp.concatenate([jnp.broadcast_to(row_part, shape), jnp.broadcast_to(col_part, shape)],
                           axis=-1).reshape(rows * GRID_W, D)


def _mlstm_layer(geo, x, mod_l, j, a_w_in, a_b_gates, a_norm, a_w_out, ln_g, ln_b,
                 state_C, state_n, state_m, w_router, e_bias):
    q, kt, v, so, gr = _proj_a(geo, x, mod_l, a_w_in[j], a_b_gates[j])
    hp, c_p, n_p, m_p = _mlstm_scan(q, kt, v, gr, row0=0, n_seq=geo.n_prompt, seq_len=geo.prompt_len)
    ns = geo.n_sample
    n0 = jnp.pad(state_n[:, j].astype(F32)[..., None], ((0, 0),) * 4 + ((0, LANES - 1),))
    m0 = jnp.pad(state_m[:, j].astype(F32), ((0, 0), (0, 0), (0, SUBLANES - NH_A)))
    m0 = jnp.broadcast_to(m0[..., None], (ns, 2, SUBLANES, LANES))
    hs, _, _, _ = _mlstm_scan(q, kt, v, gr, row0=geo.t_prompt, n_seq=ns, seq_len=geo.sample_len,
                              state=(state_C[:, j].astype(F32), n0, m0))
    x1, routing = _out_a(geo, hp, hs, so, a_norm[j], a_w_out[j], x, mod_l, ln_g, ln_b, w_router, e_bias)
    return x1, routing, c_p, n_p[..., 0], m_p[:, :, :NH_A, 0]


def _hgrn_layer(geo, x, mod_l, j, lb_layer, b_w_in, b_norm, b_w_out, ln_g, ln_b, state_S, w_router, e_bias):
    q, pre, v, sg = _proj_b(geo, x, mod_l, b_w_in[j])
    lbd = lb_layer.reshape(2, NH_B, 1, DK_B)
    op, s_p = _hgrn_scan(q, pre, v, lbd, row0=0, n_seq=geo.n_prompt, seq_len=geo.prompt_len)
    os_, _ = _hgrn_scan(q, pre, v, lbd, row0=geo.t_prompt, n_seq=geo.n_sample, seq_len=geo.sample_len,
                        state=state_S[:, j].astype(F32))
    x1, routing = _out_b(geo, op, os_, sg, b_norm[j], b_w_out[j], x, mod_l, ln_g, ln_b, w_router, e_bias)
    return x1, routing, s_p


def kernel(x_prompt, x_sample, state_mlstm_C, state_mlstm_n, state_mlstm_m, state_hgrn_S, c, c_ctx, w_mod, b_mod, ln_g, ln_b, a_w_in, a_b_gates, a_norm, a_w_out, b_w_in, b_lb, b_norm, b_w_out, w_router, e_bias, w_gate, w_up, w_down, ws_gate, ws_up, ws_down):
    bp, sp, _ = x_prompt.shape
    bs, ss, _ = x_sample.shape
    cond = jnp.zeros((COND_ROWS, D), F32).at[0].set(c_ctx).at[1:1 + bs].set(c)
    mod = _modulation(cond, w_mod, b_mod)
    x = (x_prompt.reshape(-1, D), x_sample.reshape(-1, D), _pos_embed(ss // GRID_W))
    sm = jax.nn.softmax(b_lb.astype(F32), axis=0)
    lb_all = jnp.cumsum(sm, axis=0) - sm[0]
    geo = Geometry(bp, sp, bs, ss)
    x1, routing, new_c, new_n, new_m = _mlstm_layer(geo, x, mod[0], 0, a_w_in, a_b_gates, a_norm, a_w_out,
                                                    ln_g[0, 0], ln_b[0, 0], state_mlstm_C, state_mlstm_n,
                                                    state_mlstm_m, w_router[0], e_bias[0])
    x2 = _moe_layer(geo, x1, routing, mod[0], 0, w_gate, w_up, w_down, ws_gate[0], ws_up[0], ws_down[0],
                    ln_g[0, 1], ln_b[0, 1])
    x3, routing, new_s = _hgrn_layer(geo, x2, mod[1], 0, lb_all[1], b_w_in, b_norm, b_w_out, ln_g[1, 0], ln_b[1, 0],
                                     state_hgrn_S, w_router[1], e_bias[1])
    y_p, y_s = _moe_layer(geo, x3, routing, mod[1], 1, w_gate, w_up, w_down, ws_gate[1], ws_up[1], ws_down[1],
                          ln_g[1, 1], ln_b[1, 1], split=True)
    return (y_p.reshape(bp, sp, D), y_s.reshape(bs, ss, D), new_c[:, None], new_n[:, None], new_m[:, None],
            new_s[:, None])
```

```python
import functools
import math

import jax
import jax.numpy as jnp
from jax import lax
from jax.experimental import pallas as pl
from jax.experimental.pallas import tpu as pltpu
from jax.experimental.pallas import tpu_sc as plsc

F32 = jnp.float32
BF16 = jnp.bfloat16
HIGHEST = lax.Precision.HIGHEST

D = 1024
DEPTH = 2
GRID_W = 64
POS_BASE = 10000.0
EPS = 1e-6
ALPHA = (2.0 * DEPTH) ** 0.25
NH_A, DK_A, DV_A = 4, 128, 256
QK_A, V_A = NH_A * DK_A, NH_A * DV_A
NH_B, DK_B = 8, 128
N_EXPERTS, TOP_K, N_GROUPS, TOPK_GROUPS = 64, 8, 8, 4
GROUP_SIZE = N_EXPERTS // N_GROUPS
D_EXPERT = D // 4
ROUTED_SCALE = 2.5

LANES = 128
SUBLANES = 8
COND_ROWS = 8
TOK_TILE = 256
TOK_TILE_L = 512
SLOT_TILE = 2048
CHUNK_A = 512
VMEM_BYTES_V7X = 64 * 1024 * 1024
VMEM_LIMIT = VMEM_BYTES_V7X - 8 * 1024 * 1024

NT_DIMS = (((1,), (1,)), ((), ()))


def _params(sem):
    return pltpu.CompilerParams(dimension_semantics=sem, vmem_limit_bytes=VMEM_LIMIT)


def _split3(x):
    hi = x.astype(BF16)
    r = x - hi.astype(F32)
    mid = r.astype(BF16)
    lo = (r - mid.astype(F32)).astype(BF16)
    return hi, mid, lo


def _dot3(a_bf, x):
    hi, mid, lo = _split3(x)
    return (jnp.dot(a_bf, hi, preferred_element_type=F32)
            + jnp.dot(a_bf, mid, preferred_element_type=F32)
            + jnp.dot(a_bf, lo, preferred_element_type=F32))


def _dot3_r(x, a_bf):
    hi, mid, lo = _split3(x)
    return (jnp.dot(hi, a_bf, preferred_element_type=F32)
            + jnp.dot(mid, a_bf, preferred_element_type=F32)
            + jnp.dot(lo, a_bf, preferred_element_type=F32))


def _log_sigmoid(x):
    return jnp.minimum(x, 0.0) - jnp.log1p(jnp.exp(-jnp.abs(x)))


def _layer_norm_rows(x, g, b):
    mu = jnp.mean(x, axis=-1, keepdims=True)
    xc = x - mu
    var = jnp.mean(xc * xc, axis=-1, keepdims=True)
    return xc * lax.rsqrt(var + EPS) * g + b


class Geometry:
    def __init__(self, n_prompt, prompt_len, n_sample, sample_len):
        self.n_prompt, self.prompt_len = n_prompt, prompt_len
        self.n_sample, self.sample_len = n_sample, sample_len
        self.t_prompt = n_prompt * prompt_len
        self.t_sample = n_sample * sample_len
        self.t = self.t_prompt + self.t_sample
        assert self.t_prompt % TOK_TILE_L == 0 and sample_len % TOK_TILE_L == 0
        assert n_sample + 1 <= COND_ROWS

    def cond_row(self, tile, tile_rows):
        n_p = self.t_prompt // tile_rows
        return jnp.where(tile < n_p, 0, 1 + (tile - n_p) // (self.sample_len // tile_rows))


def _mod_kernel(cond_ref, w_ref, b_ref, o_ref):
    c = cond_ref[...]
    s = c * jax.nn.sigmoid(c)
    o_ref[0, 0] = jnp.dot(s, w_ref[0], precision=HIGHEST, preferred_element_type=F32) + b_ref[0, 0]


def _modulation(cond, w_mod, b_mod):
    out = pl.pallas_call(
        _mod_kernel,
        name="modulation",
        grid=(DEPTH, 6),
        in_specs=[
            pl.BlockSpec((COND_ROWS, D), lambda l, j: (0, 0)),
            pl.BlockSpec((1, D, D), lambda l, j: (l, 0, j)),
            pl.BlockSpec((1, 1, 1, D), lambda l, j: (l, j, 0, 0)),
        ],
        out_specs=pl.BlockSpec((1, 1, COND_ROWS, D), lambda l, j: (l, j, 0, 0)),
        out_shape=jax.ShapeDtypeStruct((DEPTH, 6, COND_ROWS, D), F32),
        compiler_params=_params(("arbitrary", "arbitrary")),
    )(cond, w_mod, b_mod.reshape(DEPTH, 6, 1, D))
    return out.transpose(0, 2, 1, 3)


def _embed_specs(geo, tb):
    n_p = geo.t_prompt // tb
    per_seq = geo.sample_len // tb
    return [pl.BlockSpec((tb, D), lambda i: (jnp.minimum(i, n_p - 1), 0)),
            pl.BlockSpec((tb, D), lambda i: (jnp.maximum(i - n_p, 0), 0)),
            pl.BlockSpec((tb, D), lambda i: (jnp.maximum(i - n_p, 0) % per_seq, 0))]


def _embed_tile(xp_ref, xs_ref, pos_ref, n_prompt_tiles):
    return jnp.where(pl.program_id(0) < n_prompt_tiles, xp_ref[...], xs_ref[...] + pos_ref[...])


def _proj_a_kernel(xp_ref, xs_ref, pos_ref, mod_ref, wq_ref, wkt_ref, wvo_ref, wgt_ref, bgt_ref,
                   q_ref, kt_ref, v_ref, so_ref, gr_ref, *, n_prompt_tiles):
    x = _embed_tile(xp_ref, xs_ref, pos_ref, n_prompt_tiles)
    h = x * (1.0 + mod_ref[1:2, :]) + mod_ref[0:1, :]
    hb = h.astype(BF16)
    q_ref[...] = jnp.dot(hb, wq_ref[...], preferred_element_type=F32).astype(BF16)
    kt = lax.dot_general(wkt_ref[...], hb, NT_DIMS, preferred_element_type=F32)
    kt_ref[...] = (kt * (DK_A ** -0.5)).astype(BF16)
    vo = jnp.dot(hb, wvo_ref[...], preferred_element_type=F32)
    v_ref[...] = vo[:, :V_A].astype(BF16)
    so_ref[...] = jax.nn.sigmoid(vo[:, V_A:]).astype(BF16)
    gr_ref[...] = lax.dot_general(wgt_ref[...], h, NT_DIMS, precision=HIGHEST,
                                  preferred_element_type=F32) + bgt_ref[...]


def _proj_a(geo, x, mod_l, w_in, b_gates):
    t = geo.t
    n_gate = 4 * NH_A
    wq = w_in[:, :QK_A].astype(BF16)
    wkt = w_in[:, QK_A:2 * QK_A].T.astype(BF16)
    wvo = w_in[:, 2 * QK_A:2 * QK_A + 2 * V_A].astype(BF16)
    wg = w_in[:, 2 * QK_A + 2 * V_A:]
    bg = b_gates.reshape(n_gate).astype(F32)
    tb = TOK_TILE_L
    full = lambda shape: pl.BlockSpec(shape, lambda i: (0,) * len(shape))
    return pl.pallas_call(
        functools.partial(_proj_a_kernel, n_prompt_tiles=geo.t_prompt // tb),
        name="proj_a",
        grid=(t // tb,),
        in_specs=_embed_specs(geo, tb) + [
            pl.BlockSpec((None, 6, D), lambda i: (geo.cond_row(i, tb), 0, 0)),
            full((D, QK_A)), full((QK_A, D)), full((D, 2 * V_A)), full((n_gate, D)), full((n_gate, 1)),
        ],
        out_specs=[
            pl.BlockSpec((tb, QK_A), lambda i: (i, 0)),
            pl.BlockSpec((QK_A, tb), lambda i: (0, i)),
            pl.BlockSpec((tb, V_A), lambda i: (i, 0)),
            pl.BlockSpec((tb, V_A), lambda i: (i, 0)),
            pl.BlockSpec((n_gate, tb), lambda i: (0, i)),
        ],
        out_shape=[
            jax.ShapeDtypeStruct((t, QK_A), BF16),
            jax.ShapeDtypeStruct((QK_A, t), BF16),
            jax.ShapeDtypeStruct((t, V_A), BF16),
            jax.ShapeDtypeStruct((t, V_A), BF16),
            jax.ShapeDtypeStruct((n_gate, t), F32),
        ],
        compiler_params=_params(("parallel",)),
    )(*x, mod_l, wq, wkt, wvo, wg.T, bg.reshape(n_gate, 1))


def _mlstm_scan_kernel(*refs, chunk, has_state):
    if has_state:
        (q_ref, kt_ref, v_ref, gr_ref, c0_ref, n0_ref, m0_ref,
         h_ref, c_out, n_out, m_out, c_sc, n_sc, m_sc) = refs
    else:
        (q_ref, kt_ref, v_ref, gr_ref,
         h_ref, c_out, n_out, m_out, c_sc, n_sc, m_sc) = refs
    L = chunk
    d = pl.program_id(1)
    c = pl.program_id(2)
    fwd = d == 0

    @pl.when(c == 0)
    def _():
        if has_state:
            c_sc[...] = c0_ref[0, 0]
            n_sc[...] = n0_ref[0, 0]
            m_sc[...] = m0_ref[0, 0]
        else:
            c_sc[...] = jnp.zeros_like(c_sc)
            n_sc[...] = jnp.zeros_like(n_sc)
            m_sc[...] = jnp.zeros_like(m_sc)

    row = lax.broadcasted_iota(jnp.int32, (L, L), 0)
    col = lax.broadcasted_iota(jnp.int32, (L, L), 1)
    sgn = 1 - 2 * d
    causal = (row - col) * sgn >= 0
    tri_t = ((col - row) * sgn >= 0).astype(BF16)

    gr = gr_ref[...]
    br_all = _dot3_r(_log_sigmoid(gr), tri_t)
    bc_all = jnp.concatenate([br_all, jnp.zeros((LANES - br_all.shape[0], L), F32)], axis=0).T
    ones_blk = (lax.broadcasted_iota(jnp.int32, (L, LANES), 1) == 0).astype(BF16)

    def gate_row(direction, gate, head):
        return (direction * 2 + gate) * NH_A + head

    for h in range(NH_A):
        ff, fb = gate_row(0, 1, h), gate_row(1, 1, h)
        gi, gb = gate_row(0, 0, h), gate_row(1, 0, h)
        b_c = jnp.where(fwd, bc_all[:, ff:ff + 1], bc_all[:, fb:fb + 1])
        b_r = jnp.where(fwd, br_all[ff:ff + 1, :], br_all[fb:fb + 1, :])
        i_r = jnp.where(fwd, gr[gi:gi + 1, :], gr[gb:gb + 1, :])
        bl = jnp.where(fwd, b_r[:, L - 1:L], b_r[:, 0:1])
        q = q_ref[:, h * DK_A:(h + 1) * DK_A]
        kt = kt_ref[h * DK_A:(h + 1) * DK_A, :]
        v = v_ref[:, h * DV_A:(h + 1) * DV_A]
        m = m_sc[h:h + 1, 0:1]
        cst = c_sc[h]
        nst = n_sc[h]

        a_r = i_r - b_r
        logd = jnp.where(causal, b_c + a_r, -jnp.inf)
        inter = b_c + m
        m_t = jnp.maximum(inter, jnp.max(logd, axis=1, keepdims=True))
        dmat = jnp.exp(logd - m_t)
        e_int = jnp.exp(inter - m_t)
        s = (jnp.dot(q, kt, preferred_element_type=F32) * dmat).astype(BF16)
        num = (jnp.dot(s, v, preferred_element_type=F32)
               + e_int * jnp.dot(q, cst.astype(BF16), preferred_element_type=F32))
        den = (jnp.dot(s, ones_blk, preferred_element_type=F32)
               + e_int * jnp.dot(q, nst.astype(BF16), preferred_element_type=F32))[:, 0:1]
        h_ref[:, h * DV_A:(h + 1) * DV_A] = (num / jnp.maximum(jnp.abs(den), jnp.exp(-m_t))).astype(BF16)

        logw = bl + a_r
        m_new = jnp.maximum(bl + m, jnp.max(logw, axis=1, keepdims=True))
        w = jnp.exp(logw - m_new)
        decay = jnp.exp(bl + m - m_new)
        kw = (kt.astype(F32) * w).astype(BF16)
        c_sc[h] = decay * cst + jnp.dot(kw, v, preferred_element_type=F32)
        n_sc[h] = decay * nst + jnp.dot(kw, ones_blk, preferred_element_type=F32)
        m_sc[h:h + 1, :] = jnp.broadcast_to(m_new, (1, LANES))

    @pl.when(c == pl.num_programs(2) - 1)
    def _():
        c_out[0, 0] = c_sc[...]
        n_out[0, 0] = n_sc[...]
        m_out[0, 0] = m_sc[...]


def _mlstm_scan(q, kt, v, gr, *, row0, n_seq, seq_len, state=None):
    L = min(CHUNK_A, seq_len)
    nc = seq_len // L
    blk0 = row0 // L

    def loc_blk(b, d, c):
        return b * nc + c + d * (nc - 1 - 2 * c)

    def tok_blk(b, d, c):
        return blk0 + loc_blk(b, d, c)

    in_specs = [
        pl.BlockSpec((L, QK_A), lambda b, d, c: (tok_blk(b, d, c), 0)),
        pl.BlockSpec((QK_A, L), lambda b, d, c: (0, tok_blk(b, d, c))),
        pl.BlockSpec((L, V_A), lambda b, d, c: (tok_blk(b, d, c), 0)),
        pl.BlockSpec((4 * NH_A, L), lambda b, d, c: (0, tok_blk(b, d, c))),
    ]
    args = [q, kt, v, gr]
    if state is not None:
        in_specs += [
            pl.BlockSpec((1, 1, NH_A, DK_A, DV_A), lambda b, d, c: (b, d, 0, 0, 0)),
            pl.BlockSpec((1, 1, NH_A, DK_A, LANES), lambda b, d, c: (b, d, 0, 0, 0)),
            pl.BlockSpec((1, 1, SUBLANES, LANES), lambda b, d, c: (b, d, 0, 0)),
        ]
        args += list(state)
    return pl.pallas_call(
        functools.partial(_mlstm_scan_kernel, chunk=L, has_state=state is not None),
        name="mlstm_scan_seeded" if state is not None else "mlstm_scan",
        grid=(n_seq, 2, nc),
        in_specs=in_specs,
        out_specs=[
            pl.BlockSpec((None, L, V_A), lambda b, d, c: (d, loc_blk(b, d, c), 0)),
            pl.BlockSpec((1, 1, NH_A, DK_A, DV_A), lambda b, d, c: (b, d, 0, 0, 0)),
            pl.BlockSpec((1, 1, NH_A, DK_A, LANES), lambda b, d, c: (b, d, 0, 0, 0)),
            pl.BlockSpec((1, 1, SUBLANES, LANES), lambda b, d, c: (b, d, 0, 0)),
        ],
        out_shape=[
            jax.ShapeDtypeStruct((2, n_seq * seq_len, V_A), BF16),
            jax.ShapeDtypeStruct((n_seq, 2, NH_A, DK_A, DV_A), F32),
            jax.ShapeDtypeStruct((n_seq, 2, NH_A, DK_A, LANES), F32),
            jax.ShapeDtypeStruct((n_seq, 2, SUBLANES, LANES), F32),
        ],
        scratch_shapes=[
            pltpu.VMEM((NH_A, DK_A, DV_A), F32),
            pltpu.VMEM((NH_A, DK_A, LANES), F32),
            pltpu.VMEM((SUBLANES, LANES), F32),
        ],
        compiler_params=_params(("parallel", "parallel", "arbitrary")),
    )(*args)


def _out_a_kernel(hp_ref, hs_ref, so_ref, nw_ref, w_ref, xp_ref, xs_ref, pos_ref, mod_ref, lg_ref, lb_ref,
                  wrt_ref, eb_ref, o_ref, *route_refs, n_prompt_tiles):
    is_prompt = pl.program_id(0) < n_prompt_tiles
    x = _embed_tile(xp_ref, xs_ref, pos_ref, n_prompt_tiles)
    y = jnp.where(is_prompt, hp_ref[0].astype(F32) + hp_ref[1].astype(F32),
                  hs_ref[0].astype(F32) + hs_ref[1].astype(F32))
    parts = []
    for h in range(NH_A):
        yh = y[:, h * DV_A:(h + 1) * DV_A]
        mu = jnp.mean(yh, axis=-1, keepdims=True)
        yc = yh - mu
        var = jnp.mean(yc * yc, axis=-1, keepdims=True)
        parts.append(yc * lax.rsqrt(var + EPS))
    yn = jnp.concatenate(parts, axis=-1) * nw_ref[...] * so_ref[...].astype(F32)
    out = jnp.dot(yn.astype(BF16), w_ref[...], preferred_element_type=F32)
    x1 = _layer_norm_rows(ALPHA * x + mod_ref[2:3, :] * out, lg_ref[...], lb_ref[...])
    o_ref[...] = x1
    _route_tile(x1, mod_ref, wrt_ref, eb_ref, *route_refs)


def _out_a(geo, h_prompt, h_sample, so, norm_w, w_out, x, mod_l, ln_g, ln_b, w_router, e_bias):
    t = geo.t
    tb = TOK_TILE_L
    n_p = geo.t_prompt // tb
    full = lambda shape: pl.BlockSpec(shape, lambda i: (0,) * len(shape))
    rio = _RouterIO(t, tb, w_router, e_bias)
    outs = pl.pallas_call(
        functools.partial(_out_a_kernel, n_prompt_tiles=n_p),
        name="out_a",
        grid=(t // tb,),
        in_specs=[
            pl.BlockSpec((2, tb, V_A), lambda i: (0, jnp.minimum(i, n_p - 1), 0)),
            pl.BlockSpec((2, tb, V_A), lambda i: (0, jnp.maximum(i - n_p, 0), 0)),
            pl.BlockSpec((tb, V_A), lambda i: (i, 0)),
            full((1, V_A)), full((V_A, D)),
        ] + _embed_specs(geo, tb) + [
            pl.BlockSpec((None, 6, D), lambda i: (geo.cond_row(i, tb), 0, 0)),
            full((1, D)), full((1, D)),
        ] + rio.in_specs,
        out_specs=[pl.BlockSpec((tb, D), lambda i: (i, 0))] + rio.out_specs,
        out_shape=[jax.ShapeDtypeStruct((t, D), F32)] + rio.out_shape,
        scratch_shapes=rio.scratch,
        compiler_params=_params(("arbitrary",)),
    )(h_prompt, h_sample, so, norm_w.reshape(1, V_A).astype(F32), w_out.astype(BF16), *x, mod_l,
      ln_g.reshape(1, D), ln_b.reshape(1, D), *rio.inputs)
    return outs[0], _RouterIO.unpack(outs[1:])


def _proj_b_kernel(x_ref, mod_ref, w_ref, q_ref, pre_ref, v_ref, sg_ref):
    h = x_ref[...] * (1.0 + mod_ref[1:2, :]) + mod_ref[0:1, :]
    z = jnp.dot(h.astype(BF16), w_ref[...], preferred_element_type=F32)
    for hd in range(NH_B):
        lo = hd * DK_B
        qh = z[:, lo:lo + DK_B]
        q_ref[hd] = qh * jax.nn.sigmoid(qh)
        pre_ref[0, hd] = z[:, D + lo:D + lo + DK_B]
        pre_ref[1, hd] = z[:, 2 * D + lo:2 * D + lo + DK_B]
        v_ref[hd] = z[:, 3 * D + lo:3 * D + lo + DK_B].astype(BF16)
    g = z[:, 4 * D:]
    sg_ref[...] = (g * jax.nn.sigmoid(g)).astype(BF16)


def _proj_b(geo, x, mod_l, w_in):
    t = geo.t
    tb = TOK_TILE
    return pl.pallas_call(
        _proj_b_kernel,
        name="proj_b",
        grid=(t // tb,),
        in_specs=[
            pl.BlockSpec((tb, D), lambda i: (i, 0)),
            pl.BlockSpec((None, 6, D), lambda i: (geo.cond_row(i, tb), 0, 0)),
            pl.BlockSpec((D, 5 * D), lambda i: (0, 0)),
        ],
        out_specs=[
            pl.BlockSpec((NH_B, tb, DK_B), lambda i: (0, i, 0)),
            pl.BlockSpec((2, NH_B, tb, DK_B), lambda i: (0, 0, i, 0)),
            pl.BlockSpec((NH_B, tb, DK_B), lambda i: (0, i, 0)),
            pl.BlockSpec((tb, D), lambda i: (i, 0)),
        ],
        out_shape=[
            jax.ShapeDtypeStruct((NH_B, t, DK_B), F32),
            jax.ShapeDtypeStruct((2, NH_B, t, DK_B), F32),
            jax.ShapeDtypeStruct((NH_B, t, DK_B), BF16),
            jax.ShapeDtypeStruct((t, D), BF16),
        ],
        compiler_params=_params(("parallel",)),
    )(x, mod_l, w_in.astype(BF16))


CHUNK_B = 256
BAND = SUBLANES // 2
TN_DIMS = (((0,), (0,)), ((), ()))


def _hgrn_head(q, pre, lbv, v_bf, st, fwd):
    L = q.shape[0]
    sg = jax.nn.sigmoid(pre)
    f = lbv + (1.0 - lbv) * sg
    lf = jnp.log(f)
    kk = (1.0 - lbv) * (1.0 - sg)
    row = lax.broadcasted_iota(jnp.int32, (L, L), 0)
    col = lax.broadcasted_iota(jnp.int32, (L, L), 1)
    tri = ((row >= col) if fwd else (row <= col)).astype(BF16)
    b = _dot3(tri, lf)
    tpos = lax.broadcasted_iota(jnp.int32, (L, DK_B), 0)
    blk_bits = row ^ col
    lag = jnp.where(blk_bits < BAND, (row - col) if fwd else (col - row), -1)

    step = 1 if fwd else L - 1
    att = jnp.where(lag == 0, jnp.sum(q * kk, axis=1, keepdims=True), 0.0)
    f_r, kk_r, g = f, kk, f
    for dl in range(1, BAND):
        if dl > 1:
            f_r = pltpu.roll(f_r, step, 0)
            g = g * f_r
        kk_r = pltpu.roll(kk_r, step, 0)
        att = jnp.where(lag == dl, jnp.sum(q * kk_r * g, axis=1, keepdims=True), att)

    w = BAND
    while w < L:
        nb = L // (2 * w)
        b3 = b.reshape(nb, 2 * w, DK_B)
        edge = (b3[:, w - 1:w, :] if fwd else b3[:, w:w + 1, :])
        bmid = jnp.broadcast_to(edge, (nb, 2 * w, DK_B)).reshape(L, DK_B)
        second = (tpos & w) != 0
        t_side = second if fwd else jnp.logical_not(second)
        e = jnp.exp(jnp.where(t_side, b - bmid, bmid - b))
        qt = jnp.where(t_side, q * e, 0.0).astype(BF16)
        ks = jnp.where(t_side, 0.0, kk * e).astype(BF16)
        a = lax.dot_general(qt, ks, NT_DIMS, preferred_element_type=F32)
        att = att + jnp.where(blk_bits < 2 * w, a, 0.0)
        w *= 2
    o = jnp.dot(att.astype(BF16), v_bf, preferred_element_type=F32)

    bl = b[L - 1:L, :] if fwd else b[0:1, :]
    o = o + lax.dot_general((q * jnp.exp(b)).astype(BF16), st.astype(BF16), NT_DIMS, preferred_element_type=F32)
    kd = (kk * jnp.exp(bl - b)).astype(BF16)
    st_new = jnp.exp(bl) * st + lax.dot_general(v_bf, kd, TN_DIMS, preferred_element_type=F32)
    return o, st_new


def _hgrn_scan_kernel(*refs, has_state):
    if has_state:
        q_ref, pre_ref, v_ref, lb_ref, s0_ref, o_ref, s_out, st_sc = refs
    else:
        q_ref, pre_ref, v_ref, lb_ref, o_ref, s_out, st_sc = refs
    d = pl.program_id(1)
    c = pl.program_id(2)

    @pl.when(c == 0)
    def _():
        if has_state:
            for hd in range(NH_B):
                st_sc[hd] = s0_ref[0, 0, hd].T
        else:
            st_sc[...] = jnp.zeros_like(st_sc)

    def run(fwd):
        def head(hd, carry):
            o, st_new = _hgrn_head(q_ref[hd], pre_ref[hd], lb_ref[hd], v_ref[hd], st_sc[hd], fwd)
            o_ref[hd] = o.astype(BF16)
            st_sc[hd] = st_new
            return carry
        lax.fori_loop(0, NH_B, head, 0, unroll=8)

    @pl.when(d == 0)
    def _():
        run(True)

    @pl.when(d == 1)
    def _():
        run(False)

    @pl.when(c == pl.num_programs(2) - 1)
    def _():
        for hd in range(NH_B):
            s_out[0, 0, hd] = st_sc[hd].T


def _hgrn_scan(q, pre, v, lbd, *, row0, n_seq, seq_len, state=None):
    L = CHUNK_B
    nc = seq_len // L
    blk0 = row0 // L

    def loc_blk(b, d, c):
        return b * nc + c + d * (nc - 1 - 2 * c)

    def tok_blk(b, d, c):
        return blk0 + loc_blk(b, d, c)

    in_specs = [
        pl.BlockSpec((NH_B, L, DK_B), lambda b, d, c: (0, tok_blk(b, d, c), 0)),
        pl.BlockSpec((None, NH_B, L, DK_B), lambda b, d, c: (d, 0, tok_blk(b, d, c), 0)),
        pl.BlockSpec((NH_B, L, DK_B), lambda b, d, c: (0, tok_blk(b, d, c), 0)),
        pl.BlockSpec((None, NH_B, 1, DK_B), lambda b, d, c: (d, 0, 0, 0)),
    ]
    args = [q, pre, v, lbd]
    if state is not None:
        in_specs.append(pl.BlockSpec((1, 1, NH_B, DK_B, DK_B), lambda b, d, c: (b, d, 0, 0, 0)))
        args.append(state)
    return pl.pallas_call(
        functools.partial(_hgrn_scan_kernel, has_state=state is not None),
        name="hgrn_scan_seeded" if state is not None else "hgrn_scan",
        grid=(n_seq, 2, nc),
        in_specs=in_specs,
        out_specs=[
            pl.BlockSpec((None, NH_B, L, DK_B), lambda b, d, c: (d, 0, loc_blk(b, d, c), 0)),
            pl.BlockSpec((1, 1, NH_B, DK_B, DK_B), lambda b, d, c: (b, d, 0, 0, 0)),
        ],
        out_shape=[
            jax.ShapeDtypeStruct((2, NH_B, n_seq * seq_len, DK_B), BF16),
            jax.ShapeDtypeStruct((n_seq, 2, NH_B, DK_B, DK_B), F32),
        ],
        scratch_shapes=[pltpu.VMEM((NH_B, DK_B, DK_B), F32)],
        compiler_params=_params(("parallel", "parallel", "arbitrary")),
    )(*args)


def _out_b_kernel(op_ref, os_ref, sg_ref, nw_ref, w_ref, x_ref, mod_ref, lg_ref, lb_ref, wrt_ref, eb_ref,
                  out_ref, *route_refs, n_prompt_tiles):
    is_prompt = pl.program_id(0) < n_prompt_tiles
    parts = []
    for hd in range(NH_B):
        y = jnp.where(is_prompt, op_ref[0, hd].astype(F32) + op_ref[1, hd].astype(F32),
                      os_ref[0, hd].astype(F32) + os_ref[1, hd].astype(F32))
        parts.append(y * lax.rsqrt(jnp.mean(y * y, axis=-1, keepdims=True) + EPS))
    yn = jnp.concatenate(parts, axis=-1) * nw_ref[...] * sg_ref[...].astype(F32)
    out = jnp.dot(yn.astype(BF16), w_ref[...], preferred_element_type=F32)
    x1 = _layer_norm_rows(ALPHA * x_ref[...] + mod_ref[2:3, :] * out, lg_ref[...], lb_ref[...])
    out_ref[...] = x1
    _route_tile(x1, mod_ref, wrt_ref, eb_ref, *route_refs)


def _out_b(geo, o_prompt, o_sample, sg, norm_w, w_out, x, mod_l, ln_g, ln_b, w_router, e_bias):
    t = geo.t
    tb = TOK_TILE_L
    n_p = geo.t_prompt // tb
    full = lambda shape: pl.BlockSpec(shape, lambda i: (0,) * len(shape))
    rio = _RouterIO(t, tb, w_router, e_bias)
    outs = pl.pallas_call(
        functools.partial(_out_b_kernel, n_prompt_tiles=n_p),
        name="out_b",
        grid=(t // tb,),
        in_specs=[
            pl.BlockSpec((2, NH_B, tb, DK_B), lambda i: (0, 0, jnp.minimum(i, n_p - 1), 0)),
            pl.BlockSpec((2, NH_B, tb, DK_B), lambda i: (0, 0, jnp.maximum(i - n_p, 0), 0)),
            pl.BlockSpec((tb, D), lambda i: (i, 0)),
            full((1, D)), full((D, D)),
            pl.BlockSpec((tb, D), lambda i: (i, 0)),
            pl.BlockSpec((None, 6, D), lambda i: (geo.cond_row(i, tb), 0, 0)),
            full((1, D)), full((1, D)),
        ] + rio.in_specs,
        out_specs=[pl.BlockSpec((tb, D), lambda i: (i, 0))] + rio.out_specs,
        out_shape=[jax.ShapeDtypeStruct((t, D), F32)] + rio.out_shape,
        scratch_shapes=rio.scratch,
        compiler_params=_params(("arbitrary",)),
    )(o_prompt, o_sample, sg, norm_w.reshape(1, D).astype(F32), w_out.astype(BF16), x, mod_l,
      ln_g.reshape(1, D), ln_b.reshape(1, D), *rio.inputs)
    return outs[0], _RouterIO.unpack(outs[1:])


MOE_BLK = 1024
U32 = jnp.uint32
ROW_WORDS = D // 2
CHUNK_W = 256
ROW_CHUNKS = ROW_WORDS // CHUNK_W
SC_WINDOW = 128


def _pack_rows(x):
    return pltpu.pack_elementwise([x[:, :ROW_WORDS], x[:, ROW_WORDS:]], packed_dtype=BF16)


def _unpack_rows(words):
    return jnp.concatenate([pltpu.unpack_elementwise(words, index=i, packed_dtype=BF16, unpacked_dtype=F32)
                            for i in range(2)], axis=1)


def _store_chunks(chunk_ref, x):
    words = _pack_rows(x)
    for c in range(ROW_CHUNKS):
        chunk_ref(c)[...] = words[:, c * CHUNK_W:(c + 1) * CHUNK_W]


def _load_chunks(chunk_ref, valid_rows=None):
    words = jnp.concatenate([chunk_ref(c)[...] for c in range(ROW_CHUNKS)], axis=1)
    if valid_rows is not None:
        row = lax.broadcasted_iota(jnp.int32, (words.shape[0], 1), 0)
        words = jnp.where(row < valid_rows, words, jnp.uint32(0))
    return _unpack_rows(words)


def _first_index(hit, iota, size, axis):
    return jnp.min(jnp.where(hit, iota, size), axis=axis, keepdims=True)


def _route_tile(x, mod_ref, wrt_ref, eb_ref, e_ref, w_ref, r_ref, cnt_ref, h_ref, cnt_sc):
    i = pl.program_id(0)
    tb = x.shape[0]

    @pl.when(i == 0)
    def _():
        cnt_sc[...] = jnp.zeros_like(cnt_sc)

    h = x * (1.0 + mod_ref[4:5, :]) + mod_ref[3:4, :]
    _store_chunks(lambda c: h_ref.at[c], h)
    logits = lax.dot_general(wrt_ref[...], h, NT_DIMS, precision=HIGHEST, preferred_element_type=F32)
    scores = jax.nn.sigmoid(logits)
    sel = scores + eb_ref[...]

    g3 = sel.reshape(N_GROUPS, GROUP_SIZE, tb)
    io3 = lax.broadcasted_iota(jnp.int32, g3.shape, 1)
    m1 = jnp.max(g3, axis=1, keepdims=True)
    first = _first_index(g3 == m1, io3, GROUP_SIZE, 1)
    m2 = jnp.max(jnp.where(io3 == first, -jnp.inf, g3), axis=1, keepdims=True)
    gscore = (m1 + m2).reshape(N_GROUPS, tb)

    iog = lax.broadcasted_iota(jnp.int32, gscore.shape, 0)
    gmask = jnp.zeros(gscore.shape, F32)
    for _ in range(TOPK_GROUPS):
        gm = jnp.max(gscore, axis=0, keepdims=True)
        pick = iog == _first_index(gscore == gm, iog, N_GROUPS, 0)
        gmask = jnp.where(pick, 1.0, gmask)
        gscore = jnp.where(pick, -jnp.inf, gscore)
    emask = jnp.broadcast_to(gmask.reshape(N_GROUPS, 1, tb), (N_GROUPS, GROUP_SIZE, tb)).reshape(N_EXPERTS, tb)
    cand = jnp.where(emask > 0.0, sel, -jnp.inf)

    ioe = lax.broadcasted_iota(jnp.int32, cand.shape, 0)
    picks, wts = [], []
    onehot = jnp.zeros(cand.shape, F32)
    for _ in range(TOP_K):
        cm = jnp.max(cand, axis=0, keepdims=True)
        idx = _first_index(cand == cm, ioe, N_EXPERTS, 0)
        pick = ioe == idx
        picks.append(pick)
        wts.append(jnp.sum(jnp.where(pick, scores, 0.0), axis=0, keepdims=True))
        onehot = onehot + pick.astype(F32)
        cand = jnp.where(pick, -jnp.inf, cand)
        e_ref[pl.ds(len(picks) - 1, 1), :] = idx
    wsum = wts[0]
    for wk in wts[1:]:
        wsum = wsum + wk
    for k in range(TOP_K):
        w_ref[pl.ds(k, 1), :] = wts[k] / wsum * ROUTED_SCALE

    r_io = lax.broadcasted_iota(jnp.int32, (tb, tb), 0)
    c_io = lax.broadcasted_iota(jnp.int32, (tb, tb), 1)
    before = (r_io < c_io).astype(BF16)
    rank = cnt_sc[:, 0:1] + jnp.dot(onehot.astype(BF16), before, preferred_element_type=F32)
    for k in range(TOP_K):
        r_ref[pl.ds(k, 1), :] = jnp.sum(jnp.where(picks[k], rank, 0.0), axis=0, keepdims=True).astype(jnp.int32)
    cnt_sc[...] = cnt_sc[...] + jnp.sum(onehot, axis=1, keepdims=True)
    cnt_ref[...] = cnt_sc[...]


class _RouterIO:
    def __init__(self, t, tb, w_router, e_bias):
        full = lambda shape: pl.BlockSpec(shape, lambda i: (0,) * len(shape))
        self.inputs = [w_router.T.astype(F32), e_bias.reshape(N_EXPERTS, 1).astype(F32)]
        self.in_specs = [full((N_EXPERTS, D)), full((N_EXPERTS, 1))]
        self.out_specs = [
            pl.BlockSpec((TOP_K, tb), lambda i: (0, i)),
            pl.BlockSpec((TOP_K, tb), lambda i: (0, i)),
            pl.BlockSpec((TOP_K, tb), lambda i: (0, i)),
            full((N_EXPERTS, LANES)),
            pl.BlockSpec((ROW_CHUNKS, tb, CHUNK_W), lambda i: (0, i, 0)),
        ]
        self.out_shape = [
            jax.ShapeDtypeStruct((TOP_K, t), jnp.int32),
            jax.ShapeDtypeStruct((TOP_K, t), F32),
            jax.ShapeDtypeStruct((TOP_K, t), jnp.int32),
            jax.ShapeDtypeStruct((N_EXPERTS, LANES), F32),
            jax.ShapeDtypeStruct((ROW_CHUNKS, t, CHUNK_W), U32),
        ]
        self.scratch = [pltpu.VMEM((N_EXPERTS, LANES), F32)]

    @staticmethod
    def unpack(outs):
        e, w, r, cnt, h = outs
        return e, w, r, cnt[:, 0].astype(jnp.int32), h


def _slot_kernel(pstart_ref, e_ref, r_ref, o_ref):
    e = e_ref[...]
    slot = r_ref[...]
    for x in range(N_EXPERTS):
        slot = slot + jnp.where(e == x, pstart_ref[x], 0)
    o_ref[...] = slot


def _slots(geo, pstart, top_e, rank):
    tb = math.gcd(SLOT_TILE, geo.t)
    return pl.pallas_call(
        _slot_kernel,
        name="slots",
        grid_spec=pltpu.PrefetchScalarGridSpec(
            num_scalar_prefetch=1,
            grid=(geo.t // tb,),
            in_specs=[pl.BlockSpec((TOP_K, tb), lambda i, p: (0, i)),
                      pl.BlockSpec((TOP_K, tb), lambda i, p: (0, i))],
            out_specs=pl.BlockSpec((TOP_K, tb), lambda i, p: (0, i)),
        ),
        out_shape=jax.ShapeDtypeStruct((TOP_K, geo.t), jnp.int32),
        compiler_params=_params(("parallel",)),
    )(pstart, top_e, rank)


def _block_meta_kernel(pstart_ref, counts_ref, pend_ref, e_ref, v_ref):
    row0 = lax.broadcasted_iota(jnp.int32, e_ref.shape, 1) * MOE_BLK
    blk_e = jnp.zeros(e_ref.shape, jnp.int32)
    for x in range(N_EXPERTS):
        blk_e = blk_e + jnp.where(pend_ref[x] <= row0, 1, 0)
    blk_e = jnp.minimum(blk_e, N_EXPERTS - 1)
    last = jnp.zeros(e_ref.shape, jnp.int32)
    for x in range(N_EXPERTS):
        last = last + jnp.where(blk_e == x, pstart_ref[x] + counts_ref[x], 0)
    e_ref[...] = blk_e
    v_ref[...] = jnp.clip(last - row0, 0, MOE_BLK)


def _block_meta(pstart, counts, pend, n_blocks):
    e, v = pl.pallas_call(
        _block_meta_kernel,
        name="block_meta",
        grid_spec=pltpu.PrefetchScalarGridSpec(
            num_scalar_prefetch=3,
            grid=(1,),
            in_specs=[],
            out_specs=[pl.BlockSpec((1, n_blocks), lambda i, a, b, c: (0, 0)),
                       pl.BlockSpec((1, n_blocks), lambda i, a, b, c: (0, 0))],
        ),
        out_shape=[jax.ShapeDtypeStruct((1, n_blocks), jnp.int32), jax.ShapeDtypeStruct((1, n_blocks), jnp.int32)],
        compiler_params=_params(("arbitrary",)),
    )(pstart, counts, pend)
    return e[0], v[0]


def _sc_mesh():
    return plsc.VectorSubcoreMesh(core_axis_name="core", subcore_axis_name="subcore")


def _sc_scatter(rows, idx, n_out, copies):
    n_src = rows.shape[0]
    n_idx = idx.shape[0]
    groups = ROW_CHUNKS
    win_per_group = n_src // groups // SC_WINDOW

    def idx_block(w, k):
        return (0, ((w // win_per_group) * copies + k) * win_per_group + w % win_per_group)

    @pl.kernel(out_type=jax.ShapeDtypeStruct((n_out, CHUNK_W), rows.dtype), mesh=_sc_mesh(), scratch_types=[],
               name="sc_dispatch")
    def scatter(x_hbm, i_hbm, o_hbm):
        def body(x_vmem, *i_vmems):
            for i_vmem in i_vmems:
                pltpu.sync_copy(x_vmem, o_hbm.at[i_vmem.at[0]])

        pltpu.emit_pipeline(
            body,
            grid=(n_src // SC_WINDOW,),
            in_specs=[pl.BlockSpec((SC_WINDOW, CHUNK_W), index_map=lambda w: (w, 0))]
            + [pl.BlockSpec((1, SC_WINDOW), index_map=functools.partial(idx_block, k=k)) for k in range(copies)],
            out_specs=[],
            core_axis_name=("core", "subcore"),
            dimension_semantics=(pltpu.PARALLEL,),
        )(x_hbm, *([i_hbm] * copies))

    return scatter(rows, idx.reshape(1, n_idx))


def _sc_gather(table, idx):
    n_idx = idx.shape[0]

    @pl.kernel(out_type=jax.ShapeDtypeStruct((n_idx, CHUNK_W), table.dtype), mesh=_sc_mesh(),
               name="sc_combine_gather")
    def gather(t_hbm, i_hbm, o_hbm):
        def body(i_vmem, o_vmem):
            pltpu.sync_copy(t_hbm.at[i_vmem.at[0]], o_vmem)

        pltpu.emit_pipeline(
            body,
            grid=(n_idx // SC_WINDOW,),
            in_specs=[pl.BlockSpec((1, SC_WINDOW), index_map=lambda w: (0, w))],
            out_specs=[pl.BlockSpec((SC_WINDOW, CHUNK_W), index_map=lambda w: (w, 0))],
            core_axis_name=("core", "subcore"),
            dimension_semantics=(pltpu.PARALLEL,),
        )(i_hbm, o_hbm)

    return gather(table, idx.reshape(1, n_idx))


def _ffn_kernel(blk_e_ref, blk_valid_ref, n_used_ref, xs_ref, wg_ref, wu_ref, wd_ref, y_ref, wg_sc, wu_sc, wd_sc):
    b = pl.program_id(0)
    used = b < n_used_ref[0]
    new_expert = (b == 0) | (blk_e_ref[b] != blk_e_ref[jnp.maximum(b - 1, 0)])

    @pl.when(used & new_expert)
    def _():
        wg_sc[...] = wg_ref[...].astype(BF16)
        wu_sc[...] = wu_ref[...].astype(BF16)
        wd_sc[...] = wd_ref[...].astype(BF16)

    @pl.when(used)
    def _():
        x = _load_chunks(lambda c: xs_ref.at[c], valid_rows=blk_valid_ref[b]).astype(BF16)
        g = jnp.dot(x, wg_sc[...], preferred_element_type=F32)
        u = jnp.dot(x, wu_sc[...], preferred_element_type=F32)
        hmid = (g * jax.nn.sigmoid(g) * u).astype(BF16)
        _store_chunks(lambda c: y_ref.at[c], jnp.dot(hmid, wd_sc[...], preferred_element_type=F32))

    @pl.when(jnp.logical_not(used))
    def _():
        y_ref[...] = jnp.zeros_like(y_ref)


def _ffn(xs, blk_e, blk_valid, n_used, layer, wg, wu, wd, n_blocks):
    def blk(b, be, bv, nu):
        return jnp.maximum(jnp.minimum(b, nu[0] - 1), 0)

    def w_idx(b, be, bv, nu):
        return (layer, be[blk(b, be, bv, nu)], 0, 0)

    return pl.pallas_call(
        _ffn_kernel,
        name="expert_ffn",
        grid_spec=pltpu.PrefetchScalarGridSpec(
            num_scalar_prefetch=3,
            grid=(n_blocks,),
            in_specs=[
                pl.BlockSpec((ROW_CHUNKS, MOE_BLK, CHUNK_W), lambda b, be, bv, nu: (0, blk(b, be, bv, nu), 0)),
                pl.BlockSpec((None, None, D, D_EXPERT), w_idx),
                pl.BlockSpec((None, None, D, D_EXPERT), w_idx),
                pl.BlockSpec((None, None, D_EXPERT, D), w_idx),
            ],
            out_specs=pl.BlockSpec((ROW_CHUNKS, MOE_BLK, CHUNK_W), lambda b, be, bv, nu: (0, b, 0)),
            scratch_shapes=[pltpu.VMEM((D, D_EXPERT), BF16), pltpu.VMEM((D, D_EXPERT), BF16),
                            pltpu.VMEM((D_EXPERT, D), BF16)],
        ),
        out_shape=jax.ShapeDtypeStruct(xs.shape, U32),
        compiler_params=_params(("arbitrary",)),
    )(blk_e, blk_valid, n_used, xs, wg, wu, wd)


def _combine_kernel(x_ref, mod_ref, wt_ref, y_ref, sg_ref, su_ref, sd_ref, lg_ref, lb_ref, *o_refs, n_prompt_tiles):
    x = x_ref[...]
    hb = (x * (1.0 + mod_ref[4:5, :]) + mod_ref[3:4, :]).astype(BF16)
    g = jnp.dot(hb, sg_ref[...], preferred_element_type=F32)
    u = jnp.dot(hb, su_ref[...], preferred_element_type=F32)
    ff = jnp.dot((g * jax.nn.sigmoid(g) * u).astype(BF16), sd_ref[...], preferred_element_type=F32)
    for k in range(TOP_K):
        ff = ff + _load_chunks(lambda c: y_ref.at[c, k]) * wt_ref[:, k:k + 1]
    out = _layer_norm_rows(ALPHA * x + mod_ref[5:6, :] * ff, lg_ref[...], lb_ref[...])
    if len(o_refs) == 1:
        o_refs[0][...] = out
    else:
        is_prompt = pl.program_id(0) < n_prompt_tiles

        @pl.when(is_prompt)
        def _():
            o_refs[0][...] = out

        @pl.when(jnp.logical_not(is_prompt))
        def _():
            o_refs[1][...] = out


def _combine(geo, x, mod_l, wt, ytok, sg, su, sd, ln_g, ln_b, split=False):
    tb = TOK_TILE_L
    n_p = geo.t_prompt // tb
    full = lambda shape: pl.BlockSpec(shape, lambda i: (0,) * len(shape))
    if split:
        out_specs = [pl.BlockSpec((tb, D), lambda i: (jnp.minimum(i, n_p - 1), 0)),
                     pl.BlockSpec((tb, D), lambda i: (jnp.maximum(i - n_p, 0), 0))]
        out_shape = [jax.ShapeDtypeStruct((geo.t_prompt, D), F32), jax.ShapeDtypeStruct((geo.t_sample, D), F32)]
    else:
        out_specs = pl.BlockSpec((tb, D), lambda i: (i, 0))
        out_shape = jax.ShapeDtypeStruct((geo.t, D), F32)
    return pl.pallas_call(
        functools.partial(_combine_kernel, n_prompt_tiles=n_p),
        name="combine",
        grid=(geo.t // tb,),
        in_specs=[
            pl.BlockSpec((tb, D), lambda i: (i, 0)),
            pl.BlockSpec((None, 6, D), lambda i: (geo.cond_row(i, tb), 0, 0)),
            pl.BlockSpec((tb, TOP_K), lambda i: (i, 0)),
            pl.BlockSpec((ROW_CHUNKS, TOP_K, tb, CHUNK_W), lambda i: (0, 0, i, 0)),
            full((D, D_EXPERT)), full((D, D_EXPERT)), full((D_EXPERT, D)), full((1, D)), full((1, D)),
        ],
        out_specs=out_specs,
        out_shape=out_shape,
        compiler_params=_params(("arbitrary",)),
    )(x, mod_l, wt, ytok, sg.astype(BF16), su.astype(BF16), sd.astype(BF16),
      ln_g.reshape(1, D), ln_b.reshape(1, D))


def _moe_layer(geo, x, routing, mod_l, layer, wg, wu, wd, sg, su, sd, ln_g, ln_b, split=False):
    t = geo.t
    top_e, w, rank, counts, h = routing
    n_blocks = (t * TOP_K) // MOE_BLK + N_EXPERTS
    n_rows = n_blocks * MOE_BLK
    padded = (counts + MOE_BLK - 1) // MOE_BLK * MOE_BLK
    pend = jnp.cumsum(padded)
    pstart = (pend - padded).astype(jnp.int32)
    blk_e, blk_valid = _block_meta(pstart, counts, pend.astype(jnp.int32), n_blocks)
    n_used = (pend[-1:] // MOE_BLK).astype(jnp.int32)
    slots = _slots(geo, pstart, top_e, rank)
    idx = (slots.reshape(1, TOP_K * t) + (jnp.arange(ROW_CHUNKS, dtype=jnp.int32) * n_rows)[:, None]).reshape(-1)
    xs = _sc_scatter(h.reshape(ROW_CHUNKS * t, CHUNK_W), idx, ROW_CHUNKS * n_rows, TOP_K)
    yb = _ffn(xs.reshape(ROW_CHUNKS, n_rows, CHUNK_W), blk_e, blk_valid, n_used, layer, wg, wu, wd, n_blocks)
    ytok = _sc_gather(yb.reshape(ROW_CHUNKS * n_rows, CHUNK_W), idx)
    return _combine(geo, x, mod_l, w.T, ytok.reshape(ROW_CHUNKS, TOP_K, t, CHUNK_W), sg, su, sd, ln_g, ln_b,
                    split=split)


def _pos_embed(rows):
    quarter = D // 4
    omega = 1.0 / (POS_BASE ** (jnp.arange(quarter, dtype=F32) / quarter))
    r = jnp.arange(rows, dtype=F32)[:, None] * omega
    col = jnp.arange(GRID_W, dtype=F32)[:, None] * omega
    row_part = jnp.concatenate([jnp.sin(r), jnp.cos(r)], axis=-1)[:, None, :]
    col_part = jnp.concatenate([jnp.sin(col), jnp.cos(col)], axis=-1)[None, :, :]
    shape = (rows, GRID_W, 2 * quarter)
    return jnp.concatenate([jnp.broadcast_to(row_part, shape), jnp.broadcast_to(col_part, shape)],
                           axis=-1).reshape(rows * GRID_W, D)


def _mlstm_layer(geo, x, mod_l, j, a_w_in, a_b_gates, a_norm, a_w_out, ln_g, ln_b,
                 state_C, state_n, state_m, w_router, e_bias):
    q, kt, v, so, gr = _proj_a(geo, x, mod_l, a_w_in[j], a_b_gates[j])
    hp, c_p, n_p, m_p = _mlstm_scan(q, kt, v, gr, row0=0, n_seq=geo.n_prompt, seq_len=geo.prompt_len)
    ns = geo.n_sample
    n0 = jnp.pad(state_n[:, j].astype(F32)[..., None], ((0, 0),) * 4 + ((0, LANES - 1),))
    m0 = jnp.pad(state_m[:, j].astype(F32), ((0, 0), (0, 0), (0, SUBLANES - NH_A)))
    m0 = jnp.broadcast_to(m0[..., None], (ns, 2, SUBLANES, LANES))
    hs, _, _, _ = _mlstm_scan(q, kt, v, gr, row0=geo.t_prompt, n_seq=ns, seq_len=geo.sample_len,
                              state=(state_C[:, j].astype(F32), n0, m0))
    x1, routing = _out_a(geo, hp, hs, so, a_norm[j], a_w_out[j], x, mod_l, ln_g, ln_b, w_router, e_bias)
    return x1, routing, c_p, n_p[..., 0], m_p[:, :, :NH_A, 0]


def _hgrn_layer(geo, x, mod_l, j, lb_layer, b_w_in, b_norm, b_w_out, ln_g, ln_b, state_S, w_router, e_bias):
    q, pre, v, sg = _proj_b(geo, x, mod_l, b_w_in[j])
    lbd = lb_layer.reshape(2, NH_B, 1, DK_B)
    op, s_p = _hgrn_scan(q, pre, v, lbd, row0=0, n_seq=geo.n_prompt, seq_len=geo.prompt_len)
    os_, _ = _hgrn_scan(q, pre, v, lbd, row0=geo.t_prompt, n_seq=geo.n_sample, seq_len=geo.sample_len,
                        state=state_S[:, j].astype(F32))
    x1, routing = _out_b(geo, op, os_, sg, b_norm[j], b_w_out[j], x, mod_l, ln_g, ln_b, w_router, e_bias)
    return x1, routing, s_p


def kernel(x_prompt, x_sample, state_mlstm_C, state_mlstm_n, state_mlstm_m, state_hgrn_S, c, c_ctx, w_mod, b_mod, ln_g, ln_b, a_w_in, a_b_gates, a_norm, a_w_out, b_w_in, b_lb, b_norm, b_w_out, w_router, e_bias, w_gate, w_up, w_down, ws_gate, ws_up, ws_down):
    bp, sp, _ = x_prompt.shape
    bs, ss, _ = x_sample.shape
    cond = jnp.zeros((COND_ROWS, D), F32).at[0].set(c_ctx).at[1:1 + bs].set(c)
    mod = _modulation(cond, w_mod, b_mod)
    x = (x_prompt.reshape(-1, D), x_sample.reshape(-1, D), _pos_embed(ss // GRID_W))
    sm = jax.nn.softmax(b_lb.astype(F32), axis=0)
    lb_all = jnp.cumsum(sm, axis=0) - sm[0]
    geo = Geometry(bp, sp, bs, ss)
    x1, routing, new_c, new_n, new_m = _mlstm_layer(geo, x, mod[0], 0, a_w_in, a_b_gates, a_norm, a_w_out,
                                                    ln_g[0, 0], ln_b[0, 0], state_mlstm_C, state_mlstm_n,
                                                    state_mlstm_m, w_router[0], e_bias[0])
    x2 = _moe_layer(geo, x1, routing, mod[0], 0, w_gate, w_up, w_down, ws_gate[0], ws_up[0], ws_down[0],
                    ln_g[0, 1], ln_b[0, 1])
    x3, routing, new_s = _hgrn_layer(geo, x2, mod[1], 0, lb_all[1], b_w_in, b_norm, b_w_out, ln_g[1, 0], ln_b[1, 0],
                                     state_hgrn_S, w_router[1], e_bias[1])
    y_p, y_s = _moe_layer(geo, x3, routing, mod[1], 1, w_gate, w_up, w_down, ws_gate[1], ws_up[1], ws_down[1],
                          ln_g[1, 1], ln_b[1, 1], split=True)
    return (y_p.reshape(bp, sp, D), y_s.reshape(bs, ss, D), new_c[:, None], new_n[:, None], new_m[:, None],
            new_s[:, None])
```

```python
import functools
import math

import jax
import jax.numpy as jnp
from jax import lax
from jax.experimental import pallas as pl
from jax.experimental.pallas import tpu as pltpu
from jax.experimental.pallas import tpu_sc as plsc

F32 = jnp.float32
BF16 = jnp.bfloat16
HIGHEST = lax.Precision.HIGHEST

D = 1024
DEPTH = 2
GRID_W = 64
POS_BASE = 10000.0
EPS = 1e-6
ALPHA = (2.0 * DEPTH) ** 0.25
NH_A, DK_A, DV_A = 4, 128, 256
QK_A, V_A = NH_A * DK_A, NH_A * DV_A
NH_B, DK_B = 8, 128
N_EXPERTS, TOP_K, N_GROUPS, TOPK_GROUPS = 64, 8, 8, 4
GROUP_SIZE = N_EXPERTS // N_GROUPS
D_EXPERT = D // 4
ROUTED_SCALE = 2.5

LANES = 128
SUBLANES = 8
COND_ROWS = 8
TOK_TILE = 256
TOK_TILE_L = 512
SLOT_TILE = 2048
CHUNK_A = 512
VMEM_BYTES_V7X = 64 * 1024 * 1024
VMEM_LIMIT = VMEM_BYTES_V7X - 8 * 1024 * 1024

NT_DIMS = (((1,), (1,)), ((), ()))


def _params(sem):
    return pltpu.CompilerParams(dimension_semantics=sem, vmem_limit_bytes=VMEM_LIMIT)


def _split3(x):
    hi = x.astype(BF16)
    r = x - hi.astype(F32)
    mid = r.astype(BF16)
    lo = (r - mid.astype(F32)).astype(BF16)
    return hi, mid, lo


def _dot3(a_bf, x):
    hi, mid, lo = _split3(x)
    return (jnp.dot(a_bf, hi, preferred_element_type=F32)
            + jnp.dot(a_bf, mid, preferred_element_type=F32)
            + jnp.dot(a_bf, lo, preferred_element_type=F32))


def _dot3_r(x, a_bf):
    hi, mid, lo = _split3(x)
    return (jnp.dot(hi, a_bf, preferred_element_type=F32)
            + jnp.dot(mid, a_bf, preferred_element_type=F32)
            + jnp.dot(lo, a_bf, preferred_element_type=F32))


def _dot_nt_x3(w, x):
    w_hi = w.astype(BF16)
    w_lo = (w - w_hi.astype(F32)).astype(BF16)
    x_hi = x.astype(BF16)
    x_lo = (x - x_hi.astype(F32)).astype(BF16)
    d = lambda a, b: lax.dot_general(a, b, NT_DIMS, preferred_element_type=F32)
    return d(w_hi, x_hi) + d(w_hi, x_lo) + d(w_lo, x_hi)


def _log_sigmoid(x):
    return jnp.minimum(x, 0.0) - jnp.log1p(jnp.exp(-jnp.abs(x)))


def _layer_norm_rows(x, g, b):
    mu = jnp.mean(x, axis=-1, keepdims=True)
    xc = x - mu
    var = jnp.mean(xc * xc, axis=-1, keepdims=True)
    return xc * lax.rsqrt(var + EPS) * g + b


class Geometry:
    def __init__(self, n_prompt, prompt_len, n_sample, sample_len):
        self.n_prompt, self.prompt_len = n_prompt, prompt_len
        self.n_sample, self.sample_len = n_sample, sample_len
        self.t_prompt = n_prompt * prompt_len
        self.t_sample = n_sample * sample_len
        self.t = self.t_prompt + self.t_sample
        assert self.t_prompt % TOK_TILE_L == 0 and sample_len % TOK_TILE_L == 0
        assert n_sample + 1 <= COND_ROWS

    def cond_row(self, tile, tile_rows):
        n_p = self.t_prompt // tile_rows
        return jnp.where(tile < n_p, 0, 1 + (tile - n_p) // (self.sample_len // tile_rows))


def _mod_kernel(cond_ref, w_ref, b_ref, o_ref):
    c = cond_ref[...]
    s = c * jax.nn.sigmoid(c)
    o_ref[0, 0] = jnp.dot(s, w_ref[0], precision=HIGHEST, preferred_element_type=F32) + b_ref[0, 0]


def _modulation(cond, w_mod, b_mod):
    out = pl.pallas_call(
        _mod_kernel,
        name="modulation",
        grid=(DEPTH, 6),
        in_specs=[
            pl.BlockSpec((COND_ROWS, D), lambda l, j: (0, 0)),
            pl.BlockSpec((1, D, D), lambda l, j: (l, 0, j)),
            pl.BlockSpec((1, 1, 1, D), lambda l, j: (l, j, 0, 0)),
        ],
        out_specs=pl.BlockSpec((1, 1, COND_ROWS, D), lambda l, j: (l, j, 0, 0)),
        out_shape=jax.ShapeDtypeStruct((DEPTH, 6, COND_ROWS, D), F32),
        compiler_params=_params(("arbitrary", "arbitrary")),
    )(cond, w_mod, b_mod.reshape(DEPTH, 6, 1, D))
    return out.transpose(0, 2, 1, 3)


def _embed_specs(geo, tb):
    n_p = geo.t_prompt // tb
    per_seq = geo.sample_len // tb
    return [pl.BlockSpec((tb, D), lambda i: (jnp.minimum(i, n_p - 1), 0)),
            pl.BlockSpec((tb, D), lambda i: (jnp.maximum(i - n_p, 0), 0)),
            pl.BlockSpec((tb, D), lambda i: (jnp.maximum(i - n_p, 0) % per_seq, 0))]


def _embed_tile(xp_ref, xs_ref, pos_ref, n_prompt_tiles):
    return jnp.where(pl.program_id(0) < n_prompt_tiles, xp_ref[...], xs_ref[...] + pos_ref[...])


def _proj_a_kernel(xp_ref, xs_ref, pos_ref, mod_ref, wq_ref, wkt_ref, wvo_ref, wgt_ref, bgt_ref,
                   q_ref, kt_ref, v_ref, so_ref, gr_ref, *, n_prompt_tiles):
    x = _embed_tile(xp_ref, xs_ref, pos_ref, n_prompt_tiles)
    h = x * (1.0 + mod_ref[1:2, :]) + mod_ref[0:1, :]
    hb = h.astype(BF16)
    q_ref[...] = jnp.dot(hb, wq_ref[...], preferred_element_type=F32).astype(BF16)
    kt = lax.dot_general(wkt_ref[...], hb, NT_DIMS, preferred_element_type=F32)
    kt_ref[...] = (kt * (DK_A ** -0.5)).astype(BF16)
    vo = jnp.dot(hb, wvo_ref[...], preferred_element_type=F32)
    v_ref[...] = vo[:, :V_A].astype(BF16)
    so_ref[...] = jax.nn.sigmoid(vo[:, V_A:]).astype(BF16)
    gr_ref[...] = _dot_nt_x3(wgt_ref[...], h) + bgt_ref[...]


def _proj_a(geo, x, mod_l, w_in, b_gates):
    t = geo.t
    n_gate = 4 * NH_A
    wq = w_in[:, :QK_A].astype(BF16)
    wkt = w_in[:, QK_A:2 * QK_A].T.astype(BF16)
    wvo = w_in[:, 2 * QK_A:2 * QK_A + 2 * V_A].astype(BF16)
    wg = w_in[:, 2 * QK_A + 2 * V_A:]
    bg = b_gates.reshape(n_gate).astype(F32)
    tb = TOK_TILE_L
    full = lambda shape: pl.BlockSpec(shape, lambda i: (0,) * len(shape))
    return pl.pallas_call(
        functools.partial(_proj_a_kernel, n_prompt_tiles=geo.t_prompt // tb),
        name="proj_a",
        grid=(t // tb,),
        in_specs=_embed_specs(geo, tb) + [
            pl.BlockSpec((None, 6, D), lambda i: (geo.cond_row(i, tb), 0, 0)),
            full((D, QK_A)), full((QK_A, D)), full((D, 2 * V_A)), full((n_gate, D)), full((n_gate, 1)),
        ],
        out_specs=[
            pl.BlockSpec((tb, QK_A), lambda i: (i, 0)),
            pl.BlockSpec((QK_A, tb), lambda i: (0, i)),
            pl.BlockSpec((tb, V_A), lambda i: (i, 0)),
            pl.BlockSpec((tb, V_A), lambda i: (i, 0)),
            pl.BlockSpec((n_gate, tb), lambda i: (0, i)),
        ],
        out_shape=[
            jax.ShapeDtypeStruct((t, QK_A), BF16),
            jax.ShapeDtypeStruct((QK_A, t), BF16),
            jax.ShapeDtypeStruct((t, V_A), BF16),
            jax.ShapeDtypeStruct((t, V_A), BF16),
            jax.ShapeDtypeStruct((n_gate, t), F32),
        ],
        compiler_params=_params(("parallel",)),
    )(*x, mod_l, wq, wkt, wvo, wg.T, bg.reshape(n_gate, 1))


def _mlstm_scan_kernel(*refs, chunk, has_state):
    if has_state:
        (q_ref, kt_ref, v_ref, gr_ref, c0_ref, n0_ref, m0_ref,
         h_ref, c_out, n_out, m_out, c_sc, n_sc, m_sc) = refs
    else:
        (q_ref, kt_ref, v_ref, gr_ref,
         h_ref, c_out, n_out, m_out, c_sc, n_sc, m_sc) = refs
    L = chunk
    d = pl.program_id(1)
    c = pl.program_id(2)
    fwd = d == 0

    @pl.when(c == 0)
    def _():
        if has_state:
            c_sc[...] = c0_ref[0, 0]
            n_sc[...] = n0_ref[0, 0]
            m_sc[...] = m0_ref[0, 0]
        else:
            c_sc[...] = jnp.zeros_like(c_sc)
            n_sc[...] = jnp.zeros_like(n_sc)
            m_sc[...] = jnp.zeros_like(m_sc)

    row = lax.broadcasted_iota(jnp.int32, (L, L), 0)
    col = lax.broadcasted_iota(jnp.int32, (L, L), 1)
    sgn = 1 - 2 * d
    causal = (row - col) * sgn >= 0
    tri_t = ((col - row) * sgn >= 0).astype(BF16)

    gr = gr_ref[...]
    br_all = _dot3_r(_log_sigmoid(gr), tri_t)
    bc_all = jnp.concatenate([br_all, jnp.zeros((LANES - br_all.shape[0], L), F32)], axis=0).T
    ones_blk = (lax.broadcasted_iota(jnp.int32, (L, LANES), 1) == 0).astype(BF16)

    def gate_row(direction, gate, head):
        return (direction * 2 + gate) * NH_A + head

    for h in range(NH_A):
        ff, fb = gate_row(0, 1, h), gate_row(1, 1, h)
        gi, gb = gate_row(0, 0, h), gate_row(1, 0, h)
        b_c = jnp.where(fwd, bc_all[:, ff:ff + 1], bc_all[:, fb:fb + 1])
        b_r = jnp.where(fwd, br_all[ff:ff + 1, :], br_all[fb:fb + 1, :])
        i_r = jnp.where(fwd, gr[gi:gi + 1, :], gr[gb:gb + 1, :])
        bl = jnp.where(fwd, b_r[:, L - 1:L], b_r[:, 0:1])
        q = q_ref[:, h * DK_A:(h + 1) * DK_A]
        kt = kt_ref[h * DK_A:(h + 1) * DK_A, :]
        v = v_ref[:, h * DV_A:(h + 1) * DV_A]
        m = m_sc[h:h + 1, 0:1]
        cst = c_sc[h]
        nst = n_sc[h]

        a_r = i_r - b_r
        logd = jnp.where(causal, b_c + a_r, -jnp.inf)
        inter = b_c + m
        m_t = jnp.maximum(inter, jnp.max(logd, axis=1, keepdims=True))
        dmat = jnp.exp(logd - m_t)
        e_int = jnp.exp(inter - m_t)
        s = (jnp.dot(q, kt, preferred_element_type=F32) * dmat).astype(BF16)
        num = (jnp.dot(s, v, preferred_element_type=F32)
               + e_int * jnp.dot(q, cst.astype(BF16), preferred_element_type=F32))
        den = (jnp.dot(s, ones_blk, preferred_element_type=F32)
               + e_int * jnp.dot(q, nst.astype(BF16), preferred_element_type=F32))[:, 0:1]
        h_ref[:, h * DV_A:(h + 1) * DV_A] = (num / jnp.maximum(jnp.abs(den), jnp.exp(-m_t))).astype(BF16)

        logw = bl + a_r
        m_new = jnp.maximum(bl + m, jnp.max(logw, axis=1, keepdims=True))
        w = jnp.exp(logw - m_new)
        decay = jnp.exp(bl + m - m_new)
        kw = (kt.astype(F32) * w).astype(BF16)
        c_sc[h] = decay * cst + jnp.dot(kw, v, preferred_element_type=F32)
        n_sc[h] = decay * nst + jnp.dot(kw, ones_blk, preferred_element_type=F32)
        m_sc[h:h + 1, :] = jnp.broadcast_to(m_new, (1, LANES))

    @pl.when(c == pl.num_programs(2) - 1)
    def _():
        c_out[0, 0] = c_sc[...]
        n_out[0, 0] = n_sc[...]
        m_out[0, 0] = m_sc[...]


def _mlstm_scan(q, kt, v, gr, *, row0, n_seq, seq_len, state=None):
    L = min(CHUNK_A, seq_len)
    nc = seq_len // L
    blk0 = row0 // L

    def loc_blk(b, d, c):
        return b * nc + c + d * (nc - 1 - 2 * c)

    def tok_blk(b, d, c):
        return blk0 + loc_blk(b, d, c)

    in_specs = [
        pl.BlockSpec((L, QK_A), lambda b, d, c: (tok_blk(b, d, c), 0)),
        pl.BlockSpec((QK_A, L), lambda b, d, c: (0, tok_blk(b, d, c))),
        pl.BlockSpec((L, V_A), lambda b, d, c: (tok_blk(b, d, c), 0)),
        pl.BlockSpec((4 * NH_A, L), lambda b, d, c: (0, tok_blk(b, d, c))),
    ]
    args = [q, kt, v, gr]
    if state is not None:
        in_specs += [
            pl.BlockSpec((1, 1, NH_A, DK_A, DV_A), lambda b, d, c: (b, d, 0, 0, 0)),
            pl.BlockSpec((1, 1, NH_A, DK_A, LANES), lambda b, d, c: (b, d, 0, 0, 0)),
            pl.BlockSpec((1, 1, SUBLANES, LANES), lambda b, d, c: (b, d, 0, 0)),
        ]
        args += list(state)
    return pl.pallas_call(
        functools.partial(_mlstm_scan_kernel, chunk=L, has_state=state is not None),
        name="mlstm_scan_seeded" if state is not None else "mlstm_scan",
        grid=(n_seq, 2, nc),
        in_specs=in_specs,
        out_specs=[
            pl.BlockSpec((None, L, V_A), lambda b, d, c: (d, loc_blk(b, d, c), 0)),
            pl.BlockSpec((1, 1, NH_A, DK_A, DV_A), lambda b, d, c: (b, d, 0, 0, 0)),
            pl.BlockSpec((1, 1, NH_A, DK_A, LANES), lambda b, d, c: (b, d, 0, 0, 0)),
            pl.BlockSpec((1, 1, SUBLANES, LANES), lambda b, d, c: (b, d, 0, 0)),
        ],
        out_shape=[
            jax.ShapeDtypeStruct((2, n_seq * seq_len, V_A), BF16),
            jax.ShapeDtypeStruct((n_seq, 2, NH_A, DK_A, DV_A), F32),
            jax.ShapeDtypeStruct((n_seq, 2, NH_A, DK_A, LANES), F32),
            jax.ShapeDtypeStruct((n_seq, 2, SUBLANES, LANES), F32),
        ],
        scratch_shapes=[
            pltpu.VMEM((NH_A, DK_A, DV_A), F32),
            pltpu.VMEM((NH_A, DK_A, LANES), F32),
            pltpu.VMEM((SUBLANES, LANES), F32),
        ],
        compiler_params=_params(("parallel", "parallel", "arbitrary")),
    )(*args)


def _out_a_kernel(hp_ref, hs_ref, so_ref, nw_ref, w_ref, xp_ref, xs_ref, pos_ref, mod_ref, lg_ref, lb_ref,
                  wrt_ref, eb_ref, o_ref, *route_refs, n_prompt_tiles):
    is_prompt = pl.program_id(0) < n_prompt_tiles
    x = _embed_tile(xp_ref, xs_ref, pos_ref, n_prompt_tiles)
    y = jnp.where(is_prompt, hp_ref[0].astype(F32) + hp_ref[1].astype(F32),
                  hs_ref[0].astype(F32) + hs_ref[1].astype(F32))
    parts = []
    for h in range(NH_A):
        yh = y[:, h * DV_A:(h + 1) * DV_A]
        mu = jnp.mean(yh, axis=-1, keepdims=True)
        yc = yh - mu
        var = jnp.mean(yc * yc, axis=-1, keepdims=True)
        parts.append(yc * lax.rsqrt(var + EPS))
    yn = jnp.concatenate(parts, axis=-1) * nw_ref[...] * so_ref[...].astype(F32)
    out = jnp.dot(yn.astype(BF16), w_ref[...], preferred_element_type=F32)
    x1 = _layer_norm_rows(ALPHA * x + mod_ref[2:3, :] * out, lg_ref[...], lb_ref[...])
    o_ref[...] = x1
    _route_tile(x1, mod_ref, wrt_ref, eb_ref, *route_refs)


def _out_a(geo, h_prompt, h_sample, so, norm_w, w_out, x, mod_l, ln_g, ln_b, w_router, e_bias):
    t = geo.t
    tb = TOK_TILE_L
    n_p = geo.t_prompt // tb
    full = lambda shape: pl.BlockSpec(shape, lambda i: (0,) * len(shape))
    rio = _RouterIO(t, tb, w_router, e_bias)
    outs = pl.pallas_call(
        functools.partial(_out_a_kernel, n_prompt_tiles=n_p),
        name="out_a",
        grid=(t // tb,),
        in_specs=[
            pl.BlockSpec((2, tb, V_A), lambda i: (0, jnp.minimum(i, n_p - 1), 0)),
            pl.BlockSpec((2, tb, V_A), lambda i: (0, jnp.maximum(i - n_p, 0), 0)),
            pl.BlockSpec((tb, V_A), lambda i: (i, 0)),
            full((1, V_A)), full((V_A, D)),
        ] + _embed_specs(geo, tb) + [
            pl.BlockSpec((None, 6, D), lambda i: (geo.cond_row(i, tb), 0, 0)),
            full((1, D)), full((1, D)),
        ] + rio.in_specs,
        out_specs=[pl.BlockSpec((tb, D), lambda i: (i, 0))] + rio.out_specs,
        out_shape=[jax.ShapeDtypeStruct((t, D), F32)] + rio.out_shape,
        scratch_shapes=rio.scratch,
        compiler_params=_params(("arbitrary",)),
    )(h_prompt, h_sample, so, norm_w.reshape(1, V_A).astype(F32), w_out.astype(BF16), *x, mod_l,
      ln_g.reshape(1, D), ln_b.reshape(1, D), *rio.inputs)
    return outs[0], _RouterIO.unpack(outs[1:])


def _proj_b_kernel(x_ref, mod_ref, w_ref, q_ref, pre_ref, v_ref, sg_ref):
    h = x_ref[...] * (1.0 + mod_ref[1:2, :]) + mod_ref[0:1, :]
    z = jnp.dot(h.astype(BF16), w_ref[...], preferred_element_type=F32)
    for hd in range(NH_B):
        lo = hd * DK_B
        qh = z[:, lo:lo + DK_B]
        q_ref[hd] = qh * jax.nn.sigmoid(qh)
        pre_ref[0, hd] = z[:, D + lo:D + lo + DK_B]
        pre_ref[1, hd] = z[:, 2 * D + lo:2 * D + lo + DK_B]
        v_ref[hd] = z[:, 3 * D + lo:3 * D + lo + DK_B].astype(BF16)
    g = z[:, 4 * D:]
    sg_ref[...] = (g * jax.nn.sigmoid(g)).astype(BF16)


def _proj_b(geo, x, mod_l, w_in):
    t = geo.t
    tb = TOK_TILE
    return pl.pallas_call(
        _proj_b_kernel,
        name="proj_b",
        grid=(t // tb,),
        in_specs=[
            pl.BlockSpec((tb, D), lambda i: (i, 0)),
            pl.BlockSpec((None, 6, D), lambda i: (geo.cond_row(i, tb), 0, 0)),
            pl.BlockSpec((D, 5 * D), lambda i: (0, 0)),
        ],
        out_specs=[
            pl.BlockSpec((NH_B, tb, DK_B), lambda i: (0, i, 0)),
            pl.BlockSpec((2, NH_B, tb, DK_B), lambda i: (0, 0, i, 0)),
            pl.BlockSpec((NH_B, tb, DK_B), lambda i: (0, i, 0)),
            pl.BlockSpec((tb, D), lambda i: (i, 0)),
        ],
        out_shape=[
            jax.ShapeDtypeStruct((NH_B, t, DK_B), F32),
            jax.ShapeDtypeStruct((2, NH_B, t, DK_B), F32),
            jax.ShapeDtypeStruct((NH_B, t, DK_B), BF16),
            jax.ShapeDtypeStruct((t, D), BF16),
        ],
        compiler_params=_params(("parallel",)),
    )(x, mod_l, w_in.astype(BF16))


CHUNK_B = 256
BAND = SUBLANES // 2
TN_DIMS = (((0,), (0,)), ((), ()))


def _hgrn_head(q, pre, lbv, v_bf, st, fwd):
    L = q.shape[0]
    sg = jax.nn.sigmoid(pre)
    f = lbv + (1.0 - lbv) * sg
    lf = jnp.log(f)
    kk = (1.0 - lbv) * (1.0 - sg)
    row = lax.broadcasted_iota(jnp.int32, (L, L), 0)
    col = lax.broadcasted_iota(jnp.int32, (L, L), 1)
    tri = ((row >= col) if fwd else (row <= col)).astype(BF16)
    b = _dot3(tri, lf)
    tpos = lax.broadcasted_iota(jnp.int32, (L, DK_B), 0)
    blk_bits = row ^ col
    lag = jnp.where(blk_bits < BAND, (row - col) if fwd else (col - row), -1)

    step = 1 if fwd else L - 1
    att = jnp.where(lag == 0, jnp.sum(q * kk, axis=1, keepdims=True), 0.0)
    f_r, kk_r, g = f, kk, f
    for dl in range(1, BAND):
        if dl > 1:
            f_r = pltpu.roll(f_r, step, 0)
            g = g * f_r
        kk_r = pltpu.roll(kk_r, step, 0)
        att = jnp.where(lag == dl, jnp.sum(q * kk_r * g, axis=1, keepdims=True), att)

    w = BAND
    while w < L:
        nb = L // (2 * w)
        b3 = b.reshape(nb, 2 * w, DK_B)
        edge = (b3[:, w - 1:w, :] if fwd else b3[:, w:w + 1, :])
        bmid = jnp.broadcast_to(edge, (nb, 2 * w, DK_B)).reshape(L, DK_B)
        second = (tpos & w) != 0
        t_side = second if fwd else jnp.logical_not(second)
        e = jnp.exp(jnp.where(t_side, b - bmid, bmid - b))
        qt = jnp.where(t_side, q * e, 0.0).astype(BF16)
        ks = jnp.where(t_side, 0.0, kk * e).astype(BF16)
        a = lax.dot_general(qt, ks, NT_DIMS, preferred_element_type=F32)
        att = att + jnp.where(blk_bits < 2 * w, a, 0.0)
        w *= 2
    o = jnp.dot(att.astype(BF16), v_bf, preferred_element_type=F32)

    bl = b[L - 1:L, :] if fwd else b[0:1, :]
    o = o + lax.dot_general((q * jnp.exp(b)).astype(BF16), st.astype(BF16), NT_DIMS, preferred_element_type=F32)
    kd = (kk * jnp.exp(bl - b)).astype(BF16)
    st_new = jnp.exp(bl) * st + lax.dot_general(v_bf, kd, TN_DIMS, preferred_element_type=F32)
    return o, st_new


def _hgrn_scan_kernel(*refs, has_state):
    if has_state:
        q_ref, pre_ref, v_ref, lb_ref, s0_ref, o_ref, s_out, st_sc = refs
    else:
        q_ref, pre_ref, v_ref, lb_ref, o_ref, s_out, st_sc = refs
    d = pl.program_id(1)
    c = pl.program_id(2)

    @pl.when(c == 0)
    def _():
        if has_state:
            for hd in range(NH_B):
                st_sc[hd] = s0_ref[0, 0, hd].T
        else:
            st_sc[...] = jnp.zeros_like(st_sc)

    def run(fwd):
        def head(hd, carry):
            o, st_new = _hgrn_head(q_ref[hd], pre_ref[hd], lb_ref[hd], v_ref[hd], st_sc[hd], fwd)
            o_ref[hd] = o.astype(BF16)
            st_sc[hd] = st_new
            return carry
        lax.fori_loop(0, NH_B, head, 0, unroll=8)

    @pl.when(d == 0)
    def _():
        run(True)

    @pl.when(d == 1)
    def _():
        run(False)

    @pl.when(c == pl.num_programs(2) - 1)
    def _():
        for hd in range(NH_B):
            s_out[0, 0, hd] = st_sc[hd].T


def _hgrn_scan(q, pre, v, lbd, *, row0, n_seq, seq_len, state=None):
    L = CHUNK_B
    nc = seq_len // L
    blk0 = row0 // L

    def loc_blk(b, d, c):
        return b * nc + c + d * (nc - 1 - 2 * c)

    def tok_blk(b, d, c):
        return blk0 + loc_blk(b, d, c)

    in_specs = [
        pl.BlockSpec((NH_B, L, DK_B), lambda b, d, c: (0, tok_blk(b, d, c), 0)),
        pl.BlockSpec((None, NH_B, L, DK_B), lambda b, d, c: (d, 0, tok_blk(b, d, c), 0)),
        pl.BlockSpec((NH_B, L, DK_B), lambda b, d, c: (0, tok_blk(b, d, c), 0)),
        pl.BlockSpec((None, NH_B, 1, DK_B), lambda b, d, c: (d, 0, 0, 0)),
    ]
    args = [q, pre, v, lbd]
    if state is not None:
        in_specs.append(pl.BlockSpec((1, 1, NH_B, DK_B, DK_B), lambda b, d, c: (b, d, 0, 0, 0)))
        args.append(state)
    return pl.pallas_call(
        functools.partial(_hgrn_scan_kernel, has_state=state is not None),
        name="hgrn_scan_seeded" if state is not None else "hgrn_scan",
        grid=(n_seq, 2, nc),
        in_specs=in_specs,
        out_specs=[
            pl.BlockSpec((None, NH_B, L, DK_B), lambda b, d, c: (d, 0, loc_blk(b, d, c), 0)),
            pl.BlockSpec((1, 1, NH_B, DK_B, DK_B), lambda b, d, c: (b, d, 0, 0, 0)),
        ],
        out_shape=[
            jax.ShapeDtypeStruct((2, NH_B, n_seq * seq_len, DK_B), BF16),
            jax.ShapeDtypeStruct((n_seq, 2, NH_B, DK_B, DK_B), F32),
        ],
        scratch_shapes=[pltpu.VMEM((NH_B, DK_B, DK_B), F32)],
        compiler_params=_params(("parallel", "parallel", "arbitrary")),
    )(*args)


def _out_b_kernel(op_ref, os_ref, sg_ref, nw_ref, w_ref, x_ref, mod_ref, lg_ref, lb_ref, wrt_ref, eb_ref,
                  out_ref, *route_refs, n_prompt_tiles):
    is_prompt = pl.program_id(0) < n_prompt_tiles
    parts = []
    for hd in range(NH_B):
        y = jnp.where(is_prompt, op_ref[0, hd].astype(F32) + op_ref[1, hd].astype(F32),
                      os_ref[0, hd].astype(F32) + os_ref[1, hd].astype(F32))
        parts.append(y * lax.rsqrt(jnp.mean(y * y, axis=-1, keepdims=True) + EPS))
    yn = jnp.concatenate(parts, axis=-1) * nw_ref[...] * sg_ref[...].astype(F32)
    out = jnp.dot(yn.astype(BF16), w_ref[...], preferred_element_type=F32)
    x1 = _layer_norm_rows(ALPHA * x_ref[...] + mod_ref[2:3, :] * out, lg_ref[...], lb_ref[...])
    out_ref[...] = x1
    _route_tile(x1, mod_ref, wrt_ref, eb_ref, *route_refs)


def _out_b(geo, o_prompt, o_sample, sg, norm_w, w_out, x, mod_l, ln_g, ln_b, w_router, e_bias):
    t = geo.t
    tb = TOK_TILE_L
    n_p = geo.t_prompt // tb
    full = lambda shape: pl.BlockSpec(shape, lambda i: (0,) * len(shape))
    rio = _RouterIO(t, tb, w_router, e_bias)
    outs = pl.pallas_call(
        functools.partial(_out_b_kernel, n_prompt_tiles=n_p),
        name="out_b",
        grid=(t // tb,),
        in_specs=[
            pl.BlockSpec((2, NH_B, tb, DK_B), lambda i: (0, 0, jnp.minimum(i, n_p - 1), 0)),
            pl.BlockSpec((2, NH_B, tb, DK_B), lambda i: (0, 0, jnp.maximum(i - n_p, 0), 0)),
            pl.BlockSpec((tb, D), lambda i: (i, 0)),
            full((1, D)), full((D, D)),
            pl.BlockSpec((tb, D), lambda i: (i, 0)),
            pl.BlockSpec((None, 6, D), lambda i: (geo.cond_row(i, tb), 0, 0)),
            full((1, D)), full((1, D)),
        ] + rio.in_specs,
        out_specs=[pl.BlockSpec((tb, D), lambda i: (i, 0))] + rio.out_specs,
        out_shape=[jax.ShapeDtypeStruct((t, D), F32)] + rio.out_shape,
        scratch_shapes=rio.scratch,
        compiler_params=_params(("arbitrary",)),
    )(o_prompt, o_sample, sg, norm_w.reshape(1, D).astype(F32), w_out.astype(BF16), x, mod_l,
      ln_g.reshape(1, D), ln_b.reshape(1, D), *rio.inputs)
    return outs[0], _RouterIO.unpack(outs[1:])


MOE_BLK = 1024
U32 = jnp.uint32
ROW_WORDS = D // 2
CHUNK_W = 256
ROW_CHUNKS = ROW_WORDS // CHUNK_W
SC_WINDOW = 128


def _pack_rows(x):
    return pltpu.pack_elementwise([x[:, :ROW_WORDS], x[:, ROW_WORDS:]], packed_dtype=BF16)


def _unpack_rows(words):
    return jnp.concatenate([pltpu.unpack_elementwise(words, index=i, packed_dtype=BF16, unpacked_dtype=F32)
                            for i in range(2)], axis=1)


def _store_chunks(chunk_ref, x):
    words = _pack_rows(x)
    for c in range(ROW_CHUNKS):
        chunk_ref(c)[...] = words[:, c * CHUNK_W:(c + 1) * CHUNK_W]


def _load_chunks(chunk_ref, valid_rows=None):
    words = jnp.concatenate([chunk_ref(c)[...] for c in range(ROW_CHUNKS)], axis=1)
    if valid_rows is not None:
        row = lax.broadcasted_iota(jnp.int32, (words.shape[0], 1), 0)
        words = jnp.where(row < valid_rows, words, jnp.uint32(0))
    return _unpack_rows(words)


def _first_index(hit, iota, size, axis):
    return jnp.min(jnp.where(hit, iota, size), axis=axis, keepdims=True)


def _route_tile(x, mod_ref, wrt_ref, eb_ref, e_ref, w_ref, r_ref, cnt_ref, h_ref, cnt_sc):
    i = pl.program_id(0)
    tb = x.shape[0]

    @pl.when(i == 0)
    def _():
        cnt_sc[...] = jnp.zeros_like(cnt_sc)

    h = x * (1.0 + mod_ref[4:5, :]) + mod_ref[3:4, :]
    _store_chunks(lambda c: h_ref.at[c], h)
    logits = _dot_nt_x3(wrt_ref[...], h)
    scores = jax.nn.sigmoid(logits)
    sel = scores + eb_ref[...]

    g3 = sel.reshape(N_GROUPS, GROUP_SIZE, tb)
    io3 = lax.broadcasted_iota(jnp.int32, g3.shape, 1)
    m1 = jnp.max(g3, axis=1, keepdims=True)
    first = _first_index(g3 == m1, io3, GROUP_SIZE, 1)
    m2 = jnp.max(jnp.where(io3 == first, -jnp.inf, g3), axis=1, keepdims=True)
    gscore = (m1 + m2).reshape(N_GROUPS, tb)

    iog = lax.broadcasted_iota(jnp.int32, gscore.shape, 0)
    gmask = jnp.zeros(gscore.shape, F32)
    for _ in range(TOPK_GROUPS):
        gm = jnp.max(gscore, axis=0, keepdims=True)
        pick = iog == _first_index(gscore == gm, iog, N_GROUPS, 0)
        gmask = jnp.where(pick, 1.0, gmask)
        gscore = jnp.where(pick, -jnp.inf, gscore)
    emask = jnp.broadcast_to(gmask.reshape(N_GROUPS, 1, tb), (N_GROUPS, GROUP_SIZE, tb)).reshape(N_EXPERTS, tb)
    cand = jnp.where(emask > 0.0, sel, -jnp.inf)

    ioe = lax.broadcasted_iota(jnp.int32, cand.shape, 0)
    picks, wts = [], []
    onehot = jnp.zeros(cand.shape, F32)
    for _ in range(TOP_K):
        cm = jnp.max(cand, axis=0, keepdims=True)
        idx = _first_index(cand == cm, ioe, N_EXPERTS, 0)
        pick = ioe == idx
        picks.append(pick)
        wts.append(jnp.sum(jnp.where(pick, scores, 0.0), axis=0, keepdims=True))
        onehot = onehot + pick.astype(F32)
        cand = jnp.where(pick, -jnp.inf, cand)
        e_ref[pl.ds(len(picks) - 1, 1), :] = idx
    wsum = wts[0]
    for wk in wts[1:]:
        wsum = wsum + wk
    for k in range(TOP_K):
        w_ref[pl.ds(k, 1), :] = wts[k] / wsum * ROUTED_SCALE

    r_io = lax.broadcasted_iota(jnp.int32, (tb, tb), 0)
    c_io = lax.broadcasted_iota(jnp.int32, (tb, tb), 1)
    before = (r_io < c_io).astype(BF16)
    rank = cnt_sc[:, 0:1] + jnp.dot(onehot.astype(BF16), before, preferred_element_type=F32)
    for k in range(TOP_K):
        r_ref[pl.ds(k, 1), :] = jnp.sum(jnp.where(picks[k], rank, 0.0), axis=0, keepdims=True).astype(jnp.int32)
    cnt_sc[...] = cnt_sc[...] + jnp.sum(onehot, axis=1, keepdims=True)
    cnt_ref[...] = cnt_sc[...]


class _RouterIO:
    def __init__(self, t, tb, w_router, e_bias):
        full = lambda shape: pl.BlockSpec(shape, lambda i: (0,) * len(shape))
        self.inputs = [w_router.T.astype(F32), e_bias.reshape(N_EXPERTS, 1).astype(F32)]
        self.in_specs = [full((N_EXPERTS, D)), full((N_EXPERTS, 1))]
        self.out_specs = [
            pl.BlockSpec((TOP_K, tb), lambda i: (0, i)),
            pl.BlockSpec((TOP_K, tb), lambda i: (0, i)),
            pl.BlockSpec((TOP_K, tb), lambda i: (0, i)),
            full((N_EXPERTS, LANES)),
            pl.BlockSpec((ROW_CHUNKS, tb, CHUNK_W), lambda i: (0, i, 0)),
        ]
        self.out_shape = [
            jax.ShapeDtypeStruct((TOP_K, t), jnp.int32),
            jax.ShapeDtypeStruct((TOP_K, t), F32),
            jax.ShapeDtypeStruct((TOP_K, t), jnp.int32),
            jax.ShapeDtypeStruct((N_EXPERTS, LANES), F32),
            jax.ShapeDtypeStruct((ROW_CHUNKS, t, CHUNK_W), U32),
        ]
        self.scratch = [pltpu.VMEM((N_EXPERTS, LANES), F32)]

    @staticmethod
    def unpack(outs):
        e, w, r, cnt, h = outs
        return e, w, r, cnt[:, 0].astype(jnp.int32), h


def _slot_kernel(pstart_ref, e_ref, r_ref, o_ref):
    e = e_ref[...]
    slot = r_ref[...]
    for x in range(N_EXPERTS):
        slot = slot + jnp.where(e == x, pstart_ref[x], 0)
    o_ref[...] = slot


def _slots(geo, pstart, top_e, rank):
    tb = math.gcd(SLOT_TILE, geo.t)
    return pl.pallas_call(
        _slot_kernel,
        name="slots",
        grid_spec=pltpu.PrefetchScalarGridSpec(
            num_scalar_prefetch=1,
            grid=(geo.t // tb,),
            in_specs=[pl.BlockSpec((TOP_K, tb), lambda i, p: (0, i)),
                      pl.BlockSpec((TOP_K, tb), lambda i, p: (0, i))],
            out_specs=pl.BlockSpec((TOP_K, tb), lambda i, p: (0, i)),
        ),
        out_shape=jax.ShapeDtypeStruct((TOP_K, geo.t), jnp.int32),
        compiler_params=_params(("parallel",)),
    )(pstart, top_e, rank)


def _block_meta_kernel(pstart_ref, counts_ref, pend_ref, e_ref, v_ref):
    row0 = lax.broadcasted_iota(jnp.int32, e_ref.shape, 1) * MOE_BLK
    blk_e = jnp.zeros(e_ref.shape, jnp.int32)
    for x in range(N_EXPERTS):
        blk_e = blk_e + jnp.where(pend_ref[x] <= row0, 1, 0)
    blk_e = jnp.minimum(blk_e, N_EXPERTS - 1)
    last = jnp.zeros(e_ref.shape, jnp.int32)
    for x in range(N_EXPERTS):
        last = last + jnp.where(blk_e == x, pstart_ref[x] + counts_ref[x], 0)
    e_ref[...] = blk_e
    v_ref[...] = jnp.clip(last - row0, 0, MOE_BLK)


def _block_meta(pstart, counts, pend, n_blocks):
    e, v = pl.pallas_call(
        _block_meta_kernel,
        name="block_meta",
        grid_spec=pltpu.PrefetchScalarGridSpec(
            num_scalar_prefetch=3,
            grid=(1,),
            in_specs=[],
            out_specs=[pl.BlockSpec((1, n_blocks), lambda i, a, b, c: (0, 0)),
                       pl.BlockSpec((1, n_blocks), lambda i, a, b, c: (0, 0))],
        ),
        out_shape=[jax.ShapeDtypeStruct((1, n_blocks), jnp.int32), jax.ShapeDtypeStruct((1, n_blocks), jnp.int32)],
        compiler_params=_params(("arbitrary",)),
    )(pstart, counts, pend)
    return e[0], v[0]


def _sc_mesh():
    return plsc.VectorSubcoreMesh(core_axis_name="core", subcore_axis_name="subcore")


def _sc_scatter(rows, idx, n_out, copies):
    n_src = rows.shape[0]
    n_idx = idx.shape[0]
    groups = ROW_CHUNKS
    win_per_group = n_src // groups // SC_WINDOW

    def idx_block(w, k):
        return (0, ((w // win_per_group) * copies + k) * win_per_group + w % win_per_group)

    @pl.kernel(out_type=jax.ShapeDtypeStruct((n_out, CHUNK_W), rows.dtype), mesh=_sc_mesh(), scratch_types=[],
               name="sc_dispatch")
    def scatter(x_hbm, i_hbm, o_hbm):
        def body(x_vmem, *i_vmems):
            for i_vmem in i_vmems:
                pltpu.sync_copy(x_vmem, o_hbm.at[i_vmem.at[0]])

        pltpu.emit_pipeline(
            body,
            grid=(n_src // SC_WINDOW,),
            in_specs=[pl.BlockSpec((SC_WINDOW, CHUNK_W), index_map=lambda w: (w, 0))]
            + [pl.BlockSpec((1, SC_WINDOW), index_map=functools.partial(idx_block, k=k)) for k in range(copies)],
            out_specs=[],
            core_axis_name=("core", "subcore"),
            dimension_semantics=(pltpu.PARALLEL,),
        )(x_hbm, *([i_hbm] * copies))

    return scatter(rows, idx.reshape(1, n_idx))


def _sc_gather(table, idx):
    n_idx = idx.shape[0]

    @pl.kernel(out_type=jax.ShapeDtypeStruct((n_idx, CHUNK_W), table.dtype), mesh=_sc_mesh(),
               name="sc_combine_gather")
    def gather(t_hbm, i_hbm, o_hbm):
        def body(i_vmem, o_vmem):
            pltpu.sync_copy(t_hbm.at[i_vmem.at[0]], o_vmem)

        pltpu.emit_pipeline(
            body,
            grid=(n_idx // SC_WINDOW,),
            in_specs=[pl.BlockSpec((1, SC_WINDOW), index_map=lambda w: (0, w))],
            out_specs=[pl.BlockSpec((SC_WINDOW, CHUNK_W), index_map=lambda w: (w, 0))],
            core_axis_name=("core", "subcore"),
            dimension_semantics=(pltpu.PARALLEL,),
        )(i_hbm, o_hbm)

    return gather(table, idx.reshape(1, n_idx))


def _ffn_kernel(blk_e_ref, blk_valid_ref, n_used_ref, xs_ref, wg_ref, wu_ref, wd_ref, y_ref, wg_sc, wu_sc, wd_sc):
    b = pl.program_id(0)
    used = b < n_used_ref[0]
    new_expert = (b == 0) | (blk_e_ref[b] != blk_e_ref[jnp.maximum(b - 1, 0)])

    @pl.when(used & new_expert)
    def _():
        wg_sc[...] = wg_ref[...].astype(BF16)
        wu_sc[...] = wu_ref[...].astype(BF16)
        wd_sc[...] = wd_ref[...].astype(BF16)

    @pl.when(used)
    def _():
        x = _load_chunks(lambda c: xs_ref.at[c], valid_rows=blk_valid_ref[b]).astype(BF16)
        g = jnp.dot(x, wg_sc[...], preferred_element_type=F32)
        u = jnp.dot(x, wu_sc[...], preferred_element_type=F32)
        hmid = (g * jax.nn.sigmoid(g) * u).astype(BF16)
        _store_chunks(lambda c: y_ref.at[c], jnp.dot(hmid, wd_sc[...], preferred_element_type=F32))

    @pl.when(jnp.logical_not(used))
    def _():
        y_ref[...] = jnp.zeros_like(y_ref)


def _ffn(xs, blk_e, blk_valid, n_used, layer, wg, wu, wd, n_blocks):
    def blk(b, be, bv, nu):
        return jnp.maximum(jnp.minimum(b, nu[0] - 1), 0)

    def w_idx(b, be, bv, nu):
        return (layer, be[blk(b, be, bv, nu)], 0, 0)

    return pl.pallas_call(
        _ffn_kernel,
        name="expert_ffn",
        grid_spec=pltpu.PrefetchScalarGridSpec(
            num_scalar_prefetch=3,
            grid=(n_blocks,),
            in_specs=[
                pl.BlockSpec((ROW_CHUNKS, MOE_BLK, CHUNK_W), lambda b, be, bv, nu: (0, blk(b, be, bv, nu), 0)),
                pl.BlockSpec((None, None, D, D_EXPERT), w_idx),
                pl.BlockSpec((None, None, D, D_EXPERT), w_idx),
                pl.BlockSpec((None, None, D_EXPERT, D), w_idx),
            ],
            out_specs=pl.BlockSpec((ROW_CHUNKS, MOE_BLK, CHUNK_W), lambda b, be, bv, nu: (0, b, 0)),
            scratch_shapes=[pltpu.VMEM((D, D_EXPERT), BF16), pltpu.VMEM((D, D_EXPERT), BF16),
                            pltpu.VMEM((D_EXPERT, D), BF16)],
        ),
        out_shape=jax.ShapeDtypeStruct(xs.shape, U32),
        compiler_params=_params(("arbitrary",)),
    )(blk_e, blk_valid, n_used, xs, wg, wu, wd)


def _combine_kernel(x_ref, mod_ref, wt_ref, y_ref, sg_ref, su_ref, sd_ref, lg_ref, lb_ref, *o_refs, n_prompt_tiles):
    x = x_ref[...]
    hb = (x * (1.0 + mod_ref[4:5, :]) + mod_ref[3:4, :]).astype(BF16)
    g = jnp.dot(hb, sg_ref[...], preferred_element_type=F32)
    u = jnp.dot(hb, su_ref[...], preferred_element_type=F32)
    ff = jnp.dot((g * jax.nn.sigmoid(g) * u).astype(BF16), sd_ref[...], preferred_element_type=F32)
    for k in range(TOP_K):
        ff = ff + _load_chunks(lambda c: y_ref.at[c, k]) * wt_ref[:, k:k + 1]
    out = _layer_norm_rows(ALPHA * x + mod_ref[5:6, :] * ff, lg_ref[...], lb_ref[...])
    if len(o_refs) == 1:
        o_refs[0][...] = out
    else:
        is_prompt = pl.program_id(0) < n_prompt_tiles

        @pl.when(is_prompt)
        def _():
            o_refs[0][...] = out

        @pl.when(jnp.logical_not(is_prompt))
        def _():
            o_refs[1][...] = out


def _combine(geo, x, mod_l, wt, ytok, sg, su, sd, ln_g, ln_b, split=False):
    tb = TOK_TILE_L
    n_p = geo.t_prompt // tb
    full = lambda shape: pl.BlockSpec(shape, lambda i: (0,) * len(shape))
    if split:
        out_specs = [pl.BlockSpec((tb, D), lambda i: (jnp.minimum(i, n_p - 1), 0)),
                     pl.BlockSpec((tb, D), lambda i: (jnp.maximum(i - n_p, 0), 0))]
        out_shape = [jax.ShapeDtypeStruct((geo.t_prompt, D), F32), jax.ShapeDtypeStruct((geo.t_sample, D), F32)]
    else:
        out_specs = pl.BlockSpec((tb, D), lambda i: (i, 0))
        out_shape = jax.ShapeDtypeStruct((geo.t, D), F32)
    return pl.pallas_call(
        functools.partial(_combine_kernel, n_prompt_tiles=n_p),
        name="combine",
        grid=(geo.t // tb,),
        in_specs=[
            pl.BlockSpec((tb, D), lambda i: (i, 0)),
            pl.BlockSpec((None, 6, D), lambda i: (geo.cond_row(i, tb), 0, 0)),
            pl.BlockSpec((tb, TOP_K), lambda i: (i, 0)),
            pl.BlockSpec((ROW_CHUNKS, TOP_K, tb, CHUNK_W), lambda i: (0, 0, i, 0)),
            full((D, D_EXPERT)), full((D, D_EXPERT)), full((D_EXPERT, D)), full((1, D)), full((1, D)),
        ],
        out_specs=out_specs,
        out_shape=out_shape,
        compiler_params=_params(("arbitrary",)),
    )(x, mod_l, wt, ytok, sg.astype(BF16), su.astype(BF16), sd.astype(BF16),
      ln_g.reshape(1, D), ln_b.reshape(1, D))


def _moe_layer(geo, x, routing, mod_l, layer, wg, wu, wd, sg, su, sd, ln_g, ln_b, split=False):
    t = geo.t
    top_e, w, rank, counts, h = routing
    n_blocks = (t * TOP_K) // MOE_BLK + N_EXPERTS
    n_rows = n_blocks * MOE_BLK
    padded = (counts + MOE_BLK - 1) // MOE_BLK * MOE_BLK
    pend = jnp.cumsum(padded)
    pstart = (pend - padded).astype(jnp.int32)
    blk_e, blk_valid = _block_meta(pstart, counts, pend.astype(jnp.int32), n_blocks)
    n_used = (pend[-1:] // MOE_BLK).astype(jnp.int32)
    slots = _slots(geo, pstart, top_e, rank)
    idx = (slots.reshape(1, TOP_K * t) + (jnp.arange(ROW_CHUNKS, dtype=jnp.int32) * n_rows)[:, None]).reshape(-1)
    xs = _sc_scatter(h.reshape(ROW_CHUNKS * t, CHUNK_W), idx, ROW_CHUNKS * n_rows, TOP_K)
    yb = _ffn(xs.reshape(ROW_CHUNKS, n_rows, CHUNK_W), blk_e, blk_valid, n_used, layer, wg, wu, wd, n_blocks)
    ytok = _sc_gather(yb.reshape(ROW_CHUNKS * n_rows, CHUNK_W), idx)
    return _combine(geo, x, mod_l, w.T, ytok.reshape(ROW_CHUNKS, TOP_K, t, CHUNK_W), sg, su, sd, ln_g, ln_b,
                    split=split)


def _pos_embed(rows):
    quarter = D // 4
    omega = 1.0 / (POS_BASE ** (jnp.arange(quarter, dtype=F32) / quarter))
    r = jnp.arange(rows, dtype=F32)[:, None] * omega
    col = jnp.arange(GRID_W, dtype=F32)[:, None] * omega
    row_part = jnp.concatenate([jnp.sin(r), jnp.cos(r)], axis=-1)[:, None, :]
    col_part = jnp.concatenate([jnp.sin(col), jnp.cos(col)], axis=-1)[None, :, :]
    shape = (rows, GRID_W, 2 * quarter)
    return jnp.concatenate([jnp.broadcast_to(row_part, shape), jnp.broadcast_to(col_part, shape)],
                           axis=-1).reshape(rows * GRID_W, D)


def _mlstm_layer(geo, x, mod_l, j, a_w_in, a_b_gates, a_norm, a_w_out, ln_g, ln_b,
                 state_C, state_n, state_m, w_router, e_bias):
    q, kt, v, so, gr = _proj_a(geo, x, mod_l, a_w_in[j], a_b_gates[j])
    hp, c_p, n_p, m_p = _mlstm_scan(q, kt, v, gr, row0=0, n_seq=geo.n_prompt, seq_len=geo.prompt_len)
    ns = geo.n_sample
    n0 = jnp.pad(state_n[:, j].astype(F32)[..., None], ((0, 0),) * 4 + ((0, LANES - 1),))
    m0 = jnp.pad(state_m[:, j].astype(F32), ((0, 0), (0, 0), (0, SUBLANES - NH_A)))
    m0 = jnp.broadcast_to(m0[..., None], (ns, 2, SUBLANES, LANES))
    hs, _, _, _ = _mlstm_scan(q, kt, v, gr, row0=geo.t_prompt, n_seq=ns, seq_len=geo.sample_len,
                              state=(state_C[:, j].astype(F32), n0, m0))
    x1, routing = _out_a(geo, hp, hs, so, a_norm[j], a_w_out[j], x, mod_l, ln_g, ln_b, w_router, e_bias)
    return x1, routing, c_p, n_p[..., 0], m_p[:, :, :NH_A, 0]


def _hgrn_layer(geo, x, mod_l, j, lb_layer, b_w_in, b_norm, b_w_out, ln_g, ln_b, state_S, w_router, e_bias):
    q, pre, v, sg = _proj_b(geo, x, mod_l, b_w_in[j])
    lbd = lb_layer.reshape(2, NH_B, 1, DK_B)
    op, s_p = _hgrn_scan(q, pre, v, lbd, row0=0, n_seq=geo.n_prompt, seq_len=geo.prompt_len)
    os_, _ = _hgrn_scan(q, pre, v, lbd, row0=geo.t_prompt, n_seq=geo.n_sample, seq_len=geo.sample_len,
                        state=state_S[:, j].astype(F32))
    x1, routing = _out_b(geo, op, os_, sg, b_norm[j], b_w_out[j], x, mod_l, ln_g, ln_b, w_router, e_bias)
    return x1, routing, s_p


def kernel(x_prompt, x_sample, state_mlstm_C, state_mlstm_n, state_mlstm_m, state_hgrn_S, c, c_ctx, w_mod, b_mod, ln_g, ln_b, a_w_in, a_b_gates, a_norm, a_w_out, b_w_in, b_lb, b_norm, b_w_out, w_router, e_bias, w_gate, w_up, w_down, ws_gate, ws_up, ws_down):
    bp, sp, _ = x_prompt.shape
    bs, ss, _ = x_sample.shape
    cond = jnp.zeros((COND_ROWS, D), F32).at[0].set(c_ctx).at[1:1 + bs].set(c)
    mod = _modulation(cond, w_mod, b_mod)
    x = (x_prompt.reshape(-1, D), x_sample.reshape(-1, D), _pos_embed(ss // GRID_W))
    sm = jax.nn.softmax(b_lb.astype(F32), axis=0)
    lb_all = jnp.cumsum(sm, axis=0) - sm[0]
    geo = Geometry(bp, sp, bs, ss)
    x1, routing, new_c, new_n, new_m = _mlstm_layer(geo, x, mod[0], 0, a_w_in, a_b_gates, a_norm, a_w_out,
                                                    ln_g[0, 0], ln_b[0, 0], state_mlstm_C, state_mlstm_n,
                                                    state_mlstm_m, w_router[0], e_bias[0])
    x2 = _moe_layer(geo, x1, routing, mod[0], 0, w_gate, w_up, w_down, ws_gate[0], ws_up[0], ws_down[0],
                    ln_g[0, 1], ln_b[0, 1])
    x3, routing, new_s = _hgrn_layer(geo, x2, mod[1], 0, lb_all[1], b_w_in, b_norm, b_w_out, ln_g[1, 0], ln_b[1, 0],
                                     state_hgrn_S, w_router[1], e_bias[1])
    y_p, y_s = _moe_layer(geo, x3, routing, mod[1], 1, w_gate, w_up, w_down, ws_gate[1], ws_up[1], ws_down[1],
                          ln_g[1, 1], ln_b[1, 1], split=True)
    return (y_p.reshape(bp, sp, D), y_s.reshape(bs, ss, D), new_c[:, None], new_n[:, None], new_m[:, None],
            new_s[:, None])
```

```python
import functools
import math

import jax
import jax.numpy as jnp
from jax import lax
from jax.experimental import pallas as pl
from jax.experimental.pallas import tpu as pltpu
from jax.experimental.pallas import tpu_sc as plsc

F32 = jnp.float32
BF16 = jnp.bfloat16
HIGHEST = lax.Precision.HIGHEST

D = 1024
DEPTH = 2
GRID_W = 64
POS_BASE = 10000.0
EPS = 1e-6
ALPHA = (2.0 * DEPTH) ** 0.25
NH_A, DK_A, DV_A = 4, 128, 256
QK_A, V_A = NH_A * DK_A, NH_A * DV_A
NH_B, DK_B = 8, 128
N_EXPERTS, TOP_K, N_GROUPS, TOPK_GROUPS = 64, 8, 8, 4
GROUP_SIZE = N_EXPERTS // N_GROUPS
D_EXPERT = D // 4
ROUTED_SCALE = 2.5

LANES = 128
SUBLANES = 8
COND_ROWS = 8
TOK_TILE_L = 512
SLOT_TILE = 2048
CHUNK_A = 512
VMEM_BYTES_V7X = 64 * 1024 * 1024
VMEM_LIMIT = VMEM_BYTES_V7X - 8 * 1024 * 1024

NT_DIMS = (((1,), (1,)), ((), ()))


def _params(sem):
    return pltpu.CompilerParams(dimension_semantics=sem, vmem_limit_bytes=VMEM_LIMIT)


def _split3(x):
    hi = x.astype(BF16)
    r = x - hi.astype(F32)
    mid = r.astype(BF16)
    lo = (r - mid.astype(F32)).astype(BF16)
    return hi, mid, lo


def _dot3(a_bf, x):
    hi, mid, lo = _split3(x)
    return (jnp.dot(a_bf, hi, preferred_element_type=F32)
            + jnp.dot(a_bf, mid, preferred_element_type=F32)
            + jnp.dot(a_bf, lo, preferred_element_type=F32))


def _dot3_r(x, a_bf):
    hi, mid, lo = _split3(x)
    return (jnp.dot(hi, a_bf, preferred_element_type=F32)
            + jnp.dot(mid, a_bf, preferred_element_type=F32)
            + jnp.dot(lo, a_bf, preferred_element_type=F32))


def _dot_nt_x3(w, x):
    w_hi = w.astype(BF16)
    w_lo = (w - w_hi.astype(F32)).astype(BF16)
    x_hi = x.astype(BF16)
    x_lo = (x - x_hi.astype(F32)).astype(BF16)
    d = lambda a, b: lax.dot_general(a, b, NT_DIMS, preferred_element_type=F32)
    return d(w_hi, x_hi) + d(w_hi, x_lo) + d(w_lo, x_hi)


def _log_sigmoid(x):
    return jnp.minimum(x, 0.0) - jnp.log1p(jnp.exp(-jnp.abs(x)))


def _layer_norm_rows(x, g, b):
    mu = jnp.mean(x, axis=-1, keepdims=True)
    xc = x - mu
    var = jnp.mean(xc * xc, axis=-1, keepdims=True)
    return xc * lax.rsqrt(var + EPS) * g + b


class Geometry:
    def __init__(self, n_prompt, prompt_len, n_sample, sample_len):
        self.n_prompt, self.prompt_len = n_prompt, prompt_len
        self.n_sample, self.sample_len = n_sample, sample_len
        self.t_prompt = n_prompt * prompt_len
        self.t_sample = n_sample * sample_len
        self.t = self.t_prompt + self.t_sample
        assert self.t_prompt % TOK_TILE_L == 0 and sample_len % TOK_TILE_L == 0
        assert n_sample + 1 <= COND_ROWS

    def cond_row(self, tile, tile_rows):
        n_p = self.t_prompt // tile_rows
        return jnp.where(tile < n_p, 0, 1 + (tile - n_p) // (self.sample_len // tile_rows))


def _mod_kernel(cond_ref, w_ref, b_ref, o_ref):
    c = cond_ref[...]
    s = c * jax.nn.sigmoid(c)
    o_ref[0, 0] = jnp.dot(s, w_ref[0], precision=HIGHEST, preferred_element_type=F32) + b_ref[0, 0]


def _modulation(cond, w_mod, b_mod):
    out = pl.pallas_call(
        _mod_kernel,
        name="modulation",
        grid=(DEPTH, 6),
        in_specs=[
            pl.BlockSpec((COND_ROWS, D), lambda l, j: (0, 0)),
            pl.BlockSpec((1, D, D), lambda l, j: (l, 0, j)),
            pl.BlockSpec((1, 1, 1, D), lambda l, j: (l, j, 0, 0)),
        ],
        out_specs=pl.BlockSpec((1, 1, COND_ROWS, D), lambda l, j: (l, j, 0, 0)),
        out_shape=jax.ShapeDtypeStruct((DEPTH, 6, COND_ROWS, D), F32),
        compiler_params=_params(("arbitrary", "arbitrary")),
    )(cond, w_mod, b_mod.reshape(DEPTH, 6, 1, D))
    return out.transpose(0, 2, 1, 3)


def _embed_specs(geo, tb):
    n_p = geo.t_prompt // tb
    per_seq = geo.sample_len // tb
    return [pl.BlockSpec((tb, D), lambda i: (jnp.minimum(i, n_p - 1), 0)),
            pl.BlockSpec((tb, D), lambda i: (jnp.maximum(i - n_p, 0), 0)),
            pl.BlockSpec((tb, D), lambda i: (jnp.maximum(i - n_p, 0) % per_seq, 0))]


def _embed_tile(xp_ref, xs_ref, pos_ref, n_prompt_tiles):
    return jnp.where(pl.program_id(0) < n_prompt_tiles, xp_ref[...], xs_ref[...] + pos_ref[...])


def _proj_a_kernel(xp_ref, xs_ref, pos_ref, mod_ref, wq_ref, wkt_ref, wvo_ref, wgt_ref, bgt_ref,
                   q_ref, kt_ref, v_ref, so_ref, gr_ref, *, n_prompt_tiles):
    x = _embed_tile(xp_ref, xs_ref, pos_ref, n_prompt_tiles)
    h = x * (1.0 + mod_ref[1:2, :]) + mod_ref[0:1, :]
    hb = h.astype(BF16)
    q_ref[...] = jnp.dot(hb, wq_ref[...], preferred_element_type=F32).astype(BF16)
    kt = lax.dot_general(wkt_ref[...], hb, NT_DIMS, preferred_element_type=F32)
    kt_ref[...] = (kt * (DK_A ** -0.5)).astype(BF16)
    vo = jnp.dot(hb, wvo_ref[...], preferred_element_type=F32)
    v_ref[...] = vo[:, :V_A].astype(BF16)
    so_ref[...] = jax.nn.sigmoid(vo[:, V_A:]).astype(BF16)
    gr_ref[...] = _dot_nt_x3(wgt_ref[...], h) + bgt_ref[...]


def _proj_a(geo, x, mod_l, w_in, b_gates):
    t = geo.t
    n_gate = 4 * NH_A
    wq = w_in[:, :QK_A].astype(BF16)
    wkt = w_in[:, QK_A:2 * QK_A].T.astype(BF16)
    wvo = w_in[:, 2 * QK_A:2 * QK_A + 2 * V_A].astype(BF16)
    wg = w_in[:, 2 * QK_A + 2 * V_A:]
    bg = b_gates.reshape(n_gate).astype(F32)
    tb = TOK_TILE_L
    full = lambda shape: pl.BlockSpec(shape, lambda i: (0,) * len(shape))
    return pl.pallas_call(
        functools.partial(_proj_a_kernel, n_prompt_tiles=geo.t_prompt // tb),
        name="proj_a",
        grid=(t // tb,),
        in_specs=_embed_specs(geo, tb) + [
            pl.BlockSpec((None, 6, D), lambda i: (geo.cond_row(i, tb), 0, 0)),
            full((D, QK_A)), full((QK_A, D)), full((D, 2 * V_A)), full((n_gate, D)), full((n_gate, 1)),
        ],
        out_specs=[
            pl.BlockSpec((tb, QK_A), lambda i: (i, 0)),
            pl.BlockSpec((QK_A, tb), lambda i: (0, i)),
            pl.BlockSpec((tb, V_A), lambda i: (i, 0)),
            pl.BlockSpec((tb, V_A), lambda i: (i, 0)),
            pl.BlockSpec((n_gate, tb), lambda i: (0, i)),
        ],
        out_shape=[
            jax.ShapeDtypeStruct((t, QK_A), BF16),
            jax.ShapeDtypeStruct((QK_A, t), BF16),
            jax.ShapeDtypeStruct((t, V_A), BF16),
            jax.ShapeDtypeStruct((t, V_A), BF16),
            jax.ShapeDtypeStruct((n_gate, t), F32),
        ],
        compiler_params=_params(("parallel",)),
    )(*x, mod_l, wq, wkt, wvo, wg.T, bg.reshape(n_gate, 1))


def _mlstm_scan_kernel(*refs, chunk, has_state):
    if has_state:
        (q_ref, kt_ref, v_ref, gr_ref, c0_ref, n0_ref, m0_ref,
         h_ref, c_out, n_out, m_out, c_sc, n_sc, m_sc) = refs
    else:
        (q_ref, kt_ref, v_ref, gr_ref,
         h_ref, c_out, n_out, m_out, c_sc, n_sc, m_sc) = refs
    L = chunk
    d = pl.program_id(1)
    c = pl.program_id(2)
    fwd = d == 0

    @pl.when(c == 0)
    def _():
        if has_state:
            c_sc[...] = c0_ref[0, 0]
            n_sc[...] = n0_ref[0, 0]
            m_sc[...] = m0_ref[0, 0]
        else:
            c_sc[...] = jnp.zeros_like(c_sc)
            n_sc[...] = jnp.zeros_like(n_sc)
            m_sc[...] = jnp.zeros_like(m_sc)

    row = lax.broadcasted_iota(jnp.int32, (L, L), 0)
    col = lax.broadcasted_iota(jnp.int32, (L, L), 1)
    sgn = 1 - 2 * d
    causal = (row - col) * sgn >= 0
    tri_t = ((col - row) * sgn >= 0).astype(BF16)

    gr = gr_ref[...]
    br_all = _dot3_r(_log_sigmoid(gr), tri_t)
    bc_all = jnp.concatenate([br_all, jnp.zeros((LANES - br_all.shape[0], L), F32)], axis=0).T
    ones_blk = (lax.broadcasted_iota(jnp.int32, (L, LANES), 1) == 0).astype(BF16)

    def gate_row(direction, gate, head):
        return (direction * 2 + gate) * NH_A + head

    for h in range(NH_A):
        ff, fb = gate_row(0, 1, h), gate_row(1, 1, h)
        gi, gb = gate_row(0, 0, h), gate_row(1, 0, h)
        b_c = jnp.where(fwd, bc_all[:, ff:ff + 1], bc_all[:, fb:fb + 1])
        b_r = jnp.where(fwd, br_all[ff:ff + 1, :], br_all[fb:fb + 1, :])
        i_r = jnp.where(fwd, gr[gi:gi + 1, :], gr[gb:gb + 1, :])
        bl = jnp.where(fwd, b_r[:, L - 1:L], b_r[:, 0:1])
        q = q_ref[:, h * DK_A:(h + 1) * DK_A]
        kt = kt_ref[h * DK_A:(h + 1) * DK_A, :]
        v = v_ref[:, h * DV_A:(h + 1) * DV_A]
        m = m_sc[h:h + 1, 0:1]
        cst = c_sc[h]
        nst = n_sc[h]

        a_r = i_r - b_r
        logd = jnp.where(causal, b_c + a_r, -jnp.inf)
        inter = b_c + m
        m_t = jnp.maximum(inter, jnp.max(logd, axis=1, keepdims=True))
        dmat = jnp.exp(logd - m_t)
        e_int = jnp.exp(inter - m_t)
        s = (jnp.dot(q, kt, preferred_element_type=F32) * dmat).astype(BF16)
        num = (jnp.dot(s, v, preferred_element_type=F32)
               + e_int * jnp.dot(q, cst.astype(BF16), preferred_element_type=F32))
        den = (jnp.dot(s, ones_blk, preferred_element_type=F32)
               + e_int * jnp.dot(q, nst.astype(BF16), preferred_element_type=F32))[:, 0:1]
        h_ref[:, h * DV_A:(h + 1) * DV_A] = (num / jnp.maximum(jnp.abs(den), jnp.exp(-m_t))).astype(BF16)

        logw = bl + a_r
        m_new = jnp.maximum(bl + m, jnp.max(logw, axis=1, keepdims=True))
        w = jnp.exp(logw - m_new)
        decay = jnp.exp(bl + m - m_new)
        kw = (kt.astype(F32) * w).astype(BF16)
        c_sc[h] = decay * cst + jnp.dot(kw, v, preferred_element_type=F32)
        n_sc[h] = decay * nst + jnp.dot(kw, ones_blk, preferred_element_type=F32)
        m_sc[h:h + 1, :] = jnp.broadcast_to(m_new, (1, LANES))

    @pl.when(c == pl.num_programs(2) - 1)
    def _():
        c_out[0, 0] = c_sc[...]
        n_out[0, 0] = n_sc[...]
        m_out[0, 0] = m_sc[...]


def _mlstm_scan(q, kt, v, gr, *, row0, n_seq, seq_len, state=None):
    L = min(CHUNK_A, seq_len)
    nc = seq_len // L
    blk0 = row0 // L

    def loc_blk(b, d, c):
        return b * nc + c + d * (nc - 1 - 2 * c)

    def tok_blk(b, d, c):
        return blk0 + loc_blk(b, d, c)

    in_specs = [
        pl.BlockSpec((L, QK_A), lambda b, d, c: (tok_blk(b, d, c), 0)),
        pl.BlockSpec((QK_A, L), lambda b, d, c: (0, tok_blk(b, d, c))),
        pl.BlockSpec((L, V_A), lambda b, d, c: (tok_blk(b, d, c), 0)),
        pl.BlockSpec((4 * NH_A, L), lambda b, d, c: (0, tok_blk(b, d, c))),
    ]
    args = [q, kt, v, gr]
    if state is not None:
        in_specs += [
            pl.BlockSpec((1, 1, NH_A, DK_A, DV_A), lambda b, d, c: (b, d, 0, 0, 0)),
            pl.BlockSpec((1, 1, NH_A, DK_A, LANES), lambda b, d, c: (b, d, 0, 0, 0)),
            pl.BlockSpec((1, 1, SUBLANES, LANES), lambda b, d, c: (b, d, 0, 0)),
        ]
        args += list(state)
    return pl.pallas_call(
        functools.partial(_mlstm_scan_kernel, chunk=L, has_state=state is not None),
        name="mlstm_scan_seeded" if state is not None else "mlstm_scan",
        grid=(n_seq, 2, nc),
        in_specs=in_specs,
        out_specs=[
            pl.BlockSpec((None, L, V_A), lambda b, d, c: (d, loc_blk(b, d, c), 0)),
            pl.BlockSpec((1, 1, NH_A, DK_A, DV_A), lambda b, d, c: (b, d, 0, 0, 0)),
            pl.BlockSpec((1, 1, NH_A, DK_A, LANES), lambda b, d, c: (b, d, 0, 0, 0)),
            pl.BlockSpec((1, 1, SUBLANES, LANES), lambda b, d, c: (b, d, 0, 0)),
        ],
        out_shape=[
            jax.ShapeDtypeStruct((2, n_seq * seq_len, V_A), BF16),
            jax.ShapeDtypeStruct((n_seq, 2, NH_A, DK_A, DV_A), F32),
            jax.ShapeDtypeStruct((n_seq, 2, NH_A, DK_A, LANES), F32),
            jax.ShapeDtypeStruct((n_seq, 2, SUBLANES, LANES), F32),
        ],
        scratch_shapes=[
            pltpu.VMEM((NH_A, DK_A, DV_A), F32),
            pltpu.VMEM((NH_A, DK_A, LANES), F32),
            pltpu.VMEM((SUBLANES, LANES), F32),
        ],
        compiler_params=_params(("parallel", "parallel", "arbitrary")),
    )(*args)


def _out_a_kernel(hp_ref, hs_ref, so_ref, nw_ref, w_ref, xp_ref, xs_ref, pos_ref, mod_ref, lg_ref, lb_ref,
                  wrt_ref, eb_ref, o_ref, *route_refs, n_prompt_tiles):
    is_prompt = pl.program_id(0) < n_prompt_tiles
    x = _embed_tile(xp_ref, xs_ref, pos_ref, n_prompt_tiles)
    y = jnp.where(is_prompt, hp_ref[0].astype(F32) + hp_ref[1].astype(F32),
                  hs_ref[0].astype(F32) + hs_ref[1].astype(F32))
    parts = []
    for h in range(NH_A):
        yh = y[:, h * DV_A:(h + 1) * DV_A]
        mu = jnp.mean(yh, axis=-1, keepdims=True)
        yc = yh - mu
        var = jnp.mean(yc * yc, axis=-1, keepdims=True)
        parts.append(yc * lax.rsqrt(var + EPS))
    yn = jnp.concatenate(parts, axis=-1) * nw_ref[...] * so_ref[...].astype(F32)
    out = jnp.dot(yn.astype(BF16), w_ref[...], preferred_element_type=F32)
    x1 = _layer_norm_rows(ALPHA * x + mod_ref[2:3, :] * out, lg_ref[...], lb_ref[...])
    o_ref[...] = x1
    _route_tile(x1, mod_ref, wrt_ref, eb_ref, *route_refs)


def _out_a(geo, h_prompt, h_sample, so, norm_w, w_out, x, mod_l, ln_g, ln_b, w_router, e_bias):
    t = geo.t
    tb = TOK_TILE_L
    n_p = geo.t_prompt // tb
    full = lambda shape: pl.BlockSpec(shape, lambda i: (0,) * len(shape))
    rio = _RouterIO(t, tb, w_router, e_bias)
    outs = pl.pallas_call(
        functools.partial(_out_a_kernel, n_prompt_tiles=n_p),
        name="out_a",
        grid=(t // tb,),
        in_specs=[
            pl.BlockSpec((2, tb, V_A), lambda i: (0, jnp.minimum(i, n_p - 1), 0)),
            pl.BlockSpec((2, tb, V_A), lambda i: (0, jnp.maximum(i - n_p, 0), 0)),
            pl.BlockSpec((tb, V_A), lambda i: (i, 0)),
            full((1, V_A)), full((V_A, D)),
        ] + _embed_specs(geo, tb) + [
            pl.BlockSpec((None, 6, D), lambda i: (geo.cond_row(i, tb), 0, 0)),
            full((1, D)), full((1, D)),
        ] + rio.in_specs,
        out_specs=[pl.BlockSpec((tb, D), lambda i: (i, 0))] + rio.out_specs,
        out_shape=[jax.ShapeDtypeStruct((t, D), F32)] + rio.out_shape,
        scratch_shapes=rio.scratch,
        compiler_params=_params(("arbitrary",)),
    )(h_prompt, h_sample, so, norm_w.reshape(1, V_A).astype(F32), w_out.astype(BF16), *x, mod_l,
      ln_g.reshape(1, D), ln_b.reshape(1, D), *rio.inputs)
    return outs[0], _RouterIO.unpack(outs[1:])


def _proj_b_kernel(x_ref, mod_ref, w_ref, q_ref, pre_ref, v_ref, sg_ref):
    h = x_ref[...] * (1.0 + mod_ref[1:2, :]) + mod_ref[0:1, :]
    z = jnp.dot(h.astype(BF16), w_ref[...], preferred_element_type=F32)
    for hd in range(NH_B):
        lo = hd * DK_B
        qh = z[:, lo:lo + DK_B]
        q_ref[hd] = qh * jax.nn.sigmoid(qh)
        pre_ref[0, hd] = z[:, D + lo:D + lo + DK_B]
        pre_ref[1, hd] = z[:, 2 * D + lo:2 * D + lo + DK_B]
        v_ref[hd] = z[:, 3 * D + lo:3 * D + lo + DK_B].astype(BF16)
    g = z[:, 4 * D:]
    sg_ref[...] = (g * jax.nn.sigmoid(g)).astype(BF16)


def _proj_b(geo, x, mod_l, w_in):
    t = geo.t
    tb = TOK_TILE_L
    return pl.pallas_call(
        _proj_b_kernel,
        name="proj_b",
        grid=(t // tb,),
        in_specs=[
            pl.BlockSpec((tb, D), lambda i: (i, 0)),
            pl.BlockSpec((None, 6, D), lambda i: (geo.cond_row(i, tb), 0, 0)),
            pl.BlockSpec((D, 5 * D), lambda i: (0, 0), pipeline_mode=pl.Buffered(1)),
        ],
        out_specs=[
            pl.BlockSpec((NH_B, tb, DK_B), lambda i: (0, i, 0)),
            pl.BlockSpec((2, NH_B, tb, DK_B), lambda i: (0, 0, i, 0)),
            pl.BlockSpec((NH_B, tb, DK_B), lambda i: (0, i, 0)),
            pl.BlockSpec((tb, D), lambda i: (i, 0)),
        ],
        out_shape=[
            jax.ShapeDtypeStruct((NH_B, t, DK_B), F32),
            jax.ShapeDtypeStruct((2, NH_B, t, DK_B), F32),
            jax.ShapeDtypeStruct((NH_B, t, DK_B), BF16),
            jax.ShapeDtypeStruct((t, D), BF16),
        ],
        compiler_params=_params(("parallel",)),
    )(x, mod_l, w_in.astype(BF16))


CHUNK_B = 256
BAND = SUBLANES // 2
TN_DIMS = (((0,), (0,)), ((), ()))


def _hgrn_head(q, pre, lbv, v_bf, st, fwd):
    L = q.shape[0]
    sg = jax.nn.sigmoid(pre)
    f = lbv + (1.0 - lbv) * sg
    lf = jnp.log(f)
    kk = (1.0 - lbv) * (1.0 - sg)
    row = lax.broadcasted_iota(jnp.int32, (L, L), 0)
    col = lax.broadcasted_iota(jnp.int32, (L, L), 1)
    tri = ((row >= col) if fwd else (row <= col)).astype(BF16)
    b = _dot3(tri, lf)
    tpos = lax.broadcasted_iota(jnp.int32, (L, DK_B), 0)
    blk_bits = row ^ col
    lag = jnp.where(blk_bits < BAND, (row - col) if fwd else (col - row), -1)

    step = 1 if fwd else L - 1
    att = jnp.where(lag == 0, jnp.sum(q * kk, axis=1, keepdims=True), 0.0)
    f_r, kk_r, g = f, kk, f
    for dl in range(1, BAND):
        if dl > 1:
            f_r = pltpu.roll(f_r, step, 0)
            g = g * f_r
        kk_r = pltpu.roll(kk_r, step, 0)
        att = jnp.where(lag == dl, jnp.sum(q * kk_r * g, axis=1, keepdims=True), att)

    w = BAND
    while w < L:
        nb = L // (2 * w)
        b3 = b.reshape(nb, 2 * w, DK_B)
        edge = (b3[:, w - 1:w, :] if fwd else b3[:, w:w + 1, :])
        bmid = jnp.broadcast_to(edge, (nb, 2 * w, DK_B)).reshape(L, DK_B)
        second = (tpos & w) != 0
        t_side = second if fwd else jnp.logical_not(second)
        e = jnp.exp(jnp.where(t_side, b - bmid, bmid - b))
        qt = jnp.where(t_side, q * e, 0.0).astype(BF16)
        ks = jnp.where(t_side, 0.0, kk * e).astype(BF16)
        a = lax.dot_general(qt, ks, NT_DIMS, preferred_element_type=F32)
        att = att + jnp.where(blk_bits < 2 * w, a, 0.0)
        w *= 2
    o = jnp.dot(att.astype(BF16), v_bf, preferred_element_type=F32)

    bl = b[L - 1:L, :] if fwd else b[0:1, :]
    o = o + lax.dot_general((q * jnp.exp(b)).astype(BF16), st.astype(BF16), NT_DIMS, preferred_element_type=F32)
    kd = (kk * jnp.exp(bl - b)).astype(BF16)
    st_new = jnp.exp(bl) * st + lax.dot_general(v_bf, kd, TN_DIMS, preferred_element_type=F32)
    return o, st_new


def _hgrn_scan_kernel(*refs, has_state):
    if has_state:
        q_ref, pre_ref, v_ref, lb_ref, s0_ref, o_ref, s_out, st_sc = refs
    else:
        q_ref, pre_ref, v_ref, lb_ref, o_ref, s_out, st_sc = refs
    d = pl.program_id(1)
    c = pl.program_id(2)

    @pl.when(c == 0)
    def _():
        if has_state:
            for hd in range(NH_B):
                st_sc[hd] = s0_ref[0, 0, hd].T
        else:
            st_sc[...] = jnp.zeros_like(st_sc)

    def run(fwd):
        def head(hd, carry):
            o, st_new = _hgrn_head(q_ref[hd], pre_ref[hd], lb_ref[hd], v_ref[hd], st_sc[hd], fwd)
            o_ref[hd] = o.astype(BF16)
            st_sc[hd] = st_new
            return carry
        lax.fori_loop(0, NH_B, head, 0, unroll=8)

    @pl.when(d == 0)
    def _():
        run(True)

    @pl.when(d == 1)
    def _():
        run(False)

    @pl.when(c == pl.num_programs(2) - 1)
    def _():
        for hd in range(NH_B):
            s_out[0, 0, hd] = st_sc[hd].T


def _hgrn_scan(q, pre, v, lbd, *, row0, n_seq, seq_len, state=None):
    L = CHUNK_B
    nc = seq_len // L
    blk0 = row0 // L

    def loc_blk(b, d, c):
        return b * nc + c + d * (nc - 1 - 2 * c)

    def tok_blk(b, d, c):
        return blk0 + loc_blk(b, d, c)

    in_specs = [
        pl.BlockSpec((NH_B, L, DK_B), lambda b, d, c: (0, tok_blk(b, d, c), 0)),
        pl.BlockSpec((None, NH_B, L, DK_B), lambda b, d, c: (d, 0, tok_blk(b, d, c), 0)),
        pl.BlockSpec((NH_B, L, DK_B), lambda b, d, c: (0, tok_blk(b, d, c), 0)),
        pl.BlockSpec((None, NH_B, 1, DK_B), lambda b, d, c: (d, 0, 0, 0)),
    ]
    args = [q, pre, v, lbd]
    if state is not None:
        in_specs.append(pl.BlockSpec((1, 1, NH_B, DK_B, DK_B), lambda b, d, c: (b, d, 0, 0, 0)))
        args.append(state)
    return pl.pallas_call(
        functools.partial(_hgrn_scan_kernel, has_state=state is not None),
        name="hgrn_scan_seeded" if state is not None else "hgrn_scan",
        grid=(n_seq, 2, nc),
        in_specs=in_specs,
        out_specs=[
            pl.BlockSpec((None, NH_B, L, DK_B), lambda b, d, c: (d, 0, loc_blk(b, d, c), 0)),
            pl.BlockSpec((1, 1, NH_B, DK_B, DK_B), lambda b, d, c: (b, d, 0, 0, 0)),
        ],
        out_shape=[
            jax.ShapeDtypeStruct((2, NH_B, n_seq * seq_len, DK_B), BF16),
            jax.ShapeDtypeStruct((n_seq, 2, NH_B, DK_B, DK_B), F32),
        ],
        scratch_shapes=[pltpu.VMEM((NH_B, DK_B, DK_B), F32)],
        compiler_params=_params(("parallel", "parallel", "arbitrary")),
    )(*args)


def _out_b_kernel(op_ref, os_ref, sg_ref, nw_ref, w_ref, x_ref, mod_ref, lg_ref, lb_ref, wrt_ref, eb_ref,
                  out_ref, *route_refs, n_prompt_tiles):
    is_prompt = pl.program_id(0) < n_prompt_tiles
    parts = []
    for hd in range(NH_B):
        y = jnp.where(is_prompt, op_ref[0, hd].astype(F32) + op_ref[1, hd].astype(F32),
                      os_ref[0, hd].astype(F32) + os_ref[1, hd].astype(F32))
        parts.append(y * lax.rsqrt(jnp.mean(y * y, axis=-1, keepdims=True) + EPS))
    yn = jnp.concatenate(parts, axis=-1) * nw_ref[...] * sg_ref[...].astype(F32)
    out = jnp.dot(yn.astype(BF16), w_ref[...], preferred_element_type=F32)
    x1 = _layer_norm_rows(ALPHA * x_ref[...] + mod_ref[2:3, :] * out, lg_ref[...], lb_ref[...])
    out_ref[...] = x1
    _route_tile(x1, mod_ref, wrt_ref, eb_ref, *route_refs)


def _out_b(geo, o_prompt, o_sample, sg, norm_w, w_out, x, mod_l, ln_g, ln_b, w_router, e_bias):
    t = geo.t
    tb = TOK_TILE_L
    n_p = geo.t_prompt // tb
    full = lambda shape: pl.BlockSpec(shape, lambda i: (0,) * len(shape))
    rio = _RouterIO(t, tb, w_router, e_bias)
    outs = pl.pallas_call(
        functools.partial(_out_b_kernel, n_prompt_tiles=n_p),
        name="out_b",
        grid=(t // tb,),
        in_specs=[
            pl.BlockSpec((2, NH_B, tb, DK_B), lambda i: (0, 0, jnp.minimum(i, n_p - 1), 0)),
            pl.BlockSpec((2, NH_B, tb, DK_B), lambda i: (0, 0, jnp.maximum(i - n_p, 0), 0)),
            pl.BlockSpec((tb, D), lambda i: (i, 0)),
            full((1, D)), full((D, D)),
            pl.BlockSpec((tb, D), lambda i: (i, 0)),
            pl.BlockSpec((None, 6, D), lambda i: (geo.cond_row(i, tb), 0, 0)),
            full((1, D)), full((1, D)),
        ] + rio.in_specs,
        out_specs=[pl.BlockSpec((tb, D), lambda i: (i, 0))] + rio.out_specs,
        out_shape=[jax.ShapeDtypeStruct((t, D), F32)] + rio.out_shape,
        scratch_shapes=rio.scratch,
        compiler_params=_params(("arbitrary",)),
    )(o_prompt, o_sample, sg, norm_w.reshape(1, D).astype(F32), w_out.astype(BF16), x, mod_l,
      ln_g.reshape(1, D), ln_b.reshape(1, D), *rio.inputs)
    return outs[0], _RouterIO.unpack(outs[1:])


MOE_BLK = 1024
U32 = jnp.uint32
ROW_WORDS = D // 2
CHUNK_W = 256
ROW_CHUNKS = ROW_WORDS // CHUNK_W
SC_WINDOW = 128


def _pack_rows(x):
    return pltpu.pack_elementwise([x[:, :ROW_WORDS], x[:, ROW_WORDS:]], packed_dtype=BF16)


def _unpack_rows(words):
    return jnp.concatenate([pltpu.unpack_elementwise(words, index=i, packed_dtype=BF16, unpacked_dtype=F32)
                            for i in range(2)], axis=1)


def _store_chunks(chunk_ref, x):
    words = _pack_rows(x)
    for c in range(ROW_CHUNKS):
        chunk_ref(c)[...] = words[:, c * CHUNK_W:(c + 1) * CHUNK_W]


def _load_chunks(chunk_ref, valid_rows=None):
    words = jnp.concatenate([chunk_ref(c)[...] for c in range(ROW_CHUNKS)], axis=1)
    if valid_rows is not None:
        row = lax.broadcasted_iota(jnp.int32, (words.shape[0], 1), 0)
        words = jnp.where(row < valid_rows, words, jnp.uint32(0))
    return _unpack_rows(words)


def _first_index(hit, iota, size, axis):
    return jnp.min(jnp.where(hit, iota, size), axis=axis, keepdims=True)


def _route_tile(x, mod_ref, wrt_ref, eb_ref, e_ref, w_ref, r_ref, cnt_ref, h_ref, cnt_sc):
    i = pl.program_id(0)
    tb = x.shape[0]

    @pl.when(i == 0)
    def _():
        cnt_sc[...] = jnp.zeros_like(cnt_sc)

    h = x * (1.0 + mod_ref[4:5, :]) + mod_ref[3:4, :]
    _store_chunks(lambda c: h_ref.at[c], h)
    logits = _dot_nt_x3(wrt_ref[...], h)
    scores = jax.nn.sigmoid(logits)
    sel = scores + eb_ref[...]

    g3 = sel.reshape(N_GROUPS, GROUP_SIZE, tb)
    io3 = lax.broadcasted_iota(jnp.int32, g3.shape, 1)
    m1 = jnp.max(g3, axis=1, keepdims=True)
    first = _first_index(g3 == m1, io3, GROUP_SIZE, 1)
    m2 = jnp.max(jnp.where(io3 == first, -jnp.inf, g3), axis=1, keepdims=True)
    gscore = (m1 + m2).reshape(N_GROUPS, tb)

    iog = lax.broadcasted_iota(jnp.int32, gscore.shape, 0)
    gmask = jnp.zeros(gscore.shape, F32)
    for _ in range(TOPK_GROUPS):
        gm = jnp.max(gscore, axis=0, keepdims=True)
        pick = iog == _first_index(gscore == gm, iog, N_GROUPS, 0)
        gmask = jnp.where(pick, 1.0, gmask)
        gscore = jnp.where(pick, -jnp.inf, gscore)
    emask = jnp.broadcast_to(gmask.reshape(N_GROUPS, 1, tb), (N_GROUPS, GROUP_SIZE, tb)).reshape(N_EXPERTS, tb)
    cand = jnp.where(emask > 0.0, sel, -jnp.inf)

    ioe = lax.broadcasted_iota(jnp.int32, cand.shape, 0)
    picks, wts = [], []
    onehot = jnp.zeros(cand.shape, F32)
    for _ in range(TOP_K):
        cm = jnp.max(cand, axis=0, keepdims=True)
        idx = _first_index(cand == cm, ioe, N_EXPERTS, 0)
        pick = ioe == idx
        picks.append(pick)
        wts.append(jnp.sum(jnp.where(pick, scores, 0.0), axis=0, keepdims=True))
        onehot = onehot + pick.astype(F32)
        cand = jnp.where(pick, -jnp.inf, cand)
        e_ref[pl.ds(len(picks) - 1, 1), :] = idx
    wsum = wts[0]
    for wk in wts[1:]:
        wsum = wsum + wk
    for k in range(TOP_K):
        w_ref[pl.ds(k, 1), :] = wts[k] / wsum * ROUTED_SCALE

    r_io = lax.broadcasted_iota(jnp.int32, (tb, tb), 0)
    c_io = lax.broadcasted_iota(jnp.int32, (tb, tb), 1)
    before = (r_io < c_io).astype(BF16)
    rank = cnt_sc[:, 0:1] + jnp.dot(onehot.astype(BF16), before, preferred_element_type=F32)
    for k in range(TOP_K):
        r_ref[pl.ds(k, 1), :] = jnp.sum(jnp.where(picks[k], rank, 0.0), axis=0, keepdims=True).astype(jnp.int32)
    cnt_sc[...] = cnt_sc[...] + jnp.sum(onehot, axis=1, keepdims=True)
    cnt_ref[...] = cnt_sc[...]


class _RouterIO:
    def __init__(self, t, tb, w_router, e_bias):
        full = lambda shape: pl.BlockSpec(shape, lambda i: (0,) * len(shape))
        self.inputs = [w_router.T.astype(F32), e_bias.reshape(N_EXPERTS, 1).astype(F32)]
        self.in_specs = [full((N_EXPERTS, D)), full((N_EXPERTS, 1))]
        self.out_specs = [
            pl.BlockSpec((TOP_K, tb), lambda i: (0, i)),
            pl.BlockSpec((TOP_K, tb), lambda i: (0, i)),
            pl.BlockSpec((TOP_K, tb), lambda i: (0, i)),
            full((N_EXPERTS, LANES)),
            pl.BlockSpec((ROW_CHUNKS, tb, CHUNK_W), lambda i: (0, i, 0)),
        ]
        self.out_shape = [
            jax.ShapeDtypeStruct((TOP_K, t), jnp.int32),
            jax.ShapeDtypeStruct((TOP_K, t), F32),
            jax.ShapeDtypeStruct((TOP_K, t), jnp.int32),
            jax.ShapeDtypeStruct((N_EXPERTS, LANES), F32),
            jax.ShapeDtypeStruct((ROW_CHUNKS, t, CHUNK_W), U32),
        ]
        self.scratch = [pltpu.VMEM((N_EXPERTS, LANES), F32)]

    @staticmethod
    def unpack(outs):
        e, w, r, cnt, h = outs
        return e, w, r, cnt[:, 0].astype(jnp.int32), h


def _slot_kernel(pstart_ref, e_ref, r_ref, o_ref):
    e = e_ref[...]
    slot = r_ref[...]
    for x in range(N_EXPERTS):
        slot = slot + jnp.where(e == x, pstart_ref[x], 0)
    o_ref[...] = slot


def _slots(geo, pstart, top_e, rank):
    tb = math.gcd(SLOT_TILE, geo.t)
    return pl.pallas_call(
        _slot_kernel,
        name="slots",
        grid_spec=pltpu.PrefetchScalarGridSpec(
            num_scalar_prefetch=1,
            grid=(geo.t // tb,),
            in_specs=[pl.BlockSpec((TOP_K, tb), lambda i, p: (0, i)),
                      pl.BlockSpec((TOP_K, tb), lambda i, p: (0, i))],
            out_specs=pl.BlockSpec((TOP_K, tb), lambda i, p: (0, i)),
        ),
        out_shape=jax.ShapeDtypeStruct((TOP_K, geo.t), jnp.int32),
        compiler_params=_params(("parallel",)),
    )(pstart, top_e, rank)


def _block_meta_kernel(pstart_ref, counts_ref, pend_ref, e_ref, v_ref):
    row0 = lax.broadcasted_iota(jnp.int32, e_ref.shape, 1) * MOE_BLK
    blk_e = jnp.zeros(e_ref.shape, jnp.int32)
    for x in range(N_EXPERTS):
        blk_e = blk_e + jnp.where(pend_ref[x] <= row0, 1, 0)
    blk_e = jnp.minimum(blk_e, N_EXPERTS - 1)
    last = jnp.zeros(e_ref.shape, jnp.int32)
    for x in range(N_EXPERTS):
        last = last + jnp.where(blk_e == x, pstart_ref[x] + counts_ref[x], 0)
    e_ref[...] = blk_e
    v_ref[...] = jnp.clip(last - row0, 0, MOE_BLK)


def _block_meta(pstart, counts, pend, n_blocks):
    e, v = pl.pallas_call(
        _block_meta_kernel,
        name="block_meta",
        grid_spec=pltpu.PrefetchScalarGridSpec(
            num_scalar_prefetch=3,
            grid=(1,),
            in_specs=[],
            out_specs=[pl.BlockSpec((1, n_blocks), lambda i, a, b, c: (0, 0)),
                       pl.BlockSpec((1, n_blocks), lambda i, a, b, c: (0, 0))],
        ),
        out_shape=[jax.ShapeDtypeStruct((1, n_blocks), jnp.int32), jax.ShapeDtypeStruct((1, n_blocks), jnp.int32)],
        compiler_params=_params(("arbitrary",)),
    )(pstart, counts, pend)
    return e[0], v[0]


def _sc_mesh():
    return plsc.VectorSubcoreMesh(core_axis_name="core", subcore_axis_name="subcore")


def _sc_scatter(rows, idx, n_out, copies):
    n_src = rows.shape[0]
    n_idx = idx.shape[0]
    groups = ROW_CHUNKS
    win_per_group = n_src // groups // SC_WINDOW

    def idx_block(w, k):
        return (0, ((w // win_per_group) * copies + k) * win_per_group + w % win_per_group)

    @pl.kernel(out_type=jax.ShapeDtypeStruct((n_out, CHUNK_W), rows.dtype), mesh=_sc_mesh(), scratch_types=[],
               name="sc_dispatch")
    def scatter(x_hbm, i_hbm, o_hbm):
        def body(x_vmem, *i_vmems):
            for i_vmem in i_vmems:
                pltpu.sync_copy(x_vmem, o_hbm.at[i_vmem.at[0]])

        pltpu.emit_pipeline(
            body,
            grid=(n_src // SC_WINDOW,),
            in_specs=[pl.BlockSpec((SC_WINDOW, CHUNK_W), index_map=lambda w: (w, 0))]
            + [pl.BlockSpec((1, SC_WINDOW), index_map=functools.partial(idx_block, k=k)) for k in range(copies)],
            out_specs=[],
            core_axis_name=("core", "subcore"),
            dimension_semantics=(pltpu.PARALLEL,),
        )(x_hbm, *([i_hbm] * copies))

    return scatter(rows, idx.reshape(1, n_idx))


def _sc_gather(table, idx):
    n_idx = idx.shape[0]

    @pl.kernel(out_type=jax.ShapeDtypeStruct((n_idx, CHUNK_W), table.dtype), mesh=_sc_mesh(),
               name="sc_combine_gather")
    def gather(t_hbm, i_hbm, o_hbm):
        def body(i_vmem, o_vmem):
            pltpu.sync_copy(t_hbm.at[i_vmem.at[0]], o_vmem)

        pltpu.emit_pipeline(
            body,
            grid=(n_idx // SC_WINDOW,),
            in_specs=[pl.BlockSpec((1, SC_WINDOW), index_map=lambda w: (0, w))],
            out_specs=[pl.BlockSpec((SC_WINDOW, CHUNK_W), index_map=lambda w: (w, 0))],
            core_axis_name=("core", "subcore"),
            dimension_semantics=(pltpu.PARALLEL,),
        )(i_hbm, o_hbm)

    return gather(table, idx.reshape(1, n_idx))


def _ffn_kernel(blk_e_ref, blk_valid_ref, n_used_ref, xs_ref, wg_ref, wu_ref, wd_ref, y_ref, wg_sc, wu_sc, wd_sc):
    b = pl.program_id(0)
    used = b < n_used_ref[0]
    new_expert = (b == 0) | (blk_e_ref[b] != blk_e_ref[jnp.maximum(b - 1, 0)])

    @pl.when(used & new_expert)
    def _():
        wg_sc[...] = wg_ref[...].astype(BF16)
        wu_sc[...] = wu_ref[...].astype(BF16)
        wd_sc[...] = wd_ref[...].astype(BF16)

    @pl.when(used)
    def _():
        x = _load_chunks(lambda c: xs_ref.at[c], valid_rows=blk_valid_ref[b]).astype(BF16)
        g = jnp.dot(x, wg_sc[...], preferred_element_type=F32)
        u = jnp.dot(x, wu_sc[...], preferred_element_type=F32)
        hmid = (g * jax.nn.sigmoid(g) * u).astype(BF16)
        _store_chunks(lambda c: y_ref.at[c], jnp.dot(hmid, wd_sc[...], preferred_element_type=F32))

    @pl.when(jnp.logical_not(used))
    def _():
        y_ref[...] = jnp.zeros_like(y_ref)


def _ffn(xs, blk_e, blk_valid, n_used, layer, wg, wu, wd, n_blocks):
    def blk(b, be, bv, nu):
        return jnp.maximum(jnp.minimum(b, nu[0] - 1), 0)

    def w_idx(b, be, bv, nu):
        return (layer, be[blk(b, be, bv, nu)], 0, 0)

    return pl.pallas_call(
        _ffn_kernel,
        name="expert_ffn",
        grid_spec=pltpu.PrefetchScalarGridSpec(
            num_scalar_prefetch=3,
            grid=(n_blocks,),
            in_specs=[
                pl.BlockSpec((ROW_CHUNKS, MOE_BLK, CHUNK_W), lambda b, be, bv, nu: (0, blk(b, be, bv, nu), 0)),
                pl.BlockSpec((None, None, D, D_EXPERT), w_idx),
                pl.BlockSpec((None, None, D, D_EXPERT), w_idx),
                pl.BlockSpec((None, None, D_EXPERT, D), w_idx),
            ],
            out_specs=pl.BlockSpec((ROW_CHUNKS, MOE_BLK, CHUNK_W), lambda b, be, bv, nu: (0, b, 0)),
            scratch_shapes=[pltpu.VMEM((D, D_EXPERT), BF16), pltpu.VMEM((D, D_EXPERT), BF16),
                            pltpu.VMEM((D_EXPERT, D), BF16)],
        ),
        out_shape=jax.ShapeDtypeStruct(xs.shape, U32),
        compiler_params=_params(("arbitrary",)),
    )(blk_e, blk_valid, n_used, xs, wg, wu, wd)


def _combine_kernel(x_ref, mod_ref, wt_ref, y_ref, sg_ref, su_ref, sd_ref, lg_ref, lb_ref, *o_refs, n_prompt_tiles):
    x = x_ref[...]
    hb = (x * (1.0 + mod_ref[4:5, :]) + mod_ref[3:4, :]).astype(BF16)
    g = jnp.dot(hb, sg_ref[...], preferred_element_type=F32)
    u = jnp.dot(hb, su_ref[...], preferred_element_type=F32)
    ff = jnp.dot((g * jax.nn.sigmoid(g) * u).astype(BF16), sd_ref[...], preferred_element_type=F32)
    for k in range(TOP_K):
        ff = ff + _load_chunks(lambda c: y_ref.at[c, k]) * wt_ref[:, k:k + 1]
    out = _layer_norm_rows(ALPHA * x + mod_ref[5:6, :] * ff, lg_ref[...], lb_ref[...])
    if len(o_refs) == 1:
        o_refs[0][...] = out
    else:
        is_prompt = pl.program_id(0) < n_prompt_tiles

        @pl.when(is_prompt)
        def _():
            o_refs[0][...] = out

        @pl.when(jnp.logical_not(is_prompt))
        def _():
            o_refs[1][...] = out


def _combine(geo, x, mod_l, wt, ytok, sg, su, sd, ln_g, ln_b, split=False):
    tb = TOK_TILE_L
    n_p = geo.t_prompt // tb
    full = lambda shape: pl.BlockSpec(shape, lambda i: (0,) * len(shape))
    if split:
        out_specs = [pl.BlockSpec((tb, D), lambda i: (jnp.minimum(i, n_p - 1), 0)),
                     pl.BlockSpec((tb, D), lambda i: (jnp.maximum(i - n_p, 0), 0))]
        out_shape = [jax.ShapeDtypeStruct((geo.t_prompt, D), F32), jax.ShapeDtypeStruct((geo.t_sample, D), F32)]
    else:
        out_specs = pl.BlockSpec((tb, D), lambda i: (i, 0))
        out_shape = jax.ShapeDtypeStruct((geo.t, D), F32)
    return pl.pallas_call(
        functools.partial(_combine_kernel, n_prompt_tiles=n_p),
        name="combine",
        grid=(geo.t // tb,),
        in_specs=[
            pl.BlockSpec((tb, D), lambda i: (i, 0)),
            pl.BlockSpec((None, 6, D), lambda i: (geo.cond_row(i, tb), 0, 0)),
            pl.BlockSpec((tb, TOP_K), lambda i: (i, 0)),
            pl.BlockSpec((ROW_CHUNKS, TOP_K, tb, CHUNK_W), lambda i: (0, 0, i, 0)),
            full((D, D_EXPERT)), full((D, D_EXPERT)), full((D_EXPERT, D)), full((1, D)), full((1, D)),
        ],
        out_specs=out_specs,
        out_shape=out_shape,
        compiler_params=_params(("arbitrary",)),
    )(x, mod_l, wt, ytok, sg.astype(BF16), su.astype(BF16), sd.astype(BF16),
      ln_g.reshape(1, D), ln_b.reshape(1, D))


def _moe_layer(geo, x, routing, mod_l, layer, wg, wu, wd, sg, su, sd, ln_g, ln_b, split=False):
    t = geo.t
    top_e, w, rank, counts, h = routing
    n_blocks = (t * TOP_K) // MOE_BLK + N_EXPERTS
    n_rows = n_blocks * MOE_BLK
    padded = (counts + MOE_BLK - 1) // MOE_BLK * MOE_BLK
    pend = jnp.cumsum(padded)
    pstart = (pend - padded).astype(jnp.int32)
    blk_e, blk_valid = _block_meta(pstart, counts, pend.astype(jnp.int32), n_blocks)
    n_used = (pend[-1:] // MOE_BLK).astype(jnp.int32)
    slots = _slots(geo, pstart, top_e, rank)
    idx = (slots.reshape(1, TOP_K * t) + (jnp.arange(ROW_CHUNKS, dtype=jnp.int32) * n_rows)[:, None]).reshape(-1)
    xs = _sc_scatter(h.reshape(ROW_CHUNKS * t, CHUNK_W), idx, ROW_CHUNKS * n_rows, TOP_K)
    yb = _ffn(xs.reshape(ROW_CHUNKS, n_rows, CHUNK_W), blk_e, blk_valid, n_used, layer, wg, wu, wd, n_blocks)
    ytok = _sc_gather(yb.reshape(ROW_CHUNKS * n_rows, CHUNK_W), idx)
    return _combine(geo, x, mod_l, w.T, ytok.reshape(ROW_CHUNKS, TOP_K, t, CHUNK_W), sg, su, sd, ln_g, ln_b,
                    split=split)


def _pos_embed(rows):
    quarter = D // 4
    omega = 1.0 / (POS_BASE ** (jnp.arange(quarter, dtype=F32) / quarter))
    r = jnp.arange(rows, dtype=F32)[:, None] * omega
    col = jnp.arange(GRID_W, dtype=F32)[:, None] * omega
    row_part = jnp.concatenate([jnp.sin(r), jnp.cos(r)], axis=-1)[:, None, :]
    col_part = jnp.concatenate([jnp.sin(col), jnp.cos(col)], axis=-1)[None, :, :]
    shape = (rows, GRID_W, 2 * quarter)
    return jnp.concatenate([jnp.broadcast_to(row_part, shape), jnp.broadcast_to(col_part, shape)],
                           axis=-1).reshape(rows * GRID_W, D)


def _mlstm_layer(geo, x, mod_l, j, a_w_in, a_b_gates, a_norm, a_w_out, ln_g, ln_b,
                 state_C, state_n, state_m, w_router, e_bias):
    q, kt, v, so, gr = _proj_a(geo, x, mod_l, a_w_in[j], a_b_gates[j])
    hp, c_p, n_p, m_p = _mlstm_scan(q, kt, v, gr, row0=0, n_seq=geo.n_prompt, seq_len=geo.prompt_len)
    ns = geo.n_sample
    n0 = jnp.pad(state_n[:, j].astype(F32)[..., None], ((0, 0),) * 4 + ((0, LANES - 1),))
    m0 = jnp.pad(state_m[:, j].astype(F32), ((0, 0), (0, 0), (0, SUBLANES - NH_A)))
    m0 = jnp.broadcast_to(m0[..., None], (ns, 2, SUBLANES, LANES))
    hs, _, _, _ = _mlstm_scan(q, kt, v, gr, row0=geo.t_prompt, n_seq=ns, seq_len=geo.sample_len,
                              state=(state_C[:, j].astype(F32), n0, m0))
    x1, routing = _out_a(geo, hp, hs, so, a_norm[j], a_w_out[j], x, mod_l, ln_g, ln_b, w_router, e_bias)
    return x1, routing, c_p, n_p[..., 0], m_p[:, :, :NH_A, 0]


def _hgrn_layer(geo, x, mod_l, j, lb_layer, b_w_in, b_norm, b_w_out, ln_g, ln_b, state_S, w_router, e_bias):
    q, pre, v, sg = _proj_b(geo, x, mod_l, b_w_in[j])
    lbd = lb_layer.reshape(2, NH_B, 1, DK_B)
    op, s_p = _hgrn_scan(q, pre, v, lbd, row0=0, n_seq=geo.n_prompt, seq_len=geo.prompt_len)
    os_, _ = _hgrn_scan(q, pre, v, lbd, row0=geo.t_prompt, n_seq=geo.n_sample, seq_len=geo.sample_len,
                        state=state_S[:, j].astype(F32))
    x1, routing = _out_b(geo, op, os_, sg, b_norm[j], b_w_out[j], x, mod_l, ln_g, ln_b, w_router, e_bias)
    return x1, routing, s_p


def kernel(x_prompt, x_sample, state_mlstm_C, state_mlstm_n, state_mlstm_m, state_hgrn_S, c, c_ctx, w_mod, b_mod, ln_g, ln_b, a_w_in, a_b_gates, a_norm, a_w_out, b_w_in, b_lb, b_norm, b_w_out, w_router, e_bias, w_gate, w_up, w_down, ws_gate, ws_up, ws_down):
    bp, sp, _ = x_prompt.shape
    bs, ss, _ = x_sample.shape
    cond = jnp.zeros((COND_ROWS, D), F32).at[0].set(c_ctx).at[1:1 + bs].set(c)
    mod = _modulation(cond, w_mod, b_mod)
    x = (x_prompt.reshape(-1, D), x_sample.reshape(-1, D), _pos_embed(ss // GRID_W))
    sm = jax.nn.softmax(b_lb.astype(F32), axis=0)
    lb_all = jnp.cumsum(sm, axis=0) - sm[0]
    geo = Geometry(bp, sp, bs, ss)
    x1, routing, new_c, new_n, new_m = _mlstm_layer(geo, x, mod[0], 0, a_w_in, a_b_gates, a_norm, a_w_out,
                                                    ln_g[0, 0], ln_b[0, 0], state_mlstm_C, state_mlstm_n,
                                                    state_mlstm_m, w_router[0], e_bias[0])
    x2 = _moe_layer(geo, x1, routing, mod[0], 0, w_gate, w_up, w_down, ws_gate[0], ws_up[0], ws_down[0],
                    ln_g[0, 1], ln_b[0, 1])
    x3, routing, new_s = _hgrn_layer(geo, x2, mod[1], 0, lb_all[1], b_w_in, b_norm, b_w_out, ln_g[1, 0], ln_b[1, 0],
                                     state_hgrn_S, w_router[1], e_bias[1])
    y_p, y_s = _moe_layer(geo, x3, routing, mod[1], 1, w_gate, w_up, w_down, ws_gate[1], ws_up[1], ws_down[1],
                          ln_g[1, 1], ln_b[1, 1], split=True)
    return (y_p.reshape(bp, sp, D), y_s.reshape(bs, ss, D), new_c[:, None], new_n[:, None], new_m[:, None],
            new_s[:, None])
```

```python
import functools
import math

import jax
import jax.numpy as jnp
from jax import lax
from jax.experimental import pallas as pl
from jax.experimental.pallas import tpu as pltpu
from jax.experimental.pallas import tpu_sc as plsc

F32 = jnp.float32
BF16 = jnp.bfloat16
HIGHEST = lax.Precision.HIGHEST

D = 1024
DEPTH = 2
GRID_W = 64
POS_BASE = 10000.0
EPS = 1e-6
ALPHA = (2.0 * DEPTH) ** 0.25
NH_A, DK_A, DV_A = 4, 128, 256
QK_A, V_A = NH_A * DK_A, NH_A * DV_A
NH_B, DK_B = 8, 128
N_EXPERTS, TOP_K, N_GROUPS, TOPK_GROUPS = 64, 8, 8, 4
GROUP_SIZE = N_EXPERTS // N_GROUPS
D_EXPERT = D // 4
ROUTED_SCALE = 2.5

LANES = 128
SUBLANES = 8
COND_ROWS = 8
TOK_TILE_L = 512
SLOT_TILE = 2048
CHUNK_A = 512
VMEM_BYTES_V7X = 64 * 1024 * 1024
VMEM_LIMIT = VMEM_BYTES_V7X - 8 * 1024 * 1024

NT_DIMS = (((1,), (1,)), ((), ()))


def _params(sem):
    return pltpu.CompilerParams(dimension_semantics=sem, vmem_limit_bytes=VMEM_LIMIT)


def _split3(x):
    hi = x.astype(BF16)
    r = x - hi.astype(F32)
    mid = r.astype(BF16)
    lo = (r - mid.astype(F32)).astype(BF16)
    return hi, mid, lo


def _dot3(a_bf, x):
    hi, mid, lo = _split3(x)
    return (jnp.dot(a_bf, hi, preferred_element_type=F32)
            + jnp.dot(a_bf, mid, preferred_element_type=F32)
            + jnp.dot(a_bf, lo, preferred_element_type=F32))


def _dot3_r(x, a_bf):
    hi, mid, lo = _split3(x)
    return (jnp.dot(hi, a_bf, preferred_element_type=F32)
            + jnp.dot(mid, a_bf, preferred_element_type=F32)
            + jnp.dot(lo, a_bf, preferred_element_type=F32))


def _dot_nt_x3(w, x):
    w_hi = w.astype(BF16)
    w_lo = (w - w_hi.astype(F32)).astype(BF16)
    x_hi = x.astype(BF16)
    x_lo = (x - x_hi.astype(F32)).astype(BF16)
    d = lambda a, b: lax.dot_general(a, b, NT_DIMS, preferred_element_type=F32)
    return d(w_hi, x_hi) + d(w_hi, x_lo) + d(w_lo, x_hi)


def _log_sigmoid(x):
    return jnp.minimum(x, 0.0) - jnp.log1p(jnp.exp(-jnp.abs(x)))


def _layer_norm_rows(x, g, b):
    mu = jnp.mean(x, axis=-1, keepdims=True)
    xc = x - mu
    var = jnp.mean(xc * xc, axis=-1, keepdims=True)
    return xc * lax.rsqrt(var + EPS) * g + b


class Geometry:
    def __init__(self, n_prompt, prompt_len, n_sample, sample_len):
        self.n_prompt, self.prompt_len = n_prompt, prompt_len
        self.n_sample, self.sample_len = n_sample, sample_len
        self.t_prompt = n_prompt * prompt_len
        self.t_sample = n_sample * sample_len
        self.t = self.t_prompt + self.t_sample
        assert self.t_prompt % TOK_TILE_L == 0 and sample_len % TOK_TILE_L == 0
        assert n_sample + 1 <= COND_ROWS

    def cond_row(self, tile, tile_rows):
        n_p = self.t_prompt // tile_rows
        return jnp.where(tile < n_p, 0, 1 + (tile - n_p) // (self.sample_len // tile_rows))


def _mod_kernel(cond_ref, w_ref, b_ref, o_ref):
    c = cond_ref[...]
    s = c * jax.nn.sigmoid(c)
    o_ref[0, 0] = jnp.dot(s, w_ref[0], precision=HIGHEST, preferred_element_type=F32) + b_ref[0, 0]


def _modulation(cond, w_mod, b_mod):
    out = pl.pallas_call(
        _mod_kernel,
        name="modulation",
        grid=(DEPTH, 6),
        in_specs=[
            pl.BlockSpec((COND_ROWS, D), lambda l, j: (0, 0)),
            pl.BlockSpec((1, D, D), lambda l, j: (l, 0, j)),
            pl.BlockSpec((1, 1, 1, D), lambda l, j: (l, j, 0, 0)),
        ],
        out_specs=pl.BlockSpec((1, 1, COND_ROWS, D), lambda l, j: (l, j, 0, 0)),
        out_shape=jax.ShapeDtypeStruct((DEPTH, 6, COND_ROWS, D), F32),
        compiler_params=_params(("arbitrary", "arbitrary")),
    )(cond, w_mod, b_mod.reshape(DEPTH, 6, 1, D))
    return out.transpose(0, 2, 1, 3)


def _embed_specs(geo, tb):
    n_p = geo.t_prompt // tb
    per_seq = geo.sample_len // tb
    return [pl.BlockSpec((tb, D), lambda i: (jnp.minimum(i, n_p - 1), 0)),
            pl.BlockSpec((tb, D), lambda i: (jnp.maximum(i - n_p, 0), 0)),
            pl.BlockSpec((tb, D), lambda i: (jnp.maximum(i - n_p, 0) % per_seq, 0))]


def _embed_tile(xp_ref, xs_ref, pos_ref, n_prompt_tiles):
    return jnp.where(pl.program_id(0) < n_prompt_tiles, xp_ref[...], xs_ref[...] + pos_ref[...])


def _proj_a_kernel(xp_ref, xs_ref, pos_ref, mod_ref, wq_ref, wkt_ref, wvo_ref, wgt_ref, bgt_ref,
                   q_ref, kt_ref, v_ref, so_ref, gr_ref, *, n_prompt_tiles):
    x = _embed_tile(xp_ref, xs_ref, pos_ref, n_prompt_tiles)
    h = x * (1.0 + mod_ref[1:2, :]) + mod_ref[0:1, :]
    hb = h.astype(BF16)
    q_ref[...] = jnp.dot(hb, wq_ref[...], preferred_element_type=F32).astype(BF16)
    kt = lax.dot_general(wkt_ref[...], hb, NT_DIMS, preferred_element_type=F32)
    kt_ref[...] = (kt * (DK_A ** -0.5)).astype(BF16)
    vo = jnp.dot(hb, wvo_ref[...], preferred_element_type=F32)
    v_ref[...] = vo[:, :V_A].astype(BF16)
    so_ref[...] = jax.nn.sigmoid(vo[:, V_A:]).astype(BF16)
    gr_ref[...] = _dot_nt_x3(wgt_ref[...], h) + bgt_ref[...]


def _proj_a(geo, x, mod_l, w_in, b_gates):
    t = geo.t
    n_gate = 4 * NH_A
    wq = w_in[:, :QK_A].astype(BF16)
    wkt = w_in[:, QK_A:2 * QK_A].T.astype(BF16)
    wvo = w_in[:, 2 * QK_A:2 * QK_A + 2 * V_A].astype(BF16)
    wg = w_in[:, 2 * QK_A + 2 * V_A:]
    bg = b_gates.reshape(n_gate).astype(F32)
    tb = TOK_TILE_L
    full = lambda shape: pl.BlockSpec(shape, lambda i: (0,) * len(shape))
    return pl.pallas_call(
        functools.partial(_proj_a_kernel, n_prompt_tiles=geo.t_prompt // tb),
        name="proj_a",
        grid=(t // tb,),
        in_specs=_embed_specs(geo, tb) + [
            pl.BlockSpec((None, 6, D), lambda i: (geo.cond_row(i, tb), 0, 0)),
            full((D, QK_A)), full((QK_A, D)), full((D, 2 * V_A)), full((n_gate, D)), full((n_gate, 1)),
        ],
        out_specs=[
            pl.BlockSpec((tb, QK_A), lambda i: (i, 0)),
            pl.BlockSpec((QK_A, tb), lambda i: (0, i)),
            pl.BlockSpec((tb, V_A), lambda i: (i, 0)),
            pl.BlockSpec((tb, V_A), lambda i: (i, 0)),
            pl.BlockSpec((n_gate, tb), lambda i: (0, i)),
        ],
        out_shape=[
            jax.ShapeDtypeStruct((t, QK_A), BF16),
            jax.ShapeDtypeStruct((QK_A, t), BF16),
            jax.ShapeDtypeStruct((t, V_A), BF16),
            jax.ShapeDtypeStruct((t, V_A), BF16),
            jax.ShapeDtypeStruct((n_gate, t), F32),
        ],
        compiler_params=_params(("parallel",)),
    )(*x, mod_l, wq, wkt, wvo, wg.T, bg.reshape(n_gate, 1))


def _mlstm_scan_kernel(*refs, chunk, has_state):
    if has_state:
        (q_ref, kt_ref, v_ref, gr_ref, c0_ref, n0_ref, m0_ref,
         h_ref, c_out, n_out, m_out, c_sc, n_sc, m_sc) = refs
    else:
        (q_ref, kt_ref, v_ref, gr_ref,
         h_ref, c_out, n_out, m_out, c_sc, n_sc, m_sc) = refs
    L = chunk
    d = pl.program_id(1)
    c = pl.program_id(2)
    fwd = d == 0

    @pl.when(c == 0)
    def _():
        if has_state:
            c_sc[...] = c0_ref[0, 0]
            n_sc[...] = n0_ref[0, 0]
            m_sc[...] = m0_ref[0, 0]
        else:
            c_sc[...] = jnp.zeros_like(c_sc)
            n_sc[...] = jnp.zeros_like(n_sc)
            m_sc[...] = jnp.zeros_like(m_sc)

    row = lax.broadcasted_iota(jnp.int32, (L, L), 0)
    col = lax.broadcasted_iota(jnp.int32, (L, L), 1)
    sgn = 1 - 2 * d
    causal = (row - col) * sgn >= 0
    tri_t = ((col - row) * sgn >= 0).astype(BF16)

    gr = gr_ref[...]
    br_all = _dot3_r(_log_sigmoid(gr), tri_t)
    bc_all = jnp.concatenate([br_all, jnp.zeros((LANES - br_all.shape[0], L), F32)], axis=0).T
    ones_blk = (lax.broadcasted_iota(jnp.int32, (L, LANES), 1) == 0).astype(BF16)

    def gate_row(direction, gate, head):
        return (direction * 2 + gate) * NH_A + head

    for h in range(NH_A):
        ff, fb = gate_row(0, 1, h), gate_row(1, 1, h)
        gi, gb = gate_row(0, 0, h), gate_row(1, 0, h)
        b_c = jnp.where(fwd, bc_all[:, ff:ff + 1], bc_all[:, fb:fb + 1])
        b_r = jnp.where(fwd, br_all[ff:ff + 1, :], br_all[fb:fb + 1, :])
        i_r = jnp.where(fwd, gr[gi:gi + 1, :], gr[gb:gb + 1, :])
        bl = jnp.where(fwd, b_r[:, L - 1:L], b_r[:, 0:1])
        q = q_ref[:, h * DK_A:(h + 1) * DK_A]
        kt = kt_ref[h * DK_A:(h + 1) * DK_A, :]
        v = v_ref[:, h * DV_A:(h + 1) * DV_A]
        m = m_sc[h:h + 1, 0:1]
        cst = c_sc[h]
        nst = n_sc[h]

        a_r = i_r - b_r
        logd = jnp.where(causal, b_c + a_r, -jnp.inf)
        inter = b_c + m
        m_t = jnp.maximum(inter, jnp.max(logd, axis=1, keepdims=True))
        dmat = jnp.exp(logd - m_t)
        e_int = jnp.exp(inter - m_t)
        s = (jnp.dot(q, kt, preferred_element_type=F32) * dmat).astype(BF16)
        num = (jnp.dot(s, v, preferred_element_type=F32)
               + e_int * jnp.dot(q, cst.astype(BF16), preferred_element_type=F32))
        den = (jnp.dot(s, ones_blk, preferred_element_type=F32)
               + e_int * jnp.dot(q, nst.astype(BF16), preferred_element_type=F32))[:, 0:1]
        h_ref[:, h * DV_A:(h + 1) * DV_A] = (num / jnp.maximum(jnp.abs(den), jnp.exp(-m_t))).astype(BF16)

        logw = bl + a_r
        m_new = jnp.maximum(bl + m, jnp.max(logw, axis=1, keepdims=True))
        w = jnp.exp(logw - m_new)
        decay = jnp.exp(bl + m - m_new)
        kw = (kt.astype(F32) * w).astype(BF16)
        c_sc[h] = decay * cst + jnp.dot(kw, v, preferred_element_type=F32)
        n_sc[h] = decay * nst + jnp.dot(kw, ones_blk, preferred_element_type=F32)
        m_sc[h:h + 1, :] = jnp.broadcast_to(m_new, (1, LANES))

    @pl.when(c == pl.num_programs(2) - 1)
    def _():
        c_out[0, 0] = c_sc[...]
        n_out[0, 0] = n_sc[...]
        m_out[0, 0] = m_sc[...]


def _mlstm_scan(q, kt, v, gr, *, row0, n_seq, seq_len, state=None):
    L = min(CHUNK_A, seq_len)
    nc = seq_len // L
    blk0 = row0 // L

    def loc_blk(b, d, c):
        return b * nc + c + d * (nc - 1 - 2 * c)

    def tok_blk(b, d, c):
        return blk0 + loc_blk(b, d, c)

    in_specs = [
        pl.BlockSpec((L, QK_A), lambda b, d, c: (tok_blk(b, d, c), 0)),
        pl.BlockSpec((QK_A, L), lambda b, d, c: (0, tok_blk(b, d, c))),
        pl.BlockSpec((L, V_A), lambda b, d, c: (tok_blk(b, d, c), 0)),
        pl.BlockSpec((4 * NH_A, L), lambda b, d, c: (0, tok_blk(b, d, c))),
    ]
    args = [q, kt, v, gr]
    if state is not None:
        in_specs += [
            pl.BlockSpec((1, 1, NH_A, DK_A, DV_A), lambda b, d, c: (b, d, 0, 0, 0)),
            pl.BlockSpec((1, 1, NH_A, DK_A, LANES), lambda b, d, c: (b, d, 0, 0, 0)),
            pl.BlockSpec((1, 1, SUBLANES, LANES), lambda b, d, c: (b, d, 0, 0)),
        ]
        args += list(state)
    return pl.pallas_call(
        functools.partial(_mlstm_scan_kernel, chunk=L, has_state=state is not None),
        name="mlstm_scan_seeded" if state is not None else "mlstm_scan",
        grid=(n_seq, 2, nc),
        in_specs=in_specs,
        out_specs=[
            pl.BlockSpec((None, L, V_A), lambda b, d, c: (d, loc_blk(b, d, c), 0)),
            pl.BlockSpec((1, 1, NH_A, DK_A, DV_A), lambda b, d, c: (b, d, 0, 0, 0)),
            pl.BlockSpec((1, 1, NH_A, DK_A, LANES), lambda b, d, c: (b, d, 0, 0, 0)),
            pl.BlockSpec((1, 1, SUBLANES, LANES), lambda b, d, c: (b, d, 0, 0)),
        ],
        out_shape=[
            jax.ShapeDtypeStruct((2, n_seq * seq_len, V_A), BF16),
            jax.ShapeDtypeStruct((n_seq, 2, NH_A, DK_A, DV_A), F32),
            jax.ShapeDtypeStruct((n_seq, 2, NH_A, DK_A, LANES), F32),
            jax.ShapeDtypeStruct((n_seq, 2, SUBLANES, LANES), F32),
        ],
        scratch_shapes=[
            pltpu.VMEM((NH_A, DK_A, DV_A), F32),
            pltpu.VMEM((NH_A, DK_A, LANES), F32),
            pltpu.VMEM((SUBLANES, LANES), F32),
        ],
        compiler_params=_params(("parallel", "parallel", "arbitrary")),
    )(*args)


def _out_a_kernel(hp_ref, hs_ref, so_ref, nw_ref, w_ref, xp_ref, xs_ref, pos_ref, mod_ref, lg_ref, lb_ref,
                  wrt_ref, eb_ref, o_ref, *route_refs, n_prompt_tiles):
    is_prompt = pl.program_id(0) < n_prompt_tiles
    x = _embed_tile(xp_ref, xs_ref, pos_ref, n_prompt_tiles)
    y = jnp.where(is_prompt, hp_ref[0].astype(F32) + hp_ref[1].astype(F32),
                  hs_ref[0].astype(F32) + hs_ref[1].astype(F32))
    parts = []
    for h in range(NH_A):
        yh = y[:, h * DV_A:(h + 1) * DV_A]
        mu = jnp.mean(yh, axis=-1, keepdims=True)
        yc = yh - mu
        var = jnp.mean(yc * yc, axis=-1, keepdims=True)
        parts.append(yc * lax.rsqrt(var + EPS))
    yn = jnp.concatenate(parts, axis=-1) * nw_ref[...] * so_ref[...].astype(F32)
    out = jnp.dot(yn.astype(BF16), w_ref[...], preferred_element_type=F32)
    x1 = _layer_norm_rows(ALPHA * x + mod_ref[2:3, :] * out, lg_ref[...], lb_ref[...])
    o_ref[...] = x1
    _route_tile(x1, mod_ref, wrt_ref, eb_ref, *route_refs)


def _out_a(geo, h_prompt, h_sample, so, norm_w, w_out, x, mod_l, ln_g, ln_b, w_router, e_bias):
    t = geo.t
    tb = TOK_TILE_L
    n_p = geo.t_prompt // tb
    full = lambda shape: pl.BlockSpec(shape, lambda i: (0,) * len(shape))
    rio = _RouterIO(t, tb, w_router, e_bias)
    outs = pl.pallas_call(
        functools.partial(_out_a_kernel, n_prompt_tiles=n_p),
        name="out_a",
        grid=(t // tb,),
        in_specs=[
            pl.BlockSpec((2, tb, V_A), lambda i: (0, jnp.minimum(i, n_p - 1), 0)),
            pl.BlockSpec((2, tb, V_A), lambda i: (0, jnp.maximum(i - n_p, 0), 0)),
            pl.BlockSpec((tb, V_A), lambda i: (i, 0)),
            full((1, V_A)), full((V_A, D)),
        ] + _embed_specs(geo, tb) + [
            pl.BlockSpec((None, 6, D), lambda i: (geo.cond_row(i, tb), 0, 0)),
            full((1, D)), full((1, D)),
        ] + rio.in_specs,
        out_specs=[pl.BlockSpec((tb, D), lambda i: (i, 0))] + rio.out_specs,
        out_shape=[jax.ShapeDtypeStruct((t, D), F32)] + rio.out_shape,
        scratch_shapes=rio.scratch,
        compiler_params=_params(("arbitrary",)),
    )(h_prompt, h_sample, so, norm_w.reshape(1, V_A).astype(F32), w_out.astype(BF16), *x, mod_l,
      ln_g.reshape(1, D), ln_b.reshape(1, D), *rio.inputs)
    return outs[0], _RouterIO.unpack(outs[1:])


def _proj_b_kernel(x_ref, mod_ref, w_ref, q_ref, pre_ref, v_ref, sg_ref):
    h = x_ref[...] * (1.0 + mod_ref[1:2, :]) + mod_ref[0:1, :]
    z = jnp.dot(h.astype(BF16), w_ref[...], preferred_element_type=F32)
    for hd in range(NH_B):
        lo = hd * DK_B
        qh = z[:, lo:lo + DK_B]
        q_ref[hd] = qh * jax.nn.sigmoid(qh)
        pre_ref[0, hd] = z[:, D + lo:D + lo + DK_B]
        pre_ref[1, hd] = z[:, 2 * D + lo:2 * D + lo + DK_B]
        v_ref[hd] = z[:, 3 * D + lo:3 * D + lo + DK_B].astype(BF16)
    g = z[:, 4 * D:]
    sg_ref[...] = (g * jax.nn.sigmoid(g)).astype(BF16)


def _proj_b(geo, x, mod_l, w_in):
    t = geo.t
    tb = TOK_TILE_L
    return pl.pallas_call(
        _proj_b_kernel,
        name="proj_b",
        grid=(t // tb,),
        in_specs=[
            pl.BlockSpec((tb, D), lambda i: (i, 0)),
            pl.BlockSpec((None, 6, D), lambda i: (geo.cond_row(i, tb), 0, 0)),
            pl.BlockSpec((D, 5 * D), lambda i: (0, 0), pipeline_mode=pl.Buffered(1)),
        ],
        out_specs=[
            pl.BlockSpec((NH_B, tb, DK_B), lambda i: (0, i, 0)),
            pl.BlockSpec((2, NH_B, tb, DK_B), lambda i: (0, 0, i, 0)),
            pl.BlockSpec((NH_B, tb, DK_B), lambda i: (0, i, 0)),
            pl.BlockSpec((tb, D), lambda i: (i, 0)),
        ],
        out_shape=[
            jax.ShapeDtypeStruct((NH_B, t, DK_B), F32),
            jax.ShapeDtypeStruct((2, NH_B, t, DK_B), F32),
            jax.ShapeDtypeStruct((NH_B, t, DK_B), BF16),
            jax.ShapeDtypeStruct((t, D), BF16),
        ],
        compiler_params=_params(("parallel",)),
    )(x, mod_l, w_in.astype(BF16))


CHUNK_B = 256
BAND = SUBLANES // 2
TN_DIMS = (((0,), (0,)), ((), ()))


def _hgrn_head(q, pre, lbv, v_bf, st, fwd):
    L = q.shape[0]
    sg = jax.nn.sigmoid(pre)
    f = lbv + (1.0 - lbv) * sg
    lf = jnp.log(f)
    kk = (1.0 - lbv) * (1.0 - sg)
    row = lax.broadcasted_iota(jnp.int32, (L, L), 0)
    col = lax.broadcasted_iota(jnp.int32, (L, L), 1)
    tri = ((row >= col) if fwd else (row <= col)).astype(BF16)
    b = _dot3(tri, lf)
    tpos = lax.broadcasted_iota(jnp.int32, (L, DK_B), 0)
    blk_bits = row ^ col
    lag = jnp.where(blk_bits < BAND, (row - col) if fwd else (col - row), -1)

    step = 1 if fwd else L - 1
    att = jnp.where(lag == 0, jnp.sum(q * kk, axis=1, keepdims=True), 0.0)
    f_r, kk_r, g = f, kk, f
    for dl in range(1, BAND):
        if dl > 1:
            f_r = pltpu.roll(f_r, step, 0)
            g = g * f_r
        kk_r = pltpu.roll(kk_r, step, 0)
        att = jnp.where(lag == dl, jnp.sum(q * kk_r * g, axis=1, keepdims=True), att)

    w = BAND
    while w < L:
        nb = L // (2 * w)
        b3 = b.reshape(nb, 2 * w, DK_B)
        edge = (b3[:, w - 1:w, :] if fwd else b3[:, w:w + 1, :])
        bmid = jnp.broadcast_to(edge, (nb, 2 * w, DK_B)).reshape(L, DK_B)
        second = (tpos & w) != 0
        t_side = second if fwd else jnp.logical_not(second)
        e = jnp.exp(jnp.where(t_side, b - bmid, bmid - b))
        qt = jnp.where(t_side, q * e, 0.0).astype(BF16)
        ks = jnp.where(t_side, 0.0, kk * e).astype(BF16)
        a = lax.dot_general(qt, ks, NT_DIMS, preferred_element_type=F32)
        att = att + jnp.where(blk_bits < 2 * w, a, 0.0)
        w *= 2
    o = jnp.dot(att.astype(BF16), v_bf, preferred_element_type=F32)

    bl = b[L - 1:L, :] if fwd else b[0:1, :]
    o = o + lax.dot_general((q * jnp.exp(b)).astype(BF16), st.astype(BF16), NT_DIMS, preferred_element_type=F32)
    kd = (kk * jnp.exp(bl - b)).astype(BF16)
    st_new = jnp.exp(bl) * st + lax.dot_general(v_bf, kd, TN_DIMS, preferred_element_type=F32)
    return o, st_new


def _hgrn_scan_kernel(*refs, has_state):
    if has_state:
        q_ref, pre_ref, v_ref, lb_ref, s0_ref, o_ref, s_out, st_sc = refs
    else:
        q_ref, pre_ref, v_ref, lb_ref, o_ref, s_out, st_sc = refs
    d = pl.program_id(1)
    c = pl.program_id(2)

    @pl.when(c == 0)
    def _():
        if has_state:
            for hd in range(NH_B):
                st_sc[hd] = s0_ref[0, 0, hd].T
        else:
            st_sc[...] = jnp.zeros_like(st_sc)

    def run(fwd):
        def head(hd, carry):
            o, st_new = _hgrn_head(q_ref[hd], pre_ref[hd], lb_ref[hd], v_ref[hd], st_sc[hd], fwd)
            o_ref[hd] = o.astype(BF16)
            st_sc[hd] = st_new
            return carry
        lax.fori_loop(0, NH_B, head, 0, unroll=8)

    @pl.when(d == 0)
    def _():
        run(True)

    @pl.when(d == 1)
    def _():
        run(False)

    @pl.when(c == pl.num_programs(2) - 1)
    def _():
        for hd in range(NH_B):
            s_out[0, 0, hd] = st_sc[hd].T


def _hgrn_scan(q, pre, v, lbd, *, row0, n_seq, seq_len, state=None):
    L = CHUNK_B
    nc = seq_len // L
    blk0 = row0 // L

    def loc_blk(b, d, c):
        return b * nc + c + d * (nc - 1 - 2 * c)

    def tok_blk(b, d, c):
        return blk0 + loc_blk(b, d, c)

    in_specs = [
        pl.BlockSpec((NH_B, L, DK_B), lambda b, d, c: (0, tok_blk(b, d, c), 0)),
        pl.BlockSpec((None, NH_B, L, DK_B), lambda b, d, c: (d, 0, tok_blk(b, d, c), 0)),
        pl.BlockSpec((NH_B, L, DK_B), lambda b, d, c: (0, tok_blk(b, d, c), 0)),
        pl.BlockSpec((None, NH_B, 1, DK_B), lambda b, d, c: (d, 0, 0, 0)),
    ]
    args = [q, pre, v, lbd]
    if state is not None:
        in_specs.append(pl.BlockSpec((1, 1, NH_B, DK_B, DK_B), lambda b, d, c: (b, d, 0, 0, 0)))
        args.append(state)
    return pl.pallas_call(
        functools.partial(_hgrn_scan_kernel, has_state=state is not None),
        name="hgrn_scan_seeded" if state is not None else "hgrn_scan",
        grid=(n_seq, 2, nc),
        in_specs=in_specs,
        out_specs=[
            pl.BlockSpec((None, NH_B, L, DK_B), lambda b, d, c: (d, 0, loc_blk(b, d, c), 0)),
            pl.BlockSpec((1, 1, NH_B, DK_B, DK_B), lambda b, d, c: (b, d, 0, 0, 0)),
        ],
        out_shape=[
            jax.ShapeDtypeStruct((2, NH_B, n_seq * seq_len, DK_B), BF16),
            jax.ShapeDtypeStruct((n_seq, 2, NH_B, DK_B, DK_B), F32),
        ],
        scratch_shapes=[pltpu.VMEM((NH_B, DK_B, DK_B), F32)],
        compiler_params=_params(("parallel", "parallel", "arbitrary")),
    )(*args)


def _out_b_kernel(op_ref, os_ref, sg_ref, nw_ref, w_ref, x_ref, mod_ref, lg_ref, lb_ref, wrt_ref, eb_ref,
                  out_ref, *route_refs, n_prompt_tiles):
    is_prompt = pl.program_id(0) < n_prompt_tiles
    parts = []
    for hd in range(NH_B):
        y = jnp.where(is_prompt, op_ref[0, hd].astype(F32) + op_ref[1, hd].astype(F32),
                      os_ref[0, hd].astype(F32) + os_ref[1, hd].astype(F32))
        parts.append(y * lax.rsqrt(jnp.mean(y * y, axis=-1, keepdims=True) + EPS))
    yn = jnp.concatenate(parts, axis=-1) * nw_ref[...] * sg_ref[...].astype(F32)
    out = jnp.dot(yn.astype(BF16), w_ref[...], preferred_element_type=F32)
    x1 = _layer_norm_rows(ALPHA * x_ref[...] + mod_ref[2:3, :] * out, lg_ref[...], lb_ref[...])
    out_ref[...] = x1
    _route_tile(x1, mod_ref, wrt_ref, eb_ref, *route_refs)


def _out_b(geo, o_prompt, o_sample, sg, norm_w, w_out, x, mod_l, ln_g, ln_b, w_router, e_bias):
    t = geo.t
    tb = TOK_TILE_L
    n_p = geo.t_prompt // tb
    full = lambda shape: pl.BlockSpec(shape, lambda i: (0,) * len(shape))
    rio = _RouterIO(t, tb, w_router, e_bias)
    outs = pl.pallas_call(
        functools.partial(_out_b_kernel, n_prompt_tiles=n_p),
        name="out_b",
        grid=(t // tb,),
        in_specs=[
            pl.BlockSpec((2, NH_B, tb, DK_B), lambda i: (0, 0, jnp.minimum(i, n_p - 1), 0)),
            pl.BlockSpec((2, NH_B, tb, DK_B), lambda i: (0, 0, jnp.maximum(i - n_p, 0), 0)),
            pl.BlockSpec((tb, D), lambda i: (i, 0)),
            full((1, D)), full((D, D)),
            pl.BlockSpec((tb, D), lambda i: (i, 0)),
            pl.BlockSpec((None, 6, D), lambda i: (geo.cond_row(i, tb), 0, 0)),
            full((1, D)), full((1, D)),
        ] + rio.in_specs,
        out_specs=[pl.BlockSpec((tb, D), lambda i: (i, 0))] + rio.out_specs,
        out_shape=[jax.ShapeDtypeStruct((t, D), F32)] + rio.out_shape,
        scratch_shapes=rio.scratch,
        compiler_params=_params(("arbitrary",)),
    )(o_prompt, o_sample, sg, norm_w.reshape(1, D).astype(F32), w_out.astype(BF16), x, mod_l,
      ln_g.reshape(1, D), ln_b.reshape(1, D), *rio.inputs)
    return outs[0], _RouterIO.unpack(outs[1:])


MOE_BLK = 1024
U32 = jnp.uint32
ROW_WORDS = D // 2
CHUNK_W = 256
ROW_CHUNKS = ROW_WORDS // CHUNK_W
SC_WINDOW = 128


def _pack_rows(x):
    return pltpu.pack_elementwise([x[:, :ROW_WORDS], x[:, ROW_WORDS:]], packed_dtype=BF16)


def _unpack_rows(words):
    return jnp.concatenate([pltpu.unpack_elementwise(words, index=i, packed_dtype=BF16, unpacked_dtype=F32)
                            for i in range(2)], axis=1)


def _store_chunks(chunk_ref, x):
    words = _pack_rows(x)
    for c in range(ROW_CHUNKS):
        chunk_ref(c)[...] = words[:, c * CHUNK_W:(c + 1) * CHUNK_W]


def _load_chunks(chunk_ref, valid_rows=None):
    words = jnp.concatenate([chunk_ref(c)[...] for c in range(ROW_CHUNKS)], axis=1)
    if valid_rows is not None:
        row = lax.broadcasted_iota(jnp.int32, (words.shape[0], 1), 0)
        words = jnp.where(row < valid_rows, words, jnp.uint32(0))
    return _unpack_rows(words)


def _first_index(hit, iota, size, axis):
    return jnp.min(jnp.where(hit, iota, size), axis=axis, keepdims=True)


def _route_tile(x, mod_ref, wrt_ref, eb_ref, e_ref, w_ref, r_ref, cnt_ref, h_ref, cnt_sc):
    i = pl.program_id(0)
    tb = x.shape[0]

    @pl.when(i == 0)
    def _():
        cnt_sc[...] = jnp.zeros_like(cnt_sc)

    h = x * (1.0 + mod_ref[4:5, :]) + mod_ref[3:4, :]
    _store_chunks(lambda c: h_ref.at[c], h)
    logits = _dot_nt_x3(wrt_ref[...], h)
    scores = jax.nn.sigmoid(logits)
    sel = scores + eb_ref[...]

    g3 = sel.reshape(N_GROUPS, GROUP_SIZE, tb)
    io3 = lax.broadcasted_iota(jnp.int32, g3.shape, 1)
    m1 = jnp.max(g3, axis=1, keepdims=True)
    first = _first_index(g3 == m1, io3, GROUP_SIZE, 1)
    m2 = jnp.max(jnp.where(io3 == first, -jnp.inf, g3), axis=1, keepdims=True)
    gscore = (m1 + m2).reshape(N_GROUPS, tb)

    iog = lax.broadcasted_iota(jnp.int32, gscore.shape, 0)
    gmask = jnp.zeros(gscore.shape, F32)
    for _ in range(TOPK_GROUPS):
        gm = jnp.max(gscore, axis=0, keepdims=True)
        pick = iog == _first_index(gscore == gm, iog, N_GROUPS, 0)
        gmask = jnp.where(pick, 1.0, gmask)
        gscore = jnp.where(pick, -jnp.inf, gscore)
    emask = jnp.broadcast_to(gmask.reshape(N_GROUPS, 1, tb), (N_GROUPS, GROUP_SIZE, tb)).reshape(N_EXPERTS, tb)
    cand = jnp.where(emask > 0.0, sel, -jnp.inf)

    ioe = lax.broadcasted_iota(jnp.int32, cand.shape, 0)
    picks, wts = [], []
    onehot = jnp.zeros(cand.shape, F32)
    for _ in range(TOP_K):
        cm = jnp.max(cand, axis=0, keepdims=True)
        idx = _first_index(cand == cm, ioe, N_EXPERTS, 0)
        pick = ioe == idx
        picks.append(pick)
        wts.append(jnp.sum(jnp.where(pick, scores, 0.0), axis=0, keepdims=True))
        onehot = onehot + pick.astype(F32)
        cand = jnp.where(pick, -jnp.inf, cand)
        e_ref[pl.ds(len(picks) - 1, 1), :] = idx
    wsum = wts[0]
    for wk in wts[1:]:
        wsum = wsum + wk
    for k in range(TOP_K):
        w_ref[pl.ds(k, 1), :] = wts[k] / wsum * ROUTED_SCALE

    r_io = lax.broadcasted_iota(jnp.int32, (tb, tb), 0)
    c_io = lax.broadcasted_iota(jnp.int32, (tb, tb), 1)
    before = (r_io < c_io).astype(BF16)
    rank = cnt_sc[:, 0:1] + jnp.dot(onehot.astype(BF16), before, preferred_element_type=F32)
    for k in range(TOP_K):
        r_ref[pl.ds(k, 1), :] = jnp.sum(jnp.where(picks[k], rank, 0.0), axis=0, keepdims=True).astype(jnp.int32)
    cnt_sc[...] = cnt_sc[...] + jnp.sum(onehot, axis=1, keepdims=True)
    cnt_ref[...] = cnt_sc[...]


class _RouterIO:
    def __init__(self, t, tb, w_router, e_bias):
        full = lambda shape: pl.BlockSpec(shape, lambda i: (0,) * len(shape))
        self.inputs = [w_router.T.astype(F32), e_bias.reshape(N_EXPERTS, 1).astype(F32)]
        self.in_specs = [full((N_EXPERTS, D)), full((N_EXPERTS, 1))]
        self.out_specs = [
            pl.BlockSpec((TOP_K, tb), lambda i: (0, i)),
            pl.BlockSpec((TOP_K, tb), lambda i: (0, i)),
            pl.BlockSpec((TOP_K, tb), lambda i: (0, i)),
            full((N_EXPERTS, LANES)),
            pl.BlockSpec((ROW_CHUNKS, tb, CHUNK_W), lambda i: (0, i, 0)),
        ]
        self.out_shape = [
            jax.ShapeDtypeStruct((TOP_K, t), jnp.int32),
            jax.ShapeDtypeStruct((TOP_K, t), F32),
            jax.ShapeDtypeStruct((TOP_K, t), jnp.int32),
            jax.ShapeDtypeStruct((N_EXPERTS, LANES), F32),
            jax.ShapeDtypeStruct((ROW_CHUNKS, t, CHUNK_W), U32),
        ]
        self.scratch = [pltpu.VMEM((N_EXPERTS, LANES), F32)]

    @staticmethod
    def unpack(outs):
        e, w, r, cnt, h = outs
        return e, w, r, cnt[:, 0].astype(jnp.int32), h


def _slot_kernel(pstart_ref, e_ref, r_ref, o_ref):
    e = e_ref[...]
    slot = r_ref[...]
    for x in range(N_EXPERTS):
        slot = slot + jnp.where(e == x, pstart_ref[x], 0)
    o_ref[...] = slot


def _slots(geo, pstart, top_e, rank):
    tb = math.gcd(SLOT_TILE, geo.t)
    return pl.pallas_call(
        _slot_kernel,
        name="slots",
        grid_spec=pltpu.PrefetchScalarGridSpec(
            num_scalar_prefetch=1,
            grid=(geo.t // tb,),
            in_specs=[pl.BlockSpec((TOP_K, tb), lambda i, p: (0, i)),
                      pl.BlockSpec((TOP_K, tb), lambda i, p: (0, i))],
            out_specs=pl.BlockSpec((TOP_K, tb), lambda i, p: (0, i)),
        ),
        out_shape=jax.ShapeDtypeStruct((TOP_K, geo.t), jnp.int32),
        compiler_params=_params(("parallel",)),
    )(pstart, top_e, rank)


def _block_meta_kernel(pstart_ref, counts_ref, pend_ref, e_ref, v_ref):
    row0 = lax.broadcasted_iota(jnp.int32, e_ref.shape, 1) * MOE_BLK
    blk_e = jnp.zeros(e_ref.shape, jnp.int32)
    for x in range(N_EXPERTS):
        blk_e = blk_e + jnp.where(pend_ref[x] <= row0, 1, 0)
    blk_e = jnp.minimum(blk_e, N_EXPERTS - 1)
    last = jnp.zeros(e_ref.shape, jnp.int32)
    for x in range(N_EXPERTS):
        last = last + jnp.where(blk_e == x, pstart_ref[x] + counts_ref[x], 0)
    e_ref[...] = blk_e
    v_ref[...] = jnp.clip(last - row0, 0, MOE_BLK)


def _block_meta(pstart, counts, pend, n_blocks):
    e, v = pl.pallas_call(
        _block_meta_kernel,
        name="block_meta",
        grid_spec=pltpu.PrefetchScalarGridSpec(
            num_scalar_prefetch=3,
            grid=(1,),
            in_specs=[],
            out_specs=[pl.BlockSpec((1, n_blocks), lambda i, a, b, c: (0, 0)),
                       pl.BlockSpec((1, n_blocks), lambda i, a, b, c: (0, 0))],
        ),
        out_shape=[jax.ShapeDtypeStruct((1, n_blocks), jnp.int32), jax.ShapeDtypeStruct((1, n_blocks), jnp.int32)],
        compiler_params=_params(("arbitrary",)),
    )(pstart, counts, pend)
    return e[0], v[0]


def _sc_mesh():
    return plsc.VectorSubcoreMesh(core_axis_name="core", subcore_axis_name="subcore")


def _sc_scatter(rows, idx, n_out, copies):
    n_src = rows.shape[0]
    n_idx = idx.shape[0]
    groups = ROW_CHUNKS
    win_per_group = n_src // groups // SC_WINDOW

    def idx_block(w, k):
        return (0, ((w // win_per_group) * copies + k) * win_per_group + w % win_per_group)

    @pl.kernel(out_type=jax.ShapeDtypeStruct((n_out, CHUNK_W), rows.dtype), mesh=_sc_mesh(), scratch_types=[],
               name="sc_dispatch")
    def scatter(x_hbm, i_hbm, o_hbm):
        def body(x_vmem, *i_vmems):
            for i_vmem in i_vmems:
                pltpu.sync_copy(x_vmem, o_hbm.at[i_vmem.at[0]])

        pltpu.emit_pipeline(
            body,
            grid=(n_src // SC_WINDOW,),
            in_specs=[pl.BlockSpec((SC_WINDOW, CHUNK_W), index_map=lambda w: (w, 0))]
            + [pl.BlockSpec((1, SC_WINDOW), index_map=functools.partial(idx_block, k=k)) for k in range(copies)],
            out_specs=[],
            core_axis_name=("core", "subcore"),
            dimension_semantics=(pltpu.PARALLEL,),
        )(x_hbm, *([i_hbm] * copies))

    return scatter(rows, idx.reshape(1, n_idx))


def _sc_gather(table, idx):
    n_idx = idx.shape[0]

    @pl.kernel(out_type=jax.ShapeDtypeStruct((n_idx, CHUNK_W), table.dtype), mesh=_sc_mesh(),
               name="sc_combine_gather")
    def gather(t_hbm, i_hbm, o_hbm):
        def body(i_vmem, o_vmem):
            pltpu.sync_copy(t_hbm.at[i_vmem.at[0]], o_vmem)

        pltpu.emit_pipeline(
            body,
            grid=(n_idx // SC_WINDOW,),
            in_specs=[pl.BlockSpec((1, SC_WINDOW), index_map=lambda w: (0, w))],
            out_specs=[pl.BlockSpec((SC_WINDOW, CHUNK_W), index_map=lambda w: (w, 0))],
            core_axis_name=("core", "subcore"),
            dimension_semantics=(pltpu.PARALLEL,),
        )(i_hbm, o_hbm)

    return gather(table, idx.reshape(1, n_idx))


def _ffn_step(b, blk_e_ref, blk_valid_ref, n_used_ref, xs_ref, wg_ref, wu_ref, wd_ref, y_ref, wg_sc, wu_sc, wd_sc):
    used = b < n_used_ref[0]
    new_expert = (b == 0) | (blk_e_ref[b] != blk_e_ref[jnp.maximum(b - 1, 0)])

    @pl.when(used & new_expert)
    def _():
        wg_sc[...] = wg_ref[...].astype(BF16)
        wu_sc[...] = wu_ref[...].astype(BF16)
        wd_sc[...] = wd_ref[...].astype(BF16)

    @pl.when(used)
    def _():
        x = _load_chunks(lambda c: xs_ref.at[c], valid_rows=blk_valid_ref[b]).astype(BF16)
        g = jnp.dot(x, wg_sc[...], preferred_element_type=F32)
        u = jnp.dot(x, wu_sc[...], preferred_element_type=F32)
        hmid = (g * jax.nn.sigmoid(g) * u).astype(BF16)
        _store_chunks(lambda c: y_ref.at[c], jnp.dot(hmid, wd_sc[...], preferred_element_type=F32))

    @pl.when(jnp.logical_not(used))
    def _():
        y_ref[...] = jnp.zeros_like(y_ref)


FFN_INPUT_BUFFERS = 3


def _ffn_kernel(blk_e_ref, blk_valid_ref, n_used_ref, xs_hbm, wg_hbm, wu_hbm, wd_hbm, y_hbm,
                wg_sc, wu_sc, wd_sc, step_sm, *, layer, n_blocks):
    step_sm[0] = 0

    def blk(b):
        return jnp.maximum(jnp.minimum(b, n_used_ref[0] - 1), 0)

    def w_idx(b):
        return (layer, blk_e_ref[blk(b)], 0, 0)

    def body(xs_ref, wg_ref, wu_ref, wd_ref, y_ref):
        b = step_sm[0]
        step_sm[0] = b + 1
        _ffn_step(b, blk_e_ref, blk_valid_ref, n_used_ref, xs_ref, wg_ref, wu_ref, wd_ref, y_ref,
                  wg_sc, wu_sc, wd_sc)

    rows = (ROW_CHUNKS, MOE_BLK, CHUNK_W)
    pltpu.emit_pipeline(
        body,
        grid=(n_blocks,),
        in_specs=[
            pl.BlockSpec(rows, lambda b: (0, blk(b), 0), pipeline_mode=pl.Buffered(FFN_INPUT_BUFFERS)),
            pl.BlockSpec((None, None, D, D_EXPERT), w_idx),
            pl.BlockSpec((None, None, D, D_EXPERT), w_idx),
            pl.BlockSpec((None, None, D_EXPERT, D), w_idx),
        ],
        out_specs=[pl.BlockSpec(rows, lambda b: (0, b, 0))],
    )(xs_hbm, wg_hbm, wu_hbm, wd_hbm, y_hbm)


def _ffn(xs, blk_e, blk_valid, n_used, layer, wg, wu, wd, n_blocks):
    smem = pl.BlockSpec(memory_space=pltpu.SMEM)
    hbm = pl.BlockSpec(memory_space=pl.ANY)
    return pl.pallas_call(
        functools.partial(_ffn_kernel, layer=layer, n_blocks=n_blocks),
        name="expert_ffn",
        in_specs=[smem, smem, smem, hbm, hbm, hbm, hbm],
        out_specs=hbm,
        out_shape=jax.ShapeDtypeStruct(xs.shape, U32),
        scratch_shapes=[pltpu.VMEM((D, D_EXPERT), BF16), pltpu.VMEM((D, D_EXPERT), BF16),
                        pltpu.VMEM((D_EXPERT, D), BF16), pltpu.SMEM((1,), jnp.int32)],
        compiler_params=pltpu.CompilerParams(vmem_limit_bytes=VMEM_LIMIT),
    )(blk_e, blk_valid, n_used, xs, wg, wu, wd)


def _combine_kernel(x_ref, mod_ref, wt_ref, y_ref, sg_ref, su_ref, sd_ref, lg_ref, lb_ref, *o_refs, n_prompt_tiles):
    x = x_ref[...]
    hb = (x * (1.0 + mod_ref[4:5, :]) + mod_ref[3:4, :]).astype(BF16)
    g = jnp.dot(hb, sg_ref[...], preferred_element_type=F32)
    u = jnp.dot(hb, su_ref[...], preferred_element_type=F32)
    ff = jnp.dot((g * jax.nn.sigmoid(g) * u).astype(BF16), sd_ref[...], preferred_element_type=F32)
    for k in range(TOP_K):
        ff = ff + _load_chunks(lambda c: y_ref.at[c, k]) * wt_ref[:, k:k + 1]
    out = _layer_norm_rows(ALPHA * x + mod_ref[5:6, :] * ff, lg_ref[...], lb_ref[...])
    if len(o_refs) == 1:
        o_refs[0][...] = out
    else:
        is_prompt = pl.program_id(0) < n_prompt_tiles

        @pl.when(is_prompt)
        def _():
            o_refs[0][...] = out

        @pl.when(jnp.logical_not(is_prompt))
        def _():
            o_refs[1][...] = out


def _combine(geo, x, mod_l, wt, ytok, sg, su, sd, ln_g, ln_b, split=False):
    tb = TOK_TILE_L
    n_p = geo.t_prompt // tb
    full = lambda shape: pl.BlockSpec(shape, lambda i: (0,) * len(shape))
    if split:
        out_specs = [pl.BlockSpec((tb, D), lambda i: (jnp.minimum(i, n_p - 1), 0)),
                     pl.BlockSpec((tb, D), lambda i: (jnp.maximum(i - n_p, 0), 0))]
        out_shape = [jax.ShapeDtypeStruct((geo.t_prompt, D), F32), jax.ShapeDtypeStruct((geo.t_sample, D), F32)]
    else:
        out_specs = pl.BlockSpec((tb, D), lambda i: (i, 0))
        out_shape = jax.ShapeDtypeStruct((geo.t, D), F32)
    return pl.pallas_call(
        functools.partial(_combine_kernel, n_prompt_tiles=n_p),
        name="combine",
        grid=(geo.t // tb,),
        in_specs=[
            pl.BlockSpec((tb, D), lambda i: (i, 0)),
            pl.BlockSpec((None, 6, D), lambda i: (geo.cond_row(i, tb), 0, 0)),
            pl.BlockSpec((tb, TOP_K), lambda i: (i, 0)),
            pl.BlockSpec((ROW_CHUNKS, TOP_K, tb, CHUNK_W), lambda i: (0, 0, i, 0)),
            full((D, D_EXPERT)), full((D, D_EXPERT)), full((D_EXPERT, D)), full((1, D)), full((1, D)),
        ],
        out_specs=out_specs,
        out_shape=out_shape,
        compiler_params=_params(("arbitrary",)),
    )(x, mod_l, wt, ytok, sg.astype(BF16), su.astype(BF16), sd.astype(BF16),
      ln_g.reshape(1, D), ln_b.reshape(1, D))


def _moe_layer(geo, x, routing, mod_l, layer, wg, wu, wd, sg, su, sd, ln_g, ln_b, split=False):
    t = geo.t
    top_e, w, rank, counts, h = routing
    n_blocks = (t * TOP_K) // MOE_BLK + N_EXPERTS
    n_rows = n_blocks * MOE_BLK
    padded = (counts + MOE_BLK - 1) // MOE_BLK * MOE_BLK
    pend = jnp.cumsum(padded)
    pstart = (pend - padded).astype(jnp.int32)
    blk_e, blk_valid = _block_meta(pstart, counts, pend.astype(jnp.int32), n_blocks)
    n_used = (pend[-1:] // MOE_BLK).astype(jnp.int32)
    slots = _slots(geo, pstart, top_e, rank)
    idx = (slots.reshape(1, TOP_K * t) + (jnp.arange(ROW_CHUNKS, dtype=jnp.int32) * n_rows)[:, None]).reshape(-1)
    xs = _sc_scatter(h.reshape(ROW_CHUNKS * t, CHUNK_W), idx, ROW_CHUNKS * n_rows, TOP_K)
    yb = _ffn(xs.reshape(ROW_CHUNKS, n_rows, CHUNK_W), blk_e, blk_valid, n_used, layer, wg, wu, wd, n_blocks)
    ytok = _sc_gather(yb.reshape(ROW_CHUNKS * n_rows, CHUNK_W), idx)
    return _combine(geo, x, mod_l, w.T, ytok.reshape(ROW_CHUNKS, TOP_K, t, CHUNK_W), sg, su, sd, ln_g, ln_b,
                    split=split)


def _pos_embed(rows):
    quarter = D // 4
    omega = 1.0 / (POS_BASE ** (jnp.arange(quarter, dtype=F32) / quarter))
    r = jnp.arange(rows, dtype=F32)[:, None] * omega
    col = jnp.arange(GRID_W, dtype=F32)[:, None] * omega
    row_part = jnp.concatenate([jnp.sin(r), jnp.cos(r)], axis=-1)[:, None, :]
    col_part = jnp.concatenate([jnp.sin(col), jnp.cos(col)], axis=-1)[None, :, :]
    shape = (rows, GRID_W, 2 * quarter)
    return jnp.concatenate([jnp.broadcast_to(row_part, shape), jnp.broadcast_to(col_part, shape)],
                           axis=-1).reshape(rows * GRID_W, D)


def _mlstm_layer(geo, x, mod_l, j, a_w_in, a_b_gates, a_norm, a_w_out, ln_g, ln_b,
                 state_C, state_n, state_m, w_router, e_bias):
    q, kt, v, so, gr = _proj_a(geo, x, mod_l, a_w_in[j], a_b_gates[j])
    hp, c_p, n_p, m_p = _mlstm_scan(q, kt, v, gr, row0=0, n_seq=geo.n_prompt, seq_len=geo.prompt_len)
    ns = geo.n_sample
    n0 = jnp.pad(state_n[:, j].astype(F32)[..., None], ((0, 0),) * 4 + ((0, LANES - 1),))
    m0 = jnp.pad(state_m[:, j].astype(F32), ((0, 0), (0, 0), (0, SUBLANES - NH_A)))
    m0 = jnp.broadcast_to(m0[..., None], (ns, 2, SUBLANES, LANES))
    hs, _, _, _ = _mlstm_scan(q, kt, v, gr, row0=geo.t_prompt, n_seq=ns, seq_len=geo.sample_len,
                              state=(state_C[:, j].astype(F32), n0, m0))
    x1, routing = _out_a(geo, hp, hs, so, a_norm[j], a_w_out[j], x, mod_l, ln_g, ln_b, w_router, e_bias)
    return x1, routing, c_p, n_p[..., 0], m_p[:, :, :NH_A, 0]


def _hgrn_layer(geo, x, mod_l, j, lb_layer, b_w_in, b_norm, b_w_out, ln_g, ln_b, state_S, w_router, e_bias):
    q, pre, v, sg = _proj_b(geo, x, mod_l, b_w_in[j])
    lbd = lb_layer.reshape(2, NH_B, 1, DK_B)
    op, s_p = _hgrn_scan(q, pre, v, lbd, row0=0, n_seq=geo.n_prompt, seq_len=geo.prompt_len)
    os_, _ = _hgrn_scan(q, pre, v, lbd, row0=geo.t_prompt, n_seq=geo.n_sample, seq_len=geo.sample_len,
                        state=state_S[:, j].astype(F32))
    x1, routing = _out_b(geo, op, os_, sg, b_norm[j], b_w_out[j], x, mod_l, ln_g, ln_b, w_router, e_bias)
    return x1, routing, s_p


def kernel(x_prompt, x_sample, state_mlstm_C, state_mlstm_n, state_mlstm_m, state_hgrn_S, c, c_ctx, w_mod, b_mod, ln_g, ln_b, a_w_in, a_b_gates, a_norm, a_w_out, b_w_in, b_lb, b_norm, b_w_out, w_router, e_bias, w_gate, w_up, w_down, ws_gate, ws_up, ws_down):
    bp, sp, _ = x_prompt.shape
    bs, ss, _ = x_sample.shape
    cond = jnp.zeros((COND_ROWS, D), F32).at[0].set(c_ctx).at[1:1 + bs].set(c)
    mod = _modulation(cond, w_mod, b_mod)
    x = (x_prompt.reshape(-1, D), x_sample.reshape(-1, D), _pos_embed(ss // GRID_W))
    sm = jax.nn.softmax(b_lb.astype(F32), axis=0)
    lb_all = jnp.cumsum(sm, axis=0) - sm[0]
    geo = Geometry(bp, sp, bs, ss)
    x1, routing, new_c, new_n, new_m = _mlstm_layer(geo, x, mod[0], 0, a_w_in, a_b_gates, a_norm, a_w_out,
                                                    ln_g[0, 0], ln_b[0, 0], state_mlstm_C, state_mlstm_n,
                                                    state_mlstm_m, w_router[0], e_bias[0])
    x2 = _moe_layer(geo, x1, routing, mod[0], 0, w_gate, w_up, w_down, ws_gate[0], ws_up[0], ws_down[0],
                    ln_g[0, 1], ln_b[0, 1])
    x3, routing, new_s = _hgrn_layer(geo, x2, mod[1], 0, lb_all[1], b_w_in, b_norm, b_w_out, ln_g[1, 0], ln_b[1, 0],
                                     state_hgrn_S, w_router[1], e_bias[1])
    y_p, y_s = _moe_layer(geo, x3, routing, mod[1], 1, w_gate, w_up, w_down, ws_gate[1], ws_up[1], ws_down[1],
                          ln_g[1, 1], ln_b[1, 1], split=True)
    return (y_p.reshape(bp, sp, D), y_s.reshape(bs, ss, D), new_c[:, None], new_n[:, None], new_m[:, None],
            new_s[:, None])
```

```python
import functools
import math

import jax
import jax.numpy as jnp
from jax import lax
from jax.experimental import pallas as pl
from jax.experimental.pallas import tpu as pltpu
from jax.experimental.pallas import tpu_sc as plsc

F32 = jnp.float32
BF16 = jnp.bfloat16
HIGHEST = lax.Precision.HIGHEST

D = 1024
DEPTH = 2
GRID_W = 64
POS_BASE = 10000.0
EPS = 1e-6
ALPHA = (2.0 * DEPTH) ** 0.25
NH_A, DK_A, DV_A = 4, 128, 256
QK_A, V_A = NH_A * DK_A, NH_A * DV_A
NH_B, DK_B = 8, 128
N_EXPERTS, TOP_K, N_GROUPS, TOPK_GROUPS = 64, 8, 8, 4
GROUP_SIZE = N_EXPERTS // N_GROUPS
D_EXPERT = D // 4
ROUTED_SCALE = 2.5

LANES = 128
SUBLANES = 8
COND_ROWS = 8
TOK_TILE_L = 512
SLOT_TILE = 2048
CHUNK_A = 512
VMEM_BYTES_V7X = 64 * 1024 * 1024
VMEM_LIMIT = VMEM_BYTES_V7X - 8 * 1024 * 1024

NT_DIMS = (((1,), (1,)), ((), ()))


def _params(sem):
    return pltpu.CompilerParams(dimension_semantics=sem, vmem_limit_bytes=VMEM_LIMIT)


def _split3(x):
    hi = x.astype(BF16)
    r = x - hi.astype(F32)
    mid = r.astype(BF16)
    lo = (r - mid.astype(F32)).astype(BF16)
    return hi, mid, lo


def _dot3(a_bf, x):
    hi, mid, lo = _split3(x)
    return (jnp.dot(a_bf, hi, preferred_element_type=F32)
            + jnp.dot(a_bf, mid, preferred_element_type=F32)
            + jnp.dot(a_bf, lo, preferred_element_type=F32))


def _dot3_r(x, a_bf):
    hi, mid, lo = _split3(x)
    return (jnp.dot(hi, a_bf, preferred_element_type=F32)
            + jnp.dot(mid, a_bf, preferred_element_type=F32)
            + jnp.dot(lo, a_bf, preferred_element_type=F32))


def _dot_nt_x3(w, x):
    w_hi = w.astype(BF16)
    w_lo = (w - w_hi.astype(F32)).astype(BF16)
    x_hi = x.astype(BF16)
    x_lo = (x - x_hi.astype(F32)).astype(BF16)
    d = lambda a, b: lax.dot_general(a, b, NT_DIMS, preferred_element_type=F32)
    return d(w_hi, x_hi) + d(w_hi, x_lo) + d(w_lo, x_hi)


def _log_sigmoid(x):
    return jnp.minimum(x, 0.0) - jnp.log1p(jnp.exp(-jnp.abs(x)))


def _layer_norm_rows(x, g, b):
    mu = jnp.mean(x, axis=-1, keepdims=True)
    xc = x - mu
    var = jnp.mean(xc * xc, axis=-1, keepdims=True)
    return xc * lax.rsqrt(var + EPS) * g + b


class Geometry:
    def __init__(self, n_prompt, prompt_len, n_sample, sample_len):
        self.n_prompt, self.prompt_len = n_prompt, prompt_len
        self.n_sample, self.sample_len = n_sample, sample_len
        self.t_prompt = n_prompt * prompt_len
        self.t_sample = n_sample * sample_len
        self.t = self.t_prompt + self.t_sample
        assert self.t_prompt % TOK_TILE_L == 0 and sample_len % TOK_TILE_L == 0
        assert n_sample + 1 <= COND_ROWS

    def cond_row(self, tile, tile_rows):
        n_p = self.t_prompt // tile_rows
        return jnp.where(tile < n_p, 0, 1 + (tile - n_p) // (self.sample_len // tile_rows))


def _mod_kernel(cond_ref, w_ref, b_ref, o_ref):
    c = cond_ref[...]
    s = c * jax.nn.sigmoid(c)
    o_ref[0, 0] = jnp.dot(s, w_ref[0], precision=HIGHEST, preferred_element_type=F32) + b_ref[0, 0]


def _modulation(cond, w_mod, b_mod):
    out = pl.pallas_call(
        _mod_kernel,
        name="modulation",
        grid=(DEPTH, 6),
        in_specs=[
            pl.BlockSpec((COND_ROWS, D), lambda l, j: (0, 0)),
            pl.BlockSpec((1, D, D), lambda l, j: (l, 0, j)),
            pl.BlockSpec((1, 1, 1, D), lambda l, j: (l, j, 0, 0)),
        ],
        out_specs=pl.BlockSpec((1, 1, COND_ROWS, D), lambda l, j: (l, j, 0, 0)),
        out_shape=jax.ShapeDtypeStruct((DEPTH, 6, COND_ROWS, D), F32),
        compiler_params=_params(("arbitrary", "arbitrary")),
    )(cond, w_mod, b_mod.reshape(DEPTH, 6, 1, D))
    return out.transpose(0, 2, 1, 3)


def _embed_specs(geo, tb):
    n_p = geo.t_prompt // tb
    per_seq = geo.sample_len // tb
    return [pl.BlockSpec((tb, D), lambda i: (jnp.minimum(i, n_p - 1), 0)),
            pl.BlockSpec((tb, D), lambda i: (jnp.maximum(i - n_p, 0), 0)),
            pl.BlockSpec((tb, D), lambda i: (jnp.maximum(i - n_p, 0) % per_seq, 0))]


def _embed_tile(xp_ref, xs_ref, pos_ref, n_prompt_tiles):
    return jnp.where(pl.program_id(0) < n_prompt_tiles, xp_ref[...], xs_ref[...] + pos_ref[...])


def _proj_a_kernel(xp_ref, xs_ref, pos_ref, mod_ref, wq_ref, wkt_ref, wvo_ref, wgt_ref, bgt_ref,
                   q_ref, kt_ref, v_ref, so_ref, gr_ref, *, n_prompt_tiles):
    x = _embed_tile(xp_ref, xs_ref, pos_ref, n_prompt_tiles)
    h = x * (1.0 + mod_ref[1:2, :]) + mod_ref[0:1, :]
    hb = h.astype(BF16)
    q_ref[...] = jnp.dot(hb, wq_ref[...], preferred_element_type=F32).astype(BF16)
    kt = lax.dot_general(wkt_ref[...], hb, NT_DIMS, preferred_element_type=F32)
    kt_ref[...] = (kt * (DK_A ** -0.5)).astype(BF16)
    vo = jnp.dot(hb, wvo_ref[...], preferred_element_type=F32)
    v_ref[...] = vo[:, :V_A].astype(BF16)
    so_ref[...] = jax.nn.sigmoid(vo[:, V_A:]).astype(BF16)
    gr_ref[...] = _dot_nt_x3(wgt_ref[...], h) + bgt_ref[...]


def _proj_a(geo, x, mod_l, w_in, b_gates):
    t = geo.t
    n_gate = 4 * NH_A
    wq = w_in[:, :QK_A].astype(BF16)
    wkt = w_in[:, QK_A:2 * QK_A].T.astype(BF16)
    wvo = w_in[:, 2 * QK_A:2 * QK_A + 2 * V_A].astype(BF16)
    wg = w_in[:, 2 * QK_A + 2 * V_A:]
    bg = b_gates.reshape(n_gate).astype(F32)
    tb = TOK_TILE_L
    full = lambda shape: pl.BlockSpec(shape, lambda i: (0,) * len(shape))
    return pl.pallas_call(
        functools.partial(_proj_a_kernel, n_prompt_tiles=geo.t_prompt // tb),
        name="proj_a",
        grid=(t // tb,),
        in_specs=_embed_specs(geo, tb) + [
            pl.BlockSpec((None, 6, D), lambda i: (geo.cond_row(i, tb), 0, 0)),
            full((D, QK_A)), full((QK_A, D)), full((D, 2 * V_A)), full((n_gate, D)), full((n_gate, 1)),
        ],
        out_specs=[
            pl.BlockSpec((tb, QK_A), lambda i: (i, 0)),
            pl.BlockSpec((QK_A, tb), lambda i: (0, i)),
            pl.BlockSpec((tb, V_A), lambda i: (i, 0)),
            pl.BlockSpec((tb, V_A), lambda i: (i, 0)),
            pl.BlockSpec((n_gate, tb), lambda i: (0, i)),
        ],
        out_shape=[
            jax.ShapeDtypeStruct((t, QK_A), BF16),
            jax.ShapeDtypeStruct((QK_A, t), BF16),
            jax.ShapeDtypeStruct((t, V_A), BF16),
            jax.ShapeDtypeStruct((t, V_A), BF16),
            jax.ShapeDtypeStruct((n_gate, t), F32),
        ],
        compiler_params=_params(("parallel",)),
    )(*x, mod_l, wq, wkt, wvo, wg.T, bg.reshape(n_gate, 1))


def _mlstm_scan_kernel(*refs, chunk, has_state):
    if has_state:
        (q_ref, kt_ref, v_ref, gr_ref, c0_ref, n0_ref, m0_ref,
         h_ref, c_out, n_out, m_out, c_sc, n_sc, m_sc) = refs
    else:
        (q_ref, kt_ref, v_ref, gr_ref,
         h_ref, c_out, n_out, m_out, c_sc, n_sc, m_sc) = refs
    L = chunk
    d = pl.program_id(1)
    c = pl.program_id(2)
    fwd = d == 0

    @pl.when(c == 0)
    def _():
        if has_state:
            c_sc[...] = c0_ref[0, 0]
            n_sc[...] = n0_ref[0, 0]
            m_sc[...] = m0_ref[0, 0]
        else:
            c_sc[...] = jnp.zeros_like(c_sc)
            n_sc[...] = jnp.zeros_like(n_sc)
            m_sc[...] = jnp.zeros_like(m_sc)

    row = lax.broadcasted_iota(jnp.int32, (L, L), 0)
    col = lax.broadcasted_iota(jnp.int32, (L, L), 1)
    sgn = 1 - 2 * d
    causal = (row - col) * sgn >= 0
    tri_t = ((col - row) * sgn >= 0).astype(BF16)

    gr = gr_ref[...]
    br_all = _dot3_r(_log_sigmoid(gr), tri_t)
    bc_all = jnp.concatenate([br_all, jnp.zeros((LANES - br_all.shape[0], L), F32)], axis=0).T
    ones_blk = (lax.broadcasted_iota(jnp.int32, (L, LANES), 1) == 0).astype(BF16)

    def gate_row(direction, gate, head):
        return (direction * 2 + gate) * NH_A + head

    for h in range(NH_A):
        ff, fb = gate_row(0, 1, h), gate_row(1, 1, h)
        gi, gb = gate_row(0, 0, h), gate_row(1, 0, h)
        b_c = jnp.where(fwd, bc_all[:, ff:ff + 1], bc_all[:, fb:fb + 1])
        b_r = jnp.where(fwd, br_all[ff:ff + 1, :], br_all[fb:fb + 1, :])
        i_r = jnp.where(fwd, gr[gi:gi + 1, :], gr[gb:gb + 1, :])
        bl = jnp.where(fwd, b_r[:, L - 1:L], b_r[:, 0:1])
        q = q_ref[:, h * DK_A:(h + 1) * DK_A]
        kt = kt_ref[h * DK_A:(h + 1) * DK_A, :]
        v = v_ref[:, h * DV_A:(h + 1) * DV_A]
        m = m_sc[h:h + 1, 0:1]
        cst = c_sc[h]
        nst = n_sc[h]

        a_r = i_r - b_r
        logd = jnp.where(causal, b_c + a_r, -jnp.inf)
        inter = b_c + m
        m_t = jnp.maximum(inter, jnp.max(logd, axis=1, keepdims=True))
        dmat = jnp.exp(logd - m_t)
        e_int = jnp.exp(inter - m_t)
        s = (jnp.dot(q, kt, preferred_element_type=F32) * dmat).astype(BF16)
        num = (jnp.dot(s, v, preferred_element_type=F32)
               + e_int * jnp.dot(q, cst.astype(BF16), preferred_element_type=F32))
        den = (jnp.dot(s, ones_blk, preferred_element_type=F32)
               + e_int * jnp.dot(q, nst.astype(BF16), preferred_element_type=F32))[:, 0:1]
        h_ref[:, h * DV_A:(h + 1) * DV_A] = (num / jnp.maximum(jnp.abs(den), jnp.exp(-m_t))).astype(BF16)

        logw = bl + a_r
        m_new = jnp.maximum(bl + m, jnp.max(logw, axis=1, keepdims=True))
        w = jnp.exp(logw - m_new)
        decay = jnp.exp(bl + m - m_new)
        kw = (kt.astype(F32) * w).astype(BF16)
        c_sc[h] = decay * cst + jnp.dot(kw, v, preferred_element_type=F32)
        n_sc[h] = decay * nst + jnp.dot(kw, ones_blk, preferred_element_type=F32)
        m_sc[h:h + 1, :] = jnp.broadcast_to(m_new, (1, LANES))

    @pl.when(c == pl.num_programs(2) - 1)
    def _():
        c_out[0, 0] = c_sc[...]
        n_out[0, 0] = n_sc[...]
        m_out[0, 0] = m_sc[...]


def _mlstm_scan(q, kt, v, gr, *, row0, n_seq, seq_len, state=None):
    L = min(CHUNK_A, seq_len)
    nc = seq_len // L
    blk0 = row0 // L

    def loc_blk(b, d, c):
        return b * nc + c + d * (nc - 1 - 2 * c)

    def tok_blk(b, d, c):
        return blk0 + loc_blk(b, d, c)

    in_specs = [
        pl.BlockSpec((L, QK_A), lambda b, d, c: (tok_blk(b, d, c), 0)),
        pl.BlockSpec((QK_A, L), lambda b, d, c: (0, tok_blk(b, d, c))),
        pl.BlockSpec((L, V_A), lambda b, d, c: (tok_blk(b, d, c), 0)),
        pl.BlockSpec((4 * NH_A, L), lambda b, d, c: (0, tok_blk(b, d, c))),
    ]
    args = [q, kt, v, gr]
    if state is not None:
        in_specs += [
            pl.BlockSpec((1, 1, NH_A, DK_A, DV_A), lambda b, d, c: (b, d, 0, 0, 0)),
            pl.BlockSpec((1, 1, NH_A, DK_A, LANES), lambda b, d, c: (b, d, 0, 0, 0)),
            pl.BlockSpec((1, 1, SUBLANES, LANES), lambda b, d, c: (b, d, 0, 0)),
        ]
        args += list(state)
    return pl.pallas_call(
        functools.partial(_mlstm_scan_kernel, chunk=L, has_state=state is not None),
        name="mlstm_scan_seeded" if state is not None else "mlstm_scan",
        grid=(n_seq, 2, nc),
        in_specs=in_specs,
        out_specs=[
            pl.BlockSpec((None, L, V_A), lambda b, d, c: (d, loc_blk(b, d, c), 0)),
            pl.BlockSpec((1, 1, NH_A, DK_A, DV_A), lambda b, d, c: (b, d, 0, 0, 0)),
            pl.BlockSpec((1, 1, NH_A, DK_A, LANES), lambda b, d, c: (b, d, 0, 0, 0)),
            pl.BlockSpec((1, 1, SUBLANES, LANES), lambda b, d, c: (b, d, 0, 0)),
        ],
        out_shape=[
            jax.ShapeDtypeStruct((2, n_seq * seq_len, V_A), BF16),
            jax.ShapeDtypeStruct((n_seq, 2, NH_A, DK_A, DV_A), F32),
            jax.ShapeDtypeStruct((n_seq, 2, NH_A, DK_A, LANES), F32),
            jax.ShapeDtypeStruct((n_seq, 2, SUBLANES, LANES), F32),
        ],
        scratch_shapes=[
            pltpu.VMEM((NH_A, DK_A, DV_A), F32),
            pltpu.VMEM((NH_A, DK_A, LANES), F32),
            pltpu.VMEM((SUBLANES, LANES), F32),
        ],
        compiler_params=_params(("parallel", "parallel", "arbitrary")),
    )(*args)


def _out_a_kernel(hp_ref, hs_ref, so_ref, nw_ref, w_ref, xp_ref, xs_ref, pos_ref, mod_ref, lg_ref, lb_ref,
                  wrt_ref, eb_ref, o_ref, *route_refs, n_prompt_tiles):
    is_prompt = pl.program_id(0) < n_prompt_tiles
    x = _embed_tile(xp_ref, xs_ref, pos_ref, n_prompt_tiles)
    y = jnp.where(is_prompt, hp_ref[0].astype(F32) + hp_ref[1].astype(F32),
                  hs_ref[0].astype(F32) + hs_ref[1].astype(F32))
    parts = []
    for h in range(NH_A):
        yh = y[:, h * DV_A:(h + 1) * DV_A]
        mu = jnp.mean(yh, axis=-1, keepdims=True)
        yc = yh - mu
        var = jnp.mean(yc * yc, axis=-1, keepdims=True)
        parts.append(yc * lax.rsqrt(var + EPS))
    yn = jnp.concatenate(parts, axis=-1) * nw_ref[...] * so_ref[...].astype(F32)
    out = jnp.dot(yn.astype(BF16), w_ref[...], preferred_element_type=F32)
    x1 = _layer_norm_rows(ALPHA * x + mod_ref[2:3, :] * out, lg_ref[...], lb_ref[...])
    o_ref[...] = x1
    _route_tile(x1, mod_ref, wrt_ref, eb_ref, *route_refs)


def _out_a(geo, h_prompt, h_sample, so, norm_w, w_out, x, mod_l, ln_g, ln_b, w_router, e_bias):
    t = geo.t
    tb = TOK_TILE_L
    n_p = geo.t_prompt // tb
    full = lambda shape: pl.BlockSpec(shape, lambda i: (0,) * len(shape))
    rio = _RouterIO(t, tb, w_router, e_bias)
    outs = pl.pallas_call(
        functools.partial(_out_a_kernel, n_prompt_tiles=n_p),
        name="out_a",
        grid=(t // tb,),
        in_specs=[
            pl.BlockSpec((2, tb, V_A), lambda i: (0, jnp.minimum(i, n_p - 1), 0)),
            pl.BlockSpec((2, tb, V_A), lambda i: (0, jnp.maximum(i - n_p, 0), 0)),
            pl.BlockSpec((tb, V_A), lambda i: (i, 0)),
            full((1, V_A)), full((V_A, D)),
        ] + _embed_specs(geo, tb) + [
            pl.BlockSpec((None, 6, D), lambda i: (geo.cond_row(i, tb), 0, 0)),
            full((1, D)), full((1, D)),
        ] + rio.in_specs,
        out_specs=[pl.BlockSpec((tb, D), lambda i: (i, 0))] + rio.out_specs,
        out_shape=[jax.ShapeDtypeStruct((t, D), F32)] + rio.out_shape,
        scratch_shapes=rio.scratch,
        compiler_params=_params(("arbitrary",)),
    )(h_prompt, h_sample, so, norm_w.reshape(1, V_A).astype(F32), w_out.astype(BF16), *x, mod_l,
      ln_g.reshape(1, D), ln_b.reshape(1, D), *rio.inputs)
    return outs[0], _RouterIO.unpack(outs[1:])


def _proj_b_kernel(x_ref, mod_ref, w_ref, q_ref, pre_ref, v_ref, sg_ref):
    h = x_ref[...] * (1.0 + mod_ref[1:2, :]) + mod_ref[0:1, :]
    z = jnp.dot(h.astype(BF16), w_ref[...], preferred_element_type=F32)
    for hd in range(NH_B):
        lo = hd * DK_B
        qh = z[:, lo:lo + DK_B]
        q_ref[hd] = qh * jax.nn.sigmoid(qh)
        pre_ref[0, hd] = z[:, D + lo:D + lo + DK_B]
        pre_ref[1, hd] = z[:, 2 * D + lo:2 * D + lo + DK_B]
        v_ref[hd] = z[:, 3 * D + lo:3 * D + lo + DK_B].astype(BF16)
    g = z[:, 4 * D:]
    sg_ref[...] = (g * jax.nn.sigmoid(g)).astype(BF16)


def _proj_b(geo, x, mod_l, w_in):
    t = geo.t
    tb = TOK_TILE_L
    return pl.pallas_call(
        _proj_b_kernel,
        name="proj_b",
        grid=(t // tb,),
        in_specs=[
            pl.BlockSpec((tb, D), lambda i: (i, 0)),
            pl.BlockSpec((None, 6, D), lambda i: (geo.cond_row(i, tb), 0, 0)),
            pl.BlockSpec((D, 5 * D), lambda i: (0, 0), pipeline_mode=pl.Buffered(1)),
        ],
        out_specs=[
            pl.BlockSpec((NH_B, tb, DK_B), lambda i: (0, i, 0)),
            pl.BlockSpec((2, NH_B, tb, DK_B), lambda i: (0, 0, i, 0)),
            pl.BlockSpec((NH_B, tb, DK_B), lambda i: (0, i, 0)),
            pl.BlockSpec((tb, D), lambda i: (i, 0)),
        ],
        out_shape=[
            jax.ShapeDtypeStruct((NH_B, t, DK_B), F32),
            jax.ShapeDtypeStruct((2, NH_B, t, DK_B), F32),
            jax.ShapeDtypeStruct((NH_B, t, DK_B), BF16),
            jax.ShapeDtypeStruct((t, D), BF16),
        ],
        compiler_params=_params(("parallel",)),
    )(x, mod_l, w_in.astype(BF16))


CHUNK_B = 256
BAND = SUBLANES // 2
TN_DIMS = (((0,), (0,)), ((), ()))


def _hgrn_head(q, pre, lbv, v_bf, st, fwd):
    L = q.shape[0]
    sg = jax.nn.sigmoid(pre)
    f = lbv + (1.0 - lbv) * sg
    lf = jnp.log(f)
    kk = (1.0 - lbv) * (1.0 - sg)
    row = lax.broadcasted_iota(jnp.int32, (L, L), 0)
    col = lax.broadcasted_iota(jnp.int32, (L, L), 1)
    tri = ((row >= col) if fwd else (row <= col)).astype(BF16)
    b = _dot3(tri, lf)
    tpos = lax.broadcasted_iota(jnp.int32, (L, DK_B), 0)
    blk_bits = row ^ col
    lag = jnp.where(blk_bits < BAND, (row - col) if fwd else (col - row), -1)

    step = 1 if fwd else L - 1
    att = jnp.where(lag == 0, jnp.sum(q * kk, axis=1, keepdims=True), 0.0)
    f_r, kk_r, g = f, kk, f
    for dl in range(1, BAND):
        if dl > 1:
            f_r = pltpu.roll(f_r, step, 0)
            g = g * f_r
        kk_r = pltpu.roll(kk_r, step, 0)
        att = jnp.where(lag == dl, jnp.sum(q * kk_r * g, axis=1, keepdims=True), att)

    w = BAND
    while w < L:
        nb = L // (2 * w)
        b3 = b.reshape(nb, 2 * w, DK_B)
        edge = (b3[:, w - 1:w, :] if fwd else b3[:, w:w + 1, :])
        bmid = jnp.broadcast_to(edge, (nb, 2 * w, DK_B)).reshape(L, DK_B)
        second = (tpos & w) != 0
        t_side = second if fwd else jnp.logical_not(second)
        e = jnp.exp(jnp.where(t_side, b - bmid, bmid - b))
        qt = jnp.where(t_side, q * e, 0.0).astype(BF16)
        ks = jnp.where(t_side, 0.0, kk * e).astype(BF16)
        a = lax.dot_general(qt, ks, NT_DIMS, preferred_element_type=F32)
        att = att + jnp.where(blk_bits < 2 * w, a, 0.0)
        w *= 2
    o = jnp.dot(att.astype(BF16), v_bf, preferred_element_type=F32)

    bl = b[L - 1:L, :] if fwd else b[0:1, :]
    o = o + lax.dot_general((q * jnp.exp(b)).astype(BF16), st.astype(BF16), NT_DIMS, preferred_element_type=F32)
    kd = (kk * jnp.exp(bl - b)).astype(BF16)
    st_new = jnp.exp(bl) * st + lax.dot_general(v_bf, kd, TN_DIMS, preferred_element_type=F32)
    return o, st_new


def _hgrn_scan_kernel(*refs, has_state):
    if has_state:
        q_ref, pre_ref, v_ref, lb_ref, s0_ref, o_ref, s_out, st_sc = refs
    else:
        q_ref, pre_ref, v_ref, lb_ref, o_ref, s_out, st_sc = refs
    d = pl.program_id(1)
    c = pl.program_id(2)

    @pl.when(c == 0)
    def _():
        if has_state:
            for hd in range(NH_B):
                st_sc[hd] = s0_ref[0, 0, hd].T
        else:
            st_sc[...] = jnp.zeros_like(st_sc)

    def run(fwd):
        def head(hd, carry):
            o, st_new = _hgrn_head(q_ref[hd], pre_ref[hd], lb_ref[hd], v_ref[hd], st_sc[hd], fwd)
            o_ref[hd] = o.astype(BF16)
            st_sc[hd] = st_new
            return carry
        lax.fori_loop(0, NH_B, head, 0, unroll=8)

    @pl.when(d == 0)
    def _():
        run(True)

    @pl.when(d == 1)
    def _():
        run(False)

    @pl.when(c == pl.num_programs(2) - 1)
    def _():
        for hd in range(NH_B):
            s_out[0, 0, hd] = st_sc[hd].T


def _hgrn_scan(q, pre, v, lbd, *, row0, n_seq, seq_len, state=None):
    L = CHUNK_B
    nc = seq_len // L
    blk0 = row0 // L

    def loc_blk(b, d, c):
        return b * nc + c + d * (nc - 1 - 2 * c)

    def tok_blk(b, d, c):
        return blk0 + loc_blk(b, d, c)

    in_specs = [
        pl.BlockSpec((NH_B, L, DK_B), lambda b, d, c: (0, tok_blk(b, d, c), 0)),
        pl.BlockSpec((None, NH_B, L, DK_B), lambda b, d, c: (d, 0, tok_blk(b, d, c), 0)),
        pl.BlockSpec((NH_B, L, DK_B), lambda b, d, c: (0, tok_blk(b, d, c), 0)),
        pl.BlockSpec((None, NH_B, 1, DK_B), lambda b, d, c: (d, 0, 0, 0)),
    ]
    args = [q, pre, v, lbd]
    if state is not None:
        in_specs.append(pl.BlockSpec((1, 1, NH_B, DK_B, DK_B), lambda b, d, c: (b, d, 0, 0, 0)))
        args.append(state)
    return pl.pallas_call(
        functools.partial(_hgrn_scan_kernel, has_state=state is not None),
        name="hgrn_scan_seeded" if state is not None else "hgrn_scan",
        grid=(n_seq, 2, nc),
        in_specs=in_specs,
        out_specs=[
            pl.BlockSpec((None, NH_B, L, DK_B), lambda b, d, c: (d, 0, loc_blk(b, d, c), 0)),
            pl.BlockSpec((1, 1, NH_B, DK_B, DK_B), lambda b, d, c: (b, d, 0, 0, 0)),
        ],
        out_shape=[
            jax.ShapeDtypeStruct((2, NH_B, n_seq * seq_len, DK_B), BF16),
            jax.ShapeDtypeStruct((n_seq, 2, NH_B, DK_B, DK_B), F32),
        ],
        scratch_shapes=[pltpu.VMEM((NH_B, DK_B, DK_B), F32)],
        compiler_params=_params(("parallel", "parallel", "arbitrary")),
    )(*args)


def _out_b_kernel(op_ref, os_ref, sg_ref, nw_ref, w_ref, x_ref, mod_ref, lg_ref, lb_ref, wrt_ref, eb_ref,
                  out_ref, *route_refs, n_prompt_tiles):
    is_prompt = pl.program_id(0) < n_prompt_tiles
    parts = []
    for hd in range(NH_B):
        y = jnp.where(is_prompt, op_ref[0, hd].astype(F32) + op_ref[1, hd].astype(F32),
                      os_ref[0, hd].astype(F32) + os_ref[1, hd].astype(F32))
        parts.append(y * lax.rsqrt(jnp.mean(y * y, axis=-1, keepdims=True) + EPS))
    yn = jnp.concatenate(parts, axis=-1) * nw_ref[...] * sg_ref[...].astype(F32)
    out = jnp.dot(yn.astype(BF16), w_ref[...], preferred_element_type=F32)
    x1 = _layer_norm_rows(ALPHA * x_ref[...] + mod_ref[2:3, :] * out, lg_ref[...], lb_ref[...])
    out_ref[...] = x1
    _route_tile(x1, mod_ref, wrt_ref, eb_ref, *route_refs)


def _out_b(geo, o_prompt, o_sample, sg, norm_w, w_out, x, mod_l, ln_g, ln_b, w_router, e_bias):
    t = geo.t
    tb = TOK_TILE_L
    n_p = geo.t_prompt // tb
    full = lambda shape: pl.BlockSpec(shape, lambda i: (0,) * len(shape))
    rio = _RouterIO(t, tb, w_router, e_bias)
    outs = pl.pallas_call(
        functools.partial(_out_b_kernel, n_prompt_tiles=n_p),
        name="out_b",
        grid=(t // tb,),
        in_specs=[
            pl.BlockSpec((2, NH_B, tb, DK_B), lambda i: (0, 0, jnp.minimum(i, n_p - 1), 0)),
            pl.BlockSpec((2, NH_B, tb, DK_B), lambda i: (0, 0, jnp.maximum(i - n_p, 0), 0)),
            pl.BlockSpec((tb, D), lambda i: (i, 0)),
            full((1, D)), full((D, D)),
            pl.BlockSpec((tb, D), lambda i: (i, 0)),
            pl.BlockSpec((None, 6, D), lambda i: (geo.cond_row(i, tb), 0, 0)),
            full((1, D)), full((1, D)),
        ] + rio.in_specs,
        out_specs=[pl.BlockSpec((tb, D), lambda i: (i, 0))] + rio.out_specs,
        out_shape=[jax.ShapeDtypeStruct((t, D), F32)] + rio.out_shape,
        scratch_shapes=rio.scratch,
        compiler_params=_params(("arbitrary",)),
    )(o_prompt, o_sample, sg, norm_w.reshape(1, D).astype(F32), w_out.astype(BF16), x, mod_l,
      ln_g.reshape(1, D), ln_b.reshape(1, D), *rio.inputs)
    return outs[0], _RouterIO.unpack(outs[1:])


MOE_BLK = 1024
U32 = jnp.uint32
ROW_WORDS = D // 2
CHUNK_W = 256
ROW_CHUNKS = ROW_WORDS // CHUNK_W
SC_WINDOW = 128


def _pack_rows(x):
    return pltpu.pack_elementwise([x[:, :ROW_WORDS], x[:, ROW_WORDS:]], packed_dtype=BF16)


def _unpack_rows(words):
    return jnp.concatenate([pltpu.unpack_elementwise(words, index=i, packed_dtype=BF16, unpacked_dtype=F32)
                            for i in range(2)], axis=1)


def _store_chunks(chunk_ref, x):
    words = _pack_rows(x)
    for c in range(ROW_CHUNKS):
        chunk_ref(c)[...] = words[:, c * CHUNK_W:(c + 1) * CHUNK_W]


def _load_chunks(chunk_ref, valid_rows=None):
    words = jnp.concatenate([chunk_ref(c)[...] for c in range(ROW_CHUNKS)], axis=1)
    if valid_rows is not None:
        row = lax.broadcasted_iota(jnp.int32, (words.shape[0], 1), 0)
        words = jnp.where(row < valid_rows, words, jnp.uint32(0))
    return _unpack_rows(words)


def _first_index(hit, iota, size, axis):
    return jnp.min(jnp.where(hit, iota, size), axis=axis, keepdims=True)


def _route_tile(x, mod_ref, wrt_ref, eb_ref, e_ref, w_ref, r_ref, cnt_ref, h_ref, cnt_sc):
    i = pl.program_id(0)
    tb = x.shape[0]

    @pl.when(i == 0)
    def _():
        cnt_sc[...] = jnp.zeros_like(cnt_sc)

    h = x * (1.0 + mod_ref[4:5, :]) + mod_ref[3:4, :]
    _store_chunks(lambda c: h_ref.at[c], h)
    logits = _dot_nt_x3(wrt_ref[...], h)
    scores = jax.nn.sigmoid(logits)
    sel = scores + eb_ref[...]

    g3 = sel.reshape(N_GROUPS, GROUP_SIZE, tb)
    io3 = lax.broadcasted_iota(jnp.int32, g3.shape, 1)
    m1 = jnp.max(g3, axis=1, keepdims=True)
    first = _first_index(g3 == m1, io3, GROUP_SIZE, 1)
    m2 = jnp.max(jnp.where(io3 == first, -jnp.inf, g3), axis=1, keepdims=True)
    gscore = (m1 + m2).reshape(N_GROUPS, tb)

    iog = lax.broadcasted_iota(jnp.int32, gscore.shape, 0)
    gmask = jnp.zeros(gscore.shape, F32)
    for _ in range(TOPK_GROUPS):
        gm = jnp.max(gscore, axis=0, keepdims=True)
        pick = iog == _first_index(gscore == gm, iog, N_GROUPS, 0)
        gmask = jnp.where(pick, 1.0, gmask)
        gscore = jnp.where(pick, -jnp.inf, gscore)
    emask = jnp.broadcast_to(gmask.reshape(N_GROUPS, 1, tb), (N_GROUPS, GROUP_SIZE, tb)).reshape(N_EXPERTS, tb)
    cand = jnp.where(emask > 0.0, sel, -jnp.inf)

    ioe = lax.broadcasted_iota(jnp.int32, cand.shape, 0)
    picks, wts = [], []
    onehot = jnp.zeros(cand.shape, F32)
    for _ in range(TOP_K):
        cm = jnp.max(cand, axis=0, keepdims=True)
        idx = _first_index(cand == cm, ioe, N_EXPERTS, 0)
        pick = ioe == idx
        picks.append(pick)
        wts.append(jnp.sum(jnp.where(pick, scores, 0.0), axis=0, keepdims=True))
        onehot = onehot + pick.astype(F32)
        cand = jnp.where(pick, -jnp.inf, cand)
        e_ref[pl.ds(len(picks) - 1, 1), :] = idx
    wsum = wts[0]
    for wk in wts[1:]:
        wsum = wsum + wk
    for k in range(TOP_K):
        w_ref[pl.ds(k, 1), :] = wts[k] / wsum * ROUTED_SCALE

    r_io = lax.broadcasted_iota(jnp.int32, (tb, tb), 0)
    c_io = lax.broadcasted_iota(jnp.int32, (tb, tb), 1)
    before = (r_io < c_io).astype(BF16)
    rank = cnt_sc[:, 0:1] + jnp.dot(onehot.astype(BF16), before, preferred_element_type=F32)
    for k in range(TOP_K):
        r_ref[pl.ds(k, 1), :] = jnp.sum(jnp.where(picks[k], rank, 0.0), axis=0, keepdims=True).astype(jnp.int32)
    cnt_sc[...] = cnt_sc[...] + jnp.sum(onehot, axis=1, keepdims=True)
    cnt_ref[...] = cnt_sc[...]


class _RouterIO:
    def __init__(self, t, tb, w_router, e_bias):
        full = lambda shape: pl.BlockSpec(shape, lambda i: (0,) * len(shape))
        self.inputs = [w_router.T.astype(F32), e_bias.reshape(N_EXPERTS, 1).astype(F32)]
        self.in_specs = [full((N_EXPERTS, D)), full((N_EXPERTS, 1))]
        self.out_specs = [
            pl.BlockSpec((TOP_K, tb), lambda i: (0, i)),
            pl.BlockSpec((TOP_K, tb), lambda i: (0, i)),
            pl.BlockSpec((TOP_K, tb), lambda i: (0, i)),
            full((N_EXPERTS, LANES)),
            pl.BlockSpec((ROW_CHUNKS, tb, CHUNK_W), lambda i: (0, i, 0)),
        ]
        self.out_shape = [
            jax.ShapeDtypeStruct((TOP_K, t), jnp.int32),
            jax.ShapeDtypeStruct((TOP_K, t), F32),
            jax.ShapeDtypeStruct((TOP_K, t), jnp.int32),
            jax.ShapeDtypeStruct((N_EXPERTS, LANES), F32),
            jax.ShapeDtypeStruct((ROW_CHUNKS, t, CHUNK_W), U32),
        ]
        self.scratch = [pltpu.VMEM((N_EXPERTS, LANES), F32)]

    @staticmethod
    def unpack(outs):
        e, w, r, cnt, h = outs
        return e, w, r, cnt[:, 0].astype(jnp.int32), h


def _slot_kernel(pstart_ref, e_ref, r_ref, o_ref):
    e = e_ref[...]
    slot = r_ref[...]
    for x in range(N_EXPERTS):
        slot = slot + jnp.where(e == x, pstart_ref[x], 0)
    o_ref[...] = slot


def _slots(geo, pstart, top_e, rank):
    tb = math.gcd(SLOT_TILE, geo.t)
    return pl.pallas_call(
        _slot_kernel,
        name="slots",
        grid_spec=pltpu.PrefetchScalarGridSpec(
            num_scalar_prefetch=1,
            grid=(geo.t // tb,),
            in_specs=[pl.BlockSpec((TOP_K, tb), lambda i, p: (0, i)),
                      pl.BlockSpec((TOP_K, tb), lambda i, p: (0, i))],
            out_specs=pl.BlockSpec((TOP_K, tb), lambda i, p: (0, i)),
        ),
        out_shape=jax.ShapeDtypeStruct((TOP_K, geo.t), jnp.int32),
        compiler_params=_params(("parallel",)),
    )(pstart, top_e, rank)


def _block_meta_kernel(pstart_ref, counts_ref, pend_ref, e_ref, v_ref):
    row0 = lax.broadcasted_iota(jnp.int32, e_ref.shape, 1) * MOE_BLK
    blk_e = jnp.zeros(e_ref.shape, jnp.int32)
    for x in range(N_EXPERTS):
        blk_e = blk_e + jnp.where(pend_ref[x] <= row0, 1, 0)
    blk_e = jnp.minimum(blk_e, N_EXPERTS - 1)
    last = jnp.zeros(e_ref.shape, jnp.int32)
    for x in range(N_EXPERTS):
        last = last + jnp.where(blk_e == x, pstart_ref[x] + counts_ref[x], 0)
    e_ref[...] = blk_e
    v_ref[...] = jnp.clip(last - row0, 0, MOE_BLK)


def _block_meta(pstart, counts, pend, n_blocks):
    e, v = pl.pallas_call(
        _block_meta_kernel,
        name="block_meta",
        grid_spec=pltpu.PrefetchScalarGridSpec(
            num_scalar_prefetch=3,
            grid=(1,),
            in_specs=[],
            out_specs=[pl.BlockSpec((1, n_blocks), lambda i, a, b, c: (0, 0)),
                       pl.BlockSpec((1, n_blocks), lambda i, a, b, c: (0, 0))],
        ),
        out_shape=[jax.ShapeDtypeStruct((1, n_blocks), jnp.int32), jax.ShapeDtypeStruct((1, n_blocks), jnp.int32)],
        compiler_params=_params(("arbitrary",)),
    )(pstart, counts, pend)
    return e[0], v[0]


def _sc_mesh():
    return plsc.VectorSubcoreMesh(core_axis_name="core", subcore_axis_name="subcore")


def _sc_scatter(rows, idx, n_out, copies):
    n_src = rows.shape[0]
    n_idx = idx.shape[0]
    groups = ROW_CHUNKS
    win_per_group = n_src // groups // SC_WINDOW

    def idx_block(w, k):
        return (0, ((w // win_per_group) * copies + k) * win_per_group + w % win_per_group)

    @pl.kernel(out_type=jax.ShapeDtypeStruct((n_out, CHUNK_W), rows.dtype), mesh=_sc_mesh(), scratch_types=[],
               name="sc_dispatch")
    def scatter(x_hbm, i_hbm, o_hbm):
        def body(x_vmem, *i_vmems):
            for i_vmem in i_vmems:
                pltpu.sync_copy(x_vmem, o_hbm.at[i_vmem.at[0]])

        pltpu.emit_pipeline(
            body,
            grid=(n_src // SC_WINDOW,),
            in_specs=[pl.BlockSpec((SC_WINDOW, CHUNK_W), index_map=lambda w: (w, 0))]
            + [pl.BlockSpec((1, SC_WINDOW), index_map=functools.partial(idx_block, k=k)) for k in range(copies)],
            out_specs=[],
            core_axis_name=("core", "subcore"),
            dimension_semantics=(pltpu.PARALLEL,),
        )(x_hbm, *([i_hbm] * copies))

    return scatter(rows, idx.reshape(1, n_idx))


def _sc_gather(table, idx):
    n_idx = idx.shape[0]

    @pl.kernel(out_type=jax.ShapeDtypeStruct((n_idx, CHUNK_W), table.dtype), mesh=_sc_mesh(),
               name="sc_combine_gather")
    def gather(t_hbm, i_hbm, o_hbm):
        def body(i_vmem, o_vmem):
            pltpu.sync_copy(t_hbm.at[i_vmem.at[0]], o_vmem)

        pltpu.emit_pipeline(
            body,
            grid=(n_idx // SC_WINDOW,),
            in_specs=[pl.BlockSpec((1, SC_WINDOW), index_map=lambda w: (0, w))],
            out_specs=[pl.BlockSpec((SC_WINDOW, CHUNK_W), index_map=lambda w: (w, 0))],
            core_axis_name=("core", "subcore"),
            dimension_semantics=(pltpu.PARALLEL,),
        )(i_hbm, o_hbm)

    return gather(table, idx.reshape(1, n_idx))


def _ffn_step(b, blk_e_ref, blk_valid_ref, n_used_ref, xs_ref, wg_ref, wu_ref, wd_ref, y_ref, wg_sc, wu_sc, wd_sc):
    used = b < n_used_ref[0]
    new_expert = (b == 0) | (blk_e_ref[b] != blk_e_ref[jnp.maximum(b - 1, 0)])

    @pl.when(used & new_expert)
    def _():
        wg_sc[...] = wg_ref[...].astype(BF16)
        wu_sc[...] = wu_ref[...].astype(BF16)
        wd_sc[...] = wd_ref[...].astype(BF16)

    @pl.when(used)
    def _():
        x = _load_chunks(lambda c: xs_ref.at[c], valid_rows=blk_valid_ref[b]).astype(BF16)
        g = jnp.dot(x, wg_sc[...], preferred_element_type=F32)
        u = jnp.dot(x, wu_sc[...], preferred_element_type=F32)
        hmid = (g * jax.nn.sigmoid(g) * u).astype(BF16)
        _store_chunks(lambda c: y_ref.at[c], jnp.dot(hmid, wd_sc[...], preferred_element_type=F32))

    @pl.when(jnp.logical_not(used))
    def _():
        y_ref[...] = jnp.zeros_like(y_ref)


FFN_INPUT_BUFFERS = 3


def _ffn_kernel(blk_e_ref, blk_valid_ref, n_used_ref, xs_hbm, wg_hbm, wu_hbm, wd_hbm, y_hbm,
                wg_sc, wu_sc, wd_sc, step_sm, *, layer, n_blocks):
    step_sm[0] = 0

    def blk(b):
        return jnp.maximum(jnp.minimum(b, n_used_ref[0] - 1), 0)

    def w_idx(b):
        return (layer, blk_e_ref[blk(b)], 0, 0)

    def body(xs_ref, wg_ref, wu_ref, wd_ref, y_ref):
        b = step_sm[0]
        step_sm[0] = b + 1
        _ffn_step(b, blk_e_ref, blk_valid_ref, n_used_ref, xs_ref, wg_ref, wu_ref, wd_ref, y_ref,
                  wg_sc, wu_sc, wd_sc)

    rows = (ROW_CHUNKS, MOE_BLK, CHUNK_W)
    pltpu.emit_pipeline(
        body,
        grid=(n_blocks,),
        in_specs=[
            pl.BlockSpec(rows, lambda b: (0, blk(b), 0), pipeline_mode=pl.Buffered(FFN_INPUT_BUFFERS)),
            pl.BlockSpec((None, None, D, D_EXPERT), w_idx, pipeline_mode=pl.Buffered(2, use_lookahead=True)),
            pl.BlockSpec((None, None, D, D_EXPERT), w_idx, pipeline_mode=pl.Buffered(2, use_lookahead=True)),
            pl.BlockSpec((None, None, D_EXPERT, D), w_idx, pipeline_mode=pl.Buffered(2, use_lookahead=True)),
        ],
        out_specs=[pl.BlockSpec(rows, lambda b: (0, b, 0))],
    )(xs_hbm, wg_hbm, wu_hbm, wd_hbm, y_hbm)


def _ffn(xs, blk_e, blk_valid, n_used, layer, wg, wu, wd, n_blocks):
    smem = pl.BlockSpec(memory_space=pltpu.SMEM)
    hbm = pl.BlockSpec(memory_space=pl.ANY)
    return pl.pallas_call(
        functools.partial(_ffn_kernel, layer=layer, n_blocks=n_blocks),
        name="expert_ffn",
        in_specs=[smem, smem, smem, hbm, hbm, hbm, hbm],
        out_specs=hbm,
        out_shape=jax.ShapeDtypeStruct(xs.shape, U32),
        scratch_shapes=[pltpu.VMEM((D, D_EXPERT), BF16), pltpu.VMEM((D, D_EXPERT), BF16),
                        pltpu.VMEM((D_EXPERT, D), BF16), pltpu.SMEM((1,), jnp.int32)],
        compiler_params=pltpu.CompilerParams(vmem_limit_bytes=VMEM_LIMIT),
    )(blk_e, blk_valid, n_used, xs, wg, wu, wd)


def _combine_kernel(x_ref, mod_ref, wt_ref, y_ref, sg_ref, su_ref, sd_ref, lg_ref, lb_ref, *o_refs, n_prompt_tiles):
    x = x_ref[...]
    hb = (x * (1.0 + mod_ref[4:5, :]) + mod_ref[3:4, :]).astype(BF16)
    g = jnp.dot(hb, sg_ref[...], preferred_element_type=F32)
    u = jnp.dot(hb, su_ref[...], preferred_element_type=F32)
    ff = jnp.dot((g * jax.nn.sigmoid(g) * u).astype(BF16), sd_ref[...], preferred_element_type=F32)
    for k in range(TOP_K):
        ff = ff + _load_chunks(lambda c: y_ref.at[c, k]) * wt_ref[:, k:k + 1]
    out = _layer_norm_rows(ALPHA * x + mod_ref[5:6, :] * ff, lg_ref[...], lb_ref[...])
    if len(o_refs) == 1:
        o_refs[0][...] = out
    else:
        is_prompt = pl.program_id(0) < n_prompt_tiles

        @pl.when(is_prompt)
        def _():
            o_refs[0][...] = out

        @pl.when(jnp.logical_not(is_prompt))
        def _():
            o_refs[1][...] = out


def _combine(geo, x, mod_l, wt, ytok, sg, su, sd, ln_g, ln_b, split=False):
    tb = TOK_TILE_L
    n_p = geo.t_prompt // tb
    full = lambda shape: pl.BlockSpec(shape, lambda i: (0,) * len(shape))
    if split:
        out_specs = [pl.BlockSpec((tb, D), lambda i: (jnp.minimum(i, n_p - 1), 0)),
                     pl.BlockSpec((tb, D), lambda i: (jnp.maximum(i - n_p, 0), 0))]
        out_shape = [jax.ShapeDtypeStruct((geo.t_prompt, D), F32), jax.ShapeDtypeStruct((geo.t_sample, D), F32)]
    else:
        out_specs = pl.BlockSpec((tb, D), lambda i: (i, 0))
        out_shape = jax.ShapeDtypeStruct((geo.t, D), F32)
    return pl.pallas_call(
        functools.partial(_combine_kernel, n_prompt_tiles=n_p),
        name="combine",
        grid=(geo.t // tb,),
        in_specs=[
            pl.BlockSpec((tb, D), lambda i: (i, 0)),
            pl.BlockSpec((None, 6, D), lambda i: (geo.cond_row(i, tb), 0, 0)),
            pl.BlockSpec((tb, TOP_K), lambda i: (i, 0)),
            pl.BlockSpec((ROW_CHUNKS, TOP_K, tb, CHUNK_W), lambda i: (0, 0, i, 0)),
            full((D, D_EXPERT)), full((D, D_EXPERT)), full((D_EXPERT, D)), full((1, D)), full((1, D)),
        ],
        out_specs=out_specs,
        out_shape=out_shape,
        compiler_params=_params(("arbitrary",)),
    )(x, mod_l, wt, ytok, sg.astype(BF16), su.astype(BF16), sd.astype(BF16),
      ln_g.reshape(1, D), ln_b.reshape(1, D))


def _moe_layer(geo, x, routing, mod_l, layer, wg, wu, wd, sg, su, sd, ln_g, ln_b, split=False):
    t = geo.t
    top_e, w, rank, counts, h = routing
    n_blocks = (t * TOP_K) // MOE_BLK + N_EXPERTS
    n_rows = n_blocks * MOE_BLK
    padded = (counts + MOE_BLK - 1) // MOE_BLK * MOE_BLK
    pend = jnp.cumsum(padded)
    pstart = (pend - padded).astype(jnp.int32)
    blk_e, blk_valid = _block_meta(pstart, counts, pend.astype(jnp.int32), n_blocks)
    n_used = (pend[-1:] // MOE_BLK).astype(jnp.int32)
    slots = _slots(geo, pstart, top_e, rank)
    idx = (slots.reshape(1, TOP_K * t) + (jnp.arange(ROW_CHUNKS, dtype=jnp.int32) * n_rows)[:, None]).reshape(-1)
    xs = _sc_scatter(h.reshape(ROW_CHUNKS * t, CHUNK_W), idx, ROW_CHUNKS * n_rows, TOP_K)
    yb = _ffn(xs.reshape(ROW_CHUNKS, n_rows, CHUNK_W), blk_e, blk_valid, n_used, layer, wg, wu, wd, n_blocks)
    ytok = _sc_gather(yb.reshape(ROW_CHUNKS * n_rows, CHUNK_W), idx)
    return _combine(geo, x, mod_l, w.T, ytok.reshape(ROW_CHUNKS, TOP_K, t, CHUNK_W), sg, su, sd, ln_g, ln_b,
                    split=split)


def _pos_embed(rows):
    quarter = D // 4
    omega = 1.0 / (POS_BASE ** (jnp.arange(quarter, dtype=F32) / quarter))
    r = jnp.arange(rows, dtype=F32)[:, None] * omega
    col = jnp.arange(GRID_W, dtype=F32)[:, None] * omega
    row_part = jnp.concatenate([jnp.sin(r), jnp.cos(r)], axis=-1)[:, None, :]
    col_part = jnp.concatenate([jnp.sin(col), jnp.cos(col)], axis=-1)[None, :, :]
    shape = (rows, GRID_W, 2 * quarter)
    return jnp.concatenate([jnp.broadcast_to(row_part, shape), jnp.broadcast_to(col_part, shape)],
                           axis=-1).reshape(rows * GRID_W, D)


def _mlstm_layer(geo, x, mod_l, j, a_w_in, a_b_gates, a_norm, a_w_out, ln_g, ln_b,
                 state_C, state_n, state_m, w_router, e_bias):
    q, kt, v, so, gr = _proj_a(geo, x, mod_l, a_w_in[j], a_b_gates[j])
    hp, c_p, n_p, m_p = _mlstm_scan(q, kt, v, gr, row0=0, n_seq=geo.n_prompt, seq_len=geo.prompt_len)
    ns = geo.n_sample
    n0 = jnp.pad(state_n[:, j].astype(F32)[..., None], ((0, 0),) * 4 + ((0, LANES - 1),))
    m0 = jnp.pad(state_m[:, j].astype(F32), ((0, 0), (0, 0), (0, SUBLANES - NH_A)))
    m0 = jnp.broadcast_to(m0[..., None], (ns, 2, SUBLANES, LANES))
    hs, _, _, _ = _mlstm_scan(q, kt, v, gr, row0=geo.t_prompt, n_seq=ns, seq_len=geo.sample_len,
                              state=(state_C[:, j].astype(F32), n0, m0))
    x1, routing = _out_a(geo, hp, hs, so, a_norm[j], a_w_out[j], x, mod_l, ln_g, ln_b, w_router, e_bias)
    return x1, routing, c_p, n_p[..., 0], m_p[:, :, :NH_A, 0]


def _hgrn_layer(geo, x, mod_l, j, lb_layer, b_w_in, b_norm, b_w_out, ln_g, ln_b, state_S, w_router, e_bias):
    q, pre, v, sg = _proj_b(geo, x, mod_l, b_w_in[j])
    lbd = lb_layer.reshape(2, NH_B, 1, DK_B)
    op, s_p = _hgrn_scan(q, pre, v, lbd, row0=0, n_seq=geo.n_prompt, seq_len=geo.prompt_len)
    os_, _ = _hgrn_scan(q, pre, v, lbd, row0=geo.t_prompt, n_seq=geo.n_sample, seq_len=geo.sample_len,
                        state=state_S[:, j].astype(F32))
    x1, routing = _out_b(geo, op, os_, sg, b_norm[j], b_w_out[j], x, mod_l, ln_g, ln_b, w_router, e_bias)
    return x1, routing, s_p


def kernel(x_prompt, x_sample, state_mlstm_C, state_mlstm_n, state_mlstm_m, state_hgrn_S, c, c_ctx, w_mod, b_mod, ln_g, ln_b, a_w_in, a_b_gates, a_norm, a_w_out, b_w_in, b_lb, b_norm, b_w_out, w_router, e_bias, w_gate, w_up, w_down, ws_gate, ws_up, ws_down):
    bp, sp, _ = x_prompt.shape
    bs, ss, _ = x_sample.shape
    cond = jnp.zeros((COND_ROWS, D), F32).at[0].set(c_ctx).at[1:1 + bs].set(c)
    mod = _modulation(cond, w_mod, b_mod)
    x = (x_prompt.reshape(-1, D), x_sample.reshape(-1, D), _pos_embed(ss // GRID_W))
    sm = jax.nn.softmax(b_lb.astype(F32), axis=0)
    lb_all = jnp.cumsum(sm, axis=0) - sm[0]
    geo = Geometry(bp, sp, bs, ss)
    x1, routing, new_c, new_n, new_m = _mlstm_layer(geo, x, mod[0], 0, a_w_in, a_b_gates, a_norm, a_w_out,
                                                    ln_g[0, 0], ln_b[0, 0], state_mlstm_C, state_mlstm_n,
                                                    state_mlstm_m, w_router[0], e_bias[0])
    x2 = _moe_layer(geo, x1, routing, mod[0], 0, w_gate, w_up, w_down, ws_gate[0], ws_up[0], ws_down[0],
                    ln_g[0, 1], ln_b[0, 1])
    x3, routing, new_s = _hgrn_layer(geo, x2, mod[1], 0, lb_all[1], b_w_in, b_norm, b_w_out, ln_g[1, 0], ln_b[1, 0],
                                     state_hgrn_S, w_router[1], e_bias[1])
    y_p, y_s = _moe_layer(geo, x3, routing, mod[1], 1, w_gate, w_up, w_down, ws_gate[1], ws_up[1], ws_down[1],
                          ln_g[1, 1], ln_b[1, 1], split=True)
    return (y_p.reshape(bp, sp, D), y_s.reshape(bs, ss, D), new_c[:, None], new_n[:, None], new_m[:, None],
            new_s[:, None])
```
